```python
import jax, jax.numpy as jnp
from jax import lax
import numpy as np


D_MODEL = 1024
BATCH = 8
SEQ = 4096
DEPTH = 1

MIX_WIDTH = D_MODEL
ATTN_WIDTH = MIX_WIDTH // 2
ATTN_HEADS = 8
ATTN_HEAD_DIM = ATTN_WIDTH // ATTN_HEADS
DILATED_PATTERNS = ((128, 1), (512, 4), (2048, 16))
ATTN_BLOCK = 128
ATTN_PAD_UNIT = ATTN_BLOCK * 16
HGRN_WIDTH = MIX_WIDTH - ATTN_WIDTH
HGRN_EXPAND = 128
HGRN_HEADS = HGRN_WIDTH // HGRN_EXPAND
HGRN_FDIM = HGRN_EXPAND
HGRN_IDIM = HGRN_WIDTH // HGRN_HEADS
HGRN_CHUNK = 64
D_FF = 2816
CONV_WIDTH = 3
NORM_EPS = 1e-6
IN_SIZES = (ATTN_WIDTH, ATTN_WIDTH, ATTN_WIDTH, HGRN_WIDTH, HGRN_WIDTH, HGRN_WIDTH, HGRN_WIDTH)
IN_TOTAL = sum(IN_SIZES)
IN_SPLITS = [sum(IN_SIZES[:j + 1]) for j in range(len(IN_SIZES) - 1)]

kernel_name = 'hybrid_dilated_attn_hgrn2_convglu'


def rms_norm(x, g):
    xf = x.astype(jnp.float32)
    y = xf * lax.rsqrt(jnp.mean(xf * xf, axis=-1, keepdims=True) + NORM_EPS)
    return (y * g.astype(jnp.float32)).astype(x.dtype)


def dilated_branch(q, k, v, window, dilation):
    B, Sp, H, E = q.shape
    n_keys = window // dilation
    nb = Sp // (dilation * ATTN_BLOCK)
    shape = (B, nb, ATTN_BLOCK, dilation, H, E)
    qb, kb, vb = q.reshape(shape), k.reshape(shape), v.reshape(shape)

    def with_prev(t):
        prev = jnp.pad(t, ((0, 0), (1, 0), (0, 0), (0, 0), (0, 0), (0, 0)))[:, :-1]
        return jnp.concatenate([prev, t], axis=2)

    kw, vw = with_prev(kb), with_prev(vb)
    s = jnp.einsum('bnqrhe,bnkrhe->bnrhqk', qb, kw,
                   preferred_element_type=jnp.float32) * (E ** -0.5)
    qi = jnp.arange(ATTN_BLOCK)[:, None]
    kj = jnp.arange(2 * ATTN_BLOCK)[None, :]
    dist = qi + ATTN_BLOCK - kj
    blk = jnp.arange(nb)[:, None, None]
    valid = (dist >= 0) & (dist <= n_keys) & (blk * ATTN_BLOCK + kj - ATTN_BLOCK >= 0)
    s = jnp.where(valid[None, :, None, None], s, -jnp.inf)
    m = jnp.max(s, axis=-1)
    p = jnp.exp(s - m[..., None])
    l = jnp.sum(p, axis=-1)
    o = jnp.einsum('bnrhqk,bnkrhe->bnqrhe', p.astype(v.dtype), vw,
                   preferred_element_type=jnp.float32).reshape(B, Sp, H, E)
    m = m.transpose(0, 1, 4, 2, 3).reshape(B, Sp, H)
    l = l.transpose(0, 1, 4, 2, 3).reshape(B, Sp, H)
    return o, m, l


def dilated_attention(q, k, v):
    B, S, H, E = q.shape
    Sp = -(-S // ATTN_PAD_UNIT) * ATTN_PAD_UNIT
    pad = ((0, 0), (0, Sp - S), (0, 0), (0, 0))
    q, k, v = jnp.pad(q, pad), jnp.pad(k, pad), jnp.pad(v, pad)
    outs, maxes, sums = [], [], []
    for window, dilation in DILATED_PATTERNS:
        o, m, l = dilated_branch(q, k, v, window, dilation)
        outs.append(o)
        maxes.append(m)
        sums.append(l)
    ms, ls, os_ = jnp.stack(maxes), jnp.stack(sums), jnp.stack(outs)
    w = jnp.exp(ms - jnp.max(ms, axis=0, keepdims=True))
    den = jnp.sum(w * ls, axis=0)
    num = jnp.sum(w[..., None] * os_, axis=0)
    return (num / den[..., None])[:, :S]


def hgrn2_mixer(q, f, i, lb):
    B, S, H, K = q.shape
    V = i.shape[-1]
    C = HGRN_CHUNK
    nc = S // C
    qf = jax.nn.silu(q.astype(jnp.float32))
    forget = lb + (1.0 - lb) * jax.nn.sigmoid(f.astype(jnp.float32))
    key = 1.0 - forget
    log_f = jnp.log(forget)
    iv = i.astype(jnp.float32)
    qf = qf.reshape(B, nc, C, H, K)
    key = key.reshape(B, nc, C, H, K)
    log_f = log_f.reshape(B, nc, C, H, K)
    iv = iv.reshape(B, nc, C, H, V)
    b = jnp.cumsum(log_f, axis=2)
    q_dec = qf * jnp.exp(b)
    k_inv = key * jnp.exp(-b)
    A = jnp.einsum('bnthk,bnshk->bnhts', q_dec, k_inv)
    causal = jnp.tril(jnp.ones((C, C), dtype=bool))
    A = jnp.where(causal, A, 0.0)
    o_intra = jnp.einsum('bnhts,bnshv->bnthv', A, iv)
    b_end = b[:, :, -1]
    k_end = key * jnp.exp(b_end[:, :, None] - b)
    U = jnp.einsum('bnshk,bnshv->bnhkv', k_end, iv)
    decay = jnp.exp(b_end)

    def step(state, xs):
        d, u = xs
        return d[..., None] * state + u, state

    init = jnp.zeros((B, H, K, V), jnp.float32)
    _, states = lax.scan(step, init, (decay.transpose(1, 0, 2, 3), U.transpose(1, 0, 2, 3, 4)))
    states = states.transpose(1, 0, 2, 3, 4)
    o_inter = jnp.einsum('bnthk,bnhkv->bnthv', q_dec, states)
    return (o_intra + o_inter).reshape(B, S, H, V)


def conv_glu(u, w_up, conv_w, conv_b, w_down):
    S = u.shape[1]
    gate, val = jnp.split(u @ w_up, 2, axis=-1)
    gp = jnp.pad(gate, ((0, 0), (CONV_WIDTH - 1, 0), (0, 0)))
    conv = conv_b
    for j in range(CONV_WIDTH):
        conv = conv + conv_w[j] * gp[:, j:j + S]
    return (jax.nn.gelu(conv, approximate=False) * val) @ w_down


def _fwd_setup_inputs(seed: int = 0) -> dict:
    key = jax.random.key(seed)
    ks = jax.random.split(key, 13)
    f32 = jnp.float32
    nrm = lambda k, shape: jax.random.normal(k, shape, f32)
    return {
        'x': nrm(ks[0], (BATCH, SEQ, D_MODEL)),
        'norm1_g': 1.0 + 0.02 * nrm(ks[1], (DEPTH, D_MODEL)),
        'w_in': nrm(ks[2], (DEPTH, D_MODEL, IN_TOTAL)) * D_MODEL ** -0.5,
        'attn_norm_g': 1.0 + 0.02 * nrm(ks[3], (DEPTH, ATTN_WIDTH)),
        'hgrn_norm_g': 1.0 + 0.02 * nrm(ks[4], (DEPTH, HGRN_WIDTH)),
        'hgrn_lb_logits': 0.1 * nrm(ks[5], (DEPTH + 1, HGRN_WIDTH)),
        'w_out': nrm(ks[6], (DEPTH, MIX_WIDTH, D_MODEL)) * MIX_WIDTH ** -0.5,
        'norm2_g': 1.0 + 0.02 * nrm(ks[7], (DEPTH, D_MODEL)),
        'w_up': nrm(ks[8], (DEPTH, D_MODEL, 2 * D_FF)) * D_MODEL ** -0.5,
        'conv_w': nrm(ks[9], (DEPTH, CONV_WIDTH, D_FF)) * CONV_WIDTH ** -0.5,
        'conv_b': 0.02 * nrm(ks[10], (DEPTH, D_FF)),
        'w_down': nrm(ks[11], (DEPTH, D_FF, D_MODEL)) * D_FF ** -0.5,
        'final_norm_g': 1.0 + 0.02 * nrm(ks[12], (D_MODEL,)),
    }


def _fwd_reference(x, norm1_g, w_in, attn_norm_g, hgrn_norm_g, hgrn_lb_logits, w_out,
              norm2_g, w_up, conv_w, conv_b, w_down, final_norm_g):
    B, S, _ = x.shape
    lower_bounds = jnp.cumsum(jax.nn.softmax(hgrn_lb_logits.astype(jnp.float32), axis=0), axis=0)
    h = x
    for layer in range(DEPTH):
        u = rms_norm(h, norm1_g[layer])
        proj = u @ w_in[layer]
        aq, ak, av, hq, hf, hi, hg = jnp.split(proj, IN_SPLITS, axis=-1)
        attn = dilated_attention(aq.reshape(B, S, ATTN_HEADS, ATTN_HEAD_DIM),
                                 ak.reshape(B, S, ATTN_HEADS, ATTN_HEAD_DIM),
                                 av.reshape(B, S, ATTN_HEADS, ATTN_HEAD_DIM))
        attn = rms_norm(attn.reshape(B, S, ATTN_WIDTH), attn_norm_g[layer])
        lb = lower_bounds[layer].reshape(HGRN_HEADS, HGRN_FDIM)
        rec = hgrn2_mixer(hq.reshape(B, S, HGRN_HEADS, HGRN_FDIM),
                          hf.reshape(B, S, HGRN_HEADS, HGRN_FDIM),
                          hi.reshape(B, S, HGRN_HEADS, HGRN_IDIM), lb)
        rec = rms_norm(rec, hgrn_norm_g[layer].reshape(HGRN_HEADS, HGRN_IDIM))
        rec = rec * jax.nn.silu(hg.astype(jnp.float32).reshape(B, S, HGRN_HEADS, HGRN_IDIM))
        mixed = jnp.concatenate([attn.astype(jnp.float32), rec.reshape(B, S, HGRN_WIDTH)], axis=-1)
        h = h + mixed.astype(h.dtype) @ w_out[layer]
        u = rms_norm(h, norm2_g[layer])
        h = h + conv_glu(u, w_up[layer], conv_w[layer], conv_b[layer], w_down[layer]).astype(h.dtype)
    return rms_norm(h, final_norm_g)


import jax as _jax
import jax.numpy as _jnp

TWIN_FORMAT = 'train_step'
FWD_PARAMS = ['x', 'norm1_g', 'w_in', 'attn_norm_g', 'hgrn_norm_g', 'hgrn_lb_logits', 'w_out', 'norm2_g', 'w_up', 'conv_w', 'conv_b', 'w_down', 'final_norm_g']
TWIN_WEIGHTS = ['norm1_g', 'w_in', 'attn_norm_g', 'hgrn_norm_g', 'hgrn_lb_logits', 'w_out', 'norm2_g', 'w_up', 'conv_w', 'conv_b', 'w_down', 'final_norm_g']
TWIN_DIFF_INPUT = 'x'
TWIN_INPUTS = ['x', 'norm1_g', 'w_in', 'attn_norm_g', 'hgrn_norm_g', 'hgrn_lb_logits', 'w_out', 'norm2_g', 'w_up', 'conv_w', 'conv_b', 'w_down', 'final_norm_g', 'loss_target', 'm_norm1_g', 'm_w_in', 'm_attn_norm_g', 'm_hgrn_norm_g', 'm_hgrn_lb_logits', 'm_w_out', 'm_norm2_g', 'm_w_up', 'm_conv_w', 'm_conv_b', 'm_w_down', 'm_final_norm_g', 'v_norm1_g', 'v_w_in', 'v_attn_norm_g', 'v_hgrn_norm_g', 'v_hgrn_lb_logits', 'v_w_out', 'v_norm2_g', 'v_w_up', 'v_conv_w', 'v_conv_b', 'v_w_down', 'v_final_norm_g']
TWIN_OUTPUTS = ['loss', 'grad_x', 'grad_norm1_g', 'grad_w_in', 'grad_attn_norm_g', 'grad_hgrn_norm_g', 'grad_hgrn_lb_logits', 'grad_w_out', 'grad_norm2_g', 'grad_w_up', 'grad_conv_w', 'grad_conv_b', 'grad_w_down', 'grad_final_norm_g', 'delta_norm1_g', 'delta_w_in', 'delta_attn_norm_g', 'delta_hgrn_norm_g', 'delta_hgrn_lb_logits', 'delta_w_out', 'delta_norm2_g', 'delta_w_up', 'delta_conv_w', 'delta_conv_b', 'delta_w_down', 'delta_final_norm_g', 'new_m_norm1_g', 'new_m_w_in', 'new_m_attn_norm_g', 'new_m_hgrn_norm_g', 'new_m_hgrn_lb_logits', 'new_m_w_out', 'new_m_norm2_g', 'new_m_w_up', 'new_m_conv_w', 'new_m_conv_b', 'new_m_w_down', 'new_m_final_norm_g', 'new_v_norm1_g', 'new_v_w_in', 'new_v_attn_norm_g', 'new_v_hgrn_norm_g', 'new_v_hgrn_lb_logits', 'new_v_w_out', 'new_v_norm2_g', 'new_v_w_up', 'new_v_conv_w', 'new_v_conv_b', 'new_v_w_down', 'new_v_final_norm_g']
TWIN_LEAF_KINDS = {'loss': 'loss', 'grad_x': 'grad_x', 'grad_norm1_g': 'grad_w', 'grad_w_in': 'grad_w', 'grad_attn_norm_g': 'grad_w', 'grad_hgrn_norm_g': 'grad_w', 'grad_hgrn_lb_logits': 'grad_w', 'grad_w_out': 'grad_w', 'grad_norm2_g': 'grad_w', 'grad_w_up': 'grad_w', 'grad_conv_w': 'grad_w', 'grad_conv_b': 'grad_w', 'grad_w_down': 'grad_w', 'grad_final_norm_g': 'grad_w', 'delta_norm1_g': 'delta_w', 'delta_w_in': 'delta_w', 'delta_attn_norm_g': 'delta_w', 'delta_hgrn_norm_g': 'delta_w', 'delta_hgrn_lb_logits': 'delta_w', 'delta_w_out': 'delta_w', 'delta_norm2_g': 'delta_w', 'delta_w_up': 'delta_w', 'delta_conv_w': 'delta_w', 'delta_conv_b': 'delta_w', 'delta_w_down': 'delta_w', 'delta_final_norm_g': 'delta_w', 'new_m_norm1_g': 'new_m', 'new_m_w_in': 'new_m', 'new_m_attn_norm_g': 'new_m', 'new_m_hgrn_norm_g': 'new_m', 'new_m_hgrn_lb_logits': 'new_m', 'new_m_w_out': 'new_m', 'new_m_norm2_g': 'new_m', 'new_m_w_up': 'new_m', 'new_m_conv_w': 'new_m', 'new_m_conv_b': 'new_m', 'new_m_w_down': 'new_m', 'new_m_final_norm_g': 'new_m', 'new_v_norm1_g': 'new_v', 'new_v_w_in': 'new_v', 'new_v_attn_norm_g': 'new_v', 'new_v_hgrn_norm_g': 'new_v', 'new_v_hgrn_lb_logits': 'new_v', 'new_v_w_out': 'new_v', 'new_v_norm2_g': 'new_v', 'new_v_w_up': 'new_v', 'new_v_conv_w': 'new_v', 'new_v_conv_b': 'new_v', 'new_v_w_down': 'new_v', 'new_v_final_norm_g': 'new_v'}


def _forward(args):
    return _fwd_reference(*[args[k] for k in FWD_PARAMS])


def _output_shape():
    out = _jax.eval_shape(lambda: _forward(_fwd_setup_inputs(0)))
    return out.shape, out.dtype

N_MICROBATCH = 1
ADAM_LR = 0.001
ADAM_B1 = 0.9
ADAM_B2 = 0.999
ADAM_EPS = 1e-08
ADAM_WD = 0.01
ADAM_STEP = 10
PER_EXAMPLE_BATCH_AXIS = {'x': 0, 'loss_target': 0}
SHARED_INPUTS = []
_WEIGHT_DTYPES = {'norm1_g': _jnp.float32, 'w_in': _jnp.float32, 'attn_norm_g': _jnp.float32, 'hgrn_norm_g': _jnp.float32, 'hgrn_lb_logits': _jnp.float32, 'w_out': _jnp.float32, 'norm2_g': _jnp.float32, 'w_up': _jnp.float32, 'conv_w': _jnp.float32, 'conv_b': _jnp.float32, 'w_down': _jnp.float32, 'final_norm_g': _jnp.float32}
MOMENT_SCALE = {'norm1_g': 2.042907e-01, 'w_in': 1.106930e-01, 'attn_norm_g': 1.730903e-01, 'hgrn_norm_g': 9.491649e-02, 'hgrn_lb_logits': 8.408425e-03, 'w_out': 1.246837e-01, 'norm2_g': 1.273623e-01, 'w_up': 4.933348e-02, 'conv_w': 5.123974e-02, 'conv_b': 4.921345e-02, 'w_down': 8.044532e-02, 'final_norm_g': 3.206668e+01}


def _to_microbatches(a, axis):
    t = _jnp.moveaxis(a, axis, 0)
    t = t.reshape((N_MICROBATCH, t.shape[0] // N_MICROBATCH) + t.shape[1:])
    return _jnp.moveaxis(t, 1, axis + 1)


def setup_inputs(seed: int = 0) -> dict:
    inp = _fwd_setup_inputs(seed)
    key = _jax.random.fold_in(_jax.random.key(seed), 7919)
    shape, _ = _output_shape()
    out = dict(inp)
    out["loss_target"] = _jax.random.normal(_jax.random.fold_in(key, 0), shape, _jnp.float32)
    for i, name in enumerate(TWIN_WEIGHTS):
        w = inp[name].astype(_jnp.float32)
        if MOMENT_SCALE is None:
            s = _jnp.sqrt(_jnp.mean(_jnp.square(w)) + 1e-30)
        else:
            s = MOMENT_SCALE[name]
        km, kv = _jax.random.split(_jax.random.fold_in(key, i + 1))
        out[name] = w
        out["m_" + name] = s * _jax.random.normal(km, w.shape, _jnp.float32)
        out["v_" + name] = (s * s) * _jax.random.uniform(kv, w.shape, _jnp.float32, 0.5, 1.5)
    if N_MICROBATCH > 1:
        for name, axis in PER_EXAMPLE_BATCH_AXIS.items():
            out[name] = _to_microbatches(out[name], axis)
    return {'x': out['x'], 'norm1_g': out['norm1_g'], 'w_in': out['w_in'], 'attn_norm_g': out['attn_norm_g'], 'hgrn_norm_g': out['hgrn_norm_g'], 'hgrn_lb_logits': out['hgrn_lb_logits'], 'w_out': out['w_out'], 'norm2_g': out['norm2_g'], 'w_up': out['w_up'], 'conv_w': out['conv_w'], 'conv_b': out['conv_b'], 'w_down': out['w_down'], 'final_norm_g': out['final_norm_g'], 'loss_target': out['loss_target'], 'm_norm1_g': out['m_norm1_g'], 'm_w_in': out['m_w_in'], 'm_attn_norm_g': out['m_attn_norm_g'], 'm_hgrn_norm_g': out['m_hgrn_norm_g'], 'm_hgrn_lb_logits': out['m_hgrn_lb_logits'], 'm_w_out': out['m_w_out'], 'm_norm2_g': out['m_norm2_g'], 'm_w_up': out['m_w_up'], 'm_conv_w': out['m_conv_w'], 'm_conv_b': out['m_conv_b'], 'm_w_down': out['m_w_down'], 'm_final_norm_g': out['m_final_norm_g'], 'v_norm1_g': out['v_norm1_g'], 'v_w_in': out['v_w_in'], 'v_attn_norm_g': out['v_attn_norm_g'], 'v_hgrn_norm_g': out['v_hgrn_norm_g'], 'v_hgrn_lb_logits': out['v_hgrn_lb_logits'], 'v_w_out': out['v_w_out'], 'v_norm2_g': out['v_norm2_g'], 'v_w_up': out['v_w_up'], 'v_conv_w': out['v_conv_w'], 'v_conv_b': out['v_conv_b'], 'v_w_down': out['v_w_down'], 'v_final_norm_g': out['v_final_norm_g']}


def _loss(weights, diff, rest, loss_target):
    with _jax.named_scope("forward"):
        args = {**rest, TWIN_DIFF_INPUT: diff, **{k: w.astype(_WEIGHT_DTYPES[k]) for k, w in weights.items()}}
        y = _forward(args)
    with _jax.named_scope("loss_head"):
        err = _jnp.square(y.astype(_jnp.float32) - loss_target)
        return 0.5 * _jnp.sum(_jnp.mean(err, axis=-1)) if err.ndim else 0.5 * err


def _adamw(w, g, m, v):
    m = ADAM_B1 * m + (1.0 - ADAM_B1) * g
    v = ADAM_B2 * v + (1.0 - ADAM_B2) * _jnp.square(g)
    m_hat = m / (1.0 - ADAM_B1 ** ADAM_STEP)
    v_hat = v / (1.0 - ADAM_B2 ** ADAM_STEP)
    delta = -ADAM_LR * (m_hat / (_jnp.sqrt(v_hat) + ADAM_EPS) + ADAM_WD * w)
    return delta, m, v


def reference(x, norm1_g, w_in, attn_norm_g, hgrn_norm_g, hgrn_lb_logits, w_out, norm2_g, w_up, conv_w, conv_b, w_down, final_norm_g, loss_target, m_norm1_g, m_w_in, m_attn_norm_g, m_hgrn_norm_g, m_hgrn_lb_logits, m_w_out, m_norm2_g, m_w_up, m_conv_w, m_conv_b, m_w_down, m_final_norm_g, v_norm1_g, v_w_in, v_attn_norm_g, v_hgrn_norm_g, v_hgrn_lb_logits, v_w_out, v_norm2_g, v_w_up, v_conv_w, v_conv_b, v_w_down, v_final_norm_g):
    given = dict(x=x, norm1_g=norm1_g, w_in=w_in, attn_norm_g=attn_norm_g, hgrn_norm_g=hgrn_norm_g, hgrn_lb_logits=hgrn_lb_logits, w_out=w_out, norm2_g=norm2_g, w_up=w_up, conv_w=conv_w, conv_b=conv_b, w_down=w_down, final_norm_g=final_norm_g, loss_target=loss_target, m_norm1_g=m_norm1_g, m_w_in=m_w_in, m_attn_norm_g=m_attn_norm_g, m_hgrn_norm_g=m_hgrn_norm_g, m_hgrn_lb_logits=m_hgrn_lb_logits, m_w_out=m_w_out, m_norm2_g=m_norm2_g, m_w_up=m_w_up, m_conv_w=m_conv_w, m_conv_b=m_conv_b, m_w_down=m_w_down, m_final_norm_g=m_final_norm_g, v_norm1_g=v_norm1_g, v_w_in=v_w_in, v_attn_norm_g=v_attn_norm_g, v_hgrn_norm_g=v_hgrn_norm_g, v_hgrn_lb_logits=v_hgrn_lb_logits, v_w_out=v_w_out, v_norm2_g=v_norm2_g, v_w_up=v_w_up, v_conv_w=v_conv_w, v_conv_b=v_conv_b, v_w_down=v_w_down, v_final_norm_g=v_final_norm_g)
    weights = {n: given[n] for n in TWIN_WEIGHTS}
    shared = {n: given[n] for n in SHARED_INPUTS}
    per_example = {n: given[n] for n in ['x']}
    grad_fn = _jax.value_and_grad(_loss, argnums=(0, 1))

    def one_microbatch(ex, loss_target):
        ex = dict(ex)
        diff = ex.pop(TWIN_DIFF_INPUT)
        return grad_fn(weights, diff, {**shared, **ex}, loss_target)

    if N_MICROBATCH == 1:
        loss, (grad_w, grad_x) = one_microbatch(per_example, given["loss_target"])
    else:
        def body(carry, xs):
            loss_sum, grad_sum = carry
            l_k, (gw_k, gx_k) = one_microbatch(xs[0], xs[1])
            with _jax.named_scope("update"):
                return (loss_sum + l_k, _jax.tree.map(_jnp.add, grad_sum, gw_k)), gx_k

        init = (_jnp.zeros((), _jnp.float32), _jax.tree.map(_jnp.zeros_like, weights))
        (loss, grad_w), grad_x = _jax.lax.scan(body, init, (per_example, given["loss_target"]))
    with _jax.named_scope("update"):
        delta_w, new_m, new_v = {}, {}, {}
        for n in TWIN_WEIGHTS:
            delta_w[n], new_m[n], new_v[n] = _adamw(weights[n], grad_w[n], given["m_" + n], given["v_" + n])
    return (loss, grad_x, *[grad_w[n] for n in TWIN_WEIGHTS], *[delta_w[n] for n in TWIN_WEIGHTS],
            *[new_m[n] for n in TWIN_WEIGHTS], *[new_v[n] for n in TWIN_WEIGHTS])
```

```python
import functools
import math

import jax
import jax.numpy as jnp
from jax import lax
from jax.experimental import pallas as pl
from jax.experimental.pallas import tpu as pltpu

F32 = jnp.float32
BF16 = jnp.bfloat16

D_MODEL = 1024
ATTN_W = 512
HGRN_W = 512
HEAD_PAIR = 128
ATTN_BLK = 128
DILATIONS = (1, 4, 16)
HGRN_HEADS = 4
HGRN_DIM = 128
HGRN_CHUNK = 64
SUPER = 256
D_FF = 2816
N_CHIPS = 4
IN_TOTAL = 3584
IN_SHARD = IN_TOTAL // N_CHIPS
UP_SHARD = 2 * D_FF // N_CHIPS
QKV_W = 3 * ATTN_W
HG_W = 4 * HGRN_W
EPS = 1e-6
NEG = -1e30
V7X_VMEM_BYTES = 64 * 1024 * 1024
VMEM_LIMIT = V7X_VMEM_BYTES - 8 * 1024 * 1024

ADAM_LR = 0.001
ADAM_B1 = 0.9
ADAM_B2 = 0.999
ADAM_EPS = 1e-08
ADAM_WD = 0.01
ADAM_STEP = 10

MESH = pl.DeviceIdType.MESH


def _cp(*sem):
    return pltpu.CompilerParams(dimension_semantics=sem or None, vmem_limit_bytes=VMEM_LIMIT)


def _dot(a, b):
    return jnp.dot(a, b, preferred_element_type=F32)


def _dot_nt(a, b):
    return lax.dot_general(a, b, (((1,), (1,)), ((), ())), preferred_element_type=F32)


def _dot_tn(a, b):
    return lax.dot_general(a, b, (((0,), (0,)), ((), ())), preferred_element_type=F32)


def _sigmoid(x):
    return 1.0 / (1.0 + jnp.exp(-x))


def _rms(x, width):
    return lax.rsqrt(jnp.sum(x * x, axis=-1, keepdims=True) * (1.0 / width) + EPS)


def _rms_bwd(dn, n, r, width):
    return r * (dn - n * (jnp.sum(dn * n, axis=-1, keepdims=True) * (1.0 / width)))


def _colsum(x):
    return jnp.sum(x, axis=0, keepdims=True)


def _row(v, k):
    rid = lax.broadcasted_iota(jnp.int32, v.shape, 0)
    return jnp.sum(jnp.where(rid == k, v, 0.0), axis=0, keepdims=True)


def _full(shape):
    return pl.BlockSpec(shape, lambda *_: (0,) * len(shape))


def _once(shape):
    return pl.BlockSpec(shape, lambda *_: (0,) * len(shape), pipeline_mode=pl.Buffered(1))


def _in_proj(x, g1, w_in4, tm=256):
    T = x.shape[0]

    def body(x_ref, g_ref, w_ref, u_ref, qkv_ref, hg_ref):
        xv = x_ref[...]
        u = (xv * _rms(xv, D_MODEL) * g_ref[...]).astype(BF16)
        u_ref[...] = u
        p0 = _dot(u, w_ref[0])
        p1 = _dot(u, w_ref[1])
        qkv_ref[:, 0:IN_SHARD] = p0.astype(BF16)
        qkv_ref[:, IN_SHARD:QKV_W] = p1[:, :QKV_W - IN_SHARD].astype(BF16)
        hg_ref[:, 0:2 * IN_SHARD - QKV_W] = p1[:, QKV_W - IN_SHARD:]
        hg_ref[:, 2 * IN_SHARD - QKV_W:3 * IN_SHARD - QKV_W] = _dot(u, w_ref[2])
        hg_ref[:, 3 * IN_SHARD - QKV_W:HG_W] = _dot(u, w_ref[3])

    return pl.pallas_call(
        body, name="in_proj", grid=(T // tm,),
        in_specs=[pl.BlockSpec((tm, D_MODEL), lambda i: (i, 0)), _full((1, D_MODEL)),
                  _once((N_CHIPS, D_MODEL, IN_SHARD))],
        out_specs=[pl.BlockSpec((tm, D_MODEL), lambda i: (i, 0)), pl.BlockSpec((tm, QKV_W), lambda i: (i, 0)),
                   pl.BlockSpec((tm, HG_W), lambda i: (i, 0))],
        out_shape=[jax.ShapeDtypeStruct((T, D_MODEL), BF16), jax.ShapeDtypeStruct((T, QKV_W), BF16),
                   jax.ShapeDtypeStruct((T, HG_W), F32)],
        compiler_params=_cp("arbitrary"),
    )(x, g1, w_in4)


def _attn_masks():
    lane = lax.broadcasted_iota(jnp.int32, (ATTN_BLK, HEAD_PAIR), 1)
    first = lane < 64
    row = lax.broadcasted_iota(jnp.int32, (2 * ATTN_BLK, 2 * ATTN_BLK), 0)
    col = lax.broadcasted_iota(jnp.int32, (2 * ATTN_BLK, 2 * ATTN_BLK), 1)
    base = jnp.where(row >= ATTN_BLK, row - ATTN_BLK, row) - col
    return first, base


def _two_heads(blk, first):
    zero = jnp.zeros_like(blk)
    return jnp.concatenate([jnp.where(first, blk, zero), jnp.where(first, zero, blk)], axis=0)


def _attn_specs(S, d):
    def at(off):
        return pl.BlockSpec((S, HEAD_PAIR), lambda j: (0, (j // 4) * 12 + off + j % 4))
    return [at(0), at(4), at(8)], pl.BlockSpec((S, HEAD_PAIR), lambda j: (0, j))


def _attn_fwd(qkv, prev, d, last):
    T = qkv.shape[0]
    S = T // d
    nb = S // ATTN_BLK
    qkv_specs, col_spec = _attn_specs(S, d)
    n_prev = 0 if prev is None else 3

    def body(*refs):
        q_ref, k_ref, v_ref = refs[:3]
        prev_refs = refs[3:3 + n_prev]
        out_refs = refs[3 + n_prev:]
        first, base = _attn_masks()

        def step(n, carry):
            q0 = pl.multiple_of(n * ATTN_BLK, ATTN_BLK)
            k0 = pl.multiple_of(jnp.maximum(n - 1, 0) * ATTN_BLK, ATTN_BLK)
            q2 = _two_heads(q_ref[pl.ds(q0, ATTN_BLK), :], first)
            kw = k_ref[pl.ds(k0, 2 * ATTN_BLK), :]
            vw = v_ref[pl.ds(k0, 2 * ATTN_BLK), :]
            dist = base + (q0 - k0)
            s = jnp.where((dist >= 0) & (dist <= ATTN_BLK), _dot_nt(q2, kw) * 0.125, NEG)
            mb = jnp.max(s, axis=-1, keepdims=True)
            p = jnp.exp(s - mb)
            lb = jnp.sum(p, axis=-1, keepdims=True)
            o2 = _dot(p.astype(BF16), vw)
            o = jnp.where(first, o2[:ATTN_BLK], o2[ATTN_BLK:])
            m = jnp.where(first, mb[:ATTN_BLK], mb[ATTN_BLK:])
            l = jnp.where(first, lb[:ATTN_BLK], lb[ATTN_BLK:])
            rows = pl.ds(q0, ATTN_BLK)
            if n_prev:
                pa, pm, pl_ = (r[rows, :] for r in prev_refs)
                mn = jnp.maximum(pm, m)
                wa = jnp.exp(pm - mn)
                wb = jnp.exp(m - mn)
                o = pa * wa + o * wb
                l = pl_ * wa + l * wb
                m = mn
            if last:
                out_refs[0][rows, :] = o / l
                out_refs[1][rows, :] = m + jnp.log(l)
            else:
                out_refs[0][rows, :] = o
                out_refs[1][rows, :] = m
                out_refs[2][rows, :] = l
            return carry

        lax.fori_loop(0, nb, step, 0)

    n_out = 2 if last else 3
    qv = qkv.reshape(S, d * QKV_W)
    ins = [qv, qv, qv] + ([] if prev is None else [a.reshape(S, d * ATTN_W) for a in prev])
    outs = pl.pallas_call(
        body, name=f"attn_fwd_d{d}", grid=(4 * d,),
        in_specs=qkv_specs + [col_spec] * n_prev,
        out_specs=[col_spec] * n_out,
        out_shape=[jax.ShapeDtypeStruct((S, d * ATTN_W), F32)] * n_out,
        compiler_params=_cp("arbitrary"),
    )(*ins)
    return [a.reshape(T, ATTN_W) for a in outs]


def _attn_bwd(qkv, o, lse, do, prev, d):
    T = qkv.shape[0]
    S = T // d
    nb = S // ATTN_BLK
    qkv_specs, col_spec = _attn_specs(S, d)
    n_prev = 0 if prev is None else 3

    def body(*refs):
        q_ref, k_ref, v_ref, o_ref, lse_ref, do_ref = refs[:6]
        prev_refs = refs[6:6 + n_prev]
        dq_ref, dk_ref, dv_ref = refs[6 + n_prev:]
        first, base = _attn_masks()
        if n_prev:
            dk_ref[...] = prev_refs[1][...]
            dv_ref[...] = prev_refs[2][...]
        else:
            dk_ref[...] = jnp.zeros_like(dk_ref)
            dv_ref[...] = jnp.zeros_like(dv_ref)

        def step(n, carry):
            q0 = pl.multiple_of(n * ATTN_BLK, ATTN_BLK)
            k0 = pl.multiple_of(jnp.maximum(n - 1, 0) * ATTN_BLK, ATTN_BLK)
            rows = pl.ds(q0, ATTN_BLK)
            keys = pl.ds(k0, 2 * ATTN_BLK)
            q2 = _two_heads(q_ref[rows, :], first)
            kw = k_ref[keys, :]
            vw = v_ref[keys, :]
            dist = base + (q0 - k0)
            valid = (dist >= 0) & (dist <= ATTN_BLK)
            lse_b = lse_ref[rows, :]
            lse2 = jnp.concatenate(
                [jnp.max(jnp.where(first, lse_b, NEG), axis=-1, keepdims=True),
                 jnp.max(jnp.where(first, NEG, lse_b), axis=-1, keepdims=True)], axis=0)
            p = jnp.where(valid, jnp.exp(_dot_nt(q2, kw) * 0.125 - lse2), 0.0)
            dob = do_ref[rows, :]
            prod = dob * o_ref[rows, :]
            delta = jnp.concatenate(
                [jnp.sum(jnp.where(first, prod, 0.0), axis=-1, keepdims=True),
                 jnp.sum(jnp.where(first, 0.0, prod), axis=-1, keepdims=True)], axis=0)
            do2 = _two_heads(dob, first).astype(BF16)
            ds = (p * (_dot_nt(do2, vw) - delta) * 0.125).astype(BF16)
            dq2 = _dot(ds, kw)
            dq = jnp.where(first, dq2[:ATTN_BLK], dq2[ATTN_BLK:])
            if n_prev:
                dq = dq + prev_refs[0][rows, :]
            dq_ref[rows, :] = dq
            dk_ref[keys, :] += _dot_tn(ds, q2)
            dv_ref[keys, :] += _dot_tn(p.astype(BF16), do2)
            return carry

        lax.fori_loop(0, nb, step, 0)

    qv = qkv.reshape(S, d * QKV_W)
    ins = [qv, qv, qv] + [a.reshape(S, d * ATTN_W) for a in (o, lse, do)]
    ins += [] if prev is None else [a.reshape(S, d * ATTN_W) for a in prev]
    outs = pl.pallas_call(
        body, name=f"attn_bwd_d{d}", grid=(4 * d,),
        in_specs=qkv_specs + [col_spec] * (3 + n_prev),
        out_specs=[col_spec] * 3,
        out_shape=[jax.ShapeDtypeStruct((S, d * ATTN_W), F32)] * 3,
        compiler_params=_cp("arbitrary"),
    )(*ins)
    return [a.reshape(T, ATTN_W) for a in outs]


def _chunk_ids():
    row = lax.broadcasted_iota(jnp.int32, (SUPER, HGRN_DIM), 0)
    r2 = lax.broadcasted_iota(jnp.int32, (SUPER, SUPER), 0)
    c2 = lax.broadcasted_iota(jnp.int32, (SUPER, SUPER), 1)
    amask = ((r2 // HGRN_CHUNK) == (c2 // HGRN_CHUNK)) & (c2 <= r2)
    return row % HGRN_CHUNK, row // HGRN_CHUNK, amask


def _cumsum_chunk(x, rmod):
    s = 1
    while s < HGRN_CHUNK:
        x = x + jnp.where(rmod >= s, pltpu.roll(x, s, 0), 0.0)
        s *= 2
    return x


def _suffix_sum_chunk(x, rmod):
    s = 1
    while s < HGRN_CHUNK:
        x = x + jnp.where(rmod < HGRN_CHUNK - s, pltpu.roll(x, SUPER - s, 0), 0.0)
        s *= 2
    return x


def _chunk_rows(vs, cid):
    out = vs[-1]
    for c in reversed(range(len(vs) - 1)):
        out = jnp.where(cid == c, vs[c], out)
    return out


def _expand(x, cid):
    return jnp.concatenate([jnp.where(cid == c, x, 0.0) for c in range(SUPER // HGRN_CHUNK)], axis=1)


def _hgrn_gates(q, f, lbv, rmod, cid, tmp):
    sq = _sigmoid(q)
    sg = _sigmoid(f)
    forget = lbv + (1.0 - lbv) * sg
    key = 1.0 - forget
    b = _cumsum_chunk(jnp.log(forget), rmod)
    tmp[...] = b
    bends = [tmp[c * HGRN_CHUNK + HGRN_CHUNK - 1:(c + 1) * HGRN_CHUNK, :] for c in range(SUPER // HGRN_CHUNK)]
    eb = jnp.exp(b)
    enb = jnp.exp(-b)
    ebe = jnp.exp(_chunk_rows(bends, cid) - b)
    return sq, sg, forget, key, eb, enb, ebe, q * sq * eb, key * enb, key * ebe, [jnp.exp(v) for v in bends]


def _hgrn_fwd(hg, lb):
    T = hg.shape[0]
    nsc = T // SUPER
    NC = SUPER // HGRN_CHUNK

    def body(q_ref, f_ref, i_ref, lb_ref, o_ref, st_ref, state, tmp):
        rmod, cid, amask = _chunk_ids()
        state[...] = jnp.zeros_like(state)
        lbv = lb_ref[...]

        def step(sc, carry):
            rows = pl.ds(pl.multiple_of(sc * SUPER, SUPER), SUPER)
            iv = i_ref[rows, :].astype(BF16)
            qd, ki, ke, dec = _hgrn_gates(q_ref[rows, :], f_ref[rows, :], lbv, rmod, cid, tmp)[-4:]
            a = jnp.where(amask, _dot_nt(qd.astype(BF16), ki.astype(BF16)), 0.0)
            o = _dot(a.astype(BF16), iv)
            ut = _dot_tn(iv, _expand(ke, cid).astype(BF16))
            st = state[...]
            st_ref[0, sc] = st
            sts = []
            for c in range(NC):
                sts.append(st)
                st = st * dec[c] + ut[:, c * HGRN_DIM:(c + 1) * HGRN_DIM]
            state[...] = st
            o = o + _dot_nt(_expand(qd, cid).astype(BF16), jnp.concatenate(sts, axis=1).astype(BF16))
            o_ref[rows, :] = o
            return carry

        lax.fori_loop(0, nsc, step, 0)

    col = lambda off: pl.BlockSpec((T, HGRN_DIM), lambda h: (0, off + h))
    return pl.pallas_call(
        body, name="hgrn_fwd", grid=(HGRN_HEADS,),
        in_specs=[col(0), col(4), col(8), pl.BlockSpec((1, HGRN_DIM), lambda h: (0, h))],
        out_specs=[pl.BlockSpec((T, HGRN_DIM), lambda h: (0, h)),
                   pl.BlockSpec((1, nsc, HGRN_DIM, HGRN_DIM), lambda h: (h, 0, 0, 0))],
        out_shape=[jax.ShapeDtypeStruct((T, HGRN_W), F32),
                   jax.ShapeDtypeStruct((HGRN_HEADS, nsc, HGRN_DIM, HGRN_DIM), F32)],
        scratch_shapes=[pltpu.VMEM((HGRN_DIM, HGRN_DIM), F32), pltpu.VMEM((SUPER, HGRN_DIM), F32)],
        compiler_params=_cp("arbitrary"),
    )(hg, hg, hg, lb)


def _hgrn_bwd(hg, lb, states, do):
    T = hg.shape[0]
    nsc = T // SUPER
    NC = SUPER // HGRN_CHUNK

    def body(q_ref, f_ref, i_ref, lb_ref, st_ref, do_ref, dq_ref, df_ref, di_ref, dlb_ref, dstate, tmp):
        rmod, cid, amask = _chunk_ids()
        dstate[...] = jnp.zeros_like(dstate)
        dlb_ref[...] = jnp.zeros_like(dlb_ref)
        lbv = lb_ref[...]

        def step(k, carry):
            sc = nsc - 1 - k
            rows = pl.ds(pl.multiple_of(sc * SUPER, SUPER), SUPER)
            q = q_ref[rows, :]
            ivf = i_ref[rows, :]
            iv = ivf.astype(BF16)
            dof = do_ref[rows, :]
            dob = dof.astype(BF16)
            sq, sg, forget, key, eb, enb, ebe, qd, ki, ke, dec = _hgrn_gates(q, f_ref[rows, :], lbv, rmod, cid, tmp)
            qdb, kib = qd.astype(BF16), ki.astype(BF16)
            keexp = _expand(ke, cid).astype(BF16)
            a = jnp.where(amask, _dot_nt(qdb, kib), 0.0).astype(BF16)
            ut = _dot_tn(iv, keexp)
            st = st_ref[0, sc]
            sts = []
            for c in range(NC):
                sts.append(st)
                st = st * dec[c] + ut[:, c * HGRN_DIM:(c + 1) * HGRN_DIM]
            gt = _dot_tn(dob, _expand(qd, cid).astype(BF16))
            nxt = [None] * NC
            ddec = [None] * NC
            dst = dstate[...]
            for c in reversed(range(NC)):
                nxt[c] = dst
                ddec[c] = _colsum(dst * sts[c])
                dst = dst * dec[c] + gt[:, c * HGRN_DIM:(c + 1) * HGRN_DIM]
            dstate[...] = dst
            da = jnp.where(amask, _dot_nt(dob, iv), 0.0).astype(BF16)
            ncat = jnp.concatenate(nxt, axis=1).astype(BF16)
            nstack = jnp.concatenate(nxt, axis=0).astype(BF16)
            ststack = jnp.concatenate(sts, axis=0).astype(BF16)
            div = _dot_tn(a, dob) + _dot_nt(keexp, ncat)
            dke = _dot(_expand(ivf, cid).astype(BF16), nstack)
            dqd = _dot(da, kib) + _dot(_expand(dof, cid).astype(BF16), ststack)
            dki = _dot_tn(da, qdb)
            dkk = dke * ke
            dkey = dki * enb + dke * ebe
            db = dqd * qd - dki * ki - dkk
            dbends = [_colsum(jnp.where(cid == c, dkk, 0.0)) + ddec[c] * dec[c] for c in range(NC)]
            dlogf = _suffix_sum_chunk(db, rmod) + _chunk_rows(dbends, cid)
            dforget = dlogf / forget - dkey
            df_ref[rows, :] = dforget * (1.0 - lbv) * sg * (1.0 - sg)
            dlb_ref[...] += _colsum(dforget * (1.0 - sg))
            dq_ref[rows, :] = dqd * eb * (sq * (1.0 + q * (1.0 - sq)))
            di_ref[rows, :] = div
            return carry

        lax.fori_loop(0, nsc, step, 0)

    col = lambda off: pl.BlockSpec((T, HGRN_DIM), lambda h: (0, off + h))
    own = pl.BlockSpec((T, HGRN_DIM), lambda h: (0, h))
    vec = pl.BlockSpec((1, HGRN_DIM), lambda h: (0, h))
    return pl.pallas_call(
        body, name="hgrn_bwd", grid=(HGRN_HEADS,),
        in_specs=[col(0), col(4), col(8), vec,
                  pl.BlockSpec((1, nsc, HGRN_DIM, HGRN_DIM), lambda h: (h, 0, 0, 0)), own],
        out_specs=[own, own, own, vec],
        out_shape=[jax.ShapeDtypeStruct((T, HGRN_W), F32)] * 3 + [jax.ShapeDtypeStruct((1, HGRN_W), F32)],
        scratch_shapes=[pltpu.VMEM((HGRN_DIM, HGRN_DIM), F32), pltpu.VMEM((SUPER, HGRN_DIM), F32)],
        compiler_params=_cp("arbitrary"),
    )(hg, hg, hg, lb, states, do)


def _rec_heads(rec, gate, g_h):
    rr = jnp.concatenate(
        [jnp.broadcast_to(_rms(rec[:, h * HGRN_DIM:(h + 1) * HGRN_DIM], HGRN_DIM), (rec.shape[0], HGRN_DIM))
         for h in range(HGRN_HEADS)], axis=1)
    rn = rec * rr
    sg = _sigmoid(gate)
    return rr, rn, sg


def _mix_out(attn_o, rec_o, hg, x, g_a, g_h, w_out, tm=256):
    T = x.shape[0]

    def body(a_ref, r_ref, gt_ref, x_ref, ga_ref, gh_ref, w_ref, h1_ref, mixed_ref):
        a = a_ref[...]
        an = a * _rms(a, ATTN_W) * ga_ref[...]
        gate = gt_ref[...]
        _, rn, sg = _rec_heads(r_ref[...], gate, gh_ref[...])
        mixed = jnp.concatenate([an, rn * gh_ref[...] * (gate * sg)], axis=1).astype(BF16)
        mixed_ref[...] = mixed
        h1_ref[...] = x_ref[...] + _dot(mixed, w_ref[...])

    row = lambda w: pl.BlockSpec((tm, w), lambda i: (i, 0))
    return pl.pallas_call(
        body, name="mix_out", grid=(T // tm,),
        in_specs=[row(ATTN_W), row(HGRN_W), pl.BlockSpec((tm, HGRN_W), lambda i: (i, 3)), row(D_MODEL),
                  _full((1, ATTN_W)), _full((1, HGRN_W)), _once((D_MODEL, D_MODEL))],
        out_specs=[row(D_MODEL), row(D_MODEL)],
        out_shape=[jax.ShapeDtypeStruct((T, D_MODEL), F32), jax.ShapeDtypeStruct((T, D_MODEL), BF16)],
        compiler_params=_cp("arbitrary"),
    )(attn_o, rec_o, hg, x, g_a, g_h, w_out)


_INV_SQRT2 = 1.0 / math.sqrt(2.0)
_INV_SQRT2PI = 1.0 / math.sqrt(2.0 * math.pi)


def _gelu(x):
    return 0.5 * x * (1.0 + lax.erf(x * _INV_SQRT2))


def _gelu_grad(x):
    return 0.5 * (1.0 + lax.erf(x * _INV_SQRT2)) + x * jnp.exp(-0.5 * x * x) * _INV_SQRT2PI


def _shift_down(g, prev, rowid):
    p1 = _row(prev, prev.shape[0] - 1)
    p2 = _row(prev, prev.shape[0] - 2)
    s1 = jnp.where(rowid == 0, p1, pltpu.roll(g, 1, 0))
    s2 = jnp.where(rowid == 0, p2, jnp.where(rowid == 1, p1, pltpu.roll(g, 2, 0)))
    return s1, s2


def _mlp_fwd(h1, g2, w_up4, conv_w, conv_b, w_down, gf, tgt, tm=256):
    T = h1.shape[0]
    half = D_FF // 2

    def body(h_ref, g2_ref, wu_ref, cw_ref, cb_ref, wd_ref, gf_ref, t_ref,
             u_ref, gate_ref, val_ref, dh_ref, loss_ref, dgf_ref, carry):
        i = pl.program_id(0)

        @pl.when(i == 0)
        def _():
            carry[...] = jnp.zeros_like(carry)
            loss_ref[...] = jnp.zeros_like(loss_ref)
            dgf_ref[...] = jnp.zeros_like(dgf_ref)

        h = h_ref[...]
        u = (h * _rms(h, D_MODEL) * g2_ref[...]).astype(BF16)
        u_ref[...] = u
        rowid = lax.broadcasted_iota(jnp.int32, (tm, half), 0)
        y2 = jnp.zeros((tm, D_MODEL), F32)
        for c in range(2):
            cols = slice(c * half, (c + 1) * half)
            gb = _dot(u, wu_ref[c]).astype(BF16)
            vb = _dot(u, wu_ref[2 + c]).astype(BF16)
            gate_ref[:, cols] = gb
            val_ref[:, cols] = vb
            g = gb.astype(F32)
            s1, s2 = _shift_down(g, carry[:, cols], rowid)
            carry[:, cols] = g[tm - 8:, :]
            conv = cb_ref[:, cols] + cw_ref[0:1, cols] * s2 + cw_ref[1:2, cols] * s1 + cw_ref[2:3, cols] * g
            act = (_gelu(conv) * vb.astype(F32)).astype(BF16)
            y2 = y2 + _dot(act, wd_ref[cols, :])
        h2 = h + y2
        rf = _rms(h2, D_MODEL)
        n = h2 * rf
        gfv = gf_ref[...]
        e = n * gfv - t_ref[...]
        loss_ref[...] += jnp.sum(e * e) * (0.5 / D_MODEL)
        dy = e * (1.0 / D_MODEL)
        dgf_ref[...] += _colsum(dy * n)
        dh_ref[...] = _rms_bwd(dy * gfv, n, rf, D_MODEL)

    row = lambda w: pl.BlockSpec((tm, w), lambda i: (i, 0))
    return pl.pallas_call(
        body, name="mlp_fwd", grid=(T // tm,),
        in_specs=[row(D_MODEL), _full((1, D_MODEL)), _once((N_CHIPS, D_MODEL, UP_SHARD)), _full((3, D_FF)),
                  _full((1, D_FF)), _once((D_FF, D_MODEL)), _full((1, D_MODEL)), row(D_MODEL)],
        out_specs=[row(D_MODEL), row(D_FF), row(D_FF), row(D_MODEL), _full((1, 128)), _full((1, D_MODEL))],
        out_shape=[jax.ShapeDtypeStruct((T, D_MODEL), BF16), jax.ShapeDtypeStruct((T, D_FF), BF16),
                   jax.ShapeDtypeStruct((T, D_FF), BF16), jax.ShapeDtypeStruct((T, D_MODEL), F32),
                   jax.ShapeDtypeStruct((1, 128), F32), jax.ShapeDtypeStruct((1, D_MODEL), F32)],
        scratch_shapes=[pltpu.VMEM((8, D_FF), F32)],
        compiler_params=_cp("arbitrary"),
    )(h1, g2, w_up4, conv_w, conv_b, w_down, gf, tgt)


def _mlp_bwd(dh2, gate, val, conv_w, conv_b, w_down, tm=256):
    T = dh2.shape[0]
    nb = T // tm
    half = D_FF // 2

    def body(dh_ref, gate_ref, halo_ref, val_ref, cw_ref, cb_ref, wd_ref,
             dgv_ref, act_ref, dcw_ref, dcb_ref, carry):
        i = pl.program_id(0)

        @pl.when(i == 0)
        def _():
            carry[...] = jnp.zeros_like(carry)
            dcw_ref[...] = jnp.zeros_like(dcw_ref)
            dcb_ref[...] = jnp.zeros_like(dcb_ref)

        dhb = dh_ref[...].astype(BF16)
        rowid = lax.broadcasted_iota(jnp.int32, (tm, half), 0)
        has_prev = (i < nb - 1).astype(F32)
        for c in range(2):
            cols = slice(c * half, (c + 1) * half)
            g = gate_ref[:, cols].astype(F32)
            v = val_ref[:, cols].astype(F32)
            s1, s2 = _shift_down(g, halo_ref[:, cols].astype(F32) * has_prev, rowid)
            conv = cb_ref[:, cols] + cw_ref[0:1, cols] * s2 + cw_ref[1:2, cols] * s1 + cw_ref[2:3, cols] * g
            gl = _gelu(conv)
            act_ref[:, cols] = (gl * v).astype(BF16)
            dact = _dot_nt(dhb, wd_ref[cols, :])
            dconv = dact * v * _gelu_grad(conv)
            dcb_ref[:, cols] += _colsum(dconv)
            dcw_ref[0:1, cols] += _colsum(dconv * s2)
            dcw_ref[1:2, cols] += _colsum(dconv * s1)
            dcw_ref[2:3, cols] += _colsum(dconv * g)
            nxt = carry[:, cols]
            n0, n1 = _row(nxt, 0), _row(nxt, 1)
            u1 = jnp.where(rowid == tm - 1, n0, pltpu.roll(dconv, tm - 1, 0))
            u2 = jnp.where(rowid == tm - 1, n1, jnp.where(rowid == tm - 2, n0, pltpu.roll(dconv, tm - 2, 0)))
            carry[:, cols] = dconv[0:8, :]
            dgate = cw_ref[2:3, cols] * dconv + cw_ref[1:2, cols] * u1 + cw_ref[0:1, cols] * u2
            dgv_ref[:, cols] = dgate.astype(BF16)
            dgv_ref[:, D_FF + c * half:D_FF + (c + 1) * half] = (dact * gl).astype(BF16)

    rev = lambda w: pl.BlockSpec((tm, w), lambda i: (nb - 1 - i, 0))
    halo = pl.BlockSpec((16, D_FF), lambda i: (jnp.maximum((nb - 1 - i) * (tm // 16) - 1, 0), 0))
    return pl.pallas_call(
        body, name="mlp_bwd", grid=(nb,),
        in_specs=[rev(D_MODEL), rev(D_FF), halo, rev(D_FF), _full((3, D_FF)), _full((1, D_FF)),
                  _once((D_FF, D_MODEL))],
        out_specs=[rev(2 * D_FF), rev(D_FF), _full((3, D_FF)), _full((1, D_FF))],
        out_shape=[jax.ShapeDtypeStruct((T, 2 * D_FF), BF16), jax.ShapeDtypeStruct((T, D_FF), BF16),
                   jax.ShapeDtypeStruct((3, D_FF), F32), jax.ShapeDtypeStruct((1, D_FF), F32)],
        scratch_shapes=[pltpu.VMEM((8, D_FF), F32)],
        compiler_params=_cp("arbitrary"),
    )(dh2, gate, gate, val, conv_w, conv_b, w_down)


def _up_out_bwd(dgv, w_up4, h1, g2, dh2, w_out, attn_o, rec_o, hg, g_a, g_h, tm=256):
    T = h1.shape[0]

    def body(dgv_ref, wu_ref, h_ref, g2_ref, dh2_ref, wo_ref, a_ref, r_ref, gt_ref, ga_ref, gh_ref,
             dh1_ref, dg2_ref, da_ref, dr_ref, dgt_ref, dga_ref, dgh_ref):
        @pl.when(pl.program_id(0) == 0)
        def _():
            dg2_ref[...] = jnp.zeros_like(dg2_ref)
            dga_ref[...] = jnp.zeros_like(dga_ref)
            dgh_ref[...] = jnp.zeros_like(dgh_ref)

        du = jnp.zeros((tm, D_MODEL), F32)
        for k in range(N_CHIPS):
            du = du + _dot_nt(dgv_ref[:, k * UP_SHARD:(k + 1) * UP_SHARD], wu_ref[k])
        h = h_ref[...]
        r = _rms(h, D_MODEL)
        n = h * r
        dg2_ref[...] += _colsum(du * n)
        dh1 = dh2_ref[...] + _rms_bwd(du * g2_ref[...], n, r, D_MODEL)
        dh1_ref[...] = dh1
        dmix = _dot_nt(dh1.astype(BF16), wo_ref[...])
        dan = dmix[:, :ATTN_W]
        a = a_ref[...]
        ra = _rms(a, ATTN_W)
        na = a * ra
        dga_ref[...] += _colsum(dan * na)
        da_ref[...] = _rms_bwd(dan * ga_ref[...], na, ra, ATTN_W)
        dmr = dmix[:, ATTN_W:]
        gate = gt_ref[...]
        ghv = gh_ref[...]
        rr, rn, sg = _rec_heads(r_ref[...], gate, ghv)
        dgt_ref[...] = dmr * rn * ghv * (sg * (1.0 + gate * (1.0 - sg)))
        drecn = dmr * (gate * sg)
        dgh_ref[...] += _colsum(drecn * rn)
        drn = drecn * ghv
        prod = drn * rn
        mean = jnp.concatenate(
            [jnp.broadcast_to(jnp.sum(prod[:, h_ * HGRN_DIM:(h_ + 1) * HGRN_DIM], axis=-1, keepdims=True),
                              (tm, HGRN_DIM)) for h_ in range(HGRN_HEADS)], axis=1) * (1.0 / HGRN_DIM)
        dr_ref[...] = rr * (drn - rn * mean)

    row = lambda w: pl.BlockSpec((tm, w), lambda i: (i, 0))
    return pl.pallas_call(
        body, name="up_out_bwd", grid=(T // tm,),
        in_specs=[row(2 * D_FF), _once((N_CHIPS, D_MODEL, UP_SHARD)), row(D_MODEL), _full((1, D_MODEL)),
                  row(D_MODEL), _once((D_MODEL, D_MODEL)), row(ATTN_W), row(HGRN_W),
                  pl.BlockSpec((tm, HGRN_W), lambda i: (i, 3)), _full((1, ATTN_W)), _full((1, HGRN_W))],
        out_specs=[row(D_MODEL), _full((1, D_MODEL)), row(ATTN_W), row(HGRN_W), row(HGRN_W),
                   _full((1, ATTN_W)), _full((1, HGRN_W))],
        out_shape=[jax.ShapeDtypeStruct((T, D_MODEL), F32), jax.ShapeDtypeStruct((1, D_MODEL), F32),
                   jax.ShapeDtypeStruct((T, ATTN_W), F32), jax.ShapeDtypeStruct((T, HGRN_W), F32),
                   jax.ShapeDtypeStruct((T, HGRN_W), F32), jax.ShapeDtypeStruct((1, ATTN_W), F32),
                   jax.ShapeDtypeStruct((1, HGRN_W), F32)],
        compiler_params=_cp("arbitrary"),
    )(dgv, w_up4, h1, g2, dh2, w_out, attn_o, rec_o, hg, g_a, g_h)


def _in_bwd(dqkv, dhg, w_in4, x, g1, dh1, tm=256):
    T = x.shape[0]

    def body(*refs):
        parts = refs[:7]
        w_ref, x_ref, g_ref, dh1_ref, dp_ref, dx_ref, dg_ref = refs[7:]

        @pl.when(pl.program_id(0) == 0)
        def _():
            dg_ref[...] = jnp.zeros_like(dg_ref)

        dp = jnp.concatenate([p[...] for p in parts], axis=1).astype(BF16)
        dp_ref[...] = dp
        du = jnp.zeros((tm, D_MODEL), F32)
        for k in range(N_CHIPS):
            du = du + _dot_nt(dp[:, k * IN_SHARD:(k + 1) * IN_SHARD], w_ref[k])
        xv = x_ref[...]
        r = _rms(xv, D_MODEL)
        n = xv * r
        dg_ref[...] += _colsum(du * n)
        dx_ref[...] = dh1_ref[...] + _rms_bwd(du * g_ref[...], n, r, D_MODEL)

    row = lambda w: pl.BlockSpec((tm, w), lambda i: (i, 0))
    return pl.pallas_call(
        body, name="in_bwd", grid=(T // tm,),
        in_specs=[row(ATTN_W)] * 7 + [_once((N_CHIPS, D_MODEL, IN_SHARD)), row(D_MODEL), _full((1, D_MODEL)),
                                       row(D_MODEL)],
        out_specs=[row(IN_TOTAL), row(D_MODEL), _full((1, D_MODEL))],
        out_shape=[jax.ShapeDtypeStruct((T, IN_TOTAL), BF16), jax.ShapeDtypeStruct((T, D_MODEL), F32),
                   jax.ShapeDtypeStruct((1, D_MODEL), F32)],
        compiler_params=_cp("arbitrary"),
    )(*dqkv, *dhg, w_in4, x, g1, dh1)


def _dw(a, b, kb, nb_, name, tk=512):
    T, K = a.shape
    N = b.shape[1]
    nk, nn, nt = K // kb, N // nb_, T // tk

    def body(a_ref, b_ref, o_ref, acc):
        t = pl.program_id(2)

        @pl.when(t == 0)
        def _():
            acc[...] = jnp.zeros_like(acc)

        acc[...] += _dot_tn(a_ref[...], b_ref[...].astype(BF16))

        @pl.when(t == nt - 1)
        def _():
            o_ref[0] = acc[...].astype(BF16)

    return pl.pallas_call(
        body, name=name, grid=(nk, nn, nt),
        in_specs=[pl.BlockSpec((tk, kb), lambda i, j, t: (t, i)), pl.BlockSpec((tk, nb_), lambda i, j, t: (t, j))],
        out_specs=pl.BlockSpec((1, kb, nb_), lambda i, j, t: (i * nn + j, 0, 0)),
        out_shape=jax.ShapeDtypeStruct((nk * nn, kb, nb_), BF16),
        scratch_shapes=[pltpu.VMEM((kb, nb_), F32)],
        compiler_params=_cp("arbitrary", "arbitrary", "arbitrary"),
    )(a, b)


def _local_step(x, tgt, g1, w_in4, g_a, g_h, lb, w_out, g2, w_up4, conv_w, conv_b, w_down, gf):
    u1, qkv, hg = _in_proj(x, g1, w_in4)
    run = None
    for n, d in enumerate(DILATIONS):
        run = _attn_fwd(qkv, run, d, last=(n == len(DILATIONS) - 1))
    attn_o, lse = run
    rec_o, states = _hgrn_fwd(hg, lb)
    h1, mixed = _mix_out(attn_o, rec_o, hg, x, g_a, g_h, w_out)
    u2, gate, val, dh2, loss, dgf = _mlp_fwd(h1, g2, w_up4, conv_w, conv_b, w_down, gf, tgt)

    dgv, act, dcw, dcb = _mlp_bwd(dh2, gate, val, conv_w, conv_b, w_down)
    dw_down = _dw(act, dh2, D_FF // 2, D_MODEL, "dw_down").reshape(N_CHIPS, D_FF // N_CHIPS, D_MODEL)
    dh1, dg2, da, dr, dgt, dga, dgh = _up_out_bwd(dgv, w_up4, h1, g2, dh2, w_out, attn_o, rec_o, hg, g_a, g_h)
    dw_up = _dw(u2, dgv, D_MODEL, UP_SHARD, "dw_up")
    dw_out = _dw(mixed, dh1, D_MODEL // N_CHIPS, D_MODEL, "dw_out")
    dqkv = None
    for d in DILATIONS:
        dqkv = _attn_bwd(qkv, attn_o, lse, da, dqkv, d)
    dhq, dhf, dhi, dlb = _hgrn_bwd(hg, lb, states, dr)
    dproj, dx, dg1 = _in_bwd(dqkv, [dhq, dhf, dhi, dgt], w_in4, x, g1, dh1)
    dw_in = _dw(u1, dproj, D_MODEL, IN_SHARD, "dw_in")
    return loss, dx, dict(g1=dg1, g_a=dga, g_h=dgh, lb=dlb, g2=dg2, conv_w=dcw, conv_b=dcb, gf=dgf), \
        dict(w_in=dw_in, w_out=dw_out, w_up=dw_up, w_down=dw_down)


BIG = ("w_in", "w_out", "w_up", "w_down")
ANY = pl.BlockSpec(memory_space=pl.ANY)


def _place():
    x, y, c = lax.axis_index("x"), lax.axis_index("y"), lax.axis_index("c")
    chips = [(1 - x, y), (x, 1 - y), (1 - x, 1 - y)]
    return x, y, c, chips


def _remote(src, dst, send_sems, recv_sems, k, to):
    return pltpu.make_async_remote_copy(src_ref=src, dst_ref=dst, send_sem=send_sems.at[k], recv_sem=recv_sems.at[k],
                                        device_id=to, device_id_type=MESH)


def _gather_weights(shards, conv_w):
    n = len(shards)
    halves = [s.shape[0] // 2 for s in shards]

    def body(*refs):
        ins, cw, outs, ocw = refs[:n], refs[n], refs[n + 1:2 * n + 1], refs[2 * n + 1]
        send_sems, recv_sems, loc_sems = refs[2 * n + 2:]
        x, y, c, chips = _place()
        me, sibling = 2 * x + y, (x, y, 1 - c)
        local = [pltpu.make_async_copy(ins[w], outs[w].at[me], loc_sems.at[w]) for w in range(n)]
        local.append(pltpu.make_async_copy(cw, ocw.at[me], loc_sems.at[n]))
        for cp in local:
            cp.start()

        def part(w, chip, half):
            return outs[w].at[chip, pl.ds(half * halves[w], halves[w]), :]

        sent = []
        for j, chip in enumerate(chips):
            for w in range(n):
                sent.append(_remote(ins[w].at[pl.ds(c * halves[w], halves[w]), :], part(w, me, c),
                                    send_sems, recv_sems, w * 3 + j, (*chip, c)))
            sent.append(_remote(cw, ocw.at[me], send_sems, recv_sems, 6 * n + j, (*chip, c)))
        for cp in sent:
            cp.start()
        for j, chip in enumerate(chips):
            kj = 2 * chip[0] + chip[1]
            for w in range(n):
                _remote(part(w, kj, c), part(w, kj, c), send_sems, recv_sems, w * 3 + j, (*chip, c)).wait_recv()
                fwd = _remote(part(w, kj, c), part(w, kj, c), send_sems, recv_sems, 3 * n + w * 3 + j, sibling)
                fwd.start()
                sent.append(fwd)
        for j, chip in enumerate(chips):
            kj = 2 * chip[0] + chip[1]
            for w in range(n):
                _remote(part(w, kj, 1 - c), part(w, kj, 1 - c), send_sems, recv_sems, 3 * n + w * 3 + j,
                        sibling).wait_recv()
            _remote(cw, ocw.at[kj], send_sems, recv_sems, 6 * n + j, (*chip, c)).wait_recv()
        for cp in sent:
            cp.wait_send()
        for cp in local:
            cp.wait()

    n_sem = 6 * n + 3
    return pl.pallas_call(
        body, name="gather_weights",
        in_specs=[ANY] * (n + 1), out_specs=[ANY] * (n + 1),
        out_shape=[jax.ShapeDtypeStruct((N_CHIPS,) + s.shape, s.dtype) for s in shards]
        + [jax.ShapeDtypeStruct((N_CHIPS,) + conv_w.shape, conv_w.dtype)],
        scratch_shapes=[pltpu.SemaphoreType.DMA((n_sem,)), pltpu.SemaphoreType.DMA((n_sem,)),
                        pltpu.SemaphoreType.DMA((n + 1,))],
    )(*shards, conv_w)


def _allreduce_small(buf):
    rows = buf.shape[0]

    def body(in_ref, out_ref, slots, send_sems, recv_sems):
        x, y, c, _ = _place()
        me = 4 * x + 2 * y + c
        slots[me] = in_ref[...]
        sent = []
        for p in range(1, 8):
            to = (x ^ (p >> 2), y ^ ((p >> 1) & 1), c ^ (p & 1))
            sent.append(_remote(in_ref, slots.at[me], send_sems, recv_sems, p, to))
        for cp in sent:
            cp.start()
        for p in range(1, 8):
            frm = 4 * (x ^ (p >> 2)) + 2 * (y ^ ((p >> 1) & 1)) + (c ^ (p & 1))
            _remote(in_ref, slots.at[frm], send_sems, recv_sems, p, (x, y, c)).wait_recv()
        for cp in sent:
            cp.wait_send()
        acc = slots[0]
        for d in range(1, 8):
            acc = acc + slots[d]
        out_ref[...] = acc

    vm = pl.BlockSpec(memory_space=pltpu.VMEM)
    return pl.pallas_call(
        body, name="allreduce_small", in_specs=[vm], out_specs=vm,
        out_shape=jax.ShapeDtypeStruct(buf.shape, F32),
        scratch_shapes=[pltpu.VMEM((8, rows, 128), F32), pltpu.SemaphoreType.DMA((8,)), pltpu.SemaphoreType.DMA((8,))],
    )(buf)


def _pair_exchange(gs):
    n = len(gs)
    halves = [g.shape[1] // 2 for g in gs]

    def body(*refs):
        g, mine, got = refs[:n], refs[n:2 * n], refs[2 * n:3 * n]
        send_sems, recv_sems, loc_sems = refs[3 * n:]
        x, y, c, _ = _place()
        cps = []
        for w in range(n):
            h = halves[w]
            cps.append(pltpu.make_async_copy(g[w].at[:, pl.ds(c * h, h), :], mine[w], loc_sems.at[w]))
            cps.append(_remote(g[w].at[:, pl.ds((1 - c) * h, h), :], got[w], send_sems, recv_sems, w, (x, y, 1 - c)))
        for cp in cps:
            cp.start()
        for cp in cps:
            cp.wait()

    half_shapes = [jax.ShapeDtypeStruct((N_CHIPS, h, g.shape[2]), g.dtype) for g, h in zip(gs, halves)]
    outs = pl.pallas_call(
        body, name="pair_exchange", in_specs=[ANY] * n, out_specs=[ANY] * (2 * n), out_shape=half_shapes * 2,
        scratch_shapes=[pltpu.SemaphoreType.DMA((n,)), pltpu.SemaphoreType.DMA((n,)), pltpu.SemaphoreType.DMA((n,))],
    )(*gs)
    return outs[:n], outs[n:]


def _pair_sum(a, b, name):
    def body(a_ref, b_ref, o_ref):
        o_ref[...] = (a_ref[...].astype(F32) + b_ref[...].astype(F32)).astype(BF16)

    blk = pl.BlockSpec((1,) + a.shape[1:], lambda k: (k, 0, 0))
    return pl.pallas_call(body, name=name, grid=(a.shape[0],), in_specs=[blk, blk], out_specs=blk,
                          out_shape=jax.ShapeDtypeStruct(a.shape, BF16), compiler_params=_cp("arbitrary"))(a, b)


def _chip_exchange(ps, mines, gots):
    n = len(ps)

    def body(*refs):
        p, mine, got = refs[:n], refs[n:2 * n], refs[2 * n:3 * n]
        landed, own_a, own_b = refs[3 * n:4 * n], refs[4 * n:5 * n], refs[5 * n:6 * n]
        send_sems, recv_sems, loc_sems = refs[6 * n:]
        x, y, c, chips = _place()
        me = 2 * x + y
        cps = []
        for w in range(n):
            cps.append(pltpu.make_async_copy(mine[w].at[me], own_a[w], loc_sems.at[2 * w]))
            cps.append(pltpu.make_async_copy(got[w].at[me], own_b[w], loc_sems.at[2 * w + 1]))
            for j, chip in enumerate(chips):
                cps.append(_remote(p[w].at[2 * chip[0] + chip[1]], landed[w].at[j], send_sems, recv_sems,
                                   w * 3 + j, (*chip, c)))
        for cp in cps:
            cp.start()
        for cp in cps:
            cp.wait()

    shp = [a.shape[1:] for a in ps]
    outs = pl.pallas_call(
        body, name="chip_exchange", in_specs=[ANY] * (3 * n), out_specs=[ANY] * (3 * n),
        out_shape=[jax.ShapeDtypeStruct((3,) + s, BF16) for s in shp] + [jax.ShapeDtypeStruct(s, BF16) for s in shp] * 2,
        scratch_shapes=[pltpu.SemaphoreType.DMA((3 * n,)), pltpu.SemaphoreType.DMA((3 * n,)),
                        pltpu.SemaphoreType.DMA((2 * n,))],
    )(*ps, *mines, *gots)
    return outs[:n], outs[n:2 * n], outs[2 * n:]


def _sum_partials(own_a, own_b, landed, name):
    def body(a_ref, b_ref, l_ref, o_ref):
        acc = a_ref[...].astype(F32) + b_ref[...].astype(F32)
        for j in range(3):
            acc = acc + l_ref[j].astype(F32)
        o_ref[...] = acc

    vm = pl.BlockSpec(memory_space=pltpu.VMEM)
    return pl.pallas_call(body, name=name, in_specs=[vm] * 3, out_specs=vm,
                          out_shape=jax.ShapeDtypeStruct(own_a.shape, F32), compiler_params=_cp())(own_a, own_b, landed)


def _pair_share(reds):
    n = len(reds)

    def body(*refs):
        r, out = refs[:n], refs[n:2 * n]
        send_sems, recv_sems, loc_sems = refs[2 * n:]
        x, y, c, _ = _place()
        cps = []
        for w in range(n):
            h = r[w].shape[0]
            rows = out[w].at[pl.ds(c * h, h), :]
            cps.append(pltpu.make_async_copy(r[w], rows, loc_sems.at[w]))
            cps.append(_remote(r[w], rows, send_sems, recv_sems, w, (x, y, 1 - c)))
        for cp in cps:
            cp.start()
        for cp in cps:
            cp.wait()

    return pl.pallas_call(
        body, name="pair_share", in_specs=[ANY] * n, out_specs=[ANY] * n,
        out_shape=[jax.ShapeDtypeStruct((2 * r.shape[0], r.shape[1]), F32) for r in reds],
        scratch_shapes=[pltpu.SemaphoreType.DMA((n,)), pltpu.SemaphoreType.DMA((n,)), pltpu.SemaphoreType.DMA((n,))],
    )(*reds)


def _adamw(w, g, m, v, name, tr=64):
    R, C = w.shape
    tr = min(tr, R)

    def body(w_ref, g_ref, m_ref, v_ref, d_ref, nm_ref, nv_ref):
        gv = g_ref[...]
        nm = ADAM_B1 * m_ref[...] + (1.0 - ADAM_B1) * gv
        nv = ADAM_B2 * v_ref[...] + (1.0 - ADAM_B2) * (gv * gv)
        m_hat = nm / (1.0 - ADAM_B1 ** ADAM_STEP)
        v_hat = nv / (1.0 - ADAM_B2 ** ADAM_STEP)
        d_ref[...] = -ADAM_LR * (m_hat / (jnp.sqrt(v_hat) + ADAM_EPS) + ADAM_WD * w_ref[...])
        nm_ref[...] = nm
        nv_ref[...] = nv

    blk = pl.BlockSpec((tr, C), lambda i: (i, 0))
    return pl.pallas_call(body, name=name, grid=(R // tr,), in_specs=[blk] * 4, out_specs=[blk] * 3,
                          out_shape=[jax.ShapeDtypeStruct((R, C), F32)] * 3, compiler_params=_cp("arbitrary"))(w, g, m, v)


SMALL = (("norm1_g", 1024), ("attn_norm_g", 512), ("hgrn_norm_g", 512), ("hgrn_lb_logits", 1024), ("norm2_g", 1024),
         ("conv_b", D_FF), ("final_norm_g", 1024), ("conv_w", 3 * D_FF))
SMALL_ROWS = 136


def _pack(parts, rows):
    flat = jnp.concatenate([p.reshape(-1).astype(F32) for p in parts])
    return jnp.pad(flat, (0, rows * 128 - flat.shape[0])).reshape(rows, 128)


def kernel(x, norm1_g, w_in, attn_norm_g, hgrn_norm_g, hgrn_lb_logits, w_out, norm2_g, w_up, conv_w, conv_b, w_down, final_norm_g, loss_target, m_norm1_g, m_w_in, m_attn_norm_g, m_hgrn_norm_g, m_hgrn_lb_logits, m_w_out, m_norm2_g, m_w_up, m_conv_w, m_conv_b, m_w_down, m_final_norm_g, v_norm1_g, v_w_in, v_attn_norm_g, v_hgrn_norm_g, v_hgrn_lb_logits, v_w_out, v_norm2_g, v_w_up, v_conv_w, v_conv_b, v_w_down, v_final_norm_g):
    w = dict(norm1_g=norm1_g, w_in=w_in, attn_norm_g=attn_norm_g, hgrn_norm_g=hgrn_norm_g,
             hgrn_lb_logits=hgrn_lb_logits, w_out=w_out, norm2_g=norm2_g, w_up=w_up, conv_w=conv_w, conv_b=conv_b,
             w_down=w_down, final_norm_g=final_norm_g)
    m = dict(norm1_g=m_norm1_g, w_in=m_w_in, attn_norm_g=m_attn_norm_g, hgrn_norm_g=m_hgrn_norm_g,
             hgrn_lb_logits=m_hgrn_lb_logits, w_out=m_w_out, norm2_g=m_norm2_g, w_up=m_w_up, conv_w=m_conv_w,
             conv_b=m_conv_b, w_down=m_w_down, final_norm_g=m_final_norm_g)
    v = dict(norm1_g=v_norm1_g, w_in=v_w_in, attn_norm_g=v_attn_norm_g, hgrn_norm_g=v_hgrn_norm_g,
             hgrn_lb_logits=v_hgrn_lb_logits, w_out=v_w_out, norm2_g=v_norm2_g, w_up=v_w_up, conv_w=v_conv_w,
             conv_b=v_conv_b, w_down=v_w_down, final_norm_g=v_final_norm_g)
    names = list(w)
    chip = 2 * lax.axis_index("x") + lax.axis_index("y")

    full = _gather_weights([w[k][0].astype(BF16) for k in BIG], conv_w[0])
    w_in4, w_out4, w_up4, w_down4, conv_w4 = full
    conv_w_full = jnp.transpose(conv_w4, (1, 0, 2)).reshape(3, D_FF)
    lb = jax.nn.softmax(hgrn_lb_logits, axis=0)[0:1]

    loss, dx, small, big = _local_step(
        x[0], loss_target[0], norm1_g, w_in4, attn_norm_g, hgrn_norm_g, lb, w_out4.reshape(D_MODEL, D_MODEL), norm2_g,
        w_up4, conv_w_full, conv_b, w_down4.reshape(D_FF, D_MODEL), final_norm_g.reshape(1, D_MODEL))

    dlb = small["lb"] * lb * (1.0 - lb)
    grads_small = dict(norm1_g=small["g1"], attn_norm_g=small["g_a"], hgrn_norm_g=small["g_h"],
                       hgrn_lb_logits=jnp.concatenate([dlb, -dlb], axis=0), norm2_g=small["g2"],
                       conv_b=small["conv_b"], final_norm_g=small["gf"], conv_w=small["conv_w"])
    summed = _allreduce_small(_pack([grads_small[k] for k, _ in SMALL] + [loss[0, 0:1]], SMALL_ROWS)).reshape(-1)
    g = {}
    off = 0
    for k, size in SMALL:
        g[k] = summed[off:off + size]
        off += size
    loss_total = summed[off]
    g["conv_w"] = lax.dynamic_slice(g["conv_w"].reshape(3, D_FF), (0, chip * (D_FF // N_CHIPS)), (3, D_FF // N_CHIPS))

    gs = [big[k] for k in BIG]
    mines, gots = _pair_exchange(gs)
    ps = [_pair_sum(a, b, f"pair_sum_{k}") for a, b, k in zip(mines, gots, BIG)]
    landed, own_a, own_b = _chip_exchange(ps, mines, gots)
    reds = [_sum_partials(a, b, l, f"sum_partials_{k}") for a, b, l, k in zip(own_a, own_b, landed, BIG)]
    for k, shard in zip(BIG, _pair_share(reds)):
        g[k] = shard

    delta, new_m, new_v = {}, {}, {}
    for k in BIG:
        delta[k], new_m[k], new_v[k] = (a[None] for a in _adamw(w[k][0], g[k], m[k][0], v[k][0], f"adamw_{k}"))
        g[k] = g[k][None]
    small_names = [k for k in names if k not in BIG]
    rows = 80
    packed = _adamw(_pack([w[k] for k in small_names], rows), _pack([g[k] for k in small_names], rows),
                    _pack([m[k] for k in small_names], rows), _pack([v[k] for k in small_names], rows), "adamw_small", tr=rows)
    flat = [a.reshape(-1) for a in packed]
    off = 0
    for k in small_names:
        size = w[k].size
        delta[k], new_m[k], new_v[k] = (a[off:off + size].reshape(w[k].shape) for a in flat)
        g[k] = g[k].reshape(w[k].shape)
        off += size

    return (loss_total, dx[None], *[g[k] for k in names], *[delta[k] for k in names],
            *[new_m[k] for k in names], *[new_v[k] for k in names])
```

```python
import functools
import math

import jax
import jax.numpy as jnp
from jax import lax
from jax.experimental import pallas as pl
from jax.experimental.pallas import tpu as pltpu

F32 = jnp.float32
BF16 = jnp.bfloat16

D_MODEL = 1024
ATTN_W = 512
HGRN_W = 512
HEAD_PAIR = 128
ATTN_BLK = 128
DILATIONS = (1, 4, 16)
HGRN_HEADS = 4
HGRN_DIM = 128
HGRN_CHUNK = 64
SUPER = 256
D_FF = 2816
N_CHIPS = 4
IN_TOTAL = 3584
IN_SHARD = IN_TOTAL // N_CHIPS
UP_SHARD = 2 * D_FF // N_CHIPS
QKV_W = 3 * ATTN_W
HG_W = 4 * HGRN_W
EPS = 1e-6
NEG = -1e30
V7X_VMEM_BYTES = 64 * 1024 * 1024
VMEM_LIMIT = V7X_VMEM_BYTES - 8 * 1024 * 1024

ADAM_LR = 0.001
ADAM_B1 = 0.9
ADAM_B2 = 0.999
ADAM_EPS = 1e-08
ADAM_WD = 0.01
ADAM_STEP = 10

MESH = pl.DeviceIdType.MESH


def _cp(*sem):
    return pltpu.CompilerParams(dimension_semantics=sem or None, vmem_limit_bytes=VMEM_LIMIT)


def _dot(a, b):
    return jnp.dot(a, b, preferred_element_type=F32)


def _dot_nt(a, b):
    return lax.dot_general(a, b, (((1,), (1,)), ((), ())), preferred_element_type=F32)


def _dot_tn(a, b):
    return lax.dot_general(a, b, (((0,), (0,)), ((), ())), preferred_element_type=F32)


def _sigmoid(x):
    return 1.0 / (1.0 + jnp.exp(-x))


def _rms(x, width):
    return lax.rsqrt(jnp.sum(x * x, axis=-1, keepdims=True) * (1.0 / width) + EPS)


def _rms_bwd(dn, n, r, width):
    return r * (dn - n * (jnp.sum(dn * n, axis=-1, keepdims=True) * (1.0 / width)))


def _colsum(x):
    return jnp.sum(x, axis=0, keepdims=True)


def _row(v, k):
    rid = lax.broadcasted_iota(jnp.int32, v.shape, 0)
    return jnp.sum(jnp.where(rid == k, v, 0.0), axis=0, keepdims=True)


def _full(shape):
    return pl.BlockSpec(shape, lambda *_: (0,) * len(shape))


def _once(shape):
    return pl.BlockSpec(shape, lambda *_: (0,) * len(shape), pipeline_mode=pl.Buffered(1))


def _in_proj(x, g1, w_in4, tm=256):
    T = x.shape[0]

    def body(x_ref, g_ref, w_ref, u_ref, qkv_ref, hg_ref):
        xv = x_ref[...]
        u = (xv * _rms(xv, D_MODEL) * g_ref[...]).astype(BF16)
        u_ref[...] = u
        p0 = _dot(u, w_ref[0])
        p1 = _dot(u, w_ref[1])
        qkv_ref[:, 0:IN_SHARD] = p0
        qkv_ref[:, IN_SHARD:QKV_W] = p1[:, :QKV_W - IN_SHARD]
        hg_ref[:, 0:2 * IN_SHARD - QKV_W] = p1[:, QKV_W - IN_SHARD:]
        hg_ref[:, 2 * IN_SHARD - QKV_W:3 * IN_SHARD - QKV_W] = _dot(u, w_ref[2])
        hg_ref[:, 3 * IN_SHARD - QKV_W:HG_W] = _dot(u, w_ref[3])

    return pl.pallas_call(
        body, name="in_proj", grid=(T // tm,),
        in_specs=[pl.BlockSpec((tm, D_MODEL), lambda i: (i, 0)), _full((1, D_MODEL)),
                  _once((N_CHIPS, D_MODEL, IN_SHARD))],
        out_specs=[pl.BlockSpec((tm, D_MODEL), lambda i: (i, 0)), pl.BlockSpec((tm, QKV_W), lambda i: (i, 0)),
                   pl.BlockSpec((tm, HG_W), lambda i: (i, 0))],
        out_shape=[jax.ShapeDtypeStruct((T, D_MODEL), BF16), jax.ShapeDtypeStruct((T, QKV_W), F32),
                   jax.ShapeDtypeStruct((T, HG_W), F32)],
        compiler_params=_cp("arbitrary"),
    )(x, g1, w_in4)


def _attn_masks():
    lane = lax.broadcasted_iota(jnp.int32, (ATTN_BLK, HEAD_PAIR), 1)
    first = lane < 64
    row = lax.broadcasted_iota(jnp.int32, (2 * ATTN_BLK, 2 * ATTN_BLK), 0)
    col = lax.broadcasted_iota(jnp.int32, (2 * ATTN_BLK, 2 * ATTN_BLK), 1)
    base = jnp.where(row >= ATTN_BLK, row - ATTN_BLK, row) - col
    return first, base


def _two_heads(blk, first):
    zero = jnp.zeros_like(blk)
    return jnp.concatenate([jnp.where(first, blk, zero), jnp.where(first, zero, blk)], axis=0)


def _attn_rows(idx, nb, d):
    r, n = idx // nb, idx % nb
    kb = jnp.maximum(n - 1, 0)
    if d == 1:
        q0 = pl.multiple_of(n * ATTN_BLK, ATTN_BLK)
        k0 = pl.multiple_of(kb * ATTN_BLK, ATTN_BLK)
        return pl.ds(q0, ATTN_BLK), pl.ds(k0, 2 * ATTN_BLK), q0 - k0
    return (pl.ds(r + d * ATTN_BLK * n, ATTN_BLK, stride=d), pl.ds(r + d * ATTN_BLK * kb, 2 * ATTN_BLK, stride=d),
            (n - kb) * ATTN_BLK)


def _attn_fwd(qkv):
    T = qkv.shape[0]

    def body(q_ref, k_ref, v_ref, o_ref, m_ref, l_ref):
        first, base = _attn_masks()
        for bi, d in enumerate(DILATIONS):
            nb = T // d // ATTN_BLK

            def step(idx, carry, d=d, nb=nb, bi=bi):
                rows, keys, shift = _attn_rows(idx, nb, d)
                q2 = _two_heads(q_ref[rows, :] * 0.125, first).astype(BF16)
                kw = k_ref[keys, :].astype(BF16)
                vw = v_ref[keys, :].astype(BF16)
                dist = base + shift
                s = jnp.where((dist >= 0) & (dist <= ATTN_BLK), _dot_nt(q2, kw), NEG)
                mb = jnp.max(s, axis=-1, keepdims=True)
                p = jnp.exp(s - mb)
                lb = jnp.sum(p, axis=-1, keepdims=True)
                o2 = _dot(p.astype(BF16), vw)
                o = jnp.where(first, o2[:ATTN_BLK], o2[ATTN_BLK:])
                m = jnp.where(first, mb[:ATTN_BLK], mb[ATTN_BLK:])
                l = jnp.where(first, lb[:ATTN_BLK], lb[ATTN_BLK:])
                if bi:
                    pm = m_ref[rows, :]
                    mn = jnp.maximum(pm, m)
                    wa = jnp.exp(pm - mn)
                    wb = jnp.exp(m - mn)
                    o = o_ref[rows, :] * wa + o * wb
                    l = l_ref[rows, :] * wa + l * wb
                    m = mn
                o_ref[rows, :] = o
                m_ref[rows, :] = m
                l_ref[rows, :] = l
                return carry

            lax.fori_loop(0, d * nb, step, 0)

        def finish(i, carry):
            rows = pl.ds(pl.multiple_of(i * SUPER, SUPER), SUPER)
            l = l_ref[rows, :]
            o_ref[rows, :] = o_ref[rows, :] / l
            m_ref[rows, :] = m_ref[rows, :] + jnp.log(l)
            return carry

        lax.fori_loop(0, T // SUPER, finish, 0)

    col = lambda off: pl.BlockSpec((T, HEAD_PAIR), lambda j: (0, off + j))
    return pl.pallas_call(
        body, name="attn_fwd", grid=(4,),
        in_specs=[col(0), col(4), col(8)], out_specs=[col(0), col(0)],
        out_shape=[jax.ShapeDtypeStruct((T, ATTN_W), F32)] * 2,
        scratch_shapes=[pltpu.VMEM((T, HEAD_PAIR), F32)],
        compiler_params=_cp("arbitrary"),
    )(qkv, qkv, qkv)


def _attn_bwd(qkv, o, lse, do):
    T = qkv.shape[0]

    def body(q_ref, k_ref, v_ref, o_ref, lse_ref, do_ref, dq_ref, dk_ref, dv_ref):
        first, base = _attn_masks()
        dq_ref[...] = jnp.zeros_like(dq_ref)
        dk_ref[...] = jnp.zeros_like(dk_ref)
        dv_ref[...] = jnp.zeros_like(dv_ref)
        for d in DILATIONS:
            nb = T // d // ATTN_BLK

            def step(idx, carry, d=d, nb=nb):
                rows, keys, shift = _attn_rows(idx, nb, d)
                q2 = _two_heads(q_ref[rows, :], first).astype(BF16)
                kw = k_ref[keys, :].astype(BF16)
                vw = v_ref[keys, :].astype(BF16)
                dist = base + shift
                valid = (dist >= 0) & (dist <= ATTN_BLK)
                lse_b = lse_ref[rows, :]
                lse2 = jnp.concatenate(
                    [jnp.max(jnp.where(first, lse_b, NEG), axis=-1, keepdims=True),
                     jnp.max(jnp.where(first, NEG, lse_b), axis=-1, keepdims=True)], axis=0)
                p = jnp.where(valid, jnp.exp(_dot_nt(q2, kw) * 0.125 - lse2), 0.0)
                dob = do_ref[rows, :]
                prod = dob * o_ref[rows, :]
                delta = jnp.concatenate(
                    [jnp.sum(jnp.where(first, prod, 0.0), axis=-1, keepdims=True),
                     jnp.sum(jnp.where(first, 0.0, prod), axis=-1, keepdims=True)], axis=0)
                do2 = _two_heads(dob, first).astype(BF16)
                ds = (p * (_dot_nt(do2, vw) - delta) * 0.125).astype(BF16)
                dq2 = _dot(ds, kw)
                dq_ref[rows, :] += jnp.where(first, dq2[:ATTN_BLK], dq2[ATTN_BLK:])
                dk_ref[keys, :] += _dot_tn(ds, q2)
                dv_ref[keys, :] += _dot_tn(p.astype(BF16), do2)
                return carry

            lax.fori_loop(0, d * nb, step, 0)

    col = lambda off: pl.BlockSpec((T, HEAD_PAIR), lambda j: (0, off + j))
    return pl.pallas_call(
        body, name="attn_bwd", grid=(4,),
        in_specs=[col(0), col(4), col(8), col(0), col(0), col(0)], out_specs=[col(0)] * 3,
        out_shape=[jax.ShapeDtypeStruct((T, ATTN_W), F32)] * 3,
        compiler_params=_cp("arbitrary"),
    )(qkv, qkv, qkv, o, lse, do)


def _chunk_ids():
    row = lax.broadcasted_iota(jnp.int32, (SUPER, HGRN_DIM), 0)
    r2 = lax.broadcasted_iota(jnp.int32, (SUPER, SUPER), 0)
    c2 = lax.broadcasted_iota(jnp.int32, (SUPER, SUPER), 1)
    amask = ((r2 // HGRN_CHUNK) == (c2 // HGRN_CHUNK)) & (c2 <= r2)
    return row % HGRN_CHUNK, row // HGRN_CHUNK, amask


def _cumsum_chunk(x, rmod):
    s = 1
    while s < HGRN_CHUNK:
        x = x + jnp.where(rmod >= s, pltpu.roll(x, s, 0), 0.0)
        s *= 2
    return x


def _suffix_sum_chunk(x, rmod):
    s = 1
    while s < HGRN_CHUNK:
        x = x + jnp.where(rmod < HGRN_CHUNK - s, pltpu.roll(x, SUPER - s, 0), 0.0)
        s *= 2
    return x


def _chunk_rows(vs, cid):
    out = vs[-1]
    for c in reversed(range(len(vs) - 1)):
        out = jnp.where(cid == c, vs[c], out)
    return out


def _expand(x, cid):
    return jnp.concatenate([jnp.where(cid == c, x, 0.0) for c in range(SUPER // HGRN_CHUNK)], axis=1)


def _hgrn_gates(q, f, lbv, rmod, cid, tmp):
    sq = _sigmoid(q)
    sg = _sigmoid(f)
    forget = lbv + (1.0 - lbv) * sg
    key = 1.0 - forget
    b = _cumsum_chunk(jnp.log(forget), rmod)
    tmp[...] = b
    bends = [tmp[c * HGRN_CHUNK + HGRN_CHUNK - 1:(c + 1) * HGRN_CHUNK, :] for c in range(SUPER // HGRN_CHUNK)]
    eb = jnp.exp(b)
    enb = jnp.exp(-b)
    ebe = jnp.exp(_chunk_rows(bends, cid) - b)
    return sq, sg, forget, key, eb, enb, ebe, q * sq * eb, key * enb, key * ebe, [jnp.exp(v) for v in bends]


def _hgrn_fwd(hg, lb):
    T = hg.shape[0]
    nsc = T // SUPER
    NC = SUPER // HGRN_CHUNK

    def body(q_ref, f_ref, i_ref, lb_ref, o_ref, st_ref, state, tmp):
        rmod, cid, amask = _chunk_ids()
        state[...] = jnp.zeros_like(state)
        lbv = lb_ref[...]

        def step(sc, carry):
            rows = pl.ds(pl.multiple_of(sc * SUPER, SUPER), SUPER)
            iv = i_ref[rows, :].astype(BF16)
            qd, ki, ke, dec = _hgrn_gates(q_ref[rows, :], f_ref[rows, :], lbv, rmod, cid, tmp)[-4:]
            a = jnp.where(amask, _dot_nt(qd.astype(BF16), ki.astype(BF16)), 0.0)
            o = _dot(a.astype(BF16), iv)
            ut = _dot_tn(iv, _expand(ke, cid).astype(BF16))
            st = state[...]
            st_ref[0, sc] = st
            sts = []
            for c in range(NC):
                sts.append(st)
                st = st * dec[c] + ut[:, c * HGRN_DIM:(c + 1) * HGRN_DIM]
            state[...] = st
            o = o + _dot_nt(_expand(qd, cid).astype(BF16), jnp.concatenate(sts, axis=1).astype(BF16))
            o_ref[rows, :] = o
            return carry

        lax.fori_loop(0, nsc, step, 0)

    col = lambda off: pl.BlockSpec((T, HGRN_DIM), lambda h: (0, off + h))
    return pl.pallas_call(
        body, name="hgrn_fwd", grid=(HGRN_HEADS,),
        in_specs=[col(0), col(4), col(8), pl.BlockSpec((1, HGRN_DIM), lambda h: (0, h))],
        out_specs=[pl.BlockSpec((T, HGRN_DIM), lambda h: (0, h)),
                   pl.BlockSpec((1, nsc, HGRN_DIM, HGRN_DIM), lambda h: (h, 0, 0, 0))],
        out_shape=[jax.ShapeDtypeStruct((T, HGRN_W), F32),
                   jax.ShapeDtypeStruct((HGRN_HEADS, nsc, HGRN_DIM, HGRN_DIM), F32)],
        scratch_shapes=[pltpu.VMEM((HGRN_DIM, HGRN_DIM), F32), pltpu.VMEM((SUPER, HGRN_DIM), F32)],
        compiler_params=_cp("arbitrary"),
    )(hg, hg, hg, lb)


def _hgrn_bwd(hg, lb, states, do):
    T = hg.shape[0]
    nsc = T // SUPER
    NC = SUPER // HGRN_CHUNK

    def body(q_ref, f_ref, i_ref, lb_ref, st_ref, do_ref, dq_ref, df_ref, di_ref, dlb_ref, dstate, tmp):
        rmod, cid, amask = _chunk_ids()
        dstate[...] = jnp.zeros_like(dstate)
        dlb_ref[...] = jnp.zeros_like(dlb_ref)
        lbv = lb_ref[...]

        def step(k, carry):
            sc = nsc - 1 - k
            rows = pl.ds(pl.multiple_of(sc * SUPER, SUPER), SUPER)
            q = q_ref[rows, :]
            ivf = i_ref[rows, :]
            iv = ivf.astype(BF16)
            dof = do_ref[rows, :]
            dob = dof.astype(BF16)
            sq, sg, forget, key, eb, enb, ebe, qd, ki, ke, dec = _hgrn_gates(q, f_ref[rows, :], lbv, rmod, cid, tmp)
            qdb, kib = qd.astype(BF16), ki.astype(BF16)
            keexp = _expand(ke, cid).astype(BF16)
            a = jnp.where(amask, _dot_nt(qdb, kib), 0.0).astype(BF16)
            ut = _dot_tn(iv, keexp)
            st = st_ref[0, sc]
            sts = []
            for c in range(NC):
                sts.append(st)
                st = st * dec[c] + ut[:, c * HGRN_DIM:(c + 1) * HGRN_DIM]
            gt = _dot_tn(dob, _expand(qd, cid).astype(BF16))
            nxt = [None] * NC
            ddec = [None] * NC
            dst = dstate[...]
            for c in reversed(range(NC)):
                nxt[c] = dst
                ddec[c] = _colsum(dst * sts[c])
                dst = dst * dec[c] + gt[:, c * HGRN_DIM:(c + 1) * HGRN_DIM]
            dstate[...] = dst
            da = jnp.where(amask, _dot_nt(dob, iv), 0.0).astype(BF16)
            ncat = jnp.concatenate(nxt, axis=1).astype(BF16)
            nstack = jnp.concatenate(nxt, axis=0).astype(BF16)
            ststack = jnp.concatenate(sts, axis=0).astype(BF16)
            div = _dot_tn(a, dob) + _dot_nt(keexp, ncat)
            dke = _dot(_expand(ivf, cid).astype(BF16), nstack)
            dqd = _dot(da, kib) + _dot(_expand(dof, cid).astype(BF16), ststack)
            dki = _dot_tn(da, qdb)
            dkk = dke * ke
            dkey = dki * enb + dke * ebe
            db = dqd * qd - dki * ki - dkk
            dbends = [_colsum(jnp.where(cid == c, dkk, 0.0)) + ddec[c] * dec[c] for c in range(NC)]
            dlogf = _suffix_sum_chunk(db, rmod) + _chunk_rows(dbends, cid)
            dforget = dlogf / forget - dkey
            df_ref[rows, :] = dforget * (1.0 - lbv) * sg * (1.0 - sg)
            dlb_ref[...] += _colsum(dforget * (1.0 - sg))
            dq_ref[rows, :] = dqd * eb * (sq * (1.0 + q * (1.0 - sq)))
            di_ref[rows, :] = div
            return carry

        lax.fori_loop(0, nsc, step, 0)

    col = lambda off: pl.BlockSpec((T, HGRN_DIM), lambda h: (0, off + h))
    own = pl.BlockSpec((T, HGRN_DIM), lambda h: (0, h))
    vec = pl.BlockSpec((1, HGRN_DIM), lambda h: (0, h))
    return pl.pallas_call(
        body, name="hgrn_bwd", grid=(HGRN_HEADS,),
        in_specs=[col(0), col(4), col(8), vec,
                  pl.BlockSpec((1, nsc, HGRN_DIM, HGRN_DIM), lambda h: (h, 0, 0, 0)), own],
        out_specs=[own, own, own, vec],
        out_shape=[jax.ShapeDtypeStruct((T, HGRN_W), F32)] * 3 + [jax.ShapeDtypeStruct((1, HGRN_W), F32)],
        scratch_shapes=[pltpu.VMEM((HGRN_DIM, HGRN_DIM), F32), pltpu.VMEM((SUPER, HGRN_DIM), F32)],
        compiler_params=_cp("arbitrary"),
    )(hg, hg, hg, lb, states, do)


def _rec_heads(rec, gate, g_h):
    rr = jnp.concatenate(
        [jnp.broadcast_to(_rms(rec[:, h * HGRN_DIM:(h + 1) * HGRN_DIM], HGRN_DIM), (rec.shape[0], HGRN_DIM))
         for h in range(HGRN_HEADS)], axis=1)
    rn = rec * rr
    sg = _sigmoid(gate)
    return rr, rn, sg


def _mix_out(attn_o, rec_o, hg, x, g_a, g_h, w_out, tm=256):
    T = x.shape[0]

    def body(a_ref, r_ref, gt_ref, x_ref, ga_ref, gh_ref, w_ref, h1_ref, mixed_ref):
        a = a_ref[...]
        an = a * _rms(a, ATTN_W) * ga_ref[...]
        gate = gt_ref[...]
        _, rn, sg = _rec_heads(r_ref[...], gate, gh_ref[...])
        mixed = jnp.concatenate([an, rn * gh_ref[...] * (gate * sg)], axis=1).astype(BF16)
        mixed_ref[...] = mixed
        h1_ref[...] = x_ref[...] + _dot(mixed, w_ref[...])

    row = lambda w: pl.BlockSpec((tm, w), lambda i: (i, 0))
    return pl.pallas_call(
        body, name="mix_out", grid=(T // tm,),
        in_specs=[row(ATTN_W), row(HGRN_W), pl.BlockSpec((tm, HGRN_W), lambda i: (i, 3)), row(D_MODEL),
                  _full((1, ATTN_W)), _full((1, HGRN_W)), _once((D_MODEL, D_MODEL))],
        out_specs=[row(D_MODEL), row(D_MODEL)],
        out_shape=[jax.ShapeDtypeStruct((T, D_MODEL), F32), jax.ShapeDtypeStruct((T, D_MODEL), BF16)],
        compiler_params=_cp("arbitrary"),
    )(attn_o, rec_o, hg, x, g_a, g_h, w_out)


_INV_SQRT2 = 1.0 / math.sqrt(2.0)
_INV_SQRT2PI = 1.0 / math.sqrt(2.0 * math.pi)


def _gelu(x):
    return 0.5 * x * (1.0 + lax.erf(x * _INV_SQRT2))


def _gelu_grad(x):
    return 0.5 * (1.0 + lax.erf(x * _INV_SQRT2)) + x * jnp.exp(-0.5 * x * x) * _INV_SQRT2PI


def _shift_down(g, prev, rowid):
    p1 = _row(prev, prev.shape[0] - 1)
    p2 = _row(prev, prev.shape[0] - 2)
    s1 = jnp.where(rowid == 0, p1, pltpu.roll(g, 1, 0))
    s2 = jnp.where(rowid == 0, p2, jnp.where(rowid == 1, p1, pltpu.roll(g, 2, 0)))
    return s1, s2


def _mlp_fwd(h1, g2, w_up4, conv_w, conv_b, w_down, gf, tgt, tm=256):
    T = h1.shape[0]
    half = D_FF // 2

    def body(h_ref, g2_ref, wu_ref, cw_ref, cb_ref, wd_ref, gf_ref, t_ref,
             u_ref, gate_ref, val_ref, dh_ref, loss_ref, dgf_ref, carry):
        i = pl.program_id(0)

        @pl.when(i == 0)
        def _():
            carry[...] = jnp.zeros_like(carry)
            loss_ref[...] = jnp.zeros_like(loss_ref)
            dgf_ref[...] = jnp.zeros_like(dgf_ref)

        h = h_ref[...]
        u = (h * _rms(h, D_MODEL) * g2_ref[...]).astype(BF16)
        u_ref[...] = u
        rowid = lax.broadcasted_iota(jnp.int32, (tm, half), 0)
        y2 = jnp.zeros((tm, D_MODEL), F32)
        for c in range(2):
            cols = slice(c * half, (c + 1) * half)
            gb = _dot(u, wu_ref[c]).astype(BF16)
            vb = _dot(u, wu_ref[2 + c]).astype(BF16)
            gate_ref[:, cols] = gb
            val_ref[:, cols] = vb
            g = gb.astype(F32)
            s1, s2 = _shift_down(g, carry[:, cols], rowid)
            carry[:, cols] = g[tm - 8:, :]
            conv = cb_ref[:, cols] + cw_ref[0:1, cols] * s2 + cw_ref[1:2, cols] * s1 + cw_ref[2:3, cols] * g
            act = (_gelu(conv) * vb.astype(F32)).astype(BF16)
            y2 = y2 + _dot(act, wd_ref[cols, :])
        h2 = h + y2
        rf = _rms(h2, D_MODEL)
        n = h2 * rf
        gfv = gf_ref[...]
        e = n * gfv - t_ref[...]
        loss_ref[...] += jnp.sum(e * e) * (0.5 / D_MODEL)
        dy = e * (1.0 / D_MODEL)
        dgf_ref[...] += _colsum(dy * n)
        dh_ref[...] = _rms_bwd(dy * gfv, n, rf, D_MODEL)

    row = lambda w: pl.BlockSpec((tm, w), lambda i: (i, 0))
    return pl.pallas_call(
        body, name="mlp_fwd", grid=(T // tm,),
        in_specs=[row(D_MODEL), _full((1, D_MODEL)), _once((N_CHIPS, D_MODEL, UP_SHARD)), _full((3, D_FF)),
                  _full((1, D_FF)), _once((D_FF, D_MODEL)), _full((1, D_MODEL)), row(D_MODEL)],
        out_specs=[row(D_MODEL), row(D_FF), row(D_FF), row(D_MODEL), _full((1, 128)), _full((1, D_MODEL))],
        out_shape=[jax.ShapeDtypeStruct((T, D_MODEL), BF16), jax.ShapeDtypeStruct((T, D_FF), BF16),
                   jax.ShapeDtypeStruct((T, D_FF), BF16), jax.ShapeDtypeStruct((T, D_MODEL), F32),
                   jax.ShapeDtypeStruct((1, 128), F32), jax.ShapeDtypeStruct((1, D_MODEL), F32)],
        scratch_shapes=[pltpu.VMEM((8, D_FF), F32)],
        compiler_params=_cp("arbitrary"),
    )(h1, g2, w_up4, conv_w, conv_b, w_down, gf, tgt)


def _mlp_bwd(dh2, gate, val, conv_w, conv_b, w_down, tm=256):
    T = dh2.shape[0]
    nb = T // tm
    half = D_FF // 2

    def body(dh_ref, gate_ref, halo_ref, val_ref, cw_ref, cb_ref, wd_ref,
             dgv_ref, act_ref, dcw_ref, dcb_ref, carry):
        i = pl.program_id(0)

        @pl.when(i == 0)
        def _():
            carry[...] = jnp.zeros_like(carry)
            dcw_ref[...] = jnp.zeros_like(dcw_ref)
            dcb_ref[...] = jnp.zeros_like(dcb_ref)

        dhb = dh_ref[...].astype(BF16)
        rowid = lax.broadcasted_iota(jnp.int32, (tm, half), 0)
        has_prev = (i < nb - 1).astype(F32)
        for c in range(2):
            cols = slice(c * half, (c + 1) * half)
            g = gate_ref[:, cols].astype(F32)
            v = val_ref[:, cols].astype(F32)
            s1, s2 = _shift_down(g, halo_ref[:, cols].astype(F32) * has_prev, rowid)
            conv = cb_ref[:, cols] + cw_ref[0:1, cols] * s2 + cw_ref[1:2, cols] * s1 + cw_ref[2:3, cols] * g
            gl = _gelu(conv)
            act_ref[:, cols] = (gl * v).astype(BF16)
            dact = _dot_nt(dhb, wd_ref[cols, :])
            dconv = dact * v * _gelu_grad(conv)
            dcb_ref[:, cols] += _colsum(dconv)
            dcw_ref[0:1, cols] += _colsum(dconv * s2)
            dcw_ref[1:2, cols] += _colsum(dconv * s1)
            dcw_ref[2:3, cols] += _colsum(dconv * g)
            nxt = carry[:, cols]
            n0, n1 = _row(nxt, 0), _row(nxt, 1)
            u1 = jnp.where(rowid == tm - 1, n0, pltpu.roll(dconv, tm - 1, 0))
            u2 = jnp.where(rowid == tm - 1, n1, jnp.where(rowid == tm - 2, n0, pltpu.roll(dconv, tm - 2, 0)))
            carry[:, cols] = dconv[0:8, :]
            dgate = cw_ref[2:3, cols] * dconv + cw_ref[1:2, cols] * u1 + cw_ref[0:1, cols] * u2
            dgv_ref[:, cols] = dgate.astype(BF16)
            dgv_ref[:, D_FF + c * half:D_FF + (c + 1) * half] = (dact * gl).astype(BF16)

    rev = lambda w: pl.BlockSpec((tm, w), lambda i: (nb - 1 - i, 0))
    halo = pl.BlockSpec((16, D_FF), lambda i: (jnp.maximum((nb - 1 - i) * (tm // 16) - 1, 0), 0))
    return pl.pallas_call(
        body, name="mlp_bwd", grid=(nb,),
        in_specs=[rev(D_MODEL), rev(D_FF), halo, rev(D_FF), _full((3, D_FF)), _full((1, D_FF)),
                  _once((D_FF, D_MODEL))],
        out_specs=[rev(2 * D_FF), rev(D_FF), _full((3, D_FF)), _full((1, D_FF))],
        out_shape=[jax.ShapeDtypeStruct((T, 2 * D_FF), BF16), jax.ShapeDtypeStruct((T, D_FF), BF16),
                   jax.ShapeDtypeStruct((3, D_FF), F32), jax.ShapeDtypeStruct((1, D_FF), F32)],
        scratch_shapes=[pltpu.VMEM((8, D_FF), F32)],
        compiler_params=_cp("arbitrary"),
    )(dh2, gate, gate, val, conv_w, conv_b, w_down)


def _up_out_bwd(dgv, w_up4, h1, g2, dh2, w_out, attn_o, rec_o, hg, g_a, g_h, tm=256):
    T = h1.shape[0]

    def body(dgv_ref, wu_ref, h_ref, g2_ref, dh2_ref, wo_ref, a_ref, r_ref, gt_ref, ga_ref, gh_ref,
             dh1_ref, dg2_ref, da_ref, dr_ref, dgt_ref, dga_ref, dgh_ref):
        @pl.when(pl.program_id(0) == 0)
        def _():
            dg2_ref[...] = jnp.zeros_like(dg2_ref)
            dga_ref[...] = jnp.zeros_like(dga_ref)
            dgh_ref[...] = jnp.zeros_like(dgh_ref)

        du = jnp.zeros((tm, D_MODEL), F32)
        for k in range(N_CHIPS):
            du = du + _dot_nt(dgv_ref[:, k * UP_SHARD:(k + 1) * UP_SHARD], wu_ref[k])
        h = h_ref[...]
        r = _rms(h, D_MODEL)
        n = h * r
        dg2_ref[...] += _colsum(du * n)
        dh1 = dh2_ref[...] + _rms_bwd(du * g2_ref[...], n, r, D_MODEL)
        dh1_ref[...] = dh1
        dmix = _dot_nt(dh1.astype(BF16), wo_ref[...])
        dan = dmix[:, :ATTN_W]
        a = a_ref[...]
        ra = _rms(a, ATTN_W)
        na = a * ra
        dga_ref[...] += _colsum(dan * na)
        da_ref[...] = _rms_bwd(dan * ga_ref[...], na, ra, ATTN_W)
        dmr = dmix[:, ATTN_W:]
        gate = gt_ref[...]
        ghv = gh_ref[...]
        rr, rn, sg = _rec_heads(r_ref[...], gate, ghv)
        dgt_ref[...] = dmr * rn * ghv * (sg * (1.0 + gate * (1.0 - sg)))
        drecn = dmr * (gate * sg)
        dgh_ref[...] += _colsum(drecn * rn)
        drn = drecn * ghv
        prod = drn * rn
        mean = jnp.concatenate(
            [jnp.broadcast_to(jnp.sum(prod[:, h_ * HGRN_DIM:(h_ + 1) * HGRN_DIM], axis=-1, keepdims=True),
                              (tm, HGRN_DIM)) for h_ in range(HGRN_HEADS)], axis=1) * (1.0 / HGRN_DIM)
        dr_ref[...] = rr * (drn - rn * mean)

    row = lambda w: pl.BlockSpec((tm, w), lambda i: (i, 0))
    return pl.pallas_call(
        body, name="up_out_bwd", grid=(T // tm,),
        in_specs=[row(2 * D_FF), _once((N_CHIPS, D_MODEL, UP_SHARD)), row(D_MODEL), _full((1, D_MODEL)),
                  row(D_MODEL), _once((D_MODEL, D_MODEL)), row(ATTN_W), row(HGRN_W),
                  pl.BlockSpec((tm, HGRN_W), lambda i: (i, 3)), _full((1, ATTN_W)), _full((1, HGRN_W))],
        out_specs=[row(D_MODEL), _full((1, D_MODEL)), row(ATTN_W), row(HGRN_W), row(HGRN_W),
                   _full((1, ATTN_W)), _full((1, HGRN_W))],
        out_shape=[jax.ShapeDtypeStruct((T, D_MODEL), F32), jax.ShapeDtypeStruct((1, D_MODEL), F32),
                   jax.ShapeDtypeStruct((T, ATTN_W), F32), jax.ShapeDtypeStruct((T, HGRN_W), F32),
                   jax.ShapeDtypeStruct((T, HGRN_W), F32), jax.ShapeDtypeStruct((1, ATTN_W), F32),
                   jax.ShapeDtypeStruct((1, HGRN_W), F32)],
        compiler_params=_cp("arbitrary"),
    )(dgv, w_up4, h1, g2, dh2, w_out, attn_o, rec_o, hg, g_a, g_h)


def _in_bwd(dqkv, dhg, w_in4, x, g1, dh1, tm=256):
    T = x.shape[0]

    def body(*refs):
        parts = refs[:7]
        w_ref, x_ref, g_ref, dh1_ref, dp_ref, dx_ref, dg_ref = refs[7:]

        @pl.when(pl.program_id(0) == 0)
        def _():
            dg_ref[...] = jnp.zeros_like(dg_ref)

        dp = jnp.concatenate([p[...] for p in parts], axis=1).astype(BF16)
        dp_ref[...] = dp
        du = jnp.zeros((tm, D_MODEL), F32)
        for k in range(N_CHIPS):
            du = du + _dot_nt(dp[:, k * IN_SHARD:(k + 1) * IN_SHARD], w_ref[k])
        xv = x_ref[...]
        r = _rms(xv, D_MODEL)
        n = xv * r
        dg_ref[...] += _colsum(du * n)
        dx_ref[...] = dh1_ref[...] + _rms_bwd(du * g_ref[...], n, r, D_MODEL)

    row = lambda w: pl.BlockSpec((tm, w), lambda i: (i, 0))
    return pl.pallas_call(
        body, name="in_bwd", grid=(T // tm,),
        in_specs=[row(ATTN_W)] * 7 + [_once((N_CHIPS, D_MODEL, IN_SHARD)), row(D_MODEL), _full((1, D_MODEL)),
                                       row(D_MODEL)],
        out_specs=[row(IN_TOTAL), row(D_MODEL), _full((1, D_MODEL))],
        out_shape=[jax.ShapeDtypeStruct((T, IN_TOTAL), BF16), jax.ShapeDtypeStruct((T, D_MODEL), F32),
                   jax.ShapeDtypeStruct((1, D_MODEL), F32)],
        compiler_params=_cp("arbitrary"),
    )(*dqkv, *dhg, w_in4, x, g1, dh1)


def _dw(a, b, kb, nb_, name, tk=512):
    T, K = a.shape
    N = b.shape[1]
    nk, nn, nt = K // kb, N // nb_, T // tk

    def body(a_ref, b_ref, o_ref, acc):
        t = pl.program_id(2)

        @pl.when(t == 0)
        def _():
            acc[...] = jnp.zeros_like(acc)

        acc[...] += _dot_tn(a_ref[...], b_ref[...].astype(BF16))

        @pl.when(t == nt - 1)
        def _():
            o_ref[0] = acc[...].astype(BF16)

    return pl.pallas_call(
        body, name=name, grid=(nk, nn, nt),
        in_specs=[pl.BlockSpec((tk, kb), lambda i, j, t: (t, i)), pl.BlockSpec((tk, nb_), lambda i, j, t: (t, j))],
        out_specs=pl.BlockSpec((1, kb, nb_), lambda i, j, t: (i * nn + j, 0, 0)),
        out_shape=jax.ShapeDtypeStruct((nk * nn, kb, nb_), BF16),
        scratch_shapes=[pltpu.VMEM((kb, nb_), F32)],
        compiler_params=_cp("arbitrary", "arbitrary", "arbitrary"),
    )(a, b)


def _local_step(x, tgt, g1, w_in4, g_a, g_h, lb, w_out, g2, w_up4, conv_w, conv_b, w_down, gf):
    u1, qkv, hg = _in_proj(x, g1, w_in4)
    attn_o, lse = _attn_fwd(qkv)
    rec_o, states = _hgrn_fwd(hg, lb)
    h1, mixed = _mix_out(attn_o, rec_o, hg, x, g_a, g_h, w_out)
    u2, gate, val, dh2, loss, dgf = _mlp_fwd(h1, g2, w_up4, conv_w, conv_b, w_down, gf, tgt)

    dgv, act, dcw, dcb = _mlp_bwd(dh2, gate, val, conv_w, conv_b, w_down)
    dw_down = _dw(act, dh2, D_FF // 2, D_MODEL, "dw_down").reshape(N_CHIPS, D_FF // N_CHIPS, D_MODEL)
    dh1, dg2, da, dr, dgt, dga, dgh = _up_out_bwd(dgv, w_up4, h1, g2, dh2, w_out, attn_o, rec_o, hg, g_a, g_h)
    dw_up = _dw(u2, dgv, D_MODEL, UP_SHARD, "dw_up")
    dw_out = _dw(mixed, dh1, D_MODEL // N_CHIPS, D_MODEL, "dw_out")
    dqkv = _attn_bwd(qkv, attn_o, lse, da)
    dhq, dhf, dhi, dlb = _hgrn_bwd(hg, lb, states, dr)
    dproj, dx, dg1 = _in_bwd(dqkv, [dhq, dhf, dhi, dgt], w_in4, x, g1, dh1)
    dw_in = _dw(u1, dproj, D_MODEL, IN_SHARD, "dw_in")
    return loss, dx, dict(g1=dg1, g_a=dga, g_h=dgh, lb=dlb, g2=dg2, conv_w=dcw, conv_b=dcb, gf=dgf), \
        dict(w_in=dw_in, w_out=dw_out, w_up=dw_up, w_down=dw_down)


BIG = ("w_in", "w_out", "w_up", "w_down")
ANY = pl.BlockSpec(memory_space=pl.ANY)


def _place():
    x, y, c = lax.axis_index("x"), lax.axis_index("y"), lax.axis_index("c")
    chips = [(1 - x, y), (x, 1 - y), (1 - x, 1 - y)]
    return x, y, c, chips


def _remote(src, dst, send_sems, recv_sems, k, to):
    return pltpu.make_async_remote_copy(src_ref=src, dst_ref=dst, send_sem=send_sems.at[k], recv_sem=recv_sems.at[k],
                                        device_id=to, device_id_type=MESH)


def _gather_weights(shards, conv_w):
    n = len(shards)
    halves = [s.shape[0] // 2 for s in shards]

    def body(*refs):
        ins, cw, outs, ocw = refs[:n], refs[n], refs[n + 1:2 * n + 1], refs[2 * n + 1]
        send_sems, recv_sems = refs[2 * n + 2:]
        x, y, c, chips = _place()
        me, sibling = 2 * x + y, (x, y, 1 - c)

        def part(w, chip, half):
            return outs[w].at[chip, pl.ds(half * halves[w], halves[w]), :]

        sent = []
        for j, chip in enumerate(chips):
            for w in range(n):
                sent.append(_remote(ins[w].at[pl.ds(c * halves[w], halves[w]), :], part(w, me, c),
                                    send_sems, recv_sems, w * 3 + j, (*chip, c)))
            sent.append(_remote(cw, ocw.at[me], send_sems, recv_sems, 6 * n + j, (*chip, c)))
        for cp in sent:
            cp.start()
        for j, chip in enumerate(chips):
            kj = 2 * chip[0] + chip[1]
            for w in range(n):
                _remote(part(w, kj, c), part(w, kj, c), send_sems, recv_sems, w * 3 + j, (*chip, c)).wait_recv()
                fwd = _remote(part(w, kj, c), part(w, kj, c), send_sems, recv_sems, 3 * n + w * 3 + j, sibling)
                fwd.start()
                sent.append(fwd)
        for j, chip in enumerate(chips):
            kj = 2 * chip[0] + chip[1]
            for w in range(n):
                _remote(part(w, kj, 1 - c), part(w, kj, 1 - c), send_sems, recv_sems, 3 * n + w * 3 + j,
                        sibling).wait_recv()
            _remote(cw, ocw.at[kj], send_sems, recv_sems, 6 * n + j, (*chip, c)).wait_recv()
        for cp in sent:
            cp.wait_send()

    n_sem = 6 * n + 3
    outs = pl.pallas_call(
        body, name="gather_weights",
        in_specs=[ANY] * (n + 1), out_specs=[ANY] * (n + 1),
        out_shape=[jax.ShapeDtypeStruct((N_CHIPS,) + s.shape, s.dtype) for s in shards]
        + [jax.ShapeDtypeStruct((N_CHIPS,) + conv_w.shape, conv_w.dtype)],
        scratch_shapes=[pltpu.SemaphoreType.DMA((n_sem,)), pltpu.SemaphoreType.DMA((n_sem,))],
    )(*shards, conv_w)
    chip = 2 * lax.axis_index("x") + lax.axis_index("y")
    return [lax.dynamic_update_slice(o, s[None], (chip,) + (0,) * s.ndim) for o, s in zip(outs, [*shards, conv_w])]


def _allreduce_small(buf):
    rows = buf.shape[0]

    def body(in_ref, out_ref, slots, send_sems, recv_sems):
        x, y, c, _ = _place()
        me = 4 * x + 2 * y + c
        slots[me] = in_ref[...]
        sent = []
        for p in range(1, 8):
            to = (x ^ (p >> 2), y ^ ((p >> 1) & 1), c ^ (p & 1))
            sent.append(_remote(in_ref, slots.at[me], send_sems, recv_sems, p, to))
        for cp in sent:
            cp.start()
        for p in range(1, 8):
            frm = 4 * (x ^ (p >> 2)) + 2 * (y ^ ((p >> 1) & 1)) + (c ^ (p & 1))
            _remote(in_ref, slots.at[frm], send_sems, recv_sems, p, (x, y, c)).wait_recv()
        for cp in sent:
            cp.wait_send()
        acc = slots[0]
        for d in range(1, 8):
            acc = acc + slots[d]
        out_ref[...] = acc

    vm = pl.BlockSpec(memory_space=pltpu.VMEM)
    return pl.pallas_call(
        body, name="allreduce_small", in_specs=[vm], out_specs=vm,
        out_shape=jax.ShapeDtypeStruct(buf.shape, F32),
        scratch_shapes=[pltpu.VMEM((8, rows, 128), F32), pltpu.SemaphoreType.DMA((8,)), pltpu.SemaphoreType.DMA((8,))],
    )(buf)


def _pair_exchange(gs):
    n = len(gs)
    halves = [g.shape[1] // 2 for g in gs]

    def body(*refs):
        g, got = refs[:n], refs[n:2 * n]
        send_sems, recv_sems = refs[2 * n:]
        x, y, c, _ = _place()
        cps = [_remote(g[w].at[:, pl.ds((1 - c) * halves[w], halves[w]), :], got[w], send_sems, recv_sems, w,
                       (x, y, 1 - c)) for w in range(n)]
        for cp in cps:
            cp.start()
        for cp in cps:
            cp.wait()

    return pl.pallas_call(
        body, name="pair_exchange", in_specs=[ANY] * n, out_specs=[ANY] * n,
        out_shape=[jax.ShapeDtypeStruct((N_CHIPS, h, g.shape[2]), g.dtype) for g, h in zip(gs, halves)],
        scratch_shapes=[pltpu.SemaphoreType.DMA((n,)), pltpu.SemaphoreType.DMA((n,))],
    )(*gs)


def _core_id():
    return lax.axis_index("c").reshape(1).astype(jnp.int32)


def _pair_sum(g, got, name):
    h, C = got.shape[1:]

    def body(c_ref, g_ref, b_ref, o_ref):
        o_ref[...] = (g_ref[...].astype(F32) + b_ref[...].astype(F32)).astype(BF16)

    blk = pl.BlockSpec((1, h, C), lambda k, c_ref: (k, 0, 0))
    return pl.pallas_call(
        body, name=name,
        grid_spec=pltpu.PrefetchScalarGridSpec(
            num_scalar_prefetch=1, grid=(N_CHIPS,),
            in_specs=[pl.BlockSpec((1, h, C), lambda k, c_ref: (k, c_ref[0], 0)), blk], out_specs=blk),
        out_shape=jax.ShapeDtypeStruct(got.shape, BF16), compiler_params=_cp("arbitrary"))(_core_id(), g, got)


def _chip_exchange(ps):
    n = len(ps)

    def body(*refs):
        p, landed = refs[:n], refs[n:2 * n]
        send_sems, recv_sems = refs[2 * n:]
        x, y, c, chips = _place()
        cps = [_remote(p[w].at[2 * chip[0] + chip[1]], landed[w].at[j], send_sems, recv_sems, w * 3 + j, (*chip, c))
               for w in range(n) for j, chip in enumerate(chips)]
        for cp in cps:
            cp.start()
        for cp in cps:
            cp.wait()

    return pl.pallas_call(
        body, name="chip_exchange", in_specs=[ANY] * n, out_specs=[ANY] * n,
        out_shape=[jax.ShapeDtypeStruct((3,) + a.shape[1:], BF16) for a in ps],
        scratch_shapes=[pltpu.SemaphoreType.DMA((3 * n,)), pltpu.SemaphoreType.DMA((3 * n,))],
    )(*ps)


def _sum_partials(g, got, landed, name):
    h, C = got.shape[1:]

    def body(ids, g_ref, b_ref, l_ref, o_ref):
        acc = g_ref[0].astype(F32) + b_ref[0].astype(F32)
        for j in range(3):
            acc = acc + l_ref[j].astype(F32)
        o_ref[...] = acc

    ids = jnp.stack([2 * lax.axis_index("x") + lax.axis_index("y"), lax.axis_index("c")]).astype(jnp.int32)
    return pl.pallas_call(
        body, name=name,
        grid_spec=pltpu.PrefetchScalarGridSpec(
            num_scalar_prefetch=1, grid=(1,),
            in_specs=[pl.BlockSpec((1, h, C), lambda i, ids: (ids[0], ids[1], 0)),
                      pl.BlockSpec((1, h, C), lambda i, ids: (ids[0], 0, 0)),
                      pl.BlockSpec((3, h, C), lambda i, ids: (0, 0, 0))],
            out_specs=pl.BlockSpec((h, C), lambda i, ids: (ids[1], 0))),
        out_shape=jax.ShapeDtypeStruct((2 * h, C), F32), compiler_params=_cp("arbitrary"))(ids, g, got, landed)


def _pair_share(reds):
    n = len(reds)

    def body(*refs):
        out = refs[n:2 * n]
        send_sems, recv_sems = refs[2 * n:]
        x, y, c, _ = _place()
        def half(w, which):
            h = out[w].shape[0] // 2
            return out[w].at[pl.ds(which * h, h), :]

        cps = [_remote(half(w, c), half(w, c), send_sems, recv_sems, w, (x, y, 1 - c)) for w in range(n)]
        for cp in cps:
            cp.start()
        for w in range(n):
            _remote(half(w, 1 - c), half(w, 1 - c), send_sems, recv_sems, w, (x, y, 1 - c)).wait_recv()
        for cp in cps:
            cp.wait_send()

    return pl.pallas_call(
        body, name="pair_share", in_specs=[ANY] * n, out_specs=[ANY] * n,
        out_shape=[jax.ShapeDtypeStruct(r.shape, F32) for r in reds],
        input_output_aliases={w: w for w in range(n)},
        scratch_shapes=[pltpu.SemaphoreType.DMA((n,)), pltpu.SemaphoreType.DMA((n,))],
    )(*reds)


def _adamw(w, g, m, v, name, tr=64):
    R, C = w.shape
    tr = min(tr, R)

    def body(w_ref, g_ref, m_ref, v_ref, d_ref, nm_ref, nv_ref):
        gv = g_ref[...]
        nm = ADAM_B1 * m_ref[...] + (1.0 - ADAM_B1) * gv
        nv = ADAM_B2 * v_ref[...] + (1.0 - ADAM_B2) * (gv * gv)
        m_hat = nm / (1.0 - ADAM_B1 ** ADAM_STEP)
        v_hat = nv / (1.0 - ADAM_B2 ** ADAM_STEP)
        d_ref[...] = -ADAM_LR * (m_hat / (jnp.sqrt(v_hat) + ADAM_EPS) + ADAM_WD * w_ref[...])
        nm_ref[...] = nm
        nv_ref[...] = nv

    blk = pl.BlockSpec((tr, C), lambda i: (i, 0))
    return pl.pallas_call(body, name=name, grid=(R // tr,), in_specs=[blk] * 4, out_specs=[blk] * 3,
                          out_shape=[jax.ShapeDtypeStruct((R, C), F32)] * 3, compiler_params=_cp("arbitrary"))(w, g, m, v)


SMALL = (("norm1_g", 1024), ("attn_norm_g", 512), ("hgrn_norm_g", 512), ("hgrn_lb_logits", 1024), ("norm2_g", 1024),
         ("conv_b", D_FF), ("final_norm_g", 1024), ("conv_w", 3 * D_FF))
SMALL_ROWS = 136


def _pack(parts, rows):
    flat = jnp.concatenate([p.reshape(-1).astype(F32) for p in parts])
    return jnp.pad(flat, (0, rows * 128 - flat.shape[0])).reshape(rows, 128)


def kernel(x, norm1_g, w_in, attn_norm_g, hgrn_norm_g, hgrn_lb_logits, w_out, norm2_g, w_up, conv_w, conv_b, w_down, final_norm_g, loss_target, m_norm1_g, m_w_in, m_attn_norm_g, m_hgrn_norm_g, m_hgrn_lb_logits, m_w_out, m_norm2_g, m_w_up, m_conv_w, m_conv_b, m_w_down, m_final_norm_g, v_norm1_g, v_w_in, v_attn_norm_g, v_hgrn_norm_g, v_hgrn_lb_logits, v_w_out, v_norm2_g, v_w_up, v_conv_w, v_conv_b, v_w_down, v_final_norm_g):
    w = dict(norm1_g=norm1_g, w_in=w_in, attn_norm_g=attn_norm_g, hgrn_norm_g=hgrn_norm_g,
             hgrn_lb_logits=hgrn_lb_logits, w_out=w_out, norm2_g=norm2_g, w_up=w_up, conv_w=conv_w, conv_b=conv_b,
             w_down=w_down, final_norm_g=final_norm_g)
    m = dict(norm1_g=m_norm1_g, w_in=m_w_in, attn_norm_g=m_attn_norm_g, hgrn_norm_g=m_hgrn_norm_g,
             hgrn_lb_logits=m_hgrn_lb_logits, w_out=m_w_out, norm2_g=m_norm2_g, w_up=m_w_up, conv_w=m_conv_w,
             conv_b=m_conv_b, w_down=m_w_down, final_norm_g=m_final_norm_g)
    v = dict(norm1_g=v_norm1_g, w_in=v_w_in, attn_norm_g=v_attn_norm_g, hgrn_norm_g=v_hgrn_norm_g,
             hgrn_lb_logits=v_hgrn_lb_logits, w_out=v_w_out, norm2_g=v_norm2_g, w_up=v_w_up, conv_w=v_conv_w,
             conv_b=v_conv_b, w_down=v_w_down, final_norm_g=v_final_norm_g)
    names = list(w)
    chip = 2 * lax.axis_index("x") + lax.axis_index("y")

    full = _gather_weights([w[k][0].astype(BF16) for k in BIG], conv_w[0])
    w_in4, w_out4, w_up4, w_down4, conv_w4 = full
    conv_w_full = jnp.transpose(conv_w4, (1, 0, 2)).reshape(3, D_FF)
    lb = jax.nn.softmax(hgrn_lb_logits, axis=0)[0:1]

    loss, dx, small, big = _local_step(
        x[0], loss_target[0], norm1_g, w_in4, attn_norm_g, hgrn_norm_g, lb, w_out4.reshape(D_MODEL, D_MODEL), norm2_g,
        w_up4, conv_w_full, conv_b, w_down4.reshape(D_FF, D_MODEL), final_norm_g.reshape(1, D_MODEL))

    dlb = small["lb"] * lb * (1.0 - lb)
    grads_small = dict(norm1_g=small["g1"], attn_norm_g=small["g_a"], hgrn_norm_g=small["g_h"],
                       hgrn_lb_logits=jnp.concatenate([dlb, -dlb], axis=0), norm2_g=small["g2"],
                       conv_b=small["conv_b"], final_norm_g=small["gf"], conv_w=small["conv_w"])
    summed = _allreduce_small(_pack([grads_small[k] for k, _ in SMALL] + [loss[0, 0:1]], SMALL_ROWS)).reshape(-1)
    g = {}
    off = 0
    for k, size in SMALL:
        g[k] = summed[off:off + size]
        off += size
    loss_total = summed[off]
    g["conv_w"] = lax.dynamic_slice(g["conv_w"].reshape(3, D_FF), (0, chip * (D_FF // N_CHIPS)), (3, D_FF // N_CHIPS))

    gs = [big[k] for k in BIG]
    gots = _pair_exchange(gs)
    ps = [_pair_sum(a, b, f"pair_sum_{k}") for a, b, k in zip(gs, gots, BIG)]
    landed = _chip_exchange(ps)
    reds = [_sum_partials(a, b, l, f"sum_partials_{k}") for a, b, l, k in zip(gs, gots, landed, BIG)]
    for k, shard in zip(BIG, _pair_share(reds)):
        g[k] = shard

    delta, new_m, new_v = {}, {}, {}
    for k in BIG:
        delta[k], new_m[k], new_v[k] = (a[None] for a in _adamw(w[k][0], g[k], m[k][0], v[k][0], f"adamw_{k}"))
        g[k] = g[k][None]
    small_names = [k for k in names if k not in BIG]
    rows = 80
    packed = _adamw(_pack([w[k] for k in small_names], rows), _pack([g[k] for k in small_names], rows),
                    _pack([m[k] for k in small_names], rows), _pack([v[k] for k in small_names], rows), "adamw_small", tr=rows)
    flat = [a.reshape(-1) for a in packed]
    off = 0
    for k in small_names:
        size = w[k].size
        delta[k], new_m[k], new_v[k] = (a[off:off + size].reshape(w[k].shape) for a in flat)
        g[k] = g[k].reshape(w[k].shape)
        off += size

    return (loss_total, dx[None], *[g[k] for k in names], *[delta[k] for k in names],
            *[new_m[k] for k in names], *[new_v[k] for k in names])
```

```python
import functools
import math

import jax
import jax.numpy as jnp
from jax import lax
from jax.experimental import pallas as pl
from jax.experimental.pallas import tpu as pltpu

F32 = jnp.float32
BF16 = jnp.bfloat16

D_MODEL = 1024
ATTN_W = 512
HGRN_W = 512
HEAD_PAIR = 128
ATTN_BLK = 128
DILATIONS = (1, 4, 16)
ATTN_CHAINS = 4
HGRN_HEADS = 4
HGRN_DIM = 128
HGRN_CHUNK = 64
SUPER = 256
D_FF = 2816
N_CHIPS = 4
IN_TOTAL = 3584
IN_SHARD = IN_TOTAL // N_CHIPS
UP_SHARD = 2 * D_FF // N_CHIPS
QKV_W = 3 * ATTN_W
HG_W = 4 * HGRN_W
EPS = 1e-6
NEG = -1e30
V7X_VMEM_BYTES = 64 * 1024 * 1024
VMEM_LIMIT = V7X_VMEM_BYTES - 8 * 1024 * 1024

ADAM_LR = 0.001
ADAM_B1 = 0.9
ADAM_B2 = 0.999
ADAM_EPS = 1e-08
ADAM_WD = 0.01
ADAM_STEP = 10

MESH = pl.DeviceIdType.MESH


def _cp(*sem):
    return pltpu.CompilerParams(dimension_semantics=sem or None, vmem_limit_bytes=VMEM_LIMIT)


def _dot(a, b):
    return jnp.dot(a, b, preferred_element_type=F32)


def _dot_nt(a, b):
    return lax.dot_general(a, b, (((1,), (1,)), ((), ())), preferred_element_type=F32)


def _dot_tn(a, b):
    return lax.dot_general(a, b, (((0,), (0,)), ((), ())), preferred_element_type=F32)


def _sigmoid(x):
    return 1.0 / (1.0 + jnp.exp(-x))


def _rms(x, width):
    return lax.rsqrt(jnp.sum(x * x, axis=-1, keepdims=True) * (1.0 / width) + EPS)


def _rms_bwd(dn, n, r, width):
    return r * (dn - n * (jnp.sum(dn * n, axis=-1, keepdims=True) * (1.0 / width)))


def _colsum(x):
    return jnp.sum(x, axis=0, keepdims=True)


def _row(v, k):
    rid = lax.broadcasted_iota(jnp.int32, v.shape, 0)
    return jnp.sum(jnp.where(rid == k, v, 0.0), axis=0, keepdims=True)


def _full(shape):
    return pl.BlockSpec(shape, lambda *_: (0,) * len(shape))


def _once(shape):
    return pl.BlockSpec(shape, lambda *_: (0,) * len(shape), pipeline_mode=pl.Buffered(1))


def _in_proj(x, g1, w_in4, tm=256):
    T = x.shape[0]

    def body(x_ref, g_ref, w_ref, u_ref, qkv_ref, hg_ref):
        xv = x_ref[...]
        u = (xv * _rms(xv, D_MODEL) * g_ref[...]).astype(BF16)
        u_ref[...] = u
        p0 = _dot(u, w_ref[0])
        p1 = _dot(u, w_ref[1])
        qkv_ref[:, 0:IN_SHARD] = p0
        qkv_ref[:, IN_SHARD:QKV_W] = p1[:, :QKV_W - IN_SHARD]
        hg_ref[:, 0:2 * IN_SHARD - QKV_W] = p1[:, QKV_W - IN_SHARD:]
        hg_ref[:, 2 * IN_SHARD - QKV_W:3 * IN_SHARD - QKV_W] = _dot(u, w_ref[2])
        hg_ref[:, 3 * IN_SHARD - QKV_W:HG_W] = _dot(u, w_ref[3])

    return pl.pallas_call(
        body, name="in_proj", grid=(T // tm,),
        in_specs=[pl.BlockSpec((tm, D_MODEL), lambda i: (i, 0)), _full((1, D_MODEL)),
                  _once((N_CHIPS, D_MODEL, IN_SHARD))],
        out_specs=[pl.BlockSpec((tm, D_MODEL), lambda i: (i, 0)), pl.BlockSpec((tm, QKV_W), lambda i: (i, 0)),
                   pl.BlockSpec((tm, HG_W), lambda i: (i, 0))],
        out_shape=[jax.ShapeDtypeStruct((T, D_MODEL), BF16), jax.ShapeDtypeStruct((T, QKV_W), F32),
                   jax.ShapeDtypeStruct((T, HG_W), F32)],
        compiler_params=_cp("arbitrary"),
    )(x, g1, w_in4)


def _attn_masks(bias_ref):
    lane = lax.broadcasted_iota(jnp.int32, (ATTN_BLK, HEAD_PAIR), 1)
    row = lax.broadcasted_iota(jnp.int32, (2 * ATTN_BLK, 2 * ATTN_BLK), 0)
    col = lax.broadcasted_iota(jnp.int32, (2 * ATTN_BLK, 2 * ATTN_BLK), 1)
    base = jnp.where(row >= ATTN_BLK, row - ATTN_BLK, row) - col
    for k in range(2):
        dist = base + k * ATTN_BLK
        bias_ref[k] = jnp.where((dist >= 0) & (dist <= ATTN_BLK), 0.0, NEG)
    return lane < 64


def _two_heads(blk, first):
    zero = jnp.zeros_like(blk)
    return jnp.concatenate([jnp.where(first, blk, zero), jnp.where(first, zero, blk)], axis=0)


def _attn_rows(idx, nb, d):
    r, n = idx // nb, idx % nb
    kb = jnp.maximum(n - 1, 0)
    if d == 1:
        q0 = pl.multiple_of(n * ATTN_BLK, ATTN_BLK)
        k0 = pl.multiple_of(kb * ATTN_BLK, ATTN_BLK)
        return pl.ds(q0, ATTN_BLK), pl.ds(k0, 2 * ATTN_BLK), n - kb
    return (pl.ds(r + d * ATTN_BLK * n, ATTN_BLK, stride=d), pl.ds(r + d * ATTN_BLK * kb, 2 * ATTN_BLK, stride=d),
            n - kb)


def _attn_fwd(qkv):
    T = qkv.shape[0]

    per_chain = T // ATTN_BLK // ATTN_CHAINS

    def body(q_ref, k_ref, v_ref, o_ref, m_ref, l_ref, bias_ref):
        first = _attn_masks(bias_ref)
        for bi, d in enumerate(DILATIONS):
            nb = T // d // ATTN_BLK

            def block(idx, d=d, nb=nb, bi=bi):
                rows, keys, which = _attn_rows(idx, nb, d)
                q2 = _two_heads(q_ref[rows, :] * 0.125, first).astype(BF16)
                kw = k_ref[keys, :].astype(BF16)
                vw = v_ref[keys, :].astype(BF16)
                old = (o_ref[rows, :], m_ref[rows, :], l_ref[rows, :]) if bi else None
                s = _dot_nt(q2, kw) + bias_ref[which]
                mb = jnp.max(s, axis=-1, keepdims=True)
                p = jnp.exp(s - mb)
                lb = jnp.sum(p, axis=-1, keepdims=True)
                o2 = _dot(p.astype(BF16), vw)
                o = jnp.where(first, o2[:ATTN_BLK], o2[ATTN_BLK:])
                m = jnp.where(first, mb[:ATTN_BLK], mb[ATTN_BLK:])
                l = jnp.where(first, lb[:ATTN_BLK], lb[ATTN_BLK:])
                if bi:
                    po, pm, pl_ = old
                    mn = jnp.maximum(pm, m)
                    wa = jnp.exp(pm - mn)
                    wb = jnp.exp(m - mn)
                    o, l, m = po * wa + o * wb, pl_ * wa + l * wb, mn
                return rows, o, m, l

            def step(i, carry, block=block):
                done = [block(i + ch * per_chain) for ch in range(ATTN_CHAINS)]
                for rows, o, m, l in done:
                    o_ref[rows, :] = o
                    m_ref[rows, :] = m
                    l_ref[rows, :] = l
                return carry

            lax.fori_loop(0, per_chain, step, 0)

        def finish(i, carry):
            rows = pl.ds(pl.multiple_of(i * SUPER, SUPER), SUPER)
            l = l_ref[rows, :]
            o_ref[rows, :] = o_ref[rows, :] / l
            m_ref[rows, :] = m_ref[rows, :] + jnp.log(l)
            return carry

        lax.fori_loop(0, T // SUPER, finish, 0)

    col = lambda off: pl.BlockSpec((T, HEAD_PAIR), lambda j: (0, off + j))
    return pl.pallas_call(
        body, name="attn_fwd", grid=(4,),
        in_specs=[col(0), col(4), col(8)], out_specs=[col(0), col(0)],
        out_shape=[jax.ShapeDtypeStruct((T, ATTN_W), F32)] * 2,
        scratch_shapes=[pltpu.VMEM((T, HEAD_PAIR), F32), pltpu.VMEM((2, 2 * ATTN_BLK, 2 * ATTN_BLK), F32)],
        compiler_params=_cp("arbitrary"),
    )(qkv, qkv, qkv)


def _attn_bwd(qkv, o, lse, do):
    T = qkv.shape[0]

    per_chain = T // ATTN_BLK // ATTN_CHAINS

    def body(q_ref, k_ref, v_ref, o_ref, lse_ref, do_ref, dq_ref, dk_ref, dv_ref, bias_ref):
        first = _attn_masks(bias_ref)
        dq_ref[...] = jnp.zeros_like(dq_ref)
        dk_ref[...] = jnp.zeros_like(dk_ref)
        dv_ref[...] = jnp.zeros_like(dv_ref)
        for d in DILATIONS:
            nb = T // d // ATTN_BLK

            def block(idx, d=d, nb=nb):
                rows, keys, which = _attn_rows(idx, nb, d)
                q2 = _two_heads(q_ref[rows, :] * 0.125, first).astype(BF16)
                kw = k_ref[keys, :].astype(BF16)
                vw = v_ref[keys, :].astype(BF16)
                lse_b = lse_ref[rows, :]
                dob = do_ref[rows, :]
                prod = dob * o_ref[rows, :]
                old = dq_ref[rows, :], dk_ref[keys, :], dv_ref[keys, :]
                lse2 = jnp.concatenate(
                    [jnp.max(jnp.where(first, lse_b, NEG), axis=-1, keepdims=True),
                     jnp.max(jnp.where(first, NEG, lse_b), axis=-1, keepdims=True)], axis=0)
                p = jnp.exp(_dot_nt(q2, kw) + (bias_ref[which] - lse2))
                delta = jnp.concatenate(
                    [jnp.sum(jnp.where(first, prod, 0.0), axis=-1, keepdims=True),
                     jnp.sum(jnp.where(first, 0.0, prod), axis=-1, keepdims=True)], axis=0)
                do2 = _two_heads(dob, first).astype(BF16)
                ds = (p * (_dot_nt(do2, vw) - delta)).astype(BF16)
                dq2 = _dot(ds, kw) * 0.125
                return (rows, keys, old[0] + jnp.where(first, dq2[:ATTN_BLK], dq2[ATTN_BLK:]),
                        old[1] + _dot_tn(ds, q2), old[2] + _dot_tn(p.astype(BF16), do2))

            def step(i, carry, block=block):
                done = [block(i + ch * per_chain) for ch in range(ATTN_CHAINS)]
                for rows, keys, dq, dk, dv in done:
                    dq_ref[rows, :] = dq
                    dk_ref[keys, :] = dk
                    dv_ref[keys, :] = dv
                return carry

            lax.fori_loop(0, per_chain, step, 0)

    col = lambda off: pl.BlockSpec((T, HEAD_PAIR), lambda j: (0, off + j))
    return pl.pallas_call(
        body, name="attn_bwd", grid=(4,),
        in_specs=[col(0), col(4), col(8), col(0), col(0), col(0)], out_specs=[col(0)] * 3,
        out_shape=[jax.ShapeDtypeStruct((T, ATTN_W), F32)] * 3,
        scratch_shapes=[pltpu.VMEM((2, 2 * ATTN_BLK, 2 * ATTN_BLK), F32)],
        compiler_params=_cp("arbitrary"),
    )(qkv, qkv, qkv, o, lse, do)


def _chunk_ids():
    row = lax.broadcasted_iota(jnp.int32, (SUPER, HGRN_DIM), 0)
    r2 = lax.broadcasted_iota(jnp.int32, (SUPER, SUPER), 0)
    c2 = lax.broadcasted_iota(jnp.int32, (SUPER, SUPER), 1)
    amask = ((r2 // HGRN_CHUNK) == (c2 // HGRN_CHUNK)) & (c2 <= r2)
    return row % HGRN_CHUNK, row // HGRN_CHUNK, amask


def _cumsum_chunk(x, rmod):
    s = 1
    while s < HGRN_CHUNK:
        x = x + jnp.where(rmod >= s, pltpu.roll(x, s, 0), 0.0)
        s *= 2
    return x


def _suffix_sum_chunk(x, rmod):
    s = 1
    while s < HGRN_CHUNK:
        x = x + jnp.where(rmod < HGRN_CHUNK - s, pltpu.roll(x, SUPER - s, 0), 0.0)
        s *= 2
    return x


def _chunk_rows(vs, cid):
    out = vs[-1]
    for c in reversed(range(len(vs) - 1)):
        out = jnp.where(cid == c, vs[c], out)
    return out


def _expand(x, cid):
    return jnp.concatenate([jnp.where(cid == c, x, 0.0) for c in range(SUPER // HGRN_CHUNK)], axis=1)


def _hgrn_gates(q, f, lbv, rmod, cid, tmp):
    sq = _sigmoid(q)
    sg = _sigmoid(f)
    forget = lbv + (1.0 - lbv) * sg
    key = 1.0 - forget
    b = _cumsum_chunk(jnp.log(forget), rmod)
    tmp[...] = b
    bends = [tmp[c * HGRN_CHUNK + HGRN_CHUNK - 1:(c + 1) * HGRN_CHUNK, :] for c in range(SUPER // HGRN_CHUNK)]
    eb = jnp.exp(b)
    enb = jnp.exp(-b)
    ebe = jnp.exp(_chunk_rows(bends, cid) - b)
    return sq, sg, forget, key, eb, enb, ebe, q * sq * eb, key * enb, key * ebe, [jnp.exp(v) for v in bends]


def _hgrn_fwd(hg, lb):
    T = hg.shape[0]
    nsc = T // SUPER
    NC = SUPER // HGRN_CHUNK

    def body(q_ref, f_ref, i_ref, lb_ref, o_ref, st_ref, state, tmp):
        rmod, cid, amask = _chunk_ids()
        state[...] = jnp.zeros_like(state)
        lbv = lb_ref[...]

        def step(sc, carry):
            rows = pl.ds(pl.multiple_of(sc * SUPER, SUPER), SUPER)
            iv = i_ref[rows, :].astype(BF16)
            qd, ki, ke, dec = _hgrn_gates(q_ref[rows, :], f_ref[rows, :], lbv, rmod, cid, tmp)[-4:]
            a = jnp.where(amask, _dot_nt(qd.astype(BF16), ki.astype(BF16)), 0.0)
            o = _dot(a.astype(BF16), iv)
            ut = _dot_tn(iv, _expand(ke, cid).astype(BF16))
            st = state[...]
            st_ref[0, sc] = st
            sts = []
            for c in range(NC):
                sts.append(st)
                st = st * dec[c] + ut[:, c * HGRN_DIM:(c + 1) * HGRN_DIM]
            state[...] = st
            o = o + _dot_nt(_expand(qd, cid).astype(BF16), jnp.concatenate(sts, axis=1).astype(BF16))
            o_ref[rows, :] = o
            return carry

        lax.fori_loop(0, nsc, step, 0)

    col = lambda off: pl.BlockSpec((T, HGRN_DIM), lambda h: (0, off + h))
    return pl.pallas_call(
        body, name="hgrn_fwd", grid=(HGRN_HEADS,),
        in_specs=[col(0), col(4), col(8), pl.BlockSpec((1, HGRN_DIM), lambda h: (0, h))],
        out_specs=[pl.BlockSpec((T, HGRN_DIM), lambda h: (0, h)),
                   pl.BlockSpec((1, nsc, HGRN_DIM, HGRN_DIM), lambda h: (h, 0, 0, 0))],
        out_shape=[jax.ShapeDtypeStruct((T, HGRN_W), F32),
                   jax.ShapeDtypeStruct((HGRN_HEADS, nsc, HGRN_DIM, HGRN_DIM), F32)],
        scratch_shapes=[pltpu.VMEM((HGRN_DIM, HGRN_DIM), F32), pltpu.VMEM((SUPER, HGRN_DIM), F32)],
        compiler_params=_cp("arbitrary"),
    )(hg, hg, hg, lb)


def _hgrn_bwd(hg, lb, states, do):
    T = hg.shape[0]
    nsc = T // SUPER
    NC = SUPER // HGRN_CHUNK

    def body(q_ref, f_ref, i_ref, lb_ref, st_ref, do_ref, dq_ref, df_ref, di_ref, dlb_ref, dstate, tmp):
        rmod, cid, amask = _chunk_ids()
        dstate[...] = jnp.zeros_like(dstate)
        dlb_ref[...] = jnp.zeros_like(dlb_ref)
        lbv = lb_ref[...]

        def step(k, carry):
            sc = nsc - 1 - k
            rows = pl.ds(pl.multiple_of(sc * SUPER, SUPER), SUPER)
            q = q_ref[rows, :]
            ivf = i_ref[rows, :]
            iv = ivf.astype(BF16)
            dof = do_ref[rows, :]
            dob = dof.astype(BF16)
            sq, sg, forget, key, eb, enb, ebe, qd, ki, ke, dec = _hgrn_gates(q, f_ref[rows, :], lbv, rmod, cid, tmp)
            qdb, kib = qd.astype(BF16), ki.astype(BF16)
            keexp = _expand(ke, cid).astype(BF16)
            a = jnp.where(amask, _dot_nt(qdb, kib), 0.0).astype(BF16)
            ut = _dot_tn(iv, keexp)
            st = st_ref[0, sc]
            sts = []
            for c in range(NC):
                sts.append(st)
                st = st * dec[c] + ut[:, c * HGRN_DIM:(c + 1) * HGRN_DIM]
            gt = _dot_tn(dob, _expand(qd, cid).astype(BF16))
            nxt = [None] * NC
            ddec = [None] * NC
            dst = dstate[...]
            for c in reversed(range(NC)):
                nxt[c] = dst
                ddec[c] = _colsum(dst * sts[c])
                dst = dst * dec[c] + gt[:, c * HGRN_DIM:(c + 1) * HGRN_DIM]
            dstate[...] = dst
            da = jnp.where(amask, _dot_nt(dob, iv), 0.0).astype(BF16)
            ncat = jnp.concatenate(nxt, axis=1).astype(BF16)
            nstack = jnp.concatenate(nxt, axis=0).astype(BF16)
            ststack = jnp.concatenate(sts, axis=0).astype(BF16)
            div = _dot_tn(a, dob) + _dot_nt(keexp, ncat)
            dke = _dot(_expand(ivf, cid).astype(BF16), nstack)
            dqd = _dot(da, kib) + _dot(_expand(dof, cid).astype(BF16), ststack)
            dki = _dot_tn(da, qdb)
            dkk = dke * ke
            dkey = dki * enb + dke * ebe
            db = dqd * qd - dki * ki - dkk
            dbends = [_colsum(jnp.where(cid == c, dkk, 0.0)) + ddec[c] * dec[c] for c in range(NC)]
            dlogf = _suffix_sum_chunk(db, rmod) + _chunk_rows(dbends, cid)
            dforget = dlogf / forget - dkey
            df_ref[rows, :] = dforget * (1.0 - lbv) * sg * (1.0 - sg)
            dlb_ref[...] += _colsum(dforget * (1.0 - sg))
            dq_ref[rows, :] = dqd * eb * (sq * (1.0 + q * (1.0 - sq)))
            di_ref[rows, :] = div
            return carry

        lax.fori_loop(0, nsc, step, 0)

    col = lambda off: pl.BlockSpec((T, HGRN_DIM), lambda h: (0, off + h))
    own = pl.BlockSpec((T, HGRN_DIM), lambda h: (0, h))
    vec = pl.BlockSpec((1, HGRN_DIM), lambda h: (0, h))
    return pl.pallas_call(
        body, name="hgrn_bwd", grid=(HGRN_HEADS,),
        in_specs=[col(0), col(4), col(8), vec,
                  pl.BlockSpec((1, nsc, HGRN_DIM, HGRN_DIM), lambda h: (h, 0, 0, 0)), own],
        out_specs=[own, own, own, vec],
        out_shape=[jax.ShapeDtypeStruct((T, HGRN_W), F32)] * 3 + [jax.ShapeDtypeStruct((1, HGRN_W), F32)],
        scratch_shapes=[pltpu.VMEM((HGRN_DIM, HGRN_DIM), F32), pltpu.VMEM((SUPER, HGRN_DIM), F32)],
        compiler_params=_cp("arbitrary"),
    )(hg, hg, hg, lb, states, do)


def _rec_heads(rec, gate, g_h):
    rr = jnp.concatenate(
        [jnp.broadcast_to(_rms(rec[:, h * HGRN_DIM:(h + 1) * HGRN_DIM], HGRN_DIM), (rec.shape[0], HGRN_DIM))
         for h in range(HGRN_HEADS)], axis=1)
    rn = rec * rr
    sg = _sigmoid(gate)
    return rr, rn, sg


def _mix_out(attn_o, rec_o, hg, x, g_a, g_h, w_out, tm=256):
    T = x.shape[0]

    def body(a_ref, r_ref, gt_ref, x_ref, ga_ref, gh_ref, w_ref, h1_ref, mixed_ref):
        a = a_ref[...]
        an = a * _rms(a, ATTN_W) * ga_ref[...]
        gate = gt_ref[...]
        _, rn, sg = _rec_heads(r_ref[...], gate, gh_ref[...])
        mixed = jnp.concatenate([an, rn * gh_ref[...] * (gate * sg)], axis=1).astype(BF16)
        mixed_ref[...] = mixed
        h1_ref[...] = x_ref[...] + _dot(mixed, w_ref[...])

    row = lambda w: pl.BlockSpec((tm, w), lambda i: (i, 0))
    return pl.pallas_call(
        body, name="mix_out", grid=(T // tm,),
        in_specs=[row(ATTN_W), row(HGRN_W), pl.BlockSpec((tm, HGRN_W), lambda i: (i, 3)), row(D_MODEL),
                  _full((1, ATTN_W)), _full((1, HGRN_W)), _once((D_MODEL, D_MODEL))],
        out_specs=[row(D_MODEL), row(D_MODEL)],
        out_shape=[jax.ShapeDtypeStruct((T, D_MODEL), F32), jax.ShapeDtypeStruct((T, D_MODEL), BF16)],
        compiler_params=_cp("arbitrary"),
    )(attn_o, rec_o, hg, x, g_a, g_h, w_out)


_INV_SQRT2 = 1.0 / math.sqrt(2.0)
_INV_SQRT2PI = 1.0 / math.sqrt(2.0 * math.pi)


def _gelu(x):
    return 0.5 * x * (1.0 + lax.erf(x * _INV_SQRT2))


def _gelu_grad(x):
    return 0.5 * (1.0 + lax.erf(x * _INV_SQRT2)) + x * jnp.exp(-0.5 * x * x) * _INV_SQRT2PI


def _shift_down(g, prev, rowid):
    p1 = _row(prev, prev.shape[0] - 1)
    p2 = _row(prev, prev.shape[0] - 2)
    s1 = jnp.where(rowid == 0, p1, pltpu.roll(g, 1, 0))
    s2 = jnp.where(rowid == 0, p2, jnp.where(rowid == 1, p1, pltpu.roll(g, 2, 0)))
    return s1, s2


def _mlp_fwd(h1, g2, w_up4, conv_w, conv_b, w_down, gf, tgt, tm=256):
    T = h1.shape[0]
    half = D_FF // 2

    def body(h_ref, g2_ref, wu_ref, cw_ref, cb_ref, wd_ref, gf_ref, t_ref,
             u_ref, gate_ref, val_ref, dh_ref, loss_ref, dgf_ref, carry):
        i = pl.program_id(0)

        @pl.when(i == 0)
        def _():
            carry[...] = jnp.zeros_like(carry)
            loss_ref[...] = jnp.zeros_like(loss_ref)
            dgf_ref[...] = jnp.zeros_like(dgf_ref)

        h = h_ref[...]
        u = (h * _rms(h, D_MODEL) * g2_ref[...]).astype(BF16)
        u_ref[...] = u
        rowid = lax.broadcasted_iota(jnp.int32, (tm, half), 0)
        y2 = jnp.zeros((tm, D_MODEL), F32)
        for c in range(2):
            cols = slice(c * half, (c + 1) * half)
            gb = _dot(u, wu_ref[c]).astype(BF16)
            vb = _dot(u, wu_ref[2 + c]).astype(BF16)
            gate_ref[:, cols] = gb
            val_ref[:, cols] = vb
            g = gb.astype(F32)
            s1, s2 = _shift_down(g, carry[:, cols], rowid)
            carry[:, cols] = g[tm - 8:, :]
            conv = cb_ref[:, cols] + cw_ref[0:1, cols] * s2 + cw_ref[1:2, cols] * s1 + cw_ref[2:3, cols] * g
            act = (_gelu(conv) * vb.astype(F32)).astype(BF16)
            y2 = y2 + _dot(act, wd_ref[cols, :])
        h2 = h + y2
        rf = _rms(h2, D_MODEL)
        n = h2 * rf
        gfv = gf_ref[...]
        e = n * gfv - t_ref[...]
        loss_ref[...] += jnp.sum(e * e) * (0.5 / D_MODEL)
        dy = e * (1.0 / D_MODEL)
        dgf_ref[...] += _colsum(dy * n)
        dh_ref[...] = _rms_bwd(dy * gfv, n, rf, D_MODEL)

    row = lambda w: pl.BlockSpec((tm, w), lambda i: (i, 0))
    return pl.pallas_call(
        body, name="mlp_fwd", grid=(T // tm,),
        in_specs=[row(D_MODEL), _full((1, D_MODEL)), _once((N_CHIPS, D_MODEL, UP_SHARD)), _full((3, D_FF)),
                  _full((1, D_FF)), _once((D_FF, D_MODEL)), _full((1, D_MODEL)), row(D_MODEL)],
        out_specs=[row(D_MODEL), row(D_FF), row(D_FF), row(D_MODEL), _full((1, 128)), _full((1, D_MODEL))],
        out_shape=[jax.ShapeDtypeStruct((T, D_MODEL), BF16), jax.ShapeDtypeStruct((T, D_FF), BF16),
                   jax.ShapeDtypeStruct((T, D_FF), BF16), jax.ShapeDtypeStruct((T, D_MODEL), F32),
                   jax.ShapeDtypeStruct((1, 128), F32), jax.ShapeDtypeStruct((1, D_MODEL), F32)],
        scratch_shapes=[pltpu.VMEM((8, D_FF), F32)],
        compiler_params=_cp("arbitrary"),
    )(h1, g2, w_up4, conv_w, conv_b, w_down, gf, tgt)


def _mlp_bwd(dh2, gate, val, conv_w, conv_b, w_down, tm=256):
    T = dh2.shape[0]
    nb = T // tm
    half = D_FF // 2

    def body(dh_ref, gate_ref, halo_ref, val_ref, cw_ref, cb_ref, wd_ref,
             dgv_ref, act_ref, dcw_ref, dcb_ref, carry):
        i = pl.program_id(0)

        @pl.when(i == 0)
        def _():
            carry[...] = jnp.zeros_like(carry)
            dcw_ref[...] = jnp.zeros_like(dcw_ref)
            dcb_ref[...] = jnp.zeros_like(dcb_ref)

        dhb = dh_ref[...].astype(BF16)
        rowid = lax.broadcasted_iota(jnp.int32, (tm, half), 0)
        has_prev = (i < nb - 1).astype(F32)
        for c in range(2):
            cols = slice(c * half, (c + 1) * half)
            g = gate_ref[:, cols].astype(F32)
            v = val_ref[:, cols].astype(F32)
            s1, s2 = _shift_down(g, halo_ref[:, cols].astype(F32) * has_prev, rowid)
            conv = cb_ref[:, cols] + cw_ref[0:1, cols] * s2 + cw_ref[1:2, cols] * s1 + cw_ref[2:3, cols] * g
            gl = _gelu(conv)
            act_ref[:, cols] = (gl * v).astype(BF16)
            dact = _dot_nt(dhb, wd_ref[cols, :])
            dconv = dact * v * _gelu_grad(conv)
            dcb_ref[:, cols] += _colsum(dconv)
            dcw_ref[0:1, cols] += _colsum(dconv * s2)
            dcw_ref[1:2, cols] += _colsum(dconv * s1)
            dcw_ref[2:3, cols] += _colsum(dconv * g)
            nxt = carry[:, cols]
            n0, n1 = _row(nxt, 0), _row(nxt, 1)
            u1 = jnp.where(rowid == tm - 1, n0, pltpu.roll(dconv, tm - 1, 0))
            u2 = jnp.where(rowid == tm - 1, n1, jnp.where(rowid == tm - 2, n0, pltpu.roll(dconv, tm - 2, 0)))
            carry[:, cols] = dconv[0:8, :]
            dgate = cw_ref[2:3, cols] * dconv + cw_ref[1:2, cols] * u1 + cw_ref[0:1, cols] * u2
            dgv_ref[:, cols] = dgate.astype(BF16)
            dgv_ref[:, D_FF + c * half:D_FF + (c + 1) * half] = (dact * gl).astype(BF16)

    rev = lambda w: pl.BlockSpec((tm, w), lambda i: (nb - 1 - i, 0))
    halo = pl.BlockSpec((16, D_FF), lambda i: (jnp.maximum((nb - 1 - i) * (tm // 16) - 1, 0), 0))
    return pl.pallas_call(
        body, name="mlp_bwd", grid=(nb,),
        in_specs=[rev(D_MODEL), rev(D_FF), halo, rev(D_FF), _full((3, D_FF)), _full((1, D_FF)),
                  _once((D_FF, D_MODEL))],
        out_specs=[rev(2 * D_FF), rev(D_FF), _full((3, D_FF)), _full((1, D_FF))],
        out_shape=[jax.ShapeDtypeStruct((T, 2 * D_FF), BF16), jax.ShapeDtypeStruct((T, D_FF), BF16),
                   jax.ShapeDtypeStruct((3, D_FF), F32), jax.ShapeDtypeStruct((1, D_FF), F32)],
        scratch_shapes=[pltpu.VMEM((8, D_FF), F32)],
        compiler_params=_cp("arbitrary"),
    )(dh2, gate, gate, val, conv_w, conv_b, w_down)


def _up_out_bwd(dgv, w_up4, h1, g2, dh2, w_out, attn_o, rec_o, hg, g_a, g_h, tm=256):
    T = h1.shape[0]

    def body(dgv_ref, wu_ref, h_ref, g2_ref, dh2_ref, wo_ref, a_ref, r_ref, gt_ref, ga_ref, gh_ref,
             dh1_ref, dg2_ref, da_ref, dr_ref, dgt_ref, dga_ref, dgh_ref):
        @pl.when(pl.program_id(0) == 0)
        def _():
            dg2_ref[...] = jnp.zeros_like(dg2_ref)
            dga_ref[...] = jnp.zeros_like(dga_ref)
            dgh_ref[...] = jnp.zeros_like(dgh_ref)

        du = jnp.zeros((tm, D_MODEL), F32)
        for k in range(N_CHIPS):
            du = du + _dot_nt(dgv_ref[:, k * UP_SHARD:(k + 1) * UP_SHARD], wu_ref[k])
        h = h_ref[...]
        r = _rms(h, D_MODEL)
        n = h * r
        dg2_ref[...] += _colsum(du * n)
        dh1 = dh2_ref[...] + _rms_bwd(du * g2_ref[...], n, r, D_MODEL)
        dh1_ref[...] = dh1
        dmix = _dot_nt(dh1.astype(BF16), wo_ref[...])
        dan = dmix[:, :ATTN_W]
        a = a_ref[...]
        ra = _rms(a, ATTN_W)
        na = a * ra
        dga_ref[...] += _colsum(dan * na)
        da_ref[...] = _rms_bwd(dan * ga_ref[...], na, ra, ATTN_W)
        dmr = dmix[:, ATTN_W:]
        gate = gt_ref[...]
        ghv = gh_ref[...]
        rr, rn, sg = _rec_heads(r_ref[...], gate, ghv)
        dgt_ref[...] = dmr * rn * ghv * (sg * (1.0 + gate * (1.0 - sg)))
        drecn = dmr * (gate * sg)
        dgh_ref[...] += _colsum(drecn * rn)
        drn = drecn * ghv
        prod = drn * rn
        mean = jnp.concatenate(
            [jnp.broadcast_to(jnp.sum(prod[:, h_ * HGRN_DIM:(h_ + 1) * HGRN_DIM], axis=-1, keepdims=True),
                              (tm, HGRN_DIM)) for h_ in range(HGRN_HEADS)], axis=1) * (1.0 / HGRN_DIM)
        dr_ref[...] = rr * (drn - rn * mean)

    row = lambda w: pl.BlockSpec((tm, w), lambda i: (i, 0))
    return pl.pallas_call(
        body, name="up_out_bwd", grid=(T // tm,),
        in_specs=[row(2 * D_FF), _once((N_CHIPS, D_MODEL, UP_SHARD)), row(D_MODEL), _full((1, D_MODEL)),
                  row(D_MODEL), _once((D_MODEL, D_MODEL)), row(ATTN_W), row(HGRN_W),
                  pl.BlockSpec((tm, HGRN_W), lambda i: (i, 3)), _full((1, ATTN_W)), _full((1, HGRN_W))],
        out_specs=[row(D_MODEL), _full((1, D_MODEL)), row(ATTN_W), row(HGRN_W), row(HGRN_W),
                   _full((1, ATTN_W)), _full((1, HGRN_W))],
        out_shape=[jax.ShapeDtypeStruct((T, D_MODEL), F32), jax.ShapeDtypeStruct((1, D_MODEL), F32),
                   jax.ShapeDtypeStruct((T, ATTN_W), F32), jax.ShapeDtypeStruct((T, HGRN_W), F32),
                   jax.ShapeDtypeStruct((T, HGRN_W), F32), jax.ShapeDtypeStruct((1, ATTN_W), F32),
                   jax.ShapeDtypeStruct((1, HGRN_W), F32)],
        compiler_params=_cp("arbitrary"),
    )(dgv, w_up4, h1, g2, dh2, w_out, attn_o, rec_o, hg, g_a, g_h)


def _in_bwd(dqkv, dhg, w_in4, x, g1, dh1, tm=256):
    T = x.shape[0]

    def body(*refs):
        parts = refs[:7]
        w_ref, x_ref, g_ref, dh1_ref, dp_ref, dx_ref, dg_ref = refs[7:]

        @pl.when(pl.program_id(0) == 0)
        def _():
            dg_ref[...] = jnp.zeros_like(dg_ref)

        dp = jnp.concatenate([p[...] for p in parts], axis=1).astype(BF16)
        dp_ref[...] = dp
        du = jnp.zeros((tm, D_MODEL), F32)
        for k in range(N_CHIPS):
            du = du + _dot_nt(dp[:, k * IN_SHARD:(k + 1) * IN_SHARD], w_ref[k])
        xv = x_ref[...]
        r = _rms(xv, D_MODEL)
        n = xv * r
        dg_ref[...] += _colsum(du * n)
        dx_ref[...] = dh1_ref[...] + _rms_bwd(du * g_ref[...], n, r, D_MODEL)

    row = lambda w: pl.BlockSpec((tm, w), lambda i: (i, 0))
    return pl.pallas_call(
        body, name="in_bwd", grid=(T // tm,),
        in_specs=[row(ATTN_W)] * 7 + [_once((N_CHIPS, D_MODEL, IN_SHARD)), row(D_MODEL), _full((1, D_MODEL)),
                                       row(D_MODEL)],
        out_specs=[row(IN_TOTAL), row(D_MODEL), _full((1, D_MODEL))],
        out_shape=[jax.ShapeDtypeStruct((T, IN_TOTAL), BF16), jax.ShapeDtypeStruct((T, D_MODEL), F32),
                   jax.ShapeDtypeStruct((1, D_MODEL), F32)],
        compiler_params=_cp("arbitrary"),
    )(*dqkv, *dhg, w_in4, x, g1, dh1)


def _dw(a, b, kb, nb_, name, tk=512):
    T, K = a.shape
    N = b.shape[1]
    nk, nn, nt = K // kb, N // nb_, T // tk

    def body(a_ref, b_ref, o_ref, acc):
        t = pl.program_id(2)

        @pl.when(t == 0)
        def _():
            acc[...] = jnp.zeros_like(acc)

        acc[...] += _dot_tn(a_ref[...], b_ref[...].astype(BF16))

        @pl.when(t == nt - 1)
        def _():
            o_ref[0] = acc[...].astype(BF16)

    return pl.pallas_call(
        body, name=name, grid=(nk, nn, nt),
        in_specs=[pl.BlockSpec((tk, kb), lambda i, j, t: (t, i)), pl.BlockSpec((tk, nb_), lambda i, j, t: (t, j))],
        out_specs=pl.BlockSpec((1, kb, nb_), lambda i, j, t: (i * nn + j, 0, 0)),
        out_shape=jax.ShapeDtypeStruct((nk * nn, kb, nb_), BF16),
        scratch_shapes=[pltpu.VMEM((kb, nb_), F32)],
        compiler_params=_cp("arbitrary", "arbitrary", "arbitrary"),
    )(a, b)


def _local_step(x, tgt, g1, w_in4, g_a, g_h, lb, w_out, g2, w_up4, conv_w, conv_b, w_down, gf):
    u1, qkv, hg = _in_proj(x, g1, w_in4)
    attn_o, lse = _attn_fwd(qkv)
    rec_o, states = _hgrn_fwd(hg, lb)
    h1, mixed = _mix_out(attn_o, rec_o, hg, x, g_a, g_h, w_out)
    u2, gate, val, dh2, loss, dgf = _mlp_fwd(h1, g2, w_up4, conv_w, conv_b, w_down, gf, tgt)

    dgv, act, dcw, dcb = _mlp_bwd(dh2, gate, val, conv_w, conv_b, w_down)
    dw_down = _dw(act, dh2, D_FF // 2, D_MODEL, "dw_down").reshape(N_CHIPS, D_FF // N_CHIPS, D_MODEL)
    dh1, dg2, da, dr, dgt, dga, dgh = _up_out_bwd(dgv, w_up4, h1, g2, dh2, w_out, attn_o, rec_o, hg, g_a, g_h)
    dw_up = _dw(u2, dgv, D_MODEL, UP_SHARD, "dw_up")
    dw_out = _dw(mixed, dh1, D_MODEL, D_MODEL, "dw_out").reshape(N_CHIPS, D_MODEL // N_CHIPS, D_MODEL)
    dqkv = _attn_bwd(qkv, attn_o, lse, da)
    dhq, dhf, dhi, dlb = _hgrn_bwd(hg, lb, states, dr)
    dproj, dx, dg1 = _in_bwd(dqkv, [dhq, dhf, dhi, dgt], w_in4, x, g1, dh1)
    dw_in = _dw(u1, dproj, D_MODEL, IN_SHARD, "dw_in")
    return loss, dx, dict(g1=dg1, g_a=dga, g_h=dgh, lb=dlb, g2=dg2, conv_w=dcw, conv_b=dcb, gf=dgf), \
        dict(w_in=dw_in, w_out=dw_out, w_up=dw_up, w_down=dw_down)


BIG = ("w_in", "w_out", "w_up", "w_down")
ANY = pl.BlockSpec(memory_space=pl.ANY)


def _place():
    x, y, c = lax.axis_index("x"), lax.axis_index("y"), lax.axis_index("c")
    chips = [(1 - x, y), (x, 1 - y), (1 - x, 1 - y)]
    return x, y, c, chips


def _remote(src, dst, send_sems, recv_sems, k, to):
    return pltpu.make_async_remote_copy(src_ref=src, dst_ref=dst, send_sem=send_sems.at[k], recv_sem=recv_sems.at[k],
                                        device_id=to, device_id_type=MESH)


def _gather_weights(shards, conv_w):
    n = len(shards)
    halves = [s.shape[0] // 2 for s in shards]

    def body(*refs):
        ins, cw, outs, ocw = refs[:n], refs[n], refs[n + 1:2 * n + 1], refs[2 * n + 1]
        send_sems, recv_sems = refs[2 * n + 2:]
        x, y, c, chips = _place()
        me, sibling = 2 * x + y, (x, y, 1 - c)

        def part(w, chip, half):
            return outs[w].at[chip, pl.ds(half * halves[w], halves[w]), :]

        sent = []
        for j, chip in enumerate(chips):
            for w in range(n):
                sent.append(_remote(ins[w].at[pl.ds(c * halves[w], halves[w]), :], part(w, me, c),
                                    send_sems, recv_sems, w * 3 + j, (*chip, c)))
            sent.append(_remote(cw, ocw.at[me], send_sems, recv_sems, 6 * n + j, (*chip, c)))
        for cp in sent:
            cp.start()
        for j, chip in enumerate(chips):
            kj = 2 * chip[0] + chip[1]
            for w in range(n):
                _remote(part(w, kj, c), part(w, kj, c), send_sems, recv_sems, w * 3 + j, (*chip, c)).wait_recv()
                fwd = _remote(part(w, kj, c), part(w, kj, c), send_sems, recv_sems, 3 * n + w * 3 + j, sibling)
                fwd.start()
                sent.append(fwd)
        for j, chip in enumerate(chips):
            kj = 2 * chip[0] + chip[1]
            for w in range(n):
                _remote(part(w, kj, 1 - c), part(w, kj, 1 - c), send_sems, recv_sems, 3 * n + w * 3 + j,
                        sibling).wait_recv()
            _remote(cw, ocw.at[kj], send_sems, recv_sems, 6 * n + j, (*chip, c)).wait_recv()
        for cp in sent:
            cp.wait_send()

    n_sem = 6 * n + 3
    outs = pl.pallas_call(
        body, name="gather_weights",
        in_specs=[ANY] * (n + 1), out_specs=[ANY] * (n + 1),
        out_shape=[jax.ShapeDtypeStruct((N_CHIPS,) + s.shape, s.dtype) for s in shards]
        + [jax.ShapeDtypeStruct((N_CHIPS,) + conv_w.shape, conv_w.dtype)],
        scratch_shapes=[pltpu.SemaphoreType.DMA((n_sem,)), pltpu.SemaphoreType.DMA((n_sem,))],
    )(*shards, conv_w)
    chip = 2 * lax.axis_index("x") + lax.axis_index("y")
    return [lax.dynamic_update_slice(o, s[None], (chip,) + (0,) * s.ndim) for o, s in zip(outs, [*shards, conv_w])]


def _allreduce_small(buf):
    rows = buf.shape[0]

    def body(in_ref, out_ref, slots, send_sems, recv_sems):
        x, y, c, _ = _place()
        me = 4 * x + 2 * y + c
        slots[me] = in_ref[...]
        sent = []
        for p in range(1, 8):
            to = (x ^ (p >> 2), y ^ ((p >> 1) & 1), c ^ (p & 1))
            sent.append(_remote(in_ref, slots.at[me], send_sems, recv_sems, p, to))
        for cp in sent:
            cp.start()
        for p in range(1, 8):
            frm = 4 * (x ^ (p >> 2)) + 2 * (y ^ ((p >> 1) & 1)) + (c ^ (p & 1))
            _remote(in_ref, slots.at[frm], send_sems, recv_sems, p, (x, y, c)).wait_recv()
        for cp in sent:
            cp.wait_send()
        acc = slots[0]
        for d in range(1, 8):
            acc = acc + slots[d]
        out_ref[...] = acc

    vm = pl.BlockSpec(memory_space=pltpu.VMEM)
    return pl.pallas_call(
        body, name="allreduce_small", in_specs=[vm], out_specs=vm,
        out_shape=jax.ShapeDtypeStruct(buf.shape, F32),
        scratch_shapes=[pltpu.VMEM((8, rows, 128), F32), pltpu.SemaphoreType.DMA((8,)), pltpu.SemaphoreType.DMA((8,))],
    )(buf)


def _pair_exchange(gs):
    n = len(gs)
    halves = [g.shape[1] // 2 for g in gs]

    def body(*refs):
        g, got = refs[:n], refs[n:2 * n]
        send_sems, recv_sems = refs[2 * n:]
        x, y, c, _ = _place()
        cps = [_remote(g[w].at[:, pl.ds((1 - c) * halves[w], halves[w]), :], got[w], send_sems, recv_sems, w,
                       (x, y, 1 - c)) for w in range(n)]
        for cp in cps:
            cp.start()
        for cp in cps:
            cp.wait()

    return pl.pallas_call(
        body, name="pair_exchange", in_specs=[ANY] * n, out_specs=[ANY] * n,
        out_shape=[jax.ShapeDtypeStruct((N_CHIPS, h, g.shape[2]), g.dtype) for g, h in zip(gs, halves)],
        scratch_shapes=[pltpu.SemaphoreType.DMA((n,)), pltpu.SemaphoreType.DMA((n,))],
    )(*gs)


def _core_id():
    return lax.axis_index("c").reshape(1).astype(jnp.int32)


def _pair_sum(g, got, name):
    h, C = got.shape[1:]

    def body(c_ref, g_ref, b_ref, o_ref):
        o_ref[...] = (g_ref[...].astype(F32) + b_ref[...].astype(F32)).astype(BF16)

    blk = pl.BlockSpec((1, h, C), lambda k, c_ref: (k, 0, 0))
    return pl.pallas_call(
        body, name=name,
        grid_spec=pltpu.PrefetchScalarGridSpec(
            num_scalar_prefetch=1, grid=(N_CHIPS,),
            in_specs=[pl.BlockSpec((1, h, C), lambda k, c_ref: (k, c_ref[0], 0)), blk], out_specs=blk),
        out_shape=jax.ShapeDtypeStruct(got.shape, BF16), compiler_params=_cp("arbitrary"))(_core_id(), g, got)


def _chip_exchange(ps):
    n = len(ps)

    def body(*refs):
        p, landed = refs[:n], refs[n:2 * n]
        send_sems, recv_sems = refs[2 * n:]
        x, y, c, chips = _place()
        cps = [_remote(p[w].at[2 * chip[0] + chip[1]], landed[w].at[j], send_sems, recv_sems, w * 3 + j, (*chip, c))
               for w in range(n) for j, chip in enumerate(chips)]
        for cp in cps:
            cp.start()
        for cp in cps:
            cp.wait()

    return pl.pallas_call(
        body, name="chip_exchange", in_specs=[ANY] * n, out_specs=[ANY] * n,
        out_shape=[jax.ShapeDtypeStruct((3,) + a.shape[1:], BF16) for a in ps],
        scratch_shapes=[pltpu.SemaphoreType.DMA((3 * n,)), pltpu.SemaphoreType.DMA((3 * n,))],
    )(*ps)


def _sum_partials(g, got, landed, name):
    h, C = got.shape[1:]

    def body(ids, g_ref, b_ref, l_ref, o_ref):
        acc = g_ref[0].astype(F32) + b_ref[0].astype(F32)
        for j in range(3):
            acc = acc + l_ref[j].astype(F32)
        o_ref[...] = acc

    ids = jnp.stack([2 * lax.axis_index("x") + lax.axis_index("y"), lax.axis_index("c")]).astype(jnp.int32)
    return pl.pallas_call(
        body, name=name,
        grid_spec=pltpu.PrefetchScalarGridSpec(
            num_scalar_prefetch=1, grid=(1,),
            in_specs=[pl.BlockSpec((1, h, C), lambda i, ids: (ids[0], ids[1], 0)),
                      pl.BlockSpec((1, h, C), lambda i, ids: (ids[0], 0, 0)),
                      pl.BlockSpec((3, h, C), lambda i, ids: (0, 0, 0))],
            out_specs=pl.BlockSpec((h, C), lambda i, ids: (ids[1], 0))),
        out_shape=jax.ShapeDtypeStruct((2 * h, C), F32), compiler_params=_cp("arbitrary"))(ids, g, got, landed)


def _pair_share(reds):
    n = len(reds)

    def body(*refs):
        out = refs[n:2 * n]
        send_sems, recv_sems = refs[2 * n:]
        x, y, c, _ = _place()
        def half(w, which):
            h = out[w].shape[0] // 2
            return out[w].at[pl.ds(which * h, h), :]

        cps = [_remote(half(w, c), half(w, c), send_sems, recv_sems, w, (x, y, 1 - c)) for w in range(n)]
        for cp in cps:
            cp.start()
        for w in range(n):
            _remote(half(w, 1 - c), half(w, 1 - c), send_sems, recv_sems, w, (x, y, 1 - c)).wait_recv()
        for cp in cps:
            cp.wait_send()

    return pl.pallas_call(
        body, name="pair_share", in_specs=[ANY] * n, out_specs=[ANY] * n,
        out_shape=[jax.ShapeDtypeStruct(r.shape, F32) for r in reds],
        input_output_aliases={w: w for w in range(n)},
        scratch_shapes=[pltpu.SemaphoreType.DMA((n,)), pltpu.SemaphoreType.DMA((n,))],
    )(*reds)


def _adamw(w, g, m, v, name, tr=64):
    R, C = w.shape
    tr = min(tr, R)

    def body(w_ref, g_ref, m_ref, v_ref, d_ref, nm_ref, nv_ref):
        gv = g_ref[...]
        nm = ADAM_B1 * m_ref[...] + (1.0 - ADAM_B1) * gv
        nv = ADAM_B2 * v_ref[...] + (1.0 - ADAM_B2) * (gv * gv)
        m_hat = nm / (1.0 - ADAM_B1 ** ADAM_STEP)
        v_hat = nv / (1.0 - ADAM_B2 ** ADAM_STEP)
        d_ref[...] = -ADAM_LR * (m_hat / (jnp.sqrt(v_hat) + ADAM_EPS) + ADAM_WD * w_ref[...])
        nm_ref[...] = nm
        nv_ref[...] = nv

    blk = pl.BlockSpec((tr, C), lambda i: (i, 0))
    return pl.pallas_call(body, name=name, grid=(R // tr,), in_specs=[blk] * 4, out_specs=[blk] * 3,
                          out_shape=[jax.ShapeDtypeStruct((R, C), F32)] * 3, compiler_params=_cp("arbitrary"))(w, g, m, v)


SMALL = (("norm1_g", 1024), ("attn_norm_g", 512), ("hgrn_norm_g", 512), ("hgrn_lb_logits", 1024), ("norm2_g", 1024),
         ("conv_b", D_FF), ("final_norm_g", 1024), ("conv_w", 3 * D_FF))
SMALL_ROWS = 136


def _pack(parts, rows):
    flat = jnp.concatenate([p.reshape(-1).astype(F32) for p in parts])
    return jnp.pad(flat, (0, rows * 128 - flat.shape[0])).reshape(rows, 128)


def kernel(x, norm1_g, w_in, attn_norm_g, hgrn_norm_g, hgrn_lb_logits, w_out, norm2_g, w_up, conv_w, conv_b, w_down, final_norm_g, loss_target, m_norm1_g, m_w_in, m_attn_norm_g, m_hgrn_norm_g, m_hgrn_lb_logits, m_w_out, m_norm2_g, m_w_up, m_conv_w, m_conv_b, m_w_down, m_final_norm_g, v_norm1_g, v_w_in, v_attn_norm_g, v_hgrn_norm_g, v_hgrn_lb_logits, v_w_out, v_norm2_g, v_w_up, v_conv_w, v_conv_b, v_w_down, v_final_norm_g):
    w = dict(norm1_g=norm1_g, w_in=w_in, attn_norm_g=attn_norm_g, hgrn_norm_g=hgrn_norm_g,
             hgrn_lb_logits=hgrn_lb_logits, w_out=w_out, norm2_g=norm2_g, w_up=w_up, conv_w=conv_w, conv_b=conv_b,
             w_down=w_down, final_norm_g=final_norm_g)
    m = dict(norm1_g=m_norm1_g, w_in=m_w_in, attn_norm_g=m_attn_norm_g, hgrn_norm_g=m_hgrn_norm_g,
             hgrn_lb_logits=m_hgrn_lb_logits, w_out=m_w_out, norm2_g=m_norm2_g, w_up=m_w_up, conv_w=m_conv_w,
             conv_b=m_conv_b, w_down=m_w_down, final_norm_g=m_final_norm_g)
    v = dict(norm1_g=v_norm1_g, w_in=v_w_in, attn_norm_g=v_attn_norm_g, hgrn_norm_g=v_hgrn_norm_g,
             hgrn_lb_logits=v_hgrn_lb_logits, w_out=v_w_out, norm2_g=v_norm2_g, w_up=v_w_up, conv_w=v_conv_w,
             conv_b=v_conv_b, w_down=v_w_down, final_norm_g=v_final_norm_g)
    names = list(w)
    chip = 2 * lax.axis_index("x") + lax.axis_index("y")

    full = _gather_weights([w[k][0].astype(BF16) for k in BIG], conv_w[0])
    w_in4, w_out4, w_up4, w_down4, conv_w4 = full
    conv_w_full = jnp.transpose(conv_w4, (1, 0, 2)).reshape(3, D_FF)
    lb = jax.nn.softmax(hgrn_lb_logits, axis=0)[0:1]

    loss, dx, small, big = _local_step(
        x[0], loss_target[0], norm1_g, w_in4, attn_norm_g, hgrn_norm_g, lb, w_out4.reshape(D_MODEL, D_MODEL), norm2_g,
        w_up4, conv_w_full, conv_b, w_down4.reshape(D_FF, D_MODEL), final_norm_g.reshape(1, D_MODEL))

    dlb = small["lb"] * lb * (1.0 - lb)
    grads_small = dict(norm1_g=small["g1"], attn_norm_g=small["g_a"], hgrn_norm_g=small["g_h"],
                       hgrn_lb_logits=jnp.concatenate([dlb, -dlb], axis=0), norm2_g=small["g2"],
                       conv_b=small["conv_b"], final_norm_g=small["gf"], conv_w=small["conv_w"])
    summed = _allreduce_small(_pack([grads_small[k] for k, _ in SMALL] + [loss[0, 0:1]], SMALL_ROWS)).reshape(-1)
    g = {}
    off = 0
    for k, size in SMALL:
        g[k] = summed[off:off + size]
        off += size
    loss_total = summed[off]
    g["conv_w"] = lax.dynamic_slice(g["conv_w"].reshape(3, D_FF), (0, chip * (D_FF // N_CHIPS)), (3, D_FF // N_CHIPS))

    gs = [big[k] for k in BIG]
    gots = _pair_exchange(gs)
    ps = [_pair_sum(a, b, f"pair_sum_{k}") for a, b, k in zip(gs, gots, BIG)]
    landed = _chip_exchange(ps)
    reds = [_sum_partials(a, b, l, f"sum_partials_{k}") for a, b, l, k in zip(gs, gots, landed, BIG)]
    for k, shard in zip(BIG, _pair_share(reds)):
        g[k] = shard

    delta, new_m, new_v = {}, {}, {}
    for k in BIG:
        delta[k], new_m[k], new_v[k] = (a[None] for a in _adamw(w[k][0], g[k], m[k][0], v[k][0], f"adamw_{k}"))
        g[k] = g[k][None]
    small_names = [k for k in names if k not in BIG]
    rows = 80
    packed = _adamw(_pack([w[k] for k in small_names], rows), _pack([g[k] for k in small_names], rows),
                    _pack([m[k] for k in small_names], rows), _pack([v[k] for k in small_names], rows), "adamw_small", tr=rows)
    flat = [a.reshape(-1) for a in packed]
    off = 0
    for k in small_names:
        size = w[k].size
        delta[k], new_m[k], new_v[k] = (a[off:off + size].reshape(w[k].shape) for a in flat)
        g[k] = g[k].reshape(w[k].shape)
        off += size

    return (loss_total, dx[None], *[g[k] for k in names], *[delta[k] for k in names],
            *[new_m[k] for k in names], *[new_v[k] for k in names])
```

```python
import functools
import math

import jax
import jax.numpy as jnp
from jax import lax
from jax.experimental import pallas as pl
from jax.experimental.pallas import tpu as pltpu

F32 = jnp.float32
BF16 = jnp.bfloat16

D_MODEL = 1024
ATTN_W = 512
HGRN_W = 512
HEAD_PAIR = 128
ATTN_BLK = 128
DILATIONS = (1, 4, 16)
ATTN_CHAINS = 4
HGRN_HEADS = 4
HGRN_DIM = 128
HGRN_CHUNK = 64
SUPER = 256
D_FF = 2816
N_CHIPS = 4
IN_TOTAL = 3584
IN_SHARD = IN_TOTAL // N_CHIPS
UP_SHARD = 2 * D_FF // N_CHIPS
QKV_W = 3 * ATTN_W
HG_W = 4 * HGRN_W
EPS = 1e-6
NEG = -1e30
V7X_VMEM_BYTES = 64 * 1024 * 1024
VMEM_LIMIT = V7X_VMEM_BYTES - 8 * 1024 * 1024

ADAM_LR = 0.001
ADAM_B1 = 0.9
ADAM_B2 = 0.999
ADAM_EPS = 1e-08
ADAM_WD = 0.01
ADAM_STEP = 10

MESH = pl.DeviceIdType.MESH


def _cp(*sem):
    return pltpu.CompilerParams(dimension_semantics=sem or None, vmem_limit_bytes=VMEM_LIMIT)


def _dot(a, b):
    return jnp.dot(a, b, preferred_element_type=F32)


def _dot_nt(a, b):
    return lax.dot_general(a, b, (((1,), (1,)), ((), ())), preferred_element_type=F32)


def _dot_tn(a, b):
    return lax.dot_general(a, b, (((0,), (0,)), ((), ())), preferred_element_type=F32)


def _sigmoid(x):
    return 1.0 / (1.0 + jnp.exp(-x))


def _rms(x, width):
    return lax.rsqrt(jnp.sum(x * x, axis=-1, keepdims=True) * (1.0 / width) + EPS)


def _rms_bwd(dn, n, r, width):
    return r * (dn - n * (jnp.sum(dn * n, axis=-1, keepdims=True) * (1.0 / width)))


def _colsum(x):
    return jnp.sum(x, axis=0, keepdims=True)


def _row(v, k):
    rid = lax.broadcasted_iota(jnp.int32, v.shape, 0)
    return jnp.sum(jnp.where(rid == k, v, 0.0), axis=0, keepdims=True)


def _full(shape):
    return pl.BlockSpec(shape, lambda *_: (0,) * len(shape))


def _once(shape):
    return pl.BlockSpec(shape, lambda *_: (0,) * len(shape), pipeline_mode=pl.Buffered(1))


def _in_proj(x, g1, w_in4, tm=256):
    T = x.shape[0]

    def body(x_ref, g_ref, w_ref, u_ref, qkv_ref, hg_ref):
        xv = x_ref[...]
        u = (xv * _rms(xv, D_MODEL) * g_ref[...]).astype(BF16)
        u_ref[...] = u
        p0 = _dot(u, w_ref[0])
        p1 = _dot(u, w_ref[1])
        qkv_ref[:, 0:IN_SHARD] = p0
        qkv_ref[:, IN_SHARD:QKV_W] = p1[:, :QKV_W - IN_SHARD]
        hg_ref[:, 0:2 * IN_SHARD - QKV_W] = p1[:, QKV_W - IN_SHARD:]
        hg_ref[:, 2 * IN_SHARD - QKV_W:3 * IN_SHARD - QKV_W] = _dot(u, w_ref[2])
        hg_ref[:, 3 * IN_SHARD - QKV_W:HG_W] = _dot(u, w_ref[3])

    return pl.pallas_call(
        body, name="in_proj", grid=(T // tm,),
        in_specs=[pl.BlockSpec((tm, D_MODEL), lambda i: (i, 0)), _full((1, D_MODEL)),
                  _once((N_CHIPS, D_MODEL, IN_SHARD))],
        out_specs=[pl.BlockSpec((tm, D_MODEL), lambda i: (i, 0)), pl.BlockSpec((tm, QKV_W), lambda i: (i, 0)),
                   pl.BlockSpec((tm, HG_W), lambda i: (i, 0))],
        out_shape=[jax.ShapeDtypeStruct((T, D_MODEL), BF16), jax.ShapeDtypeStruct((T, QKV_W), F32),
                   jax.ShapeDtypeStruct((T, HG_W), F32)],
        compiler_params=_cp("arbitrary"),
    )(x, g1, w_in4)


def _attn_masks(bias_ref):
    lane = lax.broadcasted_iota(jnp.int32, (ATTN_BLK, HEAD_PAIR), 1)
    row = lax.broadcasted_iota(jnp.int32, (2 * ATTN_BLK, 2 * ATTN_BLK), 0)
    col = lax.broadcasted_iota(jnp.int32, (2 * ATTN_BLK, 2 * ATTN_BLK), 1)
    base = jnp.where(row >= ATTN_BLK, row - ATTN_BLK, row) - col
    for k in range(2):
        dist = base + k * ATTN_BLK
        bias_ref[k] = jnp.where((dist >= 0) & (dist <= ATTN_BLK), 0.0, NEG)
    return lane < 64


def _two_heads(blk, first):
    zero = jnp.zeros_like(blk)
    return jnp.concatenate([jnp.where(first, blk, zero), jnp.where(first, zero, blk)], axis=0)


def _attn_rows(idx, nb, d):
    r, n = idx // nb, idx % nb
    kb = jnp.maximum(n - 1, 0)
    if d == 1:
        q0 = pl.multiple_of(n * ATTN_BLK, ATTN_BLK)
        k0 = pl.multiple_of(kb * ATTN_BLK, ATTN_BLK)
        return pl.ds(q0, ATTN_BLK), pl.ds(k0, 2 * ATTN_BLK), n - kb
    return (pl.ds(r + d * ATTN_BLK * n, ATTN_BLK, stride=d), pl.ds(r + d * ATTN_BLK * kb, 2 * ATTN_BLK, stride=d),
            n - kb)


def _attn_fwd(qkv):
    T = qkv.shape[0]

    per_chain = T // ATTN_BLK // ATTN_CHAINS

    def body(q_ref, k_ref, v_ref, o_ref, m_ref, l_ref, bias_ref):
        first = _attn_masks(bias_ref)
        for bi, d in enumerate(DILATIONS):
            nb = T // d // ATTN_BLK

            def block(idx, d=d, nb=nb, bi=bi):
                rows, keys, which = _attn_rows(idx, nb, d)
                q2 = _two_heads(q_ref[rows, :] * 0.125, first).astype(BF16)
                kw = k_ref[keys, :].astype(BF16)
                vw = v_ref[keys, :].astype(BF16)
                old = (o_ref[rows, :], m_ref[rows, :], l_ref[rows, :]) if bi else None
                s = _dot_nt(q2, kw) + bias_ref[which]
                mb = jnp.max(s, axis=-1, keepdims=True)
                p = jnp.exp(s - mb)
                lb = jnp.sum(p, axis=-1, keepdims=True)
                o2 = _dot(p.astype(BF16), vw)
                o = jnp.where(first, o2[:ATTN_BLK], o2[ATTN_BLK:])
                m = jnp.where(first, mb[:ATTN_BLK], mb[ATTN_BLK:])
                l = jnp.where(first, lb[:ATTN_BLK], lb[ATTN_BLK:])
                if bi:
                    po, pm, pl_ = old
                    mn = jnp.maximum(pm, m)
                    wa = jnp.exp(pm - mn)
                    wb = jnp.exp(m - mn)
                    o, l, m = po * wa + o * wb, pl_ * wa + l * wb, mn
                return rows, o, m, l

            def step(i, carry, block=block):
                done = [block(i + ch * per_chain) for ch in range(ATTN_CHAINS)]
                for rows, o, m, l in done:
                    o_ref[rows, :] = o
                    m_ref[rows, :] = m
                    l_ref[rows, :] = l
                return carry

            lax.fori_loop(0, per_chain, step, 0)

        def finish(i, carry):
            rows = pl.ds(pl.multiple_of(i * SUPER, SUPER), SUPER)
            l = l_ref[rows, :]
            o_ref[rows, :] = o_ref[rows, :] / l
            m_ref[rows, :] = m_ref[rows, :] + jnp.log(l)
            return carry

        lax.fori_loop(0, T // SUPER, finish, 0)

    col = lambda off: pl.BlockSpec((T, HEAD_PAIR), lambda j: (0, off + j))
    return pl.pallas_call(
        body, name="attn_fwd", grid=(4,),
        in_specs=[col(0), col(4), col(8)], out_specs=[col(0), col(0)],
        out_shape=[jax.ShapeDtypeStruct((T, ATTN_W), F32)] * 2,
        scratch_shapes=[pltpu.VMEM((T, HEAD_PAIR), F32), pltpu.VMEM((2, 2 * ATTN_BLK, 2 * ATTN_BLK), F32)],
        compiler_params=_cp("arbitrary"),
    )(qkv, qkv, qkv)


def _attn_bwd(qkv, o, lse, do, token=None):
    T = qkv.shape[0]
    per_chain = T // ATTN_BLK // ATTN_CHAINS
    extra = [] if token is None else [token]

    def body(q_ref, k_ref, v_ref, o_ref, lse_ref, do_ref, *rest):
        dq_ref, dk_ref, dv_ref, bias_ref = rest[len(extra):]
        first = _attn_masks(bias_ref)
        dq_ref[...] = jnp.zeros_like(dq_ref)
        dk_ref[...] = jnp.zeros_like(dk_ref)
        dv_ref[...] = jnp.zeros_like(dv_ref)
        for d in DILATIONS:
            nb = T // d // ATTN_BLK

            def block(idx, d=d, nb=nb):
                rows, keys, which = _attn_rows(idx, nb, d)
                q2 = _two_heads(q_ref[rows, :] * 0.125, first).astype(BF16)
                kw = k_ref[keys, :].astype(BF16)
                vw = v_ref[keys, :].astype(BF16)
                lse_b = lse_ref[rows, :]
                dob = do_ref[rows, :]
                prod = dob * o_ref[rows, :]
                old = dq_ref[rows, :], dk_ref[keys, :], dv_ref[keys, :]
                lse2 = jnp.concatenate(
                    [jnp.max(jnp.where(first, lse_b, NEG), axis=-1, keepdims=True),
                     jnp.max(jnp.where(first, NEG, lse_b), axis=-1, keepdims=True)], axis=0)
                p = jnp.exp(_dot_nt(q2, kw) + (bias_ref[which] - lse2))
                delta = jnp.concatenate(
                    [jnp.sum(jnp.where(first, prod, 0.0), axis=-1, keepdims=True),
                     jnp.sum(jnp.where(first, 0.0, prod), axis=-1, keepdims=True)], axis=0)
                do2 = _two_heads(dob, first).astype(BF16)
                ds = (p * (_dot_nt(do2, vw) - delta)).astype(BF16)
                dq2 = _dot(ds, kw) * 0.125
                return (rows, keys, old[0] + jnp.where(first, dq2[:ATTN_BLK], dq2[ATTN_BLK:]),
                        old[1] + _dot_tn(ds, q2), old[2] + _dot_tn(p.astype(BF16), do2))

            def step(i, carry, block=block):
                done = [block(i + ch * per_chain) for ch in range(ATTN_CHAINS)]
                for rows, keys, dq, dk, dv in done:
                    dq_ref[rows, :] = dq
                    dk_ref[keys, :] = dk
                    dv_ref[keys, :] = dv
                return carry

            lax.fori_loop(0, per_chain, step, 0)

    col = lambda off: pl.BlockSpec((T, HEAD_PAIR), lambda j: (0, off + j))
    return pl.pallas_call(
        body, name="attn_bwd", grid=(4,),
        in_specs=[col(0), col(4), col(8), col(0), col(0), col(0)] + [_full(t.shape) for t in extra],
        out_specs=[col(0)] * 3,
        out_shape=[jax.ShapeDtypeStruct((T, ATTN_W), F32)] * 3,
        scratch_shapes=[pltpu.VMEM((2, 2 * ATTN_BLK, 2 * ATTN_BLK), F32)],
        compiler_params=_cp("arbitrary"),
    )(qkv, qkv, qkv, o, lse, do, *extra)


def _chunk_ids():
    row = lax.broadcasted_iota(jnp.int32, (SUPER, HGRN_DIM), 0)
    r2 = lax.broadcasted_iota(jnp.int32, (SUPER, SUPER), 0)
    c2 = lax.broadcasted_iota(jnp.int32, (SUPER, SUPER), 1)
    amask = ((r2 // HGRN_CHUNK) == (c2 // HGRN_CHUNK)) & (c2 <= r2)
    return row % HGRN_CHUNK, row // HGRN_CHUNK, amask


def _cumsum_chunk(x, rmod):
    s = 1
    while s < HGRN_CHUNK:
        x = x + jnp.where(rmod >= s, pltpu.roll(x, s, 0), 0.0)
        s *= 2
    return x


def _suffix_sum_chunk(x, rmod):
    s = 1
    while s < HGRN_CHUNK:
        x = x + jnp.where(rmod < HGRN_CHUNK - s, pltpu.roll(x, SUPER - s, 0), 0.0)
        s *= 2
    return x


def _chunk_rows(vs, cid):
    out = vs[-1]
    for c in reversed(range(len(vs) - 1)):
        out = jnp.where(cid == c, vs[c], out)
    return out


def _expand(x, cid):
    return jnp.concatenate([jnp.where(cid == c, x, 0.0) for c in range(SUPER // HGRN_CHUNK)], axis=1)


def _hgrn_gates(q, f, lbv, rmod, cid, tmp):
    sq = _sigmoid(q)
    sg = _sigmoid(f)
    forget = lbv + (1.0 - lbv) * sg
    key = 1.0 - forget
    b = _cumsum_chunk(jnp.log(forget), rmod)
    tmp[...] = b
    bends = [tmp[c * HGRN_CHUNK + HGRN_CHUNK - 1:(c + 1) * HGRN_CHUNK, :] for c in range(SUPER // HGRN_CHUNK)]
    eb = jnp.exp(b)
    enb = jnp.exp(-b)
    ebe = jnp.exp(_chunk_rows(bends, cid) - b)
    return sq, sg, forget, key, eb, enb, ebe, q * sq * eb, key * enb, key * ebe, [jnp.exp(v) for v in bends]


def _hgrn_fwd(hg, lb):
    T = hg.shape[0]
    nsc = T // SUPER
    NC = SUPER // HGRN_CHUNK

    def body(q_ref, f_ref, i_ref, lb_ref, o_ref, st_ref, state, tmp):
        rmod, cid, amask = _chunk_ids()
        state[...] = jnp.zeros_like(state)
        lbv = lb_ref[...]

        def step(sc, carry):
            rows = pl.ds(pl.multiple_of(sc * SUPER, SUPER), SUPER)
            iv = i_ref[rows, :].astype(BF16)
            qd, ki, ke, dec = _hgrn_gates(q_ref[rows, :], f_ref[rows, :], lbv, rmod, cid, tmp)[-4:]
            a = jnp.where(amask, _dot_nt(qd.astype(BF16), ki.astype(BF16)), 0.0)
            o = _dot(a.astype(BF16), iv)
            ut = _dot_tn(iv, _expand(ke, cid).astype(BF16))
            st = state[...]
            st_ref[0, sc] = st
            sts = []
            for c in range(NC):
                sts.append(st)
                st = st * dec[c] + ut[:, c * HGRN_DIM:(c + 1) * HGRN_DIM]
            state[...] = st
            o = o + _dot_nt(_expand(qd, cid).astype(BF16), jnp.concatenate(sts, axis=1).astype(BF16))
            o_ref[rows, :] = o
            return carry

        lax.fori_loop(0, nsc, step, 0)

    col = lambda off: pl.BlockSpec((T, HGRN_DIM), lambda h: (0, off + h))
    return pl.pallas_call(
        body, name="hgrn_fwd", grid=(HGRN_HEADS,),
        in_specs=[col(0), col(4), col(8), pl.BlockSpec((1, HGRN_DIM), lambda h: (0, h))],
        out_specs=[pl.BlockSpec((T, HGRN_DIM), lambda h: (0, h)),
                   pl.BlockSpec((1, nsc, HGRN_DIM, HGRN_DIM), lambda h: (h, 0, 0, 0))],
        out_shape=[jax.ShapeDtypeStruct((T, HGRN_W), F32),
                   jax.ShapeDtypeStruct((HGRN_HEADS, nsc, HGRN_DIM, HGRN_DIM), F32)],
        scratch_shapes=[pltpu.VMEM((HGRN_DIM, HGRN_DIM), F32), pltpu.VMEM((SUPER, HGRN_DIM), F32)],
        compiler_params=_cp("arbitrary"),
    )(hg, hg, hg, lb)


def _hgrn_bwd(hg, lb, states, do):
    T = hg.shape[0]
    nsc = T // SUPER
    NC = SUPER // HGRN_CHUNK

    def body(q_ref, f_ref, i_ref, lb_ref, st_ref, do_ref, dq_ref, df_ref, di_ref, dlb_ref, dstate, tmp):
        rmod, cid, amask = _chunk_ids()
        dstate[...] = jnp.zeros_like(dstate)
        dlb_ref[...] = jnp.zeros_like(dlb_ref)
        lbv = lb_ref[...]

        def step(k, carry):
            sc = nsc - 1 - k
            rows = pl.ds(pl.multiple_of(sc * SUPER, SUPER), SUPER)
            q = q_ref[rows, :]
            ivf = i_ref[rows, :]
            iv = ivf.astype(BF16)
            dof = do_ref[rows, :]
            dob = dof.astype(BF16)
            sq, sg, forget, key, eb, enb, ebe, qd, ki, ke, dec = _hgrn_gates(q, f_ref[rows, :], lbv, rmod, cid, tmp)
            qdb, kib = qd.astype(BF16), ki.astype(BF16)
            keexp = _expand(ke, cid).astype(BF16)
            a = jnp.where(amask, _dot_nt(qdb, kib), 0.0).astype(BF16)
            ut = _dot_tn(iv, keexp)
            st = st_ref[0, sc]
            sts = []
            for c in range(NC):
                sts.append(st)
                st = st * dec[c] + ut[:, c * HGRN_DIM:(c + 1) * HGRN_DIM]
            gt = _dot_tn(dob, _expand(qd, cid).astype(BF16))
            nxt = [None] * NC
            ddec = [None] * NC
            dst = dstate[...]
            for c in reversed(range(NC)):
                nxt[c] = dst
                ddec[c] = _colsum(dst * sts[c])
                dst = dst * dec[c] + gt[:, c * HGRN_DIM:(c + 1) * HGRN_DIM]
            dstate[...] = dst
            da = jnp.where(amask, _dot_nt(dob, iv), 0.0).astype(BF16)
            ncat = jnp.concatenate(nxt, axis=1).astype(BF16)
            nstack = jnp.concatenate(nxt, axis=0).astype(BF16)
            ststack = jnp.concatenate(sts, axis=0).astype(BF16)
            div = _dot_tn(a, dob) + _dot_nt(keexp, ncat)
            dke = _dot(_expand(ivf, cid).astype(BF16), nstack)
            dqd = _dot(da, kib) + _dot(_expand(dof, cid).astype(BF16), ststack)
            dki = _dot_tn(da, qdb)
            dkk = dke * ke
            dkey = dki * enb + dke * ebe
            db = dqd * qd - dki * ki - dkk
            dbends = [_colsum(jnp.where(cid == c, dkk, 0.0)) + ddec[c] * dec[c] for c in range(NC)]
            dlogf = _suffix_sum_chunk(db, rmod) + _chunk_rows(dbends, cid)
            dforget = dlogf / forget - dkey
            df_ref[rows, :] = dforget * (1.0 - lbv) * sg * (1.0 - sg)
            dlb_ref[...] += _colsum(dforget * (1.0 - sg))
            dq_ref[rows, :] = dqd * eb * (sq * (1.0 + q * (1.0 - sq)))
            di_ref[rows, :] = div
            return carry

        lax.fori_loop(0, nsc, step, 0)

    col = lambda off: pl.BlockSpec((T, HGRN_DIM), lambda h: (0, off + h))
    own = pl.BlockSpec((T, HGRN_DIM), lambda h: (0, h))
    vec = pl.BlockSpec((1, HGRN_DIM), lambda h: (0, h))
    return pl.pallas_call(
        body, name="hgrn_bwd", grid=(HGRN_HEADS,),
        in_specs=[col(0), col(4), col(8), vec,
                  pl.BlockSpec((1, nsc, HGRN_DIM, HGRN_DIM), lambda h: (h, 0, 0, 0)), own],
        out_specs=[own, own, own, vec],
        out_shape=[jax.ShapeDtypeStruct((T, HGRN_W), F32)] * 3 + [jax.ShapeDtypeStruct((1, HGRN_W), F32)],
        scratch_shapes=[pltpu.VMEM((HGRN_DIM, HGRN_DIM), F32), pltpu.VMEM((SUPER, HGRN_DIM), F32)],
        compiler_params=_cp("arbitrary"),
    )(hg, hg, hg, lb, states, do)


def _rec_heads(rec, gate, g_h):
    rr = jnp.concatenate(
        [jnp.broadcast_to(_rms(rec[:, h * HGRN_DIM:(h + 1) * HGRN_DIM], HGRN_DIM), (rec.shape[0], HGRN_DIM))
         for h in range(HGRN_HEADS)], axis=1)
    rn = rec * rr
    sg = _sigmoid(gate)
    return rr, rn, sg


def _mix_out(attn_o, rec_o, hg, x, g_a, g_h, w_out, tm=256):
    T = x.shape[0]

    def body(a_ref, r_ref, gt_ref, x_ref, ga_ref, gh_ref, w_ref, h1_ref, mixed_ref):
        a = a_ref[...]
        an = a * _rms(a, ATTN_W) * ga_ref[...]
        gate = gt_ref[...]
        _, rn, sg = _rec_heads(r_ref[...], gate, gh_ref[...])
        mixed = jnp.concatenate([an, rn * gh_ref[...] * (gate * sg)], axis=1).astype(BF16)
        mixed_ref[...] = mixed
        h1_ref[...] = x_ref[...] + _dot(mixed, w_ref[...])

    row = lambda w: pl.BlockSpec((tm, w), lambda i: (i, 0))
    return pl.pallas_call(
        body, name="mix_out", grid=(T // tm,),
        in_specs=[row(ATTN_W), row(HGRN_W), pl.BlockSpec((tm, HGRN_W), lambda i: (i, 3)), row(D_MODEL),
                  _full((1, ATTN_W)), _full((1, HGRN_W)), _once((D_MODEL, D_MODEL))],
        out_specs=[row(D_MODEL), row(D_MODEL)],
        out_shape=[jax.ShapeDtypeStruct((T, D_MODEL), F32), jax.ShapeDtypeStruct((T, D_MODEL), BF16)],
        compiler_params=_cp("arbitrary"),
    )(attn_o, rec_o, hg, x, g_a, g_h, w_out)


_INV_SQRT2 = 1.0 / math.sqrt(2.0)
_INV_SQRT2PI = 1.0 / math.sqrt(2.0 * math.pi)


def _gelu(x):
    return 0.5 * x * (1.0 + lax.erf(x * _INV_SQRT2))


def _gelu_grad(x):
    return 0.5 * (1.0 + lax.erf(x * _INV_SQRT2)) + x * jnp.exp(-0.5 * x * x) * _INV_SQRT2PI


def _shift_down(g, prev, rowid):
    p1 = _row(prev, prev.shape[0] - 1)
    p2 = _row(prev, prev.shape[0] - 2)
    s1 = jnp.where(rowid == 0, p1, pltpu.roll(g, 1, 0))
    s2 = jnp.where(rowid == 0, p2, jnp.where(rowid == 1, p1, pltpu.roll(g, 2, 0)))
    return s1, s2


def _mlp_fwd(h1, g2, w_up4, conv_w, conv_b, w_down, gf, tgt, tm=256):
    T = h1.shape[0]
    half = D_FF // 2

    def body(h_ref, g2_ref, wu_ref, cw_ref, cb_ref, wd_ref, gf_ref, t_ref,
             u_ref, gate_ref, val_ref, dh_ref, loss_ref, dgf_ref, carry):
        i = pl.program_id(0)

        @pl.when(i == 0)
        def _():
            carry[...] = jnp.zeros_like(carry)
            loss_ref[...] = jnp.zeros_like(loss_ref)
            dgf_ref[...] = jnp.zeros_like(dgf_ref)

        h = h_ref[...]
        u = (h * _rms(h, D_MODEL) * g2_ref[...]).astype(BF16)
        u_ref[...] = u
        rowid = lax.broadcasted_iota(jnp.int32, (tm, half), 0)
        y2 = jnp.zeros((tm, D_MODEL), F32)
        for c in range(2):
            cols = slice(c * half, (c + 1) * half)
            gb = _dot(u, wu_ref[c]).astype(BF16)
            vb = _dot(u, wu_ref[2 + c]).astype(BF16)
            gate_ref[:, cols] = gb
            val_ref[:, cols] = vb
            g = gb.astype(F32)
            s1, s2 = _shift_down(g, carry[:, cols], rowid)
            carry[:, cols] = g[tm - 8:, :]
            conv = cb_ref[:, cols] + cw_ref[0:1, cols] * s2 + cw_ref[1:2, cols] * s1 + cw_ref[2:3, cols] * g
            act = (_gelu(conv) * vb.astype(F32)).astype(BF16)
            y2 = y2 + _dot(act, wd_ref[cols, :])
        h2 = h + y2
        rf = _rms(h2, D_MODEL)
        n = h2 * rf
        gfv = gf_ref[...]
        e = n * gfv - t_ref[...]
        loss_ref[...] += jnp.sum(e * e) * (0.5 / D_MODEL)
        dy = e * (1.0 / D_MODEL)
        dgf_ref[...] += _colsum(dy * n)
        dh_ref[...] = _rms_bwd(dy * gfv, n, rf, D_MODEL)

    row = lambda w: pl.BlockSpec((tm, w), lambda i: (i, 0))
    return pl.pallas_call(
        body, name="mlp_fwd", grid=(T // tm,),
        in_specs=[row(D_MODEL), _full((1, D_MODEL)), _once((N_CHIPS, D_MODEL, UP_SHARD)), _full((3, D_FF)),
                  _full((1, D_FF)), _once((D_FF, D_MODEL)), _full((1, D_MODEL)), row(D_MODEL)],
        out_specs=[row(D_MODEL), row(D_FF), row(D_FF), row(D_MODEL), _full((1, 128)), _full((1, D_MODEL))],
        out_shape=[jax.ShapeDtypeStruct((T, D_MODEL), BF16), jax.ShapeDtypeStruct((T, D_FF), BF16),
                   jax.ShapeDtypeStruct((T, D_FF), BF16), jax.ShapeDtypeStruct((T, D_MODEL), F32),
                   jax.ShapeDtypeStruct((1, 128), F32), jax.ShapeDtypeStruct((1, D_MODEL), F32)],
        scratch_shapes=[pltpu.VMEM((8, D_FF), F32)],
        compiler_params=_cp("arbitrary"),
    )(h1, g2, w_up4, conv_w, conv_b, w_down, gf, tgt)


def _mlp_bwd(dh2, gate, val, conv_w, conv_b, w_down, tm=256):
    T = dh2.shape[0]
    nb = T // tm
    half = D_FF // 2

    def body(dh_ref, gate_ref, halo_ref, val_ref, cw_ref, cb_ref, wd_ref,
             dgv_ref, act_ref, dcw_ref, dcb_ref, carry):
        i = pl.program_id(0)

        @pl.when(i == 0)
        def _():
            carry[...] = jnp.zeros_like(carry)
            dcw_ref[...] = jnp.zeros_like(dcw_ref)
            dcb_ref[...] = jnp.zeros_like(dcb_ref)

        dhb = dh_ref[...].astype(BF16)
        rowid = lax.broadcasted_iota(jnp.int32, (tm, half), 0)
        has_prev = (i < nb - 1).astype(F32)
        for c in range(2):
            cols = slice(c * half, (c + 1) * half)
            g = gate_ref[:, cols].astype(F32)
            v = val_ref[:, cols].astype(F32)
            s1, s2 = _shift_down(g, halo_ref[:, cols].astype(F32) * has_prev, rowid)
            conv = cb_ref[:, cols] + cw_ref[0:1, cols] * s2 + cw_ref[1:2, cols] * s1 + cw_ref[2:3, cols] * g
            gl = _gelu(conv)
            act_ref[:, cols] = (gl * v).astype(BF16)
            dact = _dot_nt(dhb, wd_ref[cols, :])
            dconv = dact * v * _gelu_grad(conv)
            dcb_ref[:, cols] += _colsum(dconv)
            dcw_ref[0:1, cols] += _colsum(dconv * s2)
            dcw_ref[1:2, cols] += _colsum(dconv * s1)
            dcw_ref[2:3, cols] += _colsum(dconv * g)
            nxt = carry[:, cols]
            n0, n1 = _row(nxt, 0), _row(nxt, 1)
            u1 = jnp.where(rowid == tm - 1, n0, pltpu.roll(dconv, tm - 1, 0))
            u2 = jnp.where(rowid == tm - 1, n1, jnp.where(rowid == tm - 2, n0, pltpu.roll(dconv, tm - 2, 0)))
            carry[:, cols] = dconv[0:8, :]
            dgate = cw_ref[2:3, cols] * dconv + cw_ref[1:2, cols] * u1 + cw_ref[0:1, cols] * u2
            dgv_ref[:, cols] = dgate.astype(BF16)
            dgv_ref[:, D_FF + c * half:D_FF + (c + 1) * half] = (dact * gl).astype(BF16)

    rev = lambda w: pl.BlockSpec((tm, w), lambda i: (nb - 1 - i, 0))
    halo = pl.BlockSpec((16, D_FF), lambda i: (jnp.maximum((nb - 1 - i) * (tm // 16) - 1, 0), 0))
    return pl.pallas_call(
        body, name="mlp_bwd", grid=(nb,),
        in_specs=[rev(D_MODEL), rev(D_FF), halo, rev(D_FF), _full((3, D_FF)), _full((1, D_FF)),
                  _once((D_FF, D_MODEL))],
        out_specs=[rev(2 * D_FF), rev(D_FF), _full((3, D_FF)), _full((1, D_FF))],
        out_shape=[jax.ShapeDtypeStruct((T, 2 * D_FF), BF16), jax.ShapeDtypeStruct((T, D_FF), BF16),
                   jax.ShapeDtypeStruct((3, D_FF), F32), jax.ShapeDtypeStruct((1, D_FF), F32)],
        scratch_shapes=[pltpu.VMEM((8, D_FF), F32)],
        compiler_params=_cp("arbitrary"),
    )(dh2, gate, gate, val, conv_w, conv_b, w_down)


def _up_out_bwd(dgv, w_up4, h1, g2, dh2, w_out, attn_o, rec_o, hg, g_a, g_h, tm=256):
    T = h1.shape[0]

    def body(dgv_ref, wu_ref, h_ref, g2_ref, dh2_ref, wo_ref, a_ref, r_ref, gt_ref, ga_ref, gh_ref,
             dh1_ref, dg2_ref, da_ref, dr_ref, dgt_ref, dga_ref, dgh_ref):
        @pl.when(pl.program_id(0) == 0)
        def _():
            dg2_ref[...] = jnp.zeros_like(dg2_ref)
            dga_ref[...] = jnp.zeros_like(dga_ref)
            dgh_ref[...] = jnp.zeros_like(dgh_ref)

        du = jnp.zeros((tm, D_MODEL), F32)
        for k in range(N_CHIPS):
            du = du + _dot_nt(dgv_ref[:, k * UP_SHARD:(k + 1) * UP_SHARD], wu_ref[k])
        h = h_ref[...]
        r = _rms(h, D_MODEL)
        n = h * r
        dg2_ref[...] += _colsum(du * n)
        dh1 = dh2_ref[...] + _rms_bwd(du * g2_ref[...], n, r, D_MODEL)
        dh1_ref[...] = dh1
        dmix = _dot_nt(dh1.astype(BF16), wo_ref[...])
        dan = dmix[:, :ATTN_W]
        a = a_ref[...]
        ra = _rms(a, ATTN_W)
        na = a * ra
        dga_ref[...] += _colsum(dan * na)
        da_ref[...] = _rms_bwd(dan * ga_ref[...], na, ra, ATTN_W)
        dmr = dmix[:, ATTN_W:]
        gate = gt_ref[...]
        ghv = gh_ref[...]
        rr, rn, sg = _rec_heads(r_ref[...], gate, ghv)
        dgt_ref[...] = dmr * rn * ghv * (sg * (1.0 + gate * (1.0 - sg)))
        drecn = dmr * (gate * sg)
        dgh_ref[...] += _colsum(drecn * rn)
        drn = drecn * ghv
        prod = drn * rn
        mean = jnp.concatenate(
            [jnp.broadcast_to(jnp.sum(prod[:, h_ * HGRN_DIM:(h_ + 1) * HGRN_DIM], axis=-1, keepdims=True),
                              (tm, HGRN_DIM)) for h_ in range(HGRN_HEADS)], axis=1) * (1.0 / HGRN_DIM)
        dr_ref[...] = rr * (drn - rn * mean)

    row = lambda w: pl.BlockSpec((tm, w), lambda i: (i, 0))
    return pl.pallas_call(
        body, name="up_out_bwd", grid=(T // tm,),
        in_specs=[row(2 * D_FF), _once((N_CHIPS, D_MODEL, UP_SHARD)), row(D_MODEL), _full((1, D_MODEL)),
                  row(D_MODEL), _once((D_MODEL, D_MODEL)), row(ATTN_W), row(HGRN_W),
                  pl.BlockSpec((tm, HGRN_W), lambda i: (i, 3)), _full((1, ATTN_W)), _full((1, HGRN_W))],
        out_specs=[row(D_MODEL), _full((1, D_MODEL)), row(ATTN_W), row(HGRN_W), row(HGRN_W),
                   _full((1, ATTN_W)), _full((1, HGRN_W))],
        out_shape=[jax.ShapeDtypeStruct((T, D_MODEL), F32), jax.ShapeDtypeStruct((1, D_MODEL), F32),
                   jax.ShapeDtypeStruct((T, ATTN_W), F32), jax.ShapeDtypeStruct((T, HGRN_W), F32),
                   jax.ShapeDtypeStruct((T, HGRN_W), F32), jax.ShapeDtypeStruct((1, ATTN_W), F32),
                   jax.ShapeDtypeStruct((1, HGRN_W), F32)],
        compiler_params=_cp("arbitrary"),
    )(dgv, w_up4, h1, g2, dh2, w_out, attn_o, rec_o, hg, g_a, g_h)


def _in_bwd(dqkv, dhg, w_in4, x, g1, dh1, tm=256):
    T = x.shape[0]

    def body(*refs):
        parts = refs[:7]
        w_ref, x_ref, g_ref, dh1_ref, dp_ref, dx_ref, dg_ref = refs[7:]

        @pl.when(pl.program_id(0) == 0)
        def _():
            dg_ref[...] = jnp.zeros_like(dg_ref)

        dp = jnp.concatenate([p[...] for p in parts], axis=1).astype(BF16)
        dp_ref[...] = dp
        du = jnp.zeros((tm, D_MODEL), F32)
        for k in range(N_CHIPS):
            du = du + _dot_nt(dp[:, k * IN_SHARD:(k + 1) * IN_SHARD], w_ref[k])
        xv = x_ref[...]
        r = _rms(xv, D_MODEL)
        n = xv * r
        dg_ref[...] += _colsum(du * n)
        dx_ref[...] = dh1_ref[...] + _rms_bwd(du * g_ref[...], n, r, D_MODEL)

    row = lambda w: pl.BlockSpec((tm, w), lambda i: (i, 0))
    return pl.pallas_call(
        body, name="in_bwd", grid=(T // tm,),
        in_specs=[row(ATTN_W)] * 7 + [_once((N_CHIPS, D_MODEL, IN_SHARD)), row(D_MODEL), _full((1, D_MODEL)),
                                       row(D_MODEL)],
        out_specs=[row(IN_TOTAL), row(D_MODEL), _full((1, D_MODEL))],
        out_shape=[jax.ShapeDtypeStruct((T, IN_TOTAL), BF16), jax.ShapeDtypeStruct((T, D_MODEL), F32),
                   jax.ShapeDtypeStruct((1, D_MODEL), F32)],
        compiler_params=_cp("arbitrary"),
    )(*dqkv, *dhg, w_in4, x, g1, dh1)


def _dw(a, b, kb, nb_, name, tk=512):
    T, K = a.shape
    N = b.shape[1]
    nk, nn, nt = K // kb, N // nb_, T // tk

    def body(a_ref, b_ref, o_ref, acc):
        t = pl.program_id(2)

        @pl.when(t == 0)
        def _():
            acc[...] = jnp.zeros_like(acc)

        acc[...] += _dot_tn(a_ref[...], b_ref[...].astype(BF16))

        @pl.when(t == nt - 1)
        def _():
            o_ref[0] = acc[...].astype(BF16)

    return pl.pallas_call(
        body, name=name, grid=(nk, nn, nt),
        in_specs=[pl.BlockSpec((tk, kb), lambda i, j, t: (t, i)), pl.BlockSpec((tk, nb_), lambda i, j, t: (t, j))],
        out_specs=pl.BlockSpec((1, kb, nb_), lambda i, j, t: (i * nn + j, 0, 0)),
        out_shape=jax.ShapeDtypeStruct((nk * nn, kb, nb_), BF16),
        scratch_shapes=[pltpu.VMEM((kb, nb_), F32)],
        compiler_params=_cp("arbitrary", "arbitrary", "arbitrary"),
    )(a, b)


def _local_step(x, tgt, g1, w_in4, g_a, g_h, lb, w_out, g2, w_up4, conv_w, conv_b, w_down, gf):
    a = _step_mixers(x, g1, w_in4, lb)
    b = _step_channel(a, x, tgt, g_a, g_h, w_out, g2, w_up4, conv_w, conv_b, w_down, gf)
    c = _step_mixers_bwd(a, b, x, g1, w_in4, lb)
    small = dict(g1=c["dg1"], g_a=b["dga"], g_h=b["dgh"], lb=c["dlb"], g2=b["dg2"], conv_w=b["dcw"], conv_b=b["dcb"],
                 gf=b["dgf"])
    return b["loss"], c["dx"], small, dict(w_in=c["dw_in"], w_out=b["dw_out"], w_up=b["dw_up"], w_down=b["dw_down"])


def _step_mixers(x, g1, w_in4, lb):
    u1, qkv, hg = _in_proj(x, g1, w_in4)
    attn_o, lse = _attn_fwd(qkv)
    rec_o, states = _hgrn_fwd(hg, lb)
    return dict(u1=u1, qkv=qkv, hg=hg, attn_o=attn_o, lse=lse, rec_o=rec_o, states=states)


def _step_channel(a, x, tgt, g_a, g_h, w_out, g2, w_up4, conv_w, conv_b, w_down, gf):
    h1, mixed = _mix_out(a["attn_o"], a["rec_o"], a["hg"], x, g_a, g_h, w_out)
    u2, gate, val, dh2, loss, dgf = _mlp_fwd(h1, g2, w_up4, conv_w, conv_b, w_down, gf, tgt)
    dgv, act, dcw, dcb = _mlp_bwd(dh2, gate, val, conv_w, conv_b, w_down)
    dw_down = _dw(act, dh2, D_FF // 2, D_MODEL, "dw_down").reshape(N_CHIPS, D_FF // N_CHIPS, D_MODEL)
    dh1, dg2, da, dr, dgt, dga, dgh = _up_out_bwd(dgv, w_up4, h1, g2, dh2, w_out, a["attn_o"], a["rec_o"], a["hg"],
                                                  g_a, g_h)
    dw_up = _dw(u2, dgv, D_MODEL, UP_SHARD, "dw_up")
    dw_out = _dw(mixed, dh1, D_MODEL, D_MODEL, "dw_out").reshape(N_CHIPS, D_MODEL // N_CHIPS, D_MODEL)
    return dict(loss=loss, dgf=dgf, dcw=dcw, dcb=dcb, dg2=dg2, dga=dga, dgh=dgh, dh1=dh1, da=da, dr=dr, dgt=dgt,
                dw_down=dw_down, dw_up=dw_up, dw_out=dw_out)


def _step_mixers_bwd(a, b, x, g1, w_in4, lb, token=None):
    dqkv = _attn_bwd(a["qkv"], a["attn_o"], a["lse"], b["da"], token)
    dhq, dhf, dhi, dlb = _hgrn_bwd(a["hg"], lb, a["states"], b["dr"])
    dproj, dx, dg1 = _in_bwd(dqkv, [dhq, dhf, dhi, b["dgt"]], w_in4, x, g1, b["dh1"])
    dw_in = _dw(a["u1"], dproj, D_MODEL, IN_SHARD, "dw_in")
    return dict(dx=dx, dg1=dg1, dlb=dlb, dw_in=dw_in)


BIG = ("w_in", "w_out", "w_up", "w_down")
ANY = pl.BlockSpec(memory_space=pl.ANY)


def _place():
    x, y, c = lax.axis_index("x"), lax.axis_index("y"), lax.axis_index("c")
    chips = [(1 - x, y), (x, 1 - y), (1 - x, 1 - y)]
    return x, y, c, chips


def _remote(src, dst, send_sems, recv_sems, k, to):
    return pltpu.make_async_remote_copy(src_ref=src, dst_ref=dst, send_sem=send_sems.at[k], recv_sem=recv_sems.at[k],
                                        device_id=to, device_id_type=MESH)


def _gather_weights(shards, conv_w):
    n = len(shards)
    halves = [s.shape[0] // 2 for s in shards]

    def body(*refs):
        ins, cw, outs, ocw = refs[:n], refs[n], refs[n + 1:2 * n + 1], refs[2 * n + 1]
        send_sems, recv_sems = refs[2 * n + 2:]
        x, y, c, chips = _place()
        me, sibling = 2 * x + y, (x, y, 1 - c)

        def part(w, chip, half):
            return outs[w].at[chip, pl.ds(half * halves[w], halves[w]), :]

        sent = []
        for j, chip in enumerate(chips):
            for w in range(n):
                sent.append(_remote(ins[w].at[pl.ds(c * halves[w], halves[w]), :], part(w, me, c),
                                    send_sems, recv_sems, w * 3 + j, (*chip, c)))
            sent.append(_remote(cw, ocw.at[me], send_sems, recv_sems, 6 * n + j, (*chip, c)))
        for cp in sent:
            cp.start()
        for j, chip in enumerate(chips):
            kj = 2 * chip[0] + chip[1]
            for w in range(n):
                _remote(part(w, kj, c), part(w, kj, c), send_sems, recv_sems, w * 3 + j, (*chip, c)).wait_recv()
                fwd = _remote(part(w, kj, c), part(w, kj, c), send_sems, recv_sems, 3 * n + w * 3 + j, sibling)
                fwd.start()
                sent.append(fwd)
        for j, chip in enumerate(chips):
            kj = 2 * chip[0] + chip[1]
            for w in range(n):
                _remote(part(w, kj, 1 - c), part(w, kj, 1 - c), send_sems, recv_sems, 3 * n + w * 3 + j,
                        sibling).wait_recv()
            _remote(cw, ocw.at[kj], send_sems, recv_sems, 6 * n + j, (*chip, c)).wait_recv()
        for cp in sent:
            cp.wait_send()

    n_sem = 6 * n + 3
    outs = pl.pallas_call(
        body, name="gather_weights",
        in_specs=[ANY] * (n + 1), out_specs=[ANY] * (n + 1),
        out_shape=[jax.ShapeDtypeStruct((N_CHIPS,) + s.shape, s.dtype) for s in shards]
        + [jax.ShapeDtypeStruct((N_CHIPS,) + conv_w.shape, conv_w.dtype)],
        scratch_shapes=[pltpu.SemaphoreType.DMA((n_sem,)), pltpu.SemaphoreType.DMA((n_sem,))],
    )(*shards, conv_w)
    chip = 2 * lax.axis_index("x") + lax.axis_index("y")
    return [lax.dynamic_update_slice(o, s[None], (chip,) + (0,) * s.ndim) for o, s in zip(outs, [*shards, conv_w])]


def _allreduce_small(buf):
    rows = buf.shape[0]

    def body(in_ref, out_ref, slots, send_sems, recv_sems):
        x, y, c, _ = _place()
        me = 4 * x + 2 * y + c
        slots[me] = in_ref[...]
        sent = []
        for p in range(1, 8):
            to = (x ^ (p >> 2), y ^ ((p >> 1) & 1), c ^ (p & 1))
            sent.append(_remote(in_ref, slots.at[me], send_sems, recv_sems, p, to))
        for cp in sent:
            cp.start()
        for p in range(1, 8):
            frm = 4 * (x ^ (p >> 2)) + 2 * (y ^ ((p >> 1) & 1)) + (c ^ (p & 1))
            _remote(in_ref, slots.at[frm], send_sems, recv_sems, p, (x, y, c)).wait_recv()
        for cp in sent:
            cp.wait_send()
        acc = slots[0]
        for d in range(1, 8):
            acc = acc + slots[d]
        out_ref[...] = acc

    vm = pl.BlockSpec(memory_space=pltpu.VMEM)
    return pl.pallas_call(
        body, name="allreduce_small", in_specs=[vm], out_specs=vm,
        out_shape=jax.ShapeDtypeStruct(buf.shape, F32),
        scratch_shapes=[pltpu.VMEM((8, rows, 128), F32), pltpu.SemaphoreType.DMA((8,)), pltpu.SemaphoreType.DMA((8,))],
    )(buf)


def _pair_exchange(gs, name):
    n = len(gs)
    halves = [g.shape[1] // 2 for g in gs]

    def body(*refs):
        g, got = refs[:n], refs[n:2 * n]
        send_sems, recv_sems = refs[2 * n:]
        x, y, c, _ = _place()
        cps = [_remote(g[w].at[:, pl.ds((1 - c) * halves[w], halves[w]), :], got[w], send_sems, recv_sems, w,
                       (x, y, 1 - c)) for w in range(n)]
        for cp in cps:
            cp.start()
        for cp in cps:
            cp.wait()

    return pl.pallas_call(
        body, name=name, in_specs=[ANY] * n, out_specs=[ANY] * n,
        out_shape=[jax.ShapeDtypeStruct((N_CHIPS, h, g.shape[2]), g.dtype) for g, h in zip(gs, halves)],
        scratch_shapes=[pltpu.SemaphoreType.DMA((n,)), pltpu.SemaphoreType.DMA((n,))],
    )(*gs)


def _core_id():
    return lax.axis_index("c").reshape(1).astype(jnp.int32)


def _pair_sum(g, got, name):
    h, C = got.shape[1:]

    def body(c_ref, g_ref, b_ref, o_ref):
        o_ref[...] = (g_ref[...].astype(F32) + b_ref[...].astype(F32)).astype(BF16)

    blk = pl.BlockSpec((1, h, C), lambda k, c_ref: (k, 0, 0))
    return pl.pallas_call(
        body, name=name,
        grid_spec=pltpu.PrefetchScalarGridSpec(
            num_scalar_prefetch=1, grid=(N_CHIPS,),
            in_specs=[pl.BlockSpec((1, h, C), lambda k, c_ref: (k, c_ref[0], 0)), blk], out_specs=blk),
        out_shape=jax.ShapeDtypeStruct(got.shape, BF16), compiler_params=_cp("arbitrary"))(_core_id(), g, got)


def _chip_exchange(ps):
    n = len(ps)

    def body(*refs):
        p, landed = refs[:n], refs[n:2 * n]
        send_sems, recv_sems = refs[2 * n:]
        x, y, c, chips = _place()
        cps = [_remote(p[w].at[2 * chip[0] + chip[1]], landed[w].at[j], send_sems, recv_sems, w * 3 + j, (*chip, c))
               for w in range(n) for j, chip in enumerate(chips)]
        for cp in cps:
            cp.start()
        for cp in cps:
            cp.wait()

    return pl.pallas_call(
        body, name="chip_exchange", in_specs=[ANY] * n, out_specs=[ANY] * n,
        out_shape=[jax.ShapeDtypeStruct((3,) + a.shape[1:], BF16) for a in ps],
        scratch_shapes=[pltpu.SemaphoreType.DMA((3 * n,)), pltpu.SemaphoreType.DMA((3 * n,))],
    )(*ps)


def _sum_partials(g, got, landed, name):
    h, C = got.shape[1:]

    def body(ids, g_ref, b_ref, l_ref, o_ref):
        acc = g_ref[0].astype(F32) + b_ref[0].astype(F32)
        for j in range(3):
            acc = acc + l_ref[j].astype(F32)
        o_ref[...] = acc

    ids = jnp.stack([2 * lax.axis_index("x") + lax.axis_index("y"), lax.axis_index("c")]).astype(jnp.int32)
    return pl.pallas_call(
        body, name=name,
        grid_spec=pltpu.PrefetchScalarGridSpec(
            num_scalar_prefetch=1, grid=(1,),
            in_specs=[pl.BlockSpec((1, h, C), lambda i, ids: (ids[0], ids[1], 0)),
                      pl.BlockSpec((1, h, C), lambda i, ids: (ids[0], 0, 0)),
                      pl.BlockSpec((3, h, C), lambda i, ids: (0, 0, 0))],
            out_specs=pl.BlockSpec((h, C), lambda i, ids: (ids[1], 0))),
        out_shape=jax.ShapeDtypeStruct((2 * h, C), F32), compiler_params=_cp("arbitrary"))(ids, g, got, landed)


def _pair_share(reds):
    n = len(reds)

    def body(*refs):
        out = refs[n:2 * n]
        send_sems, recv_sems = refs[2 * n:]
        x, y, c, _ = _place()
        def half(w, which):
            h = out[w].shape[0] // 2
            return out[w].at[pl.ds(which * h, h), :]

        cps = [_remote(half(w, c), half(w, c), send_sems, recv_sems, w, (x, y, 1 - c)) for w in range(n)]
        for cp in cps:
            cp.start()
        for w in range(n):
            _remote(half(w, 1 - c), half(w, 1 - c), send_sems, recv_sems, w, (x, y, 1 - c)).wait_recv()
        for cp in cps:
            cp.wait_send()

    return pl.pallas_call(
        body, name="pair_share", in_specs=[ANY] * n, out_specs=[ANY] * n,
        out_shape=[jax.ShapeDtypeStruct(r.shape, F32) for r in reds],
        input_output_aliases={w: w for w in range(n)},
        scratch_shapes=[pltpu.SemaphoreType.DMA((n,)), pltpu.SemaphoreType.DMA((n,))],
    )(*reds)


HBM = pl.BlockSpec(memory_space=pltpu.HBM)
SEM = pl.BlockSpec(memory_space=pltpu.SEMAPHORE)
DATAFLOW = pltpu.SideEffectType.DATAFLOW_SIDE_EFFECTING


def _copies_start(name, srcs, lands, plan, n_copies, after):
    n, na = len(srcs), len(after)

    def body(*refs):
        src_refs, land_refs = refs[:n], refs[n:2 * n]
        send_sems, recv_sems = refs[2 * n + na:2 * n + na + 2]
        token = refs[-1]
        for k, (src, there, _, to) in enumerate(plan(src_refs, land_refs)):
            _remote(src, there, send_sems, recv_sems, k, to).start()
        token[...] = jnp.zeros_like(token)

    hbm = lambda a: pltpu.HBM(a.shape, a.dtype)
    outs = pl.pallas_call(
        body, name=name,
        out_shape=(pltpu.SemaphoreType.DMA((n_copies,)), pltpu.SemaphoreType.DMA((n_copies,)),
                   *[hbm(a) for a in srcs], *[hbm(a) for a in lands], jax.ShapeDtypeStruct((8, 128), F32)),
        in_specs=[HBM] * (2 * n) + [ANY] * na,
        out_specs=(SEM, SEM, *[HBM] * (2 * n), pl.BlockSpec(memory_space=pltpu.VMEM)),
        input_output_aliases={i: 2 + i for i in range(2 * n)},
        compiler_params=pltpu.CompilerParams(has_side_effects=DATAFLOW),
    )(*[pltpu.with_memory_space_constraint(a, pltpu.HBM) for a in (*srcs, *lands)], *after)
    return outs[0], outs[1], outs[2:2 + n], outs[2 + n:2 + 2 * n], outs[-1]


def _copies_wait(name, send_sems, recv_sems, srcs, lands, plan, after):
    n, na = len(srcs), len(after)

    def body(*refs):
        src_refs, land_refs = refs[:n], refs[n:2 * n]
        send_sems, recv_sems = refs[2 * n:2 * n + 2]
        for k, (src, _, here, to) in enumerate(plan(src_refs, land_refs)):
            cp = _remote(src, here, send_sems, recv_sems, k, to)
            cp.wait_send()
            cp.wait_recv()

    hbm = lambda a: pltpu.HBM(a.shape, a.dtype)
    outs = pl.pallas_call(
        body, name=name,
        out_shape=(*[hbm(a) for a in srcs], *[hbm(a) for a in lands]),
        in_specs=[HBM] * (2 * n) + [SEM, SEM] + [ANY] * na,
        out_specs=tuple([HBM] * (2 * n)),
        input_output_aliases={i: i for i in range(2 * n)},
        compiler_params=pltpu.CompilerParams(has_side_effects=DATAFLOW),
    )(*srcs, *lands, send_sems, recv_sems, *after)
    return outs[n:]


def _gather_plan(halves):
    def plan(shards, lands):
        x, y, c, chips = _place()
        me = 2 * x + y
        copies = []
        for w, h in enumerate(halves):
            rows = pl.ds(c * h, h)
            for chip in chips:
                copies.append((shards[w].at[rows, :], lands[w].at[me, rows, :],
                               lands[w].at[2 * chip[0] + chip[1], rows, :], (*chip, c)))
        return copies
    return plan


def _reduce_plan(n):
    def plan(ps, lands):
        x, y, c, chips = _place()
        return [(ps[w].at[2 * chip[0] + chip[1]], lands[w].at[j], lands[w].at[j], (*chip, c))
                for w in range(n) for j, chip in enumerate(chips)]
    return plan


def _pair_forward(lands, shards):
    n = len(lands)

    def body(*refs):
        out = refs[n:2 * n]
        send_sems, recv_sems = refs[2 * n:]
        x, y, c, chips = _place()

        def part(w, chip, half):
            h = out[w].shape[1] // 2
            return out[w].at[2 * chip[0] + chip[1], pl.ds(half * h, h), :]

        cps = [_remote(part(w, chip, c), part(w, chip, c), send_sems, recv_sems, w * 3 + j, (x, y, 1 - c))
               for w in range(n) for j, chip in enumerate(chips)]
        for cp in cps:
            cp.start()
        for w in range(n):
            for j, chip in enumerate(chips):
                _remote(part(w, chip, 1 - c), part(w, chip, 1 - c), send_sems, recv_sems, w * 3 + j,
                        (x, y, 1 - c)).wait_recv()
        for cp in cps:
            cp.wait_send()

    outs = pl.pallas_call(
        body, name="pair_forward", in_specs=[ANY] * n, out_specs=[ANY] * n,
        out_shape=[jax.ShapeDtypeStruct(a.shape, a.dtype) for a in lands],
        input_output_aliases={w: w for w in range(n)},
        scratch_shapes=[pltpu.SemaphoreType.DMA((3 * n,)), pltpu.SemaphoreType.DMA((3 * n,))],
    )(*lands)
    chip = 2 * lax.axis_index("x") + lax.axis_index("y")
    return [lax.dynamic_update_slice(o, s[None], (chip, 0, 0)) for o, s in zip(outs, shards)]


def _adamw(w, g, m, v, name, tr=64):
    R, C = w.shape
    tr = min(tr, R)

    def body(w_ref, g_ref, m_ref, v_ref, d_ref, nm_ref, nv_ref):
        gv = g_ref[...]
        nm = ADAM_B1 * m_ref[...] + (1.0 - ADAM_B1) * gv
        nv = ADAM_B2 * v_ref[...] + (1.0 - ADAM_B2) * (gv * gv)
        m_hat = nm / (1.0 - ADAM_B1 ** ADAM_STEP)
        v_hat = nv / (1.0 - ADAM_B2 ** ADAM_STEP)
        d_ref[...] = -ADAM_LR * (m_hat / (jnp.sqrt(v_hat) + ADAM_EPS) + ADAM_WD * w_ref[...])
        nm_ref[...] = nm
        nv_ref[...] = nv

    blk = pl.BlockSpec((tr, C), lambda i: (i, 0))
    return pl.pallas_call(body, name=name, grid=(R // tr,), in_specs=[blk] * 4, out_specs=[blk] * 3,
                          out_shape=[jax.ShapeDtypeStruct((R, C), F32)] * 3, compiler_params=_cp("arbitrary"))(w, g, m, v)


SMALL = (("norm1_g", 1024), ("attn_norm_g", 512), ("hgrn_norm_g", 512), ("hgrn_lb_logits", 1024), ("norm2_g", 1024),
         ("conv_b", D_FF), ("final_norm_g", 1024), ("conv_w", 3 * D_FF))
SMALL_ROWS = 136


def _pack(parts, rows):
    flat = jnp.concatenate([p.reshape(-1).astype(F32) for p in parts])
    return jnp.pad(flat, (0, rows * 128 - flat.shape[0])).reshape(rows, 128)


def kernel(x, norm1_g, w_in, attn_norm_g, hgrn_norm_g, hgrn_lb_logits, w_out, norm2_g, w_up, conv_w, conv_b, w_down, final_norm_g, loss_target, m_norm1_g, m_w_in, m_attn_norm_g, m_hgrn_norm_g, m_hgrn_lb_logits, m_w_out, m_norm2_g, m_w_up, m_conv_w, m_conv_b, m_w_down, m_final_norm_g, v_norm1_g, v_w_in, v_attn_norm_g, v_hgrn_norm_g, v_hgrn_lb_logits, v_w_out, v_norm2_g, v_w_up, v_conv_w, v_conv_b, v_w_down, v_final_norm_g):
    w = dict(norm1_g=norm1_g, w_in=w_in, attn_norm_g=attn_norm_g, hgrn_norm_g=hgrn_norm_g,
             hgrn_lb_logits=hgrn_lb_logits, w_out=w_out, norm2_g=norm2_g, w_up=w_up, conv_w=conv_w, conv_b=conv_b,
             w_down=w_down, final_norm_g=final_norm_g)
    m = dict(norm1_g=m_norm1_g, w_in=m_w_in, attn_norm_g=m_attn_norm_g, hgrn_norm_g=m_hgrn_norm_g,
             hgrn_lb_logits=m_hgrn_lb_logits, w_out=m_w_out, norm2_g=m_norm2_g, w_up=m_w_up, conv_w=m_conv_w,
             conv_b=m_conv_b, w_down=m_w_down, final_norm_g=m_final_norm_g)
    v = dict(norm1_g=v_norm1_g, w_in=v_w_in, attn_norm_g=v_attn_norm_g, hgrn_norm_g=v_hgrn_norm_g,
             hgrn_lb_logits=v_hgrn_lb_logits, w_out=v_w_out, norm2_g=v_norm2_g, w_up=v_w_up, conv_w=v_conv_w,
             conv_b=v_conv_b, w_down=v_w_down, final_norm_g=v_final_norm_g)
    names = list(w)
    chip = 2 * lax.axis_index("x") + lax.axis_index("y")

    shards = {k: w[k][0].astype(BF16) for k in BIG}
    w_in4, conv_w4 = _gather_weights([shards["w_in"]], conv_w[0])
    conv_w_full = jnp.transpose(conv_w4, (1, 0, 2)).reshape(3, D_FF)
    lb = jax.nn.softmax(hgrn_lb_logits, axis=0)[0:1]
    late = [shards[k] for k in BIG[1:]]
    gather_plan = _gather_plan([s.shape[0] // 2 for s in late])
    started = _copies_start("gather_start", late, [lax.empty((N_CHIPS,) + s.shape, BF16) for s in late], gather_plan,
                            3 * len(late), after=(w_in4,))
    a = _step_mixers(x[0], norm1_g + started[4][0:1, 0:1], w_in4, lb)
    landed_w = _copies_wait("gather_wait", *started[:4], gather_plan, after=(a["attn_o"], a["rec_o"]))
    w_out4, w_up4, w_down4 = _pair_forward(landed_w, late)

    b = _step_channel(a, x[0], loss_target[0], attn_norm_g, hgrn_norm_g, w_out4.reshape(D_MODEL, D_MODEL), norm2_g,
                      w_up4, conv_w_full, conv_b, w_down4.reshape(D_FF, D_MODEL), final_norm_g.reshape(1, D_MODEL))

    early = [b["dw_out"], b["dw_up"], b["dw_down"]]
    gots = _pair_exchange(early, "pair_exchange")
    ps = [_pair_sum(gk, got, f"pair_sum_{k}") for gk, got, k in zip(early, gots, BIG[1:])]
    reduce_plan = _reduce_plan(len(ps))
    started = _copies_start("reduce_start", ps, [lax.empty((3,) + p.shape[1:], BF16) for p in ps], reduce_plan,
                            3 * len(ps), after=())
    c = _step_mixers_bwd(a, b, x[0], norm1_g, w_in4, lb + started[4][0:1, 0:1], token=started[4])
    landed = _copies_wait("reduce_wait", *started[:4], reduce_plan, after=(c["dw_in"],))
    reds = [_sum_partials(gk, got, l, f"sum_partials_{k}") for gk, got, l, k in zip(early, gots, landed, BIG[1:])]
    gots_in = _pair_exchange([c["dw_in"]], "pair_exchange_w_in")
    ps_in = _pair_sum(c["dw_in"], gots_in[0], "pair_sum_w_in")
    landed_in = _chip_exchange([ps_in])
    reds.insert(0, _sum_partials(c["dw_in"], gots_in[0], landed_in[0], "sum_partials_w_in"))

    loss, dx = b["loss"], c["dx"]
    small = dict(g1=c["dg1"], g_a=b["dga"], g_h=b["dgh"], lb=c["dlb"], g2=b["dg2"], conv_w=b["dcw"], conv_b=b["dcb"],
                 gf=b["dgf"])
    dlb = small["lb"] * lb * (1.0 - lb)
    grads_small = dict(norm1_g=small["g1"], attn_norm_g=small["g_a"], hgrn_norm_g=small["g_h"],
                       hgrn_lb_logits=jnp.concatenate([dlb, -dlb], axis=0), norm2_g=small["g2"],
                       conv_b=small["conv_b"], final_norm_g=small["gf"], conv_w=small["conv_w"])
    summed = _allreduce_small(_pack([grads_small[k] for k, _ in SMALL] + [loss[0, 0:1]], SMALL_ROWS)).reshape(-1)
    g = {}
    off = 0
    for k, size in SMALL:
        g[k] = summed[off:off + size]
        off += size
    loss_total = summed[off]
    g["conv_w"] = lax.dynamic_slice(g["conv_w"].reshape(3, D_FF), (0, chip * (D_FF // N_CHIPS)), (3, D_FF // N_CHIPS))

    for k, shard in zip(BIG, _pair_share(reds)):
        g[k] = shard

    delta, new_m, new_v = {}, {}, {}
    for k in BIG:
        delta[k], new_m[k], new_v[k] = (a[None] for a in _adamw(w[k][0], g[k], m[k][0], v[k][0], f"adamw_{k}"))
        g[k] = g[k][None]
    small_names = [k for k in names if k not in BIG]
    rows = 80
    packed = _adamw(_pack([w[k] for k in small_names], rows), _pack([g[k] for k in small_names], rows),
                    _pack([m[k] for k in small_names], rows), _pack([v[k] for k in small_names], rows), "adamw_small", tr=rows)
    flat = [a.reshape(-1) for a in packed]
    off = 0
    for k in small_names:
        size = w[k].size
        delta[k], new_m[k], new_v[k] = (a[off:off + size].reshape(w[k].shape) for a in flat)
        g[k] = g[k].reshape(w[k].shape)
        off += size

    return (loss_total, dx[None], *[g[k] for k in names], *[delta[k] for k in names],
            *[new_m[k] for k in names], *[new_v[k] for k in names])
```

```python
import functools
import math

import jax
import jax.numpy as jnp
from jax import lax
from jax.experimental import pallas as pl
from jax.experimental.pallas import tpu as pltpu

F32 = jnp.float32
BF16 = jnp.bfloat16

D_MODEL = 1024
ATTN_W = 512
HGRN_W = 512
HEAD_PAIR = 128
ATTN_BLK = 128
DILATIONS = (1, 4, 16)
ATTN_CHAINS = 4
HGRN_HEADS = 4
HGRN_DIM = 128
HGRN_CHUNK = 64
SUPER = 256
D_FF = 2816
N_CHIPS = 4
IN_TOTAL = 3584
IN_SHARD = IN_TOTAL // N_CHIPS
UP_SHARD = 2 * D_FF // N_CHIPS
QKV_W = 3 * ATTN_W
HG_W = 4 * HGRN_W
EPS = 1e-6
NEG = -1e30
V7X_VMEM_BYTES = 64 * 1024 * 1024
VMEM_LIMIT = V7X_VMEM_BYTES - 8 * 1024 * 1024

ADAM_LR = 0.001
ADAM_B1 = 0.9
ADAM_B2 = 0.999
ADAM_EPS = 1e-08
ADAM_WD = 0.01
ADAM_STEP = 10

MESH = pl.DeviceIdType.MESH


def _cp(*sem):
    return pltpu.CompilerParams(dimension_semantics=sem or None, vmem_limit_bytes=VMEM_LIMIT)


def _dot(a, b):
    return jnp.dot(a, b, preferred_element_type=F32)


def _dot_nt(a, b):
    return lax.dot_general(a, b, (((1,), (1,)), ((), ())), preferred_element_type=F32)


def _dot_tn(a, b):
    return lax.dot_general(a, b, (((0,), (0,)), ((), ())), preferred_element_type=F32)


def _sigmoid(x):
    return 1.0 / (1.0 + jnp.exp(-x))


def _rms(x, width):
    return lax.rsqrt(jnp.sum(x * x, axis=-1, keepdims=True) * (1.0 / width) + EPS)


def _rms_bwd(dn, n, r, width):
    return r * (dn - n * (jnp.sum(dn * n, axis=-1, keepdims=True) * (1.0 / width)))


def _colsum(x):
    return jnp.sum(x, axis=0, keepdims=True)


def _row(v, k):
    rid = lax.broadcasted_iota(jnp.int32, v.shape, 0)
    return jnp.sum(jnp.where(rid == k, v, 0.0), axis=0, keepdims=True)


def _full(shape):
    return pl.BlockSpec(shape, lambda *_: (0,) * len(shape))


def _once(shape):
    return pl.BlockSpec(shape, lambda *_: (0,) * len(shape), pipeline_mode=pl.Buffered(1))


def _in_proj(x, g1, w_in4, tm=512):
    T = x.shape[0]

    def body(x_ref, g_ref, w_ref, u_ref, qkv_ref, hg_ref):
        xv = x_ref[...]
        u = (xv * _rms(xv, D_MODEL) * g_ref[...]).astype(BF16)
        u_ref[...] = u
        p0 = _dot(u, w_ref[0])
        p1 = _dot(u, w_ref[1])
        qkv_ref[:, 0:IN_SHARD] = p0
        qkv_ref[:, IN_SHARD:QKV_W] = p1[:, :QKV_W - IN_SHARD]
        hg_ref[:, 0:2 * IN_SHARD - QKV_W] = p1[:, QKV_W - IN_SHARD:]
        hg_ref[:, 2 * IN_SHARD - QKV_W:3 * IN_SHARD - QKV_W] = _dot(u, w_ref[2])
        hg_ref[:, 3 * IN_SHARD - QKV_W:HG_W] = _dot(u, w_ref[3])

    return pl.pallas_call(
        body, name="in_proj", grid=(T // tm,),
        in_specs=[pl.BlockSpec((tm, D_MODEL), lambda i: (i, 0)), _full((1, D_MODEL)),
                  _once((N_CHIPS, D_MODEL, IN_SHARD))],
        out_specs=[pl.BlockSpec((tm, D_MODEL), lambda i: (i, 0)), pl.BlockSpec((tm, QKV_W), lambda i: (i, 0)),
                   pl.BlockSpec((tm, HG_W), lambda i: (i, 0))],
        out_shape=[jax.ShapeDtypeStruct((T, D_MODEL), BF16), jax.ShapeDtypeStruct((T, QKV_W), F32),
                   jax.ShapeDtypeStruct((T, HG_W), F32)],
        compiler_params=_cp("arbitrary"),
    )(x, g1, w_in4)


def _attn_masks(bias_ref):
    lane = lax.broadcasted_iota(jnp.int32, (ATTN_BLK, HEAD_PAIR), 1)
    row = lax.broadcasted_iota(jnp.int32, (2 * ATTN_BLK, 2 * ATTN_BLK), 0)
    col = lax.broadcasted_iota(jnp.int32, (2 * ATTN_BLK, 2 * ATTN_BLK), 1)
    base = jnp.where(row >= ATTN_BLK, row - ATTN_BLK, row) - col
    for k in range(2):
        dist = base + k * ATTN_BLK
        bias_ref[k] = jnp.where((dist >= 0) & (dist <= ATTN_BLK), 0.0, NEG)
    return lane < 64


def _two_heads(blk, first):
    zero = jnp.zeros_like(blk)
    return jnp.concatenate([jnp.where(first, blk, zero), jnp.where(first, zero, blk)], axis=0)


def _attn_rows(idx, nb, d):
    r, n = idx // nb, idx % nb
    kb = jnp.maximum(n - 1, 0)
    if d == 1:
        q0 = pl.multiple_of(n * ATTN_BLK, ATTN_BLK)
        k0 = pl.multiple_of(kb * ATTN_BLK, ATTN_BLK)
        return pl.ds(q0, ATTN_BLK), pl.ds(k0, 2 * ATTN_BLK), n - kb
    return (pl.ds(r + d * ATTN_BLK * n, ATTN_BLK, stride=d), pl.ds(r + d * ATTN_BLK * kb, 2 * ATTN_BLK, stride=d),
            n - kb)


def _attn_fwd(qkv):
    T = qkv.shape[0]

    per_chain = T // ATTN_BLK // ATTN_CHAINS

    def body(q_ref, k_ref, v_ref, o_ref, m_ref, l_ref, bias_ref):
        first = _attn_masks(bias_ref)
        for bi, d in enumerate(DILATIONS):
            nb = T // d // ATTN_BLK

            def block(idx, d=d, nb=nb, bi=bi):
                rows, keys, which = _attn_rows(idx, nb, d)
                q2 = _two_heads(q_ref[rows, :] * 0.125, first).astype(BF16)
                kw = k_ref[keys, :].astype(BF16)
                vw = v_ref[keys, :].astype(BF16)
                old = (o_ref[rows, :], m_ref[rows, :], l_ref[rows, :]) if bi else None
                s = _dot_nt(q2, kw) + bias_ref[which]
                mb = jnp.max(s, axis=-1, keepdims=True)
                p = jnp.exp(s - mb)
                lb = jnp.sum(p, axis=-1, keepdims=True)
                o2 = _dot(p.astype(BF16), vw)
                o = jnp.where(first, o2[:ATTN_BLK], o2[ATTN_BLK:])
                m = jnp.where(first, mb[:ATTN_BLK], mb[ATTN_BLK:])
                l = jnp.where(first, lb[:ATTN_BLK], lb[ATTN_BLK:])
                if bi:
                    po, pm, pl_ = old
                    mn = jnp.maximum(pm, m)
                    wa = jnp.exp(pm - mn)
                    wb = jnp.exp(m - mn)
                    o, l, m = po * wa + o * wb, pl_ * wa + l * wb, mn
                return rows, o, m, l

            def step(i, carry, block=block):
                done = [block(i + ch * per_chain) for ch in range(ATTN_CHAINS)]
                for rows, o, m, l in done:
                    o_ref[rows, :] = o
                    m_ref[rows, :] = m
                    l_ref[rows, :] = l
                return carry

            lax.fori_loop(0, per_chain, step, 0)

        def finish(i, carry):
            rows = pl.ds(pl.multiple_of(i * SUPER, SUPER), SUPER)
            l = l_ref[rows, :]
            o_ref[rows, :] = o_ref[rows, :] / l
            m_ref[rows, :] = m_ref[rows, :] + jnp.log(l)
            return carry

        lax.fori_loop(0, T // SUPER, finish, 0)

    col = lambda off: pl.BlockSpec((T, HEAD_PAIR), lambda j: (0, off + j))
    return pl.pallas_call(
        body, name="attn_fwd", grid=(4,),
        in_specs=[col(0), col(4), col(8)], out_specs=[col(0), col(0)],
        out_shape=[jax.ShapeDtypeStruct((T, ATTN_W), F32)] * 2,
        scratch_shapes=[pltpu.VMEM((T, HEAD_PAIR), F32), pltpu.VMEM((2, 2 * ATTN_BLK, 2 * ATTN_BLK), F32)],
        compiler_params=_cp("arbitrary"),
    )(qkv, qkv, qkv)


def _attn_bwd(qkv, o, lse, do, token=None):
    T = qkv.shape[0]
    per_chain = T // ATTN_BLK // ATTN_CHAINS
    extra = [] if token is None else [token]

    def body(q_ref, k_ref, v_ref, o_ref, lse_ref, do_ref, *rest):
        dq_ref, dk_ref, dv_ref, bias_ref = rest[len(extra):]
        first = _attn_masks(bias_ref)
        dq_ref[...] = jnp.zeros_like(dq_ref)
        dk_ref[...] = jnp.zeros_like(dk_ref)
        dv_ref[...] = jnp.zeros_like(dv_ref)
        for d in DILATIONS:
            nb = T // d // ATTN_BLK

            def block(idx, d=d, nb=nb):
                rows, keys, which = _attn_rows(idx, nb, d)
                q2 = _two_heads(q_ref[rows, :] * 0.125, first).astype(BF16)
                kw = k_ref[keys, :].astype(BF16)
                vw = v_ref[keys, :].astype(BF16)
                lse_b = lse_ref[rows, :]
                dob = do_ref[rows, :]
                prod = dob * o_ref[rows, :]
                old = dq_ref[rows, :], dk_ref[keys, :], dv_ref[keys, :]
                lse2 = jnp.concatenate(
                    [jnp.max(jnp.where(first, lse_b, NEG), axis=-1, keepdims=True),
                     jnp.max(jnp.where(first, NEG, lse_b), axis=-1, keepdims=True)], axis=0)
                p = jnp.exp(_dot_nt(q2, kw) + (bias_ref[which] - lse2))
                delta = jnp.concatenate(
                    [jnp.sum(jnp.where(first, prod, 0.0), axis=-1, keepdims=True),
                     jnp.sum(jnp.where(first, 0.0, prod), axis=-1, keepdims=True)], axis=0)
                do2 = _two_heads(dob, first).astype(BF16)
                ds = (p * (_dot_nt(do2, vw) - delta)).astype(BF16)
                dq2 = _dot(ds, kw) * 0.125
                return (rows, keys, old[0] + jnp.where(first, dq2[:ATTN_BLK], dq2[ATTN_BLK:]),
                        old[1] + _dot_tn(ds, q2), old[2] + _dot_tn(p.astype(BF16), do2))

            def step(i, carry, block=block):
                done = [block(i + ch * per_chain) for ch in range(ATTN_CHAINS)]
                for rows, keys, dq, dk, dv in done:
                    dq_ref[rows, :] = dq
                    dk_ref[keys, :] = dk
                    dv_ref[keys, :] = dv
                return carry

            lax.fori_loop(0, per_chain, step, 0)

    col = lambda off: pl.BlockSpec((T, HEAD_PAIR), lambda j: (0, off + j))
    return pl.pallas_call(
        body, name="attn_bwd", grid=(4,),
        in_specs=[col(0), col(4), col(8), col(0), col(0), col(0)] + [_full(t.shape) for t in extra],
        out_specs=[col(0)] * 3,
        out_shape=[jax.ShapeDtypeStruct((T, ATTN_W), F32)] * 3,
        scratch_shapes=[pltpu.VMEM((2, 2 * ATTN_BLK, 2 * ATTN_BLK), F32)],
        compiler_params=_cp("arbitrary"),
    )(qkv, qkv, qkv, o, lse, do, *extra)


def _chunk_ids():
    row = lax.broadcasted_iota(jnp.int32, (SUPER, HGRN_DIM), 0)
    r2 = lax.broadcasted_iota(jnp.int32, (SUPER, SUPER), 0)
    c2 = lax.broadcasted_iota(jnp.int32, (SUPER, SUPER), 1)
    amask = ((r2 // HGRN_CHUNK) == (c2 // HGRN_CHUNK)) & (c2 <= r2)
    return row % HGRN_CHUNK, row // HGRN_CHUNK, amask


def _cumsum_chunk(x, rmod):
    s = 1
    while s < HGRN_CHUNK:
        x = x + jnp.where(rmod >= s, pltpu.roll(x, s, 0), 0.0)
        s *= 2
    return x


def _suffix_sum_chunk(x, rmod):
    s = 1
    while s < HGRN_CHUNK:
        x = x + jnp.where(rmod < HGRN_CHUNK - s, pltpu.roll(x, SUPER - s, 0), 0.0)
        s *= 2
    return x


def _chunk_rows(vs, cid):
    out = vs[-1]
    for c in reversed(range(len(vs) - 1)):
        out = jnp.where(cid == c, vs[c], out)
    return out


def _expand(x, cid):
    return jnp.concatenate([jnp.where(cid == c, x, 0.0) for c in range(SUPER // HGRN_CHUNK)], axis=1)


def _hgrn_gates(q, f, lbv, rmod, cid, tmp):
    sq = _sigmoid(q)
    sg = _sigmoid(f)
    forget = lbv + (1.0 - lbv) * sg
    key = 1.0 - forget
    b = _cumsum_chunk(jnp.log(forget), rmod)
    tmp[...] = b
    bends = [tmp[c * HGRN_CHUNK + HGRN_CHUNK - 1:(c + 1) * HGRN_CHUNK, :] for c in range(SUPER // HGRN_CHUNK)]
    eb = jnp.exp(b)
    enb = jnp.exp(-b)
    ebe = jnp.exp(_chunk_rows(bends, cid) - b)
    return sq, sg, forget, key, eb, enb, ebe, q * sq * eb, key * enb, key * ebe, [jnp.exp(v) for v in bends]


def _hgrn_fwd(hg, lb):
    T = hg.shape[0]
    nsc = T // SUPER
    NC = SUPER // HGRN_CHUNK

    def body(q_ref, f_ref, i_ref, lb_ref, o_ref, st_ref, state, tmp):
        rmod, cid, amask = _chunk_ids()
        state[...] = jnp.zeros_like(state)
        lbv = lb_ref[...]

        def step(sc, carry):
            rows = pl.ds(pl.multiple_of(sc * SUPER, SUPER), SUPER)
            iv = i_ref[rows, :].astype(BF16)
            qd, ki, ke, dec = _hgrn_gates(q_ref[rows, :], f_ref[rows, :], lbv, rmod, cid, tmp)[-4:]
            a = jnp.where(amask, _dot_nt(qd.astype(BF16), ki.astype(BF16)), 0.0)
            o = _dot(a.astype(BF16), iv)
            ut = _dot_tn(iv, _expand(ke, cid).astype(BF16))
            st = state[...]
            st_ref[0, sc] = st
            sts = []
            for c in range(NC):
                sts.append(st)
                st = st * dec[c] + ut[:, c * HGRN_DIM:(c + 1) * HGRN_DIM]
            state[...] = st
            o = o + _dot_nt(_expand(qd, cid).astype(BF16), jnp.concatenate(sts, axis=1).astype(BF16))
            o_ref[rows, :] = o
            return carry

        lax.fori_loop(0, nsc, step, 0)

    col = lambda off: pl.BlockSpec((T, HGRN_DIM), lambda h: (0, off + h))
    return pl.pallas_call(
        body, name="hgrn_fwd", grid=(HGRN_HEADS,),
        in_specs=[col(0), col(4), col(8), pl.BlockSpec((1, HGRN_DIM), lambda h: (0, h))],
        out_specs=[pl.BlockSpec((T, HGRN_DIM), lambda h: (0, h)),
                   pl.BlockSpec((1, nsc, HGRN_DIM, HGRN_DIM), lambda h: (h, 0, 0, 0))],
        out_shape=[jax.ShapeDtypeStruct((T, HGRN_W), F32),
                   jax.ShapeDtypeStruct((HGRN_HEADS, nsc, HGRN_DIM, HGRN_DIM), F32)],
        scratch_shapes=[pltpu.VMEM((HGRN_DIM, HGRN_DIM), F32), pltpu.VMEM((SUPER, HGRN_DIM), F32)],
        compiler_params=_cp("arbitrary"),
    )(hg, hg, hg, lb)


def _hgrn_bwd(hg, lb, states, do):
    T = hg.shape[0]
    nsc = T // SUPER
    NC = SUPER // HGRN_CHUNK

    def body(q_ref, f_ref, i_ref, lb_ref, st_ref, do_ref, dq_ref, df_ref, di_ref, dlb_ref, dstate, tmp):
        rmod, cid, amask = _chunk_ids()
        dstate[...] = jnp.zeros_like(dstate)
        dlb_ref[...] = jnp.zeros_like(dlb_ref)
        lbv = lb_ref[...]

        def step(k, carry):
            sc = nsc - 1 - k
            rows = pl.ds(pl.multiple_of(sc * SUPER, SUPER), SUPER)
            q = q_ref[rows, :]
            ivf = i_ref[rows, :]
            iv = ivf.astype(BF16)
            dof = do_ref[rows, :]
            dob = dof.astype(BF16)
            sq, sg, forget, key, eb, enb, ebe, qd, ki, ke, dec = _hgrn_gates(q, f_ref[rows, :], lbv, rmod, cid, tmp)
            qdb, kib = qd.astype(BF16), ki.astype(BF16)
            keexp = _expand(ke, cid).astype(BF16)
            a = jnp.where(amask, _dot_nt(qdb, kib), 0.0).astype(BF16)
            ut = _dot_tn(iv, keexp)
            st = st_ref[0, sc]
            sts = []
            for c in range(NC):
                sts.append(st)
                st = st * dec[c] + ut[:, c * HGRN_DIM:(c + 1) * HGRN_DIM]
            gt = _dot_tn(dob, _expand(qd, cid).astype(BF16))
            nxt = [None] * NC
            ddec = [None] * NC
            dst = dstate[...]
            for c in reversed(range(NC)):
                nxt[c] = dst
                ddec[c] = _colsum(dst * sts[c])
                dst = dst * dec[c] + gt[:, c * HGRN_DIM:(c + 1) * HGRN_DIM]
            dstate[...] = dst
            da = jnp.where(amask, _dot_nt(dob, iv), 0.0).astype(BF16)
            ncat = jnp.concatenate(nxt, axis=1).astype(BF16)
            nstack = jnp.concatenate(nxt, axis=0).astype(BF16)
            ststack = jnp.concatenate(sts, axis=0).astype(BF16)
            div = _dot_tn(a, dob) + _dot_nt(keexp, ncat)
            dke = _dot(_expand(ivf, cid).astype(BF16), nstack)
            dqd = _dot(da, kib) + _dot(_expand(dof, cid).astype(BF16), ststack)
            dki = _dot_tn(da, qdb)
            dkk = dke * ke
            dkey = dki * enb + dke * ebe
            db = dqd * qd - dki * ki - dkk
            dbends = [_colsum(jnp.where(cid == c, dkk, 0.0)) + ddec[c] * dec[c] for c in range(NC)]
            dlogf = _suffix_sum_chunk(db, rmod) + _chunk_rows(dbends, cid)
            dforget = dlogf / forget - dkey
            df_ref[rows, :] = dforget * (1.0 - lbv) * sg * (1.0 - sg)
            dlb_ref[...] += _colsum(dforget * (1.0 - sg))
            dq_ref[rows, :] = dqd * eb * (sq * (1.0 + q * (1.0 - sq)))
            di_ref[rows, :] = div
            return carry

        lax.fori_loop(0, nsc, step, 0)

    col = lambda off: pl.BlockSpec((T, HGRN_DIM), lambda h: (0, off + h))
    own = pl.BlockSpec((T, HGRN_DIM), lambda h: (0, h))
    vec = pl.BlockSpec((1, HGRN_DIM), lambda h: (0, h))
    return pl.pallas_call(
        body, name="hgrn_bwd", grid=(HGRN_HEADS,),
        in_specs=[col(0), col(4), col(8), vec,
                  pl.BlockSpec((1, nsc, HGRN_DIM, HGRN_DIM), lambda h: (h, 0, 0, 0)), own],
        out_specs=[own, own, own, vec],
        out_shape=[jax.ShapeDtypeStruct((T, HGRN_W), F32)] * 3 + [jax.ShapeDtypeStruct((1, HGRN_W), F32)],
        scratch_shapes=[pltpu.VMEM((HGRN_DIM, HGRN_DIM), F32), pltpu.VMEM((SUPER, HGRN_DIM), F32)],
        compiler_params=_cp("arbitrary"),
    )(hg, hg, hg, lb, states, do)


def _rec_heads(rec, gate, g_h):
    rr = jnp.concatenate(
        [jnp.broadcast_to(_rms(rec[:, h * HGRN_DIM:(h + 1) * HGRN_DIM], HGRN_DIM), (rec.shape[0], HGRN_DIM))
         for h in range(HGRN_HEADS)], axis=1)
    rn = rec * rr
    sg = _sigmoid(gate)
    return rr, rn, sg


def _mix_out(attn_o, rec_o, hg, x, g_a, g_h, w_out, tm=512):
    T = x.shape[0]

    def body(a_ref, r_ref, gt_ref, x_ref, ga_ref, gh_ref, w_ref, h1_ref, mixed_ref):
        a = a_ref[...]
        an = a * _rms(a, ATTN_W) * ga_ref[...]
        gate = gt_ref[...]
        _, rn, sg = _rec_heads(r_ref[...], gate, gh_ref[...])
        mixed = jnp.concatenate([an, rn * gh_ref[...] * (gate * sg)], axis=1).astype(BF16)
        mixed_ref[...] = mixed
        h1_ref[...] = x_ref[...] + _dot(mixed, w_ref[...])

    row = lambda w: pl.BlockSpec((tm, w), lambda i: (i, 0))
    return pl.pallas_call(
        body, name="mix_out", grid=(T // tm,),
        in_specs=[row(ATTN_W), row(HGRN_W), pl.BlockSpec((tm, HGRN_W), lambda i: (i, 3)), row(D_MODEL),
                  _full((1, ATTN_W)), _full((1, HGRN_W)), _once((D_MODEL, D_MODEL))],
        out_specs=[row(D_MODEL), row(D_MODEL)],
        out_shape=[jax.ShapeDtypeStruct((T, D_MODEL), F32), jax.ShapeDtypeStruct((T, D_MODEL), BF16)],
        compiler_params=_cp("arbitrary"),
    )(attn_o, rec_o, hg, x, g_a, g_h, w_out)


_INV_SQRT2 = 1.0 / math.sqrt(2.0)
_INV_SQRT2PI = 1.0 / math.sqrt(2.0 * math.pi)


def _gelu(x):
    return 0.5 * x * (1.0 + lax.erf(x * _INV_SQRT2))


def _gelu_grad(x):
    return 0.5 * (1.0 + lax.erf(x * _INV_SQRT2)) + x * jnp.exp(-0.5 * x * x) * _INV_SQRT2PI


def _shift_down(g, prev, rowid):
    p1 = _row(prev, prev.shape[0] - 1)
    p2 = _row(prev, prev.shape[0] - 2)
    s1 = jnp.where(rowid == 0, p1, pltpu.roll(g, 1, 0))
    s2 = jnp.where(rowid == 0, p2, jnp.where(rowid == 1, p1, pltpu.roll(g, 2, 0)))
    return s1, s2


def _mlp_fwd(h1, g2, w_up4, conv_w, conv_b, w_down, gf, tgt, tm=256):
    T = h1.shape[0]
    half = D_FF // 2

    def body(h_ref, g2_ref, wu_ref, cw_ref, cb_ref, wd_ref, gf_ref, t_ref,
             u_ref, gate_ref, val_ref, dh_ref, loss_ref, dgf_ref, carry):
        i = pl.program_id(0)

        @pl.when(i == 0)
        def _():
            carry[...] = jnp.zeros_like(carry)
            loss_ref[...] = jnp.zeros_like(loss_ref)
            dgf_ref[...] = jnp.zeros_like(dgf_ref)

        h = h_ref[...]
        u = (h * _rms(h, D_MODEL) * g2_ref[...]).astype(BF16)
        u_ref[...] = u
        rowid = lax.broadcasted_iota(jnp.int32, (tm, half), 0)
        y2 = jnp.zeros((tm, D_MODEL), F32)
        for c in range(2):
            cols = slice(c * half, (c + 1) * half)
            gb = _dot(u, wu_ref[c]).astype(BF16)
            vb = _dot(u, wu_ref[2 + c]).astype(BF16)
            gate_ref[:, cols] = gb
            val_ref[:, cols] = vb
            g = gb.astype(F32)
            s1, s2 = _shift_down(g, carry[:, cols], rowid)
            carry[:, cols] = g[tm - 8:, :]
            conv = cb_ref[:, cols] + cw_ref[0:1, cols] * s2 + cw_ref[1:2, cols] * s1 + cw_ref[2:3, cols] * g
            act = (_gelu(conv) * vb.astype(F32)).astype(BF16)
            y2 = y2 + _dot(act, wd_ref[cols, :])
        h2 = h + y2
        rf = _rms(h2, D_MODEL)
        n = h2 * rf
        gfv = gf_ref[...]
        e = n * gfv - t_ref[...]
        loss_ref[...] += jnp.sum(e * e) * (0.5 / D_MODEL)
        dy = e * (1.0 / D_MODEL)
        dgf_ref[...] += _colsum(dy * n)
        dh_ref[...] = _rms_bwd(dy * gfv, n, rf, D_MODEL)

    row = lambda w: pl.BlockSpec((tm, w), lambda i: (i, 0))
    return pl.pallas_call(
        body, name="mlp_fwd", grid=(T // tm,),
        in_specs=[row(D_MODEL), _full((1, D_MODEL)), _once((N_CHIPS, D_MODEL, UP_SHARD)), _full((3, D_FF)),
                  _full((1, D_FF)), _once((D_FF, D_MODEL)), _full((1, D_MODEL)), row(D_MODEL)],
        out_specs=[row(D_MODEL), row(D_FF), row(D_FF), row(D_MODEL), _full((1, 128)), _full((1, D_MODEL))],
        out_shape=[jax.ShapeDtypeStruct((T, D_MODEL), BF16), jax.ShapeDtypeStruct((T, D_FF), BF16),
                   jax.ShapeDtypeStruct((T, D_FF), BF16), jax.ShapeDtypeStruct((T, D_MODEL), F32),
                   jax.ShapeDtypeStruct((1, 128), F32), jax.ShapeDtypeStruct((1, D_MODEL), F32)],
        scratch_shapes=[pltpu.VMEM((8, D_FF), F32)],
        compiler_params=_cp("arbitrary"),
    )(h1, g2, w_up4, conv_w, conv_b, w_down, gf, tgt)


def _mlp_bwd(dh2, gate, val, conv_w, conv_b, w_down, tm=256):
    T = dh2.shape[0]
    nb = T // tm
    half = D_FF // 2

    def body(dh_ref, gate_ref, halo_ref, val_ref, cw_ref, cb_ref, wd_ref,
             dgv_ref, act_ref, dcw_ref, dcb_ref, carry):
        i = pl.program_id(0)

        @pl.when(i == 0)
        def _():
            carry[...] = jnp.zeros_like(carry)
            dcw_ref[...] = jnp.zeros_like(dcw_ref)
            dcb_ref[...] = jnp.zeros_like(dcb_ref)

        dhb = dh_ref[...].astype(BF16)
        rowid = lax.broadcasted_iota(jnp.int32, (tm, half), 0)
        has_prev = (i < nb - 1).astype(F32)
        for c in range(2):
            cols = slice(c * half, (c + 1) * half)
            g = gate_ref[:, cols].astype(F32)
            v = val_ref[:, cols].astype(F32)
            s1, s2 = _shift_down(g, halo_ref[:, cols].astype(F32) * has_prev, rowid)
            conv = cb_ref[:, cols] + cw_ref[0:1, cols] * s2 + cw_ref[1:2, cols] * s1 + cw_ref[2:3, cols] * g
            gl = _gelu(conv)
            act_ref[:, cols] = (gl * v).astype(BF16)
            dact = _dot_nt(dhb, wd_ref[cols, :])
            dconv = dact * v * _gelu_grad(conv)
            dcb_ref[:, cols] += _colsum(dconv)
            dcw_ref[0:1, cols] += _colsum(dconv * s2)
            dcw_ref[1:2, cols] += _colsum(dconv * s1)
            dcw_ref[2:3, cols] += _colsum(dconv * g)
            nxt = carry[:, cols]
            n0, n1 = _row(nxt, 0), _row(nxt, 1)
            u1 = jnp.where(rowid == tm - 1, n0, pltpu.roll(dconv, tm - 1, 0))
            u2 = jnp.where(rowid == tm - 1, n1, jnp.where(rowid == tm - 2, n0, pltpu.roll(dconv, tm - 2, 0)))
            carry[:, cols] = dconv[0:8, :]
            dgate = cw_ref[2:3, cols] * dconv + cw_ref[1:2, cols] * u1 + cw_ref[0:1, cols] * u2
            dgv_ref[:, cols] = dgate.astype(BF16)
            dgv_ref[:, D_FF + c * half:D_FF + (c + 1) * half] = (dact * gl).astype(BF16)

    rev = lambda w: pl.BlockSpec((tm, w), lambda i: (nb - 1 - i, 0))
    halo = pl.BlockSpec((16, D_FF), lambda i: (jnp.maximum((nb - 1 - i) * (tm // 16) - 1, 0), 0))
    return pl.pallas_call(
        body, name="mlp_bwd", grid=(nb,),
        in_specs=[rev(D_MODEL), rev(D_FF), halo, rev(D_FF), _full((3, D_FF)), _full((1, D_FF)),
                  _once((D_FF, D_MODEL))],
        out_specs=[rev(2 * D_FF), rev(D_FF), _full((3, D_FF)), _full((1, D_FF))],
        out_shape=[jax.ShapeDtypeStruct((T, 2 * D_FF), BF16), jax.ShapeDtypeStruct((T, D_FF), BF16),
                   jax.ShapeDtypeStruct((3, D_FF), F32), jax.ShapeDtypeStruct((1, D_FF), F32)],
        scratch_shapes=[pltpu.VMEM((8, D_FF), F32)],
        compiler_params=_cp("arbitrary"),
    )(dh2, gate, gate, val, conv_w, conv_b, w_down)


def _up_out_bwd(dgv, w_up4, h1, g2, dh2, w_out, attn_o, rec_o, hg, g_a, g_h, tm=256):
    T = h1.shape[0]

    def body(dgv_ref, wu_ref, h_ref, g2_ref, dh2_ref, wo_ref, a_ref, r_ref, gt_ref, ga_ref, gh_ref,
             dh1_ref, dg2_ref, da_ref, dr_ref, dgt_ref, dga_ref, dgh_ref):
        @pl.when(pl.program_id(0) == 0)
        def _():
            dg2_ref[...] = jnp.zeros_like(dg2_ref)
            dga_ref[...] = jnp.zeros_like(dga_ref)
            dgh_ref[...] = jnp.zeros_like(dgh_ref)

        du = jnp.zeros((tm, D_MODEL), F32)
        for k in range(N_CHIPS):
            du = du + _dot_nt(dgv_ref[:, k * UP_SHARD:(k + 1) * UP_SHARD], wu_ref[k])
        h = h_ref[...]
        r = _rms(h, D_MODEL)
        n = h * r
        dg2_ref[...] += _colsum(du * n)
        dh1 = dh2_ref[...] + _rms_bwd(du * g2_ref[...], n, r, D_MODEL)
        dh1_ref[...] = dh1
        dmix = _dot_nt(dh1.astype(BF16), wo_ref[...])
        dan = dmix[:, :ATTN_W]
        a = a_ref[...]
        ra = _rms(a, ATTN_W)
        na = a * ra
        dga_ref[...] += _colsum(dan * na)
        da_ref[...] = _rms_bwd(dan * ga_ref[...], na, ra, ATTN_W)
        dmr = dmix[:, ATTN_W:]
        gate = gt_ref[...]
        ghv = gh_ref[...]
        rr, rn, sg = _rec_heads(r_ref[...], gate, ghv)
        dgt_ref[...] = dmr * rn * ghv * (sg * (1.0 + gate * (1.0 - sg)))
        drecn = dmr * (gate * sg)
        dgh_ref[...] += _colsum(drecn * rn)
        drn = drecn * ghv
        prod = drn * rn
        mean = jnp.concatenate(
            [jnp.broadcast_to(jnp.sum(prod[:, h_ * HGRN_DIM:(h_ + 1) * HGRN_DIM], axis=-1, keepdims=True),
                              (tm, HGRN_DIM)) for h_ in range(HGRN_HEADS)], axis=1) * (1.0 / HGRN_DIM)
        dr_ref[...] = rr * (drn - rn * mean)

    row = lambda w: pl.BlockSpec((tm, w), lambda i: (i, 0))
    return pl.pallas_call(
        body, name="up_out_bwd", grid=(T // tm,),
        in_specs=[row(2 * D_FF), _once((N_CHIPS, D_MODEL, UP_SHARD)), row(D_MODEL), _full((1, D_MODEL)),
                  row(D_MODEL), _once((D_MODEL, D_MODEL)), row(ATTN_W), row(HGRN_W),
                  pl.BlockSpec((tm, HGRN_W), lambda i: (i, 3)), _full((1, ATTN_W)), _full((1, HGRN_W))],
        out_specs=[row(D_MODEL), _full((1, D_MODEL)), row(ATTN_W), row(HGRN_W), row(HGRN_W),
                   _full((1, ATTN_W)), _full((1, HGRN_W))],
        out_shape=[jax.ShapeDtypeStruct((T, D_MODEL), F32), jax.ShapeDtypeStruct((1, D_MODEL), F32),
                   jax.ShapeDtypeStruct((T, ATTN_W), F32), jax.ShapeDtypeStruct((T, HGRN_W), F32),
                   jax.ShapeDtypeStruct((T, HGRN_W), F32), jax.ShapeDtypeStruct((1, ATTN_W), F32),
                   jax.ShapeDtypeStruct((1, HGRN_W), F32)],
        compiler_params=_cp("arbitrary"),
    )(dgv, w_up4, h1, g2, dh2, w_out, attn_o, rec_o, hg, g_a, g_h)


def _in_bwd(dqkv, dhg, w_in4, x, g1, dh1, tm=256):
    T = x.shape[0]

    def body(*refs):
        parts = refs[:7]
        w_ref, x_ref, g_ref, dh1_ref, dp_ref, dx_ref, dg_ref = refs[7:]

        @pl.when(pl.program_id(0) == 0)
        def _():
            dg_ref[...] = jnp.zeros_like(dg_ref)

        dp = jnp.concatenate([p[...] for p in parts], axis=1).astype(BF16)
        dp_ref[...] = dp
        du = jnp.zeros((tm, D_MODEL), F32)
        for k in range(N_CHIPS):
            du = du + _dot_nt(dp[:, k * IN_SHARD:(k + 1) * IN_SHARD], w_ref[k])
        xv = x_ref[...]
        r = _rms(xv, D_MODEL)
        n = xv * r
        dg_ref[...] += _colsum(du * n)
        dx_ref[...] = dh1_ref[...] + _rms_bwd(du * g_ref[...], n, r, D_MODEL)

    row = lambda w: pl.BlockSpec((tm, w), lambda i: (i, 0))
    return pl.pallas_call(
        body, name="in_bwd", grid=(T // tm,),
        in_specs=[row(ATTN_W)] * 7 + [_once((N_CHIPS, D_MODEL, IN_SHARD)), row(D_MODEL), _full((1, D_MODEL)),
                                       row(D_MODEL)],
        out_specs=[row(IN_TOTAL), row(D_MODEL), _full((1, D_MODEL))],
        out_shape=[jax.ShapeDtypeStruct((T, IN_TOTAL), BF16), jax.ShapeDtypeStruct((T, D_MODEL), F32),
                   jax.ShapeDtypeStruct((1, D_MODEL), F32)],
        compiler_params=_cp("arbitrary"),
    )(*dqkv, *dhg, w_in4, x, g1, dh1)


def _dw(a, b, kb, nb_, name, tk=1024):
    T, K = a.shape
    N = b.shape[1]
    nk, nn, nt = K // kb, N // nb_, T // tk

    def body(a_ref, b_ref, o_ref, acc):
        t = pl.program_id(2)

        @pl.when(t == 0)
        def _():
            acc[...] = jnp.zeros_like(acc)

        acc[...] += _dot_tn(a_ref[...], b_ref[...].astype(BF16))

        @pl.when(t == nt - 1)
        def _():
            o_ref[0] = acc[...].astype(BF16)

    return pl.pallas_call(
        body, name=name, grid=(nk, nn, nt),
        in_specs=[pl.BlockSpec((tk, kb), lambda i, j, t: (t, i)), pl.BlockSpec((tk, nb_), lambda i, j, t: (t, j))],
        out_specs=pl.BlockSpec((1, kb, nb_), lambda i, j, t: (i * nn + j, 0, 0)),
        out_shape=jax.ShapeDtypeStruct((nk * nn, kb, nb_), BF16),
        scratch_shapes=[pltpu.VMEM((kb, nb_), F32)],
        compiler_params=_cp("arbitrary", "arbitrary", "arbitrary"),
    )(a, b)


def _local_step(x, tgt, g1, w_in4, g_a, g_h, lb, w_out, g2, w_up4, conv_w, conv_b, w_down, gf):
    a = _step_mixers(x, g1, w_in4, lb)
    b = _step_channel(a, x, tgt, g_a, g_h, w_out, g2, w_up4, conv_w, conv_b, w_down, gf)
    c = _step_mixers_bwd(a, b, x, g1, w_in4, lb)
    small = dict(g1=c["dg1"], g_a=b["dga"], g_h=b["dgh"], lb=c["dlb"], g2=b["dg2"], conv_w=b["dcw"], conv_b=b["dcb"],
                 gf=b["dgf"])
    return b["loss"], c["dx"], small, dict(w_in=c["dw_in"], w_out=b["dw_out"], w_up=b["dw_up"], w_down=b["dw_down"])


def _step_mixers(x, g1, w_in4, lb):
    u1, qkv, hg = _in_proj(x, g1, w_in4)
    attn_o, lse = _attn_fwd(qkv)
    rec_o, states = _hgrn_fwd(hg, lb)
    return dict(u1=u1, qkv=qkv, hg=hg, attn_o=attn_o, lse=lse, rec_o=rec_o, states=states)


def _step_channel(a, x, tgt, g_a, g_h, w_out, g2, w_up4, conv_w, conv_b, w_down, gf):
    h1, mixed = _mix_out(a["attn_o"], a["rec_o"], a["hg"], x, g_a, g_h, w_out)
    u2, gate, val, dh2, loss, dgf = _mlp_fwd(h1, g2, w_up4, conv_w, conv_b, w_down, gf, tgt)
    dgv, act, dcw, dcb = _mlp_bwd(dh2, gate, val, conv_w, conv_b, w_down)
    dw_down = _dw(act, dh2, D_FF // 2, D_MODEL, "dw_down").reshape(N_CHIPS, D_FF // N_CHIPS, D_MODEL)
    dh1, dg2, da, dr, dgt, dga, dgh = _up_out_bwd(dgv, w_up4, h1, g2, dh2, w_out, a["attn_o"], a["rec_o"], a["hg"],
                                                  g_a, g_h)
    dw_up = _dw(u2, dgv, D_MODEL, UP_SHARD, "dw_up")
    dw_out = _dw(mixed, dh1, D_MODEL, D_MODEL, "dw_out").reshape(N_CHIPS, D_MODEL // N_CHIPS, D_MODEL)
    return dict(loss=loss, dgf=dgf, dcw=dcw, dcb=dcb, dg2=dg2, dga=dga, dgh=dgh, dh1=dh1, da=da, dr=dr, dgt=dgt,
                dw_down=dw_down, dw_up=dw_up, dw_out=dw_out)


def _step_mixers_bwd(a, b, x, g1, w_in4, lb, token=None):
    dqkv = _attn_bwd(a["qkv"], a["attn_o"], a["lse"], b["da"], token)
    dhq, dhf, dhi, dlb = _hgrn_bwd(a["hg"], lb, a["states"], b["dr"])
    dproj, dx, dg1 = _in_bwd(dqkv, [dhq, dhf, dhi, b["dgt"]], w_in4, x, g1, b["dh1"])
    dw_in = _dw(a["u1"], dproj, D_MODEL, IN_SHARD, "dw_in")
    return dict(dx=dx, dg1=dg1, dlb=dlb, dw_in=dw_in)


BIG = ("w_in", "w_out", "w_up", "w_down")
ANY = pl.BlockSpec(memory_space=pl.ANY)


def _place():
    x, y, c = lax.axis_index("x"), lax.axis_index("y"), lax.axis_index("c")
    chips = [(1 - x, y), (x, 1 - y), (1 - x, 1 - y)]
    return x, y, c, chips


def _remote(src, dst, send_sems, recv_sems, k, to):
    return pltpu.make_async_remote_copy(src_ref=src, dst_ref=dst, send_sem=send_sems.at[k], recv_sem=recv_sems.at[k],
                                        device_id=to, device_id_type=MESH)


def _gather_weights(shards, conv_w):
    n = len(shards)
    halves = [s.shape[0] // 2 for s in shards]

    def body(*refs):
        ins, cw, outs, ocw = refs[:n], refs[n], refs[n + 1:2 * n + 1], refs[2 * n + 1]
        send_sems, recv_sems = refs[2 * n + 2:]
        x, y, c, chips = _place()
        me, sibling = 2 * x + y, (x, y, 1 - c)

        def part(w, chip, half):
            return outs[w].at[chip, pl.ds(half * halves[w], halves[w]), :]

        sent = []
        for j, chip in enumerate(chips):
            for w in range(n):
                sent.append(_remote(ins[w].at[pl.ds(c * halves[w], halves[w]), :], part(w, me, c),
                                    send_sems, recv_sems, w * 3 + j, (*chip, c)))
            sent.append(_remote(cw, ocw.at[me], send_sems, recv_sems, 6 * n + j, (*chip, c)))
        for cp in sent:
            cp.start()
        for j, chip in enumerate(chips):
            kj = 2 * chip[0] + chip[1]
            for w in range(n):
                _remote(part(w, kj, c), part(w, kj, c), send_sems, recv_sems, w * 3 + j, (*chip, c)).wait_recv()
                fwd = _remote(part(w, kj, c), part(w, kj, c), send_sems, recv_sems, 3 * n + w * 3 + j, sibling)
                fwd.start()
                sent.append(fwd)
        for j, chip in enumerate(chips):
            kj = 2 * chip[0] + chip[1]
            for w in range(n):
                _remote(part(w, kj, 1 - c), part(w, kj, 1 - c), send_sems, recv_sems, 3 * n + w * 3 + j,
                        sibling).wait_recv()
            _remote(cw, ocw.at[kj], send_sems, recv_sems, 6 * n + j, (*chip, c)).wait_recv()
        for cp in sent:
            cp.wait_send()

    n_sem = 6 * n + 3
    outs = pl.pallas_call(
        body, name="gather_weights",
        in_specs=[ANY] * (n + 1), out_specs=[ANY] * (n + 1),
        out_shape=[jax.ShapeDtypeStruct((N_CHIPS,) + s.shape, s.dtype) for s in shards]
        + [jax.ShapeDtypeStruct((N_CHIPS,) + conv_w.shape, conv_w.dtype)],
        scratch_shapes=[pltpu.SemaphoreType.DMA((n_sem,)), pltpu.SemaphoreType.DMA((n_sem,))],
    )(*shards, conv_w)
    chip = 2 * lax.axis_index("x") + lax.axis_index("y")
    return [lax.dynamic_update_slice(o, s[None], (chip,) + (0,) * s.ndim) for o, s in zip(outs, [*shards, conv_w])]


def _allreduce_small(buf):
    rows = buf.shape[0]

    def body(in_ref, out_ref, slots, send_sems, recv_sems):
        x, y, c, _ = _place()
        me = 4 * x + 2 * y + c
        slots[me] = in_ref[...]
        sent = []
        for p in range(1, 8):
            to = (x ^ (p >> 2), y ^ ((p >> 1) & 1), c ^ (p & 1))
            sent.append(_remote(in_ref, slots.at[me], send_sems, recv_sems, p, to))
        for cp in sent:
            cp.start()
        for p in range(1, 8):
            frm = 4 * (x ^ (p >> 2)) + 2 * (y ^ ((p >> 1) & 1)) + (c ^ (p & 1))
            _remote(in_ref, slots.at[frm], send_sems, recv_sems, p, (x, y, c)).wait_recv()
        for cp in sent:
            cp.wait_send()
        acc = slots[0]
        for d in range(1, 8):
            acc = acc + slots[d]
        out_ref[...] = acc

    vm = pl.BlockSpec(memory_space=pltpu.VMEM)
    return pl.pallas_call(
        body, name="allreduce_small", in_specs=[vm], out_specs=vm,
        out_shape=jax.ShapeDtypeStruct(buf.shape, F32),
        scratch_shapes=[pltpu.VMEM((8, rows, 128), F32), pltpu.SemaphoreType.DMA((8,)), pltpu.SemaphoreType.DMA((8,))],
    )(buf)


def _pair_exchange(gs, name):
    n = len(gs)
    halves = [g.shape[1] // 2 for g in gs]

    def body(*refs):
        g, got = refs[:n], refs[n:2 * n]
        send_sems, recv_sems = refs[2 * n:]
        x, y, c, _ = _place()
        cps = [_remote(g[w].at[:, pl.ds((1 - c) * halves[w], halves[w]), :], got[w], send_sems, recv_sems, w,
                       (x, y, 1 - c)) for w in range(n)]
        for cp in cps:
            cp.start()
        for cp in cps:
            cp.wait()

    return pl.pallas_call(
        body, name=name, in_specs=[ANY] * n, out_specs=[ANY] * n,
        out_shape=[jax.ShapeDtypeStruct((N_CHIPS, h, g.shape[2]), g.dtype) for g, h in zip(gs, halves)],
        scratch_shapes=[pltpu.SemaphoreType.DMA((n,)), pltpu.SemaphoreType.DMA((n,))],
    )(*gs)


def _core_id():
    return lax.axis_index("c").reshape(1).astype(jnp.int32)


def _pair_sum(g, got, name):
    h, C = got.shape[1:]

    def body(c_ref, g_ref, b_ref, o_ref):
        o_ref[...] = (g_ref[...].astype(F32) + b_ref[...].astype(F32)).astype(BF16)

    blk = pl.BlockSpec((1, h, C), lambda k, c_ref: (k, 0, 0))
    return pl.pallas_call(
        body, name=name,
        grid_spec=pltpu.PrefetchScalarGridSpec(
            num_scalar_prefetch=1, grid=(N_CHIPS,),
            in_specs=[pl.BlockSpec((1, h, C), lambda k, c_ref: (k, c_ref[0], 0)), blk], out_specs=blk),
        out_shape=jax.ShapeDtypeStruct(got.shape, BF16), compiler_params=_cp("arbitrary"))(_core_id(), g, got)


def _sum_partials(g, got, landed, name):
    h, C = got.shape[1:]

    def body(ids, g_ref, b_ref, l_ref, o_ref):
        acc = g_ref[0].astype(F32) + b_ref[0].astype(F32)
        for j in range(3):
            acc = acc + l_ref[j].astype(F32)
        o_ref[...] = acc

    ids = jnp.stack([2 * lax.axis_index("x") + lax.axis_index("y"), lax.axis_index("c")]).astype(jnp.int32)
    return pl.pallas_call(
        body, name=name,
        grid_spec=pltpu.PrefetchScalarGridSpec(
            num_scalar_prefetch=1, grid=(1,),
            in_specs=[pl.BlockSpec((1, h, C), lambda i, ids: (ids[0], ids[1], 0)),
                      pl.BlockSpec((1, h, C), lambda i, ids: (ids[0], 0, 0)),
                      pl.BlockSpec((3, h, C), lambda i, ids: (0, 0, 0))],
            out_specs=pl.BlockSpec((h, C), lambda i, ids: (ids[1], 0))),
        out_shape=jax.ShapeDtypeStruct((2 * h, C), F32), compiler_params=_cp("arbitrary"))(ids, g, got, landed)


def _pair_share(reds, name):
    n = len(reds)

    def body(*refs):
        out = refs[n:2 * n]
        send_sems, recv_sems = refs[2 * n:]
        x, y, c, _ = _place()
        def half(w, which):
            h = out[w].shape[0] // 2
            return out[w].at[pl.ds(which * h, h), :]

        cps = [_remote(half(w, c), half(w, c), send_sems, recv_sems, w, (x, y, 1 - c)) for w in range(n)]
        for cp in cps:
            cp.start()
        for w in range(n):
            _remote(half(w, 1 - c), half(w, 1 - c), send_sems, recv_sems, w, (x, y, 1 - c)).wait_recv()
        for cp in cps:
            cp.wait_send()

    return pl.pallas_call(
        body, name=name, in_specs=[ANY] * n, out_specs=[ANY] * n,
        out_shape=[jax.ShapeDtypeStruct(r.shape, F32) for r in reds],
        input_output_aliases={w: w for w in range(n)},
        scratch_shapes=[pltpu.SemaphoreType.DMA((n,)), pltpu.SemaphoreType.DMA((n,))],
    )(*reds)


HBM = pl.BlockSpec(memory_space=pltpu.HBM)
SEM = pl.BlockSpec(memory_space=pltpu.SEMAPHORE)
DATAFLOW = pltpu.SideEffectType.DATAFLOW_SIDE_EFFECTING


def _copies_start(name, srcs, lands, plan, n_copies, after):
    n, na = len(srcs), len(after)

    def body(*refs):
        src_refs, land_refs = refs[:n], refs[n:2 * n]
        send_sems, recv_sems = refs[2 * n + na:2 * n + na + 2]
        token = refs[-1]
        for k, (src, there, _, to) in enumerate(plan(src_refs, land_refs)):
            _remote(src, there, send_sems, recv_sems, k, to).start()
        token[...] = jnp.zeros_like(token)

    hbm = lambda a: pltpu.HBM(a.shape, a.dtype)
    outs = pl.pallas_call(
        body, name=name,
        out_shape=(pltpu.SemaphoreType.DMA((n_copies,)), pltpu.SemaphoreType.DMA((n_copies,)),
                   *[hbm(a) for a in srcs], *[hbm(a) for a in lands], jax.ShapeDtypeStruct((8, 128), F32)),
        in_specs=[HBM] * (2 * n) + [ANY] * na,
        out_specs=(SEM, SEM, *[HBM] * (2 * n), pl.BlockSpec(memory_space=pltpu.VMEM)),
        input_output_aliases={i: 2 + i for i in range(2 * n)},
        compiler_params=pltpu.CompilerParams(has_side_effects=DATAFLOW),
    )(*[pltpu.with_memory_space_constraint(a, pltpu.HBM) for a in (*srcs, *lands)], *after)
    return outs[0], outs[1], outs[2:2 + n], outs[2 + n:2 + 2 * n], outs[-1]


def _copies_wait(name, send_sems, recv_sems, srcs, lands, plan, after):
    n, na = len(srcs), len(after)

    def body(*refs):
        src_refs, land_refs = refs[:n], refs[n:2 * n]
        send_sems, recv_sems = refs[2 * n:2 * n + 2]
        for k, (src, _, here, to) in enumerate(plan(src_refs, land_refs)):
            cp = _remote(src, here, send_sems, recv_sems, k, to)
            cp.wait_send()
            cp.wait_recv()

    hbm = lambda a: pltpu.HBM(a.shape, a.dtype)
    outs = pl.pallas_call(
        body, name=name,
        out_shape=(*[hbm(a) for a in srcs], *[hbm(a) for a in lands]),
        in_specs=[HBM] * (2 * n) + [SEM, SEM] + [ANY] * na,
        out_specs=tuple([HBM] * (2 * n)),
        input_output_aliases={i: i for i in range(2 * n)},
        compiler_params=pltpu.CompilerParams(has_side_effects=DATAFLOW),
    )(*srcs, *lands, send_sems, recv_sems, *after)
    return outs[n:]


def _gather_plan(halves):
    def plan(shards, lands):
        x, y, c, chips = _place()
        me = 2 * x + y
        copies = []
        for w, h in enumerate(halves):
            rows = pl.ds(c * h, h)
            for chip in chips:
                copies.append((shards[w].at[rows, :], lands[w].at[me, rows, :],
                               lands[w].at[2 * chip[0] + chip[1], rows, :], (*chip, c)))
        return copies
    return plan


def _reduce_plan(n):
    def plan(ps, lands):
        x, y, c, chips = _place()
        return [(ps[w].at[2 * chip[0] + chip[1]], lands[w].at[j], lands[w].at[j], (*chip, c))
                for w in range(n) for j, chip in enumerate(chips)]
    return plan


def _pair_forward(lands, shards):
    n = len(lands)

    def body(*refs):
        out = refs[n:2 * n]
        send_sems, recv_sems = refs[2 * n:]
        x, y, c, chips = _place()

        def part(w, chip, half):
            h = out[w].shape[1] // 2
            return out[w].at[2 * chip[0] + chip[1], pl.ds(half * h, h), :]

        cps = [_remote(part(w, chip, c), part(w, chip, c), send_sems, recv_sems, w * 3 + j, (x, y, 1 - c))
               for w in range(n) for j, chip in enumerate(chips)]
        for cp in cps:
            cp.start()
        for w in range(n):
            for j, chip in enumerate(chips):
                _remote(part(w, chip, 1 - c), part(w, chip, 1 - c), send_sems, recv_sems, w * 3 + j,
                        (x, y, 1 - c)).wait_recv()
        for cp in cps:
            cp.wait_send()

    outs = pl.pallas_call(
        body, name="pair_forward", in_specs=[ANY] * n, out_specs=[ANY] * n,
        out_shape=[jax.ShapeDtypeStruct(a.shape, a.dtype) for a in lands],
        input_output_aliases={w: w for w in range(n)},
        scratch_shapes=[pltpu.SemaphoreType.DMA((3 * n,)), pltpu.SemaphoreType.DMA((3 * n,))],
    )(*lands)
    chip = 2 * lax.axis_index("x") + lax.axis_index("y")
    return [lax.dynamic_update_slice(o, s[None], (chip, 0, 0)) for o, s in zip(outs, shards)]


def _adamw(w, g, m, v, name, tr=64):
    R, C = w.shape
    tr = min(tr, R)

    def body(w_ref, g_ref, m_ref, v_ref, d_ref, nm_ref, nv_ref):
        gv = g_ref[...]
        nm = ADAM_B1 * m_ref[...] + (1.0 - ADAM_B1) * gv
        nv = ADAM_B2 * v_ref[...] + (1.0 - ADAM_B2) * (gv * gv)
        m_hat = nm / (1.0 - ADAM_B1 ** ADAM_STEP)
        v_hat = nv / (1.0 - ADAM_B2 ** ADAM_STEP)
        d_ref[...] = -ADAM_LR * (m_hat / (jnp.sqrt(v_hat) + ADAM_EPS) + ADAM_WD * w_ref[...])
        nm_ref[...] = nm
        nv_ref[...] = nv

    blk = pl.BlockSpec((tr, C), lambda i: (i, 0))
    return pl.pallas_call(body, name=name, grid=(R // tr,), in_specs=[blk] * 4, out_specs=[blk] * 3,
                          out_shape=[jax.ShapeDtypeStruct((R, C), F32)] * 3, compiler_params=_cp("arbitrary"))(w, g, m, v)


SMALL = (("norm1_g", 1024), ("attn_norm_g", 512), ("hgrn_norm_g", 512), ("hgrn_lb_logits", 1024), ("norm2_g", 1024),
         ("conv_b", D_FF), ("final_norm_g", 1024), ("conv_w", 3 * D_FF))
SMALL_ROWS = 136


def _pack(parts, rows):
    flat = jnp.concatenate([p.reshape(-1).astype(F32) for p in parts])
    return jnp.pad(flat, (0, rows * 128 - flat.shape[0])).reshape(rows, 128)


def kernel(x, norm1_g, w_in, attn_norm_g, hgrn_norm_g, hgrn_lb_logits, w_out, norm2_g, w_up, conv_w, conv_b, w_down, final_norm_g, loss_target, m_norm1_g, m_w_in, m_attn_norm_g, m_hgrn_norm_g, m_hgrn_lb_logits, m_w_out, m_norm2_g, m_w_up, m_conv_w, m_conv_b, m_w_down, m_final_norm_g, v_norm1_g, v_w_in, v_attn_norm_g, v_hgrn_norm_g, v_hgrn_lb_logits, v_w_out, v_norm2_g, v_w_up, v_conv_w, v_conv_b, v_w_down, v_final_norm_g):
    w = dict(norm1_g=norm1_g, w_in=w_in, attn_norm_g=attn_norm_g, hgrn_norm_g=hgrn_norm_g,
             hgrn_lb_logits=hgrn_lb_logits, w_out=w_out, norm2_g=norm2_g, w_up=w_up, conv_w=conv_w, conv_b=conv_b,
             w_down=w_down, final_norm_g=final_norm_g)
    m = dict(norm1_g=m_norm1_g, w_in=m_w_in, attn_norm_g=m_attn_norm_g, hgrn_norm_g=m_hgrn_norm_g,
             hgrn_lb_logits=m_hgrn_lb_logits, w_out=m_w_out, norm2_g=m_norm2_g, w_up=m_w_up, conv_w=m_conv_w,
             conv_b=m_conv_b, w_down=m_w_down, final_norm_g=m_final_norm_g)
    v = dict(norm1_g=v_norm1_g, w_in=v_w_in, attn_norm_g=v_attn_norm_g, hgrn_norm_g=v_hgrn_norm_g,
             hgrn_lb_logits=v_hgrn_lb_logits, w_out=v_w_out, norm2_g=v_norm2_g, w_up=v_w_up, conv_w=v_conv_w,
             conv_b=v_conv_b, w_down=v_w_down, final_norm_g=v_final_norm_g)
    names = list(w)
    chip = 2 * lax.axis_index("x") + lax.axis_index("y")

    shards = {k: w[k][0].astype(BF16) for k in BIG}
    w_in4, conv_w4 = _gather_weights([shards["w_in"]], conv_w[0])
    conv_w_full = jnp.transpose(conv_w4, (1, 0, 2)).reshape(3, D_FF)
    lb = jax.nn.softmax(hgrn_lb_logits, axis=0)[0:1]
    late = [shards[k] for k in BIG[1:]]
    gather_plan = _gather_plan([s.shape[0] // 2 for s in late])
    started = _copies_start("gather_start", late, [lax.empty((N_CHIPS,) + s.shape, BF16) for s in late], gather_plan,
                            3 * len(late), after=(w_in4,))
    a = _step_mixers(x[0], norm1_g + started[4][0:1, 0:1], w_in4, lb)
    landed_w = _copies_wait("gather_wait", *started[:4], gather_plan, after=(a["attn_o"], a["rec_o"]))
    w_out4, w_up4, w_down4 = _pair_forward(landed_w, late)

    b = _step_channel(a, x[0], loss_target[0], attn_norm_g, hgrn_norm_g, w_out4.reshape(D_MODEL, D_MODEL), norm2_g,
                      w_up4, conv_w_full, conv_b, w_down4.reshape(D_FF, D_MODEL), final_norm_g.reshape(1, D_MODEL))

    early = [b["dw_out"], b["dw_up"], b["dw_down"]]
    gots = _pair_exchange(early, "pair_exchange")
    ps = [_pair_sum(gk, got, f"pair_sum_{k}") for gk, got, k in zip(early, gots, BIG[1:])]
    reduce_plan = _reduce_plan(len(ps))
    started = _copies_start("reduce_start", ps, [lax.empty((3,) + p.shape[1:], BF16) for p in ps], reduce_plan,
                            3 * len(ps), after=())
    c = _step_mixers_bwd(a, b, x[0], norm1_g, w_in4, lb + started[4][0:1, 0:1], token=started[4])
    gots_in = _pair_exchange([c["dw_in"]], "pair_exchange_w_in")
    ps_in = _pair_sum(c["dw_in"], gots_in[0], "pair_sum_w_in")
    plan_in = _reduce_plan(1)
    started_in = _copies_start("reduce_start_w_in", [ps_in], [lax.empty((3,) + ps_in.shape[1:], BF16)], plan_in, 3,
                               after=())
    landed = _copies_wait("reduce_wait", *started[:4], reduce_plan, after=(started_in[4],))
    reds = [_sum_partials(gk, got, l, f"sum_partials_{k}") for gk, got, l, k in zip(early, gots, landed, BIG[1:])]
    g = dict(zip(BIG[1:], _pair_share(reds, "pair_share")))
    delta, new_m, new_v = {}, {}, {}
    for k in BIG[1:]:
        delta[k], new_m[k], new_v[k] = _adamw(w[k][0], g[k], m[k][0], v[k][0], f"adamw_{k}")

    loss, dx = b["loss"], c["dx"]
    small = dict(g1=c["dg1"], g_a=b["dga"], g_h=b["dgh"], lb=c["dlb"], g2=b["dg2"], conv_w=b["dcw"], conv_b=b["dcb"],
                 gf=b["dgf"])
    dlb = small["lb"] * lb * (1.0 - lb)
    grads_small = dict(norm1_g=small["g1"], attn_norm_g=small["g_a"], hgrn_norm_g=small["g_h"],
                       hgrn_lb_logits=jnp.concatenate([dlb, -dlb], axis=0), norm2_g=small["g2"],
                       conv_b=small["conv_b"], final_norm_g=small["gf"], conv_w=small["conv_w"])
    summed = _allreduce_small(_pack([grads_small[k] for k, _ in SMALL] + [loss[0, 0:1]], SMALL_ROWS)).reshape(-1)
    off = 0
    for k, size in SMALL:
        g[k] = summed[off:off + size]
        off += size
    loss_total = summed[off]
    g["conv_w"] = lax.dynamic_slice(g["conv_w"].reshape(3, D_FF), (0, chip * (D_FF // N_CHIPS)), (3, D_FF // N_CHIPS))
    small_names = [k for k in names if k not in BIG]
    rows = 80
    packed = _adamw(_pack([w[k] for k in small_names], rows), _pack([g[k] for k in small_names], rows),
                    _pack([m[k] for k in small_names], rows), _pack([v[k] for k in small_names], rows), "adamw_small", tr=rows)
    flat = [a.reshape(-1) for a in packed]
    off = 0
    for k in small_names:
        size = w[k].size
        delta[k], new_m[k], new_v[k] = (a[off:off + size].reshape(w[k].shape) for a in flat)
        g[k] = g[k].reshape(w[k].shape)
        off += size

    landed_in = _copies_wait("reduce_wait_w_in", *started_in[:4], plan_in, after=(packed[0], delta["w_up"]))
    red_in = _sum_partials(c["dw_in"], gots_in[0], landed_in[0], "sum_partials_w_in")
    g["w_in"] = _pair_share([red_in], "pair_share_w_in")[0]
    delta["w_in"], new_m["w_in"], new_v["w_in"] = _adamw(w_in[0], g["w_in"], m_w_in[0], v_w_in[0], "adamw_w_in")
    for k in BIG:
        g[k], delta[k], new_m[k], new_v[k] = g[k][None], delta[k][None], new_m[k][None], new_v[k][None]

    return (loss_total, dx[None], *[g[k] for k in names], *[delta[k] for k in names],
            *[new_m[k] for k in names], *[new_v[k] for k in names])
```

```python
import functools
import math

import jax
import jax.numpy as jnp
from jax import lax
from jax.experimental import pallas as pl
from jax.experimental.pallas import tpu as pltpu

F32 = jnp.float32
BF16 = jnp.bfloat16

D_MODEL = 1024
ATTN_W = 512
HGRN_W = 512
HEAD_PAIR = 128
ATTN_BLK = 128
DILATIONS = (1, 4, 16)
ATTN_CHAINS = 4
HGRN_HEADS = 4
HGRN_DIM = 128
HGRN_CHUNK = 64
SUPER = 256
HGRN_SIDE = 2
D_FF = 2816
N_CHIPS = 4
IN_TOTAL = 3584
IN_SHARD = IN_TOTAL // N_CHIPS
UP_SHARD = 2 * D_FF // N_CHIPS
QKV_W = 3 * ATTN_W
HG_W = 4 * HGRN_W
EPS = 1e-6
NEG = -1e30
V7X_VMEM_BYTES = 64 * 1024 * 1024
VMEM_LIMIT = V7X_VMEM_BYTES - 8 * 1024 * 1024

ADAM_LR = 0.001
ADAM_B1 = 0.9
ADAM_B2 = 0.999
ADAM_EPS = 1e-08
ADAM_WD = 0.01
ADAM_STEP = 10

MESH = pl.DeviceIdType.MESH


def _cp(*sem):
    return pltpu.CompilerParams(dimension_semantics=sem or None, vmem_limit_bytes=VMEM_LIMIT)


def _dot(a, b):
    return jnp.dot(a, b, preferred_element_type=F32)


def _dot_nt(a, b):
    return lax.dot_general(a, b, (((1,), (1,)), ((), ())), preferred_element_type=F32)


def _dot_tn(a, b):
    return lax.dot_general(a, b, (((0,), (0,)), ((), ())), preferred_element_type=F32)


def _sigmoid(x):
    return 1.0 / (1.0 + jnp.exp(-x))


def _rms(x, width):
    return lax.rsqrt(jnp.sum(x * x, axis=-1, keepdims=True) * (1.0 / width) + EPS)


def _rms_bwd(dn, n, r, width):
    return r * (dn - n * (jnp.sum(dn * n, axis=-1, keepdims=True) * (1.0 / width)))


def _colsum(x):
    return jnp.sum(x, axis=0, keepdims=True)


def _row(v, k):
    rid = lax.broadcasted_iota(jnp.int32, v.shape, 0)
    return jnp.sum(jnp.where(rid == k, v, 0.0), axis=0, keepdims=True)


def _full(shape):
    return pl.BlockSpec(shape, lambda *_: (0,) * len(shape))


def _once(shape):
    return pl.BlockSpec(shape, lambda *_: (0,) * len(shape), pipeline_mode=pl.Buffered(1))


def _in_proj(x, g1, w_in4, tm=512):
    T = x.shape[0]

    def body(x_ref, g_ref, w_ref, u_ref, qkv_ref, hg_ref):
        xv = x_ref[...]
        u = (xv * _rms(xv, D_MODEL) * g_ref[...]).astype(BF16)
        u_ref[...] = u
        p0 = _dot(u, w_ref[0])
        p1 = _dot(u, w_ref[1])
        qkv_ref[:, 0:IN_SHARD] = p0
        qkv_ref[:, IN_SHARD:QKV_W] = p1[:, :QKV_W - IN_SHARD]
        hg_ref[:, 0:2 * IN_SHARD - QKV_W] = p1[:, QKV_W - IN_SHARD:]
        hg_ref[:, 2 * IN_SHARD - QKV_W:3 * IN_SHARD - QKV_W] = _dot(u, w_ref[2])
        hg_ref[:, 3 * IN_SHARD - QKV_W:HG_W] = _dot(u, w_ref[3])

    return pl.pallas_call(
        body, name="in_proj", grid=(T // tm,),
        in_specs=[pl.BlockSpec((tm, D_MODEL), lambda i: (i, 0)), _full((1, D_MODEL)),
                  _once((N_CHIPS, D_MODEL, IN_SHARD))],
        out_specs=[pl.BlockSpec((tm, D_MODEL), lambda i: (i, 0)), pl.BlockSpec((tm, QKV_W), lambda i: (i, 0)),
                   pl.BlockSpec((tm, HG_W), lambda i: (i, 0))],
        out_shape=[jax.ShapeDtypeStruct((T, D_MODEL), BF16), jax.ShapeDtypeStruct((T, QKV_W), F32),
                   jax.ShapeDtypeStruct((T, HG_W), F32)],
        compiler_params=_cp("arbitrary"),
    )(x, g1, w_in4)


def _attn_masks(bias_ref):
    lane = lax.broadcasted_iota(jnp.int32, (ATTN_BLK, HEAD_PAIR), 1)
    row = lax.broadcasted_iota(jnp.int32, (2 * ATTN_BLK, 2 * ATTN_BLK), 0)
    col = lax.broadcasted_iota(jnp.int32, (2 * ATTN_BLK, 2 * ATTN_BLK), 1)
    base = jnp.where(row >= ATTN_BLK, row - ATTN_BLK, row) - col
    for k in range(2):
        dist = base + k * ATTN_BLK
        bias_ref[k] = jnp.where((dist >= 0) & (dist <= ATTN_BLK), 0.0, NEG)
    return lane < 64


def _two_heads(blk, first):
    zero = jnp.zeros_like(blk)
    return jnp.concatenate([jnp.where(first, blk, zero), jnp.where(first, zero, blk)], axis=0)


def _attn_rows(idx, nb, d):
    r, n = idx // nb, idx % nb
    kb = jnp.maximum(n - 1, 0)
    if d == 1:
        q0 = pl.multiple_of(n * ATTN_BLK, ATTN_BLK)
        k0 = pl.multiple_of(kb * ATTN_BLK, ATTN_BLK)
        return pl.ds(q0, ATTN_BLK), pl.ds(k0, 2 * ATTN_BLK), n - kb
    return (pl.ds(r + d * ATTN_BLK * n, ATTN_BLK, stride=d), pl.ds(r + d * ATTN_BLK * kb, 2 * ATTN_BLK, stride=d),
            n - kb)


def _attn_fwd(qkv):
    T = qkv.shape[0]

    per_chain = T // ATTN_BLK // ATTN_CHAINS

    def body(q_ref, k_ref, v_ref, o_ref, m_ref, l_ref, bias_ref):
        first = _attn_masks(bias_ref)
        for bi, d in enumerate(DILATIONS):
            nb = T // d // ATTN_BLK

            def block(idx, d=d, nb=nb, bi=bi):
                rows, keys, which = _attn_rows(idx, nb, d)
                q2 = _two_heads(q_ref[rows, :] * 0.125, first).astype(BF16)
                kw = k_ref[keys, :].astype(BF16)
                vw = v_ref[keys, :].astype(BF16)
                old = (o_ref[rows, :], m_ref[rows, :], l_ref[rows, :]) if bi else None
                s = _dot_nt(q2, kw) + bias_ref[which]
                mb = jnp.max(s, axis=-1, keepdims=True)
                p = jnp.exp(s - mb)
                lb = jnp.sum(p, axis=-1, keepdims=True)
                o2 = _dot(p.astype(BF16), vw)
                o = jnp.where(first, o2[:ATTN_BLK], o2[ATTN_BLK:])
                m = jnp.where(first, mb[:ATTN_BLK], mb[ATTN_BLK:])
                l = jnp.where(first, lb[:ATTN_BLK], lb[ATTN_BLK:])
                if bi:
                    po, pm, pl_ = old
                    mn = jnp.maximum(pm, m)
                    wa = jnp.exp(pm - mn)
                    wb = jnp.exp(m - mn)
                    o, l, m = po * wa + o * wb, pl_ * wa + l * wb, mn
                return rows, o, m, l

            def step(i, carry, block=block):
                done = [block(i + ch * per_chain) for ch in range(ATTN_CHAINS)]
                for rows, o, m, l in done:
                    o_ref[rows, :] = o
                    m_ref[rows, :] = m
                    l_ref[rows, :] = l
                return carry

            lax.fori_loop(0, per_chain, step, 0)

        def finish(i, carry):
            rows = pl.ds(pl.multiple_of(i * SUPER, SUPER), SUPER)
            l = l_ref[rows, :]
            o_ref[rows, :] = o_ref[rows, :] / l
            m_ref[rows, :] = m_ref[rows, :] + jnp.log(l)
            return carry

        lax.fori_loop(0, T // SUPER, finish, 0)

    col = lambda off: pl.BlockSpec((T, HEAD_PAIR), lambda j: (0, off + j))
    return pl.pallas_call(
        body, name="attn_fwd", grid=(4,),
        in_specs=[col(0), col(4), col(8)], out_specs=[col(0), col(0)],
        out_shape=[jax.ShapeDtypeStruct((T, ATTN_W), F32)] * 2,
        scratch_shapes=[pltpu.VMEM((T, HEAD_PAIR), F32), pltpu.VMEM((2, 2 * ATTN_BLK, 2 * ATTN_BLK), F32)],
        compiler_params=_cp("arbitrary"),
    )(qkv, qkv, qkv)


def _attn_bwd(qkv, o, lse, do, token=None):
    T = qkv.shape[0]
    per_chain = T // ATTN_BLK // ATTN_CHAINS
    extra = [] if token is None else [token]

    def body(q_ref, k_ref, v_ref, o_ref, lse_ref, do_ref, *rest):
        dq_ref, dk_ref, dv_ref, bias_ref = rest[len(extra):]
        first = _attn_masks(bias_ref)
        dq_ref[...] = jnp.zeros_like(dq_ref)
        dk_ref[...] = jnp.zeros_like(dk_ref)
        dv_ref[...] = jnp.zeros_like(dv_ref)
        for d in DILATIONS:
            nb = T // d // ATTN_BLK

            def block(idx, d=d, nb=nb):
                rows, keys, which = _attn_rows(idx, nb, d)
                q2 = _two_heads(q_ref[rows, :] * 0.125, first).astype(BF16)
                kw = k_ref[keys, :].astype(BF16)
                vw = v_ref[keys, :].astype(BF16)
                lse_b = lse_ref[rows, :]
                dob = do_ref[rows, :]
                prod = dob * o_ref[rows, :]
                old = dq_ref[rows, :], dk_ref[keys, :], dv_ref[keys, :]
                lse2 = jnp.concatenate(
                    [jnp.max(jnp.where(first, lse_b, NEG), axis=-1, keepdims=True),
                     jnp.max(jnp.where(first, NEG, lse_b), axis=-1, keepdims=True)], axis=0)
                p = jnp.exp(_dot_nt(q2, kw) + (bias_ref[which] - lse2))
                delta = jnp.concatenate(
                    [jnp.sum(jnp.where(first, prod, 0.0), axis=-1, keepdims=True),
                     jnp.sum(jnp.where(first, 0.0, prod), axis=-1, keepdims=True)], axis=0)
                do2 = _two_heads(dob, first).astype(BF16)
                ds = (p * (_dot_nt(do2, vw) - delta)).astype(BF16)
                dq2 = _dot(ds, kw) * 0.125
                return (rows, keys, old[0] + jnp.where(first, dq2[:ATTN_BLK], dq2[ATTN_BLK:]),
                        old[1] + _dot_tn(ds, q2), old[2] + _dot_tn(p.astype(BF16), do2))

            def step(i, carry, block=block):
                done = [block(i + ch * per_chain) for ch in range(ATTN_CHAINS)]
                for rows, keys, dq, dk, dv in done:
                    dq_ref[rows, :] = dq
                    dk_ref[keys, :] = dk
                    dv_ref[keys, :] = dv
                return carry

            lax.fori_loop(0, per_chain, step, 0)

    col = lambda off: pl.BlockSpec((T, HEAD_PAIR), lambda j: (0, off + j))
    return pl.pallas_call(
        body, name="attn_bwd", grid=(4,),
        in_specs=[col(0), col(4), col(8), col(0), col(0), col(0)] + [_full(t.shape) for t in extra],
        out_specs=[col(0)] * 3,
        out_shape=[jax.ShapeDtypeStruct((T, ATTN_W), F32)] * 3,
        scratch_shapes=[pltpu.VMEM((2, 2 * ATTN_BLK, 2 * ATTN_BLK), F32)],
        compiler_params=_cp("arbitrary"),
    )(qkv, qkv, qkv, o, lse, do, *extra)


def _chunk_ids():
    row = lax.broadcasted_iota(jnp.int32, (SUPER, HGRN_DIM), 0)
    r2 = lax.broadcasted_iota(jnp.int32, (SUPER, SUPER), 0)
    c2 = lax.broadcasted_iota(jnp.int32, (SUPER, SUPER), 1)
    amask = ((r2 // HGRN_CHUNK) == (c2 // HGRN_CHUNK)) & (c2 <= r2)
    return row % HGRN_CHUNK, row // HGRN_CHUNK, amask


def _cumsum_chunk(x, rmod):
    s = 1
    while s < HGRN_CHUNK:
        x = x + jnp.where(rmod >= s, pltpu.roll(x, s, 0), 0.0)
        s *= 2
    return x


def _suffix_sum_chunk(x, rmod):
    s = 1
    while s < HGRN_CHUNK:
        x = x + jnp.where(rmod < HGRN_CHUNK - s, pltpu.roll(x, SUPER - s, 0), 0.0)
        s *= 2
    return x


def _chunk_rows(vs, cid):
    out = vs[-1]
    for c in reversed(range(len(vs) - 1)):
        out = jnp.where(cid == c, vs[c], out)
    return out


def _expand(x, cid):
    return jnp.concatenate([jnp.where(cid == c, x, 0.0) for c in range(SUPER // HGRN_CHUNK)], axis=1)


def _hgrn_gates(q, f, lbv, rmod, cid, tmp):
    sq = _sigmoid(q)
    sg = _sigmoid(f)
    forget = lbv + (1.0 - lbv) * sg
    key = 1.0 - forget
    b = _cumsum_chunk(jnp.log(forget), rmod)
    tmp[...] = b
    bends = [tmp[c * HGRN_CHUNK + HGRN_CHUNK - 1:(c + 1) * HGRN_CHUNK, :] for c in range(SUPER // HGRN_CHUNK)]
    eb = jnp.exp(b)
    enb = jnp.exp(-b)
    ebe = jnp.exp(_chunk_rows(bends, cid) - b)
    return sq, sg, forget, key, eb, enb, ebe, q * sq * eb, key * enb, key * ebe, [jnp.exp(v) for v in bends]


def _hgrn_fwd(hg, lb):
    T = hg.shape[0]
    nsc = T // SUPER
    NC = SUPER // HGRN_CHUNK

    def body(q_ref, f_ref, i_ref, lb_ref, o_ref, st_ref, state, tmp):
        rmod, cid, amask = _chunk_ids()
        state[...] = jnp.zeros_like(state)
        lbv = lb_ref[...]

        def local(sc, u):
            rows = pl.ds(pl.multiple_of(sc * SUPER, SUPER), SUPER)
            iv = i_ref[rows, :].astype(BF16)
            qd, ki, ke, dec = _hgrn_gates(q_ref[rows, :], f_ref[rows, :], lbv, rmod, cid, tmp.at[u])[-4:]
            a = jnp.where(amask, _dot_nt(qd.astype(BF16), ki.astype(BF16)), 0.0)
            return rows, qd, dec, _dot(a.astype(BF16), iv), _dot_tn(iv, _expand(ke, cid).astype(BF16))

        def step(i, carry):
            parts = [local(i * HGRN_SIDE + u, u) for u in range(HGRN_SIDE)]
            st = state[...]
            entering = []
            for u, (_, _, dec, _, ut) in enumerate(parts):
                st_ref[0, i * HGRN_SIDE + u] = st
                sts = []
                for c in range(NC):
                    sts.append(st)
                    st = st * dec[c] + ut[:, c * HGRN_DIM:(c + 1) * HGRN_DIM]
                entering.append(jnp.concatenate(sts, axis=1).astype(BF16))
            state[...] = st
            for (rows, qd, _, o, _), sts in zip(parts, entering):
                o_ref[rows, :] = o + _dot_nt(_expand(qd, cid).astype(BF16), sts)
            return carry

        lax.fori_loop(0, nsc // HGRN_SIDE, step, 0)

    col = lambda off: pl.BlockSpec((T, HGRN_DIM), lambda h: (0, off + h))
    return pl.pallas_call(
        body, name="hgrn_fwd", grid=(HGRN_HEADS,),
        in_specs=[col(0), col(4), col(8), pl.BlockSpec((1, HGRN_DIM), lambda h: (0, h))],
        out_specs=[pl.BlockSpec((T, HGRN_DIM), lambda h: (0, h)),
                   pl.BlockSpec((1, nsc, HGRN_DIM, HGRN_DIM), lambda h: (h, 0, 0, 0))],
        out_shape=[jax.ShapeDtypeStruct((T, HGRN_W), F32),
                   jax.ShapeDtypeStruct((HGRN_HEADS, nsc, HGRN_DIM, HGRN_DIM), F32)],
        scratch_shapes=[pltpu.VMEM((HGRN_DIM, HGRN_DIM), F32), pltpu.VMEM((HGRN_SIDE, SUPER, HGRN_DIM), F32)],
        compiler_params=_cp("arbitrary"),
    )(hg, hg, hg, lb)


def _hgrn_bwd(hg, lb, states, do):
    T = hg.shape[0]
    nsc = T // SUPER
    NC = SUPER // HGRN_CHUNK

    def body(q_ref, f_ref, i_ref, lb_ref, st_ref, do_ref, dq_ref, df_ref, di_ref, dlb_ref, dstate, tmp):
        rmod, cid, amask = _chunk_ids()
        dstate[...] = jnp.zeros_like(dstate)
        dlb_ref[...] = jnp.zeros_like(dlb_ref)
        lbv = lb_ref[...]

        def local(sc, u):
            rows = pl.ds(pl.multiple_of(sc * SUPER, SUPER), SUPER)
            q = q_ref[rows, :]
            ivf = i_ref[rows, :]
            iv = ivf.astype(BF16)
            dof = do_ref[rows, :]
            dob = dof.astype(BF16)
            sq, sg, forget, key, eb, enb, ebe, qd, ki, ke, dec = _hgrn_gates(q, f_ref[rows, :], lbv, rmod, cid,
                                                                            tmp.at[u])
            qdb, kib = qd.astype(BF16), ki.astype(BF16)
            keexp = _expand(ke, cid).astype(BF16)
            a = jnp.where(amask, _dot_nt(qdb, kib), 0.0).astype(BF16)
            ut = _dot_tn(iv, keexp)
            st = st_ref[0, sc]
            sts = []
            for c in range(NC):
                sts.append(st)
                st = st * dec[c] + ut[:, c * HGRN_DIM:(c + 1) * HGRN_DIM]
            gt = _dot_tn(dob, _expand(qd, cid).astype(BF16))
            da = jnp.where(amask, _dot_nt(dob, iv), 0.0).astype(BF16)
            ststack = jnp.concatenate(sts, axis=0).astype(BF16)
            return dict(rows=rows, q=q, sq=sq, sg=sg, forget=forget, eb=eb, enb=enb, ebe=ebe, qd=qd, ki=ki, ke=ke,
                        dec=dec, sts=sts, gt=gt, keexp=keexp, ivexp=_expand(ivf, cid).astype(BF16),
                        div=_dot_tn(a, dob), dki=_dot_tn(da, qdb),
                        dqd=_dot(da, kib) + _dot(_expand(dof, cid).astype(BF16), ststack))

        def finish(p, nxt, ddec):
            ncat = jnp.concatenate(nxt, axis=1).astype(BF16)
            nstack = jnp.concatenate(nxt, axis=0).astype(BF16)
            dke = _dot(p["ivexp"], nstack)
            dkk = dke * p["ke"]
            dkey = p["dki"] * p["enb"] + dke * p["ebe"]
            db = p["dqd"] * p["qd"] - p["dki"] * p["ki"] - dkk
            dbends = [_colsum(jnp.where(cid == c, dkk, 0.0)) + ddec[c] * p["dec"][c] for c in range(NC)]
            dforget = (_suffix_sum_chunk(db, rmod) + _chunk_rows(dbends, cid)) / p["forget"] - dkey
            sg, sq, q = p["sg"], p["sq"], p["q"]
            df_ref[p["rows"], :] = dforget * (1.0 - lbv) * sg * (1.0 - sg)
            dq_ref[p["rows"], :] = p["dqd"] * p["eb"] * (sq * (1.0 + q * (1.0 - sq)))
            di_ref[p["rows"], :] = p["div"] + _dot_nt(p["keexp"], ncat)
            return _colsum(dforget * (1.0 - sg))

        def step(i, carry):
            parts = [local(nsc - 1 - (i * HGRN_SIDE + u), u) for u in range(HGRN_SIDE)]
            dst = dstate[...]
            chained = []
            for p in parts:
                nxt = [None] * NC
                ddec = [None] * NC
                for c in reversed(range(NC)):
                    nxt[c] = dst
                    ddec[c] = _colsum(dst * p["sts"][c])
                    dst = dst * p["dec"][c] + p["gt"][:, c * HGRN_DIM:(c + 1) * HGRN_DIM]
                chained.append((nxt, ddec))
            dstate[...] = dst
            dlb = dlb_ref[...]
            for p, (nxt, ddec) in zip(parts, chained):
                dlb = dlb + finish(p, nxt, ddec)
            dlb_ref[...] = dlb
            return carry

        lax.fori_loop(0, nsc // HGRN_SIDE, step, 0)

    col = lambda off: pl.BlockSpec((T, HGRN_DIM), lambda h: (0, off + h))
    own = pl.BlockSpec((T, HGRN_DIM), lambda h: (0, h))
    vec = pl.BlockSpec((1, HGRN_DIM), lambda h: (0, h))
    return pl.pallas_call(
        body, name="hgrn_bwd", grid=(HGRN_HEADS,),
        in_specs=[col(0), col(4), col(8), vec,
                  pl.BlockSpec((1, nsc, HGRN_DIM, HGRN_DIM), lambda h: (h, 0, 0, 0)), own],
        out_specs=[own, own, own, vec],
        out_shape=[jax.ShapeDtypeStruct((T, HGRN_W), F32)] * 3 + [jax.ShapeDtypeStruct((1, HGRN_W), F32)],
        scratch_shapes=[pltpu.VMEM((HGRN_DIM, HGRN_DIM), F32), pltpu.VMEM((HGRN_SIDE, SUPER, HGRN_DIM), F32)],
        compiler_params=_cp("arbitrary"),
    )(hg, hg, hg, lb, states, do)


def _rec_heads(rec, gate, g_h):
    rr = jnp.concatenate(
        [jnp.broadcast_to(_rms(rec[:, h * HGRN_DIM:(h + 1) * HGRN_DIM], HGRN_DIM), (rec.shape[0], HGRN_DIM))
         for h in range(HGRN_HEADS)], axis=1)
    rn = rec * rr
    sg = _sigmoid(gate)
    return rr, rn, sg


def _mix_out(attn_o, rec_o, hg, x, g_a, g_h, w_out, tm=512):
    T = x.shape[0]

    def body(a_ref, r_ref, gt_ref, x_ref, ga_ref, gh_ref, w_ref, h1_ref, mixed_ref):
        a = a_ref[...]
        an = a * _rms(a, ATTN_W) * ga_ref[...]
        gate = gt_ref[...]
        _, rn, sg = _rec_heads(r_ref[...], gate, gh_ref[...])
        mixed = jnp.concatenate([an, rn * gh_ref[...] * (gate * sg)], axis=1).astype(BF16)
        mixed_ref[...] = mixed
        h1_ref[...] = x_ref[...] + _dot(mixed, w_ref[...])

    row = lambda w: pl.BlockSpec((tm, w), lambda i: (i, 0))
    return pl.pallas_call(
        body, name="mix_out", grid=(T // tm,),
        in_specs=[row(ATTN_W), row(HGRN_W), pl.BlockSpec((tm, HGRN_W), lambda i: (i, 3)), row(D_MODEL),
                  _full((1, ATTN_W)), _full((1, HGRN_W)), _once((D_MODEL, D_MODEL))],
        out_specs=[row(D_MODEL), row(D_MODEL)],
        out_shape=[jax.ShapeDtypeStruct((T, D_MODEL), F32), jax.ShapeDtypeStruct((T, D_MODEL), BF16)],
        compiler_params=_cp("arbitrary"),
    )(attn_o, rec_o, hg, x, g_a, g_h, w_out)


_INV_SQRT2 = 1.0 / math.sqrt(2.0)
_INV_SQRT2PI = 1.0 / math.sqrt(2.0 * math.pi)


def _gelu(x):
    return 0.5 * x * (1.0 + lax.erf(x * _INV_SQRT2))


def _gelu_grad(x):
    return 0.5 * (1.0 + lax.erf(x * _INV_SQRT2)) + x * jnp.exp(-0.5 * x * x) * _INV_SQRT2PI


def _shift_down(g, prev, rowid):
    p1 = _row(prev, prev.shape[0] - 1)
    p2 = _row(prev, prev.shape[0] - 2)
    s1 = jnp.where(rowid == 0, p1, pltpu.roll(g, 1, 0))
    s2 = jnp.where(rowid == 0, p2, jnp.where(rowid == 1, p1, pltpu.roll(g, 2, 0)))
    return s1, s2


def _mlp_fwd(h1, g2, w_up4, conv_w, conv_b, w_down, gf, tgt, tm=256):
    T = h1.shape[0]
    half = D_FF // 2

    def body(h_ref, g2_ref, wu_ref, cw_ref, cb_ref, wd_ref, gf_ref, t_ref,
             u_ref, gate_ref, val_ref, conv_ref, act_ref, dh_ref, loss_ref, dgf_ref, carry):
        i = pl.program_id(0)

        @pl.when(i == 0)
        def _():
            carry[...] = jnp.zeros_like(carry)
            loss_ref[...] = jnp.zeros_like(loss_ref)
            dgf_ref[...] = jnp.zeros_like(dgf_ref)

        h = h_ref[...]
        u = (h * _rms(h, D_MODEL) * g2_ref[...]).astype(BF16)
        u_ref[...] = u
        rowid = lax.broadcasted_iota(jnp.int32, (tm, half), 0)
        y2 = jnp.zeros((tm, D_MODEL), F32)
        for c in range(2):
            cols = slice(c * half, (c + 1) * half)
            gb = _dot(u, wu_ref[c]).astype(BF16)
            vb = _dot(u, wu_ref[2 + c]).astype(BF16)
            gate_ref[:, cols] = gb
            val_ref[:, cols] = vb
            g = gb.astype(F32)
            s1, s2 = _shift_down(g, carry[:, cols], rowid)
            carry[:, cols] = g[tm - 8:, :]
            conv = cb_ref[:, cols] + cw_ref[0:1, cols] * s2 + cw_ref[1:2, cols] * s1 + cw_ref[2:3, cols] * g
            act = (_gelu(conv) * vb.astype(F32)).astype(BF16)
            conv_ref[:, cols] = conv.astype(BF16)
            act_ref[:, cols] = act
            y2 = y2 + _dot(act, wd_ref[cols, :])
        h2 = h + y2
        rf = _rms(h2, D_MODEL)
        n = h2 * rf
        gfv = gf_ref[...]
        e = n * gfv - t_ref[...]
        loss_ref[...] += jnp.sum(e * e) * (0.5 / D_MODEL)
        dy = e * (1.0 / D_MODEL)
        dgf_ref[...] += _colsum(dy * n)
        dh_ref[...] = _rms_bwd(dy * gfv, n, rf, D_MODEL)

    row = lambda w: pl.BlockSpec((tm, w), lambda i: (i, 0))
    return pl.pallas_call(
        body, name="mlp_fwd", grid=(T // tm,),
        in_specs=[row(D_MODEL), _full((1, D_MODEL)), _once((N_CHIPS, D_MODEL, UP_SHARD)), _full((3, D_FF)),
                  _full((1, D_FF)), _once((D_FF, D_MODEL)), _full((1, D_MODEL)), row(D_MODEL)],
        out_specs=[row(D_MODEL), row(D_FF), row(D_FF), row(D_FF), row(D_FF), row(D_MODEL), _full((1, 128)),
                   _full((1, D_MODEL))],
        out_shape=[jax.ShapeDtypeStruct((T, D_MODEL), BF16)] + [jax.ShapeDtypeStruct((T, D_FF), BF16)] * 4
        + [jax.ShapeDtypeStruct((T, D_MODEL), F32),
                   jax.ShapeDtypeStruct((1, 128), F32), jax.ShapeDtypeStruct((1, D_MODEL), F32)],
        scratch_shapes=[pltpu.VMEM((8, D_FF), F32)],
        compiler_params=_cp("arbitrary"),
    )(h1, g2, w_up4, conv_w, conv_b, w_down, gf, tgt)


def _mlp_bwd(dh2, gate, val, conv, conv_w, w_down, tm=256):
    T = dh2.shape[0]
    nb = T // tm
    half = D_FF // 2

    def body(dh_ref, gate_ref, val_ref, conv_ref, cw_ref, wd_ref, dgv_ref, dcw_ref, dcb_ref, carry):
        @pl.when(pl.program_id(0) == 0)
        def _():
            carry[...] = jnp.zeros_like(carry)
            dcw_ref[...] = jnp.zeros_like(dcw_ref)
            dcb_ref[...] = jnp.zeros_like(dcb_ref)

        dhb = dh_ref[...].astype(BF16)
        rowid = lax.broadcasted_iota(jnp.int32, (tm, half), 0)
        for c in range(2):
            cols = slice(c * half, (c + 1) * half)
            g = gate_ref[:, cols].astype(F32)
            v = val_ref[:, cols].astype(F32)
            cv = conv_ref[:, cols].astype(F32)
            dact = _dot_nt(dhb, wd_ref[cols, :])
            dconv = dact * v * _gelu_grad(cv)
            nxt = carry[:, cols]
            n0, n1 = _row(nxt, 0), _row(nxt, 1)
            u1 = jnp.where(rowid == tm - 1, n0, pltpu.roll(dconv, tm - 1, 0))
            u2 = jnp.where(rowid == tm - 1, n1, jnp.where(rowid == tm - 2, n0, pltpu.roll(dconv, tm - 2, 0)))
            carry[:, cols] = dconv[0:8, :]
            dcb_ref[:, cols] += _colsum(dconv)
            dcw_ref[0:1, cols] += _colsum(u2 * g)
            dcw_ref[1:2, cols] += _colsum(u1 * g)
            dcw_ref[2:3, cols] += _colsum(dconv * g)
            dgate = cw_ref[2:3, cols] * dconv + cw_ref[1:2, cols] * u1 + cw_ref[0:1, cols] * u2
            dgv_ref[:, cols] = dgate.astype(BF16)
            dgv_ref[:, D_FF + c * half:D_FF + (c + 1) * half] = (dact * _gelu(cv)).astype(BF16)

    rev = lambda w: pl.BlockSpec((tm, w), lambda i: (nb - 1 - i, 0))
    return pl.pallas_call(
        body, name="mlp_bwd", grid=(nb,),
        in_specs=[rev(D_MODEL), rev(D_FF), rev(D_FF), rev(D_FF), _full((3, D_FF)), _once((D_FF, D_MODEL))],
        out_specs=[rev(2 * D_FF), _full((3, D_FF)), _full((1, D_FF))],
        out_shape=[jax.ShapeDtypeStruct((T, 2 * D_FF), BF16), jax.ShapeDtypeStruct((3, D_FF), F32),
                   jax.ShapeDtypeStruct((1, D_FF), F32)],
        scratch_shapes=[pltpu.VMEM((8, D_FF), F32)],
        compiler_params=_cp("arbitrary"),
    )(dh2, gate, val, conv, conv_w, w_down)


def _up_out_bwd(dgv, w_up4, h1, g2, dh2, w_out, attn_o, rec_o, hg, g_a, g_h, tm=256):
    T = h1.shape[0]

    def body(dgv_ref, wu_ref, h_ref, g2_ref, dh2_ref, wo_ref, a_ref, r_ref, gt_ref, ga_ref, gh_ref,
             dh1_ref, dg2_ref, da_ref, dr_ref, dgt_ref, dga_ref, dgh_ref):
        @pl.when(pl.program_id(0) == 0)
        def _():
            dg2_ref[...] = jnp.zeros_like(dg2_ref)
            dga_ref[...] = jnp.zeros_like(dga_ref)
            dgh_ref[...] = jnp.zeros_like(dgh_ref)

        du = jnp.zeros((tm, D_MODEL), F32)
        for k in range(N_CHIPS):
            du = du + _dot_nt(dgv_ref[:, k * UP_SHARD:(k + 1) * UP_SHARD], wu_ref[k])
        h = h_ref[...]
        r = _rms(h, D_MODEL)
        n = h * r
        dg2_ref[...] += _colsum(du * n)
        dh1 = dh2_ref[...] + _rms_bwd(du * g2_ref[...], n, r, D_MODEL)
        dh1_ref[...] = dh1
        dmix = _dot_nt(dh1.astype(BF16), wo_ref[...])
        dan = dmix[:, :ATTN_W]
        a = a_ref[...]
        ra = _rms(a, ATTN_W)
        na = a * ra
        dga_ref[...] += _colsum(dan * na)
        da_ref[...] = _rms_bwd(dan * ga_ref[...], na, ra, ATTN_W)
        dmr = dmix[:, ATTN_W:]
        gate = gt_ref[...]
        ghv = gh_ref[...]
        rr, rn, sg = _rec_heads(r_ref[...], gate, ghv)
        dgt_ref[...] = dmr * rn * ghv * (sg * (1.0 + gate * (1.0 - sg)))
        drecn = dmr * (gate * sg)
        dgh_ref[...] += _colsum(drecn * rn)
        drn = drecn * ghv
        prod = drn * rn
        mean = jnp.concatenate(
            [jnp.broadcast_to(jnp.sum(prod[:, h_ * HGRN_DIM:(h_ + 1) * HGRN_DIM], axis=-1, keepdims=True),
                              (tm, HGRN_DIM)) for h_ in range(HGRN_HEADS)], axis=1) * (1.0 / HGRN_DIM)
        dr_ref[...] = rr * (drn - rn * mean)

    row = lambda w: pl.BlockSpec((tm, w), lambda i: (i, 0))
    return pl.pallas_call(
        body, name="up_out_bwd", grid=(T // tm,),
        in_specs=[row(2 * D_FF), _once((N_CHIPS, D_MODEL, UP_SHARD)), row(D_MODEL), _full((1, D_MODEL)),
                  row(D_MODEL), _once((D_MODEL, D_MODEL)), row(ATTN_W), row(HGRN_W),
                  pl.BlockSpec((tm, HGRN_W), lambda i: (i, 3)), _full((1, ATTN_W)), _full((1, HGRN_W))],
        out_specs=[row(D_MODEL), _full((1, D_MODEL)), row(ATTN_W), row(HGRN_W), row(HGRN_W),
                   _full((1, ATTN_W)), _full((1, HGRN_W))],
        out_shape=[jax.ShapeDtypeStruct((T, D_MODEL), F32), jax.ShapeDtypeStruct((1, D_MODEL), F32),
                   jax.ShapeDtypeStruct((T, ATTN_W), F32), jax.ShapeDtypeStruct((T, HGRN_W), F32),
                   jax.ShapeDtypeStruct((T, HGRN_W), F32), jax.ShapeDtypeStruct((1, ATTN_W), F32),
                   jax.ShapeDtypeStruct((1, HGRN_W), F32)],
        compiler_params=_cp("arbitrary"),
    )(dgv, w_up4, h1, g2, dh2, w_out, attn_o, rec_o, hg, g_a, g_h)


def _in_bwd(dqkv, dhg, w_in4, x, g1, dh1, tm=256):
    T = x.shape[0]

    def body(*refs):
        parts = refs[:7]
        w_ref, x_ref, g_ref, dh1_ref, dp_ref, dx_ref, dg_ref = refs[7:]

        @pl.when(pl.program_id(0) == 0)
        def _():
            dg_ref[...] = jnp.zeros_like(dg_ref)

        dp = jnp.concatenate([p[...] for p in parts], axis=1).astype(BF16)
        dp_ref[...] = dp
        du = jnp.zeros((tm, D_MODEL), F32)
        for k in range(N_CHIPS):
            du = du + _dot_nt(dp[:, k * IN_SHARD:(k + 1) * IN_SHARD], w_ref[k])
        xv = x_ref[...]
        r = _rms(xv, D_MODEL)
        n = xv * r
        dg_ref[...] += _colsum(du * n)
        dx_ref[...] = dh1_ref[...] + _rms_bwd(du * g_ref[...], n, r, D_MODEL)

    row = lambda w: pl.BlockSpec((tm, w), lambda i: (i, 0))
    return pl.pallas_call(
        body, name="in_bwd", grid=(T // tm,),
        in_specs=[row(ATTN_W)] * 7 + [_once((N_CHIPS, D_MODEL, IN_SHARD)), row(D_MODEL), _full((1, D_MODEL)),
                                       row(D_MODEL)],
        out_specs=[row(IN_TOTAL), row(D_MODEL), _full((1, D_MODEL))],
        out_shape=[jax.ShapeDtypeStruct((T, IN_TOTAL), BF16), jax.ShapeDtypeStruct((T, D_MODEL), F32),
                   jax.ShapeDtypeStruct((1, D_MODEL), F32)],
        compiler_params=_cp("arbitrary"),
    )(*dqkv, *dhg, w_in4, x, g1, dh1)


def _dw(a, b, kb, nb_, name, tk=1024):
    T, K = a.shape
    N = b.shape[1]
    nk, nn, nt = K // kb, N // nb_, T // tk

    def body(a_ref, b_ref, o_ref, acc):
        t = pl.program_id(2)

        @pl.when(t == 0)
        def _():
            acc[...] = jnp.zeros_like(acc)

        acc[...] += _dot_tn(a_ref[...], b_ref[...].astype(BF16))

        @pl.when(t == nt - 1)
        def _():
            o_ref[0] = acc[...].astype(BF16)

    return pl.pallas_call(
        body, name=name, grid=(nk, nn, nt),
        in_specs=[pl.BlockSpec((tk, kb), lambda i, j, t: (t, i)), pl.BlockSpec((tk, nb_), lambda i, j, t: (t, j))],
        out_specs=pl.BlockSpec((1, kb, nb_), lambda i, j, t: (i * nn + j, 0, 0)),
        out_shape=jax.ShapeDtypeStruct((nk * nn, kb, nb_), BF16),
        scratch_shapes=[pltpu.VMEM((kb, nb_), F32)],
        compiler_params=_cp("arbitrary", "arbitrary", "arbitrary"),
    )(a, b)


def _local_step(x, tgt, g1, w_in4, g_a, g_h, lb, w_out, g2, w_up4, conv_w, conv_b, w_down, gf):
    a = _step_mixers(x, g1, w_in4, lb)
    b = _step_channel(a, x, tgt, g_a, g_h, w_out, g2, w_up4, conv_w, conv_b, w_down, gf)
    c = _step_mixers_bwd(a, b, x, g1, w_in4, lb)
    small = dict(g1=c["dg1"], g_a=b["dga"], g_h=b["dgh"], lb=c["dlb"], g2=b["dg2"], conv_w=b["dcw"], conv_b=b["dcb"],
                 gf=b["dgf"])
    return b["loss"], c["dx"], small, dict(w_in=c["dw_in"], w_out=b["dw_out"], w_up=b["dw_up"], w_down=b["dw_down"])


def _step_mixers(x, g1, w_in4, lb):
    u1, qkv, hg = _in_proj(x, g1, w_in4)
    attn_o, lse = _attn_fwd(qkv)
    rec_o, states = _hgrn_fwd(hg, lb)
    return dict(u1=u1, qkv=qkv, hg=hg, attn_o=attn_o, lse=lse, rec_o=rec_o, states=states)


def _step_channel(a, x, tgt, g_a, g_h, w_out, g2, w_up4, conv_w, conv_b, w_down, gf):
    h1, mixed = _mix_out(a["attn_o"], a["rec_o"], a["hg"], x, g_a, g_h, w_out)
    u2, gate, val, conv, act, dh2, loss, dgf = _mlp_fwd(h1, g2, w_up4, conv_w, conv_b, w_down, gf, tgt)
    dgv, dcw, dcb = _mlp_bwd(dh2, gate, val, conv, conv_w, w_down)
    dw_down = _dw(act, dh2, D_FF // 2, D_MODEL, "dw_down").reshape(N_CHIPS, D_FF // N_CHIPS, D_MODEL)
    dh1, dg2, da, dr, dgt, dga, dgh = _up_out_bwd(dgv, w_up4, h1, g2, dh2, w_out, a["attn_o"], a["rec_o"], a["hg"],
                                                  g_a, g_h)
    dw_up = _dw(u2, dgv, D_MODEL, UP_SHARD, "dw_up")
    dw_out = _dw(mixed, dh1, D_MODEL, D_MODEL, "dw_out").reshape(N_CHIPS, D_MODEL // N_CHIPS, D_MODEL)
    return dict(loss=loss, dgf=dgf, dcw=dcw, dcb=dcb, dg2=dg2, dga=dga, dgh=dgh, dh1=dh1, da=da, dr=dr, dgt=dgt,
                dw_down=dw_down, dw_up=dw_up, dw_out=dw_out)


def _step_mixers_bwd(a, b, x, g1, w_in4, lb, dqkv=None):
    if dqkv is None:
        dqkv = _attn_bwd(a["qkv"], a["attn_o"], a["lse"], b["da"])
    dhq, dhf, dhi, dlb = _hgrn_bwd(a["hg"], lb, a["states"], b["dr"])
    dproj, dx, dg1 = _in_bwd(dqkv, [dhq, dhf, dhi, b["dgt"]], w_in4, x, g1, b["dh1"])
    dw_in = _dw(a["u1"], dproj, D_MODEL, IN_SHARD, "dw_in")
    return dict(dx=dx, dg1=dg1, dlb=dlb, dw_in=dw_in)


BIG = ("w_in", "w_out", "w_up", "w_down")
ANY = pl.BlockSpec(memory_space=pl.ANY)


def _place():
    x, y, c = lax.axis_index("x"), lax.axis_index("y"), lax.axis_index("c")
    chips = [(1 - x, y), (x, 1 - y), (1 - x, 1 - y)]
    return x, y, c, chips


def _remote(src, dst, send_sems, recv_sems, k, to):
    return pltpu.make_async_remote_copy(src_ref=src, dst_ref=dst, send_sem=send_sems.at[k], recv_sem=recv_sems.at[k],
                                        device_id=to, device_id_type=MESH)


def _gather_weights(shards, conv_w):
    n = len(shards)
    halves = [s.shape[0] // 2 for s in shards]

    def body(*refs):
        ins, cw, outs, ocw = refs[:n], refs[n], refs[n + 1:2 * n + 1], refs[2 * n + 1]
        send_sems, recv_sems = refs[2 * n + 2:]
        x, y, c, chips = _place()
        me, sibling = 2 * x + y, (x, y, 1 - c)

        def part(w, chip, half):
            return outs[w].at[chip, pl.ds(half * halves[w], halves[w]), :]

        sent = []
        for j, chip in enumerate(chips):
            for w in range(n):
                sent.append(_remote(ins[w].at[pl.ds(c * halves[w], halves[w]), :], part(w, me, c),
                                    send_sems, recv_sems, w * 3 + j, (*chip, c)))
            sent.append(_remote(cw, ocw.at[me], send_sems, recv_sems, 6 * n + j, (*chip, c)))
        for cp in sent:
            cp.start()
        for j, chip in enumerate(chips):
            kj = 2 * chip[0] + chip[1]
            for w in range(n):
                _remote(part(w, kj, c), part(w, kj, c), send_sems, recv_sems, w * 3 + j, (*chip, c)).wait_recv()
                fwd = _remote(part(w, kj, c), part(w, kj, c), send_sems, recv_sems, 3 * n + w * 3 + j, sibling)
                fwd.start()
                sent.append(fwd)
        for j, chip in enumerate(chips):
            kj = 2 * chip[0] + chip[1]
            for w in range(n):
                _remote(part(w, kj, 1 - c), part(w, kj, 1 - c), send_sems, recv_sems, 3 * n + w * 3 + j,
                        sibling).wait_recv()
            _remote(cw, ocw.at[kj], send_sems, recv_sems, 6 * n + j, (*chip, c)).wait_recv()
        for cp in sent:
            cp.wait_send()

    n_sem = 6 * n + 3
    outs = pl.pallas_call(
        body, name="gather_weights",
        in_specs=[ANY] * (n + 1), out_specs=[ANY] * (n + 1),
        out_shape=[jax.ShapeDtypeStruct((N_CHIPS,) + s.shape, s.dtype) for s in shards]
        + [jax.ShapeDtypeStruct((N_CHIPS,) + conv_w.shape, conv_w.dtype)],
        scratch_shapes=[pltpu.SemaphoreType.DMA((n_sem,)), pltpu.SemaphoreType.DMA((n_sem,))],
    )(*shards, conv_w)
    chip = 2 * lax.axis_index("x") + lax.axis_index("y")
    return [lax.dynamic_update_slice(o, s[None], (chip,) + (0,) * s.ndim) for o, s in zip(outs, [*shards, conv_w])]


def _allreduce_small(buf):
    rows = buf.shape[0]

    def body(in_ref, out_ref, slots, send_sems, recv_sems):
        x, y, c, _ = _place()
        me = 4 * x + 2 * y + c
        slots[me] = in_ref[...]
        sent = []
        for p in range(1, 8):
            to = (x ^ (p >> 2), y ^ ((p >> 1) & 1), c ^ (p & 1))
            sent.append(_remote(in_ref, slots.at[me], send_sems, recv_sems, p, to))
        for cp in sent:
            cp.start()
        for p in range(1, 8):
            frm = 4 * (x ^ (p >> 2)) + 2 * (y ^ ((p >> 1) & 1)) + (c ^ (p & 1))
            _remote(in_ref, slots.at[frm], send_sems, recv_sems, p, (x, y, c)).wait_recv()
        for cp in sent:
            cp.wait_send()
        acc = slots[0]
        for d in range(1, 8):
            acc = acc + slots[d]
        out_ref[...] = acc

    vm = pl.BlockSpec(memory_space=pltpu.VMEM)
    return pl.pallas_call(
        body, name="allreduce_small", in_specs=[vm], out_specs=vm,
        out_shape=jax.ShapeDtypeStruct(buf.shape, F32),
        scratch_shapes=[pltpu.VMEM((8, rows, 128), F32), pltpu.SemaphoreType.DMA((8,)), pltpu.SemaphoreType.DMA((8,))],
    )(buf)


def _pair_exchange(gs, name):
    n = len(gs)
    halves = [g.shape[1] // 2 for g in gs]

    def body(*refs):
        g, got = refs[:n], refs[n:2 * n]
        send_sems, recv_sems = refs[2 * n:]
        x, y, c, _ = _place()
        cps = [_remote(g[w].at[:, pl.ds((1 - c) * halves[w], halves[w]), :], got[w], send_sems, recv_sems, w,
                       (x, y, 1 - c)) for w in range(n)]
        for cp in cps:
            cp.start()
        for cp in cps:
            cp.wait()

    return pl.pallas_call(
        body, name=name, in_specs=[ANY] * n, out_specs=[ANY] * n,
        out_shape=[jax.ShapeDtypeStruct((N_CHIPS, h, g.shape[2]), g.dtype) for g, h in zip(gs, halves)],
        scratch_shapes=[pltpu.SemaphoreType.DMA((n,)), pltpu.SemaphoreType.DMA((n,))],
    )(*gs)


def _core_id():
    return lax.axis_index("c").reshape(1).astype(jnp.int32)


def _pair_sum(g, got, name):
    h, C = got.shape[1:]

    def body(c_ref, g_ref, b_ref, o_ref):
        o_ref[...] = (g_ref[...].astype(F32) + b_ref[...].astype(F32)).astype(BF16)

    blk = pl.BlockSpec((1, h, C), lambda k, c_ref: (k, 0, 0))
    return pl.pallas_call(
        body, name=name,
        grid_spec=pltpu.PrefetchScalarGridSpec(
            num_scalar_prefetch=1, grid=(N_CHIPS,),
            in_specs=[pl.BlockSpec((1, h, C), lambda k, c_ref: (k, c_ref[0], 0)), blk], out_specs=blk),
        out_shape=jax.ShapeDtypeStruct(got.shape, BF16), compiler_params=_cp("arbitrary"))(_core_id(), g, got)


def _sum_partials(g, got, landed, name):
    h, C = got.shape[1:]

    def body(ids, g_ref, b_ref, l_ref, o_ref):
        acc = g_ref[0].astype(F32) + b_ref[0].astype(F32)
        for j in range(3):
            acc = acc + l_ref[j].astype(F32)
        o_ref[...] = acc

    ids = jnp.stack([2 * lax.axis_index("x") + lax.axis_index("y"), lax.axis_index("c")]).astype(jnp.int32)
    return pl.pallas_call(
        body, name=name,
        grid_spec=pltpu.PrefetchScalarGridSpec(
            num_scalar_prefetch=1, grid=(1,),
            in_specs=[pl.BlockSpec((1, h, C), lambda i, ids: (ids[0], ids[1], 0)),
                      pl.BlockSpec((1, h, C), lambda i, ids: (ids[0], 0, 0)),
                      pl.BlockSpec((3, h, C), lambda i, ids: (0, 0, 0))],
            out_specs=pl.BlockSpec((h, C), lambda i, ids: (ids[1], 0))),
        out_shape=jax.ShapeDtypeStruct((2 * h, C), F32), compiler_params=_cp("arbitrary"))(ids, g, got, landed)


def _pair_share(reds, name):
    n = len(reds)

    def body(*refs):
        out = refs[n:2 * n]
        send_sems, recv_sems = refs[2 * n:]
        x, y, c, _ = _place()
        def half(w, which):
            h = out[w].shape[0] // 2
            return out[w].at[pl.ds(which * h, h), :]

        cps = [_remote(half(w, c), half(w, c), send_sems, recv_sems, w, (x, y, 1 - c)) for w in range(n)]
        for cp in cps:
            cp.start()
        for w in range(n):
            _remote(half(w, 1 - c), half(w, 1 - c), send_sems, recv_sems, w, (x, y, 1 - c)).wait_recv()
        for cp in cps:
            cp.wait_send()

    return pl.pallas_call(
        body, name=name, in_specs=[ANY] * n, out_specs=[ANY] * n,
        out_shape=[jax.ShapeDtypeStruct(r.shape, F32) for r in reds],
        input_output_aliases={w: w for w in range(n)},
        scratch_shapes=[pltpu.SemaphoreType.DMA((n,)), pltpu.SemaphoreType.DMA((n,))],
    )(*reds)


HBM = pl.BlockSpec(memory_space=pltpu.HBM)
SEM = pl.BlockSpec(memory_space=pltpu.SEMAPHORE)
DATAFLOW = pltpu.SideEffectType.DATAFLOW_SIDE_EFFECTING


def _copies_start(name, srcs, lands, plan, n_copies, after):
    ns, nb, na = len(srcs), len(srcs) + len(lands), len(after)

    def body(*refs):
        src_refs, land_refs = refs[:ns], refs[ns:nb]
        send_sems, recv_sems = refs[nb + na:nb + na + 2]
        token = refs[-1]
        for k, (src, there, _, to) in enumerate(plan(src_refs, land_refs)):
            _remote(src, there, send_sems, recv_sems, k, to).start()
        token[...] = jnp.zeros_like(token)

    hbm = lambda a: pltpu.HBM(a.shape, a.dtype)
    outs = pl.pallas_call(
        body, name=name,
        out_shape=(pltpu.SemaphoreType.DMA((n_copies,)), pltpu.SemaphoreType.DMA((n_copies,)),
                   *[hbm(a) for a in srcs], *[hbm(a) for a in lands], jax.ShapeDtypeStruct((8, 128), F32)),
        in_specs=[HBM] * nb + [ANY] * na,
        out_specs=(SEM, SEM, *[HBM] * nb, pl.BlockSpec(memory_space=pltpu.VMEM)),
        input_output_aliases={i: 2 + i for i in range(nb)},
        compiler_params=pltpu.CompilerParams(has_side_effects=DATAFLOW),
    )(*[pltpu.with_memory_space_constraint(a, pltpu.HBM) for a in (*srcs, *lands)], *after)
    return outs[0], outs[1], outs[2:2 + ns], outs[2 + ns:2 + nb], outs[-1]


def _copies_wait(name, send_sems, recv_sems, srcs, lands, plan, after):
    ns, nb, na = len(srcs), len(srcs) + len(lands), len(after)

    def body(*refs):
        src_refs, land_refs = refs[:ns], refs[ns:nb]
        send_sems, recv_sems = refs[nb:nb + 2]
        for k, (src, _, here, to) in enumerate(plan(src_refs, land_refs)):
            cp = _remote(src, here, send_sems, recv_sems, k, to)
            cp.wait_send()
            cp.wait_recv()

    hbm = lambda a: pltpu.HBM(a.shape, a.dtype)
    outs = pl.pallas_call(
        body, name=name,
        out_shape=(*[hbm(a) for a in srcs], *[hbm(a) for a in lands]),
        in_specs=[HBM] * nb + [SEM, SEM] + [ANY] * na,
        out_specs=tuple([HBM] * nb),
        input_output_aliases={i: i for i in range(nb)},
        compiler_params=pltpu.CompilerParams(has_side_effects=DATAFLOW),
    )(*srcs, *lands, send_sems, recv_sems, *after)
    return outs[:ns], outs[ns:]


def _gather_plan(halves):
    def plan(shards, lands):
        x, y, c, chips = _place()
        me = 2 * x + y
        copies = []
        for w, h in enumerate(halves):
            rows = pl.ds(c * h, h)
            for chip in chips:
                copies.append((shards[w].at[rows, :], lands[w].at[me, rows, :],
                               lands[w].at[2 * chip[0] + chip[1], rows, :], (*chip, c)))
        return copies
    return plan


def _reduce_plan(n):
    def plan(ps, lands):
        x, y, c, chips = _place()
        return [(ps[w].at[2 * chip[0] + chip[1]], lands[w].at[j], lands[w].at[j], (*chip, c))
                for w in range(n) for j, chip in enumerate(chips)]
    return plan


def _forward_plan(halves):
    def plan(_, lands):
        x, y, c, chips = _place()

        def part(w, chip, half):
            return lands[w].at[2 * chip[0] + chip[1], pl.ds(half * halves[w], halves[w]), :]

        return [(part(w, chip, c), part(w, chip, c), part(w, chip, 1 - c), (x, y, 1 - c))
                for w in range(len(halves)) for chip in chips]
    return plan


def _pair_plan(halves):
    def plan(gs, gots):
        x, y, c, _ = _place()
        return [(gs[w].at[:, pl.ds((1 - c) * h, h), :], gots[w], gots[w], (x, y, 1 - c)) for w, h in enumerate(halves)]
    return plan


def _place_own(gathered, shards):
    chip = 2 * lax.axis_index("x") + lax.axis_index("y")
    return [lax.dynamic_update_slice(o, s[None], (chip, 0, 0)) for o, s in zip(gathered, shards)]


def _adamw(w, g, m, v, name, tr=None):
    R, C = w.shape
    tr = tr or R // 4

    def body(w_ref, g_ref, m_ref, v_ref, d_ref, nm_ref, nv_ref):
        gv = g_ref[...]
        nm = ADAM_B1 * m_ref[...] + (1.0 - ADAM_B1) * gv
        nv = ADAM_B2 * v_ref[...] + (1.0 - ADAM_B2) * (gv * gv)
        m_hat = nm / (1.0 - ADAM_B1 ** ADAM_STEP)
        v_hat = nv / (1.0 - ADAM_B2 ** ADAM_STEP)
        d_ref[...] = -ADAM_LR * (m_hat / (jnp.sqrt(v_hat) + ADAM_EPS) + ADAM_WD * w_ref[...])
        nm_ref[...] = nm
        nv_ref[...] = nv

    blk = pl.BlockSpec((tr, C), lambda i: (i, 0))
    return pl.pallas_call(body, name=name, grid=(R // tr,), in_specs=[blk] * 4, out_specs=[blk] * 3,
                          out_shape=[jax.ShapeDtypeStruct((R, C), F32)] * 3, compiler_params=_cp("arbitrary"))(w, g, m, v)


SMALL = (("norm1_g", 1024), ("attn_norm_g", 512), ("hgrn_norm_g", 512), ("hgrn_lb_logits", 1024), ("norm2_g", 1024),
         ("conv_b", D_FF), ("final_norm_g", 1024), ("conv_w", 3 * D_FF))
SMALL_ROWS = 136


def _pack(parts, rows):
    flat = jnp.concatenate([p.reshape(-1).astype(F32) for p in parts])
    return jnp.pad(flat, (0, rows * 128 - flat.shape[0])).reshape(rows, 128)


def kernel(x, norm1_g, w_in, attn_norm_g, hgrn_norm_g, hgrn_lb_logits, w_out, norm2_g, w_up, conv_w, conv_b, w_down, final_norm_g, loss_target, m_norm1_g, m_w_in, m_attn_norm_g, m_hgrn_norm_g, m_hgrn_lb_logits, m_w_out, m_norm2_g, m_w_up, m_conv_w, m_conv_b, m_w_down, m_final_norm_g, v_norm1_g, v_w_in, v_attn_norm_g, v_hgrn_norm_g, v_hgrn_lb_logits, v_w_out, v_norm2_g, v_w_up, v_conv_w, v_conv_b, v_w_down, v_final_norm_g):
    w = dict(norm1_g=norm1_g, w_in=w_in, attn_norm_g=attn_norm_g, hgrn_norm_g=hgrn_norm_g,
             hgrn_lb_logits=hgrn_lb_logits, w_out=w_out, norm2_g=norm2_g, w_up=w_up, conv_w=conv_w, conv_b=conv_b,
             w_down=w_down, final_norm_g=final_norm_g)
    m = dict(norm1_g=m_norm1_g, w_in=m_w_in, attn_norm_g=m_attn_norm_g, hgrn_norm_g=m_hgrn_norm_g,
             hgrn_lb_logits=m_hgrn_lb_logits, w_out=m_w_out, norm2_g=m_norm2_g, w_up=m_w_up, conv_w=m_conv_w,
             conv_b=m_conv_b, w_down=m_w_down, final_norm_g=m_final_norm_g)
    v = dict(norm1_g=v_norm1_g, w_in=v_w_in, attn_norm_g=v_attn_norm_g, hgrn_norm_g=v_hgrn_norm_g,
             hgrn_lb_logits=v_hgrn_lb_logits, w_out=v_w_out, norm2_g=v_norm2_g, w_up=v_w_up, conv_w=v_conv_w,
             conv_b=v_conv_b, w_down=v_w_down, final_norm_g=v_final_norm_g)
    names = list(w)
    chip = 2 * lax.axis_index("x") + lax.axis_index("y")

    shards = {k: w[k][0].astype(BF16) for k in BIG}
    w_in4, conv_w4 = _gather_weights([shards["w_in"]], conv_w[0])
    conv_w_full = jnp.transpose(conv_w4, (1, 0, 2)).reshape(3, D_FF)
    lb = jax.nn.softmax(hgrn_lb_logits, axis=0)[0:1]
    late = [shards[k] for k in BIG[1:]]
    gather_plan = _gather_plan([s.shape[0] // 2 for s in late])
    started = _copies_start("gather_start", late, [lax.empty((N_CHIPS,) + s.shape, BF16) for s in late], gather_plan,
                            3 * len(late), after=(w_in4,))
    u1, qkv, hg = _in_proj(x[0], norm1_g + started[4][0:1, 0:1], w_in4)
    attn_o, lse = _attn_fwd(qkv)
    late, landed_w = _copies_wait("gather_wait", *started[:4], gather_plan, after=(attn_o,))
    forward_plan = _forward_plan([s.shape[0] // 2 for s in late])
    started = _copies_start("forward_start", [], landed_w, forward_plan, 3 * len(late), after=())
    rec_o, states = _hgrn_fwd(hg, lb + started[4][0:1, 0:1])
    a = dict(u1=u1, qkv=qkv, hg=hg, attn_o=attn_o, lse=lse, rec_o=rec_o, states=states)
    w_out4, w_up4, w_down4 = _place_own(
        _copies_wait("forward_wait", *started[:4], forward_plan, after=(rec_o,))[1], late)

    b = _step_channel(a, x[0], loss_target[0], attn_norm_g, hgrn_norm_g, w_out4.reshape(D_MODEL, D_MODEL), norm2_g,
                      w_up4, conv_w_full, conv_b, w_down4.reshape(D_FF, D_MODEL), final_norm_g.reshape(1, D_MODEL))

    early = [b["dw_out"], b["dw_up"], b["dw_down"]]
    pair_plan = _pair_plan([gk.shape[1] // 2 for gk in early])
    started = _copies_start("pair_start", early,
                            [lax.empty((N_CHIPS, gk.shape[1] // 2, gk.shape[2]), BF16) for gk in early], pair_plan,
                            len(early), after=())
    dqkv = _attn_bwd(qkv, attn_o, lse, b["da"], started[4])
    early, gots = _copies_wait("pair_wait", *started[:4], pair_plan, after=(dqkv[0],))
    ps = [_pair_sum(gk, got, f"pair_sum_{k}") for gk, got, k in zip(early, gots, BIG[1:])]
    reduce_plan = _reduce_plan(len(ps))
    started = _copies_start("reduce_start", ps, [lax.empty((3,) + p.shape[1:], BF16) for p in ps], reduce_plan,
                            3 * len(ps), after=())
    c = _step_mixers_bwd(a, b, x[0], norm1_g, w_in4, lb + started[4][0:1, 0:1], dqkv)
    gots_in = _pair_exchange([c["dw_in"]], "pair_exchange_w_in")
    ps_in = _pair_sum(c["dw_in"], gots_in[0], "pair_sum_w_in")
    plan_in = _reduce_plan(1)
    started_in = _copies_start("reduce_start_w_in", [ps_in], [lax.empty((3,) + ps_in.shape[1:], BF16)], plan_in, 3,
                               after=())
    landed = _copies_wait("reduce_wait", *started[:4], reduce_plan, after=(started_in[4],))[1]
    reds = [_sum_partials(gk, got, l, f"sum_partials_{k}") for gk, got, l, k in zip(early, gots, landed, BIG[1:])]
    g = dict(zip(BIG[1:], _pair_share(reds, "pair_share")))
    delta, new_m, new_v = {}, {}, {}
    for k in BIG[1:]:
        delta[k], new_m[k], new_v[k] = _adamw(w[k][0], g[k], m[k][0], v[k][0], f"adamw_{k}")

    loss, dx = b["loss"], c["dx"]
    small = dict(g1=c["dg1"], g_a=b["dga"], g_h=b["dgh"], lb=c["dlb"], g2=b["dg2"], conv_w=b["dcw"], conv_b=b["dcb"],
                 gf=b["dgf"])
    dlb = small["lb"] * lb * (1.0 - lb)
    grads_small = dict(norm1_g=small["g1"], attn_norm_g=small["g_a"], hgrn_norm_g=small["g_h"],
                       hgrn_lb_logits=jnp.concatenate([dlb, -dlb], axis=0), norm2_g=small["g2"],
                       conv_b=small["conv_b"], final_norm_g=small["gf"], conv_w=small["conv_w"])
    summed = _allreduce_small(_pack([grads_small[k] for k, _ in SMALL] + [loss[0, 0:1]], SMALL_ROWS)).reshape(-1)
    off = 0
    for k, size in SMALL:
        g[k] = summed[off:off + size]
        off += size
    loss_total = summed[off]
    g["conv_w"] = lax.dynamic_slice(g["conv_w"].reshape(3, D_FF), (0, chip * (D_FF // N_CHIPS)), (3, D_FF // N_CHIPS))
    small_names = [k for k in names if k not in BIG]
    rows = 80
    packed = _adamw(_pack([w[k] for k in small_names], rows), _pack([g[k] for k in small_names], rows),
                    _pack([m[k] for k in small_names], rows), _pack([v[k] for k in small_names], rows), "adamw_small", tr=rows)
    flat = [a.reshape(-1) for a in packed]
    off = 0
    for k in small_names:
        size = w[k].size
        delta[k], new_m[k], new_v[k] = (a[off:off + size].reshape(w[k].shape) for a in flat)
        g[k] = g[k].reshape(w[k].shape)
        off += size

    landed_in = _copies_wait("reduce_wait_w_in", *started_in[:4], plan_in, after=(packed[0], delta["w_up"]))[1]
    red_in = _sum_partials(c["dw_in"], gots_in[0], landed_in[0], "sum_partials_w_in")
    g["w_in"] = _pair_share([red_in], "pair_share_w_in")[0]
    delta["w_in"], new_m["w_in"], new_v["w_in"] = _adamw(w_in[0], g["w_in"], m_w_in[0], v_w_in[0], "adamw_w_in")
    for k in BIG:
        g[k], delta[k], new_m[k], new_v[k] = g[k][None], delta[k][None], new_m[k][None], new_v[k][None]

    return (loss_total, dx[None], *[g[k] for k in names], *[delta[k] for k in names],
            *[new_m[k] for k in names], *[new_v[k] for k in names])
```

```python
import functools
import math

import jax
import jax.numpy as jnp
from jax import lax
from jax.experimental import pallas as pl
from jax.experimental.pallas import tpu as pltpu

F32 = jnp.float32
BF16 = jnp.bfloat16

D_MODEL = 1024
ATTN_W = 512
HGRN_W = 512
HEAD_PAIR = 128
ATTN_BLK = 128
DILATIONS = (1, 4, 16)
ATTN_CHAINS = 4
ATTN_CHAINS_FWD = 8
HGRN_HEADS = 4
HGRN_DIM = 128
HGRN_CHUNK = 64
SUPER = 256
HGRN_SIDE = 2
D_FF = 2816
N_CHIPS = 4
IN_TOTAL = 3584
IN_SHARD = IN_TOTAL // N_CHIPS
UP_SHARD = 2 * D_FF // N_CHIPS
QKV_W = 3 * ATTN_W
HG_W = 4 * HGRN_W
EPS = 1e-6
NEG = -1e30
V7X_VMEM_BYTES = 64 * 1024 * 1024
VMEM_LIMIT = V7X_VMEM_BYTES - 8 * 1024 * 1024

ADAM_LR = 0.001
ADAM_B1 = 0.9
ADAM_B2 = 0.999
ADAM_EPS = 1e-08
ADAM_WD = 0.01
ADAM_STEP = 10

MESH = pl.DeviceIdType.MESH


def _cp(*sem):
    return pltpu.CompilerParams(dimension_semantics=sem or None, vmem_limit_bytes=VMEM_LIMIT)


def _dot(a, b):
    return jnp.dot(a, b, preferred_element_type=F32)


def _dot_nt(a, b):
    return lax.dot_general(a, b, (((1,), (1,)), ((), ())), preferred_element_type=F32)


def _dot_tn(a, b):
    return lax.dot_general(a, b, (((0,), (0,)), ((), ())), preferred_element_type=F32)


def _sigmoid(x):
    return 1.0 / (1.0 + jnp.exp(-x))


def _rms(x, width):
    return lax.rsqrt(jnp.sum(x * x, axis=-1, keepdims=True) * (1.0 / width) + EPS)


def _rms_bwd(dn, n, r, width):
    return r * (dn - n * (jnp.sum(dn * n, axis=-1, keepdims=True) * (1.0 / width)))


def _colsum(x):
    return jnp.sum(x, axis=0, keepdims=True)


def _row(v, k):
    rid = lax.broadcasted_iota(jnp.int32, v.shape, 0)
    return jnp.sum(jnp.where(rid == k, v, 0.0), axis=0, keepdims=True)


def _full(shape):
    return pl.BlockSpec(shape, lambda *_: (0,) * len(shape))


def _once(shape):
    return pl.BlockSpec(shape, lambda *_: (0,) * len(shape), pipeline_mode=pl.Buffered(1))


def _in_proj(x, g1, w_in4, tm=512):
    T = x.shape[0]

    def body(x_ref, g_ref, w_ref, u_ref, qkv_ref, hg_ref):
        xv = x_ref[...]
        u = (xv * _rms(xv, D_MODEL) * g_ref[...]).astype(BF16)
        u_ref[...] = u
        p0 = _dot(u, w_ref[0])
        p1 = _dot(u, w_ref[1])
        qkv_ref[:, 0:IN_SHARD] = p0
        qkv_ref[:, IN_SHARD:QKV_W] = p1[:, :QKV_W - IN_SHARD]
        hg_ref[:, 0:2 * IN_SHARD - QKV_W] = p1[:, QKV_W - IN_SHARD:]
        hg_ref[:, 2 * IN_SHARD - QKV_W:3 * IN_SHARD - QKV_W] = _dot(u, w_ref[2])
        hg_ref[:, 3 * IN_SHARD - QKV_W:HG_W] = _dot(u, w_ref[3])

    return pl.pallas_call(
        body, name="in_proj", grid=(T // tm,),
        in_specs=[pl.BlockSpec((tm, D_MODEL), lambda i: (i, 0)), _full((1, D_MODEL)),
                  _once((N_CHIPS, D_MODEL, IN_SHARD))],
        out_specs=[pl.BlockSpec((tm, D_MODEL), lambda i: (i, 0)), pl.BlockSpec((tm, QKV_W), lambda i: (i, 0)),
                   pl.BlockSpec((tm, HG_W), lambda i: (i, 0))],
        out_shape=[jax.ShapeDtypeStruct((T, D_MODEL), BF16), jax.ShapeDtypeStruct((T, QKV_W), F32),
                   jax.ShapeDtypeStruct((T, HG_W), F32)],
        compiler_params=_cp("arbitrary"),
    )(x, g1, w_in4)


def _attn_masks(bias_ref):
    lane = lax.broadcasted_iota(jnp.int32, (ATTN_BLK, HEAD_PAIR), 1)
    row = lax.broadcasted_iota(jnp.int32, (2 * ATTN_BLK, 2 * ATTN_BLK), 0)
    col = lax.broadcasted_iota(jnp.int32, (2 * ATTN_BLK, 2 * ATTN_BLK), 1)
    base = jnp.where(row >= ATTN_BLK, row - ATTN_BLK, row) - col
    for k in range(2):
        dist = base + k * ATTN_BLK
        bias_ref[k] = jnp.where((dist >= 0) & (dist <= ATTN_BLK), 0.0, NEG)
    return lane < 64


def _two_heads(blk, first):
    zero = jnp.zeros_like(blk)
    return jnp.concatenate([jnp.where(first, blk, zero), jnp.where(first, zero, blk)], axis=0)


def _attn_rows(idx, nb, d):
    r, n = idx // nb, idx % nb
    kb = jnp.maximum(n - 1, 0)
    if d == 1:
        q0 = pl.multiple_of(n * ATTN_BLK, ATTN_BLK)
        k0 = pl.multiple_of(kb * ATTN_BLK, ATTN_BLK)
        return pl.ds(q0, ATTN_BLK), pl.ds(k0, 2 * ATTN_BLK), n - kb
    return (pl.ds(r + d * ATTN_BLK * n, ATTN_BLK, stride=d), pl.ds(r + d * ATTN_BLK * kb, 2 * ATTN_BLK, stride=d),
            n - kb)


def _attn_fwd(qkv):
    T = qkv.shape[0]

    per_chain = T // ATTN_BLK // ATTN_CHAINS_FWD

    def body(q_ref, k_ref, v_ref, o_ref, m_ref, l_ref, bias_ref):
        first = _attn_masks(bias_ref)
        for bi, d in enumerate(DILATIONS):
            nb = T // d // ATTN_BLK

            def block(idx, d=d, nb=nb, bi=bi):
                rows, keys, which = _attn_rows(idx, nb, d)
                q2 = _two_heads(q_ref[rows, :] * 0.125, first).astype(BF16)
                kw = k_ref[keys, :].astype(BF16)
                vw = v_ref[keys, :].astype(BF16)
                old = (o_ref[rows, :], m_ref[rows, :], l_ref[rows, :]) if bi else None
                s = _dot_nt(q2, kw) + bias_ref[which]
                mb = jnp.max(s, axis=-1, keepdims=True)
                p = jnp.exp(s - mb)
                lb = jnp.sum(p, axis=-1, keepdims=True)
                o2 = _dot(p.astype(BF16), vw)
                o = jnp.where(first, o2[:ATTN_BLK], o2[ATTN_BLK:])
                m = jnp.where(first, mb[:ATTN_BLK], mb[ATTN_BLK:])
                l = jnp.where(first, lb[:ATTN_BLK], lb[ATTN_BLK:])
                if bi:
                    po, pm, pl_ = old
                    mn = jnp.maximum(pm, m)
                    wa = jnp.exp(pm - mn)
                    wb = jnp.exp(m - mn)
                    o, l, m = po * wa + o * wb, pl_ * wa + l * wb, mn
                return rows, o, m, l

            def step(i, carry, block=block):
                done = [block(i + ch * per_chain) for ch in range(ATTN_CHAINS_FWD)]
                for rows, o, m, l in done:
                    o_ref[rows, :] = o
                    m_ref[rows, :] = m
                    l_ref[rows, :] = l
                return carry

            lax.fori_loop(0, per_chain, step, 0)

        def finish(i, carry):
            rows = pl.ds(pl.multiple_of(i * SUPER, SUPER), SUPER)
            l = l_ref[rows, :]
            o_ref[rows, :] = o_ref[rows, :] / l
            m_ref[rows, :] = m_ref[rows, :] + jnp.log(l)
            return carry

        lax.fori_loop(0, T // SUPER, finish, 0)

    col = lambda off: pl.BlockSpec((T, HEAD_PAIR), lambda j: (0, off + j))
    return pl.pallas_call(
        body, name="attn_fwd", grid=(4,),
        in_specs=[col(0), col(4), col(8)], out_specs=[col(0), col(0)],
        out_shape=[jax.ShapeDtypeStruct((T, ATTN_W), F32)] * 2,
        scratch_shapes=[pltpu.VMEM((T, HEAD_PAIR), F32), pltpu.VMEM((2, 2 * ATTN_BLK, 2 * ATTN_BLK), F32)],
        compiler_params=_cp("arbitrary"),
    )(qkv, qkv, qkv)


def _attn_bwd(qkv, o, lse, do, token=None):
    T = qkv.shape[0]
    per_chain = T // ATTN_BLK // ATTN_CHAINS
    extra = [] if token is None else [token]

    def body(q_ref, k_ref, v_ref, o_ref, lse_ref, do_ref, *rest):
        outs = rest[len(extra):len(extra) + 3]
        dq_ref, dk_ref, dv_ref, bias_ref = rest[len(extra) + 3:]
        first = _attn_masks(bias_ref)
        dq_ref[...] = jnp.zeros_like(dq_ref)
        dk_ref[...] = jnp.zeros_like(dk_ref)
        dv_ref[...] = jnp.zeros_like(dv_ref)
        for d in DILATIONS:
            nb = T // d // ATTN_BLK

            def block(idx, d=d, nb=nb):
                rows, keys, which = _attn_rows(idx, nb, d)
                q2 = _two_heads(q_ref[rows, :] * 0.125, first).astype(BF16)
                kw = k_ref[keys, :].astype(BF16)
                vw = v_ref[keys, :].astype(BF16)
                lse_b = lse_ref[rows, :]
                dob = do_ref[rows, :]
                prod = dob * o_ref[rows, :]
                old = dq_ref[rows, :], dk_ref[keys, :], dv_ref[keys, :]
                lse2 = jnp.concatenate(
                    [jnp.max(jnp.where(first, lse_b, NEG), axis=-1, keepdims=True),
                     jnp.max(jnp.where(first, NEG, lse_b), axis=-1, keepdims=True)], axis=0)
                p = jnp.exp(_dot_nt(q2, kw) + (bias_ref[which] - lse2))
                delta = jnp.concatenate(
                    [jnp.sum(jnp.where(first, prod, 0.0), axis=-1, keepdims=True),
                     jnp.sum(jnp.where(first, 0.0, prod), axis=-1, keepdims=True)], axis=0)
                do2 = _two_heads(dob, first).astype(BF16)
                ds = (p * (_dot_nt(do2, vw) - delta)).astype(BF16)
                dq2 = _dot(ds, kw) * 0.125
                return (rows, keys, old[0] + jnp.where(first, dq2[:ATTN_BLK], dq2[ATTN_BLK:]),
                        old[1] + _dot_tn(ds, q2), old[2] + _dot_tn(p.astype(BF16), do2))

            def step(i, carry, block=block):
                done = [block(i + ch * per_chain) for ch in range(ATTN_CHAINS)]
                for rows, keys, dq, dk, dv in done:
                    dq_ref[rows, :] = dq
                    dk_ref[keys, :] = dk
                    dv_ref[keys, :] = dv
                return carry

            lax.fori_loop(0, per_chain, step, 0)

        def emit(i, carry):
            rows = pl.ds(pl.multiple_of(i * SUPER, SUPER), SUPER)
            for out, acc in zip(outs, (dq_ref, dk_ref, dv_ref)):
                out[rows, :] = acc[rows, :].astype(BF16)
            return carry

        lax.fori_loop(0, T // SUPER, emit, 0)

    col = lambda off: pl.BlockSpec((T, HEAD_PAIR), lambda j: (0, off + j))
    return pl.pallas_call(
        body, name="attn_bwd", grid=(4,),
        in_specs=[col(0), col(4), col(8), col(0), col(0), col(0)] + [_full(t.shape) for t in extra],
        out_specs=[col(0)] * 3,
        out_shape=[jax.ShapeDtypeStruct((T, ATTN_W), BF16)] * 3,
        scratch_shapes=[pltpu.VMEM((T, HEAD_PAIR), F32)] * 3 + [pltpu.VMEM((2, 2 * ATTN_BLK, 2 * ATTN_BLK), F32)],
        compiler_params=_cp("arbitrary"),
    )(qkv, qkv, qkv, o, lse, do, *extra)


def _chunk_ids():
    row = lax.broadcasted_iota(jnp.int32, (SUPER, HGRN_DIM), 0)
    r2 = lax.broadcasted_iota(jnp.int32, (SUPER, SUPER), 0)
    c2 = lax.broadcasted_iota(jnp.int32, (SUPER, SUPER), 1)
    amask = ((r2 // HGRN_CHUNK) == (c2 // HGRN_CHUNK)) & (c2 <= r2)
    return row % HGRN_CHUNK, row // HGRN_CHUNK, amask


def _cumsum_chunk(x, rmod):
    s = 1
    while s < HGRN_CHUNK:
        x = x + jnp.where(rmod >= s, pltpu.roll(x, s, 0), 0.0)
        s *= 2
    return x


def _suffix_sum_chunk(x, rmod):
    s = 1
    while s < HGRN_CHUNK:
        x = x + jnp.where(rmod < HGRN_CHUNK - s, pltpu.roll(x, SUPER - s, 0), 0.0)
        s *= 2
    return x


def _chunk_rows(vs, cid):
    out = vs[-1]
    for c in reversed(range(len(vs) - 1)):
        out = jnp.where(cid == c, vs[c], out)
    return out


def _expand(x, cid):
    return jnp.concatenate([jnp.where(cid == c, x, 0.0) for c in range(SUPER // HGRN_CHUNK)], axis=1)


def _hgrn_gates(q, f, lbv, rmod, cid, tmp):
    sq = _sigmoid(q)
    sg = _sigmoid(f)
    forget = lbv + (1.0 - lbv) * sg
    key = 1.0 - forget
    b = _cumsum_chunk(jnp.log(forget), rmod)
    tmp[...] = b
    bends = [tmp[c * HGRN_CHUNK + HGRN_CHUNK - 1:(c + 1) * HGRN_CHUNK, :] for c in range(SUPER // HGRN_CHUNK)]
    eb = jnp.exp(b)
    enb = jnp.exp(-b)
    ebe = jnp.exp(_chunk_rows(bends, cid) - b)
    return sq, sg, forget, key, eb, enb, ebe, q * sq * eb, key * enb, key * ebe, [jnp.exp(v) for v in bends]


def _hgrn_fwd(hg, lb):
    T = hg.shape[0]
    nsc = T // SUPER
    NC = SUPER // HGRN_CHUNK

    def body(q_ref, f_ref, i_ref, lb_ref, o_ref, st_ref, state, tmp):
        rmod, cid, amask = _chunk_ids()
        state[...] = jnp.zeros_like(state)
        lbv = lb_ref[...]

        def local(sc, u):
            rows = pl.ds(pl.multiple_of(sc * SUPER, SUPER), SUPER)
            iv = i_ref[rows, :].astype(BF16)
            qd, ki, ke, dec = _hgrn_gates(q_ref[rows, :], f_ref[rows, :], lbv, rmod, cid, tmp.at[u])[-4:]
            a = jnp.where(amask, _dot_nt(qd.astype(BF16), ki.astype(BF16)), 0.0)
            return rows, qd, dec, _dot(a.astype(BF16), iv), _dot_tn(iv, _expand(ke, cid).astype(BF16))

        def step(i, carry):
            parts = [local(i * HGRN_SIDE + u, u) for u in range(HGRN_SIDE)]
            st = state[...]
            entering = []
            for u, (_, _, dec, _, ut) in enumerate(parts):
                st_ref[0, i * HGRN_SIDE + u] = st
                sts = []
                for c in range(NC):
                    sts.append(st)
                    st = st * dec[c] + ut[:, c * HGRN_DIM:(c + 1) * HGRN_DIM]
                entering.append(jnp.concatenate(sts, axis=1).astype(BF16))
            state[...] = st
            for (rows, qd, _, o, _), sts in zip(parts, entering):
                o_ref[rows, :] = o + _dot_nt(_expand(qd, cid).astype(BF16), sts)
            return carry

        lax.fori_loop(0, nsc // HGRN_SIDE, step, 0)

    col = lambda off: pl.BlockSpec((T, HGRN_DIM), lambda h: (0, off + h))
    return pl.pallas_call(
        body, name="hgrn_fwd", grid=(HGRN_HEADS,),
        in_specs=[col(0), col(4), col(8), pl.BlockSpec((1, HGRN_DIM), lambda h: (0, h))],
        out_specs=[pl.BlockSpec((T, HGRN_DIM), lambda h: (0, h)),
                   pl.BlockSpec((1, nsc, HGRN_DIM, HGRN_DIM), lambda h: (h, 0, 0, 0))],
        out_shape=[jax.ShapeDtypeStruct((T, HGRN_W), F32),
                   jax.ShapeDtypeStruct((HGRN_HEADS, nsc, HGRN_DIM, HGRN_DIM), F32)],
        scratch_shapes=[pltpu.VMEM((HGRN_DIM, HGRN_DIM), F32), pltpu.VMEM((HGRN_SIDE, SUPER, HGRN_DIM), F32)],
        compiler_params=_cp("arbitrary"),
    )(hg, hg, hg, lb)


def _hgrn_bwd(hg, lb, states, do):
    T = hg.shape[0]
    nsc = T // SUPER
    NC = SUPER // HGRN_CHUNK

    def body(q_ref, f_ref, i_ref, lb_ref, st_ref, do_ref, dq_ref, df_ref, di_ref, dlb_ref, dstate, tmp):
        rmod, cid, amask = _chunk_ids()
        dstate[...] = jnp.zeros_like(dstate)
        dlb_ref[...] = jnp.zeros_like(dlb_ref)
        lbv = lb_ref[...]

        def local(sc, u):
            rows = pl.ds(pl.multiple_of(sc * SUPER, SUPER), SUPER)
            q = q_ref[rows, :]
            ivf = i_ref[rows, :]
            iv = ivf.astype(BF16)
            dof = do_ref[rows, :]
            dob = dof.astype(BF16)
            sq, sg, forget, key, eb, enb, ebe, qd, ki, ke, dec = _hgrn_gates(q, f_ref[rows, :], lbv, rmod, cid,
                                                                            tmp.at[u])
            qdb, kib = qd.astype(BF16), ki.astype(BF16)
            keexp = _expand(ke, cid).astype(BF16)
            a = jnp.where(amask, _dot_nt(qdb, kib), 0.0).astype(BF16)
            ut = _dot_tn(iv, keexp)
            st = st_ref[0, sc]
            sts = []
            for c in range(NC):
                sts.append(st)
                st = st * dec[c] + ut[:, c * HGRN_DIM:(c + 1) * HGRN_DIM]
            gt = _dot_tn(dob, _expand(qd, cid).astype(BF16))
            da = jnp.where(amask, _dot_nt(dob, iv), 0.0).astype(BF16)
            ststack = jnp.concatenate(sts, axis=0).astype(BF16)
            return dict(rows=rows, q=q, sq=sq, sg=sg, forget=forget, eb=eb, enb=enb, ebe=ebe, qd=qd, ki=ki, ke=ke,
                        dec=dec, sts=sts, gt=gt, keexp=keexp, ivexp=_expand(ivf, cid).astype(BF16),
                        div=_dot_tn(a, dob), dki=_dot_tn(da, qdb),
                        dqd=_dot(da, kib) + _dot(_expand(dof, cid).astype(BF16), ststack))

        def finish(p, nxt, ddec):
            ncat = jnp.concatenate(nxt, axis=1).astype(BF16)
            nstack = jnp.concatenate(nxt, axis=0).astype(BF16)
            dke = _dot(p["ivexp"], nstack)
            dkk = dke * p["ke"]
            dkey = p["dki"] * p["enb"] + dke * p["ebe"]
            db = p["dqd"] * p["qd"] - p["dki"] * p["ki"] - dkk
            dbends = [_colsum(jnp.where(cid == c, dkk, 0.0)) + ddec[c] * p["dec"][c] for c in range(NC)]
            dforget = (_suffix_sum_chunk(db, rmod) + _chunk_rows(dbends, cid)) / p["forget"] - dkey
            sg, sq, q = p["sg"], p["sq"], p["q"]
            df_ref[p["rows"], :] = (dforget * (1.0 - lbv) * sg * (1.0 - sg)).astype(BF16)
            dq_ref[p["rows"], :] = (p["dqd"] * p["eb"] * (sq * (1.0 + q * (1.0 - sq)))).astype(BF16)
            di_ref[p["rows"], :] = (p["div"] + _dot_nt(p["keexp"], ncat)).astype(BF16)
            return _colsum(dforget * (1.0 - sg))

        def step(i, carry):
            parts = [local(nsc - 1 - (i * HGRN_SIDE + u), u) for u in range(HGRN_SIDE)]
            dst = dstate[...]
            chained = []
            for p in parts:
                nxt = [None] * NC
                ddec = [None] * NC
                for c in reversed(range(NC)):
                    nxt[c] = dst
                    ddec[c] = _colsum(dst * p["sts"][c])
                    dst = dst * p["dec"][c] + p["gt"][:, c * HGRN_DIM:(c + 1) * HGRN_DIM]
                chained.append((nxt, ddec))
            dstate[...] = dst
            dlb = dlb_ref[...]
            for p, (nxt, ddec) in zip(parts, chained):
                dlb = dlb + finish(p, nxt, ddec)
            dlb_ref[...] = dlb
            return carry

        lax.fori_loop(0, nsc // HGRN_SIDE, step, 0)

    col = lambda off: pl.BlockSpec((T, HGRN_DIM), lambda h: (0, off + h))
    own = pl.BlockSpec((T, HGRN_DIM), lambda h: (0, h))
    vec = pl.BlockSpec((1, HGRN_DIM), lambda h: (0, h))
    return pl.pallas_call(
        body, name="hgrn_bwd", grid=(HGRN_HEADS,),
        in_specs=[col(0), col(4), col(8), vec,
                  pl.BlockSpec((1, nsc, HGRN_DIM, HGRN_DIM), lambda h: (h, 0, 0, 0)), own],
        out_specs=[own, own, own, vec],
        out_shape=[jax.ShapeDtypeStruct((T, HGRN_W), BF16)] * 3 + [jax.ShapeDtypeStruct((1, HGRN_W), F32)],
        scratch_shapes=[pltpu.VMEM((HGRN_DIM, HGRN_DIM), F32), pltpu.VMEM((HGRN_SIDE, SUPER, HGRN_DIM), F32)],
        compiler_params=_cp("arbitrary"),
    )(hg, hg, hg, lb, states, do)


def _rec_heads(rec, gate, g_h):
    rr = jnp.concatenate(
        [jnp.broadcast_to(_rms(rec[:, h * HGRN_DIM:(h + 1) * HGRN_DIM], HGRN_DIM), (rec.shape[0], HGRN_DIM))
         for h in range(HGRN_HEADS)], axis=1)
    rn = rec * rr
    sg = _sigmoid(gate)
    return rr, rn, sg


def _mix_out(attn_o, rec_o, hg, x, g_a, g_h, w_out, tm=512):
    T = x.shape[0]

    def body(a_ref, r_ref, gt_ref, x_ref, ga_ref, gh_ref, w_ref, h1_ref, mixed_ref):
        a = a_ref[...]
        an = a * _rms(a, ATTN_W) * ga_ref[...]
        gate = gt_ref[...]
        _, rn, sg = _rec_heads(r_ref[...], gate, gh_ref[...])
        mixed = jnp.concatenate([an, rn * gh_ref[...] * (gate * sg)], axis=1).astype(BF16)
        mixed_ref[...] = mixed
        h1_ref[...] = x_ref[...] + _dot(mixed, w_ref[...])

    row = lambda w: pl.BlockSpec((tm, w), lambda i: (i, 0))
    return pl.pallas_call(
        body, name="mix_out", grid=(T // tm,),
        in_specs=[row(ATTN_W), row(HGRN_W), pl.BlockSpec((tm, HGRN_W), lambda i: (i, 3)), row(D_MODEL),
                  _full((1, ATTN_W)), _full((1, HGRN_W)), _once((D_MODEL, D_MODEL))],
        out_specs=[row(D_MODEL), row(D_MODEL)],
        out_shape=[jax.ShapeDtypeStruct((T, D_MODEL), F32), jax.ShapeDtypeStruct((T, D_MODEL), BF16)],
        compiler_params=_cp("arbitrary"),
    )(attn_o, rec_o, hg, x, g_a, g_h, w_out)


_INV_SQRT2 = 1.0 / math.sqrt(2.0)
_INV_SQRT2PI = 1.0 / math.sqrt(2.0 * math.pi)


def _gelu(x):
    return 0.5 * x * (1.0 + lax.erf(x * _INV_SQRT2))


def _gelu_grad(x):
    return 0.5 * (1.0 + lax.erf(x * _INV_SQRT2)) + x * jnp.exp(-0.5 * x * x) * _INV_SQRT2PI


def _shift_down(g, prev, rowid):
    p1 = _row(prev, prev.shape[0] - 1)
    p2 = _row(prev, prev.shape[0] - 2)
    s1 = jnp.where(rowid == 0, p1, pltpu.roll(g, 1, 0))
    s2 = jnp.where(rowid == 0, p2, jnp.where(rowid == 1, p1, pltpu.roll(g, 2, 0)))
    return s1, s2


def _mlp_fwd(h1, g2, w_up4, conv_w, conv_b, w_down, gf, tgt, tm=256):
    T = h1.shape[0]
    half = D_FF // 2

    def body(h_ref, g2_ref, wu_ref, cw_ref, cb_ref, wd_ref, gf_ref, t_ref,
             u_ref, gate_ref, val_ref, conv_ref, act_ref, dh_ref, loss_ref, dgf_ref, carry):
        i = pl.program_id(0)

        @pl.when(i == 0)
        def _():
            carry[...] = jnp.zeros_like(carry)
            loss_ref[...] = jnp.zeros_like(loss_ref)
            dgf_ref[...] = jnp.zeros_like(dgf_ref)

        h = h_ref[...]
        u = (h * _rms(h, D_MODEL) * g2_ref[...]).astype(BF16)
        u_ref[...] = u
        rowid = lax.broadcasted_iota(jnp.int32, (tm, half), 0)
        y2 = jnp.zeros((tm, D_MODEL), F32)
        for c in range(2):
            cols = slice(c * half, (c + 1) * half)
            gb = _dot(u, wu_ref[c]).astype(BF16)
            vb = _dot(u, wu_ref[2 + c]).astype(BF16)
            gate_ref[:, cols] = gb
            val_ref[:, cols] = vb
            g = gb.astype(F32)
            s1, s2 = _shift_down(g, carry[:, cols], rowid)
            carry[:, cols] = g[tm - 8:, :]
            conv = cb_ref[:, cols] + cw_ref[0:1, cols] * s2 + cw_ref[1:2, cols] * s1 + cw_ref[2:3, cols] * g
            act = (_gelu(conv) * vb.astype(F32)).astype(BF16)
            conv_ref[:, cols] = conv.astype(BF16)
            act_ref[:, cols] = act
            y2 = y2 + _dot(act, wd_ref[cols, :])
        h2 = h + y2
        rf = _rms(h2, D_MODEL)
        n = h2 * rf
        gfv = gf_ref[...]
        e = n * gfv - t_ref[...]
        loss_ref[...] += jnp.sum(e * e) * (0.5 / D_MODEL)
        dy = e * (1.0 / D_MODEL)
        dgf_ref[...] += _colsum(dy * n)
        dh_ref[...] = _rms_bwd(dy * gfv, n, rf, D_MODEL)

    row = lambda w: pl.BlockSpec((tm, w), lambda i: (i, 0))
    return pl.pallas_call(
        body, name="mlp_fwd", grid=(T // tm,),
        in_specs=[row(D_MODEL), _full((1, D_MODEL)), _once((N_CHIPS, D_MODEL, UP_SHARD)), _full((3, D_FF)),
                  _full((1, D_FF)), _once((D_FF, D_MODEL)), _full((1, D_MODEL)), row(D_MODEL)],
        out_specs=[row(D_MODEL), row(D_FF), row(D_FF), row(D_FF), row(D_FF), row(D_MODEL), _full((1, 128)),
                   _full((1, D_MODEL))],
        out_shape=[jax.ShapeDtypeStruct((T, D_MODEL), BF16)] + [jax.ShapeDtypeStruct((T, D_FF), BF16)] * 4
        + [jax.ShapeDtypeStruct((T, D_MODEL), F32),
                   jax.ShapeDtypeStruct((1, 128), F32), jax.ShapeDtypeStruct((1, D_MODEL), F32)],
        scratch_shapes=[pltpu.VMEM((8, D_FF), F32)],
        compiler_params=_cp("arbitrary"),
    )(h1, g2, w_up4, conv_w, conv_b, w_down, gf, tgt)


def _mlp_bwd(dh2, gate, val, conv, conv_w, w_down, tm=256):
    T = dh2.shape[0]
    nb = T // tm
    half = D_FF // 2

    def body(dh_ref, gate_ref, val_ref, conv_ref, cw_ref, wd_ref, dgv_ref, dcw_ref, dcb_ref, carry):
        @pl.when(pl.program_id(0) == 0)
        def _():
            carry[...] = jnp.zeros_like(carry)
            dcw_ref[...] = jnp.zeros_like(dcw_ref)
            dcb_ref[...] = jnp.zeros_like(dcb_ref)

        dhb = dh_ref[...].astype(BF16)
        rowid = lax.broadcasted_iota(jnp.int32, (tm, half), 0)
        for c in range(2):
            cols = slice(c * half, (c + 1) * half)
            g = gate_ref[:, cols].astype(F32)
            v = val_ref[:, cols].astype(F32)
            cv = conv_ref[:, cols].astype(F32)
            dact = _dot_nt(dhb, wd_ref[cols, :])
            dconv = dact * v * _gelu_grad(cv)
            nxt = carry[:, cols]
            n0, n1 = _row(nxt, 0), _row(nxt, 1)
            u1 = jnp.where(rowid == tm - 1, n0, pltpu.roll(dconv, tm - 1, 0))
            u2 = jnp.where(rowid == tm - 1, n1, jnp.where(rowid == tm - 2, n0, pltpu.roll(dconv, tm - 2, 0)))
            carry[:, cols] = dconv[0:8, :]
            dcb_ref[:, cols] += _colsum(dconv)
            dcw_ref[0:1, cols] += _colsum(u2 * g)
            dcw_ref[1:2, cols] += _colsum(u1 * g)
            dcw_ref[2:3, cols] += _colsum(dconv * g)
            dgate = cw_ref[2:3, cols] * dconv + cw_ref[1:2, cols] * u1 + cw_ref[0:1, cols] * u2
            dgv_ref[:, cols] = dgate.astype(BF16)
            dgv_ref[:, D_FF + c * half:D_FF + (c + 1) * half] = (dact * _gelu(cv)).astype(BF16)

    rev = lambda w: pl.BlockSpec((tm, w), lambda i: (nb - 1 - i, 0))
    return pl.pallas_call(
        body, name="mlp_bwd", grid=(nb,),
        in_specs=[rev(D_MODEL), rev(D_FF), rev(D_FF), rev(D_FF), _full((3, D_FF)), _once((D_FF, D_MODEL))],
        out_specs=[rev(2 * D_FF), _full((3, D_FF)), _full((1, D_FF))],
        out_shape=[jax.ShapeDtypeStruct((T, 2 * D_FF), BF16), jax.ShapeDtypeStruct((3, D_FF), F32),
                   jax.ShapeDtypeStruct((1, D_FF), F32)],
        scratch_shapes=[pltpu.VMEM((8, D_FF), F32)],
        compiler_params=_cp("arbitrary"),
    )(dh2, gate, val, conv, conv_w, w_down)


def _up_out_bwd(dgv, w_up4, h1, g2, dh2, w_out, attn_o, rec_o, hg, g_a, g_h, tm=256):
    T = h1.shape[0]

    def body(dgv_ref, wu_ref, h_ref, g2_ref, dh2_ref, wo_ref, a_ref, r_ref, gt_ref, ga_ref, gh_ref,
             dh1_ref, dg2_ref, da_ref, dr_ref, dgt_ref, dga_ref, dgh_ref):
        @pl.when(pl.program_id(0) == 0)
        def _():
            dg2_ref[...] = jnp.zeros_like(dg2_ref)
            dga_ref[...] = jnp.zeros_like(dga_ref)
            dgh_ref[...] = jnp.zeros_like(dgh_ref)

        du = jnp.zeros((tm, D_MODEL), F32)
        for k in range(N_CHIPS):
            du = du + _dot_nt(dgv_ref[:, k * UP_SHARD:(k + 1) * UP_SHARD], wu_ref[k])
        h = h_ref[...]
        r = _rms(h, D_MODEL)
        n = h * r
        dg2_ref[...] += _colsum(du * n)
        dh1 = dh2_ref[...] + _rms_bwd(du * g2_ref[...], n, r, D_MODEL)
        dh1_ref[...] = dh1
        dmix = _dot_nt(dh1.astype(BF16), wo_ref[...])
        dan = dmix[:, :ATTN_W]
        a = a_ref[...]
        ra = _rms(a, ATTN_W)
        na = a * ra
        dga_ref[...] += _colsum(dan * na)
        da_ref[...] = _rms_bwd(dan * ga_ref[...], na, ra, ATTN_W)
        dmr = dmix[:, ATTN_W:]
        gate = gt_ref[...]
        ghv = gh_ref[...]
        rr, rn, sg = _rec_heads(r_ref[...], gate, ghv)
        dgt_ref[...] = (dmr * rn * ghv * (sg * (1.0 + gate * (1.0 - sg)))).astype(BF16)
        drecn = dmr * (gate * sg)
        dgh_ref[...] += _colsum(drecn * rn)
        drn = drecn * ghv
        prod = drn * rn
        mean = jnp.concatenate(
            [jnp.broadcast_to(jnp.sum(prod[:, h_ * HGRN_DIM:(h_ + 1) * HGRN_DIM], axis=-1, keepdims=True),
                              (tm, HGRN_DIM)) for h_ in range(HGRN_HEADS)], axis=1) * (1.0 / HGRN_DIM)
        dr_ref[...] = rr * (drn - rn * mean)

    row = lambda w: pl.BlockSpec((tm, w), lambda i: (i, 0))
    return pl.pallas_call(
        body, name="up_out_bwd", grid=(T // tm,),
        in_specs=[row(2 * D_FF), _once((N_CHIPS, D_MODEL, UP_SHARD)), row(D_MODEL), _full((1, D_MODEL)),
                  row(D_MODEL), _once((D_MODEL, D_MODEL)), row(ATTN_W), row(HGRN_W),
                  pl.BlockSpec((tm, HGRN_W), lambda i: (i, 3)), _full((1, ATTN_W)), _full((1, HGRN_W))],
        out_specs=[row(D_MODEL), _full((1, D_MODEL)), row(ATTN_W), row(HGRN_W), row(HGRN_W),
                   _full((1, ATTN_W)), _full((1, HGRN_W))],
        out_shape=[jax.ShapeDtypeStruct((T, D_MODEL), F32), jax.ShapeDtypeStruct((1, D_MODEL), F32),
                   jax.ShapeDtypeStruct((T, ATTN_W), F32), jax.ShapeDtypeStruct((T, HGRN_W), F32),
                   jax.ShapeDtypeStruct((T, HGRN_W), BF16), jax.ShapeDtypeStruct((1, ATTN_W), F32),
                   jax.ShapeDtypeStruct((1, HGRN_W), F32)],
        compiler_params=_cp("arbitrary"),
    )(dgv, w_up4, h1, g2, dh2, w_out, attn_o, rec_o, hg, g_a, g_h)


def _in_bwd(dqkv, dhg, w_in4, x, g1, dh1, tm=512):
    T = x.shape[0]

    def body(*refs):
        parts = refs[:7]
        w_ref, x_ref, g_ref, dh1_ref, dp_ref, dx_ref, dg_ref = refs[7:]

        @pl.when(pl.program_id(0) == 0)
        def _():
            dg_ref[...] = jnp.zeros_like(dg_ref)

        dp = jnp.concatenate([p[...] for p in parts], axis=1)
        dp_ref[...] = dp
        du = jnp.zeros((tm, D_MODEL), F32)
        for k in range(N_CHIPS):
            du = du + _dot_nt(dp[:, k * IN_SHARD:(k + 1) * IN_SHARD], w_ref[k])
        xv = x_ref[...]
        r = _rms(xv, D_MODEL)
        n = xv * r
        dg_ref[...] += _colsum(du * n)
        dx_ref[...] = dh1_ref[...] + _rms_bwd(du * g_ref[...], n, r, D_MODEL)

    row = lambda w: pl.BlockSpec((tm, w), lambda i: (i, 0))
    return pl.pallas_call(
        body, name="in_bwd", grid=(T // tm,),
        in_specs=[row(ATTN_W)] * 7 + [_once((N_CHIPS, D_MODEL, IN_SHARD)), row(D_MODEL), _full((1, D_MODEL)),
                                       row(D_MODEL)],
        out_specs=[row(IN_TOTAL), row(D_MODEL), _full((1, D_MODEL))],
        out_shape=[jax.ShapeDtypeStruct((T, IN_TOTAL), BF16), jax.ShapeDtypeStruct((T, D_MODEL), F32),
                   jax.ShapeDtypeStruct((1, D_MODEL), F32)],
        compiler_params=_cp("arbitrary"),
    )(*dqkv, *dhg, w_in4, x, g1, dh1)


def _dw(a, b, kb, nb_, name, tk=1024):
    T, K = a.shape
    N = b.shape[1]
    nk, nn, nt = K // kb, N // nb_, T // tk

    def body(a_ref, b_ref, o_ref, acc):
        t = pl.program_id(2)

        @pl.when(t == 0)
        def _():
            acc[...] = jnp.zeros_like(acc)

        acc[...] += _dot_tn(a_ref[...], b_ref[...].astype(BF16))

        @pl.when(t == nt - 1)
        def _():
            o_ref[0] = acc[...].astype(BF16)

    return pl.pallas_call(
        body, name=name, grid=(nk, nn, nt),
        in_specs=[pl.BlockSpec((tk, kb), lambda i, j, t: (t, i)), pl.BlockSpec((tk, nb_), lambda i, j, t: (t, j))],
        out_specs=pl.BlockSpec((1, kb, nb_), lambda i, j, t: (i * nn + j, 0, 0)),
        out_shape=jax.ShapeDtypeStruct((nk * nn, kb, nb_), BF16),
        scratch_shapes=[pltpu.VMEM((kb, nb_), F32)],
        compiler_params=_cp("arbitrary", "arbitrary", "arbitrary"),
    )(a, b)


def _local_step(x, tgt, g1, w_in4, g_a, g_h, lb, w_out, g2, w_up4, conv_w, conv_b, w_down, gf):
    a = _step_mixers(x, g1, w_in4, lb)
    b = _step_channel(a, x, tgt, g_a, g_h, w_out, g2, w_up4, conv_w, conv_b, w_down, gf)
    c = _step_mixers_bwd(a, b, x, g1, w_in4, lb)
    small = dict(g1=c["dg1"], g_a=b["dga"], g_h=b["dgh"], lb=c["dlb"], g2=b["dg2"], conv_w=b["dcw"], conv_b=b["dcb"],
                 gf=b["dgf"])
    return b["loss"], c["dx"], small, dict(w_in=c["dw_in"], w_out=b["dw_out"], w_up=b["dw_up"], w_down=b["dw_down"])


def _step_mixers(x, g1, w_in4, lb):
    u1, qkv, hg = _in_proj(x, g1, w_in4)
    attn_o, lse = _attn_fwd(qkv)
    rec_o, states = _hgrn_fwd(hg, lb)
    return dict(u1=u1, qkv=qkv, hg=hg, attn_o=attn_o, lse=lse, rec_o=rec_o, states=states)


def _step_channel(a, x, tgt, g_a, g_h, w_out, g2, w_up4, conv_w, conv_b, w_down, gf):
    h1, mixed = _mix_out(a["attn_o"], a["rec_o"], a["hg"], x, g_a, g_h, w_out)
    u2, gate, val, conv, act, dh2, loss, dgf = _mlp_fwd(h1, g2, w_up4, conv_w, conv_b, w_down, gf, tgt)
    dgv, dcw, dcb = _mlp_bwd(dh2, gate, val, conv, conv_w, w_down)
    dw_down = _dw(act, dh2, D_FF // 2, D_MODEL, "dw_down").reshape(N_CHIPS, D_FF // N_CHIPS, D_MODEL)
    dh1, dg2, da, dr, dgt, dga, dgh = _up_out_bwd(dgv, w_up4, h1, g2, dh2, w_out, a["attn_o"], a["rec_o"], a["hg"],
                                                  g_a, g_h)
    dw_up = _dw(u2, dgv, D_MODEL, UP_SHARD, "dw_up")
    dw_out = _dw(mixed, dh1, D_MODEL, D_MODEL, "dw_out").reshape(N_CHIPS, D_MODEL // N_CHIPS, D_MODEL)
    return dict(loss=loss, dgf=dgf, dcw=dcw, dcb=dcb, dg2=dg2, dga=dga, dgh=dgh, dh1=dh1, da=da, dr=dr, dgt=dgt,
                dw_down=dw_down, dw_up=dw_up, dw_out=dw_out)


def _step_mixers_bwd(a, b, x, g1, w_in4, lb, dqkv=None):
    if dqkv is None:
        dqkv = _attn_bwd(a["qkv"], a["attn_o"], a["lse"], b["da"])
    dhq, dhf, dhi, dlb = _hgrn_bwd(a["hg"], lb, a["states"], b["dr"])
    dproj, dx, dg1 = _in_bwd(dqkv, [dhq, dhf, dhi, b["dgt"]], w_in4, x, g1, b["dh1"])
    dw_in = _dw(a["u1"], dproj, D_MODEL, IN_SHARD, "dw_in")
    return dict(dx=dx, dg1=dg1, dlb=dlb, dw_in=dw_in)


BIG = ("w_in", "w_out", "w_up", "w_down")
ANY = pl.BlockSpec(memory_space=pl.ANY)


def _place():
    x, y, c = lax.axis_index("x"), lax.axis_index("y"), lax.axis_index("c")
    chips = [(1 - x, y), (x, 1 - y), (1 - x, 1 - y)]
    return x, y, c, chips


def _remote(src, dst, send_sems, recv_sems, k, to):
    return pltpu.make_async_remote_copy(src_ref=src, dst_ref=dst, send_sem=send_sems.at[k], recv_sem=recv_sems.at[k],
                                        device_id=to, device_id_type=MESH)


def _gather_weights(shards, conv_w):
    n = len(shards)
    halves = [s.shape[0] // 2 for s in shards]

    def body(*refs):
        ins, cw, outs, ocw = refs[:n], refs[n], refs[n + 1:2 * n + 1], refs[2 * n + 1]
        send_sems, recv_sems = refs[2 * n + 2:]
        x, y, c, chips = _place()
        me, sibling = 2 * x + y, (x, y, 1 - c)

        def part(w, chip, half):
            return outs[w].at[chip, pl.ds(half * halves[w], halves[w]), :]

        sent = []
        for j, chip in enumerate(chips):
            for w in range(n):
                sent.append(_remote(ins[w].at[pl.ds(c * halves[w], halves[w]), :], part(w, me, c),
                                    send_sems, recv_sems, w * 3 + j, (*chip, c)))
            sent.append(_remote(cw, ocw.at[me], send_sems, recv_sems, 6 * n + j, (*chip, c)))
        for cp in sent:
            cp.start()
        for j, chip in enumerate(chips):
            kj = 2 * chip[0] + chip[1]
            for w in range(n):
                _remote(part(w, kj, c), part(w, kj, c), send_sems, recv_sems, w * 3 + j, (*chip, c)).wait_recv()
                fwd = _remote(part(w, kj, c), part(w, kj, c), send_sems, recv_sems, 3 * n + w * 3 + j, sibling)
                fwd.start()
                sent.append(fwd)
        for j, chip in enumerate(chips):
            kj = 2 * chip[0] + chip[1]
            for w in range(n):
                _remote(part(w, kj, 1 - c), part(w, kj, 1 - c), send_sems, recv_sems, 3 * n + w * 3 + j,
                        sibling).wait_recv()
            _remote(cw, ocw.at[kj], send_sems, recv_sems, 6 * n + j, (*chip, c)).wait_recv()
        for cp in sent:
            cp.wait_send()

    n_sem = 6 * n + 3
    outs = pl.pallas_call(
        body, name="gather_weights",
        in_specs=[ANY] * (n + 1), out_specs=[ANY] * (n + 1),
        out_shape=[jax.ShapeDtypeStruct((N_CHIPS,) + s.shape, s.dtype) for s in shards]
        + [jax.ShapeDtypeStruct((N_CHIPS,) + conv_w.shape, conv_w.dtype)],
        scratch_shapes=[pltpu.SemaphoreType.DMA((n_sem,)), pltpu.SemaphoreType.DMA((n_sem,))],
    )(*shards, conv_w)
    chip = 2 * lax.axis_index("x") + lax.axis_index("y")
    return [lax.dynamic_update_slice(o, s[None], (chip,) + (0,) * s.ndim) for o, s in zip(outs, [*shards, conv_w])]


def _allreduce_small(buf):
    rows = buf.shape[0]

    def body(in_ref, out_ref, slots, send_sems, recv_sems):
        x, y, c, _ = _place()
        me = 4 * x + 2 * y + c
        slots[me] = in_ref[...]
        sent = []
        for p in range(1, 8):
            to = (x ^ (p >> 2), y ^ ((p >> 1) & 1), c ^ (p & 1))
            sent.append(_remote(in_ref, slots.at[me], send_sems, recv_sems, p, to))
        for cp in sent:
            cp.start()
        for p in range(1, 8):
            frm = 4 * (x ^ (p >> 2)) + 2 * (y ^ ((p >> 1) & 1)) + (c ^ (p & 1))
            _remote(in_ref, slots.at[frm], send_sems, recv_sems, p, (x, y, c)).wait_recv()
        for cp in sent:
            cp.wait_send()
        acc = slots[0]
        for d in range(1, 8):
            acc = acc + slots[d]
        out_ref[...] = acc

    vm = pl.BlockSpec(memory_space=pltpu.VMEM)
    return pl.pallas_call(
        body, name="allreduce_small", in_specs=[vm], out_specs=vm,
        out_shape=jax.ShapeDtypeStruct(buf.shape, F32),
        scratch_shapes=[pltpu.VMEM((8, rows, 128), F32), pltpu.SemaphoreType.DMA((8,)), pltpu.SemaphoreType.DMA((8,))],
    )(buf)


def _pair_exchange(gs, name):
    n = len(gs)
    halves = [g.shape[1] // 2 for g in gs]

    def body(*refs):
        g, got = refs[:n], refs[n:2 * n]
        send_sems, recv_sems = refs[2 * n:]
        x, y, c, _ = _place()
        cps = [_remote(g[w].at[:, pl.ds((1 - c) * halves[w], halves[w]), :], got[w], send_sems, recv_sems, w,
                       (x, y, 1 - c)) for w in range(n)]
        for cp in cps:
            cp.start()
        for cp in cps:
            cp.wait()

    return pl.pallas_call(
        body, name=name, in_specs=[ANY] * n, out_specs=[ANY] * n,
        out_shape=[jax.ShapeDtypeStruct((N_CHIPS, h, g.shape[2]), g.dtype) for g, h in zip(gs, halves)],
        scratch_shapes=[pltpu.SemaphoreType.DMA((n,)), pltpu.SemaphoreType.DMA((n,))],
    )(*gs)


def _core_id():
    return lax.axis_index("c").reshape(1).astype(jnp.int32)


def _pair_sum(g, got, name):
    h, C = got.shape[1:]

    def body(c_ref, g_ref, b_ref, o_ref):
        o_ref[...] = (g_ref[...].astype(F32) + b_ref[...].astype(F32)).astype(BF16)

    blk = pl.BlockSpec((1, h, C), lambda k, c_ref: (k, 0, 0))
    return pl.pallas_call(
        body, name=name,
        grid_spec=pltpu.PrefetchScalarGridSpec(
            num_scalar_prefetch=1, grid=(N_CHIPS,),
            in_specs=[pl.BlockSpec((1, h, C), lambda k, c_ref: (k, c_ref[0], 0)), blk], out_specs=blk),
        out_shape=jax.ShapeDtypeStruct(got.shape, BF16), compiler_params=_cp("arbitrary"))(_core_id(), g, got)


def _sum_partials(g, got, landed, name):
    h, C = got.shape[1:]

    def body(ids, g_ref, b_ref, l_ref, o_ref):
        acc = g_ref[0].astype(F32) + b_ref[0].astype(F32)
        for j in range(3):
            acc = acc + l_ref[j].astype(F32)
        o_ref[...] = acc

    ids = jnp.stack([2 * lax.axis_index("x") + lax.axis_index("y"), lax.axis_index("c")]).astype(jnp.int32)
    return pl.pallas_call(
        body, name=name,
        grid_spec=pltpu.PrefetchScalarGridSpec(
            num_scalar_prefetch=1, grid=(1,),
            in_specs=[pl.BlockSpec((1, h, C), lambda i, ids: (ids[0], ids[1], 0)),
                      pl.BlockSpec((1, h, C), lambda i, ids: (ids[0], 0, 0)),
                      pl.BlockSpec((3, h, C), lambda i, ids: (0, 0, 0))],
            out_specs=pl.BlockSpec((h, C), lambda i, ids: (ids[1], 0))),
        out_shape=jax.ShapeDtypeStruct((2 * h, C), F32), compiler_params=_cp("arbitrary"))(ids, g, got, landed)


def _pair_share(reds, name):
    n = len(reds)

    def body(*refs):
        out = refs[n:2 * n]
        send_sems, recv_sems = refs[2 * n:]
        x, y, c, _ = _place()
        def half(w, which):
            h = out[w].shape[0] // 2
            return out[w].at[pl.ds(which * h, h), :]

        cps = [_remote(half(w, c), half(w, c), send_sems, recv_sems, w, (x, y, 1 - c)) for w in range(n)]
        for cp in cps:
            cp.start()
        for w in range(n):
            _remote(half(w, 1 - c), half(w, 1 - c), send_sems, recv_sems, w, (x, y, 1 - c)).wait_recv()
        for cp in cps:
            cp.wait_send()

    return pl.pallas_call(
        body, name=name, in_specs=[ANY] * n, out_specs=[ANY] * n,
        out_shape=[jax.ShapeDtypeStruct(r.shape, F32) for r in reds],
        input_output_aliases={w: w for w in range(n)},
        scratch_shapes=[pltpu.SemaphoreType.DMA((n,)), pltpu.SemaphoreType.DMA((n,))],
    )(*reds)


HBM = pl.BlockSpec(memory_space=pltpu.HBM)
SEM = pl.BlockSpec(memory_space=pltpu.SEMAPHORE)
DATAFLOW = pltpu.SideEffectType.DATAFLOW_SIDE_EFFECTING


def _copies_start(name, srcs, lands, plan, n_copies, after):
    ns, nb, na = len(srcs), len(srcs) + len(lands), len(after)

    def body(*refs):
        src_refs, land_refs = refs[:ns], refs[ns:nb]
        send_sems, recv_sems = refs[nb + na:nb + na + 2]
        token = refs[-1]
        for k, (src, there, _, to) in enumerate(plan(src_refs, land_refs)):
            _remote(src, there, send_sems, recv_sems, k, to).start()
        token[...] = jnp.zeros_like(token)

    hbm = lambda a: pltpu.HBM(a.shape, a.dtype)
    outs = pl.pallas_call(
        body, name=name,
        out_shape=(pltpu.SemaphoreType.DMA((n_copies,)), pltpu.SemaphoreType.DMA((n_copies,)),
                   *[hbm(a) for a in srcs], *[hbm(a) for a in lands], jax.ShapeDtypeStruct((8, 128), F32)),
        in_specs=[HBM] * nb + [ANY] * na,
        out_specs=(SEM, SEM, *[HBM] * nb, pl.BlockSpec(memory_space=pltpu.VMEM)),
        input_output_aliases={i: 2 + i for i in range(nb)},
        compiler_params=pltpu.CompilerParams(has_side_effects=DATAFLOW),
    )(*[pltpu.with_memory_space_constraint(a, pltpu.HBM) for a in (*srcs, *lands)], *after)
    return outs[0], outs[1], outs[2:2 + ns], outs[2 + ns:2 + nb], outs[-1]


def _copies_wait(name, send_sems, recv_sems, srcs, lands, plan, after):
    ns, nb, na = len(srcs), len(srcs) + len(lands), len(after)

    def body(*refs):
        src_refs, land_refs = refs[:ns], refs[ns:nb]
        send_sems, recv_sems = refs[nb:nb + 2]
        for k, (src, _, here, to) in enumerate(plan(src_refs, land_refs)):
            cp = _remote(src, here, send_sems, recv_sems, k, to)
            cp.wait_send()
            cp.wait_recv()

    hbm = lambda a: pltpu.HBM(a.shape, a.dtype)
    outs = pl.pallas_call(
        body, name=name,
        out_shape=(*[hbm(a) for a in srcs], *[hbm(a) for a in lands]),
        in_specs=[HBM] * nb + [SEM, SEM] + [ANY] * na,
        out_specs=tuple([HBM] * nb),
        input_output_aliases={i: i for i in range(nb)},
        compiler_params=pltpu.CompilerParams(has_side_effects=DATAFLOW),
    )(*srcs, *lands, send_sems, recv_sems, *after)
    return outs[:ns], outs[ns:]


def _gather_plan(halves):
    def plan(shards, lands):
        x, y, c, chips = _place()
        me = 2 * x + y
        copies = []
        for w, h in enumerate(halves):
            rows = pl.ds(c * h, h)
            for chip in chips:
                copies.append((shards[w].at[rows, :], lands[w].at[me, rows, :],
                               lands[w].at[2 * chip[0] + chip[1], rows, :], (*chip, c)))
        return copies
    return plan


def _reduce_plan(n):
    def plan(ps, lands):
        x, y, c, chips = _place()
        return [(ps[w].at[2 * chip[0] + chip[1]], lands[w].at[j], lands[w].at[j], (*chip, c))
                for w in range(n) for j, chip in enumerate(chips)]
    return plan


def _forward_plan(halves):
    def plan(_, lands):
        x, y, c, chips = _place()

        def part(w, chip, half):
            return lands[w].at[2 * chip[0] + chip[1], pl.ds(half * halves[w], halves[w]), :]

        return [(part(w, chip, c), part(w, chip, c), part(w, chip, 1 - c), (x, y, 1 - c))
                for w in range(len(halves)) for chip in chips]
    return plan


def _pair_plan(halves):
    def plan(gs, gots):
        x, y, c, _ = _place()
        return [(gs[w].at[:, pl.ds((1 - c) * h, h), :], gots[w], gots[w], (x, y, 1 - c)) for w, h in enumerate(halves)]
    return plan


def _place_own(gathered, shards):
    chip = 2 * lax.axis_index("x") + lax.axis_index("y")
    return [lax.dynamic_update_slice(o, s[None], (chip, 0, 0)) for o, s in zip(gathered, shards)]


def _adamw(w, g, m, v, name, tr=None):
    R, C = w.shape
    tr = tr or R // 4

    def body(w_ref, g_ref, m_ref, v_ref, d_ref, nm_ref, nv_ref):
        gv = g_ref[...]
        nm = ADAM_B1 * m_ref[...] + (1.0 - ADAM_B1) * gv
        nv = ADAM_B2 * v_ref[...] + (1.0 - ADAM_B2) * (gv * gv)
        m_hat = nm / (1.0 - ADAM_B1 ** ADAM_STEP)
        v_hat = nv / (1.0 - ADAM_B2 ** ADAM_STEP)
        d_ref[...] = -ADAM_LR * (m_hat / (jnp.sqrt(v_hat) + ADAM_EPS) + ADAM_WD * w_ref[...])
        nm_ref[...] = nm
        nv_ref[...] = nv

    blk = pl.BlockSpec((tr, C), lambda i: (i, 0))
    return pl.pallas_call(body, name=name, grid=(R // tr,), in_specs=[blk] * 4, out_specs=[blk] * 3,
                          out_shape=[jax.ShapeDtypeStruct((R, C), F32)] * 3, compiler_params=_cp("arbitrary"))(w, g, m, v)


SMALL = (("norm1_g", 1024), ("attn_norm_g", 512), ("hgrn_norm_g", 512), ("hgrn_lb_logits", 1024), ("norm2_g", 1024),
         ("conv_b", D_FF), ("final_norm_g", 1024), ("conv_w", 3 * D_FF))
SMALL_ROWS = 136


def _pack(parts, rows):
    flat = jnp.concatenate([p.reshape(-1).astype(F32) for p in parts])
    return jnp.pad(flat, (0, rows * 128 - flat.shape[0])).reshape(rows, 128)


def kernel(x, norm1_g, w_in, attn_norm_g, hgrn_norm_g, hgrn_lb_logits, w_out, norm2_g, w_up, conv_w, conv_b, w_down, final_norm_g, loss_target, m_norm1_g, m_w_in, m_attn_norm_g, m_hgrn_norm_g, m_hgrn_lb_logits, m_w_out, m_norm2_g, m_w_up, m_conv_w, m_conv_b, m_w_down, m_final_norm_g, v_norm1_g, v_w_in, v_attn_norm_g, v_hgrn_norm_g, v_hgrn_lb_logits, v_w_out, v_norm2_g, v_w_up, v_conv_w, v_conv_b, v_w_down, v_final_norm_g):
    w = dict(norm1_g=norm1_g, w_in=w_in, attn_norm_g=attn_norm_g, hgrn_norm_g=hgrn_norm_g,
             hgrn_lb_logits=hgrn_lb_logits, w_out=w_out, norm2_g=norm2_g, w_up=w_up, conv_w=conv_w, conv_b=conv_b,
             w_down=w_down, final_norm_g=final_norm_g)
    m = dict(norm1_g=m_norm1_g, w_in=m_w_in, attn_norm_g=m_attn_norm_g, hgrn_norm_g=m_hgrn_norm_g,
             hgrn_lb_logits=m_hgrn_lb_logits, w_out=m_w_out, norm2_g=m_norm2_g, w_up=m_w_up, conv_w=m_conv_w,
             conv_b=m_conv_b, w_down=m_w_down, final_norm_g=m_final_norm_g)
    v = dict(norm1_g=v_norm1_g, w_in=v_w_in, attn_norm_g=v_attn_norm_g, hgrn_norm_g=v_hgrn_norm_g,
             hgrn_lb_logits=v_hgrn_lb_logits, w_out=v_w_out, norm2_g=v_norm2_g, w_up=v_w_up, conv_w=v_conv_w,
             conv_b=v_conv_b, w_down=v_w_down, final_norm_g=v_final_norm_g)
    names = list(w)
    chip = 2 * lax.axis_index("x") + lax.axis_index("y")

    shards = {k: w[k][0].astype(BF16) for k in BIG}
    w_in4, conv_w4 = _gather_weights([shards["w_in"]], conv_w[0])
    conv_w_full = jnp.transpose(conv_w4, (1, 0, 2)).reshape(3, D_FF)
    lb = jax.nn.softmax(hgrn_lb_logits, axis=0)[0:1]
    late = [shards[k] for k in BIG[1:]]
    gather_plan = _gather_plan([s.shape[0] // 2 for s in late])
    started = _copies_start("gather_start", late, [lax.empty((N_CHIPS,) + s.shape, BF16) for s in late], gather_plan,
                            3 * len(late), after=(w_in4,))
    u1, qkv, hg = _in_proj(x[0], norm1_g + started[4][0:1, 0:1], w_in4)
    attn_o, lse = _attn_fwd(qkv)
    late, landed_w = _copies_wait("gather_wait", *started[:4], gather_plan, after=(attn_o,))
    forward_plan = _forward_plan([s.shape[0] // 2 for s in late])
    started = _copies_start("forward_start", [], landed_w, forward_plan, 3 * len(late), after=())
    rec_o, states = _hgrn_fwd(hg, lb + started[4][0:1, 0:1])
    a = dict(u1=u1, qkv=qkv, hg=hg, attn_o=attn_o, lse=lse, rec_o=rec_o, states=states)
    w_out4, w_up4, w_down4 = _place_own(
        _copies_wait("forward_wait", *started[:4], forward_plan, after=(rec_o,))[1], late)

    b = _step_channel(a, x[0], loss_target[0], attn_norm_g, hgrn_norm_g, w_out4.reshape(D_MODEL, D_MODEL), norm2_g,
                      w_up4, conv_w_full, conv_b, w_down4.reshape(D_FF, D_MODEL), final_norm_g.reshape(1, D_MODEL))

    early = [b["dw_out"], b["dw_up"], b["dw_down"]]
    pair_plan = _pair_plan([gk.shape[1] // 2 for gk in early])
    started = _copies_start("pair_start", early,
                            [lax.empty((N_CHIPS, gk.shape[1] // 2, gk.shape[2]), BF16) for gk in early], pair_plan,
                            len(early), after=())
    dqkv = _attn_bwd(qkv, attn_o, lse, b["da"], started[4])
    early, gots = _copies_wait("pair_wait", *started[:4], pair_plan, after=(dqkv[0],))
    ps = [_pair_sum(gk, got, f"pair_sum_{k}") for gk, got, k in zip(early, gots, BIG[1:])]
    reduce_plan = _reduce_plan(len(ps))
    started = _copies_start("reduce_start", ps, [lax.empty((3,) + p.shape[1:], BF16) for p in ps], reduce_plan,
                            3 * len(ps), after=())
    c = _step_mixers_bwd(a, b, x[0], norm1_g, w_in4, lb + started[4][0:1, 0:1], dqkv)
    gots_in = _pair_exchange([c["dw_in"]], "pair_exchange_w_in")
    ps_in = _pair_sum(c["dw_in"], gots_in[0], "pair_sum_w_in")
    plan_in = _reduce_plan(1)
    started_in = _copies_start("reduce_start_w_in", [ps_in], [lax.empty((3,) + ps_in.shape[1:], BF16)], plan_in, 3,
                               after=())
    landed = _copies_wait("reduce_wait", *started[:4], reduce_plan, after=(started_in[4],))[1]
    reds = [_sum_partials(gk, got, l, f"sum_partials_{k}") for gk, got, l, k in zip(early, gots, landed, BIG[1:])]
    g = dict(zip(BIG[1:], _pair_share(reds, "pair_share")))
    delta, new_m, new_v = {}, {}, {}
    for k in BIG[1:]:
        delta[k], new_m[k], new_v[k] = _adamw(w[k][0], g[k], m[k][0], v[k][0], f"adamw_{k}")

    loss, dx = b["loss"], c["dx"]
    small = dict(g1=c["dg1"], g_a=b["dga"], g_h=b["dgh"], lb=c["dlb"], g2=b["dg2"], conv_w=b["dcw"], conv_b=b["dcb"],
                 gf=b["dgf"])
    dlb = small["lb"] * lb * (1.0 - lb)
    grads_small = dict(norm1_g=small["g1"], attn_norm_g=small["g_a"], hgrn_norm_g=small["g_h"],
                       hgrn_lb_logits=jnp.concatenate([dlb, -dlb], axis=0), norm2_g=small["g2"],
                       conv_b=small["conv_b"], final_norm_g=small["gf"], conv_w=small["conv_w"])
    summed = _allreduce_small(_pack([grads_small[k] for k, _ in SMALL] + [loss[0, 0:1]], SMALL_ROWS)).reshape(-1)
    off = 0
    for k, size in SMALL:
        g[k] = summed[off:off + size]
        off += size
    loss_total = summed[off]
    g["conv_w"] = lax.dynamic_slice(g["conv_w"].reshape(3, D_FF), (0, chip * (D_FF // N_CHIPS)), (3, D_FF // N_CHIPS))
    small_names = [k for k in names if k not in BIG]
    rows = 80
    packed = _adamw(_pack([w[k] for k in small_names], rows), _pack([g[k] for k in small_names], rows),
                    _pack([m[k] for k in small_names], rows), _pack([v[k] for k in small_names], rows), "adamw_small", tr=rows)
    flat = [a.reshape(-1) for a in packed]
    off = 0
    for k in small_names:
        size = w[k].size
        delta[k], new_m[k], new_v[k] = (a[off:off + size].reshape(w[k].shape) for a in flat)
        g[k] = g[k].reshape(w[k].shape)
        off += size

    landed_in = _copies_wait("reduce_wait_w_in", *started_in[:4], plan_in, after=(packed[0], delta["w_up"]))[1]
    red_in = _sum_partials(c["dw_in"], gots_in[0], landed_in[0], "sum_partials_w_in")
    g["w_in"] = _pair_share([red_in], "pair_share_w_in")[0]
    delta["w_in"], new_m["w_in"], new_v["w_in"] = _adamw(w_in[0], g["w_in"], m_w_in[0], v_w_in[0], "adamw_w_in")
    for k in BIG:
        g[k], delta[k], new_m[k], new_v[k] = g[k][None], delta[k][None], new_m[k][None], new_v[k][None]

    return (loss_total, dx[None], *[g[k] for k in names], *[delta[k] for k in names],
            *[new_m[k] for k in names], *[new_v[k] for k in names])
```

```python
import functools
import math

import jax
import jax.numpy as jnp
from jax import lax
from jax.experimental import pallas as pl
from jax.experimental.pallas import tpu as pltpu

F32 = jnp.float32
BF16 = jnp.bfloat16

D_MODEL = 1024
ATTN_W = 512
HGRN_W = 512
HEAD_PAIR = 128
ATTN_BLK = 128
DILATIONS = (1, 4, 16)
ATTN_CHAINS = 4
ATTN_CHAINS_FWD = 8
HGRN_HEADS = 4
HGRN_DIM = 128
HGRN_CHUNK = 64
SUPER = 256
HGRN_SIDE = 2
D_FF = 2816
FF_CHUNKS = ((0, 1536), (1536, D_FF))
N_CHIPS = 4
IN_TOTAL = 3584
IN_SHARD = IN_TOTAL // N_CHIPS
UP_SHARD = 2 * D_FF // N_CHIPS
QKV_W = 3 * ATTN_W
HG_W = 4 * HGRN_W
EPS = 1e-6
NEG = -1e30
V7X_VMEM_BYTES = 64 * 1024 * 1024
VMEM_LIMIT = V7X_VMEM_BYTES - 8 * 1024 * 1024

ADAM_LR = 0.001
ADAM_B1 = 0.9
ADAM_B2 = 0.999
ADAM_EPS = 1e-08
ADAM_WD = 0.01
ADAM_STEP = 10

MESH = pl.DeviceIdType.MESH


def _cp(*sem):
    return pltpu.CompilerParams(dimension_semantics=sem or None, vmem_limit_bytes=VMEM_LIMIT)


def _dot(a, b):
    return jnp.dot(a, b, preferred_element_type=F32)


def _dot_nt(a, b):
    return lax.dot_general(a, b, (((1,), (1,)), ((), ())), preferred_element_type=F32)


def _dot_tn(a, b):
    return lax.dot_general(a, b, (((0,), (0,)), ((), ())), preferred_element_type=F32)


def _sigmoid(x):
    return 1.0 / (1.0 + jnp.exp(-x))


def _rms(x, width):
    return lax.rsqrt(jnp.sum(x * x, axis=-1, keepdims=True) * (1.0 / width) + EPS)


def _rms_bwd(dn, n, r, width):
    return r * (dn - n * (jnp.sum(dn * n, axis=-1, keepdims=True) * (1.0 / width)))


def _colsum(x):
    return jnp.sum(x, axis=0, keepdims=True)


def _row(v, k):
    rid = lax.broadcasted_iota(jnp.int32, v.shape, 0)
    return jnp.sum(jnp.where(rid == k, v, 0.0), axis=0, keepdims=True)


def _full(shape):
    return pl.BlockSpec(shape, lambda *_: (0,) * len(shape))


def _once(shape):
    return pl.BlockSpec(shape, lambda *_: (0,) * len(shape), pipeline_mode=pl.Buffered(1))


def _load_side_by_side(w_hbm, w_full, sem):
    width = w_hbm.shape[2]
    cps = [pltpu.make_async_copy(w_hbm.at[k], w_full.at[:, pl.ds(k * width, width)], sem.at[k]) for k in range(N_CHIPS)]
    for cp in cps:
        cp.start()
    for cp in cps:
        cp.wait()


def _in_proj(x, g1, w_in4, tm=512):
    T = x.shape[0]

    def body(x_ref, g_ref, w_hbm, u_ref, qkv_ref, hg_ref, w_full, sem):
        @pl.when(pl.program_id(0) == 0)
        def _():
            _load_side_by_side(w_hbm, w_full, sem)

        xv = x_ref[...]
        u = (xv * _rms(xv, D_MODEL) * g_ref[...]).astype(BF16)
        u_ref[...] = u
        p = _dot(u, w_full[...])
        qkv_ref[...] = p[:, :QKV_W]
        hg_ref[...] = p[:, QKV_W:]

    return pl.pallas_call(
        body, name="in_proj", grid=(T // tm,),
        in_specs=[pl.BlockSpec((tm, D_MODEL), lambda i: (i, 0)), _full((1, D_MODEL)), ANY],
        out_specs=[pl.BlockSpec((tm, D_MODEL), lambda i: (i, 0)), pl.BlockSpec((tm, QKV_W), lambda i: (i, 0)),
                   pl.BlockSpec((tm, HG_W), lambda i: (i, 0))],
        out_shape=[jax.ShapeDtypeStruct((T, D_MODEL), BF16), jax.ShapeDtypeStruct((T, QKV_W), F32),
                   jax.ShapeDtypeStruct((T, HG_W), F32)],
        scratch_shapes=[pltpu.VMEM((D_MODEL, IN_TOTAL), BF16), pltpu.SemaphoreType.DMA((N_CHIPS,))],
        compiler_params=_cp("arbitrary"),
    )(x, g1, w_in4)


def _attn_masks(bias_ref):
    lane = lax.broadcasted_iota(jnp.int32, (ATTN_BLK, HEAD_PAIR), 1)
    row = lax.broadcasted_iota(jnp.int32, (2 * ATTN_BLK, 2 * ATTN_BLK), 0)
    col = lax.broadcasted_iota(jnp.int32, (2 * ATTN_BLK, 2 * ATTN_BLK), 1)
    base = jnp.where(row >= ATTN_BLK, row - ATTN_BLK, row) - col
    for k in range(2):
        dist = base + k * ATTN_BLK
        bias_ref[k] = jnp.where((dist >= 0) & (dist <= ATTN_BLK), 0.0, NEG)
    return lane < 64


def _two_heads(blk, first):
    zero = jnp.zeros_like(blk)
    return jnp.concatenate([jnp.where(first, blk, zero), jnp.where(first, zero, blk)], axis=0)


def _attn_rows(idx, nb, d):
    r, n = idx // nb, idx % nb
    kb = jnp.maximum(n - 1, 0)
    if d == 1:
        q0 = pl.multiple_of(n * ATTN_BLK, ATTN_BLK)
        k0 = pl.multiple_of(kb * ATTN_BLK, ATTN_BLK)
        return pl.ds(q0, ATTN_BLK), pl.ds(k0, 2 * ATTN_BLK), n - kb
    return (pl.ds(r + d * ATTN_BLK * n, ATTN_BLK, stride=d), pl.ds(r + d * ATTN_BLK * kb, 2 * ATTN_BLK, stride=d),
            n - kb)


def _attn_fwd(qkv):
    T = qkv.shape[0]

    per_chain = T // ATTN_BLK // ATTN_CHAINS_FWD

    def body(q_ref, k_ref, v_ref, o_ref, m_ref, l_ref, bias_ref):
        first = _attn_masks(bias_ref)
        for bi, d in enumerate(DILATIONS):
            nb = T // d // ATTN_BLK

            def block(idx, d=d, nb=nb, bi=bi):
                rows, keys, which = _attn_rows(idx, nb, d)
                q2 = _two_heads(q_ref[rows, :] * 0.125, first).astype(BF16)
                kw = k_ref[keys, :].astype(BF16)
                vw = v_ref[keys, :].astype(BF16)
                old = (o_ref[rows, :], m_ref[rows, :], l_ref[rows, :]) if bi else None
                s = _dot_nt(q2, kw) + bias_ref[which]
                mb = jnp.max(s, axis=-1, keepdims=True)
                p = jnp.exp(s - mb)
                lb = jnp.sum(p, axis=-1, keepdims=True)
                o2 = _dot(p.astype(BF16), vw)
                o = jnp.where(first, o2[:ATTN_BLK], o2[ATTN_BLK:])
                m = jnp.where(first, mb[:ATTN_BLK], mb[ATTN_BLK:])
                l = jnp.where(first, lb[:ATTN_BLK], lb[ATTN_BLK:])
                if bi:
                    po, pm, pl_ = old
                    mn = jnp.maximum(pm, m)
                    wa = jnp.exp(pm - mn)
                    wb = jnp.exp(m - mn)
                    o, l, m = po * wa + o * wb, pl_ * wa + l * wb, mn
                return rows, o, m, l

            def step(i, carry, block=block):
                done = [block(i + ch * per_chain) for ch in range(ATTN_CHAINS_FWD)]
                for rows, o, m, l in done:
                    o_ref[rows, :] = o
                    m_ref[rows, :] = m
                    l_ref[rows, :] = l
                return carry

            lax.fori_loop(0, per_chain, step, 0)

        def finish(i, carry):
            rows = pl.ds(pl.multiple_of(i * SUPER, SUPER), SUPER)
            l = l_ref[rows, :]
            o_ref[rows, :] = o_ref[rows, :] / l
            m_ref[rows, :] = m_ref[rows, :] + jnp.log(l)
            return carry

        lax.fori_loop(0, T // SUPER, finish, 0)

    col = lambda off: pl.BlockSpec((T, HEAD_PAIR), lambda j: (0, off + j))
    return pl.pallas_call(
        body, name="attn_fwd", grid=(4,),
        in_specs=[col(0), col(4), col(8)], out_specs=[col(0), col(0)],
        out_shape=[jax.ShapeDtypeStruct((T, ATTN_W), F32)] * 2,
        scratch_shapes=[pltpu.VMEM((T, HEAD_PAIR), F32), pltpu.VMEM((2, 2 * ATTN_BLK, 2 * ATTN_BLK), F32)],
        compiler_params=_cp("arbitrary"),
    )(qkv, qkv, qkv)


def _attn_bwd(qkv, o, lse, do, token=None):
    T = qkv.shape[0]
    per_chain = T // ATTN_BLK // ATTN_CHAINS
    extra = [] if token is None else [token]

    def body(q_ref, k_ref, v_ref, o_ref, lse_ref, do_ref, *rest):
        outs = rest[len(extra):len(extra) + 3]
        dq_ref, dk_ref, dv_ref, bias_ref = rest[len(extra) + 3:]
        first = _attn_masks(bias_ref)
        dq_ref[...] = jnp.zeros_like(dq_ref)
        dk_ref[...] = jnp.zeros_like(dk_ref)
        dv_ref[...] = jnp.zeros_like(dv_ref)
        for d in DILATIONS:
            nb = T // d // ATTN_BLK

            def block(idx, d=d, nb=nb):
                rows, keys, which = _attn_rows(idx, nb, d)
                q2 = _two_heads(q_ref[rows, :] * 0.125, first).astype(BF16)
                kw = k_ref[keys, :].astype(BF16)
                vw = v_ref[keys, :].astype(BF16)
                lse_b = lse_ref[rows, :]
                dob = do_ref[rows, :]
                prod = dob * o_ref[rows, :]
                old = dq_ref[rows, :], dk_ref[keys, :], dv_ref[keys, :]
                lse2 = jnp.concatenate(
                    [jnp.max(jnp.where(first, lse_b, NEG), axis=-1, keepdims=True),
                     jnp.max(jnp.where(first, NEG, lse_b), axis=-1, keepdims=True)], axis=0)
                p = jnp.exp(_dot_nt(q2, kw) + (bias_ref[which] - lse2))
                delta = jnp.concatenate(
                    [jnp.sum(jnp.where(first, prod, 0.0), axis=-1, keepdims=True),
                     jnp.sum(jnp.where(first, 0.0, prod), axis=-1, keepdims=True)], axis=0)
                do2 = _two_heads(dob, first).astype(BF16)
                ds = (p * (_dot_nt(do2, vw) - delta)).astype(BF16)
                dq2 = _dot(ds, kw) * 0.125
                return (rows, keys, old[0] + jnp.where(first, dq2[:ATTN_BLK], dq2[ATTN_BLK:]),
                        old[1] + _dot_tn(ds, q2), old[2] + _dot_tn(p.astype(BF16), do2))

            def step(i, carry, block=block):
                done = [block(i + ch * per_chain) for ch in range(ATTN_CHAINS)]
                for rows, keys, dq, dk, dv in done:
                    dq_ref[rows, :] = dq
                    dk_ref[keys, :] = dk
                    dv_ref[keys, :] = dv
                return carry

            lax.fori_loop(0, per_chain, step, 0)

        def emit(i, carry):
            rows = pl.ds(pl.multiple_of(i * SUPER, SUPER), SUPER)
            for out, acc in zip(outs, (dq_ref, dk_ref, dv_ref)):
                out[rows, :] = acc[rows, :].astype(BF16)
            return carry

        lax.fori_loop(0, T // SUPER, emit, 0)

    col = lambda off: pl.BlockSpec((T, HEAD_PAIR), lambda j: (0, off + j))
    return pl.pallas_call(
        body, name="attn_bwd", grid=(4,),
        in_specs=[col(0), col(4), col(8), col(0), col(0), col(0)] + [_full(t.shape) for t in extra],
        out_specs=[col(0)] * 3,
        out_shape=[jax.ShapeDtypeStruct((T, ATTN_W), BF16)] * 3,
        scratch_shapes=[pltpu.VMEM((T, HEAD_PAIR), F32)] * 3 + [pltpu.VMEM((2, 2 * ATTN_BLK, 2 * ATTN_BLK), F32)],
        compiler_params=_cp("arbitrary"),
    )(qkv, qkv, qkv, o, lse, do, *extra)


def _chunk_ids():
    row = lax.broadcasted_iota(jnp.int32, (SUPER, HGRN_DIM), 0)
    r2 = lax.broadcasted_iota(jnp.int32, (SUPER, SUPER), 0)
    c2 = lax.broadcasted_iota(jnp.int32, (SUPER, SUPER), 1)
    amask = ((r2 // HGRN_CHUNK) == (c2 // HGRN_CHUNK)) & (c2 <= r2)
    return row % HGRN_CHUNK, row // HGRN_CHUNK, amask


def _cumsum_chunk(x, rmod):
    s = 1
    while s < HGRN_CHUNK:
        x = x + jnp.where(rmod >= s, pltpu.roll(x, s, 0), 0.0)
        s *= 2
    return x


def _suffix_sum_chunk(x, rmod):
    s = 1
    while s < HGRN_CHUNK:
        x = x + jnp.where(rmod < HGRN_CHUNK - s, pltpu.roll(x, SUPER - s, 0), 0.0)
        s *= 2
    return x


def _chunk_rows(vs, cid):
    out = vs[-1]
    for c in reversed(range(len(vs) - 1)):
        out = jnp.where(cid == c, vs[c], out)
    return out


def _expand(x, cid):
    return jnp.concatenate([jnp.where(cid == c, x, 0.0) for c in range(SUPER // HGRN_CHUNK)], axis=1)


def _hgrn_gates(q, f, lbv, rmod, cid, tmp):
    sq = _sigmoid(q)
    sg = _sigmoid(f)
    forget = lbv + (1.0 - lbv) * sg
    key = 1.0 - forget
    b = _cumsum_chunk(jnp.log(forget), rmod)
    tmp[...] = b
    bends = [tmp[c * HGRN_CHUNK + HGRN_CHUNK - 1:(c + 1) * HGRN_CHUNK, :] for c in range(SUPER // HGRN_CHUNK)]
    eb = jnp.exp(b)
    enb = jnp.exp(-b)
    ebe = jnp.exp(_chunk_rows(bends, cid) - b)
    return sq, sg, forget, key, eb, enb, ebe, q * sq * eb, key * enb, key * ebe, [jnp.exp(v) for v in bends]


def _hgrn_fwd(hg, lb):
    T = hg.shape[0]
    nsc = T // SUPER
    NC = SUPER // HGRN_CHUNK

    def body(q_ref, f_ref, i_ref, lb_ref, o_ref, st_ref, state, tmp):
        rmod, cid, amask = _chunk_ids()
        state[...] = jnp.zeros_like(state)
        lbv = lb_ref[...]

        def local(sc, u):
            rows = pl.ds(pl.multiple_of(sc * SUPER, SUPER), SUPER)
            iv = i_ref[rows, :].astype(BF16)
            qd, ki, ke, dec = _hgrn_gates(q_ref[rows, :], f_ref[rows, :], lbv, rmod, cid, tmp.at[u])[-4:]
            a = jnp.where(amask, _dot_nt(qd.astype(BF16), ki.astype(BF16)), 0.0)
            return rows, qd, dec, _dot(a.astype(BF16), iv), _dot_tn(iv, _expand(ke, cid).astype(BF16))

        def step(i, carry):
            parts = [local(i * HGRN_SIDE + u, u) for u in range(HGRN_SIDE)]
            st = state[...]
            entering = []
            for u, (_, _, dec, _, ut) in enumerate(parts):
                st_ref[0, i * HGRN_SIDE + u] = st
                sts = []
                for c in range(NC):
                    sts.append(st)
                    st = st * dec[c] + ut[:, c * HGRN_DIM:(c + 1) * HGRN_DIM]
                entering.append(jnp.concatenate(sts, axis=1).astype(BF16))
            state[...] = st
            for (rows, qd, _, o, _), sts in zip(parts, entering):
                o_ref[rows, :] = o + _dot_nt(_expand(qd, cid).astype(BF16), sts)
            return carry

        lax.fori_loop(0, nsc // HGRN_SIDE, step, 0)

    col = lambda off: pl.BlockSpec((T, HGRN_DIM), lambda h: (0, off + h))
    return pl.pallas_call(
        body, name="hgrn_fwd", grid=(HGRN_HEADS,),
        in_specs=[col(0), col(4), col(8), pl.BlockSpec((1, HGRN_DIM), lambda h: (0, h))],
        out_specs=[pl.BlockSpec((T, HGRN_DIM), lambda h: (0, h)),
                   pl.BlockSpec((1, nsc, HGRN_DIM, HGRN_DIM), lambda h: (h, 0, 0, 0))],
        out_shape=[jax.ShapeDtypeStruct((T, HGRN_W), F32),
                   jax.ShapeDtypeStruct((HGRN_HEADS, nsc, HGRN_DIM, HGRN_DIM), F32)],
        scratch_shapes=[pltpu.VMEM((HGRN_DIM, HGRN_DIM), F32), pltpu.VMEM((HGRN_SIDE, SUPER, HGRN_DIM), F32)],
        compiler_params=_cp("arbitrary"),
    )(hg, hg, hg, lb)


def _hgrn_bwd(hg, lb, states, do):
    T = hg.shape[0]
    nsc = T // SUPER
    NC = SUPER // HGRN_CHUNK

    def body(q_ref, f_ref, i_ref, lb_ref, st_ref, do_ref, dq_ref, df_ref, di_ref, dlb_ref, dstate, tmp):
        rmod, cid, amask = _chunk_ids()
        dstate[...] = jnp.zeros_like(dstate)
        dlb_ref[...] = jnp.zeros_like(dlb_ref)
        lbv = lb_ref[...]

        def local(sc, u):
            rows = pl.ds(pl.multiple_of(sc * SUPER, SUPER), SUPER)
            q = q_ref[rows, :]
            ivf = i_ref[rows, :]
            iv = ivf.astype(BF16)
            dof = do_ref[rows, :]
            dob = dof.astype(BF16)
            sq, sg, forget, key, eb, enb, ebe, qd, ki, ke, dec = _hgrn_gates(q, f_ref[rows, :], lbv, rmod, cid,
                                                                            tmp.at[u])
            qdb, kib = qd.astype(BF16), ki.astype(BF16)
            keexp = _expand(ke, cid).astype(BF16)
            a = jnp.where(amask, _dot_nt(qdb, kib), 0.0).astype(BF16)
            ut = _dot_tn(iv, keexp)
            st = st_ref[0, sc]
            sts = []
            for c in range(NC):
                sts.append(st)
                st = st * dec[c] + ut[:, c * HGRN_DIM:(c + 1) * HGRN_DIM]
            gt = _dot_tn(dob, _expand(qd, cid).astype(BF16))
            da = jnp.where(amask, _dot_nt(dob, iv), 0.0).astype(BF16)
            ststack = jnp.concatenate(sts, axis=0).astype(BF16)
            return dict(rows=rows, q=q, sq=sq, sg=sg, forget=forget, eb=eb, enb=enb, ebe=ebe, qd=qd, ki=ki, ke=ke,
                        dec=dec, sts=sts, gt=gt, keexp=keexp, ivexp=_expand(ivf, cid).astype(BF16),
                        div=_dot_tn(a, dob), dki=_dot_tn(da, qdb),
                        dqd=_dot(da, kib) + _dot(_expand(dof, cid).astype(BF16), ststack))

        def finish(p, nxt, ddec):
            ncat = jnp.concatenate(nxt, axis=1).astype(BF16)
            nstack = jnp.concatenate(nxt, axis=0).astype(BF16)
            dke = _dot(p["ivexp"], nstack)
            dkk = dke * p["ke"]
            dkey = p["dki"] * p["enb"] + dke * p["ebe"]
            db = p["dqd"] * p["qd"] - p["dki"] * p["ki"] - dkk
            dbends = [_colsum(jnp.where(cid == c, dkk, 0.0)) + ddec[c] * p["dec"][c] for c in range(NC)]
            dforget = (_suffix_sum_chunk(db, rmod) + _chunk_rows(dbends, cid)) / p["forget"] - dkey
            sg, sq, q = p["sg"], p["sq"], p["q"]
            df_ref[p["rows"], :] = (dforget * (1.0 - lbv) * sg * (1.0 - sg)).astype(BF16)
            dq_ref[p["rows"], :] = (p["dqd"] * p["eb"] * (sq * (1.0 + q * (1.0 - sq)))).astype(BF16)
            di_ref[p["rows"], :] = (p["div"] + _dot_nt(p["keexp"], ncat)).astype(BF16)
            return _colsum(dforget * (1.0 - sg))

        def step(i, carry):
            parts = [local(nsc - 1 - (i * HGRN_SIDE + u), u) for u in range(HGRN_SIDE)]
            dst = dstate[...]
            chained = []
            for p in parts:
                nxt = [None] * NC
                ddec = [None] * NC
                for c in reversed(range(NC)):
                    nxt[c] = dst
                    ddec[c] = _colsum(dst * p["sts"][c])
                    dst = dst * p["dec"][c] + p["gt"][:, c * HGRN_DIM:(c + 1) * HGRN_DIM]
                chained.append((nxt, ddec))
            dstate[...] = dst
            dlb = dlb_ref[...]
            for p, (nxt, ddec) in zip(parts, chained):
                dlb = dlb + finish(p, nxt, ddec)
            dlb_ref[...] = dlb
            return carry

        lax.fori_loop(0, nsc // HGRN_SIDE, step, 0)

    col = lambda off: pl.BlockSpec((T, HGRN_DIM), lambda h: (0, off + h))
    own = pl.BlockSpec((T, HGRN_DIM), lambda h: (0, h))
    vec = pl.BlockSpec((1, HGRN_DIM), lambda h: (0, h))
    return pl.pallas_call(
        body, name="hgrn_bwd", grid=(HGRN_HEADS,),
        in_specs=[col(0), col(4), col(8), vec,
                  pl.BlockSpec((1, nsc, HGRN_DIM, HGRN_DIM), lambda h: (h, 0, 0, 0)), own],
        out_specs=[own, own, own, vec],
        out_shape=[jax.ShapeDtypeStruct((T, HGRN_W), BF16)] * 3 + [jax.ShapeDtypeStruct((1, HGRN_W), F32)],
        scratch_shapes=[pltpu.VMEM((HGRN_DIM, HGRN_DIM), F32), pltpu.VMEM((HGRN_SIDE, SUPER, HGRN_DIM), F32)],
        compiler_params=_cp("arbitrary"),
    )(hg, hg, hg, lb, states, do)


def _rec_heads(rec, gate, g_h):
    rr = jnp.concatenate(
        [jnp.broadcast_to(_rms(rec[:, h * HGRN_DIM:(h + 1) * HGRN_DIM], HGRN_DIM), (rec.shape[0], HGRN_DIM))
         for h in range(HGRN_HEADS)], axis=1)
    rn = rec * rr
    sg = _sigmoid(gate)
    return rr, rn, sg


def _mix_out(attn_o, rec_o, hg, x, g_a, g_h, w_out, tm=512):
    T = x.shape[0]

    def body(a_ref, r_ref, gt_ref, x_ref, ga_ref, gh_ref, w_ref, h1_ref, mixed_ref):
        a = a_ref[...]
        an = a * _rms(a, ATTN_W) * ga_ref[...]
        gate = gt_ref[...]
        _, rn, sg = _rec_heads(r_ref[...], gate, gh_ref[...])
        mixed = jnp.concatenate([an, rn * gh_ref[...] * (gate * sg)], axis=1).astype(BF16)
        mixed_ref[...] = mixed
        h1_ref[...] = x_ref[...] + _dot(mixed, w_ref[...])

    row = lambda w: pl.BlockSpec((tm, w), lambda i: (i, 0))
    return pl.pallas_call(
        body, name="mix_out", grid=(T // tm,),
        in_specs=[row(ATTN_W), row(HGRN_W), pl.BlockSpec((tm, HGRN_W), lambda i: (i, 3)), row(D_MODEL),
                  _full((1, ATTN_W)), _full((1, HGRN_W)), _once((D_MODEL, D_MODEL))],
        out_specs=[row(D_MODEL), row(D_MODEL)],
        out_shape=[jax.ShapeDtypeStruct((T, D_MODEL), F32), jax.ShapeDtypeStruct((T, D_MODEL), BF16)],
        compiler_params=_cp("arbitrary"),
    )(attn_o, rec_o, hg, x, g_a, g_h, w_out)


_INV_SQRT2 = 1.0 / math.sqrt(2.0)
_INV_SQRT2PI = 1.0 / math.sqrt(2.0 * math.pi)


def _gelu(x):
    return 0.5 * x * (1.0 + lax.erf(x * _INV_SQRT2))


def _gelu_grad(x):
    return 0.5 * (1.0 + lax.erf(x * _INV_SQRT2)) + x * jnp.exp(-0.5 * x * x) * _INV_SQRT2PI


def _shift_down(g, prev, rowid):
    p1 = _row(prev, prev.shape[0] - 1)
    p2 = _row(prev, prev.shape[0] - 2)
    s1 = jnp.where(rowid == 0, p1, pltpu.roll(g, 1, 0))
    s2 = jnp.where(rowid == 0, p2, jnp.where(rowid == 1, p1, pltpu.roll(g, 2, 0)))
    return s1, s2


def _mlp_fwd(h1, g2, w_up4, conv_w, conv_b, w_down, gf, tgt, tm=256):
    T = h1.shape[0]

    def body(h_ref, g2_ref, wu_hbm, cw_ref, cb_ref, wd_ref, gf_ref, t_ref,
             u_ref, gate_ref, val_ref, conv_ref, act_ref, dh_ref, loss_ref, dgf_ref, carry, wu_ref, sem):
        i = pl.program_id(0)

        @pl.when(i == 0)
        def _():
            carry[...] = jnp.zeros_like(carry)
            loss_ref[...] = jnp.zeros_like(loss_ref)
            dgf_ref[...] = jnp.zeros_like(dgf_ref)
            _load_side_by_side(wu_hbm, wu_ref, sem)

        h = h_ref[...]
        u = (h * _rms(h, D_MODEL) * g2_ref[...]).astype(BF16)
        u_ref[...] = u
        y2 = jnp.zeros((tm, D_MODEL), F32)
        for lo, hi in FF_CHUNKS:
            cols = slice(lo, hi)
            rowid = lax.broadcasted_iota(jnp.int32, (tm, hi - lo), 0)
            gb = _dot(u, wu_ref[:, lo:hi]).astype(BF16)
            vb = _dot(u, wu_ref[:, D_FF + lo:D_FF + hi]).astype(BF16)
            gate_ref[:, cols] = gb
            val_ref[:, cols] = vb
            g = gb.astype(F32)
            s1, s2 = _shift_down(g, carry[:, cols], rowid)
            carry[:, cols] = g[tm - 8:, :]
            conv = cb_ref[:, cols] + cw_ref[0:1, cols] * s2 + cw_ref[1:2, cols] * s1 + cw_ref[2:3, cols] * g
            act = (_gelu(conv) * vb.astype(F32)).astype(BF16)
            conv_ref[:, cols] = conv.astype(BF16)
            act_ref[:, cols] = act
            y2 = y2 + _dot(act, wd_ref[cols, :])
        h2 = h + y2
        rf = _rms(h2, D_MODEL)
        n = h2 * rf
        gfv = gf_ref[...]
        e = n * gfv - t_ref[...]
        loss_ref[...] += jnp.sum(e * e) * (0.5 / D_MODEL)
        dy = e * (1.0 / D_MODEL)
        dgf_ref[...] += _colsum(dy * n)
        dh_ref[...] = _rms_bwd(dy * gfv, n, rf, D_MODEL)

    row = lambda w: pl.BlockSpec((tm, w), lambda i: (i, 0))
    return pl.pallas_call(
        body, name="mlp_fwd", grid=(T // tm,),
        in_specs=[row(D_MODEL), _full((1, D_MODEL)), ANY, _full((3, D_FF)),
                  _full((1, D_FF)), _once((D_FF, D_MODEL)), _full((1, D_MODEL)), row(D_MODEL)],
        out_specs=[row(D_MODEL), row(D_FF), row(D_FF), row(D_FF), row(D_FF), row(D_MODEL), _full((1, 128)),
                   _full((1, D_MODEL))],
        out_shape=[jax.ShapeDtypeStruct((T, D_MODEL), BF16)] + [jax.ShapeDtypeStruct((T, D_FF), BF16)] * 4
        + [jax.ShapeDtypeStruct((T, D_MODEL), F32),
                   jax.ShapeDtypeStruct((1, 128), F32), jax.ShapeDtypeStruct((1, D_MODEL), F32)],
        scratch_shapes=[pltpu.VMEM((8, D_FF), F32), pltpu.VMEM((D_MODEL, 2 * D_FF), BF16),
                        pltpu.SemaphoreType.DMA((N_CHIPS,))],
        compiler_params=_cp("arbitrary"),
    )(h1, g2, w_up4, conv_w, conv_b, w_down, gf, tgt)


def _mlp_bwd(dh2, gate, val, conv, conv_w, w_down, tm=256):
    T = dh2.shape[0]
    nb = T // tm
    half = D_FF // 2

    def body(dh_ref, gate_ref, val_ref, conv_ref, cw_ref, wd_ref, dgv_ref, dcw_ref, dcb_ref, carry):
        @pl.when(pl.program_id(0) == 0)
        def _():
            carry[...] = jnp.zeros_like(carry)
            dcw_ref[...] = jnp.zeros_like(dcw_ref)
            dcb_ref[...] = jnp.zeros_like(dcb_ref)

        dhb = dh_ref[...].astype(BF16)
        rowid = lax.broadcasted_iota(jnp.int32, (tm, half), 0)
        for c in range(2):
            cols = slice(c * half, (c + 1) * half)
            g = gate_ref[:, cols].astype(F32)
            v = val_ref[:, cols].astype(F32)
            cv = conv_ref[:, cols].astype(F32)
            dact = _dot_nt(dhb, wd_ref[cols, :])
            dconv = dact * v * _gelu_grad(cv)
            nxt = carry[:, cols]
            n0, n1 = _row(nxt, 0), _row(nxt, 1)
            u1 = jnp.where(rowid == tm - 1, n0, pltpu.roll(dconv, tm - 1, 0))
            u2 = jnp.where(rowid == tm - 1, n1, jnp.where(rowid == tm - 2, n0, pltpu.roll(dconv, tm - 2, 0)))
            carry[:, cols] = dconv[0:8, :]
            dcb_ref[:, cols] += _colsum(dconv)
            dcw_ref[0:1, cols] += _colsum(u2 * g)
            dcw_ref[1:2, cols] += _colsum(u1 * g)
            dcw_ref[2:3, cols] += _colsum(dconv * g)
            dgate = cw_ref[2:3, cols] * dconv + cw_ref[1:2, cols] * u1 + cw_ref[0:1, cols] * u2
            dgv_ref[:, cols] = dgate.astype(BF16)
            dgv_ref[:, D_FF + c * half:D_FF + (c + 1) * half] = (dact * _gelu(cv)).astype(BF16)

    rev = lambda w: pl.BlockSpec((tm, w), lambda i: (nb - 1 - i, 0))
    return pl.pallas_call(
        body, name="mlp_bwd", grid=(nb,),
        in_specs=[rev(D_MODEL), rev(D_FF), rev(D_FF), rev(D_FF), _full((3, D_FF)), _once((D_FF, D_MODEL))],
        out_specs=[rev(2 * D_FF), _full((3, D_FF)), _full((1, D_FF))],
        out_shape=[jax.ShapeDtypeStruct((T, 2 * D_FF), BF16), jax.ShapeDtypeStruct((3, D_FF), F32),
                   jax.ShapeDtypeStruct((1, D_FF), F32)],
        scratch_shapes=[pltpu.VMEM((8, D_FF), F32)],
        compiler_params=_cp("arbitrary"),
    )(dh2, gate, val, conv, conv_w, w_down)


def _up_out_bwd(dgv, w_up4, h1, g2, dh2, w_out, attn_o, rec_o, hg, g_a, g_h, tm=256):
    T = h1.shape[0]

    def body(dgv_ref, wu_hbm, h_ref, g2_ref, dh2_ref, wo_ref, a_ref, r_ref, gt_ref, ga_ref, gh_ref,
             dh1_ref, dg2_ref, da_ref, dr_ref, dgt_ref, dga_ref, dgh_ref, wu_ref, sem):
        @pl.when(pl.program_id(0) == 0)
        def _():
            dg2_ref[...] = jnp.zeros_like(dg2_ref)
            dga_ref[...] = jnp.zeros_like(dga_ref)
            dgh_ref[...] = jnp.zeros_like(dgh_ref)
            _load_side_by_side(wu_hbm, wu_ref, sem)

        du = _dot_nt(dgv_ref[...], wu_ref[...])
        h = h_ref[...]
        r = _rms(h, D_MODEL)
        n = h * r
        dg2_ref[...] += _colsum(du * n)
        dh1 = dh2_ref[...] + _rms_bwd(du * g2_ref[...], n, r, D_MODEL)
        dh1_ref[...] = dh1
        dmix = _dot_nt(dh1.astype(BF16), wo_ref[...])
        dan = dmix[:, :ATTN_W]
        a = a_ref[...]
        ra = _rms(a, ATTN_W)
        na = a * ra
        dga_ref[...] += _colsum(dan * na)
        da_ref[...] = _rms_bwd(dan * ga_ref[...], na, ra, ATTN_W)
        dmr = dmix[:, ATTN_W:]
        gate = gt_ref[...]
        ghv = gh_ref[...]
        rr, rn, sg = _rec_heads(r_ref[...], gate, ghv)
        dgt_ref[...] = (dmr * rn * ghv * (sg * (1.0 + gate * (1.0 - sg)))).astype(BF16)
        drecn = dmr * (gate * sg)
        dgh_ref[...] += _colsum(drecn * rn)
        drn = drecn * ghv
        prod = drn * rn
        mean = jnp.concatenate(
            [jnp.broadcast_to(jnp.sum(prod[:, h_ * HGRN_DIM:(h_ + 1) * HGRN_DIM], axis=-1, keepdims=True),
                              (tm, HGRN_DIM)) for h_ in range(HGRN_HEADS)], axis=1) * (1.0 / HGRN_DIM)
        dr_ref[...] = rr * (drn - rn * mean)

    row = lambda w: pl.BlockSpec((tm, w), lambda i: (i, 0))
    return pl.pallas_call(
        body, name="up_out_bwd", grid=(T // tm,),
        in_specs=[row(2 * D_FF), ANY, row(D_MODEL), _full((1, D_MODEL)),
                  row(D_MODEL), _once((D_MODEL, D_MODEL)), row(ATTN_W), row(HGRN_W),
                  pl.BlockSpec((tm, HGRN_W), lambda i: (i, 3)), _full((1, ATTN_W)), _full((1, HGRN_W))],
        out_specs=[row(D_MODEL), _full((1, D_MODEL)), row(ATTN_W), row(HGRN_W), row(HGRN_W),
                   _full((1, ATTN_W)), _full((1, HGRN_W))],
        out_shape=[jax.ShapeDtypeStruct((T, D_MODEL), F32), jax.ShapeDtypeStruct((1, D_MODEL), F32),
                   jax.ShapeDtypeStruct((T, ATTN_W), F32), jax.ShapeDtypeStruct((T, HGRN_W), F32),
                   jax.ShapeDtypeStruct((T, HGRN_W), BF16), jax.ShapeDtypeStruct((1, ATTN_W), F32),
                   jax.ShapeDtypeStruct((1, HGRN_W), F32)],
        scratch_shapes=[pltpu.VMEM((D_MODEL, 2 * D_FF), BF16), pltpu.SemaphoreType.DMA((N_CHIPS,))],
        compiler_params=_cp("arbitrary"),
    )(dgv, w_up4, h1, g2, dh2, w_out, attn_o, rec_o, hg, g_a, g_h)


def _in_bwd(dqkv, dhg, w_in4, x, g1, dh1, tm=512):
    T = x.shape[0]

    def body(*refs):
        parts = refs[:7]
        w_hbm, x_ref, g_ref, dh1_ref, dp_ref, dx_ref, dg_ref, w_full, sem = refs[7:]

        @pl.when(pl.program_id(0) == 0)
        def _():
            dg_ref[...] = jnp.zeros_like(dg_ref)
            _load_side_by_side(w_hbm, w_full, sem)

        dp = jnp.concatenate([p[...] for p in parts], axis=1)
        dp_ref[...] = dp
        du = _dot_nt(dp, w_full[...])
        xv = x_ref[...]
        r = _rms(xv, D_MODEL)
        n = xv * r
        dg_ref[...] += _colsum(du * n)
        dx_ref[...] = dh1_ref[...] + _rms_bwd(du * g_ref[...], n, r, D_MODEL)

    row = lambda w: pl.BlockSpec((tm, w), lambda i: (i, 0))
    return pl.pallas_call(
        body, name="in_bwd", grid=(T // tm,),
        in_specs=[row(ATTN_W)] * 7 + [ANY, row(D_MODEL), _full((1, D_MODEL)), row(D_MODEL)],
        out_specs=[row(IN_TOTAL), row(D_MODEL), _full((1, D_MODEL))],
        out_shape=[jax.ShapeDtypeStruct((T, IN_TOTAL), BF16), jax.ShapeDtypeStruct((T, D_MODEL), F32),
                   jax.ShapeDtypeStruct((1, D_MODEL), F32)],
        scratch_shapes=[pltpu.VMEM((D_MODEL, IN_TOTAL), BF16), pltpu.SemaphoreType.DMA((N_CHIPS,))],
        compiler_params=_cp("arbitrary"),
    )(*dqkv, *dhg, w_in4, x, g1, dh1)


def _dw(a, b, kb, nb_, name, tk=1024, side=1):
    T, K = a.shape
    N = b.shape[1]
    nk, nn, nt = K // kb, N // (nb_ * side), T // tk

    def body(a_ref, b_ref, o_ref, acc):
        t = pl.program_id(2)

        @pl.when(t == 0)
        def _():
            acc[...] = jnp.zeros_like(acc)

        acc[...] += _dot_tn(a_ref[...], b_ref[...].astype(BF16))

        @pl.when(t == nt - 1)
        def _():
            for s in range(side):
                o_ref[s] = acc[:, s * nb_:(s + 1) * nb_].astype(BF16)

    return pl.pallas_call(
        body, name=name, grid=(nk, nn, nt),
        in_specs=[pl.BlockSpec((tk, kb), lambda i, j, t: (t, i)),
                  pl.BlockSpec((tk, nb_ * side), lambda i, j, t: (t, j))],
        out_specs=pl.BlockSpec((side, kb, nb_), lambda i, j, t: (i * nn + j, 0, 0)),
        out_shape=jax.ShapeDtypeStruct((nk * nn * side, kb, nb_), BF16),
        scratch_shapes=[pltpu.VMEM((kb, nb_ * side), F32)],
        compiler_params=_cp("arbitrary", "arbitrary", "arbitrary"),
    )(a, b)


def _local_step(x, tgt, g1, w_in4, g_a, g_h, lb, w_out, g2, w_up4, conv_w, conv_b, w_down, gf):
    a = _step_mixers(x, g1, w_in4, lb)
    b = _step_channel(a, x, tgt, g_a, g_h, w_out, g2, w_up4, conv_w, conv_b, w_down, gf)
    c = _step_mixers_bwd(a, b, x, g1, w_in4, lb)
    small = dict(g1=c["dg1"], g_a=b["dga"], g_h=b["dgh"], lb=c["dlb"], g2=b["dg2"], conv_w=b["dcw"], conv_b=b["dcb"],
                 gf=b["dgf"])
    return b["loss"], c["dx"], small, dict(w_in=c["dw_in"], w_out=b["dw_out"], w_up=b["dw_up"], w_down=b["dw_down"])


def _step_mixers(x, g1, w_in4, lb):
    u1, qkv, hg = _in_proj(x, g1, w_in4)
    attn_o, lse = _attn_fwd(qkv)
    rec_o, states = _hgrn_fwd(hg, lb)
    return dict(u1=u1, qkv=qkv, hg=hg, attn_o=attn_o, lse=lse, rec_o=rec_o, states=states)


def _step_channel(a, x, tgt, g_a, g_h, w_out, g2, w_up4, conv_w, conv_b, w_down, gf):
    h1, mixed = _mix_out(a["attn_o"], a["rec_o"], a["hg"], x, g_a, g_h, w_out)
    u2, gate, val, conv, act, dh2, loss, dgf = _mlp_fwd(h1, g2, w_up4, conv_w, conv_b, w_down, gf, tgt)
    dgv, dcw, dcb = _mlp_bwd(dh2, gate, val, conv, conv_w, w_down)
    dw_down = _dw(act, dh2, D_FF // 2, D_MODEL, "dw_down").reshape(N_CHIPS, D_FF // N_CHIPS, D_MODEL)
    dh1, dg2, da, dr, dgt, dga, dgh = _up_out_bwd(dgv, w_up4, h1, g2, dh2, w_out, a["attn_o"], a["rec_o"], a["hg"],
                                                  g_a, g_h)
    dw_up = _dw(u2, dgv, D_MODEL, UP_SHARD, "dw_up", side=2)
    dw_out = _dw(mixed, dh1, D_MODEL, D_MODEL, "dw_out").reshape(N_CHIPS, D_MODEL // N_CHIPS, D_MODEL)
    return dict(loss=loss, dgf=dgf, dcw=dcw, dcb=dcb, dg2=dg2, dga=dga, dgh=dgh, dh1=dh1, da=da, dr=dr, dgt=dgt,
                dw_down=dw_down, dw_up=dw_up, dw_out=dw_out)


def _step_mixers_bwd(a, b, x, g1, w_in4, lb, dqkv=None):
    if dqkv is None:
        dqkv = _attn_bwd(a["qkv"], a["attn_o"], a["lse"], b["da"])
    dhq, dhf, dhi, dlb = _hgrn_bwd(a["hg"], lb, a["states"], b["dr"])
    dproj, dx, dg1 = _in_bwd(dqkv, [dhq, dhf, dhi, b["dgt"]], w_in4, x, g1, b["dh1"])
    dw_in = _dw(a["u1"], dproj, D_MODEL, IN_SHARD, "dw_in", side=2)
    return dict(dx=dx, dg1=dg1, dlb=dlb, dw_in=dw_in)


BIG = ("w_in", "w_out", "w_up", "w_down")
ANY = pl.BlockSpec(memory_space=pl.ANY)


def _place():
    x, y, c = lax.axis_index("x"), lax.axis_index("y"), lax.axis_index("c")
    chips = [(1 - x, y), (x, 1 - y), (1 - x, 1 - y)]
    return x, y, c, chips


def _remote(src, dst, send_sems, recv_sems, k, to):
    return pltpu.make_async_remote_copy(src_ref=src, dst_ref=dst, send_sem=send_sems.at[k], recv_sem=recv_sems.at[k],
                                        device_id=to, device_id_type=MESH)


def _gather_weights(shards, conv_w):
    n = len(shards)
    halves = [s.shape[0] // 2 for s in shards]

    def body(*refs):
        ins, cw, outs, ocw = refs[:n], refs[n], refs[n + 1:2 * n + 1], refs[2 * n + 1]
        send_sems, recv_sems = refs[2 * n + 2:]
        x, y, c, chips = _place()
        me, sibling = 2 * x + y, (x, y, 1 - c)

        def part(w, chip, half):
            return outs[w].at[chip, pl.ds(half * halves[w], halves[w]), :]

        sent = []
        for j, chip in enumerate(chips):
            for w in range(n):
                sent.append(_remote(ins[w].at[pl.ds(c * halves[w], halves[w]), :], part(w, me, c),
                                    send_sems, recv_sems, w * 3 + j, (*chip, c)))
            sent.append(_remote(cw, ocw.at[me], send_sems, recv_sems, 6 * n + j, (*chip, c)))
        for cp in sent:
            cp.start()
        for j, chip in enumerate(chips):
            kj = 2 * chip[0] + chip[1]
            for w in range(n):
                _remote(part(w, kj, c), part(w, kj, c), send_sems, recv_sems, w * 3 + j, (*chip, c)).wait_recv()
                fwd = _remote(part(w, kj, c), part(w, kj, c), send_sems, recv_sems, 3 * n + w * 3 + j, sibling)
                fwd.start()
                sent.append(fwd)
        for j, chip in enumerate(chips):
            kj = 2 * chip[0] + chip[1]
            for w in range(n):
                _remote(part(w, kj, 1 - c), part(w, kj, 1 - c), send_sems, recv_sems, 3 * n + w * 3 + j,
                        sibling).wait_recv()
            _remote(cw, ocw.at[kj], send_sems, recv_sems, 6 * n + j, (*chip, c)).wait_recv()
        for cp in sent:
            cp.wait_send()

    n_sem = 6 * n + 3
    outs = pl.pallas_call(
        body, name="gather_weights",
        in_specs=[ANY] * (n + 1), out_specs=[ANY] * (n + 1),
        out_shape=[jax.ShapeDtypeStruct((N_CHIPS,) + s.shape, s.dtype) for s in shards]
        + [jax.ShapeDtypeStruct((N_CHIPS,) + conv_w.shape, conv_w.dtype)],
        scratch_shapes=[pltpu.SemaphoreType.DMA((n_sem,)), pltpu.SemaphoreType.DMA((n_sem,))],
    )(*shards, conv_w)
    chip = 2 * lax.axis_index("x") + lax.axis_index("y")
    return [lax.dynamic_update_slice(o, s[None], (chip,) + (0,) * s.ndim) for o, s in zip(outs, [*shards, conv_w])]


def _allreduce_small(buf):
    rows = buf.shape[0]

    def body(in_ref, out_ref, slots, send_sems, recv_sems):
        x, y, c, _ = _place()
        me = 4 * x + 2 * y + c
        slots[me] = in_ref[...]
        sent = []
        for p in range(1, 8):
            to = (x ^ (p >> 2), y ^ ((p >> 1) & 1), c ^ (p & 1))
            sent.append(_remote(in_ref, slots.at[me], send_sems, recv_sems, p, to))
        for cp in sent:
            cp.start()
        for p in range(1, 8):
            frm = 4 * (x ^ (p >> 2)) + 2 * (y ^ ((p >> 1) & 1)) + (c ^ (p & 1))
            _remote(in_ref, slots.at[frm], send_sems, recv_sems, p, (x, y, c)).wait_recv()
        for cp in sent:
            cp.wait_send()
        acc = slots[0]
        for d in range(1, 8):
            acc = acc + slots[d]
        out_ref[...] = acc

    vm = pl.BlockSpec(memory_space=pltpu.VMEM)
    return pl.pallas_call(
        body, name="allreduce_small", in_specs=[vm], out_specs=vm,
        out_shape=jax.ShapeDtypeStruct(buf.shape, F32),
        scratch_shapes=[pltpu.VMEM((8, rows, 128), F32), pltpu.SemaphoreType.DMA((8,)), pltpu.SemaphoreType.DMA((8,))],
    )(buf)


def _pair_exchange(gs, name):
    n = len(gs)
    halves = [g.shape[1] // 2 for g in gs]

    def body(*refs):
        g, got = refs[:n], refs[n:2 * n]
        send_sems, recv_sems = refs[2 * n:]
        x, y, c, _ = _place()
        cps = [_remote(g[w].at[:, pl.ds((1 - c) * halves[w], halves[w]), :], got[w], send_sems, recv_sems, w,
                       (x, y, 1 - c)) for w in range(n)]
        for cp in cps:
            cp.start()
        for cp in cps:
            cp.wait()

    return pl.pallas_call(
        body, name=name, in_specs=[ANY] * n, out_specs=[ANY] * n,
        out_shape=[jax.ShapeDtypeStruct((N_CHIPS, h, g.shape[2]), g.dtype) for g, h in zip(gs, halves)],
        scratch_shapes=[pltpu.SemaphoreType.DMA((n,)), pltpu.SemaphoreType.DMA((n,))],
    )(*gs)


def _core_id():
    return lax.axis_index("c").reshape(1).astype(jnp.int32)


def _pair_sum(g, got, name):
    h, C = got.shape[1:]

    def body(c_ref, g_ref, b_ref, o_ref):
        o_ref[...] = (g_ref[...].astype(F32) + b_ref[...].astype(F32)).astype(BF16)

    blk = pl.BlockSpec((1, h, C), lambda k, c_ref: (k, 0, 0))
    return pl.pallas_call(
        body, name=name,
        grid_spec=pltpu.PrefetchScalarGridSpec(
            num_scalar_prefetch=1, grid=(N_CHIPS,),
            in_specs=[pl.BlockSpec((1, h, C), lambda k, c_ref: (k, c_ref[0], 0)), blk], out_specs=blk),
        out_shape=jax.ShapeDtypeStruct(got.shape, BF16), compiler_params=_cp("arbitrary"))(_core_id(), g, got)


def _sum_partials(g, got, landed, name):
    h, C = got.shape[1:]

    def body(ids, g_ref, b_ref, l_ref, o_ref):
        acc = g_ref[0].astype(F32) + b_ref[0].astype(F32)
        for j in range(3):
            acc = acc + l_ref[j].astype(F32)
        o_ref[...] = acc

    ids = jnp.stack([2 * lax.axis_index("x") + lax.axis_index("y"), lax.axis_index("c")]).astype(jnp.int32)
    return pl.pallas_call(
        body, name=name,
        grid_spec=pltpu.PrefetchScalarGridSpec(
            num_scalar_prefetch=1, grid=(1,),
            in_specs=[pl.BlockSpec((1, h, C), lambda i, ids: (ids[0], ids[1], 0)),
                      pl.BlockSpec((1, h, C), lambda i, ids: (ids[0], 0, 0)),
                      pl.BlockSpec((3, h, C), lambda i, ids: (0, 0, 0))],
            out_specs=pl.BlockSpec((h, C), lambda i, ids: (ids[1], 0))),
        out_shape=jax.ShapeDtypeStruct((2 * h, C), F32), compiler_params=_cp("arbitrary"))(ids, g, got, landed)


def _pair_share(reds, name):
    n = len(reds)

    def body(*refs):
        out = refs[n:2 * n]
        send_sems, recv_sems = refs[2 * n:]
        x, y, c, _ = _place()
        def half(w, which):
            h = out[w].shape[0] // 2
            return out[w].at[pl.ds(which * h, h), :]

        cps = [_remote(half(w, c), half(w, c), send_sems, recv_sems, w, (x, y, 1 - c)) for w in range(n)]
        for cp in cps:
            cp.start()
        for w in range(n):
            _remote(half(w, 1 - c), half(w, 1 - c), send_sems, recv_sems, w, (x, y, 1 - c)).wait_recv()
        for cp in cps:
            cp.wait_send()

    return pl.pallas_call(
        body, name=name, in_specs=[ANY] * n, out_specs=[ANY] * n,
        out_shape=[jax.ShapeDtypeStruct(r.shape, F32) for r in reds],
        input_output_aliases={w: w for w in range(n)},
        scratch_shapes=[pltpu.SemaphoreType.DMA((n,)), pltpu.SemaphoreType.DMA((n,))],
    )(*reds)


HBM = pl.BlockSpec(memory_space=pltpu.HBM)
SEM = pl.BlockSpec(memory_space=pltpu.SEMAPHORE)
DATAFLOW = pltpu.SideEffectType.DATAFLOW_SIDE_EFFECTING


def _copies_start(name, srcs, lands, plan, n_copies, after):
    ns, nb, na = len(srcs), len(srcs) + len(lands), len(after)

    def body(*refs):
        src_refs, land_refs = refs[:ns], refs[ns:nb]
        send_sems, recv_sems = refs[nb + na:nb + na + 2]
        token = refs[-1]
        for k, (src, there, _, to) in enumerate(plan(src_refs, land_refs)):
            _remote(src, there, send_sems, recv_sems, k, to).start()
        token[...] = jnp.zeros_like(token)

    hbm = lambda a: pltpu.HBM(a.shape, a.dtype)
    outs = pl.pallas_call(
        body, name=name,
        out_shape=(pltpu.SemaphoreType.DMA((n_copies,)), pltpu.SemaphoreType.DMA((n_copies,)),
                   *[hbm(a) for a in srcs], *[hbm(a) for a in lands], jax.ShapeDtypeStruct((8, 128), F32)),
        in_specs=[HBM] * nb + [ANY] * na,
        out_specs=(SEM, SEM, *[HBM] * nb, pl.BlockSpec(memory_space=pltpu.VMEM)),
        input_output_aliases={i: 2 + i for i in range(nb)},
        compiler_params=pltpu.CompilerParams(has_side_effects=DATAFLOW),
    )(*[pltpu.with_memory_space_constraint(a, pltpu.HBM) for a in (*srcs, *lands)], *after)
    return outs[0], outs[1], outs[2:2 + ns], outs[2 + ns:2 + nb], outs[-1]


def _copies_wait(name, send_sems, recv_sems, srcs, lands, plan, after):
    ns, nb, na = len(srcs), len(srcs) + len(lands), len(after)

    def body(*refs):
        src_refs, land_refs = refs[:ns], refs[ns:nb]
        send_sems, recv_sems = refs[nb:nb + 2]
        for k, (src, _, here, to) in enumerate(plan(src_refs, land_refs)):
            cp = _remote(src, here, send_sems, recv_sems, k, to)
            cp.wait_send()
            cp.wait_recv()

    hbm = lambda a: pltpu.HBM(a.shape, a.dtype)
    outs = pl.pallas_call(
        body, name=name,
        out_shape=(*[hbm(a) for a in srcs], *[hbm(a) for a in lands]),
        in_specs=[HBM] * nb + [SEM, SEM] + [ANY] * na,
        out_specs=tuple([HBM] * nb),
        input_output_aliases={i: i for i in range(nb)},
        compiler_params=pltpu.CompilerParams(has_side_effects=DATAFLOW),
    )(*srcs, *lands, send_sems, recv_sems, *after)
    return outs[:ns], outs[ns:]


def _gather_plan(halves):
    def plan(shards, lands):
        x, y, c, chips = _place()
        me = 2 * x + y
        copies = []
        for w, h in enumerate(halves):
            rows = pl.ds(c * h, h)
            for chip in chips:
                copies.append((shards[w].at[rows, :], lands[w].at[me, rows, :],
                               lands[w].at[2 * chip[0] + chip[1], rows, :], (*chip, c)))
        return copies
    return plan


def _reduce_plan(n):
    def plan(ps, lands):
        x, y, c, chips = _place()
        return [(ps[w].at[2 * chip[0] + chip[1]], lands[w].at[j], lands[w].at[j], (*chip, c))
                for w in range(n) for j, chip in enumerate(chips)]
    return plan


def _forward_plan(halves):
    def plan(_, lands):
        x, y, c, chips = _place()

        def part(w, chip, half):
            return lands[w].at[2 * chip[0] + chip[1], pl.ds(half * halves[w], halves[w]), :]

        return [(part(w, chip, c), part(w, chip, c), part(w, chip, 1 - c), (x, y, 1 - c))
                for w in range(len(halves)) for chip in chips]
    return plan


def _pair_plan(halves):
    def plan(gs, gots):
        x, y, c, _ = _place()
        return [(gs[w].at[:, pl.ds((1 - c) * h, h), :], gots[w], gots[w], (x, y, 1 - c)) for w, h in enumerate(halves)]
    return plan


def _place_own(gathered, shards):
    chip = 2 * lax.axis_index("x") + lax.axis_index("y")
    return [lax.dynamic_update_slice(o, s[None], (chip, 0, 0)) for o, s in zip(gathered, shards)]


def _adamw(w, g, m, v, name, tr=None):
    R, C = w.shape
    tr = tr or R // 4

    def body(w_ref, g_ref, m_ref, v_ref, d_ref, nm_ref, nv_ref):
        gv = g_ref[...]
        nm = ADAM_B1 * m_ref[...] + (1.0 - ADAM_B1) * gv
        nv = ADAM_B2 * v_ref[...] + (1.0 - ADAM_B2) * (gv * gv)
        m_hat = nm / (1.0 - ADAM_B1 ** ADAM_STEP)
        v_hat = nv / (1.0 - ADAM_B2 ** ADAM_STEP)
        d_ref[...] = -ADAM_LR * (m_hat / (jnp.sqrt(v_hat) + ADAM_EPS) + ADAM_WD * w_ref[...])
        nm_ref[...] = nm
        nv_ref[...] = nv

    blk = pl.BlockSpec((tr, C), lambda i: (i, 0))
    return pl.pallas_call(body, name=name, grid=(R // tr,), in_specs=[blk] * 4, out_specs=[blk] * 3,
                          out_shape=[jax.ShapeDtypeStruct((R, C), F32)] * 3, compiler_params=_cp("arbitrary"))(w, g, m, v)


SMALL = (("norm1_g", 1024), ("attn_norm_g", 512), ("hgrn_norm_g", 512), ("hgrn_lb_logits", 1024), ("norm2_g", 1024),
         ("conv_b", D_FF), ("final_norm_g", 1024), ("conv_w", 3 * D_FF))
SMALL_ROWS = 136


def _pack(parts, rows):
    flat = jnp.concatenate([p.reshape(-1).astype(F32) for p in parts])
    return jnp.pad(flat, (0, rows * 128 - flat.shape[0])).reshape(rows, 128)


def kernel(x, norm1_g, w_in, attn_norm_g, hgrn_norm_g, hgrn_lb_logits, w_out, norm2_g, w_up, conv_w, conv_b, w_down, final_norm_g, loss_target, m_norm1_g, m_w_in, m_attn_norm_g, m_hgrn_norm_g, m_hgrn_lb_logits, m_w_out, m_norm2_g, m_w_up, m_conv_w, m_conv_b, m_w_down, m_final_norm_g, v_norm1_g, v_w_in, v_attn_norm_g, v_hgrn_norm_g, v_hgrn_lb_logits, v_w_out, v_norm2_g, v_w_up, v_conv_w, v_conv_b, v_w_down, v_final_norm_g):
    w = dict(norm1_g=norm1_g, w_in=w_in, attn_norm_g=attn_norm_g, hgrn_norm_g=hgrn_norm_g,
             hgrn_lb_logits=hgrn_lb_logits, w_out=w_out, norm2_g=norm2_g, w_up=w_up, conv_w=conv_w, conv_b=conv_b,
             w_down=w_down, final_norm_g=final_norm_g)
    m = dict(norm1_g=m_norm1_g, w_in=m_w_in, attn_norm_g=m_attn_norm_g, hgrn_norm_g=m_hgrn_norm_g,
             hgrn_lb_logits=m_hgrn_lb_logits, w_out=m_w_out, norm2_g=m_norm2_g, w_up=m_w_up, conv_w=m_conv_w,
             conv_b=m_conv_b, w_down=m_w_down, final_norm_g=m_final_norm_g)
    v = dict(norm1_g=v_norm1_g, w_in=v_w_in, attn_norm_g=v_attn_norm_g, hgrn_norm_g=v_hgrn_norm_g,
             hgrn_lb_logits=v_hgrn_lb_logits, w_out=v_w_out, norm2_g=v_norm2_g, w_up=v_w_up, conv_w=v_conv_w,
             conv_b=v_conv_b, w_down=v_w_down, final_norm_g=v_final_norm_g)
    names = list(w)
    chip = 2 * lax.axis_index("x") + lax.axis_index("y")

    shards = {k: w[k][0].astype(BF16) for k in BIG}
    w_in4, conv_w4 = _gather_weights([shards["w_in"]], conv_w[0])
    conv_w_full = jnp.transpose(conv_w4, (1, 0, 2)).reshape(3, D_FF)
    lb = jax.nn.softmax(hgrn_lb_logits, axis=0)[0:1]
    late = [shards[k] for k in BIG[1:]]
    gather_plan = _gather_plan([s.shape[0] // 2 for s in late])
    started = _copies_start("gather_start", late, [lax.empty((N_CHIPS,) + s.shape, BF16) for s in late], gather_plan,
                            3 * len(late), after=(w_in4,))
    u1, qkv, hg = _in_proj(x[0], norm1_g + started[4][0:1, 0:1], w_in4)
    attn_o, lse = _attn_fwd(qkv)
    late, landed_w = _copies_wait("gather_wait", *started[:4], gather_plan, after=(attn_o,))
    forward_plan = _forward_plan([s.shape[0] // 2 for s in late])
    started = _copies_start("forward_start", [], landed_w, forward_plan, 3 * len(late), after=())
    rec_o, states = _hgrn_fwd(hg, lb + started[4][0:1, 0:1])
    a = dict(u1=u1, qkv=qkv, hg=hg, attn_o=attn_o, lse=lse, rec_o=rec_o, states=states)
    w_out4, w_up4, w_down4 = _place_own(
        _copies_wait("forward_wait", *started[:4], forward_plan, after=(rec_o,))[1], late)

    b = _step_channel(a, x[0], loss_target[0], attn_norm_g, hgrn_norm_g, w_out4.reshape(D_MODEL, D_MODEL), norm2_g,
                      w_up4, conv_w_full, conv_b, w_down4.reshape(D_FF, D_MODEL), final_norm_g.reshape(1, D_MODEL))

    early = [b["dw_out"], b["dw_up"], b["dw_down"]]
    pair_plan = _pair_plan([gk.shape[1] // 2 for gk in early])
    started = _copies_start("pair_start", early,
                            [lax.empty((N_CHIPS, gk.shape[1] // 2, gk.shape[2]), BF16) for gk in early], pair_plan,
                            len(early), after=())
    dqkv = _attn_bwd(qkv, attn_o, lse, b["da"], started[4])
    early, gots = _copies_wait("pair_wait", *started[:4], pair_plan, after=(dqkv[0],))
    ps = [_pair_sum(gk, got, f"pair_sum_{k}") for gk, got, k in zip(early, gots, BIG[1:])]
    reduce_plan = _reduce_plan(len(ps))
    started = _copies_start("reduce_start", ps, [lax.empty((3,) + p.shape[1:], BF16) for p in ps], reduce_plan,
                            3 * len(ps), after=())
    c = _step_mixers_bwd(a, b, x[0], norm1_g, w_in4, lb + started[4][0:1, 0:1], dqkv)
    gots_in = _pair_exchange([c["dw_in"]], "pair_exchange_w_in")
    ps_in = _pair_sum(c["dw_in"], gots_in[0], "pair_sum_w_in")
    plan_in = _reduce_plan(1)
    started_in = _copies_start("reduce_start_w_in", [ps_in], [lax.empty((3,) + ps_in.shape[1:], BF16)], plan_in, 3,
                               after=())
    landed = _copies_wait("reduce_wait", *started[:4], reduce_plan, after=(started_in[4],))[1]
    reds = [_sum_partials(gk, got, l, f"sum_partials_{k}") for gk, got, l, k in zip(early, gots, landed, BIG[1:])]
    g = dict(zip(BIG[1:], _pair_share(reds, "pair_share")))
    delta, new_m, new_v = {}, {}, {}
    for k in BIG[1:]:
        delta[k], new_m[k], new_v[k] = _adamw(w[k][0], g[k], m[k][0], v[k][0], f"adamw_{k}")

    loss, dx = b["loss"], c["dx"]
    small = dict(g1=c["dg1"], g_a=b["dga"], g_h=b["dgh"], lb=c["dlb"], g2=b["dg2"], conv_w=b["dcw"], conv_b=b["dcb"],
                 gf=b["dgf"])
    dlb = small["lb"] * lb * (1.0 - lb)
    grads_small = dict(norm1_g=small["g1"], attn_norm_g=small["g_a"], hgrn_norm_g=small["g_h"],
                       hgrn_lb_logits=jnp.concatenate([dlb, -dlb], axis=0), norm2_g=small["g2"],
                       conv_b=small["conv_b"], final_norm_g=small["gf"], conv_w=small["conv_w"])
    summed = _allreduce_small(_pack([grads_small[k] for k, _ in SMALL] + [loss[0, 0:1]], SMALL_ROWS)).reshape(-1)
    off = 0
    for k, size in SMALL:
        g[k] = summed[off:off + size]
        off += size
    loss_total = summed[off]
    g["conv_w"] = lax.dynamic_slice(g["conv_w"].reshape(3, D_FF), (0, chip * (D_FF // N_CHIPS)), (3, D_FF // N_CHIPS))
    small_names = [k for k in names if k not in BIG]
    rows = 80
    packed = _adamw(_pack([w[k] for k in small_names], rows), _pack([g[k] for k in small_names], rows),
                    _pack([m[k] for k in small_names], rows), _pack([v[k] for k in small_names], rows), "adamw_small", tr=rows)
    flat = [a.reshape(-1) for a in packed]
    off = 0
    for k in small_names:
        size = w[k].size
        delta[k], new_m[k], new_v[k] = (a[off:off + size].reshape(w[k].shape) for a in flat)
        g[k] = g[k].reshape(w[k].shape)
        off += size

    landed_in = _copies_wait("reduce_wait_w_in", *started_in[:4], plan_in, after=(packed[0], delta["w_up"]))[1]
    red_in = _sum_partials(c["dw_in"], gots_in[0], landed_in[0], "sum_partials_w_in")
    g["w_in"] = _pair_share([red_in], "pair_share_w_in")[0]
    delta["w_in"], new_m["w_in"], new_v["w_in"] = _adamw(w_in[0], g["w_in"], m_w_in[0], v_w_in[0], "adamw_w_in")
    for k in BIG:
        g[k], delta[k], new_m[k], new_v[k] = g[k][None], delta[k][None], new_m[k][None], new_v[k][None]

    return (loss_total, dx[None], *[g[k] for k in names], *[delta[k] for k in names],
            *[new_m[k] for k in names], *[new_v[k] for k in names])
```

```python
import functools
import math

import jax
import jax.numpy as jnp
from jax import lax
from jax.experimental import pallas as pl
from jax.experimental.pallas import tpu as pltpu

F32 = jnp.float32
BF16 = jnp.bfloat16

D_MODEL = 1024
ATTN_W = 512
HGRN_W = 512
HEAD_PAIR = 128
ATTN_BLK = 128
DILATIONS = (1, 4, 16)
ATTN_CHAINS = 4
ATTN_CHAINS_FWD = 8
HGRN_HEADS = 4
HGRN_DIM = 128
HGRN_CHUNK = 64
SUPER = 256
HGRN_SIDE = 2
D_FF = 2816
FF_CHUNKS = ((0, 1536), (1536, D_FF))
N_CHIPS = 4
IN_TOTAL = 3584
IN_SHARD = IN_TOTAL // N_CHIPS
UP_SHARD = 2 * D_FF // N_CHIPS
QKV_W = 3 * ATTN_W
HG_W = 4 * HGRN_W
EPS = 1e-6
NEG = -1e30
V7X_VMEM_BYTES = 64 * 1024 * 1024
VMEM_LIMIT = V7X_VMEM_BYTES - 8 * 1024 * 1024

ADAM_LR = 0.001
ADAM_B1 = 0.9
ADAM_B2 = 0.999
ADAM_EPS = 1e-08
ADAM_WD = 0.01
ADAM_STEP = 10

MESH = pl.DeviceIdType.MESH


def _cp(*sem):
    return pltpu.CompilerParams(dimension_semantics=sem or None, vmem_limit_bytes=VMEM_LIMIT)


def _dot(a, b):
    return jnp.dot(a, b, preferred_element_type=F32)


def _dot_nt(a, b):
    return lax.dot_general(a, b, (((1,), (1,)), ((), ())), preferred_element_type=F32)


def _dot_tn(a, b):
    return lax.dot_general(a, b, (((0,), (0,)), ((), ())), preferred_element_type=F32)


def _sigmoid(x):
    return 1.0 / (1.0 + jnp.exp(-x))


def _rms(x, width):
    return lax.rsqrt(jnp.sum(x * x, axis=-1, keepdims=True) * (1.0 / width) + EPS)


def _rms_bwd(dn, n, r, width):
    return r * (dn - n * (jnp.sum(dn * n, axis=-1, keepdims=True) * (1.0 / width)))


def _colsum(x):
    return jnp.sum(x, axis=0, keepdims=True)


def _row(v, k):
    rid = lax.broadcasted_iota(jnp.int32, v.shape, 0)
    return jnp.sum(jnp.where(rid == k, v, 0.0), axis=0, keepdims=True)


def _full(shape):
    return pl.BlockSpec(shape, lambda *_: (0,) * len(shape))


def _once(shape):
    return pl.BlockSpec(shape, lambda *_: (0,) * len(shape), pipeline_mode=pl.Buffered(1))


def _load_side_by_side(w_hbm, w_full, sem):
    width = w_hbm.shape[2]
    cps = [pltpu.make_async_copy(w_hbm.at[k], w_full.at[:, pl.ds(k * width, width)], sem.at[k]) for k in range(N_CHIPS)]
    for cp in cps:
        cp.start()
    for cp in cps:
        cp.wait()


def _in_proj(x, g1, w_in4, tm=512):
    T = x.shape[0]

    def body(x_ref, g_ref, w_hbm, u_ref, qkv_ref, hg_ref, w_full, sem):
        @pl.when(pl.program_id(0) == 0)
        def _():
            _load_side_by_side(w_hbm, w_full, sem)

        xv = x_ref[...]
        u = (xv * _rms(xv, D_MODEL) * g_ref[...]).astype(BF16)
        u_ref[...] = u
        p = _dot(u, w_full[...])
        qkv_ref[...] = p[:, :QKV_W]
        hg_ref[...] = p[:, QKV_W:]

    return pl.pallas_call(
        body, name="in_proj", grid=(T // tm,),
        in_specs=[pl.BlockSpec((tm, D_MODEL), lambda i: (i, 0)), _full((1, D_MODEL)), ANY],
        out_specs=[pl.BlockSpec((tm, D_MODEL), lambda i: (i, 0)), pl.BlockSpec((tm, QKV_W), lambda i: (i, 0)),
                   pl.BlockSpec((tm, HG_W), lambda i: (i, 0))],
        out_shape=[jax.ShapeDtypeStruct((T, D_MODEL), BF16), jax.ShapeDtypeStruct((T, QKV_W), F32),
                   jax.ShapeDtypeStruct((T, HG_W), F32)],
        scratch_shapes=[pltpu.VMEM((D_MODEL, IN_TOTAL), BF16), pltpu.SemaphoreType.DMA((N_CHIPS,))],
        compiler_params=_cp("arbitrary"),
    )(x, g1, w_in4)


def _attn_masks(bias_ref):
    lane = lax.broadcasted_iota(jnp.int32, (ATTN_BLK, HEAD_PAIR), 1)
    row = lax.broadcasted_iota(jnp.int32, (2 * ATTN_BLK, 2 * ATTN_BLK), 0)
    col = lax.broadcasted_iota(jnp.int32, (2 * ATTN_BLK, 2 * ATTN_BLK), 1)
    base = jnp.where(row >= ATTN_BLK, row - ATTN_BLK, row) - col
    for k in range(2):
        dist = base + k * ATTN_BLK
        bias_ref[k] = jnp.where((dist >= 0) & (dist <= ATTN_BLK), 0.0, NEG)
    bias_ref[2] = jnp.where(col >= ATTN_BLK, bias_ref[1], NEG)
    return lane < 64


def _two_heads(blk, first):
    zero = jnp.zeros_like(blk)
    return jnp.concatenate([jnp.where(first, blk, zero), jnp.where(first, zero, blk)], axis=0)


def _attn_rows(idx, nb, d):
    r, n = idx // nb, idx % nb
    kb = jnp.maximum(n - 1, 0)
    if d == 1:
        q0 = pl.multiple_of(n * ATTN_BLK, ATTN_BLK)
        k0 = pl.multiple_of(kb * ATTN_BLK, ATTN_BLK)
        return pl.ds(q0, ATTN_BLK), pl.ds(k0, 2 * ATTN_BLK), n - kb
    return (pl.ds(r + d * ATTN_BLK * n, ATTN_BLK, stride=d), pl.ds(r + d * ATTN_BLK * kb, 2 * ATTN_BLK, stride=d),
            n - kb)


def _attn_fwd(qkv):
    T = qkv.shape[0]

    per_chain = T // ATTN_BLK // ATTN_CHAINS_FWD

    def body(q_ref, k_ref, v_ref, o_ref, m_ref, l_ref, bias_ref):
        first = _attn_masks(bias_ref)
        for bi, d in enumerate(DILATIONS):
            nb = T // d // ATTN_BLK

            carried = d > 1 and per_chain % nb == 0

            def block(idx, kept=None, d=d, nb=nb, bi=bi, carried=carried):
                rows, keys, which = _attn_rows(idx, nb, d)
                q2 = _two_heads(q_ref[rows, :] * 0.125, first).astype(BF16)
                if carried:
                    k_own, v_own = k_ref[rows, :].astype(BF16), v_ref[rows, :].astype(BF16)
                    kw = jnp.concatenate([kept[0], k_own], axis=0)
                    vw = jnp.concatenate([kept[1], v_own], axis=0)
                    which = 2 - which
                else:
                    kw = k_ref[keys, :].astype(BF16)
                    vw = v_ref[keys, :].astype(BF16)
                old = (o_ref[rows, :], m_ref[rows, :], l_ref[rows, :]) if bi else None
                s = _dot_nt(q2, kw) + bias_ref[which]
                mb = jnp.max(s, axis=-1, keepdims=True)
                p = jnp.exp(s - mb)
                lb = jnp.sum(p, axis=-1, keepdims=True)
                o2 = _dot(p.astype(BF16), vw)
                o = jnp.where(first, o2[:ATTN_BLK], o2[ATTN_BLK:])
                m = jnp.where(first, mb[:ATTN_BLK], mb[ATTN_BLK:])
                l = jnp.where(first, lb[:ATTN_BLK], lb[ATTN_BLK:])
                if bi:
                    po, pm, pl_ = old
                    mn = jnp.maximum(pm, m)
                    wa = jnp.exp(pm - mn)
                    wb = jnp.exp(m - mn)
                    o, l, m = po * wa + o * wb, pl_ * wa + l * wb, mn
                return (rows, o, m, l), ((k_own, v_own) if carried else 0)

            def step(i, kept, block=block, carried=carried):
                done = [block(i + ch * per_chain, kept[ch] if carried else None) for ch in range(ATTN_CHAINS_FWD)]
                for (rows, o, m, l), _ in done:
                    o_ref[rows, :] = o
                    m_ref[rows, :] = m
                    l_ref[rows, :] = l
                return tuple(k for _, k in done) if carried else kept

            zero = jnp.zeros((ATTN_BLK, HEAD_PAIR), BF16)
            lax.fori_loop(0, per_chain, step, ((zero, zero),) * ATTN_CHAINS_FWD if carried else 0)

        def finish(i, carry):
            rows = pl.ds(pl.multiple_of(i * SUPER, SUPER), SUPER)
            l = l_ref[rows, :]
            o_ref[rows, :] = o_ref[rows, :] / l
            m_ref[rows, :] = m_ref[rows, :] + jnp.log(l)
            return carry

        lax.fori_loop(0, T // SUPER, finish, 0)

    col = lambda off: pl.BlockSpec((T, HEAD_PAIR), lambda j: (0, off + j))
    return pl.pallas_call(
        body, name="attn_fwd", grid=(4,),
        in_specs=[col(0), col(4), col(8)], out_specs=[col(0), col(0)],
        out_shape=[jax.ShapeDtypeStruct((T, ATTN_W), F32)] * 2,
        scratch_shapes=[pltpu.VMEM((T, HEAD_PAIR), F32), pltpu.VMEM((3, 2 * ATTN_BLK, 2 * ATTN_BLK), F32)],
        compiler_params=_cp("arbitrary"),
    )(qkv, qkv, qkv)


def _attn_bwd(qkv, o, lse, do, token=None):
    T = qkv.shape[0]
    per_chain = T // ATTN_BLK // ATTN_CHAINS
    extra = [] if token is None else [token]

    def body(q_ref, k_ref, v_ref, o_ref, lse_ref, do_ref, *rest):
        outs = rest[len(extra):len(extra) + 3]
        dq_ref, dk_ref, dv_ref, dkb_ref, dvb_ref, bias_ref = rest[len(extra) + 3:]
        first = _attn_masks(bias_ref)
        dq_ref[...] = jnp.zeros_like(dq_ref)
        dk_ref[...] = jnp.zeros_like(dk_ref)
        dv_ref[...] = jnp.zeros_like(dv_ref)

        def grads(rows, kw, vw, which):
            q2 = _two_heads(q_ref[rows, :] * 0.125, first).astype(BF16)
            lse_b = lse_ref[rows, :]
            dob = do_ref[rows, :]
            prod = dob * o_ref[rows, :]
            old = dq_ref[rows, :]
            lse2 = jnp.concatenate(
                [jnp.max(jnp.where(first, lse_b, NEG), axis=-1, keepdims=True),
                 jnp.max(jnp.where(first, NEG, lse_b), axis=-1, keepdims=True)], axis=0)
            p = jnp.exp(_dot_nt(q2, kw) + (bias_ref[which] - lse2))
            delta = jnp.concatenate(
                [jnp.sum(jnp.where(first, prod, 0.0), axis=-1, keepdims=True),
                 jnp.sum(jnp.where(first, 0.0, prod), axis=-1, keepdims=True)], axis=0)
            do2 = _two_heads(dob, first).astype(BF16)
            ds = (p * (_dot_nt(do2, vw) - delta)).astype(BF16)
            dq2 = _dot(ds, kw) * 0.125
            return (old + jnp.where(first, dq2[:ATTN_BLK], dq2[ATTN_BLK:]), _dot_tn(ds, q2),
                    _dot_tn(p.astype(BF16), do2))

        def block(idx):
            rows, keys, which = _attn_rows(idx, T // ATTN_BLK, 1)
            old = dk_ref[keys, :], dv_ref[keys, :]
            dq, ck, cv = grads(rows, k_ref[keys, :].astype(BF16), v_ref[keys, :].astype(BF16), which)
            return rows, keys, dq, old[0] + ck, old[1] + cv

        def step(i, carry):
            done = [block(i + ch * per_chain) for ch in range(ATTN_CHAINS)]
            for rows, keys, dq, dk, dv in done:
                dq_ref[rows, :] = dq
                dk_ref[keys, :] = dk
                dv_ref[keys, :] = dv
            return carry

        lax.fori_loop(0, per_chain, step, 0)

        for d in DILATIONS[1:]:
            nb = T // d // ATTN_BLK

            def block(idx, kept, d=d, nb=nb):
                r, n = idx // nb, idx % nb
                rows = pl.ds(r + d * ATTN_BLK * n, ATTN_BLK, stride=d)
                before = pl.ds(r + d * ATTN_BLK * jnp.maximum(n - 1, 0), ATTN_BLK, stride=d)
                k_prev, v_prev, dk_prev, dv_prev = kept
                k_own, v_own = k_ref[rows, :].astype(BF16), v_ref[rows, :].astype(BF16)
                dq, ck, cv = grads(rows, jnp.concatenate([k_prev, k_own], axis=0),
                                   jnp.concatenate([v_prev, v_own], axis=0), jnp.where(n > 0, 1, 2))
                stores = (rows, before, dq, dk_prev + ck[:ATTN_BLK], dv_prev + cv[:ATTN_BLK], ck[ATTN_BLK:], cv[ATTN_BLK:])
                return stores, (k_own, v_own, ck[ATTN_BLK:], cv[ATTN_BLK:])

            def step(i, kept, block=block):
                done = [block(i + ch * per_chain, kept[ch]) for ch in range(ATTN_CHAINS)]
                for (rows, before, dq, dk_done, dv_done, dk_own, dv_own), _ in done:
                    dq_ref[rows, :] = dq
                    dkb_ref[before, :] = dk_done
                    dvb_ref[before, :] = dv_done
                    dkb_ref[rows, :] = dk_own
                    dvb_ref[rows, :] = dv_own
                return tuple(k for _, k in done)

            zero = jnp.zeros((ATTN_BLK, HEAD_PAIR), F32)
            lax.fori_loop(0, per_chain, step, ((zero.astype(BF16), zero.astype(BF16), zero, zero),) * ATTN_CHAINS)

            def add(i, carry):
                rows = pl.ds(pl.multiple_of(i * SUPER, SUPER), SUPER)
                dk_ref[rows, :] += dkb_ref[rows, :]
                dv_ref[rows, :] += dvb_ref[rows, :]
                return carry

            lax.fori_loop(0, T // SUPER, add, 0)

        def emit(i, carry):
            rows = pl.ds(pl.multiple_of(i * SUPER, SUPER), SUPER)
            for out, acc in zip(outs, (dq_ref, dk_ref, dv_ref)):
                out[rows, :] = acc[rows, :].astype(BF16)
            return carry

        lax.fori_loop(0, T // SUPER, emit, 0)

    col = lambda off: pl.BlockSpec((T, HEAD_PAIR), lambda j: (0, off + j))
    return pl.pallas_call(
        body, name="attn_bwd", grid=(4,),
        in_specs=[col(0), col(4), col(8), col(0), col(0), col(0)] + [_full(t.shape) for t in extra],
        out_specs=[col(0)] * 3,
        out_shape=[jax.ShapeDtypeStruct((T, ATTN_W), BF16)] * 3,
        scratch_shapes=[pltpu.VMEM((T, HEAD_PAIR), F32)] * 5 + [pltpu.VMEM((3, 2 * ATTN_BLK, 2 * ATTN_BLK), F32)],
        compiler_params=_cp("arbitrary"),
    )(qkv, qkv, qkv, o, lse, do, *extra)


def _chunk_ids():
    row = lax.broadcasted_iota(jnp.int32, (SUPER, HGRN_DIM), 0)
    r2 = lax.broadcasted_iota(jnp.int32, (SUPER, SUPER), 0)
    c2 = lax.broadcasted_iota(jnp.int32, (SUPER, SUPER), 1)
    amask = ((r2 // HGRN_CHUNK) == (c2 // HGRN_CHUNK)) & (c2 <= r2)
    return row % HGRN_CHUNK, row // HGRN_CHUNK, amask


def _cumsum_chunk(x, rmod):
    s = 1
    while s < HGRN_CHUNK:
        x = x + jnp.where(rmod >= s, pltpu.roll(x, s, 0), 0.0)
        s *= 2
    return x


def _suffix_sum_chunk(x, rmod):
    s = 1
    while s < HGRN_CHUNK:
        x = x + jnp.where(rmod < HGRN_CHUNK - s, pltpu.roll(x, SUPER - s, 0), 0.0)
        s *= 2
    return x


def _chunk_rows(vs, cid):
    out = vs[-1]
    for c in reversed(range(len(vs) - 1)):
        out = jnp.where(cid == c, vs[c], out)
    return out


def _expand(x, cid):
    return jnp.concatenate([jnp.where(cid == c, x, 0.0) for c in range(SUPER // HGRN_CHUNK)], axis=1)


def _hgrn_gates(q, f, lbv, rmod, cid, tmp):
    sq = _sigmoid(q)
    sg = _sigmoid(f)
    forget = lbv + (1.0 - lbv) * sg
    key = 1.0 - forget
    b = _cumsum_chunk(jnp.log(forget), rmod)
    tmp[...] = b
    bends = [tmp[c * HGRN_CHUNK + HGRN_CHUNK - 1:(c + 1) * HGRN_CHUNK, :] for c in range(SUPER // HGRN_CHUNK)]
    eb = jnp.exp(b)
    enb = jnp.exp(-b)
    ebe = jnp.exp(_chunk_rows(bends, cid) - b)
    return sq, sg, forget, key, eb, enb, ebe, q * sq * eb, key * enb, key * ebe, [jnp.exp(v) for v in bends]


def _hgrn_fwd(hg, lb):
    T = hg.shape[0]
    nsc = T // SUPER
    NC = SUPER // HGRN_CHUNK

    def body(q_ref, f_ref, i_ref, lb_ref, o_ref, st_ref, state, tmp):
        rmod, cid, amask = _chunk_ids()
        state[...] = jnp.zeros_like(state)
        lbv = lb_ref[...]

        def local(sc, u):
            rows = pl.ds(pl.multiple_of(sc * SUPER, SUPER), SUPER)
            iv = i_ref[rows, :].astype(BF16)
            qd, ki, ke, dec = _hgrn_gates(q_ref[rows, :], f_ref[rows, :], lbv, rmod, cid, tmp.at[u])[-4:]
            a = jnp.where(amask, _dot_nt(qd.astype(BF16), ki.astype(BF16)), 0.0)
            return rows, qd, dec, _dot(a.astype(BF16), iv), _dot_tn(iv, _expand(ke, cid).astype(BF16))

        def step(i, carry):
            parts = [local(i * HGRN_SIDE + u, u) for u in range(HGRN_SIDE)]
            st = state[...]
            entering = []
            for u, (_, _, dec, _, ut) in enumerate(parts):
                st_ref[0, i * HGRN_SIDE + u] = st
                sts = []
                for c in range(NC):
                    sts.append(st)
                    st = st * dec[c] + ut[:, c * HGRN_DIM:(c + 1) * HGRN_DIM]
                entering.append(jnp.concatenate(sts, axis=1).astype(BF16))
            state[...] = st
            for (rows, qd, _, o, _), sts in zip(parts, entering):
                o_ref[rows, :] = o + _dot_nt(_expand(qd, cid).astype(BF16), sts)
            return carry

        lax.fori_loop(0, nsc // HGRN_SIDE, step, 0)

    col = lambda off: pl.BlockSpec((T, HGRN_DIM), lambda h: (0, off + h))
    return pl.pallas_call(
        body, name="hgrn_fwd", grid=(HGRN_HEADS,),
        in_specs=[col(0), col(4), col(8), pl.BlockSpec((1, HGRN_DIM), lambda h: (0, h))],
        out_specs=[pl.BlockSpec((T, HGRN_DIM), lambda h: (0, h)),
                   pl.BlockSpec((1, nsc, HGRN_DIM, HGRN_DIM), lambda h: (h, 0, 0, 0))],
        out_shape=[jax.ShapeDtypeStruct((T, HGRN_W), F32),
                   jax.ShapeDtypeStruct((HGRN_HEADS, nsc, HGRN_DIM, HGRN_DIM), F32)],
        scratch_shapes=[pltpu.VMEM((HGRN_DIM, HGRN_DIM), F32), pltpu.VMEM((HGRN_SIDE, SUPER, HGRN_DIM), F32)],
        compiler_params=_cp("arbitrary"),
    )(hg, hg, hg, lb)


def _hgrn_bwd(hg, lb, states, do):
    T = hg.shape[0]
    nsc = T // SUPER
    NC = SUPER // HGRN_CHUNK

    def body(q_ref, f_ref, i_ref, lb_ref, st_ref, do_ref, dq_ref, df_ref, di_ref, dlb_ref, dstate, tmp):
        rmod, cid, amask = _chunk_ids()
        dstate[...] = jnp.zeros_like(dstate)
        dlb_ref[...] = jnp.zeros_like(dlb_ref)
        lbv = lb_ref[...]

        def local(sc, u):
            rows = pl.ds(pl.multiple_of(sc * SUPER, SUPER), SUPER)
            q = q_ref[rows, :]
            ivf = i_ref[rows, :]
            iv = ivf.astype(BF16)
            dof = do_ref[rows, :]
            dob = dof.astype(BF16)
            sq, sg, forget, key, eb, enb, ebe, qd, ki, ke, dec = _hgrn_gates(q, f_ref[rows, :], lbv, rmod, cid,
                                                                            tmp.at[u])
            qdb, kib = qd.astype(BF16), ki.astype(BF16)
            keexp = _expand(ke, cid).astype(BF16)
            a = jnp.where(amask, _dot_nt(qdb, kib), 0.0).astype(BF16)
            ut = _dot_tn(iv, keexp)
            st = st_ref[0, sc]
            sts = []
            for c in range(NC):
                sts.append(st)
                st = st * dec[c] + ut[:, c * HGRN_DIM:(c + 1) * HGRN_DIM]
            gt = _dot_tn(dob, _expand(qd, cid).astype(BF16))
            da = jnp.where(amask, _dot_nt(dob, iv), 0.0).astype(BF16)
            ststack = jnp.concatenate(sts, axis=0).astype(BF16)
            return dict(rows=rows, q=q, sq=sq, sg=sg, forget=forget, eb=eb, enb=enb, ebe=ebe, qd=qd, ki=ki, ke=ke,
                        dec=dec, sts=sts, gt=gt, keexp=keexp, ivexp=_expand(ivf, cid).astype(BF16),
                        div=_dot_tn(a, dob), dki=_dot_tn(da, qdb),
                        dqd=_dot(da, kib) + _dot(_expand(dof, cid).astype(BF16), ststack))

        def finish(p, nxt, ddec):
            ncat = jnp.concatenate(nxt, axis=1).astype(BF16)
            nstack = jnp.concatenate(nxt, axis=0).astype(BF16)
            dke = _dot(p["ivexp"], nstack)
            dkk = dke * p["ke"]
            dkey = p["dki"] * p["enb"] + dke * p["ebe"]
            db = p["dqd"] * p["qd"] - p["dki"] * p["ki"] - dkk
            dbends = [_colsum(jnp.where(cid == c, dkk, 0.0)) + ddec[c] * p["dec"][c] for c in range(NC)]
            dforget = (_suffix_sum_chunk(db, rmod) + _chunk_rows(dbends, cid)) / p["forget"] - dkey
            sg, sq, q = p["sg"], p["sq"], p["q"]
            df_ref[p["rows"], :] = (dforget * (1.0 - lbv) * sg * (1.0 - sg)).astype(BF16)
            dq_ref[p["rows"], :] = (p["dqd"] * p["eb"] * (sq * (1.0 + q * (1.0 - sq)))).astype(BF16)
            di_ref[p["rows"], :] = (p["div"] + _dot_nt(p["keexp"], ncat)).astype(BF16)
            return _colsum(dforget * (1.0 - sg))

        def step(i, carry):
            parts = [local(nsc - 1 - (i * HGRN_SIDE + u), u) for u in range(HGRN_SIDE)]
            dst = dstate[...]
            chained = []
            for p in parts:
                nxt = [None] * NC
                ddec = [None] * NC
                for c in reversed(range(NC)):
                    nxt[c] = dst
                    ddec[c] = _colsum(dst * p["sts"][c])
                    dst = dst * p["dec"][c] + p["gt"][:, c * HGRN_DIM:(c + 1) * HGRN_DIM]
                chained.append((nxt, ddec))
            dstate[...] = dst
            dlb = dlb_ref[...]
            for p, (nxt, ddec) in zip(parts, chained):
                dlb = dlb + finish(p, nxt, ddec)
            dlb_ref[...] = dlb
            return carry

        lax.fori_loop(0, nsc // HGRN_SIDE, step, 0)

    col = lambda off: pl.BlockSpec((T, HGRN_DIM), lambda h: (0, off + h))
    own = pl.BlockSpec((T, HGRN_DIM), lambda h: (0, h))
    vec = pl.BlockSpec((1, HGRN_DIM), lambda h: (0, h))
    return pl.pallas_call(
        body, name="hgrn_bwd", grid=(HGRN_HEADS,),
        in_specs=[col(0), col(4), col(8), vec,
                  pl.BlockSpec((1, nsc, HGRN_DIM, HGRN_DIM), lambda h: (h, 0, 0, 0)), own],
        out_specs=[own, own, own, vec],
        out_shape=[jax.ShapeDtypeStruct((T, HGRN_W), BF16)] * 3 + [jax.ShapeDtypeStruct((1, HGRN_W), F32)],
        scratch_shapes=[pltpu.VMEM((HGRN_DIM, HGRN_DIM), F32), pltpu.VMEM((HGRN_SIDE, SUPER, HGRN_DIM), F32)],
        compiler_params=_cp("arbitrary"),
    )(hg, hg, hg, lb, states, do)


def _rec_heads(rec, gate, g_h):
    rr = jnp.concatenate(
        [jnp.broadcast_to(_rms(rec[:, h * HGRN_DIM:(h + 1) * HGRN_DIM], HGRN_DIM), (rec.shape[0], HGRN_DIM))
         for h in range(HGRN_HEADS)], axis=1)
    rn = rec * rr
    sg = _sigmoid(gate)
    return rr, rn, sg


def _mix_out(attn_o, rec_o, hg, x, g_a, g_h, w_out, tm=512):
    T = x.shape[0]

    def body(a_ref, r_ref, gt_ref, x_ref, ga_ref, gh_ref, w_ref, h1_ref, mixed_ref):
        a = a_ref[...]
        an = a * _rms(a, ATTN_W) * ga_ref[...]
        gate = gt_ref[...]
        _, rn, sg = _rec_heads(r_ref[...], gate, gh_ref[...])
        mixed = jnp.concatenate([an, rn * gh_ref[...] * (gate * sg)], axis=1).astype(BF16)
        mixed_ref[...] = mixed
        h1_ref[...] = x_ref[...] + _dot(mixed, w_ref[...])

    row = lambda w: pl.BlockSpec((tm, w), lambda i: (i, 0))
    return pl.pallas_call(
        body, name="mix_out", grid=(T // tm,),
        in_specs=[row(ATTN_W), row(HGRN_W), pl.BlockSpec((tm, HGRN_W), lambda i: (i, 3)), row(D_MODEL),
                  _full((1, ATTN_W)), _full((1, HGRN_W)), _once((D_MODEL, D_MODEL))],
        out_specs=[row(D_MODEL), row(D_MODEL)],
        out_shape=[jax.ShapeDtypeStruct((T, D_MODEL), F32), jax.ShapeDtypeStruct((T, D_MODEL), BF16)],
        compiler_params=_cp("arbitrary"),
    )(attn_o, rec_o, hg, x, g_a, g_h, w_out)


_INV_SQRT2 = 1.0 / math.sqrt(2.0)
_INV_SQRT2PI = 1.0 / math.sqrt(2.0 * math.pi)


def _gelu(x):
    return 0.5 * x * (1.0 + lax.erf(x * _INV_SQRT2))


def _gelu_grad(x):
    return 0.5 * (1.0 + lax.erf(x * _INV_SQRT2)) + x * jnp.exp(-0.5 * x * x) * _INV_SQRT2PI


def _shift_down(g, prev, rowid):
    p1 = _row(prev, prev.shape[0] - 1)
    p2 = _row(prev, prev.shape[0] - 2)
    s1 = jnp.where(rowid == 0, p1, pltpu.roll(g, 1, 0))
    s2 = jnp.where(rowid == 0, p2, jnp.where(rowid == 1, p1, pltpu.roll(g, 2, 0)))
    return s1, s2


def _mlp_fwd(h1, g2, w_up4, conv_w, conv_b, w_down, gf, tgt, tm=256):
    T = h1.shape[0]

    def body(h_ref, g2_ref, wu_hbm, cw_ref, cb_ref, wd_ref, gf_ref, t_ref,
             u_ref, gate_ref, val_ref, conv_ref, act_ref, dh_ref, loss_ref, dgf_ref, carry, wu_ref, sem):
        i = pl.program_id(0)

        @pl.when(i == 0)
        def _():
            carry[...] = jnp.zeros_like(carry)
            loss_ref[...] = jnp.zeros_like(loss_ref)
            dgf_ref[...] = jnp.zeros_like(dgf_ref)
            _load_side_by_side(wu_hbm, wu_ref, sem)

        h = h_ref[...]
        u = (h * _rms(h, D_MODEL) * g2_ref[...]).astype(BF16)
        u_ref[...] = u
        y2 = jnp.zeros((tm, D_MODEL), F32)
        for lo, hi in FF_CHUNKS:
            cols = slice(lo, hi)
            rowid = lax.broadcasted_iota(jnp.int32, (tm, hi - lo), 0)
            gb = _dot(u, wu_ref[:, lo:hi]).astype(BF16)
            vb = _dot(u, wu_ref[:, D_FF + lo:D_FF + hi]).astype(BF16)
            gate_ref[:, cols] = gb
            val_ref[:, cols] = vb
            g = gb.astype(F32)
            s1, s2 = _shift_down(g, carry[:, cols], rowid)
            carry[:, cols] = g[tm - 8:, :]
            conv = cb_ref[:, cols] + cw_ref[0:1, cols] * s2 + cw_ref[1:2, cols] * s1 + cw_ref[2:3, cols] * g
            act = (_gelu(conv) * vb.astype(F32)).astype(BF16)
            conv_ref[:, cols] = conv.astype(BF16)
            act_ref[:, cols] = act
            y2 = y2 + _dot(act, wd_ref[cols, :])
        h2 = h + y2
        rf = _rms(h2, D_MODEL)
        n = h2 * rf
        gfv = gf_ref[...]
        e = n * gfv - t_ref[...]
        loss_ref[...] += jnp.sum(e * e) * (0.5 / D_MODEL)
        dy = e * (1.0 / D_MODEL)
        dgf_ref[...] += _colsum(dy * n)
        dh_ref[...] = _rms_bwd(dy * gfv, n, rf, D_MODEL)

    row = lambda w: pl.BlockSpec((tm, w), lambda i: (i, 0))
    return pl.pallas_call(
        body, name="mlp_fwd", grid=(T // tm,),
        in_specs=[row(D_MODEL), _full((1, D_MODEL)), ANY, _full((3, D_FF)),
                  _full((1, D_FF)), _once((D_FF, D_MODEL)), _full((1, D_MODEL)), row(D_MODEL)],
        out_specs=[row(D_MODEL), row(D_FF), row(D_FF), row(D_FF), row(D_FF), row(D_MODEL), _full((1, 128)),
                   _full((1, D_MODEL))],
        out_shape=[jax.ShapeDtypeStruct((T, D_MODEL), BF16)] + [jax.ShapeDtypeStruct((T, D_FF), BF16)] * 4
        + [jax.ShapeDtypeStruct((T, D_MODEL), F32),
                   jax.ShapeDtypeStruct((1, 128), F32), jax.ShapeDtypeStruct((1, D_MODEL), F32)],
        scratch_shapes=[pltpu.VMEM((8, D_FF), F32), pltpu.VMEM((D_MODEL, 2 * D_FF), BF16),
                        pltpu.SemaphoreType.DMA((N_CHIPS,))],
        compiler_params=_cp("arbitrary"),
    )(h1, g2, w_up4, conv_w, conv_b, w_down, gf, tgt)


def _mlp_bwd(dh2, gate, val, conv, conv_w, w_down, tm=256):
    T = dh2.shape[0]
    nb = T // tm
    half = D_FF // 2

    def body(dh_ref, gate_ref, val_ref, conv_ref, cw_ref, wd_ref, dgv_ref, dcw_ref, dcb_ref, carry):
        @pl.when(pl.program_id(0) == 0)
        def _():
            carry[...] = jnp.zeros_like(carry)
            dcw_ref[...] = jnp.zeros_like(dcw_ref)
            dcb_ref[...] = jnp.zeros_like(dcb_ref)

        dhb = dh_ref[...].astype(BF16)
        rowid = lax.broadcasted_iota(jnp.int32, (tm, half), 0)
        for c in range(2):
            cols = slice(c * half, (c + 1) * half)
            g = gate_ref[:, cols].astype(F32)
            v = val_ref[:, cols].astype(F32)
            cv = conv_ref[:, cols].astype(F32)
            dact = _dot_nt(dhb, wd_ref[cols, :])
            dconv = dact * v * _gelu_grad(cv)
            nxt = carry[:, cols]
            n0, n1 = _row(nxt, 0), _row(nxt, 1)
            u1 = jnp.where(rowid == tm - 1, n0, pltpu.roll(dconv, tm - 1, 0))
            u2 = jnp.where(rowid == tm - 1, n1, jnp.where(rowid == tm - 2, n0, pltpu.roll(dconv, tm - 2, 0)))
            carry[:, cols] = dconv[0:8, :]
            dcb_ref[:, cols] += _colsum(dconv)
            dcw_ref[0:1, cols] += _colsum(u2 * g)
            dcw_ref[1:2, cols] += _colsum(u1 * g)
            dcw_ref[2:3, cols] += _colsum(dconv * g)
            dgate = cw_ref[2:3, cols] * dconv + cw_ref[1:2, cols] * u1 + cw_ref[0:1, cols] * u2
            dgv_ref[:, cols] = dgate.astype(BF16)
            dgv_ref[:, D_FF + c * half:D_FF + (c + 1) * half] = (dact * _gelu(cv)).astype(BF16)

    rev = lambda w: pl.BlockSpec((tm, w), lambda i: (nb - 1 - i, 0))
    return pl.pallas_call(
        body, name="mlp_bwd", grid=(nb,),
        in_specs=[rev(D_MODEL), rev(D_FF), rev(D_FF), rev(D_FF), _full((3, D_FF)), _once((D_FF, D_MODEL))],
        out_specs=[rev(2 * D_FF), _full((3, D_FF)), _full((1, D_FF))],
        out_shape=[jax.ShapeDtypeStruct((T, 2 * D_FF), BF16), jax.ShapeDtypeStruct((3, D_FF), F32),
                   jax.ShapeDtypeStruct((1, D_FF), F32)],
        scratch_shapes=[pltpu.VMEM((8, D_FF), F32)],
        compiler_params=_cp("arbitrary"),
    )(dh2, gate, val, conv, conv_w, w_down)


def _up_out_bwd(dgv, w_up4, h1, g2, dh2, w_out, attn_o, rec_o, hg, g_a, g_h, tm=256):
    T = h1.shape[0]

    def body(dgv_ref, wu_hbm, h_ref, g2_ref, dh2_ref, wo_ref, a_ref, r_ref, gt_ref, ga_ref, gh_ref,
             dh1_ref, dg2_ref, da_ref, dr_ref, dgt_ref, dga_ref, dgh_ref, wu_ref, sem):
        @pl.when(pl.program_id(0) == 0)
        def _():
            dg2_ref[...] = jnp.zeros_like(dg2_ref)
            dga_ref[...] = jnp.zeros_like(dga_ref)
            dgh_ref[...] = jnp.zeros_like(dgh_ref)
            _load_side_by_side(wu_hbm, wu_ref, sem)

        du = _dot_nt(dgv_ref[...], wu_ref[...])
        h = h_ref[...]
        r = _rms(h, D_MODEL)
        n = h * r
        dg2_ref[...] += _colsum(du * n)
        dh1 = dh2_ref[...] + _rms_bwd(du * g2_ref[...], n, r, D_MODEL)
        dh1_ref[...] = dh1
        dmix = _dot_nt(dh1.astype(BF16), wo_ref[...])
        dan = dmix[:, :ATTN_W]
        a = a_ref[...]
        ra = _rms(a, ATTN_W)
        na = a * ra
        dga_ref[...] += _colsum(dan * na)
        da_ref[...] = _rms_bwd(dan * ga_ref[...], na, ra, ATTN_W)
        dmr = dmix[:, ATTN_W:]
        gate = gt_ref[...]
        ghv = gh_ref[...]
        rr, rn, sg = _rec_heads(r_ref[...], gate, ghv)
        dgt_ref[...] = (dmr * rn * ghv * (sg * (1.0 + gate * (1.0 - sg)))).astype(BF16)
        drecn = dmr * (gate * sg)
        dgh_ref[...] += _colsum(drecn * rn)
        drn = drecn * ghv
        prod = drn * rn
        mean = jnp.concatenate(
            [jnp.broadcast_to(jnp.sum(prod[:, h_ * HGRN_DIM:(h_ + 1) * HGRN_DIM], axis=-1, keepdims=True),
                              (tm, HGRN_DIM)) for h_ in range(HGRN_HEADS)], axis=1) * (1.0 / HGRN_DIM)
        dr_ref[...] = rr * (drn - rn * mean)

    row = lambda w: pl.BlockSpec((tm, w), lambda i: (i, 0))
    return pl.pallas_call(
        body, name="up_out_bwd", grid=(T // tm,),
        in_specs=[row(2 * D_FF), ANY, row(D_MODEL), _full((1, D_MODEL)),
                  row(D_MODEL), _once((D_MODEL, D_MODEL)), row(ATTN_W), row(HGRN_W),
                  pl.BlockSpec((tm, HGRN_W), lambda i: (i, 3)), _full((1, ATTN_W)), _full((1, HGRN_W))],
        out_specs=[row(D_MODEL), _full((1, D_MODEL)), row(ATTN_W), row(HGRN_W), row(HGRN_W),
                   _full((1, ATTN_W)), _full((1, HGRN_W))],
        out_shape=[jax.ShapeDtypeStruct((T, D_MODEL), F32), jax.ShapeDtypeStruct((1, D_MODEL), F32),
                   jax.ShapeDtypeStruct((T, ATTN_W), F32), jax.ShapeDtypeStruct((T, HGRN_W), F32),
                   jax.ShapeDtypeStruct((T, HGRN_W), BF16), jax.ShapeDtypeStruct((1, ATTN_W), F32),
                   jax.ShapeDtypeStruct((1, HGRN_W), F32)],
        scratch_shapes=[pltpu.VMEM((D_MODEL, 2 * D_FF), BF16), pltpu.SemaphoreType.DMA((N_CHIPS,))],
        compiler_params=_cp("arbitrary"),
    )(dgv, w_up4, h1, g2, dh2, w_out, attn_o, rec_o, hg, g_a, g_h)


def _in_bwd(dqkv, dhg, w_in4, x, g1, dh1, tm=512):
    T = x.shape[0]

    def body(*refs):
        parts = refs[:7]
        w_hbm, x_ref, g_ref, dh1_ref, dp_ref, dx_ref, dg_ref, w_full, sem = refs[7:]

        @pl.when(pl.program_id(0) == 0)
        def _():
            dg_ref[...] = jnp.zeros_like(dg_ref)
            _load_side_by_side(w_hbm, w_full, sem)

        dp = jnp.concatenate([p[...] for p in parts], axis=1)
        dp_ref[...] = dp
        du = _dot_nt(dp, w_full[...])
        xv = x_ref[...]
        r = _rms(xv, D_MODEL)
        n = xv * r
        dg_ref[...] += _colsum(du * n)
        dx_ref[...] = dh1_ref[...] + _rms_bwd(du * g_ref[...], n, r, D_MODEL)

    row = lambda w: pl.BlockSpec((tm, w), lambda i: (i, 0))
    return pl.pallas_call(
        body, name="in_bwd", grid=(T // tm,),
        in_specs=[row(ATTN_W)] * 7 + [ANY, row(D_MODEL), _full((1, D_MODEL)), row(D_MODEL)],
        out_specs=[row(IN_TOTAL), row(D_MODEL), _full((1, D_MODEL))],
        out_shape=[jax.ShapeDtypeStruct((T, IN_TOTAL), BF16), jax.ShapeDtypeStruct((T, D_MODEL), F32),
                   jax.ShapeDtypeStruct((1, D_MODEL), F32)],
        scratch_shapes=[pltpu.VMEM((D_MODEL, IN_TOTAL), BF16), pltpu.SemaphoreType.DMA((N_CHIPS,))],
        compiler_params=_cp("arbitrary"),
    )(*dqkv, *dhg, w_in4, x, g1, dh1)


def _dw(a, b, kb, nb_, name, tk=1024, side=1):
    T, K = a.shape
    N = b.shape[1]
    nk, nn, nt = K // kb, N // (nb_ * side), T // tk

    def body(a_ref, b_ref, o_ref, acc):
        t = pl.program_id(2)

        @pl.when(t == 0)
        def _():
            acc[...] = jnp.zeros_like(acc)

        acc[...] += _dot_tn(a_ref[...], b_ref[...].astype(BF16))

        @pl.when(t == nt - 1)
        def _():
            for s in range(side):
                o_ref[s] = acc[:, s * nb_:(s + 1) * nb_].astype(BF16)

    return pl.pallas_call(
        body, name=name, grid=(nk, nn, nt),
        in_specs=[pl.BlockSpec((tk, kb), lambda i, j, t: (t, i)),
                  pl.BlockSpec((tk, nb_ * side), lambda i, j, t: (t, j))],
        out_specs=pl.BlockSpec((side, kb, nb_), lambda i, j, t: (i * nn + j, 0, 0)),
        out_shape=jax.ShapeDtypeStruct((nk * nn * side, kb, nb_), BF16),
        scratch_shapes=[pltpu.VMEM((kb, nb_ * side), F32)],
        compiler_params=_cp("arbitrary", "arbitrary", "arbitrary"),
    )(a, b)


def _local_step(x, tgt, g1, w_in4, g_a, g_h, lb, w_out, g2, w_up4, conv_w, conv_b, w_down, gf):
    a = _step_mixers(x, g1, w_in4, lb)
    b = _step_channel(a, x, tgt, g_a, g_h, w_out, g2, w_up4, conv_w, conv_b, w_down, gf)
    c = _step_mixers_bwd(a, b, x, g1, w_in4, lb)
    small = dict(g1=c["dg1"], g_a=b["dga"], g_h=b["dgh"], lb=c["dlb"], g2=b["dg2"], conv_w=b["dcw"], conv_b=b["dcb"],
                 gf=b["dgf"])
    return b["loss"], c["dx"], small, dict(w_in=c["dw_in"], w_out=b["dw_out"], w_up=b["dw_up"], w_down=b["dw_down"])


def _step_mixers(x, g1, w_in4, lb):
    u1, qkv, hg = _in_proj(x, g1, w_in4)
    attn_o, lse = _attn_fwd(qkv)
    rec_o, states = _hgrn_fwd(hg, lb)
    return dict(u1=u1, qkv=qkv, hg=hg, attn_o=attn_o, lse=lse, rec_o=rec_o, states=states)


def _step_channel(a, x, tgt, g_a, g_h, w_out, g2, w_up4, conv_w, conv_b, w_down, gf):
    h1, mixed = _mix_out(a["attn_o"], a["rec_o"], a["hg"], x, g_a, g_h, w_out)
    u2, gate, val, conv, act, dh2, loss, dgf = _mlp_fwd(h1, g2, w_up4, conv_w, conv_b, w_down, gf, tgt)
    dgv, dcw, dcb = _mlp_bwd(dh2, gate, val, conv, conv_w, w_down)
    dw_down = _dw(act, dh2, D_FF // 2, D_MODEL, "dw_down").reshape(N_CHIPS, D_FF // N_CHIPS, D_MODEL)
    dh1, dg2, da, dr, dgt, dga, dgh = _up_out_bwd(dgv, w_up4, h1, g2, dh2, w_out, a["attn_o"], a["rec_o"], a["hg"],
                                                  g_a, g_h)
    dw_up = _dw(u2, dgv, D_MODEL, UP_SHARD, "dw_up", side=2)
    dw_out = _dw(mixed, dh1, D_MODEL, D_MODEL, "dw_out").reshape(N_CHIPS, D_MODEL // N_CHIPS, D_MODEL)
    return dict(loss=loss, dgf=dgf, dcw=dcw, dcb=dcb, dg2=dg2, dga=dga, dgh=dgh, dh1=dh1, da=da, dr=dr, dgt=dgt,
                dw_down=dw_down, dw_up=dw_up, dw_out=dw_out)


def _step_mixers_bwd(a, b, x, g1, w_in4, lb, dqkv=None):
    if dqkv is None:
        dqkv = _attn_bwd(a["qkv"], a["attn_o"], a["lse"], b["da"])
    dhq, dhf, dhi, dlb = _hgrn_bwd(a["hg"], lb, a["states"], b["dr"])
    dproj, dx, dg1 = _in_bwd(dqkv, [dhq, dhf, dhi, b["dgt"]], w_in4, x, g1, b["dh1"])
    dw_in = _dw(a["u1"], dproj, D_MODEL, IN_SHARD, "dw_in", side=2)
    return dict(dx=dx, dg1=dg1, dlb=dlb, dw_in=dw_in)


BIG = ("w_in", "w_out", "w_up", "w_down")
ANY = pl.BlockSpec(memory_space=pl.ANY)


def _place():
    x, y, c = lax.axis_index("x"), lax.axis_index("y"), lax.axis_index("c")
    chips = [(1 - x, y), (x, 1 - y), (1 - x, 1 - y)]
    return x, y, c, chips


def _remote(src, dst, send_sems, recv_sems, k, to):
    return pltpu.make_async_remote_copy(src_ref=src, dst_ref=dst, send_sem=send_sems.at[k], recv_sem=recv_sems.at[k],
                                        device_id=to, device_id_type=MESH)


def _gather_weights(shards, conv_w):
    n = len(shards)
    halves = [s.shape[0] // 2 for s in shards]

    def body(*refs):
        ins, cw, outs, ocw = refs[:n], refs[n], refs[n + 1:2 * n + 1], refs[2 * n + 1]
        send_sems, recv_sems = refs[2 * n + 2:]
        x, y, c, chips = _place()
        me, sibling = 2 * x + y, (x, y, 1 - c)

        def part(w, chip, half):
            return outs[w].at[chip, pl.ds(half * halves[w], halves[w]), :]

        sent = []
        for j, chip in enumerate(chips):
            for w in range(n):
                sent.append(_remote(ins[w].at[pl.ds(c * halves[w], halves[w]), :], part(w, me, c),
                                    send_sems, recv_sems, w * 3 + j, (*chip, c)))
            sent.append(_remote(cw, ocw.at[me], send_sems, recv_sems, 6 * n + j, (*chip, c)))
        for cp in sent:
            cp.start()
        for j, chip in enumerate(chips):
            kj = 2 * chip[0] + chip[1]
            for w in range(n):
                _remote(part(w, kj, c), part(w, kj, c), send_sems, recv_sems, w * 3 + j, (*chip, c)).wait_recv()
                fwd = _remote(part(w, kj, c), part(w, kj, c), send_sems, recv_sems, 3 * n + w * 3 + j, sibling)
                fwd.start()
                sent.append(fwd)
        for j, chip in enumerate(chips):
            kj = 2 * chip[0] + chip[1]
            for w in range(n):
                _remote(part(w, kj, 1 - c), part(w, kj, 1 - c), send_sems, recv_sems, 3 * n + w * 3 + j,
                        sibling).wait_recv()
            _remote(cw, ocw.at[kj], send_sems, recv_sems, 6 * n + j, (*chip, c)).wait_recv()
        for cp in sent:
            cp.wait_send()

    n_sem = 6 * n + 3
    outs = pl.pallas_call(
        body, name="gather_weights",
        in_specs=[ANY] * (n + 1), out_specs=[ANY] * (n + 1),
        out_shape=[jax.ShapeDtypeStruct((N_CHIPS,) + s.shape, s.dtype) for s in shards]
        + [jax.ShapeDtypeStruct((N_CHIPS,) + conv_w.shape, conv_w.dtype)],
        scratch_shapes=[pltpu.SemaphoreType.DMA((n_sem,)), pltpu.SemaphoreType.DMA((n_sem,))],
    )(*shards, conv_w)
    chip = 2 * lax.axis_index("x") + lax.axis_index("y")
    return [lax.dynamic_update_slice(o, s[None], (chip,) + (0,) * s.ndim) for o, s in zip(outs, [*shards, conv_w])]


def _allreduce_small(buf):
    rows = buf.shape[0]

    def body(in_ref, out_ref, slots, send_sems, recv_sems):
        x, y, c, _ = _place()
        me = 4 * x + 2 * y + c
        slots[me] = in_ref[...]
        sent = []
        for p in range(1, 8):
            to = (x ^ (p >> 2), y ^ ((p >> 1) & 1), c ^ (p & 1))
            sent.append(_remote(in_ref, slots.at[me], send_sems, recv_sems, p, to))
        for cp in sent:
            cp.start()
        for p in range(1, 8):
            frm = 4 * (x ^ (p >> 2)) + 2 * (y ^ ((p >> 1) & 1)) + (c ^ (p & 1))
            _remote(in_ref, slots.at[frm], send_sems, recv_sems, p, (x, y, c)).wait_recv()
        for cp in sent:
            cp.wait_send()
        acc = slots[0]
        for d in range(1, 8):
            acc = acc + slots[d]
        out_ref[...] = acc

    vm = pl.BlockSpec(memory_space=pltpu.VMEM)
    return pl.pallas_call(
        body, name="allreduce_small", in_specs=[vm], out_specs=vm,
        out_shape=jax.ShapeDtypeStruct(buf.shape, F32),
        scratch_shapes=[pltpu.VMEM((8, rows, 128), F32), pltpu.SemaphoreType.DMA((8,)), pltpu.SemaphoreType.DMA((8,))],
    )(buf)


def _pair_exchange(gs, name):
    n = len(gs)
    halves = [g.shape[1] // 2 for g in gs]

    def body(*refs):
        g, got = refs[:n], refs[n:2 * n]
        send_sems, recv_sems = refs[2 * n:]
        x, y, c, _ = _place()
        cps = [_remote(g[w].at[:, pl.ds((1 - c) * halves[w], halves[w]), :], got[w], send_sems, recv_sems, w,
                       (x, y, 1 - c)) for w in range(n)]
        for cp in cps:
            cp.start()
        for cp in cps:
            cp.wait()

    return pl.pallas_call(
        body, name=name, in_specs=[ANY] * n, out_specs=[ANY] * n,
        out_shape=[jax.ShapeDtypeStruct((N_CHIPS, h, g.shape[2]), g.dtype) for g, h in zip(gs, halves)],
        scratch_shapes=[pltpu.SemaphoreType.DMA((n,)), pltpu.SemaphoreType.DMA((n,))],
    )(*gs)


def _core_id():
    return lax.axis_index("c").reshape(1).astype(jnp.int32)


def _pair_sum(g, got, name):
    h, C = got.shape[1:]

    def body(c_ref, g_ref, b_ref, o_ref):
        o_ref[...] = (g_ref[...].astype(F32) + b_ref[...].astype(F32)).astype(BF16)

    blk = pl.BlockSpec((1, h, C), lambda k, c_ref: (k, 0, 0))
    return pl.pallas_call(
        body, name=name,
        grid_spec=pltpu.PrefetchScalarGridSpec(
            num_scalar_prefetch=1, grid=(N_CHIPS,),
            in_specs=[pl.BlockSpec((1, h, C), lambda k, c_ref: (k, c_ref[0], 0)), blk], out_specs=blk),
        out_shape=jax.ShapeDtypeStruct(got.shape, BF16), compiler_params=_cp("arbitrary"))(_core_id(), g, got)


def _sum_partials(g, got, landed, name):
    h, C = got.shape[1:]

    def body(ids, g_ref, b_ref, l_ref, o_ref):
        acc = g_ref[0].astype(F32) + b_ref[0].astype(F32)
        for j in range(3):
            acc = acc + l_ref[j].astype(F32)
        o_ref[...] = acc

    ids = jnp.stack([2 * lax.axis_index("x") + lax.axis_index("y"), lax.axis_index("c")]).astype(jnp.int32)
    return pl.pallas_call(
        body, name=name,
        grid_spec=pltpu.PrefetchScalarGridSpec(
            num_scalar_prefetch=1, grid=(1,),
            in_specs=[pl.BlockSpec((1, h, C), lambda i, ids: (ids[0], ids[1], 0)),
                      pl.BlockSpec((1, h, C), lambda i, ids: (ids[0], 0, 0)),
                      pl.BlockSpec((3, h, C), lambda i, ids: (0, 0, 0))],
            out_specs=pl.BlockSpec((h, C), lambda i, ids: (ids[1], 0))),
        out_shape=jax.ShapeDtypeStruct((2 * h, C), F32), compiler_params=_cp("arbitrary"))(ids, g, got, landed)


def _pair_share(reds, name):
    n = len(reds)

    def body(*refs):
        out = refs[n:2 * n]
        send_sems, recv_sems = refs[2 * n:]
        x, y, c, _ = _place()
        def half(w, which):
            h = out[w].shape[0] // 2
            return out[w].at[pl.ds(which * h, h), :]

        cps = [_remote(half(w, c), half(w, c), send_sems, recv_sems, w, (x, y, 1 - c)) for w in range(n)]
        for cp in cps:
            cp.start()
        for w in range(n):
            _remote(half(w, 1 - c), half(w, 1 - c), send_sems, recv_sems, w, (x, y, 1 - c)).wait_recv()
        for cp in cps:
            cp.wait_send()

    return pl.pallas_call(
        body, name=name, in_specs=[ANY] * n, out_specs=[ANY] * n,
        out_shape=[jax.ShapeDtypeStruct(r.shape, F32) for r in reds],
        input_output_aliases={w: w for w in range(n)},
        scratch_shapes=[pltpu.SemaphoreType.DMA((n,)), pltpu.SemaphoreType.DMA((n,))],
    )(*reds)


HBM = pl.BlockSpec(memory_space=pltpu.HBM)
SEM = pl.BlockSpec(memory_space=pltpu.SEMAPHORE)
DATAFLOW = pltpu.SideEffectType.DATAFLOW_SIDE_EFFECTING


def _copies_start(name, srcs, lands, plan, n_copies, after):
    ns, nb, na = len(srcs), len(srcs) + len(lands), len(after)

    def body(*refs):
        src_refs, land_refs = refs[:ns], refs[ns:nb]
        send_sems, recv_sems = refs[nb + na:nb + na + 2]
        token = refs[-1]
        for k, (src, there, _, to) in enumerate(plan(src_refs, land_refs)):
            _remote(src, there, send_sems, recv_sems, k, to).start()
        token[...] = jnp.zeros_like(token)

    hbm = lambda a: pltpu.HBM(a.shape, a.dtype)
    outs = pl.pallas_call(
        body, name=name,
        out_shape=(pltpu.SemaphoreType.DMA((n_copies,)), pltpu.SemaphoreType.DMA((n_copies,)),
                   *[hbm(a) for a in srcs], *[hbm(a) for a in lands], jax.ShapeDtypeStruct((8, 128), F32)),
        in_specs=[HBM] * nb + [ANY] * na,
        out_specs=(SEM, SEM, *[HBM] * nb, pl.BlockSpec(memory_space=pltpu.VMEM)),
        input_output_aliases={i: 2 + i for i in range(nb)},
        compiler_params=pltpu.CompilerParams(has_side_effects=DATAFLOW),
    )(*[pltpu.with_memory_space_constraint(a, pltpu.HBM) for a in (*srcs, *lands)], *after)
    return outs[0], outs[1], outs[2:2 + ns], outs[2 + ns:2 + nb], outs[-1]


def _copies_wait(name, send_sems, recv_sems, srcs, lands, plan, after):
    ns, nb, na = len(srcs), len(srcs) + len(lands), len(after)

    def body(*refs):
        src_refs, land_refs = refs[:ns], refs[ns:nb]
        send_sems, recv_sems = refs[nb:nb + 2]
        for k, (src, _, here, to) in enumerate(plan(src_refs, land_refs)):
            cp = _remote(src, here, send_sems, recv_sems, k, to)
            cp.wait_send()
            cp.wait_recv()

    hbm = lambda a: pltpu.HBM(a.shape, a.dtype)
    outs = pl.pallas_call(
        body, name=name,
        out_shape=(*[hbm(a) for a in srcs], *[hbm(a) for a in lands]),
        in_specs=[HBM] * nb + [SEM, SEM] + [ANY] * na,
        out_specs=tuple([HBM] * nb),
        input_output_aliases={i: i for i in range(nb)},
        compiler_params=pltpu.CompilerParams(has_side_effects=DATAFLOW),
    )(*srcs, *lands, send_sems, recv_sems, *after)
    return outs[:ns], outs[ns:]


def _gather_plan(halves):
    def plan(shards, lands):
        x, y, c, chips = _place()
        me = 2 * x + y
        copies = []
        for w, h in enumerate(halves):
            rows = pl.ds(c * h, h)
            for chip in chips:
                copies.append((shards[w].at[rows, :], lands[w].at[me, rows, :],
                               lands[w].at[2 * chip[0] + chip[1], rows, :], (*chip, c)))
        return copies
    return plan


def _reduce_plan(n):
    def plan(ps, lands):
        x, y, c, chips = _place()
        return [(ps[w].at[2 * chip[0] + chip[1]], lands[w].at[j], lands[w].at[j], (*chip, c))
                for w in range(n) for j, chip in enumerate(chips)]
    return plan


def _forward_plan(halves):
    def plan(_, lands):
        x, y, c, chips = _place()

        def part(w, chip, half):
            return lands[w].at[2 * chip[0] + chip[1], pl.ds(half * halves[w], halves[w]), :]

        return [(part(w, chip, c), part(w, chip, c), part(w, chip, 1 - c), (x, y, 1 - c))
                for w in range(len(halves)) for chip in chips]
    return plan


def _pair_plan(halves):
    def plan(gs, gots):
        x, y, c, _ = _place()
        return [(gs[w].at[:, pl.ds((1 - c) * h, h), :], gots[w], gots[w], (x, y, 1 - c)) for w, h in enumerate(halves)]
    return plan


def _place_own(gathered, shards):
    chip = 2 * lax.axis_index("x") + lax.axis_index("y")
    return [lax.dynamic_update_slice(o, s[None], (chip, 0, 0)) for o, s in zip(gathered, shards)]


def _adamw(w, g, m, v, name, tr=None):
    R, C = w.shape
    tr = tr or R // 4

    def body(w_ref, g_ref, m_ref, v_ref, d_ref, nm_ref, nv_ref):
        gv = g_ref[...]
        nm = ADAM_B1 * m_ref[...] + (1.0 - ADAM_B1) * gv
        nv = ADAM_B2 * v_ref[...] + (1.0 - ADAM_B2) * (gv * gv)
        m_hat = nm / (1.0 - ADAM_B1 ** ADAM_STEP)
        v_hat = nv / (1.0 - ADAM_B2 ** ADAM_STEP)
        d_ref[...] = -ADAM_LR * (m_hat / (jnp.sqrt(v_hat) + ADAM_EPS) + ADAM_WD * w_ref[...])
        nm_ref[...] = nm
        nv_ref[...] = nv

    blk = pl.BlockSpec((tr, C), lambda i: (i, 0))
    return pl.pallas_call(body, name=name, grid=(R // tr,), in_specs=[blk] * 4, out_specs=[blk] * 3,
                          out_shape=[jax.ShapeDtypeStruct((R, C), F32)] * 3, compiler_params=_cp("arbitrary"))(w, g, m, v)


SMALL = (("norm1_g", 1024), ("attn_norm_g", 512), ("hgrn_norm_g", 512), ("hgrn_lb_logits", 1024), ("norm2_g", 1024),
         ("conv_b", D_FF), ("final_norm_g", 1024), ("conv_w", 3 * D_FF))
SMALL_ROWS = 136


def _pack(parts, rows):
    flat = jnp.concatenate([p.reshape(-1).astype(F32) for p in parts])
    return jnp.pad(flat, (0, rows * 128 - flat.shape[0])).reshape(rows, 128)


def kernel(x, norm1_g, w_in, attn_norm_g, hgrn_norm_g, hgrn_lb_logits, w_out, norm2_g, w_up, conv_w, conv_b, w_down, final_norm_g, loss_target, m_norm1_g, m_w_in, m_attn_norm_g, m_hgrn_norm_g, m_hgrn_lb_logits, m_w_out, m_norm2_g, m_w_up, m_conv_w, m_conv_b, m_w_down, m_final_norm_g, v_norm1_g, v_w_in, v_attn_norm_g, v_hgrn_norm_g, v_hgrn_lb_logits, v_w_out, v_norm2_g, v_w_up, v_conv_w, v_conv_b, v_w_down, v_final_norm_g):
    w = dict(norm1_g=norm1_g, w_in=w_in, attn_norm_g=attn_norm_g, hgrn_norm_g=hgrn_norm_g,
             hgrn_lb_logits=hgrn_lb_logits, w_out=w_out, norm2_g=norm2_g, w_up=w_up, conv_w=conv_w, conv_b=conv_b,
             w_down=w_down, final_norm_g=final_norm_g)
    m = dict(norm1_g=m_norm1_g, w_in=m_w_in, attn_norm_g=m_attn_norm_g, hgrn_norm_g=m_hgrn_norm_g,
             hgrn_lb_logits=m_hgrn_lb_logits, w_out=m_w_out, norm2_g=m_norm2_g, w_up=m_w_up, conv_w=m_conv_w,
             conv_b=m_conv_b, w_down=m_w_down, final_norm_g=m_final_norm_g)
    v = dict(norm1_g=v_norm1_g, w_in=v_w_in, attn_norm_g=v_attn_norm_g, hgrn_norm_g=v_hgrn_norm_g,
             hgrn_lb_logits=v_hgrn_lb_logits, w_out=v_w_out, norm2_g=v_norm2_g, w_up=v_w_up, conv_w=v_conv_w,
             conv_b=v_conv_b, w_down=v_w_down, final_norm_g=v_final_norm_g)
    names = list(w)
    chip = 2 * lax.axis_index("x") + lax.axis_index("y")

    shards = {k: w[k][0].astype(BF16) for k in BIG}
    w_in4, conv_w4 = _gather_weights([shards["w_in"]], conv_w[0])
    conv_w_full = jnp.transpose(conv_w4, (1, 0, 2)).reshape(3, D_FF)
    lb = jax.nn.softmax(hgrn_lb_logits, axis=0)[0:1]
    late = [shards[k] for k in BIG[1:]]
    gather_plan = _gather_plan([s.shape[0] // 2 for s in late])
    started = _copies_start("gather_start", late, [lax.empty((N_CHIPS,) + s.shape, BF16) for s in late], gather_plan,
                            3 * len(late), after=(w_in4,))
    u1, qkv, hg = _in_proj(x[0], norm1_g + started[4][0:1, 0:1], w_in4)
    attn_o, lse = _attn_fwd(qkv)
    late, landed_w = _copies_wait("gather_wait", *started[:4], gather_plan, after=(attn_o,))
    forward_plan = _forward_plan([s.shape[0] // 2 for s in late])
    started = _copies_start("forward_start", [], landed_w, forward_plan, 3 * len(late), after=())
    rec_o, states = _hgrn_fwd(hg, lb + started[4][0:1, 0:1])
    a = dict(u1=u1, qkv=qkv, hg=hg, attn_o=attn_o, lse=lse, rec_o=rec_o, states=states)
    w_out4, w_up4, w_down4 = _place_own(
        _copies_wait("forward_wait", *started[:4], forward_plan, after=(rec_o,))[1], late)

    b = _step_channel(a, x[0], loss_target[0], attn_norm_g, hgrn_norm_g, w_out4.reshape(D_MODEL, D_MODEL), norm2_g,
                      w_up4, conv_w_full, conv_b, w_down4.reshape(D_FF, D_MODEL), final_norm_g.reshape(1, D_MODEL))

    early = [b["dw_out"], b["dw_up"], b["dw_down"]]
    pair_plan = _pair_plan([gk.shape[1] // 2 for gk in early])
    started = _copies_start("pair_start", early,
                            [lax.empty((N_CHIPS, gk.shape[1] // 2, gk.shape[2]), BF16) for gk in early], pair_plan,
                            len(early), after=())
    dqkv = _attn_bwd(qkv, attn_o, lse, b["da"], started[4])
    early, gots = _copies_wait("pair_wait", *started[:4], pair_plan, after=(dqkv[0],))
    ps = [_pair_sum(gk, got, f"pair_sum_{k}") for gk, got, k in zip(early, gots, BIG[1:])]
    reduce_plan = _reduce_plan(len(ps))
    started = _copies_start("reduce_start", ps, [lax.empty((3,) + p.shape[1:], BF16) for p in ps], reduce_plan,
                            3 * len(ps), after=())
    c = _step_mixers_bwd(a, b, x[0], norm1_g, w_in4, lb + started[4][0:1, 0:1], dqkv)
    gots_in = _pair_exchange([c["dw_in"]], "pair_exchange_w_in")
    ps_in = _pair_sum(c["dw_in"], gots_in[0], "pair_sum_w_in")
    plan_in = _reduce_plan(1)
    started_in = _copies_start("reduce_start_w_in", [ps_in], [lax.empty((3,) + ps_in.shape[1:], BF16)], plan_in, 3,
                               after=())
    landed = _copies_wait("reduce_wait", *started[:4], reduce_plan, after=(started_in[4],))[1]
    reds = [_sum_partials(gk, got, l, f"sum_partials_{k}") for gk, got, l, k in zip(early, gots, landed, BIG[1:])]
    g = dict(zip(BIG[1:], _pair_share(reds, "pair_share")))
    delta, new_m, new_v = {}, {}, {}
    for k in BIG[1:]:
        delta[k], new_m[k], new_v[k] = _adamw(w[k][0], g[k], m[k][0], v[k][0], f"adamw_{k}")

    loss, dx = b["loss"], c["dx"]
    small = dict(g1=c["dg1"], g_a=b["dga"], g_h=b["dgh"], lb=c["dlb"], g2=b["dg2"], conv_w=b["dcw"], conv_b=b["dcb"],
                 gf=b["dgf"])
    dlb = small["lb"] * lb * (1.0 - lb)
    grads_small = dict(norm1_g=small["g1"], attn_norm_g=small["g_a"], hgrn_norm_g=small["g_h"],
                       hgrn_lb_logits=jnp.concatenate([dlb, -dlb], axis=0), norm2_g=small["g2"],
                       conv_b=small["conv_b"], final_norm_g=small["gf"], conv_w=small["conv_w"])
    summed = _allreduce_small(_pack([grads_small[k] for k, _ in SMALL] + [loss[0, 0:1]], SMALL_ROWS)).reshape(-1)
    off = 0
    for k, size in SMALL:
        g[k] = summed[off:off + size]
        off += size
    loss_total = summed[off]
    g["conv_w"] = lax.dynamic_slice(g["conv_w"].reshape(3, D_FF), (0, chip * (D_FF // N_CHIPS)), (3, D_FF // N_CHIPS))
    small_names = [k for k in names if k not in BIG]
    rows = 80
    packed = _adamw(_pack([w[k] for k in small_names], rows), _pack([g[k] for k in small_names], rows),
                    _pack([m[k] for k in small_names], rows), _pack([v[k] for k in small_names], rows), "adamw_small", tr=rows)
    flat = [a.reshape(-1) for a in packed]
    off = 0
    for k in small_names:
        size = w[k].size
        delta[k], new_m[k], new_v[k] = (a[off:off + size].reshape(w[k].shape) for a in flat)
        g[k] = g[k].reshape(w[k].shape)
        off += size

    landed_in = _copies_wait("reduce_wait_w_in", *started_in[:4], plan_in, after=(packed[0], delta["w_up"]))[1]
    red_in = _sum_partials(c["dw_in"], gots_in[0], landed_in[0], "sum_partials_w_in")
    g["w_in"] = _pair_share([red_in], "pair_share_w_in")[0]
    delta["w_in"], new_m["w_in"], new_v["w_in"] = _adamw(w_in[0], g["w_in"], m_w_in[0], v_w_in[0], "adamw_w_in")
    for k in BIG:
        g[k], delta[k], new_m[k], new_v[k] = g[k][None], delta[k][None], new_m[k][None], new_v[k][None]

    return (loss_total, dx[None], *[g[k] for k in names], *[delta[k] for k in names],
            *[new_m[k] for k in names], *[new_v[k] for k in names])
```

```python
import functools
import math

import jax
import jax.numpy as jnp
from jax import lax
from jax.experimental import pallas as pl
from jax.experimental.pallas import tpu as pltpu

F32 = jnp.float32
BF16 = jnp.bfloat16

D_MODEL = 1024
ATTN_W = 512
HGRN_W = 512
HEAD_PAIR = 128
ATTN_BLK = 128
DILATIONS = (1, 4, 16)
ATTN_CHAINS = 4
ATTN_CHAINS_FWD = 8
HGRN_HEADS = 4
HGRN_DIM = 128
HGRN_CHUNK = 64
SUPER = 256
HGRN_SIDE = 2
D_FF = 2816
FF_CHUNKS = ((0, 1536), (1536, D_FF))
N_CHIPS = 4
IN_TOTAL = 3584
IN_SHARD = IN_TOTAL // N_CHIPS
UP_SHARD = 2 * D_FF // N_CHIPS
QKV_W = 3 * ATTN_W
HG_W = 4 * HGRN_W
EPS = 1e-6
NEG = -1e30
V7X_VMEM_BYTES = 64 * 1024 * 1024
VMEM_LIMIT = V7X_VMEM_BYTES - 8 * 1024 * 1024

ADAM_LR = 0.001
ADAM_B1 = 0.9
ADAM_B2 = 0.999
ADAM_EPS = 1e-08
ADAM_WD = 0.01
ADAM_STEP = 10

MESH = pl.DeviceIdType.MESH


def _cp(*sem):
    return pltpu.CompilerParams(dimension_semantics=sem or None, vmem_limit_bytes=VMEM_LIMIT)


def _dot(a, b):
    return jnp.dot(a, b, preferred_element_type=F32)


def _dot_nt(a, b):
    return lax.dot_general(a, b, (((1,), (1,)), ((), ())), preferred_element_type=F32)


def _dot_tn(a, b):
    return lax.dot_general(a, b, (((0,), (0,)), ((), ())), preferred_element_type=F32)


def _sigmoid(x):
    return 1.0 / (1.0 + jnp.exp(-x))


def _rms(x, width):
    return lax.rsqrt(jnp.sum(x * x, axis=-1, keepdims=True) * (1.0 / width) + EPS)


def _rms_bwd(dn, n, r, width):
    return r * (dn - n * (jnp.sum(dn * n, axis=-1, keepdims=True) * (1.0 / width)))


def _colsum(x):
    return jnp.sum(x, axis=0, keepdims=True)


def _row(v, k):
    rid = lax.broadcasted_iota(jnp.int32, v.shape, 0)
    return jnp.sum(jnp.where(rid == k, v, 0.0), axis=0, keepdims=True)


def _full(shape):
    return pl.BlockSpec(shape, lambda *_: (0,) * len(shape))


def _once(shape):
    return pl.BlockSpec(shape, lambda *_: (0,) * len(shape), pipeline_mode=pl.Buffered(1))


def _load_side_by_side(w_hbm, w_full, sem):
    width = w_hbm.shape[2]
    cps = [pltpu.make_async_copy(w_hbm.at[k], w_full.at[:, pl.ds(k * width, width)], sem.at[k]) for k in range(N_CHIPS)]
    for cp in cps:
        cp.start()
    for cp in cps:
        cp.wait()


def _in_proj(x, g1, w_in4, tm=512):
    T = x.shape[0]

    def body(x_ref, g_ref, w_hbm, u_ref, qkv_ref, hg_ref, w_full, sem):
        @pl.when(pl.program_id(0) == 0)
        def _():
            _load_side_by_side(w_hbm, w_full, sem)

        xv = x_ref[...]
        u = (xv * _rms(xv, D_MODEL) * g_ref[...]).astype(BF16)
        u_ref[...] = u
        p = _dot(u, w_full[...])
        qkv_ref[...] = p[:, :QKV_W]
        hg_ref[...] = p[:, QKV_W:]

    return pl.pallas_call(
        body, name="in_proj", grid=(T // tm,),
        in_specs=[pl.BlockSpec((tm, D_MODEL), lambda i: (i, 0)), _full((1, D_MODEL)), ANY],
        out_specs=[pl.BlockSpec((tm, D_MODEL), lambda i: (i, 0)), pl.BlockSpec((tm, QKV_W), lambda i: (i, 0)),
                   pl.BlockSpec((tm, HG_W), lambda i: (i, 0))],
        out_shape=[jax.ShapeDtypeStruct((T, D_MODEL), BF16), jax.ShapeDtypeStruct((T, QKV_W), F32),
                   jax.ShapeDtypeStruct((T, HG_W), F32)],
        scratch_shapes=[pltpu.VMEM((D_MODEL, IN_TOTAL), BF16), pltpu.SemaphoreType.DMA((N_CHIPS,))],
        compiler_params=_cp("arbitrary"),
    )(x, g1, w_in4)


def _attn_masks(bias_ref):
    lane = lax.broadcasted_iota(jnp.int32, (ATTN_BLK, HEAD_PAIR), 1)
    row = lax.broadcasted_iota(jnp.int32, (2 * ATTN_BLK, 2 * ATTN_BLK), 0)
    col = lax.broadcasted_iota(jnp.int32, (2 * ATTN_BLK, 2 * ATTN_BLK), 1)
    base = jnp.where(row >= ATTN_BLK, row - ATTN_BLK, row) - col
    for k in range(2):
        dist = base + k * ATTN_BLK
        bias_ref[k] = jnp.where((dist >= 0) & (dist <= ATTN_BLK), 0.0, NEG)
    bias_ref[2] = jnp.where(col >= ATTN_BLK, bias_ref[1], NEG)
    return lane < 64


def _two_heads(blk, first):
    zero = jnp.zeros_like(blk)
    return jnp.concatenate([jnp.where(first, blk, zero), jnp.where(first, zero, blk)], axis=0)


def _attn_rows(idx, nb, d):
    r, n = idx // nb, idx % nb
    kb = jnp.maximum(n - 1, 0)
    if d == 1:
        q0 = pl.multiple_of(n * ATTN_BLK, ATTN_BLK)
        k0 = pl.multiple_of(kb * ATTN_BLK, ATTN_BLK)
        return pl.ds(q0, ATTN_BLK), pl.ds(k0, 2 * ATTN_BLK), n - kb
    return (pl.ds(r + d * ATTN_BLK * n, ATTN_BLK, stride=d), pl.ds(r + d * ATTN_BLK * kb, 2 * ATTN_BLK, stride=d),
            n - kb)


def _attn_fwd(qkv):
    T = qkv.shape[0]

    per_chain = T // ATTN_BLK // ATTN_CHAINS_FWD

    def body(q_ref, k_ref, v_ref, o_ref, m_ref, l_ref, bias_ref):
        first = _attn_masks(bias_ref)
        for bi, d in enumerate(DILATIONS):
            nb = T // d // ATTN_BLK

            carried = d > 1 and per_chain % nb == 0

            def block(idx, kept=None, d=d, nb=nb, bi=bi, carried=carried):
                rows, keys, which = _attn_rows(idx, nb, d)
                q2 = _two_heads(q_ref[rows, :] * 0.125, first).astype(BF16)
                if carried:
                    k_own, v_own = k_ref[rows, :].astype(BF16), v_ref[rows, :].astype(BF16)
                    kw = jnp.concatenate([kept[0], k_own], axis=0)
                    vw = jnp.concatenate([kept[1], v_own], axis=0)
                    which = 2 - which
                else:
                    kw = k_ref[keys, :].astype(BF16)
                    vw = v_ref[keys, :].astype(BF16)
                old = (o_ref[rows, :], m_ref[rows, :], l_ref[rows, :]) if bi else None
                s = _dot_nt(q2, kw) + bias_ref[which]
                mb = jnp.max(s, axis=-1, keepdims=True)
                p = jnp.exp(s - mb)
                lb = jnp.sum(p, axis=-1, keepdims=True)
                o2 = _dot(p.astype(BF16), vw)
                o = jnp.where(first, o2[:ATTN_BLK], o2[ATTN_BLK:])
                m = jnp.where(first, mb[:ATTN_BLK], mb[ATTN_BLK:])
                l = jnp.where(first, lb[:ATTN_BLK], lb[ATTN_BLK:])
                if bi:
                    po, pm, pl_ = old
                    mn = jnp.maximum(pm, m)
                    wa = jnp.exp(pm - mn)
                    wb = jnp.exp(m - mn)
                    o, l, m = po * wa + o * wb, pl_ * wa + l * wb, mn
                return (rows, o, m, l), ((k_own, v_own) if carried else 0)

            def step(i, kept, block=block, carried=carried):
                done = [block(i + ch * per_chain, kept[ch] if carried else None) for ch in range(ATTN_CHAINS_FWD)]
                for (rows, o, m, l), _ in done:
                    o_ref[rows, :] = o
                    m_ref[rows, :] = m
                    l_ref[rows, :] = l
                return tuple(k for _, k in done) if carried else kept

            zero = jnp.zeros((ATTN_BLK, HEAD_PAIR), BF16)
            lax.fori_loop(0, per_chain, step, ((zero, zero),) * ATTN_CHAINS_FWD if carried else 0)

        def finish(i, carry):
            rows = pl.ds(pl.multiple_of(i * SUPER, SUPER), SUPER)
            l = l_ref[rows, :]
            o_ref[rows, :] = o_ref[rows, :] / l
            m_ref[rows, :] = m_ref[rows, :] + jnp.log(l)
            return carry

        lax.fori_loop(0, T // SUPER, finish, 0)

    col = lambda off: pl.BlockSpec((T, HEAD_PAIR), lambda j: (0, off + j))
    return pl.pallas_call(
        body, name="attn_fwd", grid=(4,),
        in_specs=[col(0), col(4), col(8)], out_specs=[col(0), col(0)],
        out_shape=[jax.ShapeDtypeStruct((T, ATTN_W), F32)] * 2,
        scratch_shapes=[pltpu.VMEM((T, HEAD_PAIR), F32), pltpu.VMEM((3, 2 * ATTN_BLK, 2 * ATTN_BLK), F32)],
        compiler_params=_cp("arbitrary"),
    )(qkv, qkv, qkv)


def _attn_bwd(qkv, o, lse, do, token=None):
    T = qkv.shape[0]
    per_chain = T // ATTN_BLK // ATTN_CHAINS
    extra = [] if token is None else [token]

    def body(q_ref, k_ref, v_ref, o_ref, lse_ref, do_ref, *rest):
        outs = rest[len(extra):len(extra) + 3]
        dq_ref, dk_ref, dv_ref, dkb_ref, dvb_ref, bias_ref = rest[len(extra) + 3:]
        first = _attn_masks(bias_ref)
        dq_ref[...] = jnp.zeros_like(dq_ref)
        dk_ref[...] = jnp.zeros_like(dk_ref)
        dv_ref[...] = jnp.zeros_like(dv_ref)

        def grads(rows, kw, vw, which):
            q2 = _two_heads(q_ref[rows, :] * 0.125, first).astype(BF16)
            lse_b = lse_ref[rows, :]
            dob = do_ref[rows, :]
            prod = dob * o_ref[rows, :]
            old = dq_ref[rows, :]
            lse2 = jnp.concatenate(
                [jnp.max(jnp.where(first, lse_b, NEG), axis=-1, keepdims=True),
                 jnp.max(jnp.where(first, NEG, lse_b), axis=-1, keepdims=True)], axis=0)
            p = jnp.exp(_dot_nt(q2, kw) + (bias_ref[which] - lse2))
            delta = jnp.concatenate(
                [jnp.sum(jnp.where(first, prod, 0.0), axis=-1, keepdims=True),
                 jnp.sum(jnp.where(first, 0.0, prod), axis=-1, keepdims=True)], axis=0)
            do2 = _two_heads(dob, first).astype(BF16)
            ds = (p * (_dot_nt(do2, vw) - delta)).astype(BF16)
            dq2 = _dot(ds, kw) * 0.125
            return (old + jnp.where(first, dq2[:ATTN_BLK], dq2[ATTN_BLK:]), _dot_tn(ds, q2),
                    _dot_tn(p.astype(BF16), do2))

        def block(idx):
            rows, keys, which = _attn_rows(idx, T // ATTN_BLK, 1)
            old = dk_ref[keys, :], dv_ref[keys, :]
            dq, ck, cv = grads(rows, k_ref[keys, :].astype(BF16), v_ref[keys, :].astype(BF16), which)
            return rows, keys, dq, old[0] + ck, old[1] + cv

        def step(i, carry):
            done = [block(i + ch * per_chain) for ch in range(ATTN_CHAINS)]
            for rows, keys, dq, dk, dv in done:
                dq_ref[rows, :] = dq
                dk_ref[keys, :] = dk
                dv_ref[keys, :] = dv
            return carry

        lax.fori_loop(0, per_chain, step, 0)

        for d in DILATIONS[1:]:
            nb = T // d // ATTN_BLK

            def block(idx, kept, d=d, nb=nb):
                r, n = idx // nb, idx % nb
                rows = pl.ds(r + d * ATTN_BLK * n, ATTN_BLK, stride=d)
                before = pl.ds(r + d * ATTN_BLK * jnp.maximum(n - 1, 0), ATTN_BLK, stride=d)
                k_prev, v_prev, dk_prev, dv_prev = kept
                k_own, v_own = k_ref[rows, :].astype(BF16), v_ref[rows, :].astype(BF16)
                dq, ck, cv = grads(rows, jnp.concatenate([k_prev, k_own], axis=0),
                                   jnp.concatenate([v_prev, v_own], axis=0), jnp.where(n > 0, 1, 2))
                stores = (rows, before, dq, dk_prev + ck[:ATTN_BLK], dv_prev + cv[:ATTN_BLK], ck[ATTN_BLK:], cv[ATTN_BLK:])
                return stores, (k_own, v_own, ck[ATTN_BLK:], cv[ATTN_BLK:])

            def step(i, kept, block=block):
                done = [block(i + ch * per_chain, kept[ch]) for ch in range(ATTN_CHAINS)]
                for (rows, before, dq, dk_done, dv_done, dk_own, dv_own), _ in done:
                    dq_ref[rows, :] = dq
                    dkb_ref[before, :] = dk_done
                    dvb_ref[before, :] = dv_done
                    dkb_ref[rows, :] = dk_own
                    dvb_ref[rows, :] = dv_own
                return tuple(k for _, k in done)

            zero = jnp.zeros((ATTN_BLK, HEAD_PAIR), F32)
            lax.fori_loop(0, per_chain, step, ((zero.astype(BF16), zero.astype(BF16), zero, zero),) * ATTN_CHAINS)

            def add(i, carry):
                rows = pl.ds(pl.multiple_of(i * SUPER, SUPER), SUPER)
                dk_ref[rows, :] += dkb_ref[rows, :]
                dv_ref[rows, :] += dvb_ref[rows, :]
                return carry

            lax.fori_loop(0, T // SUPER, add, 0)

        def emit(i, carry):
            rows = pl.ds(pl.multiple_of(i * SUPER, SUPER), SUPER)
            for out, acc in zip(outs, (dq_ref, dk_ref, dv_ref)):
                out[rows, :] = acc[rows, :].astype(BF16)
            return carry

        lax.fori_loop(0, T // SUPER, emit, 0)

    col = lambda off: pl.BlockSpec((T, HEAD_PAIR), lambda j: (0, off + j))
    return pl.pallas_call(
        body, name="attn_bwd", grid=(4,),
        in_specs=[col(0), col(4), col(8), col(0), col(0), col(0)] + [_full(t.shape) for t in extra],
        out_specs=[col(0)] * 3,
        out_shape=[jax.ShapeDtypeStruct((T, ATTN_W), BF16)] * 3,
        scratch_shapes=[pltpu.VMEM((T, HEAD_PAIR), F32)] * 5 + [pltpu.VMEM((3, 2 * ATTN_BLK, 2 * ATTN_BLK), F32)],
        compiler_params=_cp("arbitrary"),
    )(qkv, qkv, qkv, o, lse, do, *extra)


def _chunk_ids():
    row = lax.broadcasted_iota(jnp.int32, (SUPER, HGRN_DIM), 0)
    r2 = lax.broadcasted_iota(jnp.int32, (SUPER, SUPER), 0)
    c2 = lax.broadcasted_iota(jnp.int32, (SUPER, SUPER), 1)
    amask = ((r2 // HGRN_CHUNK) == (c2 // HGRN_CHUNK)) & (c2 <= r2)
    return row % HGRN_CHUNK, row // HGRN_CHUNK, amask


def _cumsum_chunk(x, rmod):
    s = 1
    while s < HGRN_CHUNK:
        x = x + jnp.where(rmod >= s, pltpu.roll(x, s, 0), 0.0)
        s *= 2
    return x


def _suffix_sum_chunk(x, rmod):
    s = 1
    while s < HGRN_CHUNK:
        x = x + jnp.where(rmod < HGRN_CHUNK - s, pltpu.roll(x, SUPER - s, 0), 0.0)
        s *= 2
    return x


def _chunk_rows(vs, cid):
    out = vs[-1]
    for c in reversed(range(len(vs) - 1)):
        out = jnp.where(cid == c, vs[c], out)
    return out


def _expand(x, cid):
    return jnp.concatenate([jnp.where(cid == c, x, 0.0) for c in range(SUPER // HGRN_CHUNK)], axis=1)


def _hgrn_gates(q, f, lbv, rmod, cid, tmp):
    sq = _sigmoid(q)
    sg = _sigmoid(f)
    forget = lbv + (1.0 - lbv) * sg
    key = 1.0 - forget
    b = _cumsum_chunk(jnp.log(forget), rmod)
    tmp[...] = b
    bends = [tmp[c * HGRN_CHUNK + HGRN_CHUNK - 1:(c + 1) * HGRN_CHUNK, :] for c in range(SUPER // HGRN_CHUNK)]
    eb = jnp.exp(b)
    enb = jnp.exp(-b)
    ebe = jnp.exp(_chunk_rows(bends, cid) - b)
    return sq, sg, forget, key, eb, enb, ebe, q * sq * eb, key * enb, key * ebe, [jnp.exp(v) for v in bends]


def _hgrn_fwd(hg, lb):
    T = hg.shape[0]
    nsc = T // SUPER
    NC = SUPER // HGRN_CHUNK

    def body(q_ref, f_ref, i_ref, lb_ref, o_ref, st_ref, state, tmp):
        rmod, cid, amask = _chunk_ids()
        state[...] = jnp.zeros_like(state)
        lbv = lb_ref[...]

        def local(sc, u):
            rows = pl.ds(pl.multiple_of(sc * SUPER, SUPER), SUPER)
            iv = i_ref[rows, :].astype(BF16)
            qd, ki, ke, dec = _hgrn_gates(q_ref[rows, :], f_ref[rows, :], lbv, rmod, cid, tmp.at[u])[-4:]
            a = jnp.where(amask, _dot_nt(qd.astype(BF16), ki.astype(BF16)), 0.0)
            return rows, qd, dec, _dot(a.astype(BF16), iv), _dot_tn(iv, _expand(ke, cid).astype(BF16))

        def step(i, carry):
            parts = [local(i * HGRN_SIDE + u, u) for u in range(HGRN_SIDE)]
            st = state[...]
            entering = []
            for u, (_, _, dec, _, ut) in enumerate(parts):
                st_ref[0, i * HGRN_SIDE + u] = st
                sts = []
                for c in range(NC):
                    sts.append(st)
                    st = st * dec[c] + ut[:, c * HGRN_DIM:(c + 1) * HGRN_DIM]
                entering.append(jnp.concatenate(sts, axis=1).astype(BF16))
            state[...] = st
            for (rows, qd, _, o, _), sts in zip(parts, entering):
                o_ref[rows, :] = o + _dot_nt(_expand(qd, cid).astype(BF16), sts)
            return carry

        lax.fori_loop(0, nsc // HGRN_SIDE, step, 0)

    col = lambda off: pl.BlockSpec((T, HGRN_DIM), lambda h: (0, off + h))
    return pl.pallas_call(
        body, name="hgrn_fwd", grid=(HGRN_HEADS,),
        in_specs=[col(0), col(4), col(8), pl.BlockSpec((1, HGRN_DIM), lambda h: (0, h))],
        out_specs=[pl.BlockSpec((T, HGRN_DIM), lambda h: (0, h)),
                   pl.BlockSpec((1, nsc, HGRN_DIM, HGRN_DIM), lambda h: (h, 0, 0, 0))],
        out_shape=[jax.ShapeDtypeStruct((T, HGRN_W), F32),
                   jax.ShapeDtypeStruct((HGRN_HEADS, nsc, HGRN_DIM, HGRN_DIM), F32)],
        scratch_shapes=[pltpu.VMEM((HGRN_DIM, HGRN_DIM), F32), pltpu.VMEM((HGRN_SIDE, SUPER, HGRN_DIM), F32)],
        compiler_params=_cp("arbitrary"),
    )(hg, hg, hg, lb)


def _hgrn_bwd(hg, lb, states, do):
    T = hg.shape[0]
    nsc = T // SUPER
    NC = SUPER // HGRN_CHUNK

    def body(q_ref, f_ref, i_ref, lb_ref, st_ref, do_ref, dq_ref, df_ref, di_ref, dlb_ref, dstate, tmp):
        rmod, cid, amask = _chunk_ids()
        dstate[...] = jnp.zeros_like(dstate)
        dlb_ref[...] = jnp.zeros_like(dlb_ref)
        lbv = lb_ref[...]

        def local(sc, u):
            rows = pl.ds(pl.multiple_of(sc * SUPER, SUPER), SUPER)
            q = q_ref[rows, :]
            ivf = i_ref[rows, :]
            iv = ivf.astype(BF16)
            dof = do_ref[rows, :]
            dob = dof.astype(BF16)
            sq, sg, forget, key, eb, enb, ebe, qd, ki, ke, dec = _hgrn_gates(q, f_ref[rows, :], lbv, rmod, cid,
                                                                            tmp.at[u])
            qdb, kib = qd.astype(BF16), ki.astype(BF16)
            keexp = _expand(ke, cid).astype(BF16)
            a = jnp.where(amask, _dot_nt(qdb, kib), 0.0).astype(BF16)
            ut = _dot_tn(iv, keexp)
            st = st_ref[0, sc]
            sts = []
            for c in range(NC):
                sts.append(st)
                st = st * dec[c] + ut[:, c * HGRN_DIM:(c + 1) * HGRN_DIM]
            gt = _dot_tn(dob, _expand(qd, cid).astype(BF16))
            da = jnp.where(amask, _dot_nt(dob, iv), 0.0).astype(BF16)
            ststack = jnp.concatenate(sts, axis=0).astype(BF16)
            return dict(rows=rows, q=q, sq=sq, sg=sg, forget=forget, eb=eb, enb=enb, ebe=ebe, qd=qd, ki=ki, ke=ke,
                        dec=dec, sts=sts, gt=gt, keexp=keexp, ivexp=_expand(ivf, cid).astype(BF16),
                        div=_dot_tn(a, dob), dki=_dot_tn(da, qdb),
                        dqd=_dot(da, kib) + _dot(_expand(dof, cid).astype(BF16), ststack))

        def finish(p, nxt, ddec):
            ncat = jnp.concatenate(nxt, axis=1).astype(BF16)
            nstack = jnp.concatenate(nxt, axis=0).astype(BF16)
            dke = _dot(p["ivexp"], nstack)
            dkk = dke * p["ke"]
            dkey = p["dki"] * p["enb"] + dke * p["ebe"]
            db = p["dqd"] * p["qd"] - p["dki"] * p["ki"] - dkk
            dbends = [_colsum(jnp.where(cid == c, dkk, 0.0)) + ddec[c] * p["dec"][c] for c in range(NC)]
            dforget = (_suffix_sum_chunk(db, rmod) + _chunk_rows(dbends, cid)) / p["forget"] - dkey
            sg, sq, q = p["sg"], p["sq"], p["q"]
            df_ref[p["rows"], :] = (dforget * (1.0 - lbv) * sg * (1.0 - sg)).astype(BF16)
            dq_ref[p["rows"], :] = (p["dqd"] * p["eb"] * (sq * (1.0 + q * (1.0 - sq)))).astype(BF16)
            di_ref[p["rows"], :] = (p["div"] + _dot_nt(p["keexp"], ncat)).astype(BF16)
            return _colsum(dforget * (1.0 - sg))

        def step(i, carry):
            parts = [local(nsc - 1 - (i * HGRN_SIDE + u), u) for u in range(HGRN_SIDE)]
            dst = dstate[...]
            chained = []
            for p in parts:
                nxt = [None] * NC
                ddec = [None] * NC
                for c in reversed(range(NC)):
                    nxt[c] = dst
                    ddec[c] = _colsum(dst * p["sts"][c])
                    dst = dst * p["dec"][c] + p["gt"][:, c * HGRN_DIM:(c + 1) * HGRN_DIM]
                chained.append((nxt, ddec))
            dstate[...] = dst
            dlb = dlb_ref[...]
            for p, (nxt, ddec) in zip(parts, chained):
                dlb = dlb + finish(p, nxt, ddec)
            dlb_ref[...] = dlb
            return carry

        lax.fori_loop(0, nsc // HGRN_SIDE, step, 0)

    col = lambda off: pl.BlockSpec((T, HGRN_DIM), lambda h: (0, off + h))
    own = pl.BlockSpec((T, HGRN_DIM), lambda h: (0, h))
    vec = pl.BlockSpec((1, HGRN_DIM), lambda h: (0, h))
    return pl.pallas_call(
        body, name="hgrn_bwd", grid=(HGRN_HEADS,),
        in_specs=[col(0), col(4), col(8), vec,
                  pl.BlockSpec((1, nsc, HGRN_DIM, HGRN_DIM), lambda h: (h, 0, 0, 0)), own],
        out_specs=[own, own, own, vec],
        out_shape=[jax.ShapeDtypeStruct((T, HGRN_W), BF16)] * 3 + [jax.ShapeDtypeStruct((1, HGRN_W), F32)],
        scratch_shapes=[pltpu.VMEM((HGRN_DIM, HGRN_DIM), F32), pltpu.VMEM((HGRN_SIDE, SUPER, HGRN_DIM), F32)],
        compiler_params=_cp("arbitrary"),
    )(hg, hg, hg, lb, states, do)


def _rec_heads(rec, gate, g_h):
    rr = jnp.concatenate(
        [jnp.broadcast_to(_rms(rec[:, h * HGRN_DIM:(h + 1) * HGRN_DIM], HGRN_DIM), (rec.shape[0], HGRN_DIM))
         for h in range(HGRN_HEADS)], axis=1)
    rn = rec * rr
    sg = _sigmoid(gate)
    return rr, rn, sg


def _mix_out(attn_o, rec_o, hg, x, g_a, g_h, w_out, tm=512):
    T = x.shape[0]

    def body(a_ref, r_ref, gt_ref, x_ref, ga_ref, gh_ref, w_ref, h1_ref, mixed_ref):
        a = a_ref[...]
        an = a * _rms(a, ATTN_W) * ga_ref[...]
        gate = gt_ref[...]
        _, rn, sg = _rec_heads(r_ref[...], gate, gh_ref[...])
        mixed = jnp.concatenate([an, rn * gh_ref[...] * (gate * sg)], axis=1).astype(BF16)
        mixed_ref[...] = mixed
        h1_ref[...] = x_ref[...] + _dot(mixed, w_ref[...])

    row = lambda w: pl.BlockSpec((tm, w), lambda i: (i, 0))
    return pl.pallas_call(
        body, name="mix_out", grid=(T // tm,),
        in_specs=[row(ATTN_W), row(HGRN_W), pl.BlockSpec((tm, HGRN_W), lambda i: (i, 3)), row(D_MODEL),
                  _full((1, ATTN_W)), _full((1, HGRN_W)), _once((D_MODEL, D_MODEL))],
        out_specs=[row(D_MODEL), row(D_MODEL)],
        out_shape=[jax.ShapeDtypeStruct((T, D_MODEL), F32), jax.ShapeDtypeStruct((T, D_MODEL), BF16)],
        compiler_params=_cp("arbitrary"),
    )(attn_o, rec_o, hg, x, g_a, g_h, w_out)


_INV_SQRT2 = 1.0 / math.sqrt(2.0)
_INV_SQRT2PI = 1.0 / math.sqrt(2.0 * math.pi)


def _gelu(x):
    return 0.5 * x * (1.0 + lax.erf(x * _INV_SQRT2))


def _gelu_grad(x):
    return 0.5 * (1.0 + lax.erf(x * _INV_SQRT2)) + x * jnp.exp(-0.5 * x * x) * _INV_SQRT2PI


def _shift_down(g, prev, rowid):
    p1 = _row(prev, prev.shape[0] - 1)
    p2 = _row(prev, prev.shape[0] - 2)
    s1 = jnp.where(rowid == 0, p1, pltpu.roll(g, 1, 0))
    s2 = jnp.where(rowid == 0, p2, jnp.where(rowid == 1, p1, pltpu.roll(g, 2, 0)))
    return s1, s2


def _mlp_fwd(h1, g2, w_up4, conv_w, conv_b, w_down, gf, tgt, tm=256):
    T = h1.shape[0]

    def body(h_ref, g2_ref, wu_hbm, cw_ref, cb_ref, wd_ref, gf_ref, t_ref,
             u_ref, gate_ref, val_ref, conv_ref, act_ref, dh_ref, loss_ref, dgf_ref, carry, wu_ref, sem):
        i = pl.program_id(0)

        @pl.when(i == 0)
        def _():
            carry[...] = jnp.zeros_like(carry)
            loss_ref[...] = jnp.zeros_like(loss_ref)
            dgf_ref[...] = jnp.zeros_like(dgf_ref)
            _load_side_by_side(wu_hbm, wu_ref, sem)

        h = h_ref[...]
        u = (h * _rms(h, D_MODEL) * g2_ref[...]).astype(BF16)
        u_ref[...] = u
        y2 = jnp.zeros((tm, D_MODEL), F32)
        for lo, hi in FF_CHUNKS:
            cols = slice(lo, hi)
            rowid = lax.broadcasted_iota(jnp.int32, (tm, hi - lo), 0)
            gb = _dot(u, wu_ref[:, lo:hi]).astype(BF16)
            vb = _dot(u, wu_ref[:, D_FF + lo:D_FF + hi]).astype(BF16)
            gate_ref[:, cols] = gb
            val_ref[:, cols] = vb
            g = gb.astype(F32)
            s1, s2 = _shift_down(g, carry[:, cols], rowid)
            carry[:, cols] = g[tm - 8:, :]
            conv = cb_ref[:, cols] + cw_ref[0:1, cols] * s2 + cw_ref[1:2, cols] * s1 + cw_ref[2:3, cols] * g
            act = (_gelu(conv) * vb.astype(F32)).astype(BF16)
            conv_ref[:, cols] = conv.astype(BF16)
            act_ref[:, cols] = act
            y2 = y2 + _dot(act, wd_ref[cols, :])
        h2 = h + y2
        rf = _rms(h2, D_MODEL)
        n = h2 * rf
        gfv = gf_ref[...]
        e = n * gfv - t_ref[...]
        loss_ref[...] += jnp.sum(e * e) * (0.5 / D_MODEL)
        dy = e * (1.0 / D_MODEL)
        dgf_ref[...] += _colsum(dy * n)
        dh_ref[...] = _rms_bwd(dy * gfv, n, rf, D_MODEL)

    row = lambda w: pl.BlockSpec((tm, w), lambda i: (i, 0))
    return pl.pallas_call(
        body, name="mlp_fwd", grid=(T // tm,),
        in_specs=[row(D_MODEL), _full((1, D_MODEL)), ANY, _full((3, D_FF)),
                  _full((1, D_FF)), _once((D_FF, D_MODEL)), _full((1, D_MODEL)), row(D_MODEL)],
        out_specs=[row(D_MODEL), row(D_FF), row(D_FF), row(D_FF), row(D_FF), row(D_MODEL), _full((1, 128)),
                   _full((1, D_MODEL))],
        out_shape=[jax.ShapeDtypeStruct((T, D_MODEL), BF16)] + [jax.ShapeDtypeStruct((T, D_FF), BF16)] * 4
        + [jax.ShapeDtypeStruct((T, D_MODEL), F32),
                   jax.ShapeDtypeStruct((1, 128), F32), jax.ShapeDtypeStruct((1, D_MODEL), F32)],
        scratch_shapes=[pltpu.VMEM((8, D_FF), F32), pltpu.VMEM((D_MODEL, 2 * D_FF), BF16),
                        pltpu.SemaphoreType.DMA((N_CHIPS,))],
        compiler_params=_cp("arbitrary"),
    )(h1, g2, w_up4, conv_w, conv_b, w_down, gf, tgt)


def _mlp_bwd(dh2, gate, val, conv, conv_w, w_down, tm=256):
    T = dh2.shape[0]
    nb = T // tm
    half = D_FF // 2

    def body(dh_ref, gate_ref, val_ref, conv_ref, cw_ref, wd_ref, dgv_ref, dcw_ref, dcb_ref, carry):
        @pl.when(pl.program_id(0) == 0)
        def _():
            carry[...] = jnp.zeros_like(carry)
            dcw_ref[...] = jnp.zeros_like(dcw_ref)
            dcb_ref[...] = jnp.zeros_like(dcb_ref)

        dhb = dh_ref[...].astype(BF16)
        rowid = lax.broadcasted_iota(jnp.int32, (tm, half), 0)
        for c in range(2):
            cols = slice(c * half, (c + 1) * half)
            g = gate_ref[:, cols].astype(F32)
            v = val_ref[:, cols].astype(F32)
            cv = conv_ref[:, cols].astype(F32)
            dact = _dot_nt(dhb, wd_ref[cols, :])
            dconv = dact * v * _gelu_grad(cv)
            nxt = carry[:, cols]
            n0, n1 = _row(nxt, 0), _row(nxt, 1)
            u1 = jnp.where(rowid == tm - 1, n0, pltpu.roll(dconv, tm - 1, 0))
            u2 = jnp.where(rowid == tm - 1, n1, jnp.where(rowid == tm - 2, n0, pltpu.roll(dconv, tm - 2, 0)))
            carry[:, cols] = dconv[0:8, :]
            dcb_ref[:, cols] += _colsum(dconv)
            dcw_ref[0:1, cols] += _colsum(u2 * g)
            dcw_ref[1:2, cols] += _colsum(u1 * g)
            dcw_ref[2:3, cols] += _colsum(dconv * g)
            dgate = cw_ref[2:3, cols] * dconv + cw_ref[1:2, cols] * u1 + cw_ref[0:1, cols] * u2
            dgv_ref[:, cols] = dgate.astype(BF16)
            dgv_ref[:, D_FF + c * half:D_FF + (c + 1) * half] = (dact * _gelu(cv)).astype(BF16)

    rev = lambda w: pl.BlockSpec((tm, w), lambda i: (nb - 1 - i, 0))
    return pl.pallas_call(
        body, name="mlp_bwd", grid=(nb,),
        in_specs=[rev(D_MODEL), rev(D_FF), rev(D_FF), rev(D_FF), _full((3, D_FF)), _once((D_FF, D_MODEL))],
        out_specs=[rev(2 * D_FF), _full((3, D_FF)), _full((1, D_FF))],
        out_shape=[jax.ShapeDtypeStruct((T, 2 * D_FF), BF16), jax.ShapeDtypeStruct((3, D_FF), F32),
                   jax.ShapeDtypeStruct((1, D_FF), F32)],
        scratch_shapes=[pltpu.VMEM((8, D_FF), F32)],
        compiler_params=_cp("arbitrary"),
    )(dh2, gate, val, conv, conv_w, w_down)


def _up_out_bwd(dgv, w_up4, h1, g2, dh2, w_out, attn_o, rec_o, hg, g_a, g_h, tm=256):
    T = h1.shape[0]

    def body(dgv_ref, wu_hbm, h_ref, g2_ref, dh2_ref, wo_ref, a_ref, r_ref, gt_ref, ga_ref, gh_ref,
             dh1_ref, dg2_ref, da_ref, dr_ref, dgt_ref, dga_ref, dgh_ref, wu_ref, sem):
        @pl.when(pl.program_id(0) == 0)
        def _():
            dg2_ref[...] = jnp.zeros_like(dg2_ref)
            dga_ref[...] = jnp.zeros_like(dga_ref)
            dgh_ref[...] = jnp.zeros_like(dgh_ref)
            _load_side_by_side(wu_hbm, wu_ref, sem)

        du = _dot_nt(dgv_ref[...], wu_ref[...])
        h = h_ref[...]
        r = _rms(h, D_MODEL)
        n = h * r
        dg2_ref[...] += _colsum(du * n)
        dh1 = dh2_ref[...] + _rms_bwd(du * g2_ref[...], n, r, D_MODEL)
        dh1_ref[...] = dh1
        dmix = _dot_nt(dh1.astype(BF16), wo_ref[...])
        dan = dmix[:, :ATTN_W]
        a = a_ref[...]
        ra = _rms(a, ATTN_W)
        na = a * ra
        dga_ref[...] += _colsum(dan * na)
        da_ref[...] = _rms_bwd(dan * ga_ref[...], na, ra, ATTN_W)
        dmr = dmix[:, ATTN_W:]
        gate = gt_ref[...]
        ghv = gh_ref[...]
        rr, rn, sg = _rec_heads(r_ref[...], gate, ghv)
        dgt_ref[...] = (dmr * rn * ghv * (sg * (1.0 + gate * (1.0 - sg)))).astype(BF16)
        drecn = dmr * (gate * sg)
        dgh_ref[...] += _colsum(drecn * rn)
        drn = drecn * ghv
        prod = drn * rn
        mean = jnp.concatenate(
            [jnp.broadcast_to(jnp.sum(prod[:, h_ * HGRN_DIM:(h_ + 1) * HGRN_DIM], axis=-1, keepdims=True),
                              (tm, HGRN_DIM)) for h_ in range(HGRN_HEADS)], axis=1) * (1.0 / HGRN_DIM)
        dr_ref[...] = rr * (drn - rn * mean)

    row = lambda w: pl.BlockSpec((tm, w), lambda i: (i, 0))
    return pl.pallas_call(
        body, name="up_out_bwd", grid=(T // tm,),
        in_specs=[row(2 * D_FF), ANY, row(D_MODEL), _full((1, D_MODEL)),
                  row(D_MODEL), _once((D_MODEL, D_MODEL)), row(ATTN_W), row(HGRN_W),
                  pl.BlockSpec((tm, HGRN_W), lambda i: (i, 3)), _full((1, ATTN_W)), _full((1, HGRN_W))],
        out_specs=[row(D_MODEL), _full((1, D_MODEL)), row(ATTN_W), row(HGRN_W), row(HGRN_W),
                   _full((1, ATTN_W)), _full((1, HGRN_W))],
        out_shape=[jax.ShapeDtypeStruct((T, D_MODEL), F32), jax.ShapeDtypeStruct((1, D_MODEL), F32),
                   jax.ShapeDtypeStruct((T, ATTN_W), F32), jax.ShapeDtypeStruct((T, HGRN_W), F32),
                   jax.ShapeDtypeStruct((T, HGRN_W), BF16), jax.ShapeDtypeStruct((1, ATTN_W), F32),
                   jax.ShapeDtypeStruct((1, HGRN_W), F32)],
        scratch_shapes=[pltpu.VMEM((D_MODEL, 2 * D_FF), BF16), pltpu.SemaphoreType.DMA((N_CHIPS,))],
        compiler_params=_cp("arbitrary"),
    )(dgv, w_up4, h1, g2, dh2, w_out, attn_o, rec_o, hg, g_a, g_h)


def _in_bwd(dqkv, dhg, w_in4, x, g1, dh1, tm=512):
    T = x.shape[0]

    def body(*refs):
        parts = refs[:7]
        w_hbm, x_ref, g_ref, dh1_ref, dp_ref, dx_ref, dg_ref, w_full, sem = refs[7:]

        @pl.when(pl.program_id(0) == 0)
        def _():
            dg_ref[...] = jnp.zeros_like(dg_ref)
            _load_side_by_side(w_hbm, w_full, sem)

        dp = jnp.concatenate([p[...] for p in parts], axis=1)
        dp_ref[...] = dp
        du = _dot_nt(dp, w_full[...])
        xv = x_ref[...]
        r = _rms(xv, D_MODEL)
        n = xv * r
        dg_ref[...] += _colsum(du * n)
        dx_ref[...] = dh1_ref[...] + _rms_bwd(du * g_ref[...], n, r, D_MODEL)

    row = lambda w: pl.BlockSpec((tm, w), lambda i: (i, 0))
    return pl.pallas_call(
        body, name="in_bwd", grid=(T // tm,),
        in_specs=[row(ATTN_W)] * 7 + [ANY, row(D_MODEL), _full((1, D_MODEL)), row(D_MODEL)],
        out_specs=[row(IN_TOTAL), row(D_MODEL), _full((1, D_MODEL))],
        out_shape=[jax.ShapeDtypeStruct((T, IN_TOTAL), BF16), jax.ShapeDtypeStruct((T, D_MODEL), F32),
                   jax.ShapeDtypeStruct((1, D_MODEL), F32)],
        scratch_shapes=[pltpu.VMEM((D_MODEL, IN_TOTAL), BF16), pltpu.SemaphoreType.DMA((N_CHIPS,))],
        compiler_params=_cp("arbitrary"),
    )(*dqkv, *dhg, w_in4, x, g1, dh1)


def _dw(a, b, kb, nb_, name, tk=1024, side=1):
    T, K = a.shape
    N = b.shape[1]
    nk, nn, nt = K // kb, N // (nb_ * side), T // tk

    def body(a_ref, b_ref, o_ref, acc):
        t = pl.program_id(2)

        @pl.when(t == 0)
        def _():
            acc[...] = jnp.zeros_like(acc)

        acc[...] += _dot_tn(a_ref[...], b_ref[...].astype(BF16))

        @pl.when(t == nt - 1)
        def _():
            for s in range(side):
                o_ref[s] = acc[:, s * nb_:(s + 1) * nb_].astype(BF16)

    return pl.pallas_call(
        body, name=name, grid=(nk, nn, nt),
        in_specs=[pl.BlockSpec((tk, kb), lambda i, j, t: (t, i)),
                  pl.BlockSpec((tk, nb_ * side), lambda i, j, t: (t, j))],
        out_specs=pl.BlockSpec((side, kb, nb_), lambda i, j, t: (i * nn + j, 0, 0)),
        out_shape=jax.ShapeDtypeStruct((nk * nn * side, kb, nb_), BF16),
        scratch_shapes=[pltpu.VMEM((kb, nb_ * side), F32)],
        compiler_params=_cp("arbitrary", "arbitrary", "arbitrary"),
    )(a, b)


def _local_step(x, tgt, g1, w_in4, g_a, g_h, lb, w_out, g2, w_up4, conv_w, conv_b, w_down, gf):
    a = _step_mixers(x, g1, w_in4, lb)
    b = _step_channel(a, x, tgt, g_a, g_h, w_out, g2, w_up4, conv_w, conv_b, w_down, gf)
    c = _step_mixers_bwd(a, b, x, g1, w_in4, lb)
    small = dict(g1=c["dg1"], g_a=b["dga"], g_h=b["dgh"], lb=c["dlb"], g2=b["dg2"], conv_w=b["dcw"], conv_b=b["dcb"],
                 gf=b["dgf"])
    return b["loss"], c["dx"], small, dict(w_in=c["dw_in"], w_out=b["dw_out"], w_up=b["dw_up"], w_down=b["dw_down"])


def _step_mixers(x, g1, w_in4, lb):
    u1, qkv, hg = _in_proj(x, g1, w_in4)
    attn_o, lse = _attn_fwd(qkv)
    rec_o, states = _hgrn_fwd(hg, lb)
    return dict(u1=u1, qkv=qkv, hg=hg, attn_o=attn_o, lse=lse, rec_o=rec_o, states=states)


def _step_channel(a, x, tgt, g_a, g_h, w_out, g2, w_up4, conv_w, conv_b, w_down, gf):
    h1, mixed = _mix_out(a["attn_o"], a["rec_o"], a["hg"], x, g_a, g_h, w_out)
    u2, gate, val, conv, act, dh2, loss, dgf = _mlp_fwd(h1, g2, w_up4, conv_w, conv_b, w_down, gf, tgt)
    dgv, dcw, dcb = _mlp_bwd(dh2, gate, val, conv, conv_w, w_down)
    dw_down = _dw(act, dh2, D_FF // 2, D_MODEL, "dw_down").reshape(N_CHIPS, D_FF // N_CHIPS, D_MODEL)
    dh1, dg2, da, dr, dgt, dga, dgh = _up_out_bwd(dgv, w_up4, h1, g2, dh2, w_out, a["attn_o"], a["rec_o"], a["hg"],
                                                  g_a, g_h)
    dw_up = _dw(u2, dgv, D_MODEL, UP_SHARD, "dw_up", side=2)
    dw_out = _dw(mixed, dh1, D_MODEL, D_MODEL, "dw_out").reshape(N_CHIPS, D_MODEL // N_CHIPS, D_MODEL)
    return dict(loss=loss, dgf=dgf, dcw=dcw, dcb=dcb, dg2=dg2, dga=dga, dgh=dgh, dh1=dh1, da=da, dr=dr, dgt=dgt,
                dw_down=dw_down, dw_up=dw_up, dw_out=dw_out)


def _step_mixers_bwd(a, b, x, g1, w_in4, lb, dqkv=None):
    if dqkv is None:
        dqkv = _attn_bwd(a["qkv"], a["attn_o"], a["lse"], b["da"])
    dhq, dhf, dhi, dlb = _hgrn_bwd(a["hg"], lb, a["states"], b["dr"])
    dproj, dx, dg1 = _in_bwd(dqkv, [dhq, dhf, dhi, b["dgt"]], w_in4, x, g1, b["dh1"])
    dw_in = _dw(a["u1"], dproj, D_MODEL, IN_SHARD, "dw_in", side=2)
    return dict(dx=dx, dg1=dg1, dlb=dlb, dw_in=dw_in)


BIG = ("w_in", "w_out", "w_up", "w_down")
ANY = pl.BlockSpec(memory_space=pl.ANY)


def _place():
    x, y, c = lax.axis_index("x"), lax.axis_index("y"), lax.axis_index("c")
    chips = [(1 - x, y), (x, 1 - y), (1 - x, 1 - y)]
    return x, y, c, chips


def _remote(src, dst, send_sems, recv_sems, k, to):
    return pltpu.make_async_remote_copy(src_ref=src, dst_ref=dst, send_sem=send_sems.at[k], recv_sem=recv_sems.at[k],
                                        device_id=to, device_id_type=MESH)


def _gather_weights(shards, conv_w):
    n = len(shards)
    halves = [s.shape[0] // 2 for s in shards]

    def body(*refs):
        ins, cw, outs, ocw = refs[:n], refs[n], refs[n + 1:2 * n + 1], refs[2 * n + 1]
        send_sems, recv_sems = refs[2 * n + 2:]
        x, y, c, chips = _place()
        me, sibling = 2 * x + y, (x, y, 1 - c)

        def part(w, chip, half):
            return outs[w].at[chip, pl.ds(half * halves[w], halves[w]), :]

        sent = []
        for j, chip in enumerate(chips):
            for w in range(n):
                sent.append(_remote(ins[w].at[pl.ds(c * halves[w], halves[w]), :], part(w, me, c),
                                    send_sems, recv_sems, w * 3 + j, (*chip, c)))
            sent.append(_remote(cw, ocw.at[me], send_sems, recv_sems, 6 * n + j, (*chip, c)))
        for cp in sent:
            cp.start()
        for j, chip in enumerate(chips):
            kj = 2 * chip[0] + chip[1]
            for w in range(n):
                _remote(part(w, kj, c), part(w, kj, c), send_sems, recv_sems, w * 3 + j, (*chip, c)).wait_recv()
                fwd = _remote(part(w, kj, c), part(w, kj, c), send_sems, recv_sems, 3 * n + w * 3 + j, sibling)
                fwd.start()
                sent.append(fwd)
        for j, chip in enumerate(chips):
            kj = 2 * chip[0] + chip[1]
            for w in range(n):
                _remote(part(w, kj, 1 - c), part(w, kj, 1 - c), send_sems, recv_sems, 3 * n + w * 3 + j,
                        sibling).wait_recv()
            _remote(cw, ocw.at[kj], send_sems, recv_sems, 6 * n + j, (*chip, c)).wait_recv()
        for cp in sent:
            cp.wait_send()

    n_sem = 6 * n + 3
    outs = pl.pallas_call(
        body, name="gather_weights",
        in_specs=[ANY] * (n + 1), out_specs=[ANY] * (n + 1),
        out_shape=[jax.ShapeDtypeStruct((N_CHIPS,) + s.shape, s.dtype) for s in shards]
        + [jax.ShapeDtypeStruct((N_CHIPS,) + conv_w.shape, conv_w.dtype)],
        scratch_shapes=[pltpu.SemaphoreType.DMA((n_sem,)), pltpu.SemaphoreType.DMA((n_sem,))],
    )(*shards, conv_w)
    chip = 2 * lax.axis_index("x") + lax.axis_index("y")
    return [lax.dynamic_update_slice(o, s[None], (chip,) + (0,) * s.ndim) for o, s in zip(outs, [*shards, conv_w])]


def _allreduce_small(buf):
    rows = buf.shape[0]

    def body(in_ref, out_ref, slots, send_sems, recv_sems):
        x, y, c, _ = _place()
        me = 4 * x + 2 * y + c
        slots[me] = in_ref[...]
        sent = []
        for p in range(1, 8):
            to = (x ^ (p >> 2), y ^ ((p >> 1) & 1), c ^ (p & 1))
            sent.append(_remote(in_ref, slots.at[me], send_sems, recv_sems, p, to))
        for cp in sent:
            cp.start()
        for p in range(1, 8):
            frm = 4 * (x ^ (p >> 2)) + 2 * (y ^ ((p >> 1) & 1)) + (c ^ (p & 1))
            _remote(in_ref, slots.at[frm], send_sems, recv_sems, p, (x, y, c)).wait_recv()
        for cp in sent:
            cp.wait_send()
        acc = slots[0]
        for d in range(1, 8):
            acc = acc + slots[d]
        out_ref[...] = acc

    vm = pl.BlockSpec(memory_space=pltpu.VMEM)
    return pl.pallas_call(
        body, name="allreduce_small", in_specs=[vm], out_specs=vm,
        out_shape=jax.ShapeDtypeStruct(buf.shape, F32),
        scratch_shapes=[pltpu.VMEM((8, rows, 128), F32), pltpu.SemaphoreType.DMA((8,)), pltpu.SemaphoreType.DMA((8,))],
    )(buf)


def _pair_exchange(gs, name):
    n = len(gs)
    halves = [g.shape[1] // 2 for g in gs]

    def body(*refs):
        g, got = refs[:n], refs[n:2 * n]
        send_sems, recv_sems = refs[2 * n:]
        x, y, c, _ = _place()
        cps = [_remote(g[w].at[:, pl.ds((1 - c) * halves[w], halves[w]), :], got[w], send_sems, recv_sems, w,
                       (x, y, 1 - c)) for w in range(n)]
        for cp in cps:
            cp.start()
        for cp in cps:
            cp.wait()

    return pl.pallas_call(
        body, name=name, in_specs=[ANY] * n, out_specs=[ANY] * n,
        out_shape=[jax.ShapeDtypeStruct((N_CHIPS, h, g.shape[2]), g.dtype) for g, h in zip(gs, halves)],
        scratch_shapes=[pltpu.SemaphoreType.DMA((n,)), pltpu.SemaphoreType.DMA((n,))],
    )(*gs)


def _core_id():
    return lax.axis_index("c").reshape(1).astype(jnp.int32)


def _pair_sum(g, got, name):
    h, C = got.shape[1:]

    def body(c_ref, g_ref, b_ref, o_ref):
        o_ref[...] = (g_ref[...].astype(F32) + b_ref[...].astype(F32)).astype(BF16)

    blk = pl.BlockSpec((1, h, C), lambda k, c_ref: (k, 0, 0))
    return pl.pallas_call(
        body, name=name,
        grid_spec=pltpu.PrefetchScalarGridSpec(
            num_scalar_prefetch=1, grid=(N_CHIPS,),
            in_specs=[pl.BlockSpec((1, h, C), lambda k, c_ref: (k, c_ref[0], 0)), blk], out_specs=blk),
        out_shape=jax.ShapeDtypeStruct(got.shape, BF16), compiler_params=_cp("arbitrary"))(_core_id(), g, got)


def _sum_partials(g, got, landed, name):
    h, C = got.shape[1:]

    def body(ids, g_ref, b_ref, l_ref, o_ref):
        acc = g_ref[0].astype(F32) + b_ref[0].astype(F32)
        for j in range(3):
            acc = acc + l_ref[j].astype(F32)
        o_ref[...] = acc

    ids = jnp.stack([2 * lax.axis_index("x") + lax.axis_index("y"), lax.axis_index("c")]).astype(jnp.int32)
    return pl.pallas_call(
        body, name=name,
        grid_spec=pltpu.PrefetchScalarGridSpec(
            num_scalar_prefetch=1, grid=(1,),
            in_specs=[pl.BlockSpec((1, h, C), lambda i, ids: (ids[0], ids[1], 0)),
                      pl.BlockSpec((1, h, C), lambda i, ids: (ids[0], 0, 0)),
                      pl.BlockSpec((3, h, C), lambda i, ids: (0, 0, 0))],
            out_specs=pl.BlockSpec((h, C), lambda i, ids: (ids[1], 0))),
        out_shape=jax.ShapeDtypeStruct((2 * h, C), F32), compiler_params=_cp("arbitrary"))(ids, g, got, landed)


def _pair_share(reds, name):
    n = len(reds)

    def body(*refs):
        out = refs[n:2 * n]
        send_sems, recv_sems = refs[2 * n:]
        x, y, c, _ = _place()
        def half(w, which):
            h = out[w].shape[0] // 2
            return out[w].at[pl.ds(which * h, h), :]

        cps = [_remote(half(w, c), half(w, c), send_sems, recv_sems, w, (x, y, 1 - c)) for w in range(n)]
        for cp in cps:
            cp.start()
        for w in range(n):
            _remote(half(w, 1 - c), half(w, 1 - c), send_sems, recv_sems, w, (x, y, 1 - c)).wait_recv()
        for cp in cps:
            cp.wait_send()

    return pl.pallas_call(
        body, name=name, in_specs=[ANY] * n, out_specs=[ANY] * n,
        out_shape=[jax.ShapeDtypeStruct(r.shape, F32) for r in reds],
        input_output_aliases={w: w for w in range(n)},
        scratch_shapes=[pltpu.SemaphoreType.DMA((n,)), pltpu.SemaphoreType.DMA((n,))],
    )(*reds)


HBM = pl.BlockSpec(memory_space=pltpu.HBM)
SEM = pl.BlockSpec(memory_space=pltpu.SEMAPHORE)
DATAFLOW = pltpu.SideEffectType.DATAFLOW_SIDE_EFFECTING


def _copies_start(name, srcs, lands, plan, n_copies, after):
    ns, nb, na = len(srcs), len(srcs) + len(lands), len(after)

    def body(*refs):
        src_refs, land_refs = refs[:ns], refs[ns:nb]
        send_sems, recv_sems = refs[nb + na:nb + na + 2]
        token = refs[-1]
        for k, (src, there, _, to) in enumerate(plan(src_refs, land_refs)):
            _remote(src, there, send_sems, recv_sems, k, to).start()
        token[...] = jnp.zeros_like(token)

    hbm = lambda a: pltpu.HBM(a.shape, a.dtype)
    outs = pl.pallas_call(
        body, name=name,
        out_shape=(pltpu.SemaphoreType.DMA((n_copies,)), pltpu.SemaphoreType.DMA((n_copies,)),
                   *[hbm(a) for a in srcs], *[hbm(a) for a in lands], jax.ShapeDtypeStruct((8, 128), F32)),
        in_specs=[HBM] * nb + [ANY] * na,
        out_specs=(SEM, SEM, *[HBM] * nb, pl.BlockSpec(memory_space=pltpu.VMEM)),
        input_output_aliases={i: 2 + i for i in range(nb)},
        compiler_params=pltpu.CompilerParams(has_side_effects=DATAFLOW),
    )(*[pltpu.with_memory_space_constraint(a, pltpu.HBM) for a in (*srcs, *lands)], *after)
    return outs[0], outs[1], outs[2:2 + ns], outs[2 + ns:2 + nb], outs[-1]


def _copies_wait(name, send_sems, recv_sems, srcs, lands, plan, after):
    ns, nb, na = len(srcs), len(srcs) + len(lands), len(after)

    def body(*refs):
        src_refs, land_refs = refs[:ns], refs[ns:nb]
        send_sems, recv_sems = refs[nb:nb + 2]
        for k, (src, _, here, to) in enumerate(plan(src_refs, land_refs)):
            cp = _remote(src, here, send_sems, recv_sems, k, to)
            cp.wait_send()
            cp.wait_recv()

    hbm = lambda a: pltpu.HBM(a.shape, a.dtype)
    outs = pl.pallas_call(
        body, name=name,
        out_shape=(*[hbm(a) for a in srcs], *[hbm(a) for a in lands]),
        in_specs=[HBM] * nb + [SEM, SEM] + [ANY] * na,
        out_specs=tuple([HBM] * nb),
        input_output_aliases={i: i for i in range(nb)},
        compiler_params=pltpu.CompilerParams(has_side_effects=DATAFLOW),
    )(*srcs, *lands, send_sems, recv_sems, *after)
    return outs[:ns], outs[ns:]


def _gather_plan(halves):
    def plan(shards, lands):
        x, y, c, chips = _place()
        me = 2 * x + y
        copies = []
        for w, h in enumerate(halves):
            rows = pl.ds(c * h, h)
            for chip in chips:
                copies.append((shards[w].at[rows, :], lands[w].at[me, rows, :],
                               lands[w].at[2 * chip[0] + chip[1], rows, :], (*chip, c)))
        return copies
    return plan


def _reduce_plan(n):
    def plan(ps, lands):
        x, y, c, chips = _place()
        return [(ps[w].at[2 * chip[0] + chip[1]], lands[w].at[j], lands[w].at[j], (*chip, c))
                for w in range(n) for j, chip in enumerate(chips)]
    return plan


def _forward_plan(halves):
    def plan(_, lands):
        x, y, c, chips = _place()

        def part(w, chip, half):
            return lands[w].at[2 * chip[0] + chip[1], pl.ds(half * halves[w], halves[w]), :]

        return [(part(w, chip, c), part(w, chip, c), part(w, chip, 1 - c), (x, y, 1 - c))
                for w in range(len(halves)) for chip in chips]
    return plan


def _pair_plan(halves):
    def plan(gs, gots):
        x, y, c, _ = _place()
        return [(gs[w].at[:, pl.ds((1 - c) * h, h), :], gots[w], gots[w], (x, y, 1 - c)) for w, h in enumerate(halves)]
    return plan


def _place_own(gathered, shards):
    chip = 2 * lax.axis_index("x") + lax.axis_index("y")
    return [lax.dynamic_update_slice(o, s[None], (chip, 0, 0)) for o, s in zip(gathered, shards)]


def _adamw(w, g, m, v, name, tr=None):
    R, C = w.shape
    tr = tr or R // 4

    def body(w_ref, g_ref, m_ref, v_ref, d_ref, nm_ref, nv_ref):
        gv = g_ref[...]
        nm = ADAM_B1 * m_ref[...] + (1.0 - ADAM_B1) * gv
        nv = ADAM_B2 * v_ref[...] + (1.0 - ADAM_B2) * (gv * gv)
        m_hat = nm / (1.0 - ADAM_B1 ** ADAM_STEP)
        v_hat = nv / (1.0 - ADAM_B2 ** ADAM_STEP)
        d_ref[...] = -ADAM_LR * (m_hat / (jnp.sqrt(v_hat) + ADAM_EPS) + ADAM_WD * w_ref[...])
        nm_ref[...] = nm
        nv_ref[...] = nv

    blk = pl.BlockSpec((tr, C), lambda i: (i, 0))
    return pl.pallas_call(body, name=name, grid=(R // tr,), in_specs=[blk] * 4, out_specs=[blk] * 3,
                          out_shape=[jax.ShapeDtypeStruct((R, C), F32)] * 3, compiler_params=_cp("arbitrary"))(w, g, m, v)


SMALL = (("norm1_g", 1, 1024), ("attn_norm_g", 1, 512), ("hgrn_norm_g", 1, 512), ("hgrn_lb_logits", 2, 512),
         ("norm2_g", 1, 1024), ("conv_b", 1, D_FF), ("final_norm_g", 1, 1024), ("conv_w", 3, D_FF))
LOSS_ROW = sum(r * c for _, r, c in SMALL) // 128
SMALL_ROWS = 136


def _rows_to_lanes(ref, row, width):
    return jnp.concatenate([ref[row + j:row + j + 1, :] for j in range(width // 128)], axis=1)


def _pack_small(grads, dlb, lb, loss):
    def body(*refs):
        parts, dlb_ref, lb_ref, loss_ref, out = refs[:len(SMALL) - 1], refs[-4], refs[-3], refs[-2], refs[-1]
        out[...] = jnp.zeros_like(out)
        lbv = lb_ref[...]
        dl = dlb_ref[...] * lbv * (1.0 - lbv)
        row = 0
        parts = list(parts)
        for name, rows, width in SMALL:
            for r in range(rows):
                if name == "hgrn_lb_logits":
                    src = dl if r == 0 else -dl
                    for j in range(width // 128):
                        out[row + j:row + j + 1, :] = src[:, 128 * j:128 * (j + 1)]
                else:
                    for j in range(width // 128):
                        out[row + j:row + j + 1, :] = parts[0][r:r + 1, 128 * j:128 * (j + 1)]
                row += width // 128
            if name != "hgrn_lb_logits":
                parts.pop(0)
        out[LOSS_ROW:LOSS_ROW + 1, :] = loss_ref[...]

    vm = pl.BlockSpec(memory_space=pltpu.VMEM)
    return pl.pallas_call(body, name="pack_small", in_specs=[vm] * (len(grads) + 3), out_specs=vm,
                          out_shape=jax.ShapeDtypeStruct((SMALL_ROWS, 128), F32))(*grads, dlb, lb, loss)


def _adamw_math(w, g, m, v):
    nm = ADAM_B1 * m + (1.0 - ADAM_B1) * g
    nv = ADAM_B2 * v + (1.0 - ADAM_B2) * (g * g)
    m_hat = nm / (1.0 - ADAM_B1 ** ADAM_STEP)
    v_hat = nv / (1.0 - ADAM_B2 ** ADAM_STEP)
    return -ADAM_LR * (m_hat / (jnp.sqrt(v_hat) + ADAM_EPS) + ADAM_WD * w), nm, nv


def _small_update(summed, g_conv_w, ws, ms, vs):
    n = len(SMALL)

    def body(*refs):
        s_ref, gcw_ref = refs[:2]
        w_refs, m_refs, v_refs = refs[2:2 + n], refs[2 + n:2 + 2 * n], refs[2 + 2 * n:2 + 3 * n]
        outs = refs[2 + 3 * n:]
        row = 0
        for k, (name, rows, width) in enumerate(SMALL):
            if name == "conv_w":
                g = gcw_ref[...]
            else:
                g = jnp.concatenate([_rows_to_lanes(s_ref, row + r * (width // 128), width) for r in range(rows)], axis=0)
            row += rows * (width // 128)
            d, nm, nv = _adamw_math(w_refs[k][...], g, m_refs[k][...], v_refs[k][...])
            for o, val in zip(outs[4 * k:4 * k + 4], (g, d, nm, nv)):
                o[...] = val

    vm = pl.BlockSpec(memory_space=pltpu.VMEM)
    outs = pl.pallas_call(
        body, name="small_update", in_specs=[vm] * (2 + 3 * n), out_specs=[vm] * (4 * n),
        out_shape=[jax.ShapeDtypeStruct(a.shape, F32) for a in ws for _ in range(4)],
    )(summed, g_conv_w, *ws, *ms, *vs)
    return [outs[4 * k:4 * k + 4] for k in range(n)]


def kernel(x, norm1_g, w_in, attn_norm_g, hgrn_norm_g, hgrn_lb_logits, w_out, norm2_g, w_up, conv_w, conv_b, w_down, final_norm_g, loss_target, m_norm1_g, m_w_in, m_attn_norm_g, m_hgrn_norm_g, m_hgrn_lb_logits, m_w_out, m_norm2_g, m_w_up, m_conv_w, m_conv_b, m_w_down, m_final_norm_g, v_norm1_g, v_w_in, v_attn_norm_g, v_hgrn_norm_g, v_hgrn_lb_logits, v_w_out, v_norm2_g, v_w_up, v_conv_w, v_conv_b, v_w_down, v_final_norm_g):
    w = dict(norm1_g=norm1_g, w_in=w_in, attn_norm_g=attn_norm_g, hgrn_norm_g=hgrn_norm_g,
             hgrn_lb_logits=hgrn_lb_logits, w_out=w_out, norm2_g=norm2_g, w_up=w_up, conv_w=conv_w, conv_b=conv_b,
             w_down=w_down, final_norm_g=final_norm_g)
    m = dict(norm1_g=m_norm1_g, w_in=m_w_in, attn_norm_g=m_attn_norm_g, hgrn_norm_g=m_hgrn_norm_g,
             hgrn_lb_logits=m_hgrn_lb_logits, w_out=m_w_out, norm2_g=m_norm2_g, w_up=m_w_up, conv_w=m_conv_w,
             conv_b=m_conv_b, w_down=m_w_down, final_norm_g=m_final_norm_g)
    v = dict(norm1_g=v_norm1_g, w_in=v_w_in, attn_norm_g=v_attn_norm_g, hgrn_norm_g=v_hgrn_norm_g,
             hgrn_lb_logits=v_hgrn_lb_logits, w_out=v_w_out, norm2_g=v_norm2_g, w_up=v_w_up, conv_w=v_conv_w,
             conv_b=v_conv_b, w_down=v_w_down, final_norm_g=v_final_norm_g)
    names = list(w)
    chip = 2 * lax.axis_index("x") + lax.axis_index("y")

    shards = {k: w[k][0].astype(BF16) for k in BIG}
    w_in4, conv_w4 = _gather_weights([shards["w_in"]], conv_w[0])
    conv_w_full = jnp.transpose(conv_w4, (1, 0, 2)).reshape(3, D_FF)
    lb = jax.nn.softmax(hgrn_lb_logits, axis=0)[0:1]
    late = [shards[k] for k in BIG[1:]]
    gather_plan = _gather_plan([s.shape[0] // 2 for s in late])
    started = _copies_start("gather_start", late, [lax.empty((N_CHIPS,) + s.shape, BF16) for s in late], gather_plan,
                            3 * len(late), after=(w_in4,))
    u1, qkv, hg = _in_proj(x[0], norm1_g + started[4][0:1, 0:1], w_in4)
    attn_o, lse = _attn_fwd(qkv)
    late, landed_w = _copies_wait("gather_wait", *started[:4], gather_plan, after=(attn_o,))
    forward_plan = _forward_plan([s.shape[0] // 2 for s in late])
    started = _copies_start("forward_start", [], landed_w, forward_plan, 3 * len(late), after=())
    rec_o, states = _hgrn_fwd(hg, lb + started[4][0:1, 0:1])
    a = dict(u1=u1, qkv=qkv, hg=hg, attn_o=attn_o, lse=lse, rec_o=rec_o, states=states)
    w_out4, w_up4, w_down4 = _place_own(
        _copies_wait("forward_wait", *started[:4], forward_plan, after=(rec_o,))[1], late)

    b = _step_channel(a, x[0], loss_target[0], attn_norm_g, hgrn_norm_g, w_out4.reshape(D_MODEL, D_MODEL), norm2_g,
                      w_up4, conv_w_full, conv_b, w_down4.reshape(D_FF, D_MODEL), final_norm_g.reshape(1, D_MODEL))

    early = [b["dw_out"], b["dw_up"], b["dw_down"]]
    pair_plan = _pair_plan([gk.shape[1] // 2 for gk in early])
    started = _copies_start("pair_start", early,
                            [lax.empty((N_CHIPS, gk.shape[1] // 2, gk.shape[2]), BF16) for gk in early], pair_plan,
                            len(early), after=())
    dqkv = _attn_bwd(qkv, attn_o, lse, b["da"], started[4])
    early, gots = _copies_wait("pair_wait", *started[:4], pair_plan, after=(dqkv[0],))
    ps = [_pair_sum(gk, got, f"pair_sum_{k}") for gk, got, k in zip(early, gots, BIG[1:])]
    reduce_plan = _reduce_plan(len(ps))
    started = _copies_start("reduce_start", ps, [lax.empty((3,) + p.shape[1:], BF16) for p in ps], reduce_plan,
                            3 * len(ps), after=())
    c = _step_mixers_bwd(a, b, x[0], norm1_g, w_in4, lb + started[4][0:1, 0:1], dqkv)
    gots_in = _pair_exchange([c["dw_in"]], "pair_exchange_w_in")
    ps_in = _pair_sum(c["dw_in"], gots_in[0], "pair_sum_w_in")
    plan_in = _reduce_plan(1)
    started_in = _copies_start("reduce_start_w_in", [ps_in], [lax.empty((3,) + ps_in.shape[1:], BF16)], plan_in, 3,
                               after=())
    landed = _copies_wait("reduce_wait", *started[:4], reduce_plan, after=(started_in[4],))[1]
    reds = [_sum_partials(gk, got, l, f"sum_partials_{k}") for gk, got, l, k in zip(early, gots, landed, BIG[1:])]
    g = dict(zip(BIG[1:], _pair_share(reds, "pair_share")))
    delta, new_m, new_v = {}, {}, {}
    for k in BIG[1:]:
        delta[k], new_m[k], new_v[k] = _adamw(w[k][0], g[k], m[k][0], v[k][0], f"adamw_{k}")

    loss, dx = b["loss"], c["dx"]
    small = dict(g1=c["dg1"], g_a=b["dga"], g_h=b["dgh"], lb=c["dlb"], g2=b["dg2"], conv_w=b["dcw"], conv_b=b["dcb"],
                 gf=b["dgf"])
    summed = _allreduce_small(_pack_small(
        [small["g1"], small["g_a"], small["g_h"], small["g2"], small["conv_b"], small["gf"], small["conv_w"]],
        small["lb"], lb, loss))
    loss_total = summed[LOSS_ROW, 0]
    g_conv_w = lax.dynamic_slice(summed[LOSS_ROW - 3 * D_FF // 128:LOSS_ROW].reshape(3, D_FF),
                                 (0, chip * (D_FF // N_CHIPS)), (3, D_FF // N_CHIPS))
    two_d = lambda p, k: p[k].reshape(-1, p[k].shape[-1])
    updated = _small_update(summed, g_conv_w, *[[two_d(p, k) for k, _, _ in SMALL] for p in (w, m, v)])
    for (k, _, _), parts in zip(SMALL, updated):
        g[k], delta[k], new_m[k], new_v[k] = (a.reshape(w[k].shape) for a in parts)

    landed_in = _copies_wait("reduce_wait_w_in", *started_in[:4], plan_in, after=(updated[0][1], delta["w_up"]))[1]
    red_in = _sum_partials(c["dw_in"], gots_in[0], landed_in[0], "sum_partials_w_in")
    g["w_in"] = _pair_share([red_in], "pair_share_w_in")[0]
    delta["w_in"], new_m["w_in"], new_v["w_in"] = _adamw(w_in[0], g["w_in"], m_w_in[0], v_w_in[0], "adamw_w_in")
    for k in BIG:
        g[k], delta[k], new_m[k], new_v[k] = g[k][None], delta[k][None], new_m[k][None], new_v[k][None]

    return (loss_total, dx[None], *[g[k] for k in names], *[delta[k] for k in names],
            *[new_m[k] for k in names], *[new_v[k] for k in names])
```

```python
import functools
import math

import jax
import jax.numpy as jnp
from jax import lax
from jax.experimental import pallas as pl
from jax.experimental.pallas import tpu as pltpu

F32 = jnp.float32
BF16 = jnp.bfloat16

D_MODEL = 1024
ATTN_W = 512
HGRN_W = 512
HEAD_PAIR = 128
ATTN_BLK = 128
DILATIONS = (1, 4, 16)
ATTN_CHAINS = 4
ATTN_CHAINS_FWD = 8
HGRN_HEADS = 4
HGRN_DIM = 128
HGRN_CHUNK = 64
SUPER = 256
HGRN_SIDE = 2
D_FF = 2816
FF_CHUNKS = ((0, 1536), (1536, D_FF))
N_CHIPS = 4
IN_TOTAL = 3584
IN_SHARD = IN_TOTAL // N_CHIPS
UP_SHARD = 2 * D_FF // N_CHIPS
QKV_W = 3 * ATTN_W
HG_W = 4 * HGRN_W
EPS = 1e-6
NEG = -1e30
V7X_VMEM_BYTES = 64 * 1024 * 1024
VMEM_LIMIT = V7X_VMEM_BYTES - 8 * 1024 * 1024

ADAM_LR = 0.001
ADAM_B1 = 0.9
ADAM_B2 = 0.999
ADAM_EPS = 1e-08
ADAM_WD = 0.01
ADAM_STEP = 10

MESH = pl.DeviceIdType.MESH


def _cp(*sem):
    return pltpu.CompilerParams(dimension_semantics=sem or None, vmem_limit_bytes=VMEM_LIMIT)


def _dot(a, b):
    return jnp.dot(a, b, preferred_element_type=F32)


def _dot_nt(a, b):
    return lax.dot_general(a, b, (((1,), (1,)), ((), ())), preferred_element_type=F32)


def _dot_tn(a, b):
    return lax.dot_general(a, b, (((0,), (0,)), ((), ())), preferred_element_type=F32)


def _sigmoid(x):
    return 1.0 / (1.0 + jnp.exp(-x))


def _rms(x, width):
    return lax.rsqrt(jnp.sum(x * x, axis=-1, keepdims=True) * (1.0 / width) + EPS)


def _rms_bwd(dn, n, r, width):
    return r * (dn - n * (jnp.sum(dn * n, axis=-1, keepdims=True) * (1.0 / width)))


def _colsum(x):
    return jnp.sum(x, axis=0, keepdims=True)


def _row(v, k):
    rid = lax.broadcasted_iota(jnp.int32, v.shape, 0)
    return jnp.sum(jnp.where(rid == k, v, 0.0), axis=0, keepdims=True)


def _full(shape):
    return pl.BlockSpec(shape, lambda *_: (0,) * len(shape))


def _once(shape):
    return pl.BlockSpec(shape, lambda *_: (0,) * len(shape), pipeline_mode=pl.Buffered(1))


def _load_side_by_side(w_hbm, w_full, sem):
    width = w_hbm.shape[2]
    cps = [pltpu.make_async_copy(w_hbm.at[k], w_full.at[:, pl.ds(k * width, width)], sem.at[k]) for k in range(N_CHIPS)]
    for cp in cps:
        cp.start()
    for cp in cps:
        cp.wait()


def _in_proj(x, g1, w_in4, tm=512):
    T = x.shape[0]

    def body(x_ref, g_ref, w_hbm, u_ref, qkv_ref, hg_ref, w_full, sem):
        @pl.when(pl.program_id(0) == 0)
        def _():
            _load_side_by_side(w_hbm, w_full, sem)

        xv = x_ref[...]
        u = (xv * _rms(xv, D_MODEL) * g_ref[...]).astype(BF16)
        u_ref[...] = u
        p = _dot(u, w_full[...])
        qkv_ref[...] = p[:, :QKV_W]
        hg_ref[...] = p[:, QKV_W:]

    return pl.pallas_call(
        body, name="in_proj", grid=(T // tm,),
        in_specs=[pl.BlockSpec((tm, D_MODEL), lambda i: (i, 0)), _full((1, D_MODEL)), ANY],
        out_specs=[pl.BlockSpec((tm, D_MODEL), lambda i: (i, 0)), pl.BlockSpec((tm, QKV_W), lambda i: (i, 0)),
                   pl.BlockSpec((tm, HG_W), lambda i: (i, 0))],
        out_shape=[jax.ShapeDtypeStruct((T, D_MODEL), BF16), jax.ShapeDtypeStruct((T, QKV_W), F32),
                   jax.ShapeDtypeStruct((T, HG_W), F32)],
        scratch_shapes=[pltpu.VMEM((D_MODEL, IN_TOTAL), BF16), pltpu.SemaphoreType.DMA((N_CHIPS,))],
        compiler_params=_cp("arbitrary"),
    )(x, g1, w_in4)


def _attn_masks(bias_ref):
    lane = lax.broadcasted_iota(jnp.int32, (ATTN_BLK, HEAD_PAIR), 1)
    row = lax.broadcasted_iota(jnp.int32, (2 * ATTN_BLK, 2 * ATTN_BLK), 0)
    col = lax.broadcasted_iota(jnp.int32, (2 * ATTN_BLK, 2 * ATTN_BLK), 1)
    base = jnp.where(row >= ATTN_BLK, row - ATTN_BLK, row) - col
    for k in range(2):
        dist = base + k * ATTN_BLK
        bias_ref[k] = jnp.where((dist >= 0) & (dist <= ATTN_BLK), 0.0, NEG)
    bias_ref[2] = jnp.where(col >= ATTN_BLK, bias_ref[1], NEG)
    return lane < 64


def _two_heads(blk, first):
    zero = jnp.zeros_like(blk)
    return jnp.concatenate([jnp.where(first, blk, zero), jnp.where(first, zero, blk)], axis=0)


def _attn_rows(idx, nb, d):
    r, n = idx // nb, idx % nb
    kb = jnp.maximum(n - 1, 0)
    if d == 1:
        q0 = pl.multiple_of(n * ATTN_BLK, ATTN_BLK)
        k0 = pl.multiple_of(kb * ATTN_BLK, ATTN_BLK)
        return pl.ds(q0, ATTN_BLK), pl.ds(k0, 2 * ATTN_BLK), n - kb
    return (pl.ds(r + d * ATTN_BLK * n, ATTN_BLK, stride=d), pl.ds(r + d * ATTN_BLK * kb, 2 * ATTN_BLK, stride=d),
            n - kb)


def _attn_fwd(qkv):
    T = qkv.shape[0]

    per_chain = T // ATTN_BLK // ATTN_CHAINS_FWD

    def body(q_ref, k_ref, v_ref, o_ref, m_ref, l_ref, bias_ref):
        first = _attn_masks(bias_ref)
        for bi, d in enumerate(DILATIONS):
            nb = T // d // ATTN_BLK

            carried = d > 1 and per_chain % nb == 0

            def block(idx, kept=None, d=d, nb=nb, bi=bi, carried=carried):
                rows, keys, which = _attn_rows(idx, nb, d)
                q2 = _two_heads(q_ref[rows, :] * 0.125, first).astype(BF16)
                if carried:
                    k_own, v_own = k_ref[rows, :].astype(BF16), v_ref[rows, :].astype(BF16)
                    kw = jnp.concatenate([kept[0], k_own], axis=0)
                    vw = jnp.concatenate([kept[1], v_own], axis=0)
                    which = 2 - which
                else:
                    kw = k_ref[keys, :].astype(BF16)
                    vw = v_ref[keys, :].astype(BF16)
                old = (o_ref[rows, :], m_ref[rows, :], l_ref[rows, :]) if bi else None
                s = _dot_nt(q2, kw) + bias_ref[which]
                mb = jnp.max(s, axis=-1, keepdims=True)
                p = jnp.exp(s - mb)
                lb = jnp.sum(p, axis=-1, keepdims=True)
                o2 = _dot(p.astype(BF16), vw)
                o = jnp.where(first, o2[:ATTN_BLK], o2[ATTN_BLK:])
                m = jnp.where(first, mb[:ATTN_BLK], mb[ATTN_BLK:])
                l = jnp.where(first, lb[:ATTN_BLK], lb[ATTN_BLK:])
                if bi:
                    po, pm, pl_ = old
                    mn = jnp.maximum(pm, m)
                    wa = jnp.exp(pm - mn)
                    wb = jnp.exp(m - mn)
                    o, l, m = po * wa + o * wb, pl_ * wa + l * wb, mn
                return (rows, o, m, l), ((k_own, v_own) if carried else 0)

            def step(i, kept, block=block, carried=carried):
                done = [block(i + ch * per_chain, kept[ch] if carried else None) for ch in range(ATTN_CHAINS_FWD)]
                for (rows, o, m, l), _ in done:
                    o_ref[rows, :] = o
                    m_ref[rows, :] = m
                    l_ref[rows, :] = l
                return tuple(k for _, k in done) if carried else kept

            zero = jnp.zeros((ATTN_BLK, HEAD_PAIR), BF16)
            lax.fori_loop(0, per_chain, step, ((zero, zero),) * ATTN_CHAINS_FWD if carried else 0)

        def finish(i, carry):
            rows = pl.ds(pl.multiple_of(i * SUPER, SUPER), SUPER)
            l = l_ref[rows, :]
            o_ref[rows, :] = o_ref[rows, :] / l
            m_ref[rows, :] = m_ref[rows, :] + jnp.log(l)
            return carry

        lax.fori_loop(0, T // SUPER, finish, 0)

    col = lambda off: pl.BlockSpec((T, HEAD_PAIR), lambda j: (0, off + j))
    return pl.pallas_call(
        body, name="attn_fwd", grid=(4,),
        in_specs=[col(0), col(4), col(8)], out_specs=[col(0), col(0)],
        out_shape=[jax.ShapeDtypeStruct((T, ATTN_W), F32)] * 2,
        scratch_shapes=[pltpu.VMEM((T, HEAD_PAIR), F32), pltpu.VMEM((3, 2 * ATTN_BLK, 2 * ATTN_BLK), F32)],
        compiler_params=_cp("arbitrary"),
    )(qkv, qkv, qkv)


def _attn_bwd(qkv, o, lse, do, token=None):
    T = qkv.shape[0]
    per_chain = T // ATTN_BLK // ATTN_CHAINS
    extra = [] if token is None else [token]

    def body(q_ref, k_ref, v_ref, o_ref, lse_ref, do_ref, *rest):
        outs = rest[len(extra):len(extra) + 3]
        dq_ref, dk_ref, dv_ref, dkb_ref, dvb_ref, bias_ref = rest[len(extra) + 3:]
        first = _attn_masks(bias_ref)
        dq_ref[...] = jnp.zeros_like(dq_ref)
        dk_ref[...] = jnp.zeros_like(dk_ref)
        dv_ref[...] = jnp.zeros_like(dv_ref)

        def grads(rows, kw, vw, which):
            q2 = _two_heads(q_ref[rows, :] * 0.125, first).astype(BF16)
            lse_b = lse_ref[rows, :]
            dob = do_ref[rows, :]
            prod = dob * o_ref[rows, :]
            old = dq_ref[rows, :]
            lse2 = jnp.concatenate(
                [jnp.max(jnp.where(first, lse_b, NEG), axis=-1, keepdims=True),
                 jnp.max(jnp.where(first, NEG, lse_b), axis=-1, keepdims=True)], axis=0)
            p = jnp.exp(_dot_nt(q2, kw) + (bias_ref[which] - lse2))
            delta = jnp.concatenate(
                [jnp.sum(jnp.where(first, prod, 0.0), axis=-1, keepdims=True),
                 jnp.sum(jnp.where(first, 0.0, prod), axis=-1, keepdims=True)], axis=0)
            do2 = _two_heads(dob, first).astype(BF16)
            ds = (p * (_dot_nt(do2, vw) - delta)).astype(BF16)
            dq2 = _dot(ds, kw) * 0.125
            return (old + jnp.where(first, dq2[:ATTN_BLK], dq2[ATTN_BLK:]), _dot_tn(ds, q2),
                    _dot_tn(p.astype(BF16), do2))

        def block(idx):
            rows, keys, which = _attn_rows(idx, T // ATTN_BLK, 1)
            old = dk_ref[keys, :], dv_ref[keys, :]
            dq, ck, cv = grads(rows, k_ref[keys, :].astype(BF16), v_ref[keys, :].astype(BF16), which)
            return rows, keys, dq, old[0] + ck, old[1] + cv

        def step(i, carry):
            done = [block(i + ch * per_chain) for ch in range(ATTN_CHAINS)]
            for rows, keys, dq, dk, dv in done:
                dq_ref[rows, :] = dq
                dk_ref[keys, :] = dk
                dv_ref[keys, :] = dv
            return carry

        lax.fori_loop(0, per_chain, step, 0)

        for d in DILATIONS[1:]:
            nb = T // d // ATTN_BLK

            def block(idx, kept, d=d, nb=nb):
                r, n = idx // nb, idx % nb
                rows = pl.ds(r + d * ATTN_BLK * n, ATTN_BLK, stride=d)
                before = pl.ds(r + d * ATTN_BLK * jnp.maximum(n - 1, 0), ATTN_BLK, stride=d)
                k_prev, v_prev, dk_prev, dv_prev = kept
                k_own, v_own = k_ref[rows, :].astype(BF16), v_ref[rows, :].astype(BF16)
                dq, ck, cv = grads(rows, jnp.concatenate([k_prev, k_own], axis=0),
                                   jnp.concatenate([v_prev, v_own], axis=0), jnp.where(n > 0, 1, 2))
                stores = (rows, before, dq, dk_prev + ck[:ATTN_BLK], dv_prev + cv[:ATTN_BLK], ck[ATTN_BLK:], cv[ATTN_BLK:])
                return stores, (k_own, v_own, ck[ATTN_BLK:], cv[ATTN_BLK:])

            def step(i, kept, block=block):
                done = [block(i + ch * per_chain, kept[ch]) for ch in range(ATTN_CHAINS)]
                for (rows, before, dq, dk_done, dv_done, dk_own, dv_own), _ in done:
                    dq_ref[rows, :] = dq
                    dkb_ref[before, :] = dk_done
                    dvb_ref[before, :] = dv_done
                    dkb_ref[rows, :] = dk_own
                    dvb_ref[rows, :] = dv_own
                return tuple(k for _, k in done)

            zero = jnp.zeros((ATTN_BLK, HEAD_PAIR), F32)
            lax.fori_loop(0, per_chain, step, ((zero.astype(BF16), zero.astype(BF16), zero, zero),) * ATTN_CHAINS)

            def add(i, carry):
                rows = pl.ds(pl.multiple_of(i * SUPER, SUPER), SUPER)
                dk_ref[rows, :] += dkb_ref[rows, :]
                dv_ref[rows, :] += dvb_ref[rows, :]
                return carry

            lax.fori_loop(0, T // SUPER, add, 0)

        def emit(i, carry):
            rows = pl.ds(pl.multiple_of(i * SUPER, SUPER), SUPER)
            for out, acc in zip(outs, (dq_ref, dk_ref, dv_ref)):
                out[rows, :] = acc[rows, :].astype(BF16)
            return carry

        lax.fori_loop(0, T // SUPER, emit, 0)

    col = lambda off: pl.BlockSpec((T, HEAD_PAIR), lambda j: (0, off + j))
    return pl.pallas_call(
        body, name="attn_bwd", grid=(4,),
        in_specs=[col(0), col(4), col(8), col(0), col(0), col(0)] + [_full(t.shape) for t in extra],
        out_specs=[col(0)] * 3,
        out_shape=[jax.ShapeDtypeStruct((T, ATTN_W), BF16)] * 3,
        scratch_shapes=[pltpu.VMEM((T, HEAD_PAIR), F32)] * 5 + [pltpu.VMEM((3, 2 * ATTN_BLK, 2 * ATTN_BLK), F32)],
        compiler_params=_cp("arbitrary"),
    )(qkv, qkv, qkv, o, lse, do, *extra)


def _chunk_ids():
    row = lax.broadcasted_iota(jnp.int32, (SUPER, HGRN_DIM), 0)
    r2 = lax.broadcasted_iota(jnp.int32, (SUPER, SUPER), 0)
    c2 = lax.broadcasted_iota(jnp.int32, (SUPER, SUPER), 1)
    amask = ((r2 // HGRN_CHUNK) == (c2 // HGRN_CHUNK)) & (c2 <= r2)
    return row % HGRN_CHUNK, row // HGRN_CHUNK, amask


def _cumsum_chunk(x, rmod):
    s = 1
    while s < HGRN_CHUNK:
        x = x + jnp.where(rmod >= s, pltpu.roll(x, s, 0), 0.0)
        s *= 2
    return x


def _suffix_sum_chunk(x, rmod):
    s = 1
    while s < HGRN_CHUNK:
        x = x + jnp.where(rmod < HGRN_CHUNK - s, pltpu.roll(x, SUPER - s, 0), 0.0)
        s *= 2
    return x


def _chunk_rows(vs, cid):
    out = vs[-1]
    for c in reversed(range(len(vs) - 1)):
        out = jnp.where(cid == c, vs[c], out)
    return out


def _expand(x, cid):
    return jnp.concatenate([jnp.where(cid == c, x, 0.0) for c in range(SUPER // HGRN_CHUNK)], axis=1)


def _hgrn_gates(q, f, lbv, rmod, cid, tmp):
    sq = _sigmoid(q)
    sg = _sigmoid(f)
    forget = lbv + (1.0 - lbv) * sg
    key = 1.0 - forget
    b = _cumsum_chunk(jnp.log(forget), rmod)
    tmp[...] = b
    bends = [tmp[c * HGRN_CHUNK + HGRN_CHUNK - 1:(c + 1) * HGRN_CHUNK, :] for c in range(SUPER // HGRN_CHUNK)]
    eb = jnp.exp(b)
    enb = jnp.exp(-b)
    ebe = jnp.exp(_chunk_rows(bends, cid) - b)
    return sq, sg, forget, key, eb, enb, ebe, q * sq * eb, key * enb, key * ebe, [jnp.exp(v) for v in bends]


def _hgrn_fwd(hg, lb):
    T = hg.shape[0]
    nsc = T // SUPER
    NC = SUPER // HGRN_CHUNK

    def body(q_ref, f_ref, i_ref, lb_ref, o_ref, st_ref, state, tmp):
        rmod, cid, amask = _chunk_ids()
        state[...] = jnp.zeros_like(state)
        lbv = lb_ref[...]

        def local(sc, u):
            rows = pl.ds(pl.multiple_of(sc * SUPER, SUPER), SUPER)
            iv = i_ref[rows, :].astype(BF16)
            qd, ki, ke, dec = _hgrn_gates(q_ref[rows, :], f_ref[rows, :], lbv, rmod, cid, tmp.at[u])[-4:]
            a = jnp.where(amask, _dot_nt(qd.astype(BF16), ki.astype(BF16)), 0.0)
            return rows, qd, dec, _dot(a.astype(BF16), iv), _dot_tn(iv, _expand(ke, cid).astype(BF16))

        def step(i, carry):
            parts = [local(i * HGRN_SIDE + u, u) for u in range(HGRN_SIDE)]
            st = state[...]
            entering = []
            for u, (_, _, dec, _, ut) in enumerate(parts):
                st_ref[0, i * HGRN_SIDE + u] = st
                sts = []
                for c in range(NC):
                    sts.append(st)
                    st = st * dec[c] + ut[:, c * HGRN_DIM:(c + 1) * HGRN_DIM]
                entering.append(jnp.concatenate(sts, axis=1).astype(BF16))
            state[...] = st
            for (rows, qd, _, o, _), sts in zip(parts, entering):
                o_ref[rows, :] = o + _dot_nt(_expand(qd, cid).astype(BF16), sts)
            return carry

        lax.fori_loop(0, nsc // HGRN_SIDE, step, 0)

    col = lambda off: pl.BlockSpec((T, HGRN_DIM), lambda h: (0, off + h))
    return pl.pallas_call(
        body, name="hgrn_fwd", grid=(HGRN_HEADS,),
        in_specs=[col(0), col(4), col(8), pl.BlockSpec((1, HGRN_DIM), lambda h: (0, h))],
        out_specs=[pl.BlockSpec((T, HGRN_DIM), lambda h: (0, h)),
                   pl.BlockSpec((1, nsc, HGRN_DIM, HGRN_DIM), lambda h: (h, 0, 0, 0))],
        out_shape=[jax.ShapeDtypeStruct((T, HGRN_W), F32),
                   jax.ShapeDtypeStruct((HGRN_HEADS, nsc, HGRN_DIM, HGRN_DIM), F32)],
        scratch_shapes=[pltpu.VMEM((HGRN_DIM, HGRN_DIM), F32), pltpu.VMEM((HGRN_SIDE, SUPER, HGRN_DIM), F32)],
        compiler_params=_cp("arbitrary"),
    )(hg, hg, hg, lb)


def _hgrn_bwd(hg, lb, states, do):
    T = hg.shape[0]
    nsc = T // SUPER
    NC = SUPER // HGRN_CHUNK

    def body(q_ref, f_ref, i_ref, lb_ref, st_ref, do_ref, dq_ref, df_ref, di_ref, dlb_ref, dstate, tmp):
        rmod, cid, amask = _chunk_ids()
        dstate[...] = jnp.zeros_like(dstate)
        dlb_ref[...] = jnp.zeros_like(dlb_ref)
        lbv = lb_ref[...]

        def local(sc, u):
            rows = pl.ds(pl.multiple_of(sc * SUPER, SUPER), SUPER)
            q = q_ref[rows, :]
            ivf = i_ref[rows, :]
            iv = ivf.astype(BF16)
            dof = do_ref[rows, :]
            dob = dof.astype(BF16)
            sq, sg, forget, key, eb, enb, ebe, qd, ki, ke, dec = _hgrn_gates(q, f_ref[rows, :], lbv, rmod, cid,
                                                                            tmp.at[u])
            qdb, kib = qd.astype(BF16), ki.astype(BF16)
            keexp = _expand(ke, cid).astype(BF16)
            a = jnp.where(amask, _dot_nt(qdb, kib), 0.0).astype(BF16)
            ut = _dot_tn(iv, keexp)
            st = st_ref[0, sc]
            sts = []
            for c in range(NC):
                sts.append(st)
                st = st * dec[c] + ut[:, c * HGRN_DIM:(c + 1) * HGRN_DIM]
            gt = _dot_tn(dob, _expand(qd, cid).astype(BF16))
            da = jnp.where(amask, _dot_nt(dob, iv), 0.0).astype(BF16)
            ststack = jnp.concatenate(sts, axis=0).astype(BF16)
            return dict(rows=rows, q=q, sq=sq, sg=sg, forget=forget, eb=eb, enb=enb, ebe=ebe, qd=qd, ki=ki, ke=ke,
                        dec=dec, sts=sts, gt=gt, keexp=keexp, ivexp=_expand(ivf, cid).astype(BF16),
                        div=_dot_tn(a, dob), dki=_dot_tn(da, qdb),
                        dqd=_dot(da, kib) + _dot(_expand(dof, cid).astype(BF16), ststack))

        def finish(p, nxt, ddec):
            ncat = jnp.concatenate(nxt, axis=1).astype(BF16)
            nstack = jnp.concatenate(nxt, axis=0).astype(BF16)
            dke = _dot(p["ivexp"], nstack)
            dkk = dke * p["ke"]
            dkey = p["dki"] * p["enb"] + dke * p["ebe"]
            db = p["dqd"] * p["qd"] - p["dki"] * p["ki"] - dkk
            dbends = [_colsum(jnp.where(cid == c, dkk, 0.0)) + ddec[c] * p["dec"][c] for c in range(NC)]
            dforget = (_suffix_sum_chunk(db, rmod) + _chunk_rows(dbends, cid)) / p["forget"] - dkey
            sg, sq, q = p["sg"], p["sq"], p["q"]
            df_ref[p["rows"], :] = (dforget * (1.0 - lbv) * sg * (1.0 - sg)).astype(BF16)
            dq_ref[p["rows"], :] = (p["dqd"] * p["eb"] * (sq * (1.0 + q * (1.0 - sq)))).astype(BF16)
            di_ref[p["rows"], :] = (p["div"] + _dot_nt(p["keexp"], ncat)).astype(BF16)
            return _colsum(dforget * (1.0 - sg))

        def step(i, carry):
            parts = [local(nsc - 1 - (i * HGRN_SIDE + u), u) for u in range(HGRN_SIDE)]
            dst = dstate[...]
            chained = []
            for p in parts:
                nxt = [None] * NC
                ddec = [None] * NC
                for c in reversed(range(NC)):
                    nxt[c] = dst
                    ddec[c] = _colsum(dst * p["sts"][c])
                    dst = dst * p["dec"][c] + p["gt"][:, c * HGRN_DIM:(c + 1) * HGRN_DIM]
                chained.append((nxt, ddec))
            dstate[...] = dst
            dlb = dlb_ref[...]
            for p, (nxt, ddec) in zip(parts, chained):
                dlb = dlb + finish(p, nxt, ddec)
            dlb_ref[...] = dlb
            return carry

        lax.fori_loop(0, nsc // HGRN_SIDE, step, 0)

    col = lambda off: pl.BlockSpec((T, HGRN_DIM), lambda h: (0, off + h))
    own = pl.BlockSpec((T, HGRN_DIM), lambda h: (0, h))
    vec = pl.BlockSpec((1, HGRN_DIM), lambda h: (0, h))
    return pl.pallas_call(
        body, name="hgrn_bwd", grid=(HGRN_HEADS,),
        in_specs=[col(0), col(4), col(8), vec,
                  pl.BlockSpec((1, nsc, HGRN_DIM, HGRN_DIM), lambda h: (h, 0, 0, 0)), own],
        out_specs=[own, own, own, vec],
        out_shape=[jax.ShapeDtypeStruct((T, HGRN_W), BF16)] * 3 + [jax.ShapeDtypeStruct((1, HGRN_W), F32)],
        scratch_shapes=[pltpu.VMEM((HGRN_DIM, HGRN_DIM), F32), pltpu.VMEM((HGRN_SIDE, SUPER, HGRN_DIM), F32)],
        compiler_params=_cp("arbitrary"),
    )(hg, hg, hg, lb, states, do)


def _rec_heads(rec, gate, g_h):
    rr = jnp.concatenate(
        [jnp.broadcast_to(_rms(rec[:, h * HGRN_DIM:(h + 1) * HGRN_DIM], HGRN_DIM), (rec.shape[0], HGRN_DIM))
         for h in range(HGRN_HEADS)], axis=1)
    rn = rec * rr
    sg = _sigmoid(gate)
    return rr, rn, sg


def _mix_out(attn_o, rec_o, hg, x, g_a, g_h, w_out, tm=512):
    T = x.shape[0]

    def body(a_ref, r_ref, gt_ref, x_ref, ga_ref, gh_ref, w_ref, h1_ref, mixed_ref):
        a = a_ref[...]
        an = a * _rms(a, ATTN_W) * ga_ref[...]
        gate = gt_ref[...]
        _, rn, sg = _rec_heads(r_ref[...], gate, gh_ref[...])
        mixed = jnp.concatenate([an, rn * gh_ref[...] * (gate * sg)], axis=1).astype(BF16)
        mixed_ref[...] = mixed
        h1_ref[...] = x_ref[...] + _dot(mixed, w_ref[...])

    row = lambda w: pl.BlockSpec((tm, w), lambda i: (i, 0))
    return pl.pallas_call(
        body, name="mix_out", grid=(T // tm,),
        in_specs=[row(ATTN_W), row(HGRN_W), pl.BlockSpec((tm, HGRN_W), lambda i: (i, 3)), row(D_MODEL),
                  _full((1, ATTN_W)), _full((1, HGRN_W)), _once((D_MODEL, D_MODEL))],
        out_specs=[row(D_MODEL), row(D_MODEL)],
        out_shape=[jax.ShapeDtypeStruct((T, D_MODEL), F32), jax.ShapeDtypeStruct((T, D_MODEL), BF16)],
        compiler_params=_cp("arbitrary"),
    )(attn_o, rec_o, hg, x, g_a, g_h, w_out)


_INV_SQRT2 = 1.0 / math.sqrt(2.0)
_INV_SQRT2PI = 1.0 / math.sqrt(2.0 * math.pi)


def _gelu(x):
    return 0.5 * x * (1.0 + lax.erf(x * _INV_SQRT2))


def _gelu_grad(x):
    return 0.5 * (1.0 + lax.erf(x * _INV_SQRT2)) + x * jnp.exp(-0.5 * x * x) * _INV_SQRT2PI


def _shift_down(g, prev, rowid):
    p1 = _row(prev, prev.shape[0] - 1)
    p2 = _row(prev, prev.shape[0] - 2)
    s1 = jnp.where(rowid == 0, p1, pltpu.roll(g, 1, 0))
    s2 = jnp.where(rowid == 0, p2, jnp.where(rowid == 1, p1, pltpu.roll(g, 2, 0)))
    return s1, s2


def _mlp_fwd(h1, g2, w_up4, conv_w, conv_b, w_down, gf, tgt, tm=256):
    T = h1.shape[0]

    def body(h_ref, g2_ref, wu_hbm, cw_ref, cb_ref, wd_ref, gf_ref, t_ref,
             u_ref, gate_ref, val_ref, conv_ref, act_ref, dh_ref, loss_ref, dgf_ref, carry, wu_ref, sem):
        i = pl.program_id(0)

        @pl.when(i == 0)
        def _():
            carry[...] = jnp.zeros_like(carry)
            loss_ref[...] = jnp.zeros_like(loss_ref)
            dgf_ref[...] = jnp.zeros_like(dgf_ref)
            _load_side_by_side(wu_hbm, wu_ref, sem)

        h = h_ref[...]
        u = (h * _rms(h, D_MODEL) * g2_ref[...]).astype(BF16)
        u_ref[...] = u
        y2 = jnp.zeros((tm, D_MODEL), F32)
        for lo, hi in FF_CHUNKS:
            cols = slice(lo, hi)
            rowid = lax.broadcasted_iota(jnp.int32, (tm, hi - lo), 0)
            gb = _dot(u, wu_ref[:, lo:hi]).astype(BF16)
            vb = _dot(u, wu_ref[:, D_FF + lo:D_FF + hi]).astype(BF16)
            gate_ref[:, cols] = gb
            val_ref[:, cols] = vb
            g = gb.astype(F32)
            s1, s2 = _shift_down(g, carry[:, cols], rowid)
            carry[:, cols] = g[tm - 8:, :]
            conv = cb_ref[:, cols] + cw_ref[0:1, cols] * s2 + cw_ref[1:2, cols] * s1 + cw_ref[2:3, cols] * g
            act = (_gelu(conv) * vb.astype(F32)).astype(BF16)
            conv_ref[:, cols] = conv.astype(BF16)
            act_ref[:, cols] = act
            y2 = y2 + _dot(act, wd_ref[cols, :])
        h2 = h + y2
        rf = _rms(h2, D_MODEL)
        n = h2 * rf
        gfv = gf_ref[...]
        e = n * gfv - t_ref[...]
        loss_ref[...] += jnp.sum(e * e) * (0.5 / D_MODEL)
        dy = e * (1.0 / D_MODEL)
        dgf_ref[...] += _colsum(dy * n)
        dh_ref[...] = _rms_bwd(dy * gfv, n, rf, D_MODEL)

    row = lambda w: pl.BlockSpec((tm, w), lambda i: (i, 0))
    return pl.pallas_call(
        body, name="mlp_fwd", grid=(T // tm,),
        in_specs=[row(D_MODEL), _full((1, D_MODEL)), ANY, _full((3, D_FF)),
                  _full((1, D_FF)), _once((D_FF, D_MODEL)), _full((1, D_MODEL)), row(D_MODEL)],
        out_specs=[row(D_MODEL), row(D_FF), row(D_FF), row(D_FF), row(D_FF), row(D_MODEL), _full((1, 128)),
                   _full((1, D_MODEL))],
        out_shape=[jax.ShapeDtypeStruct((T, D_MODEL), BF16)] + [jax.ShapeDtypeStruct((T, D_FF), BF16)] * 4
        + [jax.ShapeDtypeStruct((T, D_MODEL), F32),
                   jax.ShapeDtypeStruct((1, 128), F32), jax.ShapeDtypeStruct((1, D_MODEL), F32)],
        scratch_shapes=[pltpu.VMEM((8, D_FF), F32), pltpu.VMEM((D_MODEL, 2 * D_FF), BF16),
                        pltpu.SemaphoreType.DMA((N_CHIPS,))],
        compiler_params=_cp("arbitrary"),
    )(h1, g2, w_up4, conv_w, conv_b, w_down, gf, tgt)


def _mlp_bwd(dh2, gate, val, conv, conv_w, w_down, tm=256):
    T = dh2.shape[0]
    nb = T // tm
    half = D_FF // 2

    def body(dh_ref, gate_ref, val_ref, conv_ref, cw_ref, wd_ref, dgv_ref, dcw_ref, dcb_ref, carry):
        @pl.when(pl.program_id(0) == 0)
        def _():
            carry[...] = jnp.zeros_like(carry)
            dcw_ref[...] = jnp.zeros_like(dcw_ref)
            dcb_ref[...] = jnp.zeros_like(dcb_ref)

        dhb = dh_ref[...].astype(BF16)
        rowid = lax.broadcasted_iota(jnp.int32, (tm, half), 0)
        for c in range(2):
            cols = slice(c * half, (c + 1) * half)
            g = gate_ref[:, cols].astype(F32)
            v = val_ref[:, cols].astype(F32)
            cv = conv_ref[:, cols].astype(F32)
            dact = _dot_nt(dhb, wd_ref[cols, :])
            dconv = dact * v * _gelu_grad(cv)
            nxt = carry[:, cols]
            n0, n1 = _row(nxt, 0), _row(nxt, 1)
            u1 = jnp.where(rowid == tm - 1, n0, pltpu.roll(dconv, tm - 1, 0))
            u2 = jnp.where(rowid == tm - 1, n1, jnp.where(rowid == tm - 2, n0, pltpu.roll(dconv, tm - 2, 0)))
            carry[:, cols] = dconv[0:8, :]
            dcb_ref[:, cols] += _colsum(dconv)
            dcw_ref[0:1, cols] += _colsum(u2 * g)
            dcw_ref[1:2, cols] += _colsum(u1 * g)
            dcw_ref[2:3, cols] += _colsum(dconv * g)
            dgate = cw_ref[2:3, cols] * dconv + cw_ref[1:2, cols] * u1 + cw_ref[0:1, cols] * u2
            dgv_ref[:, cols] = dgate.astype(BF16)
            dgv_ref[:, D_FF + c * half:D_FF + (c + 1) * half] = (dact * _gelu(cv)).astype(BF16)

    rev = lambda w: pl.BlockSpec((tm, w), lambda i: (nb - 1 - i, 0))
    return pl.pallas_call(
        body, name="mlp_bwd", grid=(nb,),
        in_specs=[rev(D_MODEL), rev(D_FF), rev(D_FF), rev(D_FF), _full((3, D_FF)), _once((D_FF, D_MODEL))],
        out_specs=[rev(2 * D_FF), _full((3, D_FF)), _full((1, D_FF))],
        out_shape=[jax.ShapeDtypeStruct((T, 2 * D_FF), BF16), jax.ShapeDtypeStruct((3, D_FF), F32),
                   jax.ShapeDtypeStruct((1, D_FF), F32)],
        scratch_shapes=[pltpu.VMEM((8, D_FF), F32)],
        compiler_params=_cp("arbitrary"),
    )(dh2, gate, val, conv, conv_w, w_down)


def _up_out_bwd(dgv, w_up4, h1, g2, dh2, w_out, attn_o, rec_o, hg, g_a, g_h, tm=256):
    T = h1.shape[0]

    def body(dgv_ref, wu_hbm, h_ref, g2_ref, dh2_ref, wo_ref, a_ref, r_ref, gt_ref, ga_ref, gh_ref,
             dh1_ref, dg2_ref, da_ref, dr_ref, dgt_ref, dga_ref, dgh_ref, wu_ref, sem):
        @pl.when(pl.program_id(0) == 0)
        def _():
            dg2_ref[...] = jnp.zeros_like(dg2_ref)
            dga_ref[...] = jnp.zeros_like(dga_ref)
            dgh_ref[...] = jnp.zeros_like(dgh_ref)
            _load_side_by_side(wu_hbm, wu_ref, sem)

        du = _dot_nt(dgv_ref[...], wu_ref[...])
        h = h_ref[...]
        r = _rms(h, D_MODEL)
        n = h * r
        dg2_ref[...] += _colsum(du * n)
        dh1 = dh2_ref[...] + _rms_bwd(du * g2_ref[...], n, r, D_MODEL)
        dh1_ref[...] = dh1
        dmix = _dot_nt(dh1.astype(BF16), wo_ref[...])
        dan = dmix[:, :ATTN_W]
        a = a_ref[...]
        ra = _rms(a, ATTN_W)
        na = a * ra
        dga_ref[...] += _colsum(dan * na)
        da_ref[...] = _rms_bwd(dan * ga_ref[...], na, ra, ATTN_W)
        dmr = dmix[:, ATTN_W:]
        gate = gt_ref[...]
        ghv = gh_ref[...]
        rr, rn, sg = _rec_heads(r_ref[...], gate, ghv)
        dgt_ref[...] = (dmr * rn * ghv * (sg * (1.0 + gate * (1.0 - sg)))).astype(BF16)
        drecn = dmr * (gate * sg)
        dgh_ref[...] += _colsum(drecn * rn)
        drn = drecn * ghv
        prod = drn * rn
        mean = jnp.concatenate(
            [jnp.broadcast_to(jnp.sum(prod[:, h_ * HGRN_DIM:(h_ + 1) * HGRN_DIM], axis=-1, keepdims=True),
                              (tm, HGRN_DIM)) for h_ in range(HGRN_HEADS)], axis=1) * (1.0 / HGRN_DIM)
        dr_ref[...] = rr * (drn - rn * mean)

    row = lambda w: pl.BlockSpec((tm, w), lambda i: (i, 0))
    return pl.pallas_call(
        body, name="up_out_bwd", grid=(T // tm,),
        in_specs=[row(2 * D_FF), ANY, row(D_MODEL), _full((1, D_MODEL)),
                  row(D_MODEL), _once((D_MODEL, D_MODEL)), row(ATTN_W), row(HGRN_W),
                  pl.BlockSpec((tm, HGRN_W), lambda i: (i, 3)), _full((1, ATTN_W)), _full((1, HGRN_W))],
        out_specs=[row(D_MODEL), _full((1, D_MODEL)), row(ATTN_W), row(HGRN_W), row(HGRN_W),
                   _full((1, ATTN_W)), _full((1, HGRN_W))],
        out_shape=[jax.ShapeDtypeStruct((T, D_MODEL), F32), jax.ShapeDtypeStruct((1, D_MODEL), F32),
                   jax.ShapeDtypeStruct((T, ATTN_W), F32), jax.ShapeDtypeStruct((T, HGRN_W), F32),
                   jax.ShapeDtypeStruct((T, HGRN_W), BF16), jax.ShapeDtypeStruct((1, ATTN_W), F32),
                   jax.ShapeDtypeStruct((1, HGRN_W), F32)],
        scratch_shapes=[pltpu.VMEM((D_MODEL, 2 * D_FF), BF16), pltpu.SemaphoreType.DMA((N_CHIPS,))],
        compiler_params=_cp("arbitrary"),
    )(dgv, w_up4, h1, g2, dh2, w_out, attn_o, rec_o, hg, g_a, g_h)


def _in_bwd(dqkv, dhg, w_in4, x, g1, dh1, tm=512):
    T = x.shape[0]

    def body(*refs):
        parts = refs[:7]
        w_hbm, x_ref, g_ref, dh1_ref, dp_ref, dx_ref, dg_ref, w_full, sem = refs[7:]

        @pl.when(pl.program_id(0) == 0)
        def _():
            dg_ref[...] = jnp.zeros_like(dg_ref)
            _load_side_by_side(w_hbm, w_full, sem)

        dp = jnp.concatenate([p[...] for p in parts], axis=1)
        dp_ref[...] = dp
        du = _dot_nt(dp, w_full[...])
        xv = x_ref[...]
        r = _rms(xv, D_MODEL)
        n = xv * r
        dg_ref[...] += _colsum(du * n)
        dx_ref[...] = dh1_ref[...] + _rms_bwd(du * g_ref[...], n, r, D_MODEL)

    row = lambda w: pl.BlockSpec((tm, w), lambda i: (i, 0))
    return pl.pallas_call(
        body, name="in_bwd", grid=(T // tm,),
        in_specs=[row(ATTN_W)] * 7 + [ANY, row(D_MODEL), _full((1, D_MODEL)), row(D_MODEL)],
        out_specs=[row(IN_TOTAL), row(D_MODEL), _full((1, D_MODEL))],
        out_shape=[jax.ShapeDtypeStruct((T, IN_TOTAL), BF16), jax.ShapeDtypeStruct((T, D_MODEL), F32),
                   jax.ShapeDtypeStruct((1, D_MODEL), F32)],
        scratch_shapes=[pltpu.VMEM((D_MODEL, IN_TOTAL), BF16), pltpu.SemaphoreType.DMA((N_CHIPS,))],
        compiler_params=_cp("arbitrary"),
    )(*dqkv, *dhg, w_in4, x, g1, dh1)


def _dw(a, b, kb, nb_, name, tk=1024, side=1):
    T, K = a.shape
    N = b.shape[1]
    nk, nn, nt = K // kb, N // (nb_ * side), T // tk

    def body(a_ref, b_ref, o_ref, acc):
        t = pl.program_id(2)

        @pl.when(t == 0)
        def _():
            acc[...] = jnp.zeros_like(acc)

        acc[...] += _dot_tn(a_ref[...], b_ref[...].astype(BF16))

        @pl.when(t == nt - 1)
        def _():
            for s in range(side):
                o_ref[s] = acc[:, s * nb_:(s + 1) * nb_].astype(BF16)

    return pl.pallas_call(
        body, name=name, grid=(nk, nn, nt),
        in_specs=[pl.BlockSpec((tk, kb), lambda i, j, t: (t, i)),
                  pl.BlockSpec((tk, nb_ * side), lambda i, j, t: (t, j))],
        out_specs=pl.BlockSpec((side, kb, nb_), lambda i, j, t: (i * nn + j, 0, 0)),
        out_shape=jax.ShapeDtypeStruct((nk * nn * side, kb, nb_), BF16),
        scratch_shapes=[pltpu.VMEM((kb, nb_ * side), F32)],
        compiler_params=_cp("arbitrary", "arbitrary", "arbitrary"),
    )(a, b)


def _local_step(x, tgt, g1, w_in4, g_a, g_h, lb, w_out, g2, w_up4, conv_w, conv_b, w_down, gf):
    a = _step_mixers(x, g1, w_in4, lb)
    b = _step_channel(a, x, tgt, g_a, g_h, w_out, g2, w_up4, conv_w, conv_b, w_down, gf)
    c = _step_mixers_bwd(a, b, x, g1, w_in4, lb)
    small = dict(g1=c["dg1"], g_a=b["dga"], g_h=b["dgh"], lb=c["dlb"], g2=b["dg2"], conv_w=b["dcw"], conv_b=b["dcb"],
                 gf=b["dgf"])
    return b["loss"], c["dx"], small, dict(w_in=c["dw_in"], w_out=b["dw_out"], w_up=b["dw_up"], w_down=b["dw_down"])


def _step_mixers(x, g1, w_in4, lb):
    u1, qkv, hg = _in_proj(x, g1, w_in4)
    attn_o, lse = _attn_fwd(qkv)
    rec_o, states = _hgrn_fwd(hg, lb)
    return dict(u1=u1, qkv=qkv, hg=hg, attn_o=attn_o, lse=lse, rec_o=rec_o, states=states)


def _step_channel(a, x, tgt, g_a, g_h, w_out, g2, w_up4, conv_w, conv_b, w_down, gf):
    h1, mixed = _mix_out(a["attn_o"], a["rec_o"], a["hg"], x, g_a, g_h, w_out)
    u2, gate, val, conv, act, dh2, loss, dgf = _mlp_fwd(h1, g2, w_up4, conv_w, conv_b, w_down, gf, tgt)
    dgv, dcw, dcb = _mlp_bwd(dh2, gate, val, conv, conv_w, w_down)
    dw_down = _dw(act, dh2, D_FF // 2, D_MODEL, "dw_down").reshape(N_CHIPS, D_FF // N_CHIPS, D_MODEL)
    dh1, dg2, da, dr, dgt, dga, dgh = _up_out_bwd(dgv, w_up4, h1, g2, dh2, w_out, a["attn_o"], a["rec_o"], a["hg"],
                                                  g_a, g_h)
    dw_up = _dw(u2, dgv, D_MODEL, UP_SHARD, "dw_up", side=2)
    dw_out = _dw(mixed, dh1, D_MODEL, D_MODEL, "dw_out").reshape(N_CHIPS, D_MODEL // N_CHIPS, D_MODEL)
    return dict(loss=loss, dgf=dgf, dcw=dcw, dcb=dcb, dg2=dg2, dga=dga, dgh=dgh, dh1=dh1, da=da, dr=dr, dgt=dgt,
                dw_down=dw_down, dw_up=dw_up, dw_out=dw_out)


def _step_mixers_bwd(a, b, x, g1, w_in4, lb, dqkv=None):
    if dqkv is None:
        dqkv = _attn_bwd(a["qkv"], a["attn_o"], a["lse"], b["da"])
    dhq, dhf, dhi, dlb = _hgrn_bwd(a["hg"], lb, a["states"], b["dr"])
    dproj, dx, dg1 = _in_bwd(dqkv, [dhq, dhf, dhi, b["dgt"]], w_in4, x, g1, b["dh1"])
    dw_in = _dw(a["u1"], dproj, D_MODEL, IN_SHARD, "dw_in", side=2)
    return dict(dx=dx, dg1=dg1, dlb=dlb, dw_in=dw_in)


BIG = ("w_in", "w_out", "w_up", "w_down")
ANY = pl.BlockSpec(memory_space=pl.ANY)


def _place():
    x, y, c = lax.axis_index("x"), lax.axis_index("y"), lax.axis_index("c")
    chips = [(1 - x, y), (x, 1 - y), (1 - x, 1 - y)]
    return x, y, c, chips


def _remote(src, dst, send_sems, recv_sems, k, to):
    return pltpu.make_async_remote_copy(src_ref=src, dst_ref=dst, send_sem=send_sems.at[k], recv_sem=recv_sems.at[k],
                                        device_id=to, device_id_type=MESH)


def _gather_weights(shards, conv_w):
    n = len(shards)
    halves = [s.shape[0] // 2 for s in shards]

    def body(*refs):
        ins, cw, outs, ocw = refs[:n], refs[n], refs[n + 1:2 * n + 1], refs[2 * n + 1]
        send_sems, recv_sems = refs[2 * n + 2:]
        x, y, c, chips = _place()
        me, sibling = 2 * x + y, (x, y, 1 - c)

        def part(w, chip, half):
            return outs[w].at[chip, pl.ds(half * halves[w], halves[w]), :]

        sent = []
        for j, chip in enumerate(chips):
            for w in range(n):
                sent.append(_remote(ins[w].at[pl.ds(c * halves[w], halves[w]), :], part(w, me, c),
                                    send_sems, recv_sems, w * 3 + j, (*chip, c)))
            sent.append(_remote(cw, ocw.at[me], send_sems, recv_sems, 6 * n + j, (*chip, c)))
        for cp in sent:
            cp.start()
        for j, chip in enumerate(chips):
            kj = 2 * chip[0] + chip[1]
            for w in range(n):
                _remote(part(w, kj, c), part(w, kj, c), send_sems, recv_sems, w * 3 + j, (*chip, c)).wait_recv()
                fwd = _remote(part(w, kj, c), part(w, kj, c), send_sems, recv_sems, 3 * n + w * 3 + j, sibling)
                fwd.start()
                sent.append(fwd)
        for j, chip in enumerate(chips):
            kj = 2 * chip[0] + chip[1]
            for w in range(n):
                _remote(part(w, kj, 1 - c), part(w, kj, 1 - c), send_sems, recv_sems, 3 * n + w * 3 + j,
                        sibling).wait_recv()
            _remote(cw, ocw.at[kj], send_sems, recv_sems, 6 * n + j, (*chip, c)).wait_recv()
        for cp in sent:
            cp.wait_send()

    n_sem = 6 * n + 3
    outs = pl.pallas_call(
        body, name="gather_weights",
        in_specs=[ANY] * (n + 1), out_specs=[ANY] * (n + 1),
        out_shape=[jax.ShapeDtypeStruct((N_CHIPS,) + s.shape, s.dtype) for s in shards]
        + [jax.ShapeDtypeStruct((N_CHIPS,) + conv_w.shape, conv_w.dtype)],
        scratch_shapes=[pltpu.SemaphoreType.DMA((n_sem,)), pltpu.SemaphoreType.DMA((n_sem,))],
    )(*shards, conv_w)
    chip = 2 * lax.axis_index("x") + lax.axis_index("y")
    return [lax.dynamic_update_slice(o, s[None], (chip,) + (0,) * s.ndim) for o, s in zip(outs, [*shards, conv_w])]


def _allreduce_small(buf):
    rows = buf.shape[0]

    def body(in_ref, out_ref, slots, send_sems, recv_sems):
        x, y, c, _ = _place()
        me = 4 * x + 2 * y + c
        slots[me] = in_ref[...]
        sent = []
        for p in range(1, 8):
            to = (x ^ (p >> 2), y ^ ((p >> 1) & 1), c ^ (p & 1))
            sent.append(_remote(in_ref, slots.at[me], send_sems, recv_sems, p, to))
        for cp in sent:
            cp.start()
        for p in range(1, 8):
            frm = 4 * (x ^ (p >> 2)) + 2 * (y ^ ((p >> 1) & 1)) + (c ^ (p & 1))
            _remote(in_ref, slots.at[frm], send_sems, recv_sems, p, (x, y, c)).wait_recv()
        for cp in sent:
            cp.wait_send()
        acc = slots[0]
        for d in range(1, 8):
            acc = acc + slots[d]
        out_ref[...] = acc

    vm = pl.BlockSpec(memory_space=pltpu.VMEM)
    return pl.pallas_call(
        body, name="allreduce_small", in_specs=[vm], out_specs=vm,
        out_shape=jax.ShapeDtypeStruct(buf.shape, F32),
        scratch_shapes=[pltpu.VMEM((8, rows, 128), F32), pltpu.SemaphoreType.DMA((8,)), pltpu.SemaphoreType.DMA((8,))],
    )(buf)


def _sibling_peer():
    x, y, c, _ = _place()
    return [(x, y, 1 - c)]


def _chip_peers():
    x, y, c, chips = _place()
    return [(*chip, c) for chip in chips]


def _handshake(peers):
    barrier = pltpu.get_barrier_semaphore()
    for peer in peers:
        pl.semaphore_signal(barrier, inc=1, device_id=peer, device_id_type=MESH)
    pl.semaphore_wait(barrier, len(peers))


def _pair_exchange(gs, name, barrier_id):
    n = len(gs)
    halves = [g.shape[1] // 2 for g in gs]

    def body(*refs):
        g, got = refs[:n], refs[n:2 * n]
        send_sems, recv_sems = refs[2 * n:]
        _handshake(_sibling_peer())
        x, y, c, _ = _place()
        cps = [_remote(g[w].at[:, pl.ds((1 - c) * halves[w], halves[w]), :], got[w], send_sems, recv_sems, w,
                       (x, y, 1 - c)) for w in range(n)]
        for cp in cps:
            cp.start()
        for cp in cps:
            cp.wait()

    return pl.pallas_call(
        body, name=name, in_specs=[ANY] * n, out_specs=[ANY] * n,
        out_shape=[jax.ShapeDtypeStruct((N_CHIPS, h, g.shape[2]), g.dtype) for g, h in zip(gs, halves)],
        scratch_shapes=[pltpu.SemaphoreType.DMA((n,)), pltpu.SemaphoreType.DMA((n,))],
        compiler_params=pltpu.CompilerParams(collective_id=barrier_id),
    )(*gs)


def _core_id():
    return lax.axis_index("c").reshape(1).astype(jnp.int32)


def _pair_sum(g, got, name):
    h, C = got.shape[1:]

    def body(c_ref, g_ref, b_ref, o_ref):
        o_ref[...] = (g_ref[...].astype(F32) + b_ref[...].astype(F32)).astype(BF16)

    blk = pl.BlockSpec((1, h, C), lambda k, c_ref: (k, 0, 0))
    return pl.pallas_call(
        body, name=name,
        grid_spec=pltpu.PrefetchScalarGridSpec(
            num_scalar_prefetch=1, grid=(N_CHIPS,),
            in_specs=[pl.BlockSpec((1, h, C), lambda k, c_ref: (k, c_ref[0], 0)), blk], out_specs=blk),
        out_shape=jax.ShapeDtypeStruct(got.shape, BF16), compiler_params=_cp("arbitrary"))(_core_id(), g, got)


def _sum_partials(g, got, landed, name):
    h, C = got.shape[1:]

    def body(ids, g_ref, b_ref, l_ref, o_ref):
        acc = g_ref[0].astype(F32) + b_ref[0].astype(F32)
        for j in range(3):
            acc = acc + l_ref[j].astype(F32)
        o_ref[...] = acc

    ids = jnp.stack([2 * lax.axis_index("x") + lax.axis_index("y"), lax.axis_index("c")]).astype(jnp.int32)
    return pl.pallas_call(
        body, name=name,
        grid_spec=pltpu.PrefetchScalarGridSpec(
            num_scalar_prefetch=1, grid=(1,),
            in_specs=[pl.BlockSpec((1, h, C), lambda i, ids: (ids[0], ids[1], 0)),
                      pl.BlockSpec((1, h, C), lambda i, ids: (ids[0], 0, 0)),
                      pl.BlockSpec((3, h, C), lambda i, ids: (0, 0, 0))],
            out_specs=pl.BlockSpec((h, C), lambda i, ids: (ids[1], 0))),
        out_shape=jax.ShapeDtypeStruct((2 * h, C), F32), compiler_params=_cp("arbitrary"))(ids, g, got, landed)


def _pair_share(reds, name, barrier_id):
    n = len(reds)

    def body(*refs):
        out = refs[n:2 * n]
        send_sems, recv_sems = refs[2 * n:]
        _handshake(_sibling_peer())
        x, y, c, _ = _place()
        def half(w, which):
            h = out[w].shape[0] // 2
            return out[w].at[pl.ds(which * h, h), :]

        cps = [_remote(half(w, c), half(w, c), send_sems, recv_sems, w, (x, y, 1 - c)) for w in range(n)]
        for cp in cps:
            cp.start()
        for w in range(n):
            _remote(half(w, 1 - c), half(w, 1 - c), send_sems, recv_sems, w, (x, y, 1 - c)).wait_recv()
        for cp in cps:
            cp.wait_send()

    return pl.pallas_call(
        body, name=name, in_specs=[ANY] * n, out_specs=[ANY] * n,
        out_shape=[jax.ShapeDtypeStruct(r.shape, F32) for r in reds],
        input_output_aliases={w: w for w in range(n)},
        scratch_shapes=[pltpu.SemaphoreType.DMA((n,)), pltpu.SemaphoreType.DMA((n,))],
        compiler_params=pltpu.CompilerParams(collective_id=barrier_id),
    )(*reds)


HBM = pl.BlockSpec(memory_space=pltpu.HBM)
SEM = pl.BlockSpec(memory_space=pltpu.SEMAPHORE)
DATAFLOW = pltpu.SideEffectType.DATAFLOW_SIDE_EFFECTING


def _copies_start(name, srcs, lands, plan, n_copies, after, peers, barrier_id):
    ns, nb, na = len(srcs), len(srcs) + len(lands), len(after)

    def body(*refs):
        src_refs, land_refs = refs[:ns], refs[ns:nb]
        send_sems, recv_sems = refs[nb + na:nb + na + 2]
        token = refs[-1]
        _handshake(peers())
        for k, (src, there, _, to) in enumerate(plan(src_refs, land_refs)):
            _remote(src, there, send_sems, recv_sems, k, to).start()
        token[...] = jnp.zeros_like(token)

    hbm = lambda a: pltpu.HBM(a.shape, a.dtype)
    outs = pl.pallas_call(
        body, name=name,
        out_shape=(pltpu.SemaphoreType.DMA((n_copies,)), pltpu.SemaphoreType.DMA((n_copies,)),
                   *[hbm(a) for a in srcs], *[hbm(a) for a in lands], jax.ShapeDtypeStruct((8, 128), F32)),
        in_specs=[HBM] * nb + [ANY] * na,
        out_specs=(SEM, SEM, *[HBM] * nb, pl.BlockSpec(memory_space=pltpu.VMEM)),
        input_output_aliases={i: 2 + i for i in range(nb)},
        compiler_params=pltpu.CompilerParams(has_side_effects=DATAFLOW, collective_id=barrier_id),
    )(*[pltpu.with_memory_space_constraint(a, pltpu.HBM) for a in (*srcs, *lands)], *after)
    return outs[0], outs[1], outs[2:2 + ns], outs[2 + ns:2 + nb], outs[-1]


def _copies_wait(name, send_sems, recv_sems, srcs, lands, plan, after):
    ns, nb, na = len(srcs), len(srcs) + len(lands), len(after)

    def body(*refs):
        src_refs, land_refs = refs[:ns], refs[ns:nb]
        send_sems, recv_sems = refs[nb:nb + 2]
        for k, (src, _, here, to) in enumerate(plan(src_refs, land_refs)):
            cp = _remote(src, here, send_sems, recv_sems, k, to)
            cp.wait_send()
            cp.wait_recv()

    hbm = lambda a: pltpu.HBM(a.shape, a.dtype)
    outs = pl.pallas_call(
        body, name=name,
        out_shape=(*[hbm(a) for a in srcs], *[hbm(a) for a in lands]),
        in_specs=[HBM] * nb + [SEM, SEM] + [ANY] * na,
        out_specs=tuple([HBM] * nb),
        input_output_aliases={i: i for i in range(nb)},
        compiler_params=pltpu.CompilerParams(has_side_effects=DATAFLOW),
    )(*srcs, *lands, send_sems, recv_sems, *after)
    return outs[:ns], outs[ns:]


def _gather_plan(halves):
    def plan(shards, lands):
        x, y, c, chips = _place()
        me = 2 * x + y
        copies = []
        for w, h in enumerate(halves):
            rows = pl.ds(c * h, h)
            for chip in chips:
                copies.append((shards[w].at[rows, :], lands[w].at[me, rows, :],
                               lands[w].at[2 * chip[0] + chip[1], rows, :], (*chip, c)))
        return copies
    return plan


def _reduce_plan(n):
    def plan(ps, lands):
        x, y, c, chips = _place()
        return [(ps[w].at[2 * chip[0] + chip[1]], lands[w].at[j], lands[w].at[j], (*chip, c))
                for w in range(n) for j, chip in enumerate(chips)]
    return plan


def _forward_plan(halves):
    def plan(_, lands):
        x, y, c, chips = _place()

        def part(w, chip, half):
            return lands[w].at[2 * chip[0] + chip[1], pl.ds(half * halves[w], halves[w]), :]

        return [(part(w, chip, c), part(w, chip, c), part(w, chip, 1 - c), (x, y, 1 - c))
                for w in range(len(halves)) for chip in chips]
    return plan


def _pair_plan(halves):
    def plan(gs, gots):
        x, y, c, _ = _place()
        return [(gs[w].at[:, pl.ds((1 - c) * h, h), :], gots[w], gots[w], (x, y, 1 - c)) for w, h in enumerate(halves)]
    return plan


def _place_own(gathered, shards):
    chip = 2 * lax.axis_index("x") + lax.axis_index("y")
    return [lax.dynamic_update_slice(o, s[None], (chip, 0, 0)) for o, s in zip(gathered, shards)]


def _adamw(w, g, m, v, name, tr=None):
    R, C = w.shape
    tr = tr or R // 4

    def body(w_ref, g_ref, m_ref, v_ref, d_ref, nm_ref, nv_ref):
        gv = g_ref[...]
        nm = ADAM_B1 * m_ref[...] + (1.0 - ADAM_B1) * gv
        nv = ADAM_B2 * v_ref[...] + (1.0 - ADAM_B2) * (gv * gv)
        m_hat = nm / (1.0 - ADAM_B1 ** ADAM_STEP)
        v_hat = nv / (1.0 - ADAM_B2 ** ADAM_STEP)
        d_ref[...] = -ADAM_LR * (m_hat / (jnp.sqrt(v_hat) + ADAM_EPS) + ADAM_WD * w_ref[...])
        nm_ref[...] = nm
        nv_ref[...] = nv

    blk = pl.BlockSpec((tr, C), lambda i: (i, 0))
    return pl.pallas_call(body, name=name, grid=(R // tr,), in_specs=[blk] * 4, out_specs=[blk] * 3,
                          out_shape=[jax.ShapeDtypeStruct((R, C), F32)] * 3, compiler_params=_cp("arbitrary"))(w, g, m, v)


SMALL = (("norm1_g", 1, 1024), ("attn_norm_g", 1, 512), ("hgrn_norm_g", 1, 512), ("hgrn_lb_logits", 2, 512),
         ("norm2_g", 1, 1024), ("conv_b", 1, D_FF), ("final_norm_g", 1, 1024), ("conv_w", 3, D_FF))
LOSS_ROW = sum(r * c for _, r, c in SMALL) // 128
SMALL_ROWS = 136


def _rows_to_lanes(ref, row, width):
    return jnp.concatenate([ref[row + j:row + j + 1, :] for j in range(width // 128)], axis=1)


def _pack_small(grads, dlb, lb, loss):
    def body(*refs):
        parts, dlb_ref, lb_ref, loss_ref, out = refs[:len(SMALL) - 1], refs[-4], refs[-3], refs[-2], refs[-1]
        out[...] = jnp.zeros_like(out)
        lbv = lb_ref[...]
        dl = dlb_ref[...] * lbv * (1.0 - lbv)
        row = 0
        parts = list(parts)
        for name, rows, width in SMALL:
            for r in range(rows):
                if name == "hgrn_lb_logits":
                    src = dl if r == 0 else -dl
                    for j in range(width // 128):
                        out[row + j:row + j + 1, :] = src[:, 128 * j:128 * (j + 1)]
                else:
                    for j in range(width // 128):
                        out[row + j:row + j + 1, :] = parts[0][r:r + 1, 128 * j:128 * (j + 1)]
                row += width // 128
            if name != "hgrn_lb_logits":
                parts.pop(0)
        out[LOSS_ROW:LOSS_ROW + 1, :] = loss_ref[...]

    vm = pl.BlockSpec(memory_space=pltpu.VMEM)
    return pl.pallas_call(body, name="pack_small", in_specs=[vm] * (len(grads) + 3), out_specs=vm,
                          out_shape=jax.ShapeDtypeStruct((SMALL_ROWS, 128), F32))(*grads, dlb, lb, loss)


def _adamw_math(w, g, m, v):
    nm = ADAM_B1 * m + (1.0 - ADAM_B1) * g
    nv = ADAM_B2 * v + (1.0 - ADAM_B2) * (g * g)
    m_hat = nm / (1.0 - ADAM_B1 ** ADAM_STEP)
    v_hat = nv / (1.0 - ADAM_B2 ** ADAM_STEP)
    return -ADAM_LR * (m_hat / (jnp.sqrt(v_hat) + ADAM_EPS) + ADAM_WD * w), nm, nv


def _small_update(summed, g_conv_w, ws, ms, vs):
    n = len(SMALL)

    def body(*refs):
        s_ref, gcw_ref = refs[:2]
        w_refs, m_refs, v_refs = refs[2:2 + n], refs[2 + n:2 + 2 * n], refs[2 + 2 * n:2 + 3 * n]
        outs = refs[2 + 3 * n:]
        row = 0
        for k, (name, rows, width) in enumerate(SMALL):
            if name == "conv_w":
                g = gcw_ref[...]
            else:
                g = jnp.concatenate([_rows_to_lanes(s_ref, row + r * (width // 128), width) for r in range(rows)], axis=0)
            row += rows * (width // 128)
            d, nm, nv = _adamw_math(w_refs[k][...], g, m_refs[k][...], v_refs[k][...])
            for o, val in zip(outs[4 * k:4 * k + 4], (g, d, nm, nv)):
                o[...] = val

    vm = pl.BlockSpec(memory_space=pltpu.VMEM)
    outs = pl.pallas_call(
        body, name="small_update", in_specs=[vm] * (2 + 3 * n), out_specs=[vm] * (4 * n),
        out_shape=[jax.ShapeDtypeStruct(a.shape, F32) for a in ws for _ in range(4)],
    )(summed, g_conv_w, *ws, *ms, *vs)
    return [outs[4 * k:4 * k + 4] for k in range(n)]


def kernel(x, norm1_g, w_in, attn_norm_g, hgrn_norm_g, hgrn_lb_logits, w_out, norm2_g, w_up, conv_w, conv_b, w_down, final_norm_g, loss_target, m_norm1_g, m_w_in, m_attn_norm_g, m_hgrn_norm_g, m_hgrn_lb_logits, m_w_out, m_norm2_g, m_w_up, m_conv_w, m_conv_b, m_w_down, m_final_norm_g, v_norm1_g, v_w_in, v_attn_norm_g, v_hgrn_norm_g, v_hgrn_lb_logits, v_w_out, v_norm2_g, v_w_up, v_conv_w, v_conv_b, v_w_down, v_final_norm_g):
    w = dict(norm1_g=norm1_g, w_in=w_in, attn_norm_g=attn_norm_g, hgrn_norm_g=hgrn_norm_g,
             hgrn_lb_logits=hgrn_lb_logits, w_out=w_out, norm2_g=norm2_g, w_up=w_up, conv_w=conv_w, conv_b=conv_b,
             w_down=w_down, final_norm_g=final_norm_g)
    m = dict(norm1_g=m_norm1_g, w_in=m_w_in, attn_norm_g=m_attn_norm_g, hgrn_norm_g=m_hgrn_norm_g,
             hgrn_lb_logits=m_hgrn_lb_logits, w_out=m_w_out, norm2_g=m_norm2_g, w_up=m_w_up, conv_w=m_conv_w,
             conv_b=m_conv_b, w_down=m_w_down, final_norm_g=m_final_norm_g)
    v = dict(norm1_g=v_norm1_g, w_in=v_w_in, attn_norm_g=v_attn_norm_g, hgrn_norm_g=v_hgrn_norm_g,
             hgrn_lb_logits=v_hgrn_lb_logits, w_out=v_w_out, norm2_g=v_norm2_g, w_up=v_w_up, conv_w=v_conv_w,
             conv_b=v_conv_b, w_down=v_w_down, final_norm_g=v_final_norm_g)
    names = list(w)
    chip = 2 * lax.axis_index("x") + lax.axis_index("y")

    shards = {k: w[k][0].astype(BF16) for k in BIG}
    w_in4, conv_w4 = _gather_weights([shards["w_in"]], conv_w[0])
    conv_w_full = jnp.transpose(conv_w4, (1, 0, 2)).reshape(3, D_FF)
    lb = jax.nn.softmax(hgrn_lb_logits, axis=0)[0:1]
    late = [shards[k] for k in BIG[1:]]
    gather_plan = _gather_plan([s.shape[0] // 2 for s in late])
    started = _copies_start("gather_start", late, [lax.empty((N_CHIPS,) + s.shape, BF16) for s in late], gather_plan,
                            3 * len(late), after=(w_in4,), peers=_chip_peers, barrier_id=0)
    u1, qkv, hg = _in_proj(x[0], norm1_g + started[4][0:1, 0:1], w_in4)
    attn_o, lse = _attn_fwd(qkv)
    late, landed_w = _copies_wait("gather_wait", *started[:4], gather_plan, after=(attn_o,))
    forward_plan = _forward_plan([s.shape[0] // 2 for s in late])
    started = _copies_start("forward_start", [], landed_w, forward_plan, 3 * len(late), after=(),
                            peers=_sibling_peer, barrier_id=1)
    rec_o, states = _hgrn_fwd(hg, lb + started[4][0:1, 0:1])
    a = dict(u1=u1, qkv=qkv, hg=hg, attn_o=attn_o, lse=lse, rec_o=rec_o, states=states)
    w_out4, w_up4, w_down4 = _place_own(
        _copies_wait("forward_wait", *started[:4], forward_plan, after=(rec_o,))[1], late)

    b = _step_channel(a, x[0], loss_target[0], attn_norm_g, hgrn_norm_g, w_out4.reshape(D_MODEL, D_MODEL), norm2_g,
                      w_up4, conv_w_full, conv_b, w_down4.reshape(D_FF, D_MODEL), final_norm_g.reshape(1, D_MODEL))

    early = [b["dw_out"], b["dw_up"], b["dw_down"]]
    pair_plan = _pair_plan([gk.shape[1] // 2 for gk in early])
    started = _copies_start("pair_start", early,
                            [lax.empty((N_CHIPS, gk.shape[1] // 2, gk.shape[2]), BF16) for gk in early], pair_plan,
                            len(early), after=(), peers=_sibling_peer, barrier_id=2)
    dqkv = _attn_bwd(qkv, attn_o, lse, b["da"], started[4])
    early, gots = _copies_wait("pair_wait", *started[:4], pair_plan, after=(dqkv[0],))
    ps = [_pair_sum(gk, got, f"pair_sum_{k}") for gk, got, k in zip(early, gots, BIG[1:])]
    reduce_plan = _reduce_plan(len(ps))
    started = _copies_start("reduce_start", ps, [lax.empty((3,) + p.shape[1:], BF16) for p in ps], reduce_plan,
                            3 * len(ps), after=(), peers=_chip_peers, barrier_id=3)
    c = _step_mixers_bwd(a, b, x[0], norm1_g, w_in4, lb + started[4][0:1, 0:1], dqkv)
    gots_in = _pair_exchange([c["dw_in"]], "pair_exchange_w_in", barrier_id=4)
    ps_in = _pair_sum(c["dw_in"], gots_in[0], "pair_sum_w_in")
    plan_in = _reduce_plan(1)
    started_in = _copies_start("reduce_start_w_in", [ps_in], [lax.empty((3,) + ps_in.shape[1:], BF16)], plan_in, 3,
                               after=(), peers=_chip_peers, barrier_id=5)
    landed = _copies_wait("reduce_wait", *started[:4], reduce_plan, after=(started_in[4],))[1]
    reds = [_sum_partials(gk, got, l, f"sum_partials_{k}") for gk, got, l, k in zip(early, gots, landed, BIG[1:])]
    g = dict(zip(BIG[1:], _pair_share(reds, "pair_share", barrier_id=6)))
    delta, new_m, new_v = {}, {}, {}
    for k in BIG[1:]:
        delta[k], new_m[k], new_v[k] = _adamw(w[k][0], g[k], m[k][0], v[k][0], f"adamw_{k}")

    loss, dx = b["loss"], c["dx"]
    small = dict(g1=c["dg1"], g_a=b["dga"], g_h=b["dgh"], lb=c["dlb"], g2=b["dg2"], conv_w=b["dcw"], conv_b=b["dcb"],
                 gf=b["dgf"])
    summed = _allreduce_small(_pack_small(
        [small["g1"], small["g_a"], small["g_h"], small["g2"], small["conv_b"], small["gf"], small["conv_w"]],
        small["lb"], lb, loss))
    loss_total = summed[LOSS_ROW, 0]
    g_conv_w = lax.dynamic_slice(summed[LOSS_ROW - 3 * D_FF // 128:LOSS_ROW].reshape(3, D_FF),
                                 (0, chip * (D_FF // N_CHIPS)), (3, D_FF // N_CHIPS))
    two_d = lambda p, k: p[k].reshape(-1, p[k].shape[-1])
    updated = _small_update(summed, g_conv_w, *[[two_d(p, k) for k, _, _ in SMALL] for p in (w, m, v)])
    for (k, _, _), parts in zip(SMALL, updated):
        g[k], delta[k], new_m[k], new_v[k] = (a.reshape(w[k].shape) for a in parts)

    landed_in = _copies_wait("reduce_wait_w_in", *started_in[:4], plan_in, after=(updated[0][1], delta["w_up"]))[1]
    red_in = _sum_partials(c["dw_in"], gots_in[0], landed_in[0], "sum_partials_w_in")
    g["w_in"] = _pair_share([red_in], "pair_share_w_in", barrier_id=7)[0]
    delta["w_in"], new_m["w_in"], new_v["w_in"] = _adamw(w_in[0], g["w_in"], m_w_in[0], v_w_in[0], "adamw_w_in")
    for k in BIG:
        g[k], delta[k], new_m[k], new_v[k] = g[k][None], delta[k][None], new_m[k][None], new_v[k][None]

    return (loss_total, dx[None], *[g[k] for k in names], *[delta[k] for k in names],
            *[new_m[k] for k in names], *[new_v[k] for k in names])
```

```python
import functools
import math

import jax
import jax.numpy as jnp
from jax import lax
from jax.experimental import pallas as pl
from jax.experimental.pallas import tpu as pltpu

F32 = jnp.float32
BF16 = jnp.bfloat16

D_MODEL = 1024
ATTN_W = 512
HGRN_W = 512
HEAD_PAIR = 128
ATTN_BLK = 128
DILATIONS = (1, 4, 16)
ATTN_CHAINS = 4
ATTN_CHAINS_FWD = 8
HGRN_HEADS = 4
HGRN_DIM = 128
HGRN_CHUNK = 64
SUPER = 256
HGRN_SIDE = 4
D_FF = 2816
FF_CHUNKS = ((0, 1536), (1536, D_FF))
N_CHIPS = 4
IN_TOTAL = 3584
IN_SHARD = IN_TOTAL // N_CHIPS
UP_SHARD = 2 * D_FF // N_CHIPS
QKV_W = 3 * ATTN_W
HG_W = 4 * HGRN_W
EPS = 1e-6
NEG = -1e30
V7X_VMEM_BYTES = 64 * 1024 * 1024
VMEM_LIMIT = V7X_VMEM_BYTES - 8 * 1024 * 1024

ADAM_LR = 0.001
ADAM_B1 = 0.9
ADAM_B2 = 0.999
ADAM_EPS = 1e-08
ADAM_WD = 0.01
ADAM_STEP = 10

MESH = pl.DeviceIdType.MESH


def _cp(*sem):
    return pltpu.CompilerParams(dimension_semantics=sem or None, vmem_limit_bytes=VMEM_LIMIT)


def _dot(a, b):
    return jnp.dot(a, b, preferred_element_type=F32)


def _dot_nt(a, b):
    return lax.dot_general(a, b, (((1,), (1,)), ((), ())), preferred_element_type=F32)


def _dot_tn(a, b):
    return lax.dot_general(a, b, (((0,), (0,)), ((), ())), preferred_element_type=F32)


def _sigmoid(x):
    return 1.0 / (1.0 + jnp.exp(-x))


def _rms(x, width):
    return lax.rsqrt(jnp.sum(x * x, axis=-1, keepdims=True) * (1.0 / width) + EPS)


def _rms_bwd(dn, n, r, width):
    return r * (dn - n * (jnp.sum(dn * n, axis=-1, keepdims=True) * (1.0 / width)))


def _colsum(x):
    return jnp.sum(x, axis=0, keepdims=True)


def _row(v, k):
    rid = lax.broadcasted_iota(jnp.int32, v.shape, 0)
    return jnp.sum(jnp.where(rid == k, v, 0.0), axis=0, keepdims=True)


def _full(shape):
    return pl.BlockSpec(shape, lambda *_: (0,) * len(shape))


def _once(shape):
    return pl.BlockSpec(shape, lambda *_: (0,) * len(shape), pipeline_mode=pl.Buffered(1))


def _load_side_by_side(w_hbm, w_full, sem):
    width = w_hbm.shape[2]
    cps = [pltpu.make_async_copy(w_hbm.at[k], w_full.at[:, pl.ds(k * width, width)], sem.at[k]) for k in range(N_CHIPS)]
    for cp in cps:
        cp.start()
    for cp in cps:
        cp.wait()


def _in_proj(x, g1, w_in4, tm=512):
    T = x.shape[0]

    def body(x_ref, g_ref, w_hbm, u_ref, qkv_ref, hg_ref, w_full, sem):
        @pl.when(pl.program_id(0) == 0)
        def _():
            _load_side_by_side(w_hbm, w_full, sem)

        xv = x_ref[...]
        u = (xv * _rms(xv, D_MODEL) * g_ref[...]).astype(BF16)
        u_ref[...] = u
        p = _dot(u, w_full[...])
        qkv_ref[...] = p[:, :QKV_W]
        hg_ref[...] = p[:, QKV_W:]

    return pl.pallas_call(
        body, name="in_proj", grid=(T // tm,),
        in_specs=[pl.BlockSpec((tm, D_MODEL), lambda i: (i, 0)), _full((1, D_MODEL)), ANY],
        out_specs=[pl.BlockSpec((tm, D_MODEL), lambda i: (i, 0)), pl.BlockSpec((tm, QKV_W), lambda i: (i, 0)),
                   pl.BlockSpec((tm, HG_W), lambda i: (i, 0))],
        out_shape=[jax.ShapeDtypeStruct((T, D_MODEL), BF16), jax.ShapeDtypeStruct((T, QKV_W), F32),
                   jax.ShapeDtypeStruct((T, HG_W), F32)],
        scratch_shapes=[pltpu.VMEM((D_MODEL, IN_TOTAL), BF16), pltpu.SemaphoreType.DMA((N_CHIPS,))],
        compiler_params=_cp("arbitrary"),
    )(x, g1, w_in4)


def _attn_masks(bias_ref):
    lane = lax.broadcasted_iota(jnp.int32, (ATTN_BLK, HEAD_PAIR), 1)
    row = lax.broadcasted_iota(jnp.int32, (2 * ATTN_BLK, 2 * ATTN_BLK), 0)
    col = lax.broadcasted_iota(jnp.int32, (2 * ATTN_BLK, 2 * ATTN_BLK), 1)
    base = jnp.where(row >= ATTN_BLK, row - ATTN_BLK, row) - col
    for k in range(2):
        dist = base + k * ATTN_BLK
        bias_ref[k] = jnp.where((dist >= 0) & (dist <= ATTN_BLK), 0.0, NEG)
    bias_ref[2] = jnp.where(col >= ATTN_BLK, bias_ref[1], NEG)
    return lane < 64


def _two_heads(blk, first):
    zero = jnp.zeros_like(blk)
    return jnp.concatenate([jnp.where(first, blk, zero), jnp.where(first, zero, blk)], axis=0)


def _attn_rows(idx, nb, d):
    r, n = idx // nb, idx % nb
    kb = jnp.maximum(n - 1, 0)
    if d == 1:
        q0 = pl.multiple_of(n * ATTN_BLK, ATTN_BLK)
        k0 = pl.multiple_of(kb * ATTN_BLK, ATTN_BLK)
        return pl.ds(q0, ATTN_BLK), pl.ds(k0, 2 * ATTN_BLK), n - kb
    return (pl.ds(r + d * ATTN_BLK * n, ATTN_BLK, stride=d), pl.ds(r + d * ATTN_BLK * kb, 2 * ATTN_BLK, stride=d),
            n - kb)


def _attn_fwd(qkv):
    T = qkv.shape[0]

    n_blocks = T // ATTN_BLK

    def body(q_ref, k_ref, v_ref, o_ref, m_ref, l_ref, bias_ref):
        first = _attn_masks(bias_ref)
        for bi, d in enumerate(DILATIONS):
            nb = T // d // ATTN_BLK

            chains = ATTN_CHAINS_FWD
            per_chain = n_blocks // chains
            carried = d > 1 and per_chain % nb == 0

            def block(idx, kept=None, d=d, nb=nb, bi=bi, carried=carried):
                rows, keys, which = _attn_rows(idx, nb, d)
                q2 = _two_heads(q_ref[rows, :] * 0.125, first).astype(BF16)
                if carried:
                    k_own, v_own = k_ref[rows, :].astype(BF16), v_ref[rows, :].astype(BF16)
                    kw = jnp.concatenate([kept[0], k_own], axis=0)
                    vw = jnp.concatenate([kept[1], v_own], axis=0)
                    which = 2 - which
                else:
                    kw = k_ref[keys, :].astype(BF16)
                    vw = v_ref[keys, :].astype(BF16)
                old = (o_ref[rows, :], m_ref[rows, :], l_ref[rows, :]) if bi else None
                s = _dot_nt(q2, kw) + bias_ref[which]
                mb = jnp.max(s, axis=-1, keepdims=True)
                p = jnp.exp(s - mb)
                lb = jnp.sum(p, axis=-1, keepdims=True)
                o2 = _dot(p.astype(BF16), vw)
                o = jnp.where(first, o2[:ATTN_BLK], o2[ATTN_BLK:])
                m = jnp.where(first, mb[:ATTN_BLK], mb[ATTN_BLK:])
                l = jnp.where(first, lb[:ATTN_BLK], lb[ATTN_BLK:])
                if bi:
                    po, pm, pl_ = old
                    mn = jnp.maximum(pm, m)
                    wa = jnp.exp(pm - mn)
                    wb = jnp.exp(m - mn)
                    o, l, m = po * wa + o * wb, pl_ * wa + l * wb, mn
                return (rows, o, m, l), ((k_own, v_own) if carried else 0)

            def step(i, kept, block=block, carried=carried, chains=chains, per_chain=per_chain):
                done = [block(i + ch * per_chain, kept[ch] if carried else None) for ch in range(chains)]
                for (rows, o, m, l), _ in done:
                    o_ref[rows, :] = o
                    m_ref[rows, :] = m
                    l_ref[rows, :] = l
                return tuple(k for _, k in done) if carried else kept

            zero = jnp.zeros((ATTN_BLK, HEAD_PAIR), BF16)
            lax.fori_loop(0, per_chain, step, ((zero, zero),) * chains if carried else 0)

        def finish(i, carry):
            rows = pl.ds(pl.multiple_of(i * SUPER, SUPER), SUPER)
            l = l_ref[rows, :]
            o_ref[rows, :] = o_ref[rows, :] / l
            m_ref[rows, :] = m_ref[rows, :] + jnp.log(l)
            return carry

        lax.fori_loop(0, T // SUPER, finish, 0)

    col = lambda off: pl.BlockSpec((T, HEAD_PAIR), lambda j: (0, off + j))
    return pl.pallas_call(
        body, name="attn_fwd", grid=(4,),
        in_specs=[col(0), col(4), col(8)], out_specs=[col(0), col(0)],
        out_shape=[jax.ShapeDtypeStruct((T, ATTN_W), F32)] * 2,
        scratch_shapes=[pltpu.VMEM((T, HEAD_PAIR), F32), pltpu.VMEM((3, 2 * ATTN_BLK, 2 * ATTN_BLK), F32)],
        compiler_params=_cp("arbitrary"),
    )(qkv, qkv, qkv)


def _attn_bwd(qkv, o, lse, do, token=None):
    T = qkv.shape[0]
    per_chain = T // ATTN_BLK // ATTN_CHAINS
    extra = [] if token is None else [token]

    def body(q_ref, k_ref, v_ref, o_ref, lse_ref, do_ref, *rest):
        outs = rest[len(extra):len(extra) + 3]
        dq_ref, dk_ref, dv_ref, dkb_ref, dvb_ref, bias_ref = rest[len(extra) + 3:]
        first = _attn_masks(bias_ref)
        dq_ref[...] = jnp.zeros_like(dq_ref)
        dk_ref[...] = jnp.zeros_like(dk_ref)
        dv_ref[...] = jnp.zeros_like(dv_ref)

        def grads(rows, kw, vw, which):
            q2 = _two_heads(q_ref[rows, :] * 0.125, first).astype(BF16)
            lse_b = lse_ref[rows, :]
            dob = do_ref[rows, :]
            prod = dob * o_ref[rows, :]
            old = dq_ref[rows, :]
            lse2 = jnp.concatenate(
                [jnp.max(jnp.where(first, lse_b, NEG), axis=-1, keepdims=True),
                 jnp.max(jnp.where(first, NEG, lse_b), axis=-1, keepdims=True)], axis=0)
            p = jnp.exp(_dot_nt(q2, kw) + (bias_ref[which] - lse2))
            delta = jnp.concatenate(
                [jnp.sum(jnp.where(first, prod, 0.0), axis=-1, keepdims=True),
                 jnp.sum(jnp.where(first, 0.0, prod), axis=-1, keepdims=True)], axis=0)
            do2 = _two_heads(dob, first).astype(BF16)
            ds = (p * (_dot_nt(do2, vw) - delta)).astype(BF16)
            dq2 = _dot(ds, kw) * 0.125
            return (old + jnp.where(first, dq2[:ATTN_BLK], dq2[ATTN_BLK:]), _dot_tn(ds, q2),
                    _dot_tn(p.astype(BF16), do2))

        def block(idx):
            rows, keys, which = _attn_rows(idx, T // ATTN_BLK, 1)
            old = dk_ref[keys, :], dv_ref[keys, :]
            dq, ck, cv = grads(rows, k_ref[keys, :].astype(BF16), v_ref[keys, :].astype(BF16), which)
            return rows, keys, dq, old[0] + ck, old[1] + cv

        def step(i, carry):
            done = [block(i + ch * per_chain) for ch in range(ATTN_CHAINS)]
            for rows, keys, dq, dk, dv in done:
                dq_ref[rows, :] = dq
                dk_ref[keys, :] = dk
                dv_ref[keys, :] = dv
            return carry

        lax.fori_loop(0, per_chain, step, 0)

        for d in DILATIONS[1:]:
            nb = T // d // ATTN_BLK

            def block(idx, kept, d=d, nb=nb):
                r, n = idx // nb, idx % nb
                rows = pl.ds(r + d * ATTN_BLK * n, ATTN_BLK, stride=d)
                before = pl.ds(r + d * ATTN_BLK * jnp.maximum(n - 1, 0), ATTN_BLK, stride=d)
                k_prev, v_prev, dk_prev, dv_prev = kept
                k_own, v_own = k_ref[rows, :].astype(BF16), v_ref[rows, :].astype(BF16)
                dq, ck, cv = grads(rows, jnp.concatenate([k_prev, k_own], axis=0),
                                   jnp.concatenate([v_prev, v_own], axis=0), jnp.where(n > 0, 1, 2))
                stores = (rows, before, dq, dk_prev + ck[:ATTN_BLK], dv_prev + cv[:ATTN_BLK], ck[ATTN_BLK:], cv[ATTN_BLK:])
                return stores, (k_own, v_own, ck[ATTN_BLK:], cv[ATTN_BLK:])

            def step(i, kept, block=block):
                done = [block(i + ch * per_chain, kept[ch]) for ch in range(ATTN_CHAINS)]
                for (rows, before, dq, dk_done, dv_done, dk_own, dv_own), _ in done:
                    dq_ref[rows, :] = dq
                    dkb_ref[before, :] = dk_done
                    dvb_ref[before, :] = dv_done
                    dkb_ref[rows, :] = dk_own
                    dvb_ref[rows, :] = dv_own
                return tuple(k for _, k in done)

            zero = jnp.zeros((ATTN_BLK, HEAD_PAIR), F32)
            lax.fori_loop(0, per_chain, step, ((zero.astype(BF16), zero.astype(BF16), zero, zero),) * ATTN_CHAINS)

            def add(i, carry):
                rows = pl.ds(pl.multiple_of(i * SUPER, SUPER), SUPER)
                dk_ref[rows, :] += dkb_ref[rows, :]
                dv_ref[rows, :] += dvb_ref[rows, :]
                return carry

            lax.fori_loop(0, T // SUPER, add, 0)

        def emit(i, carry):
            rows = pl.ds(pl.multiple_of(i * SUPER, SUPER), SUPER)
            for out, acc in zip(outs, (dq_ref, dk_ref, dv_ref)):
                out[rows, :] = acc[rows, :].astype(BF16)
            return carry

        lax.fori_loop(0, T // SUPER, emit, 0)

    col = lambda off: pl.BlockSpec((T, HEAD_PAIR), lambda j: (0, off + j))
    return pl.pallas_call(
        body, name="attn_bwd", grid=(4,),
        in_specs=[col(0), col(4), col(8), col(0), col(0), col(0)] + [_full(t.shape) for t in extra],
        out_specs=[col(0)] * 3,
        out_shape=[jax.ShapeDtypeStruct((T, ATTN_W), BF16)] * 3,
        scratch_shapes=[pltpu.VMEM((T, HEAD_PAIR), F32)] * 5 + [pltpu.VMEM((3, 2 * ATTN_BLK, 2 * ATTN_BLK), F32)],
        compiler_params=_cp("arbitrary"),
    )(qkv, qkv, qkv, o, lse, do, *extra)


def _chunk_ids():
    row = lax.broadcasted_iota(jnp.int32, (SUPER, HGRN_DIM), 0)
    r2 = lax.broadcasted_iota(jnp.int32, (SUPER, SUPER), 0)
    c2 = lax.broadcasted_iota(jnp.int32, (SUPER, SUPER), 1)
    amask = ((r2 // HGRN_CHUNK) == (c2 // HGRN_CHUNK)) & (c2 <= r2)
    return row % HGRN_CHUNK, row // HGRN_CHUNK, amask


def _cumsum_chunk(x, rmod):
    s = 1
    while s < HGRN_CHUNK:
        x = x + jnp.where(rmod >= s, pltpu.roll(x, s, 0), 0.0)
        s *= 2
    return x


def _suffix_sum_chunk(x, rmod):
    s = 1
    while s < HGRN_CHUNK:
        x = x + jnp.where(rmod < HGRN_CHUNK - s, pltpu.roll(x, SUPER - s, 0), 0.0)
        s *= 2
    return x


def _chunk_rows(vs, cid):
    out = vs[-1]
    for c in reversed(range(len(vs) - 1)):
        out = jnp.where(cid == c, vs[c], out)
    return out


def _expand(x, cid):
    return jnp.concatenate([jnp.where(cid == c, x, 0.0) for c in range(SUPER // HGRN_CHUNK)], axis=1)


def _hgrn_gates(q, f, lbv, rmod, cid, tmp):
    sq = _sigmoid(q)
    sg = _sigmoid(f)
    forget = lbv + (1.0 - lbv) * sg
    key = 1.0 - forget
    b = _cumsum_chunk(jnp.log(forget), rmod)
    tmp[...] = b
    bends = [tmp[c * HGRN_CHUNK + HGRN_CHUNK - 1:(c + 1) * HGRN_CHUNK, :] for c in range(SUPER // HGRN_CHUNK)]
    eb = jnp.exp(b)
    enb = jnp.exp(-b)
    ebe = jnp.exp(_chunk_rows(bends, cid) - b)
    return sq, sg, forget, key, eb, enb, ebe, q * sq * eb, key * enb, key * ebe, [jnp.exp(v) for v in bends]


def _hgrn_fwd(hg, lb):
    T = hg.shape[0]
    nsc = T // SUPER
    NC = SUPER // HGRN_CHUNK

    def body(q_ref, f_ref, i_ref, lb_ref, o_ref, st_ref, state, tmp):
        rmod, cid, amask = _chunk_ids()
        state[...] = jnp.zeros_like(state)
        lbv = lb_ref[...]

        def local(sc, u):
            rows = pl.ds(pl.multiple_of(sc * SUPER, SUPER), SUPER)
            iv = i_ref[rows, :].astype(BF16)
            qd, ki, ke, dec = _hgrn_gates(q_ref[rows, :], f_ref[rows, :], lbv, rmod, cid, tmp.at[u])[-4:]
            a = jnp.where(amask, _dot_nt(qd.astype(BF16), ki.astype(BF16)), 0.0)
            return rows, qd, dec, _dot(a.astype(BF16), iv), _dot_tn(iv, _expand(ke, cid).astype(BF16))

        def step(i, carry):
            parts = [local(i * HGRN_SIDE + u, u) for u in range(HGRN_SIDE)]
            st = state[...]
            entering = []
            for u, (_, _, dec, _, ut) in enumerate(parts):
                st_ref[0, i * HGRN_SIDE + u] = st
                sts = []
                for c in range(NC):
                    sts.append(st)
                    st = st * dec[c] + ut[:, c * HGRN_DIM:(c + 1) * HGRN_DIM]
                entering.append(jnp.concatenate(sts, axis=1).astype(BF16))
            state[...] = st
            for (rows, qd, _, o, _), sts in zip(parts, entering):
                o_ref[rows, :] = o + _dot_nt(_expand(qd, cid).astype(BF16), sts)
            return carry

        lax.fori_loop(0, nsc // HGRN_SIDE, step, 0)

    col = lambda off: pl.BlockSpec((T, HGRN_DIM), lambda h: (0, off + h))
    return pl.pallas_call(
        body, name="hgrn_fwd", grid=(HGRN_HEADS,),
        in_specs=[col(0), col(4), col(8), pl.BlockSpec((1, HGRN_DIM), lambda h: (0, h))],
        out_specs=[pl.BlockSpec((T, HGRN_DIM), lambda h: (0, h)),
                   pl.BlockSpec((1, nsc, HGRN_DIM, HGRN_DIM), lambda h: (h, 0, 0, 0))],
        out_shape=[jax.ShapeDtypeStruct((T, HGRN_W), F32),
                   jax.ShapeDtypeStruct((HGRN_HEADS, nsc, HGRN_DIM, HGRN_DIM), F32)],
        scratch_shapes=[pltpu.VMEM((HGRN_DIM, HGRN_DIM), F32), pltpu.VMEM((HGRN_SIDE, SUPER, HGRN_DIM), F32)],
        compiler_params=_cp("arbitrary"),
    )(hg, hg, hg, lb)


def _hgrn_bwd(hg, lb, states, do):
    T = hg.shape[0]
    nsc = T // SUPER
    NC = SUPER // HGRN_CHUNK

    def body(q_ref, f_ref, i_ref, lb_ref, st_ref, do_ref, dq_ref, df_ref, di_ref, dlb_ref, dstate, tmp):
        rmod, cid, amask = _chunk_ids()
        dstate[...] = jnp.zeros_like(dstate)
        dlb_ref[...] = jnp.zeros_like(dlb_ref)
        lbv = lb_ref[...]

        def local(sc, u):
            rows = pl.ds(pl.multiple_of(sc * SUPER, SUPER), SUPER)
            q = q_ref[rows, :]
            ivf = i_ref[rows, :]
            iv = ivf.astype(BF16)
            dof = do_ref[rows, :]
            dob = dof.astype(BF16)
            sq, sg, forget, key, eb, enb, ebe, qd, ki, ke, dec = _hgrn_gates(q, f_ref[rows, :], lbv, rmod, cid,
                                                                            tmp.at[u])
            qdb, kib = qd.astype(BF16), ki.astype(BF16)
            keexp = _expand(ke, cid).astype(BF16)
            a = jnp.where(amask, _dot_nt(qdb, kib), 0.0).astype(BF16)
            ut = _dot_tn(iv, keexp)
            st = st_ref[0, sc]
            sts = []
            for c in range(NC):
                sts.append(st)
                st = st * dec[c] + ut[:, c * HGRN_DIM:(c + 1) * HGRN_DIM]
            gt = _dot_tn(dob, _expand(qd, cid).astype(BF16))
            da = jnp.where(amask, _dot_nt(dob, iv), 0.0).astype(BF16)
            ststack = jnp.concatenate(sts, axis=0).astype(BF16)
            return dict(rows=rows, q=q, sq=sq, sg=sg, forget=forget, eb=eb, enb=enb, ebe=ebe, qd=qd, ki=ki, ke=ke,
                        dec=dec, sts=sts, gt=gt, keexp=keexp, ivexp=_expand(ivf, cid).astype(BF16),
                        div=_dot_tn(a, dob), dki=_dot_tn(da, qdb),
                        dqd=_dot(da, kib) + _dot(_expand(dof, cid).astype(BF16), ststack))

        def finish(p, nxt, ddec):
            ncat = jnp.concatenate(nxt, axis=1).astype(BF16)
            nstack = jnp.concatenate(nxt, axis=0).astype(BF16)
            dke = _dot(p["ivexp"], nstack)
            dkk = dke * p["ke"]
            dkey = p["dki"] * p["enb"] + dke * p["ebe"]
            db = p["dqd"] * p["qd"] - p["dki"] * p["ki"] - dkk
            dbends = [_colsum(jnp.where(cid == c, dkk, 0.0)) + ddec[c] * p["dec"][c] for c in range(NC)]
            dforget = (_suffix_sum_chunk(db, rmod) + _chunk_rows(dbends, cid)) / p["forget"] - dkey
            sg, sq, q = p["sg"], p["sq"], p["q"]
            df_ref[p["rows"], :] = (dforget * (1.0 - lbv) * sg * (1.0 - sg)).astype(BF16)
            dq_ref[p["rows"], :] = (p["dqd"] * p["eb"] * (sq * (1.0 + q * (1.0 - sq)))).astype(BF16)
            di_ref[p["rows"], :] = (p["div"] + _dot_nt(p["keexp"], ncat)).astype(BF16)
            return _colsum(dforget * (1.0 - sg))

        def step(i, carry):
            parts = [local(nsc - 1 - (i * HGRN_SIDE + u), u) for u in range(HGRN_SIDE)]
            dst = dstate[...]
            chained = []
            for p in parts:
                nxt = [None] * NC
                ddec = [None] * NC
                for c in reversed(range(NC)):
                    nxt[c] = dst
                    ddec[c] = _colsum(dst * p["sts"][c])
                    dst = dst * p["dec"][c] + p["gt"][:, c * HGRN_DIM:(c + 1) * HGRN_DIM]
                chained.append((nxt, ddec))
            dstate[...] = dst
            dlb = dlb_ref[...]
            for p, (nxt, ddec) in zip(parts, chained):
                dlb = dlb + finish(p, nxt, ddec)
            dlb_ref[...] = dlb
            return carry

        lax.fori_loop(0, nsc // HGRN_SIDE, step, 0)

    col = lambda off: pl.BlockSpec((T, HGRN_DIM), lambda h: (0, off + h))
    own = pl.BlockSpec((T, HGRN_DIM), lambda h: (0, h))
    vec = pl.BlockSpec((1, HGRN_DIM), lambda h: (0, h))
    return pl.pallas_call(
        body, name="hgrn_bwd", grid=(HGRN_HEADS,),
        in_specs=[col(0), col(4), col(8), vec,
                  pl.BlockSpec((1, nsc, HGRN_DIM, HGRN_DIM), lambda h: (h, 0, 0, 0)), own],
        out_specs=[own, own, own, vec],
        out_shape=[jax.ShapeDtypeStruct((T, HGRN_W), BF16)] * 3 + [jax.ShapeDtypeStruct((1, HGRN_W), F32)],
        scratch_shapes=[pltpu.VMEM((HGRN_DIM, HGRN_DIM), F32), pltpu.VMEM((HGRN_SIDE, SUPER, HGRN_DIM), F32)],
        compiler_params=_cp("arbitrary"),
    )(hg, hg, hg, lb, states, do)


def _rec_heads(rec, gate, g_h):
    rr = jnp.concatenate(
        [jnp.broadcast_to(_rms(rec[:, h * HGRN_DIM:(h + 1) * HGRN_DIM], HGRN_DIM), (rec.shape[0], HGRN_DIM))
         for h in range(HGRN_HEADS)], axis=1)
    rn = rec * rr
    sg = _sigmoid(gate)
    return rr, rn, sg


def _mix_out(attn_o, rec_o, hg, x, g_a, g_h, w_out, tm=512):
    T = x.shape[0]

    def body(a_ref, r_ref, gt_ref, x_ref, ga_ref, gh_ref, w_ref, h1_ref, mixed_ref):
        a = a_ref[...]
        an = a * _rms(a, ATTN_W) * ga_ref[...]
        gate = gt_ref[...]
        _, rn, sg = _rec_heads(r_ref[...], gate, gh_ref[...])
        mixed = jnp.concatenate([an, rn * gh_ref[...] * (gate * sg)], axis=1).astype(BF16)
        mixed_ref[...] = mixed
        h1_ref[...] = x_ref[...] + _dot(mixed, w_ref[...])

    row = lambda w: pl.BlockSpec((tm, w), lambda i: (i, 0))
    return pl.pallas_call(
        body, name="mix_out", grid=(T // tm,),
        in_specs=[row(ATTN_W), row(HGRN_W), pl.BlockSpec((tm, HGRN_W), lambda i: (i, 3)), row(D_MODEL),
                  _full((1, ATTN_W)), _full((1, HGRN_W)), _once((D_MODEL, D_MODEL))],
        out_specs=[row(D_MODEL), row(D_MODEL)],
        out_shape=[jax.ShapeDtypeStruct((T, D_MODEL), F32), jax.ShapeDtypeStruct((T, D_MODEL), BF16)],
        compiler_params=_cp("arbitrary"),
    )(attn_o, rec_o, hg, x, g_a, g_h, w_out)


_INV_SQRT2 = 1.0 / math.sqrt(2.0)
_INV_SQRT2PI = 1.0 / math.sqrt(2.0 * math.pi)


def _gelu(x):
    return 0.5 * x * (1.0 + lax.erf(x * _INV_SQRT2))


def _gelu_and_grad(x):
    z = x * _INV_SQRT2
    cdf = 0.5 * (1.0 + lax.erf(z))
    return x * cdf, cdf + (x * _INV_SQRT2PI) * jnp.exp(-(z * z))


def _shift_down(g, prev, rowid):
    p1 = _row(prev, prev.shape[0] - 1)
    p2 = _row(prev, prev.shape[0] - 2)
    s1 = jnp.where(rowid == 0, p1, pltpu.roll(g, 1, 0))
    s2 = jnp.where(rowid == 0, p2, jnp.where(rowid == 1, p1, pltpu.roll(g, 2, 0)))
    return s1, s2


def _mlp_fwd(h1, g2, w_up4, conv_w, conv_b, w_down, gf, tgt, tm=256):
    T = h1.shape[0]

    def body(h_ref, g2_ref, wu_hbm, cw_ref, cb_ref, wd_ref, gf_ref, t_ref,
             u_ref, gate_ref, val_ref, conv_ref, act_ref, dh_ref, loss_ref, dgf_ref, carry, wu_ref, sem):
        i = pl.program_id(0)

        @pl.when(i == 0)
        def _():
            carry[...] = jnp.zeros_like(carry)
            loss_ref[...] = jnp.zeros_like(loss_ref)
            dgf_ref[...] = jnp.zeros_like(dgf_ref)
            _load_side_by_side(wu_hbm, wu_ref, sem)

        h = h_ref[...]
        u = (h * _rms(h, D_MODEL) * g2_ref[...]).astype(BF16)
        u_ref[...] = u
        y2 = jnp.zeros((tm, D_MODEL), F32)
        for lo, hi in FF_CHUNKS:
            cols = slice(lo, hi)
            rowid = lax.broadcasted_iota(jnp.int32, (tm, hi - lo), 0)
            gb = _dot(u, wu_ref[:, lo:hi]).astype(BF16)
            vb = _dot(u, wu_ref[:, D_FF + lo:D_FF + hi]).astype(BF16)
            gate_ref[:, cols] = gb
            val_ref[:, cols] = vb
            g = gb.astype(F32)
            s1, s2 = _shift_down(g, carry[:, cols], rowid)
            carry[:, cols] = g[tm - 8:, :]
            conv = cb_ref[:, cols] + cw_ref[0:1, cols] * s2 + cw_ref[1:2, cols] * s1 + cw_ref[2:3, cols] * g
            act = (_gelu(conv) * vb.astype(F32)).astype(BF16)
            conv_ref[:, cols] = conv.astype(BF16)
            act_ref[:, cols] = act
            y2 = y2 + _dot(act, wd_ref[cols, :])
        h2 = h + y2
        rf = _rms(h2, D_MODEL)
        n = h2 * rf
        gfv = gf_ref[...]
        e = n * gfv - t_ref[...]
        loss_ref[...] += jnp.sum(e * e) * (0.5 / D_MODEL)
        dy = e * (1.0 / D_MODEL)
        dgf_ref[...] += _colsum(dy * n)
        dh_ref[...] = _rms_bwd(dy * gfv, n, rf, D_MODEL)

    row = lambda w: pl.BlockSpec((tm, w), lambda i: (i, 0))
    return pl.pallas_call(
        body, name="mlp_fwd", grid=(T // tm,),
        in_specs=[row(D_MODEL), _full((1, D_MODEL)), ANY, _full((3, D_FF)),
                  _full((1, D_FF)), _once((D_FF, D_MODEL)), _full((1, D_MODEL)), row(D_MODEL)],
        out_specs=[row(D_MODEL), row(D_FF), row(D_FF), row(D_FF), row(D_FF), row(D_MODEL), _full((1, 128)),
                   _full((1, D_MODEL))],
        out_shape=[jax.ShapeDtypeStruct((T, D_MODEL), BF16)] + [jax.ShapeDtypeStruct((T, D_FF), BF16)] * 4
        + [jax.ShapeDtypeStruct((T, D_MODEL), F32),
                   jax.ShapeDtypeStruct((1, 128), F32), jax.ShapeDtypeStruct((1, D_MODEL), F32)],
        scratch_shapes=[pltpu.VMEM((8, D_FF), F32), pltpu.VMEM((D_MODEL, 2 * D_FF), BF16),
                        pltpu.SemaphoreType.DMA((N_CHIPS,))],
        compiler_params=_cp("arbitrary"),
    )(h1, g2, w_up4, conv_w, conv_b, w_down, gf, tgt)


def _mlp_bwd(dh2, gate, val, conv, conv_w, w_down, tm=256):
    T = dh2.shape[0]
    nb = T // tm
    half = D_FF // 2

    def body(dh_ref, gate_ref, val_ref, conv_ref, cw_ref, wd_ref, dgv_ref, dcw_ref, dcb_ref, carry):
        @pl.when(pl.program_id(0) == 0)
        def _():
            carry[...] = jnp.zeros_like(carry)
            dcw_ref[...] = jnp.zeros_like(dcw_ref)
            dcb_ref[...] = jnp.zeros_like(dcb_ref)

        dhb = dh_ref[...].astype(BF16)
        rowid = lax.broadcasted_iota(jnp.int32, (tm, half), 0)
        for c in range(2):
            cols = slice(c * half, (c + 1) * half)
            g = gate_ref[:, cols].astype(F32)
            v = val_ref[:, cols].astype(F32)
            cv = conv_ref[:, cols].astype(F32)
            dact = _dot_nt(dhb, wd_ref[cols, :])
            gl, gp = _gelu_and_grad(cv)
            dconv = dact * v * gp
            nxt = carry[:, cols]
            n0, n1 = _row(nxt, 0), _row(nxt, 1)
            u1 = jnp.where(rowid == tm - 1, n0, pltpu.roll(dconv, tm - 1, 0))
            u2 = jnp.where(rowid == tm - 1, n1, jnp.where(rowid == tm - 2, n0, pltpu.roll(dconv, tm - 2, 0)))
            carry[:, cols] = dconv[0:8, :]
            dcb_ref[:, cols] += _colsum(dconv)
            dcw_ref[0:1, cols] += _colsum(u2 * g)
            dcw_ref[1:2, cols] += _colsum(u1 * g)
            dcw_ref[2:3, cols] += _colsum(dconv * g)
            dgate = cw_ref[2:3, cols] * dconv + cw_ref[1:2, cols] * u1 + cw_ref[0:1, cols] * u2
            dgv_ref[:, cols] = dgate.astype(BF16)
            dgv_ref[:, D_FF + c * half:D_FF + (c + 1) * half] = (dact * gl).astype(BF16)

    rev = lambda w: pl.BlockSpec((tm, w), lambda i: (nb - 1 - i, 0))
    return pl.pallas_call(
        body, name="mlp_bwd", grid=(nb,),
        in_specs=[rev(D_MODEL), rev(D_FF), rev(D_FF), rev(D_FF), _full((3, D_FF)), _once((D_FF, D_MODEL))],
        out_specs=[rev(2 * D_FF), _full((3, D_FF)), _full((1, D_FF))],
        out_shape=[jax.ShapeDtypeStruct((T, 2 * D_FF), BF16), jax.ShapeDtypeStruct((3, D_FF), F32),
                   jax.ShapeDtypeStruct((1, D_FF), F32)],
        scratch_shapes=[pltpu.VMEM((8, D_FF), F32)],
        compiler_params=_cp("arbitrary"),
    )(dh2, gate, val, conv, conv_w, w_down)


def _up_out_bwd(dgv, w_up4, h1, g2, dh2, w_out, attn_o, rec_o, hg, g_a, g_h, tm=256):
    T = h1.shape[0]

    def body(dgv_ref, wu_hbm, h_ref, g2_ref, dh2_ref, wo_ref, a_ref, r_ref, gt_ref, ga_ref, gh_ref,
             dh1_ref, dg2_ref, da_ref, dr_ref, dgt_ref, dga_ref, dgh_ref, wu_ref, sem):
        @pl.when(pl.program_id(0) == 0)
        def _():
            dg2_ref[...] = jnp.zeros_like(dg2_ref)
            dga_ref[...] = jnp.zeros_like(dga_ref)
            dgh_ref[...] = jnp.zeros_like(dgh_ref)
            _load_side_by_side(wu_hbm, wu_ref, sem)

        du = _dot_nt(dgv_ref[...], wu_ref[...])
        h = h_ref[...]
        r = _rms(h, D_MODEL)
        n = h * r
        dg2_ref[...] += _colsum(du * n)
        dh1 = dh2_ref[...] + _rms_bwd(du * g2_ref[...], n, r, D_MODEL)
        dh1_ref[...] = dh1
        dmix = _dot_nt(dh1.astype(BF16), wo_ref[...])
        dan = dmix[:, :ATTN_W]
        a = a_ref[...]
        ra = _rms(a, ATTN_W)
        na = a * ra
        dga_ref[...] += _colsum(dan * na)
        da_ref[...] = _rms_bwd(dan * ga_ref[...], na, ra, ATTN_W)
        dmr = dmix[:, ATTN_W:]
        gate = gt_ref[...]
        ghv = gh_ref[...]
        rr, rn, sg = _rec_heads(r_ref[...], gate, ghv)
        dgt_ref[...] = (dmr * rn * ghv * (sg * (1.0 + gate * (1.0 - sg)))).astype(BF16)
        drecn = dmr * (gate * sg)
        dgh_ref[...] += _colsum(drecn * rn)
        drn = drecn * ghv
        prod = drn * rn
        mean = jnp.concatenate(
            [jnp.broadcast_to(jnp.sum(prod[:, h_ * HGRN_DIM:(h_ + 1) * HGRN_DIM], axis=-1, keepdims=True),
                              (tm, HGRN_DIM)) for h_ in range(HGRN_HEADS)], axis=1) * (1.0 / HGRN_DIM)
        dr_ref[...] = rr * (drn - rn * mean)

    row = lambda w: pl.BlockSpec((tm, w), lambda i: (i, 0))
    return pl.pallas_call(
        body, name="up_out_bwd", grid=(T // tm,),
        in_specs=[row(2 * D_FF), ANY, row(D_MODEL), _full((1, D_MODEL)),
                  row(D_MODEL), _once((D_MODEL, D_MODEL)), row(ATTN_W), row(HGRN_W),
                  pl.BlockSpec((tm, HGRN_W), lambda i: (i, 3)), _full((1, ATTN_W)), _full((1, HGRN_W))],
        out_specs=[row(D_MODEL), _full((1, D_MODEL)), row(ATTN_W), row(HGRN_W), row(HGRN_W),
                   _full((1, ATTN_W)), _full((1, HGRN_W))],
        out_shape=[jax.ShapeDtypeStruct((T, D_MODEL), F32), jax.ShapeDtypeStruct((1, D_MODEL), F32),
                   jax.ShapeDtypeStruct((T, ATTN_W), F32), jax.ShapeDtypeStruct((T, HGRN_W), F32),
                   jax.ShapeDtypeStruct((T, HGRN_W), BF16), jax.ShapeDtypeStruct((1, ATTN_W), F32),
                   jax.ShapeDtypeStruct((1, HGRN_W), F32)],
        scratch_shapes=[pltpu.VMEM((D_MODEL, 2 * D_FF), BF16), pltpu.SemaphoreType.DMA((N_CHIPS,))],
        compiler_params=_cp("arbitrary"),
    )(dgv, w_up4, h1, g2, dh2, w_out, attn_o, rec_o, hg, g_a, g_h)


def _in_bwd(dqkv, dhg, w_in4, x, g1, dh1, tm=512):
    T = x.shape[0]

    def body(*refs):
        parts = refs[:7]
        w_hbm, x_ref, g_ref, dh1_ref, dp_ref, dx_ref, dg_ref, w_full, sem = refs[7:]

        @pl.when(pl.program_id(0) == 0)
        def _():
            dg_ref[...] = jnp.zeros_like(dg_ref)
            _load_side_by_side(w_hbm, w_full, sem)

        dp = jnp.concatenate([p[...] for p in parts], axis=1)
        dp_ref[...] = dp
        du = _dot_nt(dp, w_full[...])
        xv = x_ref[...]
        r = _rms(xv, D_MODEL)
        n = xv * r
        dg_ref[...] += _colsum(du * n)
        dx_ref[...] = dh1_ref[...] + _rms_bwd(du * g_ref[...], n, r, D_MODEL)

    row = lambda w: pl.BlockSpec((tm, w), lambda i: (i, 0))
    return pl.pallas_call(
        body, name="in_bwd", grid=(T // tm,),
        in_specs=[row(ATTN_W)] * 7 + [ANY, row(D_MODEL), _full((1, D_MODEL)), row(D_MODEL)],
        out_specs=[row(IN_TOTAL), row(D_MODEL), _full((1, D_MODEL))],
        out_shape=[jax.ShapeDtypeStruct((T, IN_TOTAL), BF16), jax.ShapeDtypeStruct((T, D_MODEL), F32),
                   jax.ShapeDtypeStruct((1, D_MODEL), F32)],
        scratch_shapes=[pltpu.VMEM((D_MODEL, IN_TOTAL), BF16), pltpu.SemaphoreType.DMA((N_CHIPS,))],
        compiler_params=_cp("arbitrary"),
    )(*dqkv, *dhg, w_in4, x, g1, dh1)


def _dw(a, b, kb, nb_, name, tk=1024, side=1):
    T, K = a.shape
    N = b.shape[1]
    nk, nn, nt = K // kb, N // (nb_ * side), T // tk

    def body(a_ref, b_ref, o_ref, acc):
        t = pl.program_id(2)

        @pl.when(t == 0)
        def _():
            acc[...] = jnp.zeros_like(acc)

        acc[...] += _dot_tn(a_ref[...], b_ref[...].astype(BF16))

        @pl.when(t == nt - 1)
        def _():
            for s in range(side):
                o_ref[s] = acc[:, s * nb_:(s + 1) * nb_].astype(BF16)

    return pl.pallas_call(
        body, name=name, grid=(nk, nn, nt),
        in_specs=[pl.BlockSpec((tk, kb), lambda i, j, t: (t, i)),
                  pl.BlockSpec((tk, nb_ * side), lambda i, j, t: (t, j))],
        out_specs=pl.BlockSpec((side, kb, nb_), lambda i, j, t: (i * nn + j, 0, 0)),
        out_shape=jax.ShapeDtypeStruct((nk * nn * side, kb, nb_), BF16),
        scratch_shapes=[pltpu.VMEM((kb, nb_ * side), F32)],
        compiler_params=_cp("arbitrary", "arbitrary", "arbitrary"),
    )(a, b)


def _local_step(x, tgt, g1, w_in4, g_a, g_h, lb, w_out, g2, w_up4, conv_w, conv_b, w_down, gf):
    a = _step_mixers(x, g1, w_in4, lb)
    b = _step_channel(a, x, tgt, g_a, g_h, w_out, g2, w_up4, conv_w, conv_b, w_down, gf)
    c = _step_mixers_bwd(a, b, x, g1, w_in4, lb)
    small = dict(g1=c["dg1"], g_a=b["dga"], g_h=b["dgh"], lb=c["dlb"], g2=b["dg2"], conv_w=b["dcw"], conv_b=b["dcb"],
                 gf=b["dgf"])
    return b["loss"], c["dx"], small, dict(w_in=c["dw_in"], w_out=b["dw_out"], w_up=b["dw_up"], w_down=b["dw_down"])


def _step_mixers(x, g1, w_in4, lb):
    u1, qkv, hg = _in_proj(x, g1, w_in4)
    attn_o, lse = _attn_fwd(qkv)
    rec_o, states = _hgrn_fwd(hg, lb)
    return dict(u1=u1, qkv=qkv, hg=hg, attn_o=attn_o, lse=lse, rec_o=rec_o, states=states)


def _step_channel(a, x, tgt, g_a, g_h, w_out, g2, w_up4, conv_w, conv_b, w_down, gf):
    h1, mixed = _mix_out(a["attn_o"], a["rec_o"], a["hg"], x, g_a, g_h, w_out)
    u2, gate, val, conv, act, dh2, loss, dgf = _mlp_fwd(h1, g2, w_up4, conv_w, conv_b, w_down, gf, tgt)
    dgv, dcw, dcb = _mlp_bwd(dh2, gate, val, conv, conv_w, w_down)
    dw_down = _dw(act, dh2, D_FF // 2, D_MODEL, "dw_down").reshape(N_CHIPS, D_FF // N_CHIPS, D_MODEL)
    dh1, dg2, da, dr, dgt, dga, dgh = _up_out_bwd(dgv, w_up4, h1, g2, dh2, w_out, a["attn_o"], a["rec_o"], a["hg"],
                                                  g_a, g_h)
    dw_up = _dw(u2, dgv, D_MODEL, UP_SHARD, "dw_up", side=2)
    dw_out = _dw(mixed, dh1, D_MODEL, D_MODEL, "dw_out").reshape(N_CHIPS, D_MODEL // N_CHIPS, D_MODEL)
    return dict(loss=loss, dgf=dgf, dcw=dcw, dcb=dcb, dg2=dg2, dga=dga, dgh=dgh, dh1=dh1, da=da, dr=dr, dgt=dgt,
                dw_down=dw_down, dw_up=dw_up, dw_out=dw_out)


def _step_mixers_bwd(a, b, x, g1, w_in4, lb, dqkv=None):
    if dqkv is None:
        dqkv = _attn_bwd(a["qkv"], a["attn_o"], a["lse"], b["da"])
    dhq, dhf, dhi, dlb = _hgrn_bwd(a["hg"], lb, a["states"], b["dr"])
    dproj, dx, dg1 = _in_bwd(dqkv, [dhq, dhf, dhi, b["dgt"]], w_in4, x, g1, b["dh1"])
    dw_in = _dw(a["u1"], dproj, D_MODEL, IN_SHARD, "dw_in", side=2)
    return dict(dx=dx, dg1=dg1, dlb=dlb, dw_in=dw_in)


BIG = ("w_in", "w_out", "w_up", "w_down")
ANY = pl.BlockSpec(memory_space=pl.ANY)


def _place():
    x, y, c = lax.axis_index("x"), lax.axis_index("y"), lax.axis_index("c")
    chips = [(1 - x, y), (x, 1 - y), (1 - x, 1 - y)]
    return x, y, c, chips


def _remote(src, dst, send_sems, recv_sems, k, to):
    return pltpu.make_async_remote_copy(src_ref=src, dst_ref=dst, send_sem=send_sems.at[k], recv_sem=recv_sems.at[k],
                                        device_id=to, device_id_type=MESH)


def _gather_weights(shards, conv_w):
    n = len(shards)
    halves = [s.shape[0] // 2 for s in shards]

    def body(*refs):
        ins, cw, outs, ocw = refs[:n], refs[n], refs[n + 1:2 * n + 1], refs[2 * n + 1]
        send_sems, recv_sems = refs[2 * n + 2:]
        x, y, c, chips = _place()
        me, sibling = 2 * x + y, (x, y, 1 - c)

        def part(w, chip, half):
            return outs[w].at[chip, pl.ds(half * halves[w], halves[w]), :]

        sent = []
        for j, chip in enumerate(chips):
            for w in range(n):
                sent.append(_remote(ins[w].at[pl.ds(c * halves[w], halves[w]), :], part(w, me, c),
                                    send_sems, recv_sems, w * 3 + j, (*chip, c)))
            sent.append(_remote(cw, ocw.at[me], send_sems, recv_sems, 6 * n + j, (*chip, c)))
        for cp in sent:
            cp.start()
        for j, chip in enumerate(chips):
            kj = 2 * chip[0] + chip[1]
            for w in range(n):
                _remote(part(w, kj, c), part(w, kj, c), send_sems, recv_sems, w * 3 + j, (*chip, c)).wait_recv()
                fwd = _remote(part(w, kj, c), part(w, kj, c), send_sems, recv_sems, 3 * n + w * 3 + j, sibling)
                fwd.start()
                sent.append(fwd)
        for j, chip in enumerate(chips):
            kj = 2 * chip[0] + chip[1]
            for w in range(n):
                _remote(part(w, kj, 1 - c), part(w, kj, 1 - c), send_sems, recv_sems, 3 * n + w * 3 + j,
                        sibling).wait_recv()
            _remote(cw, ocw.at[kj], send_sems, recv_sems, 6 * n + j, (*chip, c)).wait_recv()
        for cp in sent:
            cp.wait_send()

    n_sem = 6 * n + 3
    outs = pl.pallas_call(
        body, name="gather_weights",
        in_specs=[ANY] * (n + 1), out_specs=[ANY] * (n + 1),
        out_shape=[jax.ShapeDtypeStruct((N_CHIPS,) + s.shape, s.dtype) for s in shards]
        + [jax.ShapeDtypeStruct((N_CHIPS,) + conv_w.shape, conv_w.dtype)],
        scratch_shapes=[pltpu.SemaphoreType.DMA((n_sem,)), pltpu.SemaphoreType.DMA((n_sem,))],
    )(*shards, conv_w)
    chip = 2 * lax.axis_index("x") + lax.axis_index("y")
    return [lax.dynamic_update_slice(o, s[None], (chip,) + (0,) * s.ndim) for o, s in zip(outs, [*shards, conv_w])]


def _allreduce_small(buf):
    rows = buf.shape[0]

    def body(in_ref, out_ref, slots, send_sems, recv_sems):
        x, y, c, _ = _place()
        me = 4 * x + 2 * y + c
        slots[me] = in_ref[...]
        sent = []
        for p in range(1, 8):
            to = (x ^ (p >> 2), y ^ ((p >> 1) & 1), c ^ (p & 1))
            sent.append(_remote(in_ref, slots.at[me], send_sems, recv_sems, p, to))
        for cp in sent:
            cp.start()
        for p in range(1, 8):
            frm = 4 * (x ^ (p >> 2)) + 2 * (y ^ ((p >> 1) & 1)) + (c ^ (p & 1))
            _remote(in_ref, slots.at[frm], send_sems, recv_sems, p, (x, y, c)).wait_recv()
        for cp in sent:
            cp.wait_send()
        acc = slots[0]
        for d in range(1, 8):
            acc = acc + slots[d]
        out_ref[...] = acc

    vm = pl.BlockSpec(memory_space=pltpu.VMEM)
    return pl.pallas_call(
        body, name="allreduce_small", in_specs=[vm], out_specs=vm,
        out_shape=jax.ShapeDtypeStruct(buf.shape, F32),
        scratch_shapes=[pltpu.VMEM((8, rows, 128), F32), pltpu.SemaphoreType.DMA((8,)), pltpu.SemaphoreType.DMA((8,))],
    )(buf)


def _sibling_peer():
    x, y, c, _ = _place()
    return [(x, y, 1 - c)]


def _chip_peers():
    x, y, c, chips = _place()
    return [(*chip, c) for chip in chips]


def _handshake(peers):
    barrier = pltpu.get_barrier_semaphore()
    for peer in peers:
        pl.semaphore_signal(barrier, inc=1, device_id=peer, device_id_type=MESH)
    pl.semaphore_wait(barrier, len(peers))


def _pair_exchange(gs, name, barrier_id):
    n = len(gs)
    halves = [g.shape[1] // 2 for g in gs]

    def body(*refs):
        g, got = refs[:n], refs[n:2 * n]
        send_sems, recv_sems = refs[2 * n:]
        _handshake(_sibling_peer())
        x, y, c, _ = _place()
        cps = [_remote(g[w].at[:, pl.ds((1 - c) * halves[w], halves[w]), :], got[w], send_sems, recv_sems, w,
                       (x, y, 1 - c)) for w in range(n)]
        for cp in cps:
            cp.start()
        for cp in cps:
            cp.wait()

    return pl.pallas_call(
        body, name=name, in_specs=[ANY] * n, out_specs=[ANY] * n,
        out_shape=[jax.ShapeDtypeStruct((N_CHIPS, h, g.shape[2]), g.dtype) for g, h in zip(gs, halves)],
        scratch_shapes=[pltpu.SemaphoreType.DMA((n,)), pltpu.SemaphoreType.DMA((n,))],
        compiler_params=pltpu.CompilerParams(collective_id=barrier_id),
    )(*gs)


def _core_id():
    return lax.axis_index("c").reshape(1).astype(jnp.int32)


def _pair_sum(g, got, name):
    h, C = got.shape[1:]

    def body(c_ref, g_ref, b_ref, o_ref):
        o_ref[...] = (g_ref[...].astype(F32) + b_ref[...].astype(F32)).astype(BF16)

    blk = pl.BlockSpec((1, h, C), lambda k, c_ref: (k, 0, 0))
    return pl.pallas_call(
        body, name=name,
        grid_spec=pltpu.PrefetchScalarGridSpec(
            num_scalar_prefetch=1, grid=(N_CHIPS,),
            in_specs=[pl.BlockSpec((1, h, C), lambda k, c_ref: (k, c_ref[0], 0)), blk], out_specs=blk),
        out_shape=jax.ShapeDtypeStruct(got.shape, BF16), compiler_params=_cp("arbitrary"))(_core_id(), g, got)


def _sum_partials(g, got, landed, name):
    h, C = got.shape[1:]

    def body(ids, g_ref, b_ref, l_ref, o_ref):
        acc = g_ref[0].astype(F32) + b_ref[0].astype(F32)
        for j in range(3):
            acc = acc + l_ref[j].astype(F32)
        o_ref[...] = acc

    ids = jnp.stack([2 * lax.axis_index("x") + lax.axis_index("y"), lax.axis_index("c")]).astype(jnp.int32)
    return pl.pallas_call(
        body, name=name,
        grid_spec=pltpu.PrefetchScalarGridSpec(
            num_scalar_prefetch=1, grid=(1,),
            in_specs=[pl.BlockSpec((1, h, C), lambda i, ids: (ids[0], ids[1], 0)),
                      pl.BlockSpec((1, h, C), lambda i, ids: (ids[0], 0, 0)),
                      pl.BlockSpec((3, h, C), lambda i, ids: (0, 0, 0))],
            out_specs=pl.BlockSpec((h, C), lambda i, ids: (ids[1], 0))),
        out_shape=jax.ShapeDtypeStruct((2 * h, C), F32), compiler_params=_cp("arbitrary"))(ids, g, got, landed)


def _pair_share(reds, name, barrier_id):
    n = len(reds)

    def body(*refs):
        out = refs[n:2 * n]
        send_sems, recv_sems = refs[2 * n:]
        _handshake(_sibling_peer())
        x, y, c, _ = _place()
        def half(w, which):
            h = out[w].shape[0] // 2
            return out[w].at[pl.ds(which * h, h), :]

        cps = [_remote(half(w, c), half(w, c), send_sems, recv_sems, w, (x, y, 1 - c)) for w in range(n)]
        for cp in cps:
            cp.start()
        for w in range(n):
            _remote(half(w, 1 - c), half(w, 1 - c), send_sems, recv_sems, w, (x, y, 1 - c)).wait_recv()
        for cp in cps:
            cp.wait_send()

    return pl.pallas_call(
        body, name=name, in_specs=[ANY] * n, out_specs=[ANY] * n,
        out_shape=[jax.ShapeDtypeStruct(r.shape, F32) for r in reds],
        input_output_aliases={w: w for w in range(n)},
        scratch_shapes=[pltpu.SemaphoreType.DMA((n,)), pltpu.SemaphoreType.DMA((n,))],
        compiler_params=pltpu.CompilerParams(collective_id=barrier_id),
    )(*reds)


HBM = pl.BlockSpec(memory_space=pltpu.HBM)
SEM = pl.BlockSpec(memory_space=pltpu.SEMAPHORE)
DATAFLOW = pltpu.SideEffectType.DATAFLOW_SIDE_EFFECTING


def _copies_start(name, srcs, lands, plan, n_copies, after, peers, barrier_id):
    ns, nb, na = len(srcs), len(srcs) + len(lands), len(after)

    def body(*refs):
        src_refs, land_refs = refs[:ns], refs[ns:nb]
        send_sems, recv_sems = refs[nb + na:nb + na + 2]
        token = refs[-1]
        _handshake(peers())
        for k, (src, there, _, to) in enumerate(plan(src_refs, land_refs)):
            _remote(src, there, send_sems, recv_sems, k, to).start()
        token[...] = jnp.zeros_like(token)

    hbm = lambda a: pltpu.HBM(a.shape, a.dtype)
    outs = pl.pallas_call(
        body, name=name,
        out_shape=(pltpu.SemaphoreType.DMA((n_copies,)), pltpu.SemaphoreType.DMA((n_copies,)),
                   *[hbm(a) for a in srcs], *[hbm(a) for a in lands], jax.ShapeDtypeStruct((8, 128), F32)),
        in_specs=[HBM] * nb + [ANY] * na,
        out_specs=(SEM, SEM, *[HBM] * nb, pl.BlockSpec(memory_space=pltpu.VMEM)),
        input_output_aliases={i: 2 + i for i in range(nb)},
        compiler_params=pltpu.CompilerParams(has_side_effects=DATAFLOW, collective_id=barrier_id),
    )(*[pltpu.with_memory_space_constraint(a, pltpu.HBM) for a in (*srcs, *lands)], *after)
    return outs[0], outs[1], outs[2:2 + ns], outs[2 + ns:2 + nb], outs[-1]


def _copies_wait(name, send_sems, recv_sems, srcs, lands, plan, after):
    ns, nb, na = len(srcs), len(srcs) + len(lands), len(after)

    def body(*refs):
        src_refs, land_refs = refs[:ns], refs[ns:nb]
        send_sems, recv_sems = refs[nb:nb + 2]
        for k, (src, _, here, to) in enumerate(plan(src_refs, land_refs)):
            cp = _remote(src, here, send_sems, recv_sems, k, to)
            cp.wait_send()
            cp.wait_recv()

    hbm = lambda a: pltpu.HBM(a.shape, a.dtype)
    outs = pl.pallas_call(
        body, name=name,
        out_shape=(*[hbm(a) for a in srcs], *[hbm(a) for a in lands]),
        in_specs=[HBM] * nb + [SEM, SEM] + [ANY] * na,
        out_specs=tuple([HBM] * nb),
        input_output_aliases={i: i for i in range(nb)},
        compiler_params=pltpu.CompilerParams(has_side_effects=DATAFLOW),
    )(*srcs, *lands, send_sems, recv_sems, *after)
    return outs[:ns], outs[ns:]


def _gather_plan(halves):
    def plan(shards, lands):
        x, y, c, chips = _place()
        me = 2 * x + y
        copies = []
        for w, h in enumerate(halves):
            rows = pl.ds(c * h, h)
            for chip in chips:
                copies.append((shards[w].at[rows, :], lands[w].at[me, rows, :],
                               lands[w].at[2 * chip[0] + chip[1], rows, :], (*chip, c)))
        return copies
    return plan


def _reduce_plan(n):
    def plan(ps, lands):
        x, y, c, chips = _place()
        return [(ps[w].at[2 * chip[0] + chip[1]], lands[w].at[j], lands[w].at[j], (*chip, c))
                for w in range(n) for j, chip in enumerate(chips)]
    return plan


def _forward_plan(halves):
    def plan(_, lands):
        x, y, c, chips = _place()

        def part(w, chip, half):
            return lands[w].at[2 * chip[0] + chip[1], pl.ds(half * halves[w], halves[w]), :]

        return [(part(w, chip, c), part(w, chip, c), part(w, chip, 1 - c), (x, y, 1 - c))
                for w in range(len(halves)) for chip in chips]
    return plan


def _pair_plan(halves):
    def plan(gs, gots):
        x, y, c, _ = _place()
        return [(gs[w].at[:, pl.ds((1 - c) * h, h), :], gots[w], gots[w], (x, y, 1 - c)) for w, h in enumerate(halves)]
    return plan


def _place_own(gathered, shards):
    chip = 2 * lax.axis_index("x") + lax.axis_index("y")
    return [lax.dynamic_update_slice(o, s[None], (chip, 0, 0)) for o, s in zip(gathered, shards)]


def _adamw(w, g, m, v, name, tr=None):
    R, C = w.shape
    tr = tr or R // 4

    def body(w_ref, g_ref, m_ref, v_ref, d_ref, nm_ref, nv_ref):
        gv = g_ref[...]
        nm = ADAM_B1 * m_ref[...] + (1.0 - ADAM_B1) * gv
        nv = ADAM_B2 * v_ref[...] + (1.0 - ADAM_B2) * (gv * gv)
        m_hat = nm / (1.0 - ADAM_B1 ** ADAM_STEP)
        v_hat = nv / (1.0 - ADAM_B2 ** ADAM_STEP)
        d_ref[...] = -ADAM_LR * (m_hat / (jnp.sqrt(v_hat) + ADAM_EPS) + ADAM_WD * w_ref[...])
        nm_ref[...] = nm
        nv_ref[...] = nv

    blk = pl.BlockSpec((tr, C), lambda i: (i, 0))
    return pl.pallas_call(body, name=name, grid=(R // tr,), in_specs=[blk] * 4, out_specs=[blk] * 3,
                          out_shape=[jax.ShapeDtypeStruct((R, C), F32)] * 3, compiler_params=_cp("arbitrary"))(w, g, m, v)


SMALL = (("norm1_g", 1, 1024), ("attn_norm_g", 1, 512), ("hgrn_norm_g", 1, 512), ("hgrn_lb_logits", 2, 512),
         ("norm2_g", 1, 1024), ("conv_b", 1, D_FF), ("final_norm_g", 1, 1024), ("conv_w", 3, D_FF))
LOSS_ROW = sum(r * c for _, r, c in SMALL) // 128
SMALL_ROWS = 136


def _rows_to_lanes(ref, row, width):
    return jnp.concatenate([ref[row + j:row + j + 1, :] for j in range(width // 128)], axis=1)


def _pack_small(grads, dlb, lb, loss):
    def body(*refs):
        parts, dlb_ref, lb_ref, loss_ref, out = refs[:len(SMALL) - 1], refs[-4], refs[-3], refs[-2], refs[-1]
        out[...] = jnp.zeros_like(out)
        lbv = lb_ref[...]
        dl = dlb_ref[...] * lbv * (1.0 - lbv)
        row = 0
        parts = list(parts)
        for name, rows, width in SMALL:
            for r in range(rows):
                if name == "hgrn_lb_logits":
                    src = dl if r == 0 else -dl
                    for j in range(width // 128):
                        out[row + j:row + j + 1, :] = src[:, 128 * j:128 * (j + 1)]
                else:
                    for j in range(width // 128):
                        out[row + j:row + j + 1, :] = parts[0][r:r + 1, 128 * j:128 * (j + 1)]
                row += width // 128
            if name != "hgrn_lb_logits":
                parts.pop(0)
        out[LOSS_ROW:LOSS_ROW + 1, :] = loss_ref[...]

    vm = pl.BlockSpec(memory_space=pltpu.VMEM)
    return pl.pallas_call(body, name="pack_small", in_specs=[vm] * (len(grads) + 3), out_specs=vm,
                          out_shape=jax.ShapeDtypeStruct((SMALL_ROWS, 128), F32))(*grads, dlb, lb, loss)


def _adamw_math(w, g, m, v):
    nm = ADAM_B1 * m + (1.0 - ADAM_B1) * g
    nv = ADAM_B2 * v + (1.0 - ADAM_B2) * (g * g)
    m_hat = nm / (1.0 - ADAM_B1 ** ADAM_STEP)
    v_hat = nv / (1.0 - ADAM_B2 ** ADAM_STEP)
    return -ADAM_LR * (m_hat / (jnp.sqrt(v_hat) + ADAM_EPS) + ADAM_WD * w), nm, nv


def _small_update(summed, g_conv_w, ws, ms, vs):
    n = len(SMALL)

    def body(*refs):
        s_ref, gcw_ref = refs[:2]
        w_refs, m_refs, v_refs = refs[2:2 + n], refs[2 + n:2 + 2 * n], refs[2 + 2 * n:2 + 3 * n]
        outs = refs[2 + 3 * n:]
        row = 0
        for k, (name, rows, width) in enumerate(SMALL):
            if name == "conv_w":
                g = gcw_ref[...]
            else:
                g = jnp.concatenate([_rows_to_lanes(s_ref, row + r * (width // 128), width) for r in range(rows)], axis=0)
            row += rows * (width // 128)
            d, nm, nv = _adamw_math(w_refs[k][...], g, m_refs[k][...], v_refs[k][...])
            for o, val in zip(outs[4 * k:4 * k + 4], (g, d, nm, nv)):
                o[...] = val

    vm = pl.BlockSpec(memory_space=pltpu.VMEM)
    outs = pl.pallas_call(
        body, name="small_update", in_specs=[vm] * (2 + 3 * n), out_specs=[vm] * (4 * n),
        out_shape=[jax.ShapeDtypeStruct(a.shape, F32) for a in ws for _ in range(4)],
    )(summed, g_conv_w, *ws, *ms, *vs)
    return [outs[4 * k:4 * k + 4] for k in range(n)]


def kernel(x, norm1_g, w_in, attn_norm_g, hgrn_norm_g, hgrn_lb_logits, w_out, norm2_g, w_up, conv_w, conv_b, w_down, final_norm_g, loss_target, m_norm1_g, m_w_in, m_attn_norm_g, m_hgrn_norm_g, m_hgrn_lb_logits, m_w_out, m_norm2_g, m_w_up, m_conv_w, m_conv_b, m_w_down, m_final_norm_g, v_norm1_g, v_w_in, v_attn_norm_g, v_hgrn_norm_g, v_hgrn_lb_logits, v_w_out, v_norm2_g, v_w_up, v_conv_w, v_conv_b, v_w_down, v_final_norm_g):
    w = dict(norm1_g=norm1_g, w_in=w_in, attn_norm_g=attn_norm_g, hgrn_norm_g=hgrn_norm_g,
             hgrn_lb_logits=hgrn_lb_logits, w_out=w_out, norm2_g=norm2_g, w_up=w_up, conv_w=conv_w, conv_b=conv_b,
             w_down=w_down, final_norm_g=final_norm_g)
    m = dict(norm1_g=m_norm1_g, w_in=m_w_in, attn_norm_g=m_attn_norm_g, hgrn_norm_g=m_hgrn_norm_g,
             hgrn_lb_logits=m_hgrn_lb_logits, w_out=m_w_out, norm2_g=m_norm2_g, w_up=m_w_up, conv_w=m_conv_w,
             conv_b=m_conv_b, w_down=m_w_down, final_norm_g=m_final_norm_g)
    v = dict(norm1_g=v_norm1_g, w_in=v_w_in, attn_norm_g=v_attn_norm_g, hgrn_norm_g=v_hgrn_norm_g,
             hgrn_lb_logits=v_hgrn_lb_logits, w_out=v_w_out, norm2_g=v_norm2_g, w_up=v_w_up, conv_w=v_conv_w,
             conv_b=v_conv_b, w_down=v_w_down, final_norm_g=v_final_norm_g)
    names = list(w)
    chip = 2 * lax.axis_index("x") + lax.axis_index("y")

    shards = {k: w[k][0].astype(BF16) for k in BIG}
    w_in4, conv_w4 = _gather_weights([shards["w_in"]], conv_w[0])
    conv_w_full = jnp.transpose(conv_w4, (1, 0, 2)).reshape(3, D_FF)
    lb = jax.nn.softmax(hgrn_lb_logits, axis=0)[0:1]
    late = [shards[k] for k in BIG[1:]]
    gather_plan = _gather_plan([s.shape[0] // 2 for s in late])
    started = _copies_start("gather_start", late, [lax.empty((N_CHIPS,) + s.shape, BF16) for s in late], gather_plan,
                            3 * len(late), after=(w_in4,), peers=_chip_peers, barrier_id=0)
    u1, qkv, hg = _in_proj(x[0], norm1_g + started[4][0:1, 0:1], w_in4)
    attn_o, lse = _attn_fwd(qkv)
    late, landed_w = _copies_wait("gather_wait", *started[:4], gather_plan, after=(attn_o,))
    forward_plan = _forward_plan([s.shape[0] // 2 for s in late])
    started = _copies_start("forward_start", [], landed_w, forward_plan, 3 * len(late), after=(),
                            peers=_sibling_peer, barrier_id=1)
    rec_o, states = _hgrn_fwd(hg, lb + started[4][0:1, 0:1])
    a = dict(u1=u1, qkv=qkv, hg=hg, attn_o=attn_o, lse=lse, rec_o=rec_o, states=states)
    w_out4, w_up4, w_down4 = _place_own(
        _copies_wait("forward_wait", *started[:4], forward_plan, after=(rec_o,))[1], late)

    b = _step_channel(a, x[0], loss_target[0], attn_norm_g, hgrn_norm_g, w_out4.reshape(D_MODEL, D_MODEL), norm2_g,
                      w_up4, conv_w_full, conv_b, w_down4.reshape(D_FF, D_MODEL), final_norm_g.reshape(1, D_MODEL))

    early = [b["dw_out"], b["dw_up"], b["dw_down"]]
    pair_plan = _pair_plan([gk.shape[1] // 2 for gk in early])
    started = _copies_start("pair_start", early,
                            [lax.empty((N_CHIPS, gk.shape[1] // 2, gk.shape[2]), BF16) for gk in early], pair_plan,
                            len(early), after=(), peers=_sibling_peer, barrier_id=2)
    dqkv = _attn_bwd(qkv, attn_o, lse, b["da"], started[4])
    early, gots = _copies_wait("pair_wait", *started[:4], pair_plan, after=(dqkv[0],))
    ps = [_pair_sum(gk, got, f"pair_sum_{k}") for gk, got, k in zip(early, gots, BIG[1:])]
    reduce_plan = _reduce_plan(len(ps))
    started = _copies_start("reduce_start", ps, [lax.empty((3,) + p.shape[1:], BF16) for p in ps], reduce_plan,
                            3 * len(ps), after=(), peers=_chip_peers, barrier_id=3)
    c = _step_mixers_bwd(a, b, x[0], norm1_g, w_in4, lb + started[4][0:1, 0:1], dqkv)
    gots_in = _pair_exchange([c["dw_in"]], "pair_exchange_w_in", barrier_id=4)
    ps_in = _pair_sum(c["dw_in"], gots_in[0], "pair_sum_w_in")
    plan_in = _reduce_plan(1)
    started_in = _copies_start("reduce_start_w_in", [ps_in], [lax.empty((3,) + ps_in.shape[1:], BF16)], plan_in, 3,
                               after=(), peers=_chip_peers, barrier_id=5)
    landed = _copies_wait("reduce_wait", *started[:4], reduce_plan, after=(started_in[4],))[1]
    reds = [_sum_partials(gk, got, l, f"sum_partials_{k}") for gk, got, l, k in zip(early, gots, landed, BIG[1:])]
    g = dict(zip(BIG[1:], _pair_share(reds, "pair_share", barrier_id=6)))
    delta, new_m, new_v = {}, {}, {}
    for k in BIG[1:]:
        delta[k], new_m[k], new_v[k] = _adamw(w[k][0], g[k], m[k][0], v[k][0], f"adamw_{k}")

    loss, dx = b["loss"], c["dx"]
    small = dict(g1=c["dg1"], g_a=b["dga"], g_h=b["dgh"], lb=c["dlb"], g2=b["dg2"], conv_w=b["dcw"], conv_b=b["dcb"],
                 gf=b["dgf"])
    summed = _allreduce_small(_pack_small(
        [small["g1"], small["g_a"], small["g_h"], small["g2"], small["conv_b"], small["gf"], small["conv_w"]],
        small["lb"], lb, loss))
    loss_total = summed[LOSS_ROW, 0]
    g_conv_w = lax.dynamic_slice(summed[LOSS_ROW - 3 * D_FF // 128:LOSS_ROW].reshape(3, D_FF),
                                 (0, chip * (D_FF // N_CHIPS)), (3, D_FF // N_CHIPS))
    two_d = lambda p, k: p[k].reshape(-1, p[k].shape[-1])
    updated = _small_update(summed, g_conv_w, *[[two_d(p, k) for k, _, _ in SMALL] for p in (w, m, v)])
    for (k, _, _), parts in zip(SMALL, updated):
        g[k], delta[k], new_m[k], new_v[k] = (a.reshape(w[k].shape) for a in parts)

    landed_in = _copies_wait("reduce_wait_w_in", *started_in[:4], plan_in, after=(updated[0][1], delta["w_up"]))[1]
    red_in = _sum_partials(c["dw_in"], gots_in[0], landed_in[0], "sum_partials_w_in")
    g["w_in"] = _pair_share([red_in], "pair_share_w_in", barrier_id=7)[0]
    delta["w_in"], new_m["w_in"], new_v["w_in"] = _adamw(w_in[0], g["w_in"], m_w_in[0], v_w_in[0], "adamw_w_in")
    for k in BIG:
        g[k], delta[k], new_m[k], new_v[k] = g[k][None], delta[k][None], new_m[k][None], new_v[k][None]

    return (loss_total, dx[None], *[g[k] for k in names], *[delta[k] for k in names],
            *[new_m[k] for k in names], *[new_v[k] for k in names])
```

```python
import math

import jax
import jax.numpy as jnp
from jax import lax
from jax.experimental import pallas as pl
from jax.experimental.pallas import tpu as pltpu

F32 = jnp.float32
BF16 = jnp.bfloat16

D_MODEL = 1024
ATTN_W = 512
HGRN_W = 512
HEAD_PAIR = 128
ATTN_BLK = 128
DILATIONS = (1, 4, 16)
ATTN_CHAINS = 4
ATTN_CHAINS_FWD = 8
HGRN_HEADS = 4
HGRN_DIM = 128
HGRN_CHUNK = 64
SUPER = 256
HGRN_SIDE = 4
D_FF = 2816
FF_CHUNKS = ((0, 1536), (1536, D_FF))
N_CHIPS = 4
IN_TOTAL = 3584
IN_SHARD = IN_TOTAL // N_CHIPS
UP_SHARD = 2 * D_FF // N_CHIPS
QKV_W = 3 * ATTN_W
HG_W = 4 * HGRN_W
EPS = 1e-6
NEG = -1e30
V7X_VMEM_BYTES = 64 * 1024 * 1024
VMEM_LIMIT = V7X_VMEM_BYTES - 8 * 1024 * 1024

ADAM_LR = 0.001
ADAM_B1 = 0.9
ADAM_B2 = 0.999
ADAM_EPS = 1e-08
ADAM_WD = 0.01
ADAM_STEP = 10

MESH = pl.DeviceIdType.MESH


def _cp(*sem):
    return pltpu.CompilerParams(dimension_semantics=sem or None, vmem_limit_bytes=VMEM_LIMIT)


def _dot(a, b):
    return jnp.dot(a, b, preferred_element_type=F32)


def _dot_nt(a, b):
    return lax.dot_general(a, b, (((1,), (1,)), ((), ())), preferred_element_type=F32)


def _dot_tn(a, b):
    return lax.dot_general(a, b, (((0,), (0,)), ((), ())), preferred_element_type=F32)


def _sigmoid(x):
    return 1.0 / (1.0 + jnp.exp(-x))


def _rms(x, width):
    return lax.rsqrt(jnp.sum(x * x, axis=-1, keepdims=True) * (1.0 / width) + EPS)


def _rms_bwd(dn, n, r, width):
    return r * (dn - n * (jnp.sum(dn * n, axis=-1, keepdims=True) * (1.0 / width)))


def _colsum(x):
    return jnp.sum(x, axis=0, keepdims=True)


def _row(v, k):
    rid = lax.broadcasted_iota(jnp.int32, v.shape, 0)
    return jnp.sum(jnp.where(rid == k, v, 0.0), axis=0, keepdims=True)


def _full(shape):
    return pl.BlockSpec(shape, lambda *_: (0,) * len(shape))


def _once(shape):
    return pl.BlockSpec(shape, lambda *_: (0,) * len(shape), pipeline_mode=pl.Buffered(1))


def _load_side_by_side(w_hbm, w_full, sem):
    width = w_hbm.shape[2]
    cps = [pltpu.make_async_copy(w_hbm.at[k], w_full.at[:, pl.ds(k * width, width)], sem.at[k]) for k in range(N_CHIPS)]
    for cp in cps:
        cp.start()
    for cp in cps:
        cp.wait()


def _in_proj(x, g1, w_in4, tm=512):
    T = x.shape[0]

    def body(x_ref, g_ref, w_hbm, u_ref, qkv_ref, hg_ref, w_full, sem):
        @pl.when(pl.program_id(0) == 0)
        def _():
            _load_side_by_side(w_hbm, w_full, sem)

        xv = x_ref[...]
        u = (xv * _rms(xv, D_MODEL) * g_ref[...]).astype(BF16)
        u_ref[...] = u
        p = _dot(u, w_full[...])
        qkv_ref[...] = p[:, :QKV_W]
        hg_ref[...] = p[:, QKV_W:]

    return pl.pallas_call(
        body, name="in_proj", grid=(T // tm,),
        in_specs=[pl.BlockSpec((tm, D_MODEL), lambda i: (i, 0)), _full((1, D_MODEL)), ANY],
        out_specs=[pl.BlockSpec((tm, D_MODEL), lambda i: (i, 0)), pl.BlockSpec((tm, QKV_W), lambda i: (i, 0)),
                   pl.BlockSpec((tm, HG_W), lambda i: (i, 0))],
        out_shape=[jax.ShapeDtypeStruct((T, D_MODEL), BF16), jax.ShapeDtypeStruct((T, QKV_W), F32),
                   jax.ShapeDtypeStruct((T, HG_W), F32)],
        scratch_shapes=[pltpu.VMEM((D_MODEL, IN_TOTAL), BF16), pltpu.SemaphoreType.DMA((N_CHIPS,))],
        compiler_params=_cp("arbitrary"),
    )(x, g1, w_in4)


def _attn_masks(bias_ref):
    lane = lax.broadcasted_iota(jnp.int32, (ATTN_BLK, HEAD_PAIR), 1)
    row = lax.broadcasted_iota(jnp.int32, (2 * ATTN_BLK, 2 * ATTN_BLK), 0)
    col = lax.broadcasted_iota(jnp.int32, (2 * ATTN_BLK, 2 * ATTN_BLK), 1)
    base = jnp.where(row >= ATTN_BLK, row - ATTN_BLK, row) - col
    for k in range(2):
        dist = base + k * ATTN_BLK
        bias_ref[k] = jnp.where((dist >= 0) & (dist <= ATTN_BLK), 0.0, NEG)
    bias_ref[2] = jnp.where(col >= ATTN_BLK, bias_ref[1], NEG)
    return lane < 64


def _two_heads(blk, first):
    zero = jnp.zeros_like(blk)
    return jnp.concatenate([jnp.where(first, blk, zero), jnp.where(first, zero, blk)], axis=0)


def _attn_rows(idx, nb, d):
    r, n = idx // nb, idx % nb
    kb = jnp.maximum(n - 1, 0)
    if d == 1:
        q0 = pl.multiple_of(n * ATTN_BLK, ATTN_BLK)
        k0 = pl.multiple_of(kb * ATTN_BLK, ATTN_BLK)
        return pl.ds(q0, ATTN_BLK), pl.ds(k0, 2 * ATTN_BLK), n - kb
    return (pl.ds(r + d * ATTN_BLK * n, ATTN_BLK, stride=d), pl.ds(r + d * ATTN_BLK * kb, 2 * ATTN_BLK, stride=d),
            n - kb)


def _attn_fwd(qkv):
    T = qkv.shape[0]

    n_blocks = T // ATTN_BLK

    def body(q_ref, k_ref, v_ref, o_ref, m_ref, l_ref, bias_ref):
        first = _attn_masks(bias_ref)
        for bi, d in enumerate(DILATIONS):
            nb = T // d // ATTN_BLK

            chains = ATTN_CHAINS_FWD
            per_chain = n_blocks // chains
            carried = d > 1 and per_chain % nb == 0

            def block(idx, kept=None, d=d, nb=nb, bi=bi, carried=carried):
                rows, keys, which = _attn_rows(idx, nb, d)
                q2 = _two_heads(q_ref[rows, :] * 0.125, first).astype(BF16)
                if carried:
                    k_own, v_own = k_ref[rows, :].astype(BF16), v_ref[rows, :].astype(BF16)
                    kw = jnp.concatenate([kept[0], k_own], axis=0)
                    vw = jnp.concatenate([kept[1], v_own], axis=0)
                    which = 2 - which
                else:
                    kw = k_ref[keys, :].astype(BF16)
                    vw = v_ref[keys, :].astype(BF16)
                old = (o_ref[rows, :], m_ref[rows, :], l_ref[rows, :]) if bi else None
                s = _dot_nt(q2, kw) + bias_ref[which]
                mb = jnp.max(s, axis=-1, keepdims=True)
                p = jnp.exp(s - mb)
                lb = jnp.sum(p, axis=-1, keepdims=True)
                o2 = _dot(p.astype(BF16), vw)
                o = jnp.where(first, o2[:ATTN_BLK], o2[ATTN_BLK:])
                m = jnp.where(first, mb[:ATTN_BLK], mb[ATTN_BLK:])
                l = jnp.where(first, lb[:ATTN_BLK], lb[ATTN_BLK:])
                if bi:
                    po, pm, pl_ = old
                    mn = jnp.maximum(pm, m)
                    wa = jnp.exp(pm - mn)
                    wb = jnp.exp(m - mn)
                    o, l, m = po * wa + o * wb, pl_ * wa + l * wb, mn
                return (rows, o, m, l), ((k_own, v_own) if carried else 0)

            def step(i, kept, block=block, carried=carried, chains=chains, per_chain=per_chain):
                done = [block(i + ch * per_chain, kept[ch] if carried else None) for ch in range(chains)]
                for (rows, o, m, l), _ in done:
                    o_ref[rows, :] = o
                    m_ref[rows, :] = m
                    l_ref[rows, :] = l
                return tuple(k for _, k in done) if carried else kept

            zero = jnp.zeros((ATTN_BLK, HEAD_PAIR), BF16)
            lax.fori_loop(0, per_chain, step, ((zero, zero),) * chains if carried else 0)

        def finish(i, carry):
            rows = pl.ds(pl.multiple_of(i * SUPER, SUPER), SUPER)
            l = l_ref[rows, :]
            o_ref[rows, :] = o_ref[rows, :] / l
            m_ref[rows, :] = m_ref[rows, :] + jnp.log(l)
            return carry

        lax.fori_loop(0, T // SUPER, finish, 0)

    col = lambda off: pl.BlockSpec((T, HEAD_PAIR), lambda j: (0, off + j))
    return pl.pallas_call(
        body, name="attn_fwd", grid=(4,),
        in_specs=[col(0), col(4), col(8)], out_specs=[col(0), col(0)],
        out_shape=[jax.ShapeDtypeStruct((T, ATTN_W), F32)] * 2,
        scratch_shapes=[pltpu.VMEM((T, HEAD_PAIR), F32), pltpu.VMEM((3, 2 * ATTN_BLK, 2 * ATTN_BLK), F32)],
        compiler_params=_cp("arbitrary"),
    )(qkv, qkv, qkv)


def _attn_bwd(qkv, o, lse, do, token=None):
    T = qkv.shape[0]
    per_chain = T // ATTN_BLK // ATTN_CHAINS
    extra = [] if token is None else [token]

    def body(q_ref, k_ref, v_ref, o_ref, lse_ref, do_ref, *rest):
        outs = rest[len(extra):len(extra) + 3]
        dq_ref, dk_ref, dv_ref, dkb_ref, dvb_ref, bias_ref = rest[len(extra) + 3:]
        first = _attn_masks(bias_ref)
        dq_ref[...] = jnp.zeros_like(dq_ref)
        dk_ref[...] = jnp.zeros_like(dk_ref)
        dv_ref[...] = jnp.zeros_like(dv_ref)

        def grads(rows, kw, vw, which):
            q2 = _two_heads(q_ref[rows, :] * 0.125, first).astype(BF16)
            lse_b = lse_ref[rows, :]
            dob = do_ref[rows, :]
            prod = dob * o_ref[rows, :]
            old = dq_ref[rows, :]
            lse2 = jnp.concatenate(
                [jnp.max(jnp.where(first, lse_b, NEG), axis=-1, keepdims=True),
                 jnp.max(jnp.where(first, NEG, lse_b), axis=-1, keepdims=True)], axis=0)
            p = jnp.exp(_dot_nt(q2, kw) + (bias_ref[which] - lse2))
            delta = jnp.concatenate(
                [jnp.sum(jnp.where(first, prod, 0.0), axis=-1, keepdims=True),
                 jnp.sum(jnp.where(first, 0.0, prod), axis=-1, keepdims=True)], axis=0)
            do2 = _two_heads(dob, first).astype(BF16)
            ds = (p * (_dot_nt(do2, vw) - delta)).astype(BF16)
            dq2 = _dot(ds, kw) * 0.125
            return (old + jnp.where(first, dq2[:ATTN_BLK], dq2[ATTN_BLK:]), _dot_tn(ds, q2),
                    _dot_tn(p.astype(BF16), do2))

        def block(idx):
            rows, keys, which = _attn_rows(idx, T // ATTN_BLK, 1)
            old = dk_ref[keys, :], dv_ref[keys, :]
            dq, ck, cv = grads(rows, k_ref[keys, :].astype(BF16), v_ref[keys, :].astype(BF16), which)
            return rows, keys, dq, old[0] + ck, old[1] + cv

        def step(i, carry):
            done = [block(i + ch * per_chain) for ch in range(ATTN_CHAINS)]
            for rows, keys, dq, dk, dv in done:
                dq_ref[rows, :] = dq
                dk_ref[keys, :] = dk
                dv_ref[keys, :] = dv
            return carry

        lax.fori_loop(0, per_chain, step, 0)

        for d in DILATIONS[1:]:
            nb = T // d // ATTN_BLK

            def block(idx, kept, d=d, nb=nb):
                r, n = idx // nb, idx % nb
                rows = pl.ds(r + d * ATTN_BLK * n, ATTN_BLK, stride=d)
                before = pl.ds(r + d * ATTN_BLK * jnp.maximum(n - 1, 0), ATTN_BLK, stride=d)
                k_prev, v_prev, dk_prev, dv_prev = kept
                k_own, v_own = k_ref[rows, :].astype(BF16), v_ref[rows, :].astype(BF16)
                dq, ck, cv = grads(rows, jnp.concatenate([k_prev, k_own], axis=0),
                                   jnp.concatenate([v_prev, v_own], axis=0), jnp.where(n > 0, 1, 2))
                stores = (rows, before, dq, dk_prev + ck[:ATTN_BLK], dv_prev + cv[:ATTN_BLK], ck[ATTN_BLK:], cv[ATTN_BLK:])
                return stores, (k_own, v_own, ck[ATTN_BLK:], cv[ATTN_BLK:])

            def step(i, kept, block=block):
                done = [block(i + ch * per_chain, kept[ch]) for ch in range(ATTN_CHAINS)]
                for (rows, before, dq, dk_done, dv_done, dk_own, dv_own), _ in done:
                    dq_ref[rows, :] = dq
                    dkb_ref[before, :] = dk_done
                    dvb_ref[before, :] = dv_done
                    dkb_ref[rows, :] = dk_own
                    dvb_ref[rows, :] = dv_own
                return tuple(k for _, k in done)

            zero = jnp.zeros((ATTN_BLK, HEAD_PAIR), F32)
            lax.fori_loop(0, per_chain, step, ((zero.astype(BF16), zero.astype(BF16), zero, zero),) * ATTN_CHAINS)

            def add(i, carry):
                rows = pl.ds(pl.multiple_of(i * SUPER, SUPER), SUPER)
                dk_ref[rows, :] += dkb_ref[rows, :]
                dv_ref[rows, :] += dvb_ref[rows, :]
                return carry

            lax.fori_loop(0, T // SUPER, add, 0)

        def emit(i, carry):
            rows = pl.ds(pl.multiple_of(i * SUPER, SUPER), SUPER)
            for out, acc in zip(outs, (dq_ref, dk_ref, dv_ref)):
                out[rows, :] = acc[rows, :].astype(BF16)
            return carry

        lax.fori_loop(0, T // SUPER, emit, 0)

    col = lambda off: pl.BlockSpec((T, HEAD_PAIR), lambda j: (0, off + j))
    return pl.pallas_call(
        body, name="attn_bwd", grid=(4,),
        in_specs=[col(0), col(4), col(8), col(0), col(0), col(0)] + [_full(t.shape) for t in extra],
        out_specs=[col(0)] * 3,
        out_shape=[jax.ShapeDtypeStruct((T, ATTN_W), BF16)] * 3,
        scratch_shapes=[pltpu.VMEM((T, HEAD_PAIR), F32)] * 5 + [pltpu.VMEM((3, 2 * ATTN_BLK, 2 * ATTN_BLK), F32)],
        compiler_params=_cp("arbitrary"),
    )(qkv, qkv, qkv, o, lse, do, *extra)


def _chunk_ids():
    row = lax.broadcasted_iota(jnp.int32, (SUPER, HGRN_DIM), 0)
    r2 = lax.broadcasted_iota(jnp.int32, (SUPER, SUPER), 0)
    c2 = lax.broadcasted_iota(jnp.int32, (SUPER, SUPER), 1)
    amask = ((r2 // HGRN_CHUNK) == (c2 // HGRN_CHUNK)) & (c2 <= r2)
    return row % HGRN_CHUNK, row // HGRN_CHUNK, amask


def _cumsum_chunk(x, rmod):
    s = 1
    while s < HGRN_CHUNK:
        x = x + jnp.where(rmod >= s, pltpu.roll(x, s, 0), 0.0)
        s *= 2
    return x


def _suffix_sum_chunk(x, rmod):
    s = 1
    while s < HGRN_CHUNK:
        x = x + jnp.where(rmod < HGRN_CHUNK - s, pltpu.roll(x, SUPER - s, 0), 0.0)
        s *= 2
    return x


def _chunk_rows(vs, cid):
    out = vs[-1]
    for c in reversed(range(len(vs) - 1)):
        out = jnp.where(cid == c, vs[c], out)
    return out


def _expand(x, cid):
    return jnp.concatenate([jnp.where(cid == c, x, 0.0) for c in range(SUPER // HGRN_CHUNK)], axis=1)


def _hgrn_gates(q, f, lbv, rmod, cid, tmp):
    sq = _sigmoid(q)
    sg = _sigmoid(f)
    forget = lbv + (1.0 - lbv) * sg
    key = 1.0 - forget
    b = _cumsum_chunk(jnp.log(forget), rmod)
    tmp[...] = b
    bends = [tmp[c * HGRN_CHUNK + HGRN_CHUNK - 1:(c + 1) * HGRN_CHUNK, :] for c in range(SUPER // HGRN_CHUNK)]
    eb = jnp.exp(b)
    enb = jnp.exp(-b)
    ebe = jnp.exp(_chunk_rows(bends, cid) - b)
    return sq, sg, forget, key, eb, enb, ebe, q * sq * eb, key * enb, key * ebe, [jnp.exp(v) for v in bends]


def _hgrn_fwd(hg, lb):
    T = hg.shape[0]
    nsc = T // SUPER
    NC = SUPER // HGRN_CHUNK

    def body(q_ref, f_ref, i_ref, lb_ref, o_ref, st_ref, state, tmp):
        rmod, cid, amask = _chunk_ids()
        state[...] = jnp.zeros_like(state)
        lbv = lb_ref[...]

        def local(sc, u):
            rows = pl.ds(pl.multiple_of(sc * SUPER, SUPER), SUPER)
            iv = i_ref[rows, :].astype(BF16)
            qd, ki, ke, dec = _hgrn_gates(q_ref[rows, :], f_ref[rows, :], lbv, rmod, cid, tmp.at[u])[-4:]
            a = jnp.where(amask, _dot_nt(qd.astype(BF16), ki.astype(BF16)), 0.0)
            return rows, qd, dec, _dot(a.astype(BF16), iv), _dot_tn(iv, _expand(ke, cid).astype(BF16))

        def step(i, carry):
            parts = [local(i * HGRN_SIDE + u, u) for u in range(HGRN_SIDE)]
            st = state[...]
            entering = []
            for u, (_, _, dec, _, ut) in enumerate(parts):
                st_ref[0, i * HGRN_SIDE + u] = st
                sts = []
                for c in range(NC):
                    sts.append(st)
                    st = st * dec[c] + ut[:, c * HGRN_DIM:(c + 1) * HGRN_DIM]
                entering.append(jnp.concatenate(sts, axis=1).astype(BF16))
            state[...] = st
            for (rows, qd, _, o, _), sts in zip(parts, entering):
                o_ref[rows, :] = o + _dot_nt(_expand(qd, cid).astype(BF16), sts)
            return carry

        lax.fori_loop(0, nsc // HGRN_SIDE, step, 0)

    col = lambda off: pl.BlockSpec((T, HGRN_DIM), lambda h: (0, off + h))
    return pl.pallas_call(
        body, name="hgrn_fwd", grid=(HGRN_HEADS,),
        in_specs=[col(0), col(4), col(8), pl.BlockSpec((1, HGRN_DIM), lambda h: (0, h))],
        out_specs=[pl.BlockSpec((T, HGRN_DIM), lambda h: (0, h)),
                   pl.BlockSpec((1, nsc, HGRN_DIM, HGRN_DIM), lambda h: (h, 0, 0, 0))],
        out_shape=[jax.ShapeDtypeStruct((T, HGRN_W), F32),
                   jax.ShapeDtypeStruct((HGRN_HEADS, nsc, HGRN_DIM, HGRN_DIM), F32)],
        scratch_shapes=[pltpu.VMEM((HGRN_DIM, HGRN_DIM), F32), pltpu.VMEM((HGRN_SIDE, SUPER, HGRN_DIM), F32)],
        compiler_params=_cp("arbitrary"),
    )(hg, hg, hg, lb)


def _hgrn_bwd(hg, lb, states, do):
    T = hg.shape[0]
    nsc = T // SUPER
    NC = SUPER // HGRN_CHUNK

    def body(q_ref, f_ref, i_ref, lb_ref, st_ref, do_ref, dq_ref, df_ref, di_ref, dlb_ref, dstate, tmp):
        rmod, cid, amask = _chunk_ids()
        dstate[...] = jnp.zeros_like(dstate)
        dlb_ref[...] = jnp.zeros_like(dlb_ref)
        lbv = lb_ref[...]

        def local(sc, u):
            rows = pl.ds(pl.multiple_of(sc * SUPER, SUPER), SUPER)
            q = q_ref[rows, :]
            ivf = i_ref[rows, :]
            iv = ivf.astype(BF16)
            dof = do_ref[rows, :]
            dob = dof.astype(BF16)
            sq, sg, forget, key, eb, enb, ebe, qd, ki, ke, dec = _hgrn_gates(q, f_ref[rows, :], lbv, rmod, cid,
                                                                            tmp.at[u])
            qdb, kib = qd.astype(BF16), ki.astype(BF16)
            keexp = _expand(ke, cid).astype(BF16)
            a = jnp.where(amask, _dot_nt(qdb, kib), 0.0).astype(BF16)
            ut = _dot_tn(iv, keexp)
            st = st_ref[0, sc]
            sts = []
            for c in range(NC):
                sts.append(st)
                st = st * dec[c] + ut[:, c * HGRN_DIM:(c + 1) * HGRN_DIM]
            gt = _dot_tn(dob, _expand(qd, cid).astype(BF16))
            da = jnp.where(amask, _dot_nt(dob, iv), 0.0).astype(BF16)
            ststack = jnp.concatenate(sts, axis=0).astype(BF16)
            return dict(rows=rows, q=q, sq=sq, sg=sg, forget=forget, eb=eb, enb=enb, ebe=ebe, qd=qd, ki=ki, ke=ke,
                        dec=dec, sts=sts, gt=gt, keexp=keexp, ivexp=_expand(ivf, cid).astype(BF16),
                        div=_dot_tn(a, dob), dki=_dot_tn(da, qdb),
                        dqd=_dot(da, kib) + _dot(_expand(dof, cid).astype(BF16), ststack))

        def finish(p, nxt, ddec):
            ncat = jnp.concatenate(nxt, axis=1).astype(BF16)
            nstack = jnp.concatenate(nxt, axis=0).astype(BF16)
            dke = _dot(p["ivexp"], nstack)
            dkk = dke * p["ke"]
            dkey = p["dki"] * p["enb"] + dke * p["ebe"]
            db = p["dqd"] * p["qd"] - p["dki"] * p["ki"] - dkk
            dbends = [_colsum(jnp.where(cid == c, dkk, 0.0)) + ddec[c] * p["dec"][c] for c in range(NC)]
            dforget = (_suffix_sum_chunk(db, rmod) + _chunk_rows(dbends, cid)) / p["forget"] - dkey
            sg, sq, q = p["sg"], p["sq"], p["q"]
            df_ref[p["rows"], :] = (dforget * (1.0 - lbv) * sg * (1.0 - sg)).astype(BF16)
            dq_ref[p["rows"], :] = (p["dqd"] * p["eb"] * (sq * (1.0 + q * (1.0 - sq)))).astype(BF16)
            di_ref[p["rows"], :] = (p["div"] + _dot_nt(p["keexp"], ncat)).astype(BF16)
            return _colsum(dforget * (1.0 - sg))

        def step(i, carry):
            parts = [local(nsc - 1 - (i * HGRN_SIDE + u), u) for u in range(HGRN_SIDE)]
            dst = dstate[...]
            chained = []
            for p in parts:
                nxt = [None] * NC
                ddec = [None] * NC
                for c in reversed(range(NC)):
                    nxt[c] = dst
                    ddec[c] = _colsum(dst * p["sts"][c])
                    dst = dst * p["dec"][c] + p["gt"][:, c * HGRN_DIM:(c + 1) * HGRN_DIM]
                chained.append((nxt, ddec))
            dstate[...] = dst
            dlb = dlb_ref[...]
            for p, (nxt, ddec) in zip(parts, chained):
                dlb = dlb + finish(p, nxt, ddec)
            dlb_ref[...] = dlb
            return carry

        lax.fori_loop(0, nsc // HGRN_SIDE, step, 0)

    col = lambda off: pl.BlockSpec((T, HGRN_DIM), lambda h: (0, off + h))
    own = pl.BlockSpec((T, HGRN_DIM), lambda h: (0, h))
    vec = pl.BlockSpec((1, HGRN_DIM), lambda h: (0, h))
    return pl.pallas_call(
        body, name="hgrn_bwd", grid=(HGRN_HEADS,),
        in_specs=[col(0), col(4), col(8), vec,
                  pl.BlockSpec((1, nsc, HGRN_DIM, HGRN_DIM), lambda h: (h, 0, 0, 0)), own],
        out_specs=[own, own, own, vec],
        out_shape=[jax.ShapeDtypeStruct((T, HGRN_W), BF16)] * 3 + [jax.ShapeDtypeStruct((1, HGRN_W), F32)],
        scratch_shapes=[pltpu.VMEM((HGRN_DIM, HGRN_DIM), F32), pltpu.VMEM((HGRN_SIDE, SUPER, HGRN_DIM), F32)],
        compiler_params=_cp("arbitrary"),
    )(hg, hg, hg, lb, states, do)


def _rec_heads(rec, gate, g_h):
    rr = jnp.concatenate(
        [jnp.broadcast_to(_rms(rec[:, h * HGRN_DIM:(h + 1) * HGRN_DIM], HGRN_DIM), (rec.shape[0], HGRN_DIM))
         for h in range(HGRN_HEADS)], axis=1)
    rn = rec * rr
    sg = _sigmoid(gate)
    return rr, rn, sg


def _mix_out(attn_o, rec_o, hg, x, g_a, g_h, w_out, tm=512):
    T = x.shape[0]

    def body(a_ref, r_ref, gt_ref, x_ref, ga_ref, gh_ref, w_ref, h1_ref, mixed_ref):
        a = a_ref[...]
        an = a * _rms(a, ATTN_W) * ga_ref[...]
        gate = gt_ref[...]
        _, rn, sg = _rec_heads(r_ref[...], gate, gh_ref[...])
        mixed = jnp.concatenate([an, rn * gh_ref[...] * (gate * sg)], axis=1).astype(BF16)
        mixed_ref[...] = mixed
        h1_ref[...] = x_ref[...] + _dot(mixed, w_ref[...])

    row = lambda w: pl.BlockSpec((tm, w), lambda i: (i, 0))
    return pl.pallas_call(
        body, name="mix_out", grid=(T // tm,),
        in_specs=[row(ATTN_W), row(HGRN_W), pl.BlockSpec((tm, HGRN_W), lambda i: (i, 3)), row(D_MODEL),
                  _full((1, ATTN_W)), _full((1, HGRN_W)), _once((D_MODEL, D_MODEL))],
        out_specs=[row(D_MODEL), row(D_MODEL)],
        out_shape=[jax.ShapeDtypeStruct((T, D_MODEL), F32), jax.ShapeDtypeStruct((T, D_MODEL), BF16)],
        compiler_params=_cp("arbitrary"),
    )(attn_o, rec_o, hg, x, g_a, g_h, w_out)


_INV_SQRT2 = 1.0 / math.sqrt(2.0)
_INV_SQRT2PI = 1.0 / math.sqrt(2.0 * math.pi)


def _gelu(x):
    return 0.5 * x * (1.0 + lax.erf(x * _INV_SQRT2))


def _gelu_and_grad(x):
    z = x * _INV_SQRT2
    cdf = 0.5 * (1.0 + lax.erf(z))
    return x * cdf, cdf + (x * _INV_SQRT2PI) * jnp.exp(-(z * z))


def _shift_down(g, prev, rowid):
    p1 = _row(prev, prev.shape[0] - 1)
    p2 = _row(prev, prev.shape[0] - 2)
    s1 = jnp.where(rowid == 0, p1, pltpu.roll(g, 1, 0))
    s2 = jnp.where(rowid == 0, p2, jnp.where(rowid == 1, p1, pltpu.roll(g, 2, 0)))
    return s1, s2


def _mlp_fwd(h1, g2, w_up4, conv_w, conv_b, w_down, gf, tgt, tm=256):
    T = h1.shape[0]

    def body(h_ref, g2_ref, wu_hbm, cw_ref, cb_ref, wd_ref, gf_ref, t_ref,
             u_ref, gate_ref, val_ref, conv_ref, act_ref, dh_ref, loss_ref, dgf_ref, carry, wu_ref, sem):
        i = pl.program_id(0)

        @pl.when(i == 0)
        def _():
            carry[...] = jnp.zeros_like(carry)
            loss_ref[...] = jnp.zeros_like(loss_ref)
            dgf_ref[...] = jnp.zeros_like(dgf_ref)
            _load_side_by_side(wu_hbm, wu_ref, sem)

        h = h_ref[...]
        u = (h * _rms(h, D_MODEL) * g2_ref[...]).astype(BF16)
        u_ref[...] = u
        y2 = jnp.zeros((tm, D_MODEL), F32)
        for lo, hi in FF_CHUNKS:
            cols = slice(lo, hi)
            rowid = lax.broadcasted_iota(jnp.int32, (tm, hi - lo), 0)
            gb = _dot(u, wu_ref[:, lo:hi]).astype(BF16)
            vb = _dot(u, wu_ref[:, D_FF + lo:D_FF + hi]).astype(BF16)
            gate_ref[:, cols] = gb
            val_ref[:, cols] = vb
            g = gb.astype(F32)
            s1, s2 = _shift_down(g, carry[:, cols], rowid)
            carry[:, cols] = g[tm - 8:, :]
            conv = cb_ref[:, cols] + cw_ref[0:1, cols] * s2 + cw_ref[1:2, cols] * s1 + cw_ref[2:3, cols] * g
            act = (_gelu(conv) * vb.astype(F32)).astype(BF16)
            conv_ref[:, cols] = conv.astype(BF16)
            act_ref[:, cols] = act
            y2 = y2 + _dot(act, wd_ref[cols, :])
        h2 = h + y2
        rf = _rms(h2, D_MODEL)
        n = h2 * rf
        gfv = gf_ref[...]
        e = n * gfv - t_ref[...]
        loss_ref[...] += jnp.sum(e * e) * (0.5 / D_MODEL)
        dy = e * (1.0 / D_MODEL)
        dgf_ref[...] += _colsum(dy * n)
        dh_ref[...] = _rms_bwd(dy * gfv, n, rf, D_MODEL)

    row = lambda w: pl.BlockSpec((tm, w), lambda i: (i, 0))
    return pl.pallas_call(
        body, name="mlp_fwd", grid=(T // tm,),
        in_specs=[row(D_MODEL), _full((1, D_MODEL)), ANY, _full((3, D_FF)),
                  _full((1, D_FF)), _once((D_FF, D_MODEL)), _full((1, D_MODEL)), row(D_MODEL)],
        out_specs=[row(D_MODEL), row(D_FF), row(D_FF), row(D_FF), row(D_FF), row(D_MODEL), _full((1, 128)),
                   _full((1, D_MODEL))],
        out_shape=[jax.ShapeDtypeStruct((T, D_MODEL), BF16)] + [jax.ShapeDtypeStruct((T, D_FF), BF16)] * 4
        + [jax.ShapeDtypeStruct((T, D_MODEL), F32),
                   jax.ShapeDtypeStruct((1, 128), F32), jax.ShapeDtypeStruct((1, D_MODEL), F32)],
        scratch_shapes=[pltpu.VMEM((8, D_FF), F32), pltpu.VMEM((D_MODEL, 2 * D_FF), BF16),
                        pltpu.SemaphoreType.DMA((N_CHIPS,))],
        compiler_params=_cp("arbitrary"),
    )(h1, g2, w_up4, conv_w, conv_b, w_down, gf, tgt)


def _mlp_bwd(dh2, gate, val, conv, conv_w, w_down, tm=256):
    T = dh2.shape[0]
    nb = T // tm
    half = D_FF // 2

    def body(dh_ref, gate_ref, val_ref, conv_ref, cw_ref, wd_ref, dgv_ref, dcw_ref, dcb_ref, carry):
        @pl.when(pl.program_id(0) == 0)
        def _():
            carry[...] = jnp.zeros_like(carry)
            dcw_ref[...] = jnp.zeros_like(dcw_ref)
            dcb_ref[...] = jnp.zeros_like(dcb_ref)

        dhb = dh_ref[...].astype(BF16)
        rowid = lax.broadcasted_iota(jnp.int32, (tm, half), 0)
        for c in range(2):
            cols = slice(c * half, (c + 1) * half)
            g = gate_ref[:, cols].astype(F32)
            v = val_ref[:, cols].astype(F32)
            cv = conv_ref[:, cols].astype(F32)
            dact = _dot_nt(dhb, wd_ref[cols, :])
            gl, gp = _gelu_and_grad(cv)
            dconv = dact * v * gp
            nxt = carry[:, cols]
            n0, n1 = _row(nxt, 0), _row(nxt, 1)
            u1 = jnp.where(rowid == tm - 1, n0, pltpu.roll(dconv, tm - 1, 0))
            u2 = jnp.where(rowid == tm - 1, n1, jnp.where(rowid == tm - 2, n0, pltpu.roll(dconv, tm - 2, 0)))
            carry[:, cols] = dconv[0:8, :]
            dcb_ref[:, cols] += _colsum(dconv)
            dcw_ref[0:1, cols] += _colsum(u2 * g)
            dcw_ref[1:2, cols] += _colsum(u1 * g)
            dcw_ref[2:3, cols] += _colsum(dconv * g)
            dgate = cw_ref[2:3, cols] * dconv + cw_ref[1:2, cols] * u1 + cw_ref[0:1, cols] * u2
            dgv_ref[:, cols] = dgate.astype(BF16)
            dgv_ref[:, D_FF + c * half:D_FF + (c + 1) * half] = (dact * gl).astype(BF16)

    rev = lambda w: pl.BlockSpec((tm, w), lambda i: (nb - 1 - i, 0))
    return pl.pallas_call(
        body, name="mlp_bwd", grid=(nb,),
        in_specs=[rev(D_MODEL), rev(D_FF), rev(D_FF), rev(D_FF), _full((3, D_FF)), _once((D_FF, D_MODEL))],
        out_specs=[rev(2 * D_FF), _full((3, D_FF)), _full((1, D_FF))],
        out_shape=[jax.ShapeDtypeStruct((T, 2 * D_FF), BF16), jax.ShapeDtypeStruct((3, D_FF), F32),
                   jax.ShapeDtypeStruct((1, D_FF), F32)],
        scratch_shapes=[pltpu.VMEM((8, D_FF), F32)],
        compiler_params=_cp("arbitrary"),
    )(dh2, gate, val, conv, conv_w, w_down)


def _up_out_bwd(dgv, w_up4, h1, g2, dh2, w_out, attn_o, rec_o, hg, g_a, g_h, tm=256):
    T = h1.shape[0]

    def body(dgv_ref, wu_hbm, h_ref, g2_ref, dh2_ref, wo_ref, a_ref, r_ref, gt_ref, ga_ref, gh_ref,
             dh1_ref, dg2_ref, da_ref, dr_ref, dgt_ref, dga_ref, dgh_ref, wu_ref, sem):
        @pl.when(pl.program_id(0) == 0)
        def _():
            dg2_ref[...] = jnp.zeros_like(dg2_ref)
            dga_ref[...] = jnp.zeros_like(dga_ref)
            dgh_ref[...] = jnp.zeros_like(dgh_ref)
            _load_side_by_side(wu_hbm, wu_ref, sem)

        du = _dot_nt(dgv_ref[...], wu_ref[...])
        h = h_ref[...]
        r = _rms(h, D_MODEL)
        n = h * r
        dg2_ref[...] += _colsum(du * n)
        dh1 = dh2_ref[...] + _rms_bwd(du * g2_ref[...], n, r, D_MODEL)
        dh1_ref[...] = dh1
        dmix = _dot_nt(dh1.astype(BF16), wo_ref[...])
        dan = dmix[:, :ATTN_W]
        a = a_ref[...]
        ra = _rms(a, ATTN_W)
        na = a * ra
        dga_ref[...] += _colsum(dan * na)
        da_ref[...] = _rms_bwd(dan * ga_ref[...], na, ra, ATTN_W)
        dmr = dmix[:, ATTN_W:]
        gate = gt_ref[...]
        ghv = gh_ref[...]
        rr, rn, sg = _rec_heads(r_ref[...], gate, ghv)
        dgt_ref[...] = (dmr * rn * ghv * (sg * (1.0 + gate * (1.0 - sg)))).astype(BF16)
        drecn = dmr * (gate * sg)
        dgh_ref[...] += _colsum(drecn * rn)
        drn = drecn * ghv
        prod = drn * rn
        mean = jnp.concatenate(
            [jnp.broadcast_to(jnp.sum(prod[:, h_ * HGRN_DIM:(h_ + 1) * HGRN_DIM], axis=-1, keepdims=True),
                              (tm, HGRN_DIM)) for h_ in range(HGRN_HEADS)], axis=1) * (1.0 / HGRN_DIM)
        dr_ref[...] = rr * (drn - rn * mean)

    row = lambda w: pl.BlockSpec((tm, w), lambda i: (i, 0))
    return pl.pallas_call(
        body, name="up_out_bwd", grid=(T // tm,),
        in_specs=[row(2 * D_FF), ANY, row(D_MODEL), _full((1, D_MODEL)),
                  row(D_MODEL), _once((D_MODEL, D_MODEL)), row(ATTN_W), row(HGRN_W),
                  pl.BlockSpec((tm, HGRN_W), lambda i: (i, 3)), _full((1, ATTN_W)), _full((1, HGRN_W))],
        out_specs=[row(D_MODEL), _full((1, D_MODEL)), row(ATTN_W), row(HGRN_W), row(HGRN_W),
                   _full((1, ATTN_W)), _full((1, HGRN_W))],
        out_shape=[jax.ShapeDtypeStruct((T, D_MODEL), F32), jax.ShapeDtypeStruct((1, D_MODEL), F32),
                   jax.ShapeDtypeStruct((T, ATTN_W), F32), jax.ShapeDtypeStruct((T, HGRN_W), F32),
                   jax.ShapeDtypeStruct((T, HGRN_W), BF16), jax.ShapeDtypeStruct((1, ATTN_W), F32),
                   jax.ShapeDtypeStruct((1, HGRN_W), F32)],
        scratch_shapes=[pltpu.VMEM((D_MODEL, 2 * D_FF), BF16), pltpu.SemaphoreType.DMA((N_CHIPS,))],
        compiler_params=_cp("arbitrary"),
    )(dgv, w_up4, h1, g2, dh2, w_out, attn_o, rec_o, hg, g_a, g_h)


def _in_bwd(dqkv, dhg, w_in4, x, g1, dh1, tm=512):
    T = x.shape[0]

    def body(*refs):
        parts = refs[:7]
        w_hbm, x_ref, g_ref, dh1_ref, dp_ref, dx_ref, dg_ref, w_full, sem = refs[7:]

        @pl.when(pl.program_id(0) == 0)
        def _():
            dg_ref[...] = jnp.zeros_like(dg_ref)
            _load_side_by_side(w_hbm, w_full, sem)

        dp = jnp.concatenate([p[...] for p in parts], axis=1)
        dp_ref[...] = dp
        du = _dot_nt(dp, w_full[...])
        xv = x_ref[...]
        r = _rms(xv, D_MODEL)
        n = xv * r
        dg_ref[...] += _colsum(du * n)
        dx_ref[...] = dh1_ref[...] + _rms_bwd(du * g_ref[...], n, r, D_MODEL)

    row = lambda w: pl.BlockSpec((tm, w), lambda i: (i, 0))
    return pl.pallas_call(
        body, name="in_bwd", grid=(T // tm,),
        in_specs=[row(ATTN_W)] * 7 + [ANY, row(D_MODEL), _full((1, D_MODEL)), row(D_MODEL)],
        out_specs=[row(IN_TOTAL), row(D_MODEL), _full((1, D_MODEL))],
        out_shape=[jax.ShapeDtypeStruct((T, IN_TOTAL), BF16), jax.ShapeDtypeStruct((T, D_MODEL), F32),
                   jax.ShapeDtypeStruct((1, D_MODEL), F32)],
        scratch_shapes=[pltpu.VMEM((D_MODEL, IN_TOTAL), BF16), pltpu.SemaphoreType.DMA((N_CHIPS,))],
        compiler_params=_cp("arbitrary"),
    )(*dqkv, *dhg, w_in4, x, g1, dh1)


def _dw(a, b, kb, nb_, name, tk=1024, side=1):
    T, K = a.shape
    N = b.shape[1]
    nk, nn, nt = K // kb, N // (nb_ * side), T // tk

    def body(a_ref, b_ref, o_ref, acc):
        t = pl.program_id(2)

        @pl.when(t == 0)
        def _():
            acc[...] = jnp.zeros_like(acc)

        acc[...] += _dot_tn(a_ref[...], b_ref[...].astype(BF16))

        @pl.when(t == nt - 1)
        def _():
            for s in range(side):
                o_ref[s] = acc[:, s * nb_:(s + 1) * nb_].astype(BF16)

    return pl.pallas_call(
        body, name=name, grid=(nk, nn, nt),
        in_specs=[pl.BlockSpec((tk, kb), lambda i, j, t: (t, i)),
                  pl.BlockSpec((tk, nb_ * side), lambda i, j, t: (t, j))],
        out_specs=pl.BlockSpec((side, kb, nb_), lambda i, j, t: (i * nn + j, 0, 0)),
        out_shape=jax.ShapeDtypeStruct((nk * nn * side, kb, nb_), BF16),
        scratch_shapes=[pltpu.VMEM((kb, nb_ * side), F32)],
        compiler_params=_cp("arbitrary", "arbitrary", "arbitrary"),
    )(a, b)


def _step_channel(a, x, tgt, g_a, g_h, w_out, g2, w_up4, conv_w, conv_b, w_down, gf):
    h1, mixed = _mix_out(a["attn_o"], a["rec_o"], a["hg"], x, g_a, g_h, w_out)
    u2, gate, val, conv, act, dh2, loss, dgf = _mlp_fwd(h1, g2, w_up4, conv_w, conv_b, w_down, gf, tgt)
    dgv, dcw, dcb = _mlp_bwd(dh2, gate, val, conv, conv_w, w_down)
    dw_down = _dw(act, dh2, D_FF // 2, D_MODEL, "dw_down").reshape(N_CHIPS, D_FF // N_CHIPS, D_MODEL)
    dh1, dg2, da, dr, dgt, dga, dgh = _up_out_bwd(dgv, w_up4, h1, g2, dh2, w_out, a["attn_o"], a["rec_o"], a["hg"],
                                                  g_a, g_h)
    dw_up = _dw(u2, dgv, D_MODEL, UP_SHARD, "dw_up", side=2)
    dw_out = _dw(mixed, dh1, D_MODEL, D_MODEL, "dw_out").reshape(N_CHIPS, D_MODEL // N_CHIPS, D_MODEL)
    return dict(loss=loss, dgf=dgf, dcw=dcw, dcb=dcb, dg2=dg2, dga=dga, dgh=dgh, dh1=dh1, da=da, dr=dr, dgt=dgt,
                dw_down=dw_down, dw_up=dw_up, dw_out=dw_out)


def _step_mixers_bwd(a, b, x, g1, w_in4, lb, dqkv):
    dhq, dhf, dhi, dlb = _hgrn_bwd(a["hg"], lb, a["states"], b["dr"])
    dproj, dx, dg1 = _in_bwd(dqkv, [dhq, dhf, dhi, b["dgt"]], w_in4, x, g1, b["dh1"])
    dw_in = _dw(a["u1"], dproj, D_MODEL, IN_SHARD, "dw_in", side=2)
    return dict(dx=dx, dg1=dg1, dlb=dlb, dw_in=dw_in)


BIG = ("w_in", "w_out", "w_up", "w_down")
ANY = pl.BlockSpec(memory_space=pl.ANY)


def _place():
    x, y, c = lax.axis_index("x"), lax.axis_index("y"), lax.axis_index("c")
    chips = [(1 - x, y), (x, 1 - y), (1 - x, 1 - y)]
    return x, y, c, chips


def _remote(src, dst, send_sems, recv_sems, k, to):
    return pltpu.make_async_remote_copy(src_ref=src, dst_ref=dst, send_sem=send_sems.at[k], recv_sem=recv_sems.at[k],
                                        device_id=to, device_id_type=MESH)


def _gather_weights(shards, conv_w):
    n = len(shards)
    halves = [s.shape[0] // 2 for s in shards]

    def body(*refs):
        ins, cw, outs, ocw = refs[:n], refs[n], refs[n + 1:2 * n + 1], refs[2 * n + 1]
        send_sems, recv_sems = refs[2 * n + 2:]
        x, y, c, chips = _place()
        me, sibling = 2 * x + y, (x, y, 1 - c)

        def part(w, chip, half):
            return outs[w].at[chip, pl.ds(half * halves[w], halves[w]), :]

        sent = []
        for j, chip in enumerate(chips):
            for w in range(n):
                sent.append(_remote(ins[w].at[pl.ds(c * halves[w], halves[w]), :], part(w, me, c),
                                    send_sems, recv_sems, w * 3 + j, (*chip, c)))
            sent.append(_remote(cw, ocw.at[me], send_sems, recv_sems, 6 * n + j, (*chip, c)))
        for cp in sent:
            cp.start()
        for j, chip in enumerate(chips):
            kj = 2 * chip[0] + chip[1]
            for w in range(n):
                _remote(part(w, kj, c), part(w, kj, c), send_sems, recv_sems, w * 3 + j, (*chip, c)).wait_recv()
                fwd = _remote(part(w, kj, c), part(w, kj, c), send_sems, recv_sems, 3 * n + w * 3 + j, sibling)
                fwd.start()
                sent.append(fwd)
        for j, chip in enumerate(chips):
            kj = 2 * chip[0] + chip[1]
            for w in range(n):
                _remote(part(w, kj, 1 - c), part(w, kj, 1 - c), send_sems, recv_sems, 3 * n + w * 3 + j,
                        sibling).wait_recv()
            _remote(cw, ocw.at[kj], send_sems, recv_sems, 6 * n + j, (*chip, c)).wait_recv()
        for cp in sent:
            cp.wait_send()

    n_sem = 6 * n + 3
    outs = pl.pallas_call(
        body, name="gather_weights",
        in_specs=[ANY] * (n + 1), out_specs=[ANY] * (n + 1),
        out_shape=[jax.ShapeDtypeStruct((N_CHIPS,) + s.shape, s.dtype) for s in shards]
        + [jax.ShapeDtypeStruct((N_CHIPS,) + conv_w.shape, conv_w.dtype)],
        scratch_shapes=[pltpu.SemaphoreType.DMA((n_sem,)), pltpu.SemaphoreType.DMA((n_sem,))],
    )(*shards, conv_w)
    chip = 2 * lax.axis_index("x") + lax.axis_index("y")
    return [lax.dynamic_update_slice(o, s[None], (chip,) + (0,) * s.ndim) for o, s in zip(outs, [*shards, conv_w])]


def _allreduce_small(buf):
    rows = buf.shape[0]

    def body(in_ref, out_ref, slots, send_sems, recv_sems):
        x, y, c, _ = _place()
        me = 4 * x + 2 * y + c
        slots[me] = in_ref[...]
        sent = []
        for p in range(1, 8):
            to = (x ^ (p >> 2), y ^ ((p >> 1) & 1), c ^ (p & 1))
            sent.append(_remote(in_ref, slots.at[me], send_sems, recv_sems, p, to))
        for cp in sent:
            cp.start()
        for p in range(1, 8):
            frm = 4 * (x ^ (p >> 2)) + 2 * (y ^ ((p >> 1) & 1)) + (c ^ (p & 1))
            _remote(in_ref, slots.at[frm], send_sems, recv_sems, p, (x, y, c)).wait_recv()
        for cp in sent:
            cp.wait_send()
        acc = slots[0]
        for d in range(1, 8):
            acc = acc + slots[d]
        out_ref[...] = acc

    vm = pl.BlockSpec(memory_space=pltpu.VMEM)
    return pl.pallas_call(
        body, name="allreduce_small", in_specs=[vm], out_specs=vm,
        out_shape=jax.ShapeDtypeStruct(buf.shape, F32),
        scratch_shapes=[pltpu.VMEM((8, rows, 128), F32), pltpu.SemaphoreType.DMA((8,)), pltpu.SemaphoreType.DMA((8,))],
    )(buf)


def _sibling_peer():
    x, y, c, _ = _place()
    return [(x, y, 1 - c)]


def _chip_peers():
    x, y, c, chips = _place()
    return [(*chip, c) for chip in chips]


def _handshake(peers):
    barrier = pltpu.get_barrier_semaphore()
    for peer in peers:
        pl.semaphore_signal(barrier, inc=1, device_id=peer, device_id_type=MESH)
    pl.semaphore_wait(barrier, len(peers))


def _pair_exchange(gs, name, barrier_id):
    n = len(gs)
    halves = [g.shape[1] // 2 for g in gs]

    def body(*refs):
        g, got = refs[:n], refs[n:2 * n]
        send_sems, recv_sems = refs[2 * n:]
        _handshake(_sibling_peer())
        x, y, c, _ = _place()
        cps = [_remote(g[w].at[:, pl.ds((1 - c) * halves[w], halves[w]), :], got[w], send_sems, recv_sems, w,
                       (x, y, 1 - c)) for w in range(n)]
        for cp in cps:
            cp.start()
        for cp in cps:
            cp.wait()

    return pl.pallas_call(
        body, name=name, in_specs=[ANY] * n, out_specs=[ANY] * n,
        out_shape=[jax.ShapeDtypeStruct((N_CHIPS, h, g.shape[2]), g.dtype) for g, h in zip(gs, halves)],
        scratch_shapes=[pltpu.SemaphoreType.DMA((n,)), pltpu.SemaphoreType.DMA((n,))],
        compiler_params=pltpu.CompilerParams(collective_id=barrier_id),
    )(*gs)


def _core_id():
    return lax.axis_index("c").reshape(1).astype(jnp.int32)


def _pair_sum(g, got, name):
    h, C = got.shape[1:]

    def body(c_ref, g_ref, b_ref, o_ref):
        o_ref[...] = (g_ref[...].astype(F32) + b_ref[...].astype(F32)).astype(BF16)

    blk = pl.BlockSpec((1, h, C), lambda k, c_ref: (k, 0, 0))
    return pl.pallas_call(
        body, name=name,
        grid_spec=pltpu.PrefetchScalarGridSpec(
            num_scalar_prefetch=1, grid=(N_CHIPS,),
            in_specs=[pl.BlockSpec((1, h, C), lambda k, c_ref: (k, c_ref[0], 0)), blk], out_specs=blk),
        out_shape=jax.ShapeDtypeStruct(got.shape, BF16), compiler_params=_cp("arbitrary"))(_core_id(), g, got)


def _sum_partials(g, got, landed, name):
    h, C = got.shape[1:]

    def body(ids, g_ref, b_ref, l_ref, o_ref):
        acc = g_ref[0].astype(F32) + b_ref[0].astype(F32)
        for j in range(3):
            acc = acc + l_ref[j].astype(F32)
        o_ref[...] = acc

    ids = jnp.stack([2 * lax.axis_index("x") + lax.axis_index("y"), lax.axis_index("c")]).astype(jnp.int32)
    return pl.pallas_call(
        body, name=name,
        grid_spec=pltpu.PrefetchScalarGridSpec(
            num_scalar_prefetch=1, grid=(1,),
            in_specs=[pl.BlockSpec((1, h, C), lambda i, ids: (ids[0], ids[1], 0)),
                      pl.BlockSpec((1, h, C), lambda i, ids: (ids[0], 0, 0)),
                      pl.BlockSpec((3, h, C), lambda i, ids: (0, 0, 0))],
            out_specs=pl.BlockSpec((h, C), lambda i, ids: (ids[1], 0))),
        out_shape=jax.ShapeDtypeStruct((2 * h, C), F32), compiler_params=_cp("arbitrary"))(ids, g, got, landed)


def _pair_share(reds, name, barrier_id):
    n = len(reds)

    def body(*refs):
        out = refs[n:2 * n]
        send_sems, recv_sems = refs[2 * n:]
        _handshake(_sibling_peer())
        x, y, c, _ = _place()
        def half(w, which):
            h = out[w].shape[0] // 2
            return out[w].at[pl.ds(which * h, h), :]

        cps = [_remote(half(w, c), half(w, c), send_sems, recv_sems, w, (x, y, 1 - c)) for w in range(n)]
        for cp in cps:
            cp.start()
        for w in range(n):
            _remote(half(w, 1 - c), half(w, 1 - c), send_sems, recv_sems, w, (x, y, 1 - c)).wait_recv()
        for cp in cps:
            cp.wait_send()

    return pl.pallas_call(
        body, name=name, in_specs=[ANY] * n, out_specs=[ANY] * n,
        out_shape=[jax.ShapeDtypeStruct(r.shape, F32) for r in reds],
        input_output_aliases={w: w for w in range(n)},
        scratch_shapes=[pltpu.SemaphoreType.DMA((n,)), pltpu.SemaphoreType.DMA((n,))],
        compiler_params=pltpu.CompilerParams(collective_id=barrier_id),
    )(*reds)


HBM = pl.BlockSpec(memory_space=pltpu.HBM)
SEM = pl.BlockSpec(memory_space=pltpu.SEMAPHORE)
DATAFLOW = pltpu.SideEffectType.DATAFLOW_SIDE_EFFECTING


def _copies_start(name, srcs, lands, plan, n_copies, after, peers, barrier_id):
    ns, nb, na = len(srcs), len(srcs) + len(lands), len(after)

    def body(*refs):
        src_refs, land_refs = refs[:ns], refs[ns:nb]
        send_sems, recv_sems = refs[nb + na:nb + na + 2]
        token = refs[-1]
        _handshake(peers())
        for k, (src, there, _, to) in enumerate(plan(src_refs, land_refs)):
            _remote(src, there, send_sems, recv_sems, k, to).start()
        token[...] = jnp.zeros_like(token)

    hbm = lambda a: pltpu.HBM(a.shape, a.dtype)
    outs = pl.pallas_call(
        body, name=name,
        out_shape=(pltpu.SemaphoreType.DMA((n_copies,)), pltpu.SemaphoreType.DMA((n_copies,)),
                   *[hbm(a) for a in srcs], *[hbm(a) for a in lands], jax.ShapeDtypeStruct((8, 128), F32)),
        in_specs=[HBM] * nb + [ANY] * na,
        out_specs=(SEM, SEM, *[HBM] * nb, pl.BlockSpec(memory_space=pltpu.VMEM)),
        input_output_aliases={i: 2 + i for i in range(nb)},
        compiler_params=pltpu.CompilerParams(has_side_effects=DATAFLOW, collective_id=barrier_id),
    )(*[pltpu.with_memory_space_constraint(a, pltpu.HBM) for a in (*srcs, *lands)], *after)
    return outs[0], outs[1], outs[2:2 + ns], outs[2 + ns:2 + nb], outs[-1]


def _copies_wait(name, send_sems, recv_sems, srcs, lands, plan, after):
    ns, nb, na = len(srcs), len(srcs) + len(lands), len(after)

    def body(*refs):
        src_refs, land_refs = refs[:ns], refs[ns:nb]
        send_sems, recv_sems = refs[nb:nb + 2]
        for k, (src, _, here, to) in enumerate(plan(src_refs, land_refs)):
            cp = _remote(src, here, send_sems, recv_sems, k, to)
            cp.wait_send()
            cp.wait_recv()

    hbm = lambda a: pltpu.HBM(a.shape, a.dtype)
    outs = pl.pallas_call(
        body, name=name,
        out_shape=(*[hbm(a) for a in srcs], *[hbm(a) for a in lands]),
        in_specs=[HBM] * nb + [SEM, SEM] + [ANY] * na,
        out_specs=tuple([HBM] * nb),
        input_output_aliases={i: i for i in range(nb)},
        compiler_params=pltpu.CompilerParams(has_side_effects=DATAFLOW),
    )(*srcs, *lands, send_sems, recv_sems, *after)
    return outs[:ns], outs[ns:]


def _gather_plan(halves):
    def plan(shards, lands):
        x, y, c, chips = _place()
        me = 2 * x + y
        copies = []
        for w, h in enumerate(halves):
            rows = pl.ds(c * h, h)
            for chip in chips:
                copies.append((shards[w].at[rows, :], lands[w].at[me, rows, :],
                               lands[w].at[2 * chip[0] + chip[1], rows, :], (*chip, c)))
        return copies
    return plan


def _reduce_plan(n):
    def plan(ps, lands):
        x, y, c, chips = _place()
        return [(ps[w].at[2 * chip[0] + chip[1]], lands[w].at[j], lands[w].at[j], (*chip, c))
                for w in range(n) for j, chip in enumerate(chips)]
    return plan


def _forward_plan(halves):
    def plan(_, lands):
        x, y, c, chips = _place()

        def part(w, chip, half):
            return lands[w].at[2 * chip[0] + chip[1], pl.ds(half * halves[w], halves[w]), :]

        return [(part(w, chip, c), part(w, chip, c), part(w, chip, 1 - c), (x, y, 1 - c))
                for w in range(len(halves)) for chip in chips]
    return plan


def _pair_plan(halves):
    def plan(gs, gots):
        x, y, c, _ = _place()
        return [(gs[w].at[:, pl.ds((1 - c) * h, h), :], gots[w], gots[w], (x, y, 1 - c)) for w, h in enumerate(halves)]
    return plan


def _place_own(gathered, shards):
    chip = 2 * lax.axis_index("x") + lax.axis_index("y")
    return [lax.dynamic_update_slice(o, s[None], (chip, 0, 0)) for o, s in zip(gathered, shards)]


def _adamw(w, g, m, v, name, tr=None):
    R, C = w.shape
    tr = tr or R // 4

    def body(w_ref, g_ref, m_ref, v_ref, d_ref, nm_ref, nv_ref):
        d_ref[...], nm_ref[...], nv_ref[...] = _adamw_math(w_ref[...], g_ref[...], m_ref[...], v_ref[...])

    blk = pl.BlockSpec((tr, C), lambda i: (i, 0))
    return pl.pallas_call(body, name=name, grid=(R // tr,), in_specs=[blk] * 4, out_specs=[blk] * 3,
                          out_shape=[jax.ShapeDtypeStruct((R, C), F32)] * 3, compiler_params=_cp("arbitrary"))(w, g, m, v)


SMALL = (("norm1_g", 1, 1024), ("attn_norm_g", 1, 512), ("hgrn_norm_g", 1, 512), ("hgrn_lb_logits", 2, 512),
         ("norm2_g", 1, 1024), ("conv_b", 1, D_FF), ("final_norm_g", 1, 1024), ("conv_w", 3, D_FF))
LOSS_ROW = sum(r * c for _, r, c in SMALL) // 128
SMALL_ROWS = 136


def _rows_to_lanes(ref, row, width):
    return jnp.concatenate([ref[row + j:row + j + 1, :] for j in range(width // 128)], axis=1)


def _pack_small(grads, dlb, lb, loss):
    def body(*refs):
        parts, dlb_ref, lb_ref, loss_ref, out = refs[:len(SMALL) - 1], refs[-4], refs[-3], refs[-2], refs[-1]
        out[...] = jnp.zeros_like(out)
        lbv = lb_ref[...]
        dl = dlb_ref[...] * lbv * (1.0 - lbv)
        row = 0
        parts = list(parts)
        for name, rows, width in SMALL:
            for r in range(rows):
                if name == "hgrn_lb_logits":
                    src = dl if r == 0 else -dl
                    for j in range(width // 128):
                        out[row + j:row + j + 1, :] = src[:, 128 * j:128 * (j + 1)]
                else:
                    for j in range(width // 128):
                        out[row + j:row + j + 1, :] = parts[0][r:r + 1, 128 * j:128 * (j + 1)]
                row += width // 128
            if name != "hgrn_lb_logits":
                parts.pop(0)
        out[LOSS_ROW:LOSS_ROW + 1, :] = loss_ref[...]

    vm = pl.BlockSpec(memory_space=pltpu.VMEM)
    return pl.pallas_call(body, name="pack_small", in_specs=[vm] * (len(grads) + 3), out_specs=vm,
                          out_shape=jax.ShapeDtypeStruct((SMALL_ROWS, 128), F32))(*grads, dlb, lb, loss)


def _adamw_math(w, g, m, v):
    nm = ADAM_B1 * m + (1.0 - ADAM_B1) * g
    nv = ADAM_B2 * v + (1.0 - ADAM_B2) * (g * g)
    m_hat = nm / (1.0 - ADAM_B1 ** ADAM_STEP)
    v_hat = nv / (1.0 - ADAM_B2 ** ADAM_STEP)
    return -ADAM_LR * (m_hat / (jnp.sqrt(v_hat) + ADAM_EPS) + ADAM_WD * w), nm, nv


def _small_update(summed, g_conv_w, ws, ms, vs):
    n = len(SMALL)

    def body(*refs):
        s_ref, gcw_ref = refs[:2]
        w_refs, m_refs, v_refs = refs[2:2 + n], refs[2 + n:2 + 2 * n], refs[2 + 2 * n:2 + 3 * n]
        outs = refs[2 + 3 * n:]
        row = 0
        for k, (name, rows, width) in enumerate(SMALL):
            if name == "conv_w":
                g = gcw_ref[...]
            else:
                g = jnp.concatenate([_rows_to_lanes(s_ref, row + r * (width // 128), width) for r in range(rows)], axis=0)
            row += rows * (width // 128)
            d, nm, nv = _adamw_math(w_refs[k][...], g, m_refs[k][...], v_refs[k][...])
            for o, val in zip(outs[4 * k:4 * k + 4], (g, d, nm, nv)):
                o[...] = val

    vm = pl.BlockSpec(memory_space=pltpu.VMEM)
    outs = pl.pallas_call(
        body, name="small_update", in_specs=[vm] * (2 + 3 * n), out_specs=[vm] * (4 * n),
        out_shape=[jax.ShapeDtypeStruct(a.shape, F32) for a in ws for _ in range(4)],
    )(summed, g_conv_w, *ws, *ms, *vs)
    return [outs[4 * k:4 * k + 4] for k in range(n)]


def kernel(x, norm1_g, w_in, attn_norm_g, hgrn_norm_g, hgrn_lb_logits, w_out, norm2_g, w_up, conv_w, conv_b, w_down, final_norm_g, loss_target, m_norm1_g, m_w_in, m_attn_norm_g, m_hgrn_norm_g, m_hgrn_lb_logits, m_w_out, m_norm2_g, m_w_up, m_conv_w, m_conv_b, m_w_down, m_final_norm_g, v_norm1_g, v_w_in, v_attn_norm_g, v_hgrn_norm_g, v_hgrn_lb_logits, v_w_out, v_norm2_g, v_w_up, v_conv_w, v_conv_b, v_w_down, v_final_norm_g):
    w = dict(norm1_g=norm1_g, w_in=w_in, attn_norm_g=attn_norm_g, hgrn_norm_g=hgrn_norm_g,
             hgrn_lb_logits=hgrn_lb_logits, w_out=w_out, norm2_g=norm2_g, w_up=w_up, conv_w=conv_w, conv_b=conv_b,
             w_down=w_down, final_norm_g=final_norm_g)
    m = dict(norm1_g=m_norm1_g, w_in=m_w_in, attn_norm_g=m_attn_norm_g, hgrn_norm_g=m_hgrn_norm_g,
             hgrn_lb_logits=m_hgrn_lb_logits, w_out=m_w_out, norm2_g=m_norm2_g, w_up=m_w_up, conv_w=m_conv_w,
             conv_b=m_conv_b, w_down=m_w_down, final_norm_g=m_final_norm_g)
    v = dict(norm1_g=v_norm1_g, w_in=v_w_in, attn_norm_g=v_attn_norm_g, hgrn_norm_g=v_hgrn_norm_g,
             hgrn_lb_logits=v_hgrn_lb_logits, w_out=v_w_out, norm2_g=v_norm2_g, w_up=v_w_up, conv_w=v_conv_w,
             conv_b=v_conv_b, w_down=v_w_down, final_norm_g=v_final_norm_g)
    names = list(w)
    chip = 2 * lax.axis_index("x") + lax.axis_index("y")

    shards = {k: w[k][0].astype(BF16) for k in BIG}
    w_in4, conv_w4 = _gather_weights([shards["w_in"]], conv_w[0])
    conv_w_full = jnp.transpose(conv_w4, (1, 0, 2)).reshape(3, D_FF)
    lb = jax.nn.softmax(hgrn_lb_logits, axis=0)[0:1]
    late = [shards[k] for k in BIG[1:]]
    gather_plan = _gather_plan([s.shape[0] // 2 for s in late])
    started = _copies_start("gather_start", late, [lax.empty((N_CHIPS,) + s.shape, BF16) for s in late], gather_plan,
                            3 * len(late), after=(w_in4,), peers=_chip_peers, barrier_id=0)
    u1, qkv, hg = _in_proj(x[0], norm1_g + started[4][0:1, 0:1], w_in4)
    attn_o, lse = _attn_fwd(qkv)
    late, landed_w = _copies_wait("gather_wait", *started[:4], gather_plan, after=(attn_o,))
    forward_plan = _forward_plan([s.shape[0] // 2 for s in late])
    started = _copies_start("forward_start", [], landed_w, forward_plan, 3 * len(late), after=(),
                            peers=_sibling_peer, barrier_id=1)
    rec_o, states = _hgrn_fwd(hg, lb + started[4][0:1, 0:1])
    a = dict(u1=u1, qkv=qkv, hg=hg, attn_o=attn_o, lse=lse, rec_o=rec_o, states=states)
    w_out4, w_up4, w_down4 = _place_own(
        _copies_wait("forward_wait", *started[:4], forward_plan, after=(rec_o,))[1], late)

    b = _step_channel(a, x[0], loss_target[0], attn_norm_g, hgrn_norm_g, w_out4.reshape(D_MODEL, D_MODEL), norm2_g,
                      w_up4, conv_w_full, conv_b, w_down4.reshape(D_FF, D_MODEL), final_norm_g.reshape(1, D_MODEL))

    early = [b["dw_out"], b["dw_up"], b["dw_down"]]
    pair_plan = _pair_plan([gk.shape[1] // 2 for gk in early])
    started = _copies_start("pair_start", early,
                            [lax.empty((N_CHIPS, gk.shape[1] // 2, gk.shape[2]), BF16) for gk in early], pair_plan,
                            len(early), after=(), peers=_sibling_peer, barrier_id=2)
    dqkv = _attn_bwd(qkv, attn_o, lse, b["da"], started[4])
    early, gots = _copies_wait("pair_wait", *started[:4], pair_plan, after=(dqkv[0],))
    ps = [_pair_sum(gk, got, f"pair_sum_{k}") for gk, got, k in zip(early, gots, BIG[1:])]
    reduce_plan = _reduce_plan(len(ps))
    started = _copies_start("reduce_start", ps, [lax.empty((3,) + p.shape[1:], BF16) for p in ps], reduce_plan,
                            3 * len(ps), after=(), peers=_chip_peers, barrier_id=3)
    c = _step_mixers_bwd(a, b, x[0], norm1_g, w_in4, lb + started[4][0:1, 0:1], dqkv)
    gots_in = _pair_exchange([c["dw_in"]], "pair_exchange_w_in", barrier_id=4)
    ps_in = _pair_sum(c["dw_in"], gots_in[0], "pair_sum_w_in")
    plan_in = _reduce_plan(1)
    started_in = _copies_start("reduce_start_w_in", [ps_in], [lax.empty((3,) + ps_in.shape[1:], BF16)], plan_in, 3,
                               after=(), peers=_chip_peers, barrier_id=5)
    landed = _copies_wait("reduce_wait", *started[:4], reduce_plan, after=(started_in[4],))[1]
    reds = [_sum_partials(gk, got, l, f"sum_partials_{k}") for gk, got, l, k in zip(early, gots, landed, BIG[1:])]
    g = dict(zip(BIG[1:], _pair_share(reds, "pair_share", barrier_id=6)))
    delta, new_m, new_v = {}, {}, {}
    for k in BIG[1:]:
        delta[k], new_m[k], new_v[k] = _adamw(w[k][0], g[k], m[k][0], v[k][0], f"adamw_{k}")

    loss, dx = b["loss"], c["dx"]
    small = dict(g1=c["dg1"], g_a=b["dga"], g_h=b["dgh"], lb=c["dlb"], g2=b["dg2"], conv_w=b["dcw"], conv_b=b["dcb"],
                 gf=b["dgf"])
    summed = _allreduce_small(_pack_small(
        [small["g1"], small["g_a"], small["g_h"], small["g2"], small["conv_b"], small["gf"], small["conv_w"]],
        small["lb"], lb, loss))
    loss_total = summed[LOSS_ROW, 0]
    g_conv_w = lax.dynamic_slice(summed[LOSS_ROW - 3 * D_FF // 128:LOSS_ROW].reshape(3, D_FF),
                                 (0, chip * (D_FF // N_CHIPS)), (3, D_FF // N_CHIPS))
    two_d = lambda p, k: p[k].reshape(-1, p[k].shape[-1])
    updated = _small_update(summed, g_conv_w, *[[two_d(p, k) for k, _, _ in SMALL] for p in (w, m, v)])
    for (k, _, _), parts in zip(SMALL, updated):
        g[k], delta[k], new_m[k], new_v[k] = (a.reshape(w[k].shape) for a in parts)

    landed_in = _copies_wait("reduce_wait_w_in", *started_in[:4], plan_in, after=(updated[0][1], delta["w_up"]))[1]
    red_in = _sum_partials(c["dw_in"], gots_in[0], landed_in[0], "sum_partials_w_in")
    g["w_in"] = _pair_share([red_in], "pair_share_w_in", barrier_id=7)[0]
    delta["w_in"], new_m["w_in"], new_v["w_in"] = _adamw(w_in[0], g["w_in"], m_w_in[0], v_w_in[0], "adamw_w_in")
    for k in BIG:
        g[k], delta[k], new_m[k], new_v[k] = g[k][None], delta[k][None], new_m[k][None], new_v[k][None]

    return (loss_total, dx[None], *[g[k] for k in names], *[delta[k] for k in names],
            *[new_m[k] for k in names], *[new_v[k] for k in names])
```

```python
import math

import jax
import jax.numpy as jnp
from jax import lax
from jax.experimental import pallas as pl
from jax.experimental.pallas import tpu as pltpu

F32 = jnp.float32
BF16 = jnp.bfloat16

D_MODEL = 1024
ATTN_W = 512
HGRN_W = 512
HEAD_PAIR = 128
ATTN_BLK = 128
DILATIONS = (1, 4, 16)
ATTN_CHAINS = 4
ATTN_CHAINS_FWD = 8
HGRN_HEADS = 4
HGRN_DIM = 128
HGRN_CHUNK = 64
SUPER = 256
HGRN_SIDE = 4
D_FF = 2816
FF_CHUNKS = ((0, 1536), (1536, D_FF))
MLP_BWD_CHUNKS = ((0, 768), (768, 1408), (1408, 2176), (2176, D_FF))
N_CHIPS = 4
IN_TOTAL = 3584
IN_SHARD = IN_TOTAL // N_CHIPS
UP_SHARD = 2 * D_FF // N_CHIPS
QKV_W = 3 * ATTN_W
HG_W = 4 * HGRN_W
EPS = 1e-6
NEG = -1e30
V7X_VMEM_BYTES = 64 * 1024 * 1024
VMEM_LIMIT = V7X_VMEM_BYTES - 8 * 1024 * 1024

ADAM_LR = 0.001
ADAM_B1 = 0.9
ADAM_B2 = 0.999
ADAM_EPS = 1e-08
ADAM_WD = 0.01
ADAM_STEP = 10

MESH = pl.DeviceIdType.MESH


def _cp(*sem):
    return pltpu.CompilerParams(dimension_semantics=sem or None, vmem_limit_bytes=VMEM_LIMIT)


def _dot(a, b):
    return jnp.dot(a, b, preferred_element_type=F32)


def _dot_nt(a, b):
    return lax.dot_general(a, b, (((1,), (1,)), ((), ())), preferred_element_type=F32)


def _dot_tn(a, b):
    return lax.dot_general(a, b, (((0,), (0,)), ((), ())), preferred_element_type=F32)


def _sigmoid(x):
    return 1.0 / (1.0 + jnp.exp(-x))


def _rms(x, width):
    return lax.rsqrt(jnp.sum(x * x, axis=-1, keepdims=True) * (1.0 / width) + EPS)


def _rms_bwd(dn, n, r, width):
    return r * (dn - n * (jnp.sum(dn * n, axis=-1, keepdims=True) * (1.0 / width)))


def _colsum(x):
    return jnp.sum(x, axis=0, keepdims=True)


def _row(v, k):
    rid = lax.broadcasted_iota(jnp.int32, v.shape, 0)
    return jnp.sum(jnp.where(rid == k, v, 0.0), axis=0, keepdims=True)


def _full(shape):
    return pl.BlockSpec(shape, lambda *_: (0,) * len(shape))


def _once(shape):
    return pl.BlockSpec(shape, lambda *_: (0,) * len(shape), pipeline_mode=pl.Buffered(1))


def _load_side_by_side(w_hbm, w_full, sem):
    width = w_hbm.shape[2]
    cps = [pltpu.make_async_copy(w_hbm.at[k], w_full.at[:, pl.ds(k * width, width)], sem.at[k]) for k in range(N_CHIPS)]
    for cp in cps:
        cp.start()
    for cp in cps:
        cp.wait()


def _in_proj(x, g1, w_in4, tm=512):
    T = x.shape[0]

    def body(x_ref, g_ref, w_hbm, u_ref, qkv_ref, hg_ref, w_full, sem):
        @pl.when(pl.program_id(0) == 0)
        def _():
            _load_side_by_side(w_hbm, w_full, sem)

        xv = x_ref[...]
        u = (xv * _rms(xv, D_MODEL) * g_ref[...]).astype(BF16)
        u_ref[...] = u
        p = _dot(u, w_full[...])
        qkv_ref[...] = p[:, :QKV_W]
        hg_ref[...] = p[:, QKV_W:]

    return pl.pallas_call(
        body, name="in_proj", grid=(T // tm,),
        in_specs=[pl.BlockSpec((tm, D_MODEL), lambda i: (i, 0)), _full((1, D_MODEL)), ANY],
        out_specs=[pl.BlockSpec((tm, D_MODEL), lambda i: (i, 0)), pl.BlockSpec((tm, QKV_W), lambda i: (i, 0)),
                   pl.BlockSpec((tm, HG_W), lambda i: (i, 0))],
        out_shape=[jax.ShapeDtypeStruct((T, D_MODEL), BF16), jax.ShapeDtypeStruct((T, QKV_W), F32),
                   jax.ShapeDtypeStruct((T, HG_W), F32)],
        scratch_shapes=[pltpu.VMEM((D_MODEL, IN_TOTAL), BF16), pltpu.SemaphoreType.DMA((N_CHIPS,))],
        compiler_params=_cp("arbitrary"),
    )(x, g1, w_in4)


def _attn_masks(bias_ref):
    lane = lax.broadcasted_iota(jnp.int32, (ATTN_BLK, HEAD_PAIR), 1)
    row = lax.broadcasted_iota(jnp.int32, (2 * ATTN_BLK, 2 * ATTN_BLK), 0)
    col = lax.broadcasted_iota(jnp.int32, (2 * ATTN_BLK, 2 * ATTN_BLK), 1)
    base = jnp.where(row >= ATTN_BLK, row - ATTN_BLK, row) - col
    for k in range(2):
        dist = base + k * ATTN_BLK
        bias_ref[k] = jnp.where((dist >= 0) & (dist <= ATTN_BLK), 0.0, NEG)
    bias_ref[2] = jnp.where(col >= ATTN_BLK, bias_ref[1], NEG)
    return lane < 64


def _two_heads(blk, first):
    zero = jnp.zeros_like(blk)
    return jnp.concatenate([jnp.where(first, blk, zero), jnp.where(first, zero, blk)], axis=0)


def _attn_rows(idx, nb, d):
    r, n = idx // nb, idx % nb
    kb = jnp.maximum(n - 1, 0)
    if d == 1:
        q0 = pl.multiple_of(n * ATTN_BLK, ATTN_BLK)
        k0 = pl.multiple_of(kb * ATTN_BLK, ATTN_BLK)
        return pl.ds(q0, ATTN_BLK), pl.ds(k0, 2 * ATTN_BLK), n - kb
    return (pl.ds(r + d * ATTN_BLK * n, ATTN_BLK, stride=d), pl.ds(r + d * ATTN_BLK * kb, 2 * ATTN_BLK, stride=d),
            n - kb)


def _attn_fwd(qkv):
    T = qkv.shape[0]

    n_blocks = T // ATTN_BLK

    def body(q_ref, k_ref, v_ref, o_ref, m_ref, l_ref, bias_ref):
        first = _attn_masks(bias_ref)
        for bi, d in enumerate(DILATIONS):
            nb = T // d // ATTN_BLK

            chains = ATTN_CHAINS_FWD
            per_chain = n_blocks // chains
            carried = d > 1 and per_chain % nb == 0

            def block(idx, kept=None, d=d, nb=nb, bi=bi, carried=carried):
                rows, keys, which = _attn_rows(idx, nb, d)
                q2 = _two_heads(q_ref[rows, :] * 0.125, first).astype(BF16)
                if carried:
                    k_own, v_own = k_ref[rows, :].astype(BF16), v_ref[rows, :].astype(BF16)
                    kw = jnp.concatenate([kept[0], k_own], axis=0)
                    vw = jnp.concatenate([kept[1], v_own], axis=0)
                    which = 2 - which
                else:
                    kw = k_ref[keys, :].astype(BF16)
                    vw = v_ref[keys, :].astype(BF16)
                old = (o_ref[rows, :], m_ref[rows, :], l_ref[rows, :]) if bi else None
                s = _dot_nt(q2, kw) + bias_ref[which]
                mb = jnp.max(s, axis=-1, keepdims=True)
                p = jnp.exp(s - mb)
                lb = jnp.sum(p, axis=-1, keepdims=True)
                o2 = _dot(p.astype(BF16), vw)
                o = jnp.where(first, o2[:ATTN_BLK], o2[ATTN_BLK:])
                m = jnp.where(first, mb[:ATTN_BLK], mb[ATTN_BLK:])
                l = jnp.where(first, lb[:ATTN_BLK], lb[ATTN_BLK:])
                if bi:
                    po, pm, pl_ = old
                    mn = jnp.maximum(pm, m)
                    wa = jnp.exp(pm - mn)
                    wb = jnp.exp(m - mn)
                    o, l, m = po * wa + o * wb, pl_ * wa + l * wb, mn
                return (rows, o, m, l), ((k_own, v_own) if carried else 0)

            def step(i, kept, block=block, carried=carried, chains=chains, per_chain=per_chain):
                done = [block(i + ch * per_chain, kept[ch] if carried else None) for ch in range(chains)]
                for (rows, o, m, l), _ in done:
                    o_ref[rows, :] = o
                    m_ref[rows, :] = m
                    l_ref[rows, :] = l
                return tuple(k for _, k in done) if carried else kept

            zero = jnp.zeros((ATTN_BLK, HEAD_PAIR), BF16)
            lax.fori_loop(0, per_chain, step, ((zero, zero),) * chains if carried else 0)

        def finish(i, carry):
            rows = pl.ds(pl.multiple_of(i * SUPER, SUPER), SUPER)
            l = l_ref[rows, :]
            o_ref[rows, :] = o_ref[rows, :] / l
            m_ref[rows, :] = m_ref[rows, :] + jnp.log(l)
            return carry

        lax.fori_loop(0, T // SUPER, finish, 0)

    col = lambda off: pl.BlockSpec((T, HEAD_PAIR), lambda j: (0, off + j))
    return pl.pallas_call(
        body, name="attn_fwd", grid=(4,),
        in_specs=[col(0), col(4), col(8)], out_specs=[col(0), col(0)],
        out_shape=[jax.ShapeDtypeStruct((T, ATTN_W), F32)] * 2,
        scratch_shapes=[pltpu.VMEM((T, HEAD_PAIR), F32), pltpu.VMEM((3, 2 * ATTN_BLK, 2 * ATTN_BLK), F32)],
        compiler_params=_cp("arbitrary"),
    )(qkv, qkv, qkv)


def _attn_bwd(qkv, o, lse, do, token=None):
    T = qkv.shape[0]
    per_chain = T // ATTN_BLK // ATTN_CHAINS
    extra = [] if token is None else [token]

    def body(q_ref, k_ref, v_ref, o_ref, lse_ref, do_ref, *rest):
        outs = rest[len(extra):len(extra) + 3]
        dq_ref, dk_ref, dv_ref, dkb_ref, dvb_ref, bias_ref = rest[len(extra) + 3:]
        first = _attn_masks(bias_ref)
        dq_ref[...] = jnp.zeros_like(dq_ref)
        dk_ref[...] = jnp.zeros_like(dk_ref)
        dv_ref[...] = jnp.zeros_like(dv_ref)

        def grads(rows, kw, vw, which):
            q2 = _two_heads(q_ref[rows, :] * 0.125, first).astype(BF16)
            lse_b = lse_ref[rows, :]
            dob = do_ref[rows, :]
            prod = dob * o_ref[rows, :]
            old = dq_ref[rows, :]
            lse2 = jnp.concatenate(
                [jnp.max(jnp.where(first, lse_b, NEG), axis=-1, keepdims=True),
                 jnp.max(jnp.where(first, NEG, lse_b), axis=-1, keepdims=True)], axis=0)
            p = jnp.exp(_dot_nt(q2, kw) + (bias_ref[which] - lse2))
            delta = jnp.concatenate(
                [jnp.sum(jnp.where(first, prod, 0.0), axis=-1, keepdims=True),
                 jnp.sum(jnp.where(first, 0.0, prod), axis=-1, keepdims=True)], axis=0)
            do2 = _two_heads(dob, first).astype(BF16)
            ds = (p * (_dot_nt(do2, vw) - delta)).astype(BF16)
            dq2 = _dot(ds, kw) * 0.125
            return (old + jnp.where(first, dq2[:ATTN_BLK], dq2[ATTN_BLK:]), _dot_tn(ds, q2),
                    _dot_tn(p.astype(BF16), do2))

        def block(idx):
            rows, keys, which = _attn_rows(idx, T // ATTN_BLK, 1)
            old = dk_ref[keys, :], dv_ref[keys, :]
            dq, ck, cv = grads(rows, k_ref[keys, :].astype(BF16), v_ref[keys, :].astype(BF16), which)
            return rows, keys, dq, old[0] + ck, old[1] + cv

        def step(i, carry):
            done = [block(i + ch * per_chain) for ch in range(ATTN_CHAINS)]
            for rows, keys, dq, dk, dv in done:
                dq_ref[rows, :] = dq
                dk_ref[keys, :] = dk
                dv_ref[keys, :] = dv
            return carry

        lax.fori_loop(0, per_chain, step, 0)

        for d in DILATIONS[1:]:
            nb = T // d // ATTN_BLK

            def block(idx, kept, d=d, nb=nb):
                r, n = idx // nb, idx % nb
                rows = pl.ds(r + d * ATTN_BLK * n, ATTN_BLK, stride=d)
                before = pl.ds(r + d * ATTN_BLK * jnp.maximum(n - 1, 0), ATTN_BLK, stride=d)
                k_prev, v_prev, dk_prev, dv_prev = kept
                k_own, v_own = k_ref[rows, :].astype(BF16), v_ref[rows, :].astype(BF16)
                dq, ck, cv = grads(rows, jnp.concatenate([k_prev, k_own], axis=0),
                                   jnp.concatenate([v_prev, v_own], axis=0), jnp.where(n > 0, 1, 2))
                stores = (rows, before, dq, dk_prev + ck[:ATTN_BLK], dv_prev + cv[:ATTN_BLK], ck[ATTN_BLK:], cv[ATTN_BLK:])
                return stores, (k_own, v_own, ck[ATTN_BLK:], cv[ATTN_BLK:])

            def step(i, kept, block=block):
                done = [block(i + ch * per_chain, kept[ch]) for ch in range(ATTN_CHAINS)]
                for (rows, before, dq, dk_done, dv_done, dk_own, dv_own), _ in done:
                    dq_ref[rows, :] = dq
                    dkb_ref[before, :] = dk_done
                    dvb_ref[before, :] = dv_done
                    dkb_ref[rows, :] = dk_own
                    dvb_ref[rows, :] = dv_own
                return tuple(k for _, k in done)

            zero = jnp.zeros((ATTN_BLK, HEAD_PAIR), F32)
            lax.fori_loop(0, per_chain, step, ((zero.astype(BF16), zero.astype(BF16), zero, zero),) * ATTN_CHAINS)

            def add(i, carry):
                rows = pl.ds(pl.multiple_of(i * SUPER, SUPER), SUPER)
                dk_ref[rows, :] += dkb_ref[rows, :]
                dv_ref[rows, :] += dvb_ref[rows, :]
                return carry

            lax.fori_loop(0, T // SUPER, add, 0)

        def emit(i, carry):
            rows = pl.ds(pl.multiple_of(i * SUPER, SUPER), SUPER)
            for out, acc in zip(outs, (dq_ref, dk_ref, dv_ref)):
                out[rows, :] = acc[rows, :].astype(BF16)
            return carry

        lax.fori_loop(0, T // SUPER, emit, 0)

    col = lambda off: pl.BlockSpec((T, HEAD_PAIR), lambda j: (0, off + j))
    return pl.pallas_call(
        body, name="attn_bwd", grid=(4,),
        in_specs=[col(0), col(4), col(8), col(0), col(0), col(0)] + [_full(t.shape) for t in extra],
        out_specs=[col(0)] * 3,
        out_shape=[jax.ShapeDtypeStruct((T, ATTN_W), BF16)] * 3,
        scratch_shapes=[pltpu.VMEM((T, HEAD_PAIR), F32)] * 5 + [pltpu.VMEM((3, 2 * ATTN_BLK, 2 * ATTN_BLK), F32)],
        compiler_params=_cp("arbitrary"),
    )(qkv, qkv, qkv, o, lse, do, *extra)


def _chunk_ids():
    row = lax.broadcasted_iota(jnp.int32, (SUPER, HGRN_DIM), 0)
    r2 = lax.broadcasted_iota(jnp.int32, (SUPER, SUPER), 0)
    c2 = lax.broadcasted_iota(jnp.int32, (SUPER, SUPER), 1)
    amask = ((r2 // HGRN_CHUNK) == (c2 // HGRN_CHUNK)) & (c2 <= r2)
    return row % HGRN_CHUNK, row // HGRN_CHUNK, amask


def _cumsum_chunk(x, rmod):
    s = 1
    while s < HGRN_CHUNK:
        x = x + jnp.where(rmod >= s, pltpu.roll(x, s, 0), 0.0)
        s *= 2
    return x


def _suffix_sum_chunk(x, rmod):
    s = 1
    while s < HGRN_CHUNK:
        x = x + jnp.where(rmod < HGRN_CHUNK - s, pltpu.roll(x, SUPER - s, 0), 0.0)
        s *= 2
    return x


def _chunk_rows(vs, cid):
    out = vs[-1]
    for c in reversed(range(len(vs) - 1)):
        out = jnp.where(cid == c, vs[c], out)
    return out


def _expand(x, cid):
    return jnp.concatenate([jnp.where(cid == c, x, 0.0) for c in range(SUPER // HGRN_CHUNK)], axis=1)


def _hgrn_gates(q, f, lbv, rmod, cid, tmp):
    sq = _sigmoid(q)
    sg = _sigmoid(f)
    forget = lbv + (1.0 - lbv) * sg
    key = 1.0 - forget
    b = _cumsum_chunk(jnp.log(forget), rmod)
    tmp[...] = b
    bends = [tmp[c * HGRN_CHUNK + HGRN_CHUNK - 1:(c + 1) * HGRN_CHUNK, :] for c in range(SUPER // HGRN_CHUNK)]
    eb = jnp.exp(b)
    enb = jnp.exp(-b)
    ebe = jnp.exp(_chunk_rows(bends, cid) - b)
    return sq, sg, forget, key, eb, enb, ebe, q * sq * eb, key * enb, key * ebe, [jnp.exp(v) for v in bends]


def _hgrn_fwd(hg, lb):
    T = hg.shape[0]
    nsc = T // SUPER
    NC = SUPER // HGRN_CHUNK

    def body(q_ref, f_ref, i_ref, lb_ref, o_ref, st_ref, state, tmp):
        rmod, cid, amask = _chunk_ids()
        state[...] = jnp.zeros_like(state)
        lbv = lb_ref[...]

        def local(sc, u):
            rows = pl.ds(pl.multiple_of(sc * SUPER, SUPER), SUPER)
            iv = i_ref[rows, :].astype(BF16)
            qd, ki, ke, dec = _hgrn_gates(q_ref[rows, :], f_ref[rows, :], lbv, rmod, cid, tmp.at[u])[-4:]
            a = jnp.where(amask, _dot_nt(qd.astype(BF16), ki.astype(BF16)), 0.0)
            return rows, qd, dec, _dot(a.astype(BF16), iv), _dot_tn(iv, _expand(ke, cid).astype(BF16))

        def step(i, carry):
            parts = [local(i * HGRN_SIDE + u, u) for u in range(HGRN_SIDE)]
            st = state[...]
            entering = []
            for u, (_, _, dec, _, ut) in enumerate(parts):
                st_ref[0, i * HGRN_SIDE + u] = st
                sts = []
                for c in range(NC):
                    sts.append(st)
                    st = st * dec[c] + ut[:, c * HGRN_DIM:(c + 1) * HGRN_DIM]
                entering.append(jnp.concatenate(sts, axis=1).astype(BF16))
            state[...] = st
            for (rows, qd, _, o, _), sts in zip(parts, entering):
                o_ref[rows, :] = o + _dot_nt(_expand(qd, cid).astype(BF16), sts)
            return carry

        lax.fori_loop(0, nsc // HGRN_SIDE, step, 0)

    col = lambda off: pl.BlockSpec((T, HGRN_DIM), lambda h: (0, off + h))
    return pl.pallas_call(
        body, name="hgrn_fwd", grid=(HGRN_HEADS,),
        in_specs=[col(0), col(4), col(8), pl.BlockSpec((1, HGRN_DIM), lambda h: (0, h))],
        out_specs=[pl.BlockSpec((T, HGRN_DIM), lambda h: (0, h)),
                   pl.BlockSpec((1, nsc, HGRN_DIM, HGRN_DIM), lambda h: (h, 0, 0, 0))],
        out_shape=[jax.ShapeDtypeStruct((T, HGRN_W), F32),
                   jax.ShapeDtypeStruct((HGRN_HEADS, nsc, HGRN_DIM, HGRN_DIM), F32)],
        scratch_shapes=[pltpu.VMEM((HGRN_DIM, HGRN_DIM), F32), pltpu.VMEM((HGRN_SIDE, SUPER, HGRN_DIM), F32)],
        compiler_params=_cp("arbitrary"),
    )(hg, hg, hg, lb)


def _hgrn_bwd(hg, lb, states, do):
    T = hg.shape[0]
    nsc = T // SUPER
    NC = SUPER // HGRN_CHUNK

    def body(q_ref, f_ref, i_ref, lb_ref, st_ref, do_ref, dq_ref, df_ref, di_ref, dlb_ref, dstate, tmp):
        rmod, cid, amask = _chunk_ids()
        dstate[...] = jnp.zeros_like(dstate)
        dlb_ref[...] = jnp.zeros_like(dlb_ref)
        lbv = lb_ref[...]

        def local(sc, u):
            rows = pl.ds(pl.multiple_of(sc * SUPER, SUPER), SUPER)
            q = q_ref[rows, :]
            ivf = i_ref[rows, :]
            iv = ivf.astype(BF16)
            dof = do_ref[rows, :]
            dob = dof.astype(BF16)
            sq, sg, forget, key, eb, enb, ebe, qd, ki, ke, dec = _hgrn_gates(q, f_ref[rows, :], lbv, rmod, cid,
                                                                            tmp.at[u])
            qdb, kib = qd.astype(BF16), ki.astype(BF16)
            keexp = _expand(ke, cid).astype(BF16)
            a = jnp.where(amask, _dot_nt(qdb, kib), 0.0).astype(BF16)
            ut = _dot_tn(iv, keexp)
            st = st_ref[0, sc]
            sts = []
            for c in range(NC):
                sts.append(st)
                st = st * dec[c] + ut[:, c * HGRN_DIM:(c + 1) * HGRN_DIM]
            gt = _dot_tn(dob, _expand(qd, cid).astype(BF16))
            da = jnp.where(amask, _dot_nt(dob, iv), 0.0).astype(BF16)
            ststack = jnp.concatenate(sts, axis=0).astype(BF16)
            return dict(rows=rows, q=q, sq=sq, sg=sg, forget=forget, eb=eb, enb=enb, ebe=ebe, qd=qd, ki=ki, ke=ke,
                        dec=dec, sts=sts, gt=gt, keexp=keexp, ivexp=_expand(ivf, cid).astype(BF16),
                        div=_dot_tn(a, dob), dki=_dot_tn(da, qdb),
                        dqd=_dot(da, kib) + _dot(_expand(dof, cid).astype(BF16), ststack))

        def finish(p, nxt, ddec):
            ncat = jnp.concatenate(nxt, axis=1).astype(BF16)
            nstack = jnp.concatenate(nxt, axis=0).astype(BF16)
            dke = _dot(p["ivexp"], nstack)
            dkk = dke * p["ke"]
            dkey = p["dki"] * p["enb"] + dke * p["ebe"]
            db = p["dqd"] * p["qd"] - p["dki"] * p["ki"] - dkk
            dbends = [_colsum(jnp.where(cid == c, dkk, 0.0)) + ddec[c] * p["dec"][c] for c in range(NC)]
            dforget = (_suffix_sum_chunk(db, rmod) + _chunk_rows(dbends, cid)) / p["forget"] - dkey
            sg, sq, q = p["sg"], p["sq"], p["q"]
            df_ref[p["rows"], :] = (dforget * (1.0 - lbv) * sg * (1.0 - sg)).astype(BF16)
            dq_ref[p["rows"], :] = (p["dqd"] * p["eb"] * (sq * (1.0 + q * (1.0 - sq)))).astype(BF16)
            di_ref[p["rows"], :] = (p["div"] + _dot_nt(p["keexp"], ncat)).astype(BF16)
            return _colsum(dforget * (1.0 - sg))

        def step(i, carry):
            parts = [local(nsc - 1 - (i * HGRN_SIDE + u), u) for u in range(HGRN_SIDE)]
            dst = dstate[...]
            chained = []
            for p in parts:
                nxt = [None] * NC
                ddec = [None] * NC
                for c in reversed(range(NC)):
                    nxt[c] = dst
                    ddec[c] = _colsum(dst * p["sts"][c])
                    dst = dst * p["dec"][c] + p["gt"][:, c * HGRN_DIM:(c + 1) * HGRN_DIM]
                chained.append((nxt, ddec))
            dstate[...] = dst
            dlb = dlb_ref[...]
            for p, (nxt, ddec) in zip(parts, chained):
                dlb = dlb + finish(p, nxt, ddec)
            dlb_ref[...] = dlb
            return carry

        lax.fori_loop(0, nsc // HGRN_SIDE, step, 0)

    col = lambda off: pl.BlockSpec((T, HGRN_DIM), lambda h: (0, off + h))
    own = pl.BlockSpec((T, HGRN_DIM), lambda h: (0, h))
    vec = pl.BlockSpec((1, HGRN_DIM), lambda h: (0, h))
    return pl.pallas_call(
        body, name="hgrn_bwd", grid=(HGRN_HEADS,),
        in_specs=[col(0), col(4), col(8), vec,
                  pl.BlockSpec((1, nsc, HGRN_DIM, HGRN_DIM), lambda h: (h, 0, 0, 0)), own],
        out_specs=[own, own, own, vec],
        out_shape=[jax.ShapeDtypeStruct((T, HGRN_W), BF16)] * 3 + [jax.ShapeDtypeStruct((1, HGRN_W), F32)],
        scratch_shapes=[pltpu.VMEM((HGRN_DIM, HGRN_DIM), F32), pltpu.VMEM((HGRN_SIDE, SUPER, HGRN_DIM), F32)],
        compiler_params=_cp("arbitrary"),
    )(hg, hg, hg, lb, states, do)


def _rec_heads(rec, gate, g_h):
    rr = jnp.concatenate(
        [jnp.broadcast_to(_rms(rec[:, h * HGRN_DIM:(h + 1) * HGRN_DIM], HGRN_DIM), (rec.shape[0], HGRN_DIM))
         for h in range(HGRN_HEADS)], axis=1)
    rn = rec * rr
    sg = _sigmoid(gate)
    return rr, rn, sg


def _mix_out(attn_o, rec_o, hg, x, g_a, g_h, w_out, tm=512):
    T = x.shape[0]

    def body(a_ref, r_ref, gt_ref, x_ref, ga_ref, gh_ref, w_ref, h1_ref, mixed_ref):
        a = a_ref[...]
        an = a * _rms(a, ATTN_W) * ga_ref[...]
        gate = gt_ref[...]
        _, rn, sg = _rec_heads(r_ref[...], gate, gh_ref[...])
        mixed = jnp.concatenate([an, rn * gh_ref[...] * (gate * sg)], axis=1).astype(BF16)
        mixed_ref[...] = mixed
        h1_ref[...] = x_ref[...] + _dot(mixed, w_ref[...])

    row = lambda w: pl.BlockSpec((tm, w), lambda i: (i, 0))
    return pl.pallas_call(
        body, name="mix_out", grid=(T // tm,),
        in_specs=[row(ATTN_W), row(HGRN_W), pl.BlockSpec((tm, HGRN_W), lambda i: (i, 3)), row(D_MODEL),
                  _full((1, ATTN_W)), _full((1, HGRN_W)), _once((D_MODEL, D_MODEL))],
        out_specs=[row(D_MODEL), row(D_MODEL)],
        out_shape=[jax.ShapeDtypeStruct((T, D_MODEL), F32), jax.ShapeDtypeStruct((T, D_MODEL), BF16)],
        compiler_params=_cp("arbitrary"),
    )(attn_o, rec_o, hg, x, g_a, g_h, w_out)


_INV_SQRT2 = 1.0 / math.sqrt(2.0)
_INV_SQRT2PI = 1.0 / math.sqrt(2.0 * math.pi)


def _gelu(x):
    return 0.5 * x * (1.0 + lax.erf(x * _INV_SQRT2))


def _gelu_and_grad(x):
    z = x * _INV_SQRT2
    cdf = 0.5 * (1.0 + lax.erf(z))
    return x * cdf, cdf + (x * _INV_SQRT2PI) * jnp.exp(-(z * z))


def _shift_down(g, prev, rowid):
    p1 = _row(prev, prev.shape[0] - 1)
    p2 = _row(prev, prev.shape[0] - 2)
    s1 = jnp.where(rowid == 0, p1, pltpu.roll(g, 1, 0))
    s2 = jnp.where(rowid == 0, p2, jnp.where(rowid == 1, p1, pltpu.roll(g, 2, 0)))
    return s1, s2


def _mlp_fwd(h1, g2, w_up4, conv_w, conv_b, w_down, gf, tgt, tm=256):
    T = h1.shape[0]

    def body(h_ref, g2_ref, wu_hbm, cw_ref, cb_ref, wd_ref, gf_ref, t_ref,
             u_ref, gate_ref, val_ref, conv_ref, act_ref, dh_ref, loss_ref, dgf_ref, carry, wu_ref, sem):
        i = pl.program_id(0)

        @pl.when(i == 0)
        def _():
            carry[...] = jnp.zeros_like(carry)
            loss_ref[...] = jnp.zeros_like(loss_ref)
            dgf_ref[...] = jnp.zeros_like(dgf_ref)
            _load_side_by_side(wu_hbm, wu_ref, sem)

        h = h_ref[...]
        u = (h * _rms(h, D_MODEL) * g2_ref[...]).astype(BF16)
        u_ref[...] = u
        y2 = jnp.zeros((tm, D_MODEL), F32)
        for lo, hi in FF_CHUNKS:
            cols = slice(lo, hi)
            rowid = lax.broadcasted_iota(jnp.int32, (tm, hi - lo), 0)
            gb = _dot(u, wu_ref[:, lo:hi]).astype(BF16)
            vb = _dot(u, wu_ref[:, D_FF + lo:D_FF + hi]).astype(BF16)
            gate_ref[:, cols] = gb
            val_ref[:, cols] = vb
            g = gb.astype(F32)
            s1, s2 = _shift_down(g, carry[:, cols], rowid)
            carry[:, cols] = g[tm - 8:, :]
            conv = cb_ref[:, cols] + cw_ref[0:1, cols] * s2 + cw_ref[1:2, cols] * s1 + cw_ref[2:3, cols] * g
            act = (_gelu(conv) * vb.astype(F32)).astype(BF16)
            conv_ref[:, cols] = conv.astype(BF16)
            act_ref[:, cols] = act
            y2 = y2 + _dot(act, wd_ref[cols, :])
        h2 = h + y2
        rf = _rms(h2, D_MODEL)
        n = h2 * rf
        gfv = gf_ref[...]
        e = n * gfv - t_ref[...]
        loss_ref[...] += jnp.sum(e * e) * (0.5 / D_MODEL)
        dy = e * (1.0 / D_MODEL)
        dgf_ref[...] += _colsum(dy * n)
        dh_ref[...] = _rms_bwd(dy * gfv, n, rf, D_MODEL)

    row = lambda w: pl.BlockSpec((tm, w), lambda i: (i, 0))
    return pl.pallas_call(
        body, name="mlp_fwd", grid=(T // tm,),
        in_specs=[row(D_MODEL), _full((1, D_MODEL)), ANY, _full((3, D_FF)),
                  _full((1, D_FF)), _once((D_FF, D_MODEL)), _full((1, D_MODEL)), row(D_MODEL)],
        out_specs=[row(D_MODEL), row(D_FF), row(D_FF), row(D_FF), row(D_FF), row(D_MODEL), _full((1, 128)),
                   _full((1, D_MODEL))],
        out_shape=[jax.ShapeDtypeStruct((T, D_MODEL), BF16)] + [jax.ShapeDtypeStruct((T, D_FF), BF16)] * 4
        + [jax.ShapeDtypeStruct((T, D_MODEL), F32),
                   jax.ShapeDtypeStruct((1, 128), F32), jax.ShapeDtypeStruct((1, D_MODEL), F32)],
        scratch_shapes=[pltpu.VMEM((8, D_FF), F32), pltpu.VMEM((D_MODEL, 2 * D_FF), BF16),
                        pltpu.SemaphoreType.DMA((N_CHIPS,))],
        compiler_params=_cp("arbitrary"),
    )(h1, g2, w_up4, conv_w, conv_b, w_down, gf, tgt)


def _mlp_bwd(dh2, gate, val, conv, act, conv_w, w_down, tm=256):
    T = dh2.shape[0]
    nb = T // tm

    def body(dh_ref, gate_ref, val_ref, conv_ref, act_ref, cw_ref, wd_ref, dgv_ref, dcw_ref, dcb_ref, dwd_ref,
             carry, acc):
        i = pl.program_id(0)

        @pl.when(i == 0)
        def _():
            carry[...] = jnp.zeros_like(carry)
            dcw_ref[...] = jnp.zeros_like(dcw_ref)
            dcb_ref[...] = jnp.zeros_like(dcb_ref)
            acc[...] = jnp.zeros_like(acc)

        dhb = dh_ref[...].astype(BF16)
        for lo, hi in MLP_BWD_CHUNKS:
            cols = slice(lo, hi)
            rowid = lax.broadcasted_iota(jnp.int32, (tm, hi - lo), 0)
            acc[cols, :] += _dot_tn(act_ref[:, cols], dhb)
            g = gate_ref[:, cols].astype(F32)
            v = val_ref[:, cols].astype(F32)
            cv = conv_ref[:, cols].astype(F32)
            dact = _dot_nt(dhb, wd_ref[cols, :])
            gl, gp = _gelu_and_grad(cv)
            dconv = dact * v * gp
            nxt = carry[:, cols]
            n0, n1 = _row(nxt, 0), _row(nxt, 1)
            u1 = jnp.where(rowid == tm - 1, n0, pltpu.roll(dconv, tm - 1, 0))
            u2 = jnp.where(rowid == tm - 1, n1, jnp.where(rowid == tm - 2, n0, pltpu.roll(dconv, tm - 2, 0)))
            carry[:, cols] = dconv[0:8, :]
            dcb_ref[:, cols] += _colsum(dconv)
            dcw_ref[0:1, cols] += _colsum(u2 * g)
            dcw_ref[1:2, cols] += _colsum(u1 * g)
            dcw_ref[2:3, cols] += _colsum(dconv * g)
            dgate = cw_ref[2:3, cols] * dconv + cw_ref[1:2, cols] * u1 + cw_ref[0:1, cols] * u2
            dgv_ref[:, cols] = dgate.astype(BF16)
            dgv_ref[:, D_FF + lo:D_FF + hi] = (dact * gl).astype(BF16)

        @pl.when(i == nb - 1)
        def _():
            for lo, hi in MLP_BWD_CHUNKS:
                dwd_ref[lo:hi, :] = acc[lo:hi, :].astype(BF16)

    rev = lambda w: pl.BlockSpec((tm, w), lambda i: (nb - 1 - i, 0))
    return pl.pallas_call(
        body, name="mlp_bwd", grid=(nb,),
        in_specs=[rev(D_MODEL), rev(D_FF), rev(D_FF), rev(D_FF), rev(D_FF), _full((3, D_FF)), _once((D_FF, D_MODEL))],
        out_specs=[rev(2 * D_FF), _full((3, D_FF)), _full((1, D_FF)), _once((D_FF, D_MODEL))],
        out_shape=[jax.ShapeDtypeStruct((T, 2 * D_FF), BF16), jax.ShapeDtypeStruct((3, D_FF), F32),
                   jax.ShapeDtypeStruct((1, D_FF), F32), jax.ShapeDtypeStruct((D_FF, D_MODEL), BF16)],
        scratch_shapes=[pltpu.VMEM((8, D_FF), F32), pltpu.VMEM((D_FF, D_MODEL), F32)],
        compiler_params=_cp("arbitrary"),
    )(dh2, gate, val, conv, act, conv_w, w_down)


def _up_out_bwd(dgv, w_up4, h1, g2, dh2, w_out, attn_o, rec_o, hg, g_a, g_h, tm=256):
    T = h1.shape[0]

    def body(dgv_ref, wu_hbm, h_ref, g2_ref, dh2_ref, wo_ref, a_ref, r_ref, gt_ref, ga_ref, gh_ref,
             dh1_ref, dg2_ref, da_ref, dr_ref, dgt_ref, dga_ref, dgh_ref, wu_ref, sem):
        @pl.when(pl.program_id(0) == 0)
        def _():
            dg2_ref[...] = jnp.zeros_like(dg2_ref)
            dga_ref[...] = jnp.zeros_like(dga_ref)
            dgh_ref[...] = jnp.zeros_like(dgh_ref)
            _load_side_by_side(wu_hbm, wu_ref, sem)

        du = _dot_nt(dgv_ref[...], wu_ref[...])
        h = h_ref[...]
        r = _rms(h, D_MODEL)
        n = h * r
        dg2_ref[...] += _colsum(du * n)
        dh1 = dh2_ref[...] + _rms_bwd(du * g2_ref[...], n, r, D_MODEL)
        dh1_ref[...] = dh1
        dmix = _dot_nt(dh1.astype(BF16), wo_ref[...])
        dan = dmix[:, :ATTN_W]
        a = a_ref[...]
        ra = _rms(a, ATTN_W)
        na = a * ra
        dga_ref[...] += _colsum(dan * na)
        da_ref[...] = _rms_bwd(dan * ga_ref[...], na, ra, ATTN_W)
        dmr = dmix[:, ATTN_W:]
        gate = gt_ref[...]
        ghv = gh_ref[...]
        rr, rn, sg = _rec_heads(r_ref[...], gate, ghv)
        dgt_ref[...] = (dmr * rn * ghv * (sg * (1.0 + gate * (1.0 - sg)))).astype(BF16)
        drecn = dmr * (gate * sg)
        dgh_ref[...] += _colsum(drecn * rn)
        drn = drecn * ghv
        prod = drn * rn
        mean = jnp.concatenate(
            [jnp.broadcast_to(jnp.sum(prod[:, h_ * HGRN_DIM:(h_ + 1) * HGRN_DIM], axis=-1, keepdims=True),
                              (tm, HGRN_DIM)) for h_ in range(HGRN_HEADS)], axis=1) * (1.0 / HGRN_DIM)
        dr_ref[...] = rr * (drn - rn * mean)

    row = lambda w: pl.BlockSpec((tm, w), lambda i: (i, 0))
    return pl.pallas_call(
        body, name="up_out_bwd", grid=(T // tm,),
        in_specs=[row(2 * D_FF), ANY, row(D_MODEL), _full((1, D_MODEL)),
                  row(D_MODEL), _once((D_MODEL, D_MODEL)), row(ATTN_W), row(HGRN_W),
                  pl.BlockSpec((tm, HGRN_W), lambda i: (i, 3)), _full((1, ATTN_W)), _full((1, HGRN_W))],
        out_specs=[row(D_MODEL), _full((1, D_MODEL)), row(ATTN_W), row(HGRN_W), row(HGRN_W),
                   _full((1, ATTN_W)), _full((1, HGRN_W))],
        out_shape=[jax.ShapeDtypeStruct((T, D_MODEL), F32), jax.ShapeDtypeStruct((1, D_MODEL), F32),
                   jax.ShapeDtypeStruct((T, ATTN_W), F32), jax.ShapeDtypeStruct((T, HGRN_W), F32),
                   jax.ShapeDtypeStruct((T, HGRN_W), BF16), jax.ShapeDtypeStruct((1, ATTN_W), F32),
                   jax.ShapeDtypeStruct((1, HGRN_W), F32)],
        scratch_shapes=[pltpu.VMEM((D_MODEL, 2 * D_FF), BF16), pltpu.SemaphoreType.DMA((N_CHIPS,))],
        compiler_params=_cp("arbitrary"),
    )(dgv, w_up4, h1, g2, dh2, w_out, attn_o, rec_o, hg, g_a, g_h)


def _in_bwd(dqkv, dhg, w_in4, x, g1, dh1, tm=512):
    T = x.shape[0]

    def body(*refs):
        parts = refs[:7]
        w_hbm, x_ref, g_ref, dh1_ref, dp_ref, dx_ref, dg_ref, w_full, sem = refs[7:]

        @pl.when(pl.program_id(0) == 0)
        def _():
            dg_ref[...] = jnp.zeros_like(dg_ref)
            _load_side_by_side(w_hbm, w_full, sem)

        dp = jnp.concatenate([p[...] for p in parts], axis=1)
        dp_ref[...] = dp
        du = _dot_nt(dp, w_full[...])
        xv = x_ref[...]
        r = _rms(xv, D_MODEL)
        n = xv * r
        dg_ref[...] += _colsum(du * n)
        dx_ref[...] = dh1_ref[...] + _rms_bwd(du * g_ref[...], n, r, D_MODEL)

    row = lambda w: pl.BlockSpec((tm, w), lambda i: (i, 0))
    return pl.pallas_call(
        body, name="in_bwd", grid=(T // tm,),
        in_specs=[row(ATTN_W)] * 7 + [ANY, row(D_MODEL), _full((1, D_MODEL)), row(D_MODEL)],
        out_specs=[row(IN_TOTAL), row(D_MODEL), _full((1, D_MODEL))],
        out_shape=[jax.ShapeDtypeStruct((T, IN_TOTAL), BF16), jax.ShapeDtypeStruct((T, D_MODEL), F32),
                   jax.ShapeDtypeStruct((1, D_MODEL), F32)],
        scratch_shapes=[pltpu.VMEM((D_MODEL, IN_TOTAL), BF16), pltpu.SemaphoreType.DMA((N_CHIPS,))],
        compiler_params=_cp("arbitrary"),
    )(*dqkv, *dhg, w_in4, x, g1, dh1)


def _dw(a, b, kb, nb_, name, tk=1024, side=1):
    T, K = a.shape
    N = b.shape[1]
    nk, nn, nt = K // kb, N // (nb_ * side), T // tk

    def body(a_ref, b_ref, o_ref, acc):
        t = pl.program_id(2)

        @pl.when(t == 0)
        def _():
            acc[...] = jnp.zeros_like(acc)

        acc[...] += _dot_tn(a_ref[...], b_ref[...].astype(BF16))

        @pl.when(t == nt - 1)
        def _():
            for s in range(side):
                o_ref[s] = acc[:, s * nb_:(s + 1) * nb_].astype(BF16)

    return pl.pallas_call(
        body, name=name, grid=(nk, nn, nt),
        in_specs=[pl.BlockSpec((tk, kb), lambda i, j, t: (t, i)),
                  pl.BlockSpec((tk, nb_ * side), lambda i, j, t: (t, j))],
        out_specs=pl.BlockSpec((side, kb, nb_), lambda i, j, t: (i * nn + j, 0, 0)),
        out_shape=jax.ShapeDtypeStruct((nk * nn * side, kb, nb_), BF16),
        scratch_shapes=[pltpu.VMEM((kb, nb_ * side), F32)],
        compiler_params=_cp("arbitrary", "arbitrary", "arbitrary"),
    )(a, b)


def _step_channel(a, x, tgt, g_a, g_h, w_out, g2, w_up4, conv_w, conv_b, w_down, gf):
    h1, mixed = _mix_out(a["attn_o"], a["rec_o"], a["hg"], x, g_a, g_h, w_out)
    u2, gate, val, conv, act, dh2, loss, dgf = _mlp_fwd(h1, g2, w_up4, conv_w, conv_b, w_down, gf, tgt)
    dgv, dcw, dcb, dw_down = _mlp_bwd(dh2, gate, val, conv, act, conv_w, w_down)
    dw_down = dw_down.reshape(N_CHIPS, D_FF // N_CHIPS, D_MODEL)
    dh1, dg2, da, dr, dgt, dga, dgh = _up_out_bwd(dgv, w_up4, h1, g2, dh2, w_out, a["attn_o"], a["rec_o"], a["hg"],
                                                  g_a, g_h)
    dw_up = _dw(u2, dgv, D_MODEL, UP_SHARD, "dw_up", side=2)
    dw_out = _dw(mixed, dh1, D_MODEL, D_MODEL, "dw_out").reshape(N_CHIPS, D_MODEL // N_CHIPS, D_MODEL)
    return dict(loss=loss, dgf=dgf, dcw=dcw, dcb=dcb, dg2=dg2, dga=dga, dgh=dgh, dh1=dh1, da=da, dr=dr, dgt=dgt,
                dw_down=dw_down, dw_up=dw_up, dw_out=dw_out)


def _step_mixers_bwd(a, b, x, g1, w_in4, lb, dqkv):
    dhq, dhf, dhi, dlb = _hgrn_bwd(a["hg"], lb, a["states"], b["dr"])
    dproj, dx, dg1 = _in_bwd(dqkv, [dhq, dhf, dhi, b["dgt"]], w_in4, x, g1, b["dh1"])
    dw_in = _dw(a["u1"], dproj, D_MODEL, IN_SHARD, "dw_in", side=2)
    return dict(dx=dx, dg1=dg1, dlb=dlb, dw_in=dw_in)


BIG = ("w_in", "w_out", "w_up", "w_down")
ANY = pl.BlockSpec(memory_space=pl.ANY)


def _place():
    x, y, c = lax.axis_index("x"), lax.axis_index("y"), lax.axis_index("c")
    chips = [(1 - x, y), (x, 1 - y), (1 - x, 1 - y)]
    return x, y, c, chips


def _remote(src, dst, send_sems, recv_sems, k, to):
    return pltpu.make_async_remote_copy(src_ref=src, dst_ref=dst, send_sem=send_sems.at[k], recv_sem=recv_sems.at[k],
                                        device_id=to, device_id_type=MESH)


def _gather_weights(shards, conv_w):
    n = len(shards)
    halves = [s.shape[0] // 2 for s in shards]

    def body(*refs):
        ins, cw, outs, ocw = refs[:n], refs[n], refs[n + 1:2 * n + 1], refs[2 * n + 1]
        send_sems, recv_sems = refs[2 * n + 2:]
        x, y, c, chips = _place()
        me, sibling = 2 * x + y, (x, y, 1 - c)

        def part(w, chip, half):
            return outs[w].at[chip, pl.ds(half * halves[w], halves[w]), :]

        sent = []
        for j, chip in enumerate(chips):
            for w in range(n):
                sent.append(_remote(ins[w].at[pl.ds(c * halves[w], halves[w]), :], part(w, me, c),
                                    send_sems, recv_sems, w * 3 + j, (*chip, c)))
            sent.append(_remote(cw, ocw.at[me], send_sems, recv_sems, 6 * n + j, (*chip, c)))
        for cp in sent:
            cp.start()
        for j, chip in enumerate(chips):
            kj = 2 * chip[0] + chip[1]
            for w in range(n):
                _remote(part(w, kj, c), part(w, kj, c), send_sems, recv_sems, w * 3 + j, (*chip, c)).wait_recv()
                fwd = _remote(part(w, kj, c), part(w, kj, c), send_sems, recv_sems, 3 * n + w * 3 + j, sibling)
                fwd.start()
                sent.append(fwd)
        for j, chip in enumerate(chips):
            kj = 2 * chip[0] + chip[1]
            for w in range(n):
                _remote(part(w, kj, 1 - c), part(w, kj, 1 - c), send_sems, recv_sems, 3 * n + w * 3 + j,
                        sibling).wait_recv()
            _remote(cw, ocw.at[kj], send_sems, recv_sems, 6 * n + j, (*chip, c)).wait_recv()
        for cp in sent:
            cp.wait_send()

    n_sem = 6 * n + 3
    outs = pl.pallas_call(
        body, name="gather_weights",
        in_specs=[ANY] * (n + 1), out_specs=[ANY] * (n + 1),
        out_shape=[jax.ShapeDtypeStruct((N_CHIPS,) + s.shape, s.dtype) for s in shards]
        + [jax.ShapeDtypeStruct((N_CHIPS,) + conv_w.shape, conv_w.dtype)],
        scratch_shapes=[pltpu.SemaphoreType.DMA((n_sem,)), pltpu.SemaphoreType.DMA((n_sem,))],
    )(*shards, conv_w)
    chip = 2 * lax.axis_index("x") + lax.axis_index("y")
    return [lax.dynamic_update_slice(o, s[None], (chip,) + (0,) * s.ndim) for o, s in zip(outs, [*shards, conv_w])]


def _allreduce_small(buf):
    rows = buf.shape[0]

    def body(in_ref, out_ref, slots, send_sems, recv_sems):
        x, y, c, _ = _place()
        me = 4 * x + 2 * y + c
        slots[me] = in_ref[...]
        sent = []
        for p in range(1, 8):
            to = (x ^ (p >> 2), y ^ ((p >> 1) & 1), c ^ (p & 1))
            sent.append(_remote(in_ref, slots.at[me], send_sems, recv_sems, p, to))
        for cp in sent:
            cp.start()
        for p in range(1, 8):
            frm = 4 * (x ^ (p >> 2)) + 2 * (y ^ ((p >> 1) & 1)) + (c ^ (p & 1))
            _remote(in_ref, slots.at[frm], send_sems, recv_sems, p, (x, y, c)).wait_recv()
        for cp in sent:
            cp.wait_send()
        acc = slots[0]
        for d in range(1, 8):
            acc = acc + slots[d]
        out_ref[...] = acc

    vm = pl.BlockSpec(memory_space=pltpu.VMEM)
    return pl.pallas_call(
        body, name="allreduce_small", in_specs=[vm], out_specs=vm,
        out_shape=jax.ShapeDtypeStruct(buf.shape, F32),
        scratch_shapes=[pltpu.VMEM((8, rows, 128), F32), pltpu.SemaphoreType.DMA((8,)), pltpu.SemaphoreType.DMA((8,))],
    )(buf)


def _sibling_peer():
    x, y, c, _ = _place()
    return [(x, y, 1 - c)]


def _chip_peers():
    x, y, c, chips = _place()
    return [(*chip, c) for chip in chips]


def _handshake(peers):
    barrier = pltpu.get_barrier_semaphore()
    for peer in peers:
        pl.semaphore_signal(barrier, inc=1, device_id=peer, device_id_type=MESH)
    pl.semaphore_wait(barrier, len(peers))


def _pair_exchange(gs, name, barrier_id):
    n = len(gs)
    halves = [g.shape[1] // 2 for g in gs]

    def body(*refs):
        g, got = refs[:n], refs[n:2 * n]
        send_sems, recv_sems = refs[2 * n:]
        _handshake(_sibling_peer())
        x, y, c, _ = _place()
        cps = [_remote(g[w].at[:, pl.ds((1 - c) * halves[w], halves[w]), :], got[w], send_sems, recv_sems, w,
                       (x, y, 1 - c)) for w in range(n)]
        for cp in cps:
            cp.start()
        for cp in cps:
            cp.wait()

    return pl.pallas_call(
        body, name=name, in_specs=[ANY] * n, out_specs=[ANY] * n,
        out_shape=[jax.ShapeDtypeStruct((N_CHIPS, h, g.shape[2]), g.dtype) for g, h in zip(gs, halves)],
        scratch_shapes=[pltpu.SemaphoreType.DMA((n,)), pltpu.SemaphoreType.DMA((n,))],
        compiler_params=pltpu.CompilerParams(collective_id=barrier_id),
    )(*gs)


def _core_id():
    return lax.axis_index("c").reshape(1).astype(jnp.int32)


def _pair_sum(g, got, name):
    h, C = got.shape[1:]

    def body(c_ref, g_ref, b_ref, o_ref):
        o_ref[...] = (g_ref[...].astype(F32) + b_ref[...].astype(F32)).astype(BF16)

    blk = pl.BlockSpec((1, h, C), lambda k, c_ref: (k, 0, 0))
    return pl.pallas_call(
        body, name=name,
        grid_spec=pltpu.PrefetchScalarGridSpec(
            num_scalar_prefetch=1, grid=(N_CHIPS,),
            in_specs=[pl.BlockSpec((1, h, C), lambda k, c_ref: (k, c_ref[0], 0)), blk], out_specs=blk),
        out_shape=jax.ShapeDtypeStruct(got.shape, BF16), compiler_params=_cp("arbitrary"))(_core_id(), g, got)


def _sum_partials(g, got, landed, name):
    h, C = got.shape[1:]

    def body(ids, g_ref, b_ref, l_ref, o_ref):
        acc = g_ref[0].astype(F32) + b_ref[0].astype(F32)
        for j in range(3):
            acc = acc + l_ref[j].astype(F32)
        o_ref[...] = acc

    ids = jnp.stack([2 * lax.axis_index("x") + lax.axis_index("y"), lax.axis_index("c")]).astype(jnp.int32)
    return pl.pallas_call(
        body, name=name,
        grid_spec=pltpu.PrefetchScalarGridSpec(
            num_scalar_prefetch=1, grid=(1,),
            in_specs=[pl.BlockSpec((1, h, C), lambda i, ids: (ids[0], ids[1], 0)),
                      pl.BlockSpec((1, h, C), lambda i, ids: (ids[0], 0, 0)),
                      pl.BlockSpec((3, h, C), lambda i, ids: (0, 0, 0))],
            out_specs=pl.BlockSpec((h, C), lambda i, ids: (ids[1], 0))),
        out_shape=jax.ShapeDtypeStruct((2 * h, C), F32), compiler_params=_cp("arbitrary"))(ids, g, got, landed)


def _pair_share(reds, name, barrier_id):
    n = len(reds)

    def body(*refs):
        out = refs[n:2 * n]
        send_sems, recv_sems = refs[2 * n:]
        _handshake(_sibling_peer())
        x, y, c, _ = _place()
        def half(w, which):
            h = out[w].shape[0] // 2
            return out[w].at[pl.ds(which * h, h), :]

        cps = [_remote(half(w, c), half(w, c), send_sems, recv_sems, w, (x, y, 1 - c)) for w in range(n)]
        for cp in cps:
            cp.start()
        for w in range(n):
            _remote(half(w, 1 - c), half(w, 1 - c), send_sems, recv_sems, w, (x, y, 1 - c)).wait_recv()
        for cp in cps:
            cp.wait_send()

    return pl.pallas_call(
        body, name=name, in_specs=[ANY] * n, out_specs=[ANY] * n,
        out_shape=[jax.ShapeDtypeStruct(r.shape, F32) for r in reds],
        input_output_aliases={w: w for w in range(n)},
        scratch_shapes=[pltpu.SemaphoreType.DMA((n,)), pltpu.SemaphoreType.DMA((n,))],
        compiler_params=pltpu.CompilerParams(collective_id=barrier_id),
    )(*reds)


HBM = pl.BlockSpec(memory_space=pltpu.HBM)
SEM = pl.BlockSpec(memory_space=pltpu.SEMAPHORE)
DATAFLOW = pltpu.SideEffectType.DATAFLOW_SIDE_EFFECTING


def _copies_start(name, srcs, lands, plan, n_copies, after, peers, barrier_id):
    ns, nb, na = len(srcs), len(srcs) + len(lands), len(after)

    def body(*refs):
        src_refs, land_refs = refs[:ns], refs[ns:nb]
        send_sems, recv_sems = refs[nb + na:nb + na + 2]
        token = refs[-1]
        _handshake(peers())
        for k, (src, there, _, to) in enumerate(plan(src_refs, land_refs)):
            _remote(src, there, send_sems, recv_sems, k, to).start()
        token[...] = jnp.zeros_like(token)

    hbm = lambda a: pltpu.HBM(a.shape, a.dtype)
    outs = pl.pallas_call(
        body, name=name,
        out_shape=(pltpu.SemaphoreType.DMA((n_copies,)), pltpu.SemaphoreType.DMA((n_copies,)),
                   *[hbm(a) for a in srcs], *[hbm(a) for a in lands], jax.ShapeDtypeStruct((8, 128), F32)),
        in_specs=[HBM] * nb + [ANY] * na,
        out_specs=(SEM, SEM, *[HBM] * nb, pl.BlockSpec(memory_space=pltpu.VMEM)),
        input_output_aliases={i: 2 + i for i in range(nb)},
        compiler_params=pltpu.CompilerParams(has_side_effects=DATAFLOW, collective_id=barrier_id),
    )(*[pltpu.with_memory_space_constraint(a, pltpu.HBM) for a in (*srcs, *lands)], *after)
    return outs[0], outs[1], outs[2:2 + ns], outs[2 + ns:2 + nb], outs[-1]


def _copies_wait(name, send_sems, recv_sems, srcs, lands, plan, after):
    ns, nb, na = len(srcs), len(srcs) + len(lands), len(after)

    def body(*refs):
        src_refs, land_refs = refs[:ns], refs[ns:nb]
        send_sems, recv_sems = refs[nb:nb + 2]
        for k, (src, _, here, to) in enumerate(plan(src_refs, land_refs)):
            cp = _remote(src, here, send_sems, recv_sems, k, to)
            cp.wait_send()
            cp.wait_recv()

    hbm = lambda a: pltpu.HBM(a.shape, a.dtype)
    outs = pl.pallas_call(
        body, name=name,
        out_shape=(*[hbm(a) for a in srcs], *[hbm(a) for a in lands]),
        in_specs=[HBM] * nb + [SEM, SEM] + [ANY] * na,
        out_specs=tuple([HBM] * nb),
        input_output_aliases={i: i for i in range(nb)},
        compiler_params=pltpu.CompilerParams(has_side_effects=DATAFLOW),
    )(*srcs, *lands, send_sems, recv_sems, *after)
    return outs[:ns], outs[ns:]


def _gather_plan(halves):
    def plan(shards, lands):
        x, y, c, chips = _place()
        me = 2 * x + y
        copies = []
        for w, h in enumerate(halves):
            rows = pl.ds(c * h, h)
            for chip in chips:
                copies.append((shards[w].at[rows, :], lands[w].at[me, rows, :],
                               lands[w].at[2 * chip[0] + chip[1], rows, :], (*chip, c)))
        return copies
    return plan


def _reduce_plan(n):
    def plan(ps, lands):
        x, y, c, chips = _place()
        return [(ps[w].at[2 * chip[0] + chip[1]], lands[w].at[j], lands[w].at[j], (*chip, c))
                for w in range(n) for j, chip in enumerate(chips)]
    return plan


def _forward_plan(halves):
    def plan(_, lands):
        x, y, c, chips = _place()

        def part(w, chip, half):
            return lands[w].at[2 * chip[0] + chip[1], pl.ds(half * halves[w], halves[w]), :]

        return [(part(w, chip, c), part(w, chip, c), part(w, chip, 1 - c), (x, y, 1 - c))
                for w in range(len(halves)) for chip in chips]
    return plan


def _pair_plan(halves):
    def plan(gs, gots):
        x, y, c, _ = _place()
        return [(gs[w].at[:, pl.ds((1 - c) * h, h), :], gots[w], gots[w], (x, y, 1 - c)) for w, h in enumerate(halves)]
    return plan


def _place_own(gathered, shards):
    chip = 2 * lax.axis_index("x") + lax.axis_index("y")
    return [lax.dynamic_update_slice(o, s[None], (chip, 0, 0)) for o, s in zip(gathered, shards)]


def _adamw(w, g, m, v, name, tr=None):
    R, C = w.shape
    tr = tr or R // 4

    def body(w_ref, g_ref, m_ref, v_ref, d_ref, nm_ref, nv_ref):
        d_ref[...], nm_ref[...], nv_ref[...] = _adamw_math(w_ref[...], g_ref[...], m_ref[...], v_ref[...])

    blk = pl.BlockSpec((tr, C), lambda i: (i, 0))
    return pl.pallas_call(body, name=name, grid=(R // tr,), in_specs=[blk] * 4, out_specs=[blk] * 3,
                          out_shape=[jax.ShapeDtypeStruct((R, C), F32)] * 3, compiler_params=_cp("arbitrary"))(w, g, m, v)


SMALL = (("norm1_g", 1, 1024), ("attn_norm_g", 1, 512), ("hgrn_norm_g", 1, 512), ("hgrn_lb_logits", 2, 512),
         ("norm2_g", 1, 1024), ("conv_b", 1, D_FF), ("final_norm_g", 1, 1024), ("conv_w", 3, D_FF))
LOSS_ROW = sum(r * c for _, r, c in SMALL) // 128
SMALL_ROWS = 136


def _rows_to_lanes(ref, row, width):
    return jnp.concatenate([ref[row + j:row + j + 1, :] for j in range(width // 128)], axis=1)


def _pack_small(grads, dlb, lb, loss):
    def body(*refs):
        parts, dlb_ref, lb_ref, loss_ref, out = refs[:len(SMALL) - 1], refs[-4], refs[-3], refs[-2], refs[-1]
        out[...] = jnp.zeros_like(out)
        lbv = lb_ref[...]
        dl = dlb_ref[...] * lbv * (1.0 - lbv)
        row = 0
        parts = list(parts)
        for name, rows, width in SMALL:
            for r in range(rows):
                if name == "hgrn_lb_logits":
                    src = dl if r == 0 else -dl
                    for j in range(width // 128):
                        out[row + j:row + j + 1, :] = src[:, 128 * j:128 * (j + 1)]
                else:
                    for j in range(width // 128):
                        out[row + j:row + j + 1, :] = parts[0][r:r + 1, 128 * j:128 * (j + 1)]
                row += width // 128
            if name != "hgrn_lb_logits":
                parts.pop(0)
        out[LOSS_ROW:LOSS_ROW + 1, :] = loss_ref[...]

    vm = pl.BlockSpec(memory_space=pltpu.VMEM)
    return pl.pallas_call(body, name="pack_small", in_specs=[vm] * (len(grads) + 3), out_specs=vm,
                          out_shape=jax.ShapeDtypeStruct((SMALL_ROWS, 128), F32))(*grads, dlb, lb, loss)


def _adamw_math(w, g, m, v):
    nm = ADAM_B1 * m + (1.0 - ADAM_B1) * g
    nv = ADAM_B2 * v + (1.0 - ADAM_B2) * (g * g)
    m_hat = nm / (1.0 - ADAM_B1 ** ADAM_STEP)
    v_hat = nv / (1.0 - ADAM_B2 ** ADAM_STEP)
    return -ADAM_LR * (m_hat / (jnp.sqrt(v_hat) + ADAM_EPS) + ADAM_WD * w), nm, nv


def _small_update(summed, g_conv_w, ws, ms, vs):
    n = len(SMALL)

    def body(*refs):
        s_ref, gcw_ref = refs[:2]
        w_refs, m_refs, v_refs = refs[2:2 + n], refs[2 + n:2 + 2 * n], refs[2 + 2 * n:2 + 3 * n]
        outs = refs[2 + 3 * n:]
        row = 0
        for k, (name, rows, width) in enumerate(SMALL):
            if name == "conv_w":
                g = gcw_ref[...]
            else:
                g = jnp.concatenate([_rows_to_lanes(s_ref, row + r * (width // 128), width) for r in range(rows)], axis=0)
            row += rows * (width // 128)
            d, nm, nv = _adamw_math(w_refs[k][...], g, m_refs[k][...], v_refs[k][...])
            for o, val in zip(outs[4 * k:4 * k + 4], (g, d, nm, nv)):
                o[...] = val

    vm = pl.BlockSpec(memory_space=pltpu.VMEM)
    outs = pl.pallas_call(
        body, name="small_update", in_specs=[vm] * (2 + 3 * n), out_specs=[vm] * (4 * n),
        out_shape=[jax.ShapeDtypeStruct(a.shape, F32) for a in ws for _ in range(4)],
    )(summed, g_conv_w, *ws, *ms, *vs)
    return [outs[4 * k:4 * k + 4] for k in range(n)]


def kernel(x, norm1_g, w_in, attn_norm_g, hgrn_norm_g, hgrn_lb_logits, w_out, norm2_g, w_up, conv_w, conv_b, w_down, final_norm_g, loss_target, m_norm1_g, m_w_in, m_attn_norm_g, m_hgrn_norm_g, m_hgrn_lb_logits, m_w_out, m_norm2_g, m_w_up, m_conv_w, m_conv_b, m_w_down, m_final_norm_g, v_norm1_g, v_w_in, v_attn_norm_g, v_hgrn_norm_g, v_hgrn_lb_logits, v_w_out, v_norm2_g, v_w_up, v_conv_w, v_conv_b, v_w_down, v_final_norm_g):
    w = dict(norm1_g=norm1_g, w_in=w_in, attn_norm_g=attn_norm_g, hgrn_norm_g=hgrn_norm_g,
             hgrn_lb_logits=hgrn_lb_logits, w_out=w_out, norm2_g=norm2_g, w_up=w_up, conv_w=conv_w, conv_b=conv_b,
             w_down=w_down, final_norm_g=final_norm_g)
    m = dict(norm1_g=m_norm1_g, w_in=m_w_in, attn_norm_g=m_attn_norm_g, hgrn_norm_g=m_hgrn_norm_g,
             hgrn_lb_logits=m_hgrn_lb_logits, w_out=m_w_out, norm2_g=m_norm2_g, w_up=m_w_up, conv_w=m_conv_w,
             conv_b=m_conv_b, w_down=m_w_down, final_norm_g=m_final_norm_g)
    v = dict(norm1_g=v_norm1_g, w_in=v_w_in, attn_norm_g=v_attn_norm_g, hgrn_norm_g=v_hgrn_norm_g,
             hgrn_lb_logits=v_hgrn_lb_logits, w_out=v_w_out, norm2_g=v_norm2_g, w_up=v_w_up, conv_w=v_conv_w,
             conv_b=v_conv_b, w_down=v_w_down, final_norm_g=v_final_norm_g)
    names = list(w)
    chip = 2 * lax.axis_index("x") + lax.axis_index("y")

    shards = {k: w[k][0].astype(BF16) for k in BIG}
    w_in4, conv_w4 = _gather_weights([shards["w_in"]], conv_w[0])
    conv_w_full = jnp.transpose(conv_w4, (1, 0, 2)).reshape(3, D_FF)
    lb = jax.nn.softmax(hgrn_lb_logits, axis=0)[0:1]
    late = [shards[k] for k in BIG[1:]]
    gather_plan = _gather_plan([s.shape[0] // 2 for s in late])
    started = _copies_start("gather_start", late, [lax.empty((N_CHIPS,) + s.shape, BF16) for s in late], gather_plan,
                            3 * len(late), after=(w_in4,), peers=_chip_peers, barrier_id=0)
    u1, qkv, hg = _in_proj(x[0], norm1_g + started[4][0:1, 0:1], w_in4)
    attn_o, lse = _attn_fwd(qkv)
    late, landed_w = _copies_wait("gather_wait", *started[:4], gather_plan, after=(attn_o,))
    forward_plan = _forward_plan([s.shape[0] // 2 for s in late])
    started = _copies_start("forward_start", [], landed_w, forward_plan, 3 * len(late), after=(),
                            peers=_sibling_peer, barrier_id=1)
    rec_o, states = _hgrn_fwd(hg, lb + started[4][0:1, 0:1])
    a = dict(u1=u1, qkv=qkv, hg=hg, attn_o=attn_o, lse=lse, rec_o=rec_o, states=states)
    w_out4, w_up4, w_down4 = _place_own(
        _copies_wait("forward_wait", *started[:4], forward_plan, after=(rec_o,))[1], late)

    b = _step_channel(a, x[0], loss_target[0], attn_norm_g, hgrn_norm_g, w_out4.reshape(D_MODEL, D_MODEL), norm2_g,
                      w_up4, conv_w_full, conv_b, w_down4.reshape(D_FF, D_MODEL), final_norm_g.reshape(1, D_MODEL))

    early = [b["dw_out"], b["dw_up"], b["dw_down"]]
    pair_plan = _pair_plan([gk.shape[1] // 2 for gk in early])
    started = _copies_start("pair_start", early,
                            [lax.empty((N_CHIPS, gk.shape[1] // 2, gk.shape[2]), BF16) for gk in early], pair_plan,
                            len(early), after=(), peers=_sibling_peer, barrier_id=2)
    dqkv = _attn_bwd(qkv, attn_o, lse, b["da"], started[4])
    early, gots = _copies_wait("pair_wait", *started[:4], pair_plan, after=(dqkv[0],))
    ps = [_pair_sum(gk, got, f"pair_sum_{k}") for gk, got, k in zip(early, gots, BIG[1:])]
    reduce_plan = _reduce_plan(len(ps))
    started = _copies_start("reduce_start", ps, [lax.empty((3,) + p.shape[1:], BF16) for p in ps], reduce_plan,
                            3 * len(ps), after=(), peers=_chip_peers, barrier_id=3)
    c = _step_mixers_bwd(a, b, x[0], norm1_g, w_in4, lb + started[4][0:1, 0:1], dqkv)
    gots_in = _pair_exchange([c["dw_in"]], "pair_exchange_w_in", barrier_id=4)
    ps_in = _pair_sum(c["dw_in"], gots_in[0], "pair_sum_w_in")
    plan_in = _reduce_plan(1)
    started_in = _copies_start("reduce_start_w_in", [ps_in], [lax.empty((3,) + ps_in.shape[1:], BF16)], plan_in, 3,
                               after=(), peers=_chip_peers, barrier_id=5)
    landed = _copies_wait("reduce_wait", *started[:4], reduce_plan, after=(started_in[4],))[1]
    reds = [_sum_partials(gk, got, l, f"sum_partials_{k}") for gk, got, l, k in zip(early, gots, landed, BIG[1:])]
    g = dict(zip(BIG[1:], _pair_share(reds, "pair_share", barrier_id=6)))
    delta, new_m, new_v = {}, {}, {}
    for k in BIG[1:]:
        delta[k], new_m[k], new_v[k] = _adamw(w[k][0], g[k], m[k][0], v[k][0], f"adamw_{k}")

    loss, dx = b["loss"], c["dx"]
    small = dict(g1=c["dg1"], g_a=b["dga"], g_h=b["dgh"], lb=c["dlb"], g2=b["dg2"], conv_w=b["dcw"], conv_b=b["dcb"],
                 gf=b["dgf"])
    summed = _allreduce_small(_pack_small(
        [small["g1"], small["g_a"], small["g_h"], small["g2"], small["conv_b"], small["gf"], small["conv_w"]],
        small["lb"], lb, loss))
    loss_total = summed[LOSS_ROW, 0]
    g_conv_w = lax.dynamic_slice(summed[LOSS_ROW - 3 * D_FF // 128:LOSS_ROW].reshape(3, D_FF),
                                 (0, chip * (D_FF // N_CHIPS)), (3, D_FF // N_CHIPS))
    two_d = lambda p, k: p[k].reshape(-1, p[k].shape[-1])
    updated = _small_update(summed, g_conv_w, *[[two_d(p, k) for k, _, _ in SMALL] for p in (w, m, v)])
    for (k, _, _), parts in zip(SMALL, updated):
        g[k], delta[k], new_m[k], new_v[k] = (a.reshape(w[k].shape) for a in parts)

    landed_in = _copies_wait("reduce_wait_w_in", *started_in[:4], plan_in, after=(updated[0][1], delta["w_up"]))[1]
    red_in = _sum_partials(c["dw_in"], gots_in[0], landed_in[0], "sum_partials_w_in")
    g["w_in"] = _pair_share([red_in], "pair_share_w_in", barrier_id=7)[0]
    delta["w_in"], new_m["w_in"], new_v["w_in"] = _adamw(w_in[0], g["w_in"], m_w_in[0], v_w_in[0], "adamw_w_in")
    for k in BIG:
        g[k], delta[k], new_m[k], new_v[k] = g[k][None], delta[k][None], new_m[k][None], new_v[k][None]

    return (loss_total, dx[None], *[g[k] for k in names], *[delta[k] for k in names],
            *[new_m[k] for k in names], *[new_v[k] for k in names])
```

```python
import math

import jax
import jax.numpy as jnp
from jax import lax
from jax.experimental import pallas as pl
from jax.experimental.pallas import tpu as pltpu

F32 = jnp.float32
BF16 = jnp.bfloat16

D_MODEL = 1024
ATTN_W = 512
HGRN_W = 512
HEAD_PAIR = 128
ATTN_BLK = 128
DILATIONS = (1, 4, 16)
ATTN_CHAINS = 4
ATTN_CHAINS_FWD = 8
HGRN_HEADS = 4
HGRN_DIM = 128
HGRN_CHUNK = 64
SUPER = 256
HGRN_SIDE = 4
D_FF = 2816
FF_CHUNKS = ((0, 1536), (1536, D_FF))
MLP_BWD_CHUNKS = ((0, 768), (768, 1408), (1408, 2176), (2176, D_FF))
N_CHIPS = 4
IN_TOTAL = 3584
IN_SHARD = IN_TOTAL // N_CHIPS
UP_SHARD = 2 * D_FF // N_CHIPS
QKV_W = 3 * ATTN_W
HG_W = 4 * HGRN_W
EPS = 1e-6
NEG = -1e30
V7X_VMEM_BYTES = 64 * 1024 * 1024
VMEM_LIMIT = V7X_VMEM_BYTES - 8 * 1024 * 1024

ADAM_LR = 0.001
ADAM_B1 = 0.9
ADAM_B2 = 0.999
ADAM_EPS = 1e-08
ADAM_WD = 0.01
ADAM_STEP = 10

MESH = pl.DeviceIdType.MESH


def _cp(*sem):
    return pltpu.CompilerParams(dimension_semantics=sem or None, vmem_limit_bytes=VMEM_LIMIT)


def _dot(a, b):
    return jnp.dot(a, b, preferred_element_type=F32)


def _dot_nt(a, b):
    return lax.dot_general(a, b, (((1,), (1,)), ((), ())), preferred_element_type=F32)


def _dot_tn(a, b):
    return lax.dot_general(a, b, (((0,), (0,)), ((), ())), preferred_element_type=F32)


def _sigmoid(x):
    return 1.0 / (1.0 + jnp.exp(-x))


def _rms(x, width):
    return lax.rsqrt(jnp.sum(x * x, axis=-1, keepdims=True) * (1.0 / width) + EPS)


def _rms_bwd(dn, n, r, width):
    return r * (dn - n * (jnp.sum(dn * n, axis=-1, keepdims=True) * (1.0 / width)))


def _colsum(x):
    return jnp.sum(x, axis=0, keepdims=True)


def _row(v, k):
    rid = lax.broadcasted_iota(jnp.int32, v.shape, 0)
    return jnp.sum(jnp.where(rid == k, v, 0.0), axis=0, keepdims=True)


def _full(shape):
    return pl.BlockSpec(shape, lambda *_: (0,) * len(shape))


def _once(shape):
    return pl.BlockSpec(shape, lambda *_: (0,) * len(shape), pipeline_mode=pl.Buffered(1))


def _load_side_by_side(w_hbm, w_full, sem):
    width = w_hbm.shape[2]
    cps = [pltpu.make_async_copy(w_hbm.at[k], w_full.at[:, pl.ds(k * width, width)], sem.at[k]) for k in range(N_CHIPS)]
    for cp in cps:
        cp.start()
    for cp in cps:
        cp.wait()


def _in_proj(x, g1, w_in4, tm=512):
    T = x.shape[0]

    def body(x_ref, g_ref, w_hbm, u_ref, qkv_ref, hg_ref, w_full, sem):
        @pl.when(pl.program_id(0) == 0)
        def _():
            _load_side_by_side(w_hbm, w_full, sem)

        xv = x_ref[...]
        u = (xv * _rms(xv, D_MODEL) * g_ref[...]).astype(BF16)
        u_ref[...] = u
        p = _dot(u, w_full[...])
        qkv_ref[...] = p[:, :QKV_W]
        hg_ref[...] = p[:, QKV_W:]

    return pl.pallas_call(
        body, name="in_proj", grid=(T // tm,),
        in_specs=[pl.BlockSpec((tm, D_MODEL), lambda i: (i, 0)), _full((1, D_MODEL)), ANY],
        out_specs=[pl.BlockSpec((tm, D_MODEL), lambda i: (i, 0)), pl.BlockSpec((tm, QKV_W), lambda i: (i, 0)),
                   pl.BlockSpec((tm, HG_W), lambda i: (i, 0))],
        out_shape=[jax.ShapeDtypeStruct((T, D_MODEL), BF16), jax.ShapeDtypeStruct((T, QKV_W), F32),
                   jax.ShapeDtypeStruct((T, HG_W), F32)],
        scratch_shapes=[pltpu.VMEM((D_MODEL, IN_TOTAL), BF16), pltpu.SemaphoreType.DMA((N_CHIPS,))],
        compiler_params=_cp("arbitrary"),
    )(x, g1, w_in4)


def _attn_masks(bias_ref):
    lane = lax.broadcasted_iota(jnp.int32, (ATTN_BLK, HEAD_PAIR), 1)
    row = lax.broadcasted_iota(jnp.int32, (2 * ATTN_BLK, 2 * ATTN_BLK), 0)
    col = lax.broadcasted_iota(jnp.int32, (2 * ATTN_BLK, 2 * ATTN_BLK), 1)
    base = jnp.where(row >= ATTN_BLK, row - ATTN_BLK, row) - col
    for k in range(2):
        dist = base + k * ATTN_BLK
        bias_ref[k] = jnp.where((dist >= 0) & (dist <= ATTN_BLK), 0.0, NEG)
    bias_ref[2] = jnp.where(col >= ATTN_BLK, bias_ref[1], NEG)
    return lane < 64


def _two_heads(blk, first):
    zero = jnp.zeros_like(blk)
    return jnp.concatenate([jnp.where(first, blk, zero), jnp.where(first, zero, blk)], axis=0)


def _attn_rows(idx, nb, d):
    r, n = idx // nb, idx % nb
    kb = jnp.maximum(n - 1, 0)
    if d == 1:
        q0 = pl.multiple_of(n * ATTN_BLK, ATTN_BLK)
        k0 = pl.multiple_of(kb * ATTN_BLK, ATTN_BLK)
        return pl.ds(q0, ATTN_BLK), pl.ds(k0, 2 * ATTN_BLK), n - kb
    return (pl.ds(r + d * ATTN_BLK * n, ATTN_BLK, stride=d), pl.ds(r + d * ATTN_BLK * kb, 2 * ATTN_BLK, stride=d),
            n - kb)


def _attn_fwd(qkv):
    T = qkv.shape[0]

    n_blocks = T // ATTN_BLK

    def body(q_ref, k_ref, v_ref, o_ref, m_ref, l_ref, bias_ref):
        first = _attn_masks(bias_ref)
        for bi, d in enumerate(DILATIONS):
            nb = T // d // ATTN_BLK

            chains = ATTN_CHAINS_FWD
            per_chain = n_blocks // chains
            carried = d > 1 and per_chain % nb == 0

            def block(idx, kept=None, d=d, nb=nb, bi=bi, carried=carried):
                rows, keys, which = _attn_rows(idx, nb, d)
                q2 = _two_heads(q_ref[rows, :] * 0.125, first).astype(BF16)
                if carried:
                    k_own, v_own = k_ref[rows, :].astype(BF16), v_ref[rows, :].astype(BF16)
                    kw = jnp.concatenate([kept[0], k_own], axis=0)
                    vw = jnp.concatenate([kept[1], v_own], axis=0)
                    which = 2 - which
                else:
                    kw = k_ref[keys, :].astype(BF16)
                    vw = v_ref[keys, :].astype(BF16)
                old = (o_ref[rows, :], m_ref[rows, :], l_ref[rows, :]) if bi else None
                s = _dot_nt(q2, kw) + bias_ref[which]
                mb = jnp.max(s, axis=-1, keepdims=True)
                p = jnp.exp(s - mb)
                lb = jnp.sum(p, axis=-1, keepdims=True)
                o2 = _dot(p.astype(BF16), vw)
                o = jnp.where(first, o2[:ATTN_BLK], o2[ATTN_BLK:])
                m = jnp.where(first, mb[:ATTN_BLK], mb[ATTN_BLK:])
                l = jnp.where(first, lb[:ATTN_BLK], lb[ATTN_BLK:])
                if bi:
                    po, pm, pl_ = old
                    mn = jnp.maximum(pm, m)
                    wa = jnp.exp(pm - mn)
                    wb = jnp.exp(m - mn)
                    o, l, m = po * wa + o * wb, pl_ * wa + l * wb, mn
                return (rows, o, m, l), ((k_own, v_own) if carried else 0)

            def step(i, kept, block=block, carried=carried, chains=chains, per_chain=per_chain):
                done = [block(i + ch * per_chain, kept[ch] if carried else None) for ch in range(chains)]
                for (rows, o, m, l), _ in done:
                    o_ref[rows, :] = o
                    m_ref[rows, :] = m
                    l_ref[rows, :] = l
                return tuple(k for _, k in done) if carried else kept

            zero = jnp.zeros((ATTN_BLK, HEAD_PAIR), BF16)
            lax.fori_loop(0, per_chain, step, ((zero, zero),) * chains if carried else 0)

        def finish(i, carry):
            rows = pl.ds(pl.multiple_of(i * SUPER, SUPER), SUPER)
            l = l_ref[rows, :]
            o_ref[rows, :] = o_ref[rows, :] / l
            m_ref[rows, :] = m_ref[rows, :] + jnp.log(l)
            return carry

        lax.fori_loop(0, T // SUPER, finish, 0)

    col = lambda off: pl.BlockSpec((T, HEAD_PAIR), lambda j: (0, off + j))
    return pl.pallas_call(
        body, name="attn_fwd", grid=(4,),
        in_specs=[col(0), col(4), col(8)], out_specs=[col(0), col(0)],
        out_shape=[jax.ShapeDtypeStruct((T, ATTN_W), F32)] * 2,
        scratch_shapes=[pltpu.VMEM((T, HEAD_PAIR), F32), pltpu.VMEM((3, 2 * ATTN_BLK, 2 * ATTN_BLK), F32)],
        compiler_params=_cp("arbitrary"),
    )(qkv, qkv, qkv)


def _attn_bwd(qkv, o, lse, do, token=None):
    T = qkv.shape[0]
    per_chain = T // ATTN_BLK // ATTN_CHAINS
    extra = [] if token is None else [token]

    def body(q_ref, k_ref, v_ref, o_ref, lse_ref, do_ref, *rest):
        outs = rest[len(extra):len(extra) + 3]
        dq_ref, dk_ref, dv_ref, dkb_ref, dvb_ref, bias_ref = rest[len(extra) + 3:]
        first = _attn_masks(bias_ref)
        dq_ref[...] = jnp.zeros_like(dq_ref)
        dk_ref[...] = jnp.zeros_like(dk_ref)
        dv_ref[...] = jnp.zeros_like(dv_ref)

        def grads(rows, kw, vw, which):
            q2 = _two_heads(q_ref[rows, :] * 0.125, first).astype(BF16)
            lse_b = lse_ref[rows, :]
            dob = do_ref[rows, :]
            prod = dob * o_ref[rows, :]
            old = dq_ref[rows, :]
            lse2 = jnp.concatenate(
                [jnp.max(jnp.where(first, lse_b, NEG), axis=-1, keepdims=True),
                 jnp.max(jnp.where(first, NEG, lse_b), axis=-1, keepdims=True)], axis=0)
            p = jnp.exp(_dot_nt(q2, kw) + (bias_ref[which] - lse2))
            delta = jnp.concatenate(
                [jnp.sum(jnp.where(first, prod, 0.0), axis=-1, keepdims=True),
                 jnp.sum(jnp.where(first, 0.0, prod), axis=-1, keepdims=True)], axis=0)
            do2 = _two_heads(dob, first).astype(BF16)
            ds = (p * (_dot_nt(do2, vw) - delta)).astype(BF16)
            dq2 = _dot(ds, kw) * 0.125
            return (old + jnp.where(first, dq2[:ATTN_BLK], dq2[ATTN_BLK:]), _dot_tn(ds, q2),
                    _dot_tn(p.astype(BF16), do2))

        def block(idx):
            rows, keys, which = _attn_rows(idx, T // ATTN_BLK, 1)
            old = dk_ref[keys, :], dv_ref[keys, :]
            dq, ck, cv = grads(rows, k_ref[keys, :].astype(BF16), v_ref[keys, :].astype(BF16), which)
            return rows, keys, dq, old[0] + ck, old[1] + cv

        def step(i, carry):
            done = [block(i + ch * per_chain) for ch in range(ATTN_CHAINS)]
            for rows, keys, dq, dk, dv in done:
                dq_ref[rows, :] = dq
                dk_ref[keys, :] = dk
                dv_ref[keys, :] = dv
            return carry

        lax.fori_loop(0, per_chain, step, 0)

        for d in DILATIONS[1:]:
            nb = T // d // ATTN_BLK

            def block(idx, kept, d=d, nb=nb):
                r, n = idx // nb, idx % nb
                rows = pl.ds(r + d * ATTN_BLK * n, ATTN_BLK, stride=d)
                before = pl.ds(r + d * ATTN_BLK * jnp.maximum(n - 1, 0), ATTN_BLK, stride=d)
                k_prev, v_prev, dk_prev, dv_prev = kept
                k_own, v_own = k_ref[rows, :].astype(BF16), v_ref[rows, :].astype(BF16)
                dq, ck, cv = grads(rows, jnp.concatenate([k_prev, k_own], axis=0),
                                   jnp.concatenate([v_prev, v_own], axis=0), jnp.where(n > 0, 1, 2))
                stores = (rows, before, dq, dk_prev + ck[:ATTN_BLK], dv_prev + cv[:ATTN_BLK], ck[ATTN_BLK:], cv[ATTN_BLK:])
                return stores, (k_own, v_own, ck[ATTN_BLK:], cv[ATTN_BLK:])

            def step(i, kept, block=block):
                done = [block(i + ch * per_chain, kept[ch]) for ch in range(ATTN_CHAINS)]
                for (rows, before, dq, dk_done, dv_done, dk_own, dv_own), _ in done:
                    dq_ref[rows, :] = dq
                    dkb_ref[before, :] = dk_done
                    dvb_ref[before, :] = dv_done
                    dkb_ref[rows, :] = dk_own
                    dvb_ref[rows, :] = dv_own
                return tuple(k for _, k in done)

            zero = jnp.zeros((ATTN_BLK, HEAD_PAIR), F32)
            lax.fori_loop(0, per_chain, step, ((zero.astype(BF16), zero.astype(BF16), zero, zero),) * ATTN_CHAINS)

            def add(i, carry):
                rows = pl.ds(pl.multiple_of(i * SUPER, SUPER), SUPER)
                dk_ref[rows, :] += dkb_ref[rows, :]
                dv_ref[rows, :] += dvb_ref[rows, :]
                return carry

            lax.fori_loop(0, T // SUPER, add, 0)

        def emit(i, carry):
            rows = pl.ds(pl.multiple_of(i * SUPER, SUPER), SUPER)
            for out, acc in zip(outs, (dq_ref, dk_ref, dv_ref)):
                out[rows, :] = acc[rows, :].astype(BF16)
            return carry

        lax.fori_loop(0, T // SUPER, emit, 0)

    col = lambda off: pl.BlockSpec((T, HEAD_PAIR), lambda j: (0, off + j))
    return pl.pallas_call(
        body, name="attn_bwd", grid=(4,),
        in_specs=[col(0), col(4), col(8), col(0), col(0), col(0)] + [_full(t.shape) for t in extra],
        out_specs=[col(0)] * 3,
        out_shape=[jax.ShapeDtypeStruct((T, ATTN_W), BF16)] * 3,
        scratch_shapes=[pltpu.VMEM((T, HEAD_PAIR), F32)] * 5 + [pltpu.VMEM((3, 2 * ATTN_BLK, 2 * ATTN_BLK), F32)],
        compiler_params=_cp("arbitrary"),
    )(qkv, qkv, qkv, o, lse, do, *extra)


def _chunk_ids():
    row = lax.broadcasted_iota(jnp.int32, (SUPER, HGRN_DIM), 0)
    r2 = lax.broadcasted_iota(jnp.int32, (SUPER, SUPER), 0)
    c2 = lax.broadcasted_iota(jnp.int32, (SUPER, SUPER), 1)
    amask = ((r2 // HGRN_CHUNK) == (c2 // HGRN_CHUNK)) & (c2 <= r2)
    return row % HGRN_CHUNK, row // HGRN_CHUNK, amask


def _cumsum_chunk(x, rmod):
    s = 1
    while s < HGRN_CHUNK:
        x = x + jnp.where(rmod >= s, pltpu.roll(x, s, 0), 0.0)
        s *= 2
    return x


def _suffix_sum_chunk(x, rmod):
    s = 1
    while s < HGRN_CHUNK:
        x = x + jnp.where(rmod < HGRN_CHUNK - s, pltpu.roll(x, SUPER - s, 0), 0.0)
        s *= 2
    return x


def _chunk_rows(vs, cid):
    out = vs[-1]
    for c in reversed(range(len(vs) - 1)):
        out = jnp.where(cid == c, vs[c], out)
    return out


def _expand(x, cid):
    return jnp.concatenate([jnp.where(cid == c, x, 0.0) for c in range(SUPER // HGRN_CHUNK)], axis=1)


def _hgrn_gates(q, f, lbv, rmod, cid, tmp):
    sq = _sigmoid(q)
    sg = _sigmoid(f)
    forget = lbv + (1.0 - lbv) * sg
    key = 1.0 - forget
    b = _cumsum_chunk(jnp.log(forget), rmod)
    tmp[...] = b
    bends = [tmp[c * HGRN_CHUNK + HGRN_CHUNK - 1:(c + 1) * HGRN_CHUNK, :] for c in range(SUPER // HGRN_CHUNK)]
    eb = jnp.exp(b)
    enb = jnp.exp(-b)
    ebe = jnp.exp(_chunk_rows(bends, cid) - b)
    return sq, sg, forget, key, eb, enb, ebe, q * sq * eb, key * enb, key * ebe, [jnp.exp(v) for v in bends]


def _hgrn_fwd(hg, lb):
    T = hg.shape[0]
    nsc = T // SUPER
    NC = SUPER // HGRN_CHUNK

    def body(q_ref, f_ref, i_ref, lb_ref, o_ref, st_ref, state, tmp):
        rmod, cid, amask = _chunk_ids()
        state[...] = jnp.zeros_like(state)
        lbv = lb_ref[...]

        def local(sc, u):
            rows = pl.ds(pl.multiple_of(sc * SUPER, SUPER), SUPER)
            iv = i_ref[rows, :].astype(BF16)
            qd, ki, ke, dec = _hgrn_gates(q_ref[rows, :], f_ref[rows, :], lbv, rmod, cid, tmp.at[u])[-4:]
            a = jnp.where(amask, _dot_nt(qd.astype(BF16), ki.astype(BF16)), 0.0)
            return rows, qd, dec, _dot(a.astype(BF16), iv), _dot_tn(iv, _expand(ke, cid).astype(BF16))

        def step(i, carry):
            parts = [local(i * HGRN_SIDE + u, u) for u in range(HGRN_SIDE)]
            st = state[...]
            entering = []
            for u, (_, _, dec, _, ut) in enumerate(parts):
                st_ref[0, i * HGRN_SIDE + u] = st
                sts = []
                for c in range(NC):
                    sts.append(st)
                    st = st * dec[c] + ut[:, c * HGRN_DIM:(c + 1) * HGRN_DIM]
                entering.append(jnp.concatenate(sts, axis=1).astype(BF16))
            state[...] = st
            for (rows, qd, _, o, _), sts in zip(parts, entering):
                o_ref[rows, :] = o + _dot_nt(_expand(qd, cid).astype(BF16), sts)
            return carry

        lax.fori_loop(0, nsc // HGRN_SIDE, step, 0)

    col = lambda off: pl.BlockSpec((T, HGRN_DIM), lambda h: (0, off + h))
    return pl.pallas_call(
        body, name="hgrn_fwd", grid=(HGRN_HEADS,),
        in_specs=[col(0), col(4), col(8), pl.BlockSpec((1, HGRN_DIM), lambda h: (0, h))],
        out_specs=[pl.BlockSpec((T, HGRN_DIM), lambda h: (0, h)),
                   pl.BlockSpec((1, nsc, HGRN_DIM, HGRN_DIM), lambda h: (h, 0, 0, 0))],
        out_shape=[jax.ShapeDtypeStruct((T, HGRN_W), F32),
                   jax.ShapeDtypeStruct((HGRN_HEADS, nsc, HGRN_DIM, HGRN_DIM), F32)],
        scratch_shapes=[pltpu.VMEM((HGRN_DIM, HGRN_DIM), F32), pltpu.VMEM((HGRN_SIDE, SUPER, HGRN_DIM), F32)],
        compiler_params=_cp("arbitrary"),
    )(hg, hg, hg, lb)


def _hgrn_bwd(hg, lb, states, do, mixed, dh1b):
    T = hg.shape[0]
    nsc = T // SUPER
    NC = SUPER // HGRN_CHUNK
    span = HGRN_SIDE * SUPER

    def body(q_ref, f_ref, i_ref, lb_ref, st_ref, do_ref, mx_ref, dh_ref, dq_ref, df_ref, di_ref, dlb_ref, dwo_ref,
             dstate, tmp, acc):
        rmod, cid, amask = _chunk_ids()
        dstate[...] = jnp.zeros_like(dstate)
        dlb_ref[...] = jnp.zeros_like(dlb_ref)
        acc[...] = jnp.zeros_like(acc)
        lbv = lb_ref[...]

        def local(sc, u):
            rows = pl.ds(pl.multiple_of(sc * SUPER, SUPER), SUPER)
            q = q_ref[rows, :]
            ivf = i_ref[rows, :]
            iv = ivf.astype(BF16)
            dof = do_ref[rows, :]
            dob = dof.astype(BF16)
            sq, sg, forget, key, eb, enb, ebe, qd, ki, ke, dec = _hgrn_gates(q, f_ref[rows, :], lbv, rmod, cid,
                                                                            tmp.at[u])
            qdb, kib = qd.astype(BF16), ki.astype(BF16)
            keexp = _expand(ke, cid).astype(BF16)
            a = jnp.where(amask, _dot_nt(qdb, kib), 0.0).astype(BF16)
            ut = _dot_tn(iv, keexp)
            st = st_ref[0, sc]
            sts = []
            for c in range(NC):
                sts.append(st)
                st = st * dec[c] + ut[:, c * HGRN_DIM:(c + 1) * HGRN_DIM]
            gt = _dot_tn(dob, _expand(qd, cid).astype(BF16))
            da = jnp.where(amask, _dot_nt(dob, iv), 0.0).astype(BF16)
            ststack = jnp.concatenate(sts, axis=0).astype(BF16)
            return dict(rows=rows, q=q, sq=sq, sg=sg, forget=forget, eb=eb, enb=enb, ebe=ebe, qd=qd, ki=ki, ke=ke,
                        dec=dec, sts=sts, gt=gt, keexp=keexp, ivexp=_expand(ivf, cid).astype(BF16),
                        div=_dot_tn(a, dob), dki=_dot_tn(da, qdb),
                        dqd=_dot(da, kib) + _dot(_expand(dof, cid).astype(BF16), ststack))

        def finish(p, nxt, ddec):
            ncat = jnp.concatenate(nxt, axis=1).astype(BF16)
            nstack = jnp.concatenate(nxt, axis=0).astype(BF16)
            dke = _dot(p["ivexp"], nstack)
            dkk = dke * p["ke"]
            dkey = p["dki"] * p["enb"] + dke * p["ebe"]
            db = p["dqd"] * p["qd"] - p["dki"] * p["ki"] - dkk
            dbends = [_colsum(jnp.where(cid == c, dkk, 0.0)) + ddec[c] * p["dec"][c] for c in range(NC)]
            dforget = (_suffix_sum_chunk(db, rmod) + _chunk_rows(dbends, cid)) / p["forget"] - dkey
            sg, sq, q = p["sg"], p["sq"], p["q"]
            df_ref[p["rows"], :] = (dforget * (1.0 - lbv) * sg * (1.0 - sg)).astype(BF16)
            dq_ref[p["rows"], :] = (p["dqd"] * p["eb"] * (sq * (1.0 + q * (1.0 - sq)))).astype(BF16)
            di_ref[p["rows"], :] = (p["div"] + _dot_nt(p["keexp"], ncat)).astype(BF16)
            return _colsum(dforget * (1.0 - sg))

        def step(i, carry):
            parts = [local(nsc - 1 - (i * HGRN_SIDE + u), u) for u in range(HGRN_SIDE)]
            dst = dstate[...]
            chained = []
            for p in parts:
                nxt = [None] * NC
                ddec = [None] * NC
                for c in reversed(range(NC)):
                    nxt[c] = dst
                    ddec[c] = _colsum(dst * p["sts"][c])
                    dst = dst * p["dec"][c] + p["gt"][:, c * HGRN_DIM:(c + 1) * HGRN_DIM]
                chained.append((nxt, ddec))
            dstate[...] = dst
            dlb = dlb_ref[...]
            for p, (nxt, ddec) in zip(parts, chained):
                dlb = dlb + finish(p, nxt, ddec)
            dlb_ref[...] = dlb
            rows = pl.ds(pl.multiple_of(i * span, span), span)
            acc[...] += _dot_tn(mx_ref[rows, :], dh_ref[rows, :])
            return carry

        lax.fori_loop(0, nsc // HGRN_SIDE, step, 0)
        dwo_ref[...] = acc[...].astype(BF16)

    col = lambda off: pl.BlockSpec((T, HGRN_DIM), lambda h: (0, off + h))
    own = pl.BlockSpec((T, HGRN_DIM), lambda h: (0, h))
    vec = pl.BlockSpec((1, HGRN_DIM), lambda h: (0, h))
    quarter = D_MODEL // HGRN_HEADS
    return pl.pallas_call(
        body, name="hgrn_bwd", grid=(HGRN_HEADS,),
        in_specs=[col(0), col(4), col(8), vec,
                  pl.BlockSpec((1, nsc, HGRN_DIM, HGRN_DIM), lambda h: (h, 0, 0, 0)), own,
                  pl.BlockSpec((T, quarter), lambda h: (0, h)), _once((T, D_MODEL))],
        out_specs=[own, own, own, vec, pl.BlockSpec((quarter, D_MODEL), lambda h: (h, 0))],
        out_shape=[jax.ShapeDtypeStruct((T, HGRN_W), BF16)] * 3 + [jax.ShapeDtypeStruct((1, HGRN_W), F32),
                                                                   jax.ShapeDtypeStruct((D_MODEL, D_MODEL), BF16)],
        scratch_shapes=[pltpu.VMEM((HGRN_DIM, HGRN_DIM), F32), pltpu.VMEM((HGRN_SIDE, SUPER, HGRN_DIM), F32),
                        pltpu.VMEM((quarter, D_MODEL), F32)],
        compiler_params=_cp("arbitrary"),
    )(hg, hg, hg, lb, states, do, mixed, dh1b)


def _rec_heads(rec, gate, g_h):
    rr = jnp.concatenate(
        [jnp.broadcast_to(_rms(rec[:, h * HGRN_DIM:(h + 1) * HGRN_DIM], HGRN_DIM), (rec.shape[0], HGRN_DIM))
         for h in range(HGRN_HEADS)], axis=1)
    rn = rec * rr
    sg = _sigmoid(gate)
    return rr, rn, sg


def _mix_out(attn_o, rec_o, hg, x, g_a, g_h, w_out, tm=512):
    T = x.shape[0]

    def body(a_ref, r_ref, gt_ref, x_ref, ga_ref, gh_ref, w_ref, h1_ref, mixed_ref):
        a = a_ref[...]
        an = a * _rms(a, ATTN_W) * ga_ref[...]
        gate = gt_ref[...]
        _, rn, sg = _rec_heads(r_ref[...], gate, gh_ref[...])
        mixed = jnp.concatenate([an, rn * gh_ref[...] * (gate * sg)], axis=1).astype(BF16)
        mixed_ref[...] = mixed
        h1_ref[...] = x_ref[...] + _dot(mixed, w_ref[...])

    row = lambda w: pl.BlockSpec((tm, w), lambda i: (i, 0))
    return pl.pallas_call(
        body, name="mix_out", grid=(T // tm,),
        in_specs=[row(ATTN_W), row(HGRN_W), pl.BlockSpec((tm, HGRN_W), lambda i: (i, 3)), row(D_MODEL),
                  _full((1, ATTN_W)), _full((1, HGRN_W)), _once((D_MODEL, D_MODEL))],
        out_specs=[row(D_MODEL), row(D_MODEL)],
        out_shape=[jax.ShapeDtypeStruct((T, D_MODEL), F32), jax.ShapeDtypeStruct((T, D_MODEL), BF16)],
        compiler_params=_cp("arbitrary"),
    )(attn_o, rec_o, hg, x, g_a, g_h, w_out)


_INV_SQRT2 = 1.0 / math.sqrt(2.0)
_INV_SQRT2PI = 1.0 / math.sqrt(2.0 * math.pi)


def _gelu(x):
    return 0.5 * x * (1.0 + lax.erf(x * _INV_SQRT2))


def _gelu_and_grad(x):
    z = x * _INV_SQRT2
    cdf = 0.5 * (1.0 + lax.erf(z))
    return x * cdf, cdf + (x * _INV_SQRT2PI) * jnp.exp(-(z * z))


def _shift_down(g, prev, rowid):
    p1 = _row(prev, prev.shape[0] - 1)
    p2 = _row(prev, prev.shape[0] - 2)
    s1 = jnp.where(rowid == 0, p1, pltpu.roll(g, 1, 0))
    s2 = jnp.where(rowid == 0, p2, jnp.where(rowid == 1, p1, pltpu.roll(g, 2, 0)))
    return s1, s2


def _mlp_fwd(h1, g2, w_up4, conv_w, conv_b, w_down, gf, tgt, tm=256):
    T = h1.shape[0]

    def body(h_ref, g2_ref, wu_hbm, cw_ref, cb_ref, wd_ref, gf_ref, t_ref,
             u_ref, gate_ref, val_ref, conv_ref, act_ref, dh_ref, loss_ref, dgf_ref, carry, wu_ref, sem):
        i = pl.program_id(0)

        @pl.when(i == 0)
        def _():
            carry[...] = jnp.zeros_like(carry)
            loss_ref[...] = jnp.zeros_like(loss_ref)
            dgf_ref[...] = jnp.zeros_like(dgf_ref)
            _load_side_by_side(wu_hbm, wu_ref, sem)

        h = h_ref[...]
        u = (h * _rms(h, D_MODEL) * g2_ref[...]).astype(BF16)
        u_ref[...] = u
        y2 = jnp.zeros((tm, D_MODEL), F32)
        for lo, hi in FF_CHUNKS:
            cols = slice(lo, hi)
            rowid = lax.broadcasted_iota(jnp.int32, (tm, hi - lo), 0)
            gb = _dot(u, wu_ref[:, lo:hi]).astype(BF16)
            vb = _dot(u, wu_ref[:, D_FF + lo:D_FF + hi]).astype(BF16)
            gate_ref[:, cols] = gb
            val_ref[:, cols] = vb
            g = gb.astype(F32)
            s1, s2 = _shift_down(g, carry[:, cols], rowid)
            carry[:, cols] = g[tm - 8:, :]
            conv = cb_ref[:, cols] + cw_ref[0:1, cols] * s2 + cw_ref[1:2, cols] * s1 + cw_ref[2:3, cols] * g
            act = (_gelu(conv) * vb.astype(F32)).astype(BF16)
            conv_ref[:, cols] = conv.astype(BF16)
            act_ref[:, cols] = act
            y2 = y2 + _dot(act, wd_ref[cols, :])
        h2 = h + y2
        rf = _rms(h2, D_MODEL)
        n = h2 * rf
        gfv = gf_ref[...]
        e = n * gfv - t_ref[...]
        loss_ref[...] += jnp.sum(e * e) * (0.5 / D_MODEL)
        dy = e * (1.0 / D_MODEL)
        dgf_ref[...] += _colsum(dy * n)
        dh_ref[...] = _rms_bwd(dy * gfv, n, rf, D_MODEL)

    row = lambda w: pl.BlockSpec((tm, w), lambda i: (i, 0))
    return pl.pallas_call(
        body, name="mlp_fwd", grid=(T // tm,),
        in_specs=[row(D_MODEL), _full((1, D_MODEL)), ANY, _full((3, D_FF)),
                  _full((1, D_FF)), _once((D_FF, D_MODEL)), _full((1, D_MODEL)), row(D_MODEL)],
        out_specs=[row(D_MODEL), row(D_FF), row(D_FF), row(D_FF), row(D_FF), row(D_MODEL), _full((1, 128)),
                   _full((1, D_MODEL))],
        out_shape=[jax.ShapeDtypeStruct((T, D_MODEL), BF16)] + [jax.ShapeDtypeStruct((T, D_FF), BF16)] * 4
        + [jax.ShapeDtypeStruct((T, D_MODEL), F32),
                   jax.ShapeDtypeStruct((1, 128), F32), jax.ShapeDtypeStruct((1, D_MODEL), F32)],
        scratch_shapes=[pltpu.VMEM((8, D_FF), F32), pltpu.VMEM((D_MODEL, 2 * D_FF), BF16),
                        pltpu.SemaphoreType.DMA((N_CHIPS,))],
        compiler_params=_cp("arbitrary"),
    )(h1, g2, w_up4, conv_w, conv_b, w_down, gf, tgt)


def _mlp_bwd(dh2, gate, val, conv, act, conv_w, w_down, tm=256):
    T = dh2.shape[0]
    nb = T // tm

    def body(dh_ref, gate_ref, val_ref, conv_ref, act_ref, cw_ref, wd_ref, dgv_ref, dcw_ref, dcb_ref, dwd_ref,
             carry, acc):
        i = pl.program_id(0)

        @pl.when(i == 0)
        def _():
            carry[...] = jnp.zeros_like(carry)
            dcw_ref[...] = jnp.zeros_like(dcw_ref)
            dcb_ref[...] = jnp.zeros_like(dcb_ref)
            acc[...] = jnp.zeros_like(acc)

        dhb = dh_ref[...].astype(BF16)
        for lo, hi in MLP_BWD_CHUNKS:
            cols = slice(lo, hi)
            rowid = lax.broadcasted_iota(jnp.int32, (tm, hi - lo), 0)
            acc[cols, :] += _dot_tn(act_ref[:, cols], dhb)
            g = gate_ref[:, cols].astype(F32)
            v = val_ref[:, cols].astype(F32)
            cv = conv_ref[:, cols].astype(F32)
            dact = _dot_nt(dhb, wd_ref[cols, :])
            gl, gp = _gelu_and_grad(cv)
            dconv = dact * v * gp
            nxt = carry[:, cols]
            n0, n1 = _row(nxt, 0), _row(nxt, 1)
            u1 = jnp.where(rowid == tm - 1, n0, pltpu.roll(dconv, tm - 1, 0))
            u2 = jnp.where(rowid == tm - 1, n1, jnp.where(rowid == tm - 2, n0, pltpu.roll(dconv, tm - 2, 0)))
            carry[:, cols] = dconv[0:8, :]
            dcb_ref[:, cols] += _colsum(dconv)
            dcw_ref[0:1, cols] += _colsum(u2 * g)
            dcw_ref[1:2, cols] += _colsum(u1 * g)
            dcw_ref[2:3, cols] += _colsum(dconv * g)
            dgate = cw_ref[2:3, cols] * dconv + cw_ref[1:2, cols] * u1 + cw_ref[0:1, cols] * u2
            dgv_ref[:, cols] = dgate.astype(BF16)
            dgv_ref[:, D_FF + lo:D_FF + hi] = (dact * gl).astype(BF16)

        @pl.when(i == nb - 1)
        def _():
            for lo, hi in MLP_BWD_CHUNKS:
                dwd_ref[lo:hi, :] = acc[lo:hi, :].astype(BF16)

    rev = lambda w: pl.BlockSpec((tm, w), lambda i: (nb - 1 - i, 0))
    return pl.pallas_call(
        body, name="mlp_bwd", grid=(nb,),
        in_specs=[rev(D_MODEL), rev(D_FF), rev(D_FF), rev(D_FF), rev(D_FF), _full((3, D_FF)), _once((D_FF, D_MODEL))],
        out_specs=[rev(2 * D_FF), _full((3, D_FF)), _full((1, D_FF)), _once((D_FF, D_MODEL))],
        out_shape=[jax.ShapeDtypeStruct((T, 2 * D_FF), BF16), jax.ShapeDtypeStruct((3, D_FF), F32),
                   jax.ShapeDtypeStruct((1, D_FF), F32), jax.ShapeDtypeStruct((D_FF, D_MODEL), BF16)],
        scratch_shapes=[pltpu.VMEM((8, D_FF), F32), pltpu.VMEM((D_FF, D_MODEL), F32)],
        compiler_params=_cp("arbitrary"),
    )(dh2, gate, val, conv, act, conv_w, w_down)


def _up_out_bwd(dgv, w_up4, h1, g2, dh2, w_out, attn_o, rec_o, hg, g_a, g_h, tm=256):
    T = h1.shape[0]

    def body(dgv_ref, wu_hbm, h_ref, g2_ref, dh2_ref, wo_ref, a_ref, r_ref, gt_ref, ga_ref, gh_ref,
             dh1_ref, dg2_ref, da_ref, dr_ref, dgt_ref, dga_ref, dgh_ref, dh1b_ref, wu_ref, sem):
        @pl.when(pl.program_id(0) == 0)
        def _():
            dg2_ref[...] = jnp.zeros_like(dg2_ref)
            dga_ref[...] = jnp.zeros_like(dga_ref)
            dgh_ref[...] = jnp.zeros_like(dgh_ref)
            _load_side_by_side(wu_hbm, wu_ref, sem)

        du = _dot_nt(dgv_ref[...], wu_ref[...])
        h = h_ref[...]
        r = _rms(h, D_MODEL)
        n = h * r
        dg2_ref[...] += _colsum(du * n)
        dh1 = dh2_ref[...] + _rms_bwd(du * g2_ref[...], n, r, D_MODEL)
        dh1_ref[...] = dh1
        dh1b = dh1.astype(BF16)
        dh1b_ref[...] = dh1b
        dmix = _dot_nt(dh1b, wo_ref[...])
        dan = dmix[:, :ATTN_W]
        a = a_ref[...]
        ra = _rms(a, ATTN_W)
        na = a * ra
        dga_ref[...] += _colsum(dan * na)
        da_ref[...] = _rms_bwd(dan * ga_ref[...], na, ra, ATTN_W)
        dmr = dmix[:, ATTN_W:]
        gate = gt_ref[...]
        ghv = gh_ref[...]
        rr, rn, sg = _rec_heads(r_ref[...], gate, ghv)
        dgt_ref[...] = (dmr * rn * ghv * (sg * (1.0 + gate * (1.0 - sg)))).astype(BF16)
        drecn = dmr * (gate * sg)
        dgh_ref[...] += _colsum(drecn * rn)
        drn = drecn * ghv
        prod = drn * rn
        mean = jnp.concatenate(
            [jnp.broadcast_to(jnp.sum(prod[:, h_ * HGRN_DIM:(h_ + 1) * HGRN_DIM], axis=-1, keepdims=True),
                              (tm, HGRN_DIM)) for h_ in range(HGRN_HEADS)], axis=1) * (1.0 / HGRN_DIM)
        dr_ref[...] = rr * (drn - rn * mean)

    row = lambda w: pl.BlockSpec((tm, w), lambda i: (i, 0))
    return pl.pallas_call(
        body, name="up_out_bwd", grid=(T // tm,),
        in_specs=[row(2 * D_FF), ANY, row(D_MODEL), _full((1, D_MODEL)),
                  row(D_MODEL), _once((D_MODEL, D_MODEL)), row(ATTN_W), row(HGRN_W),
                  pl.BlockSpec((tm, HGRN_W), lambda i: (i, 3)), _full((1, ATTN_W)), _full((1, HGRN_W))],
        out_specs=[row(D_MODEL), _full((1, D_MODEL)), row(ATTN_W), row(HGRN_W), row(HGRN_W),
                   _full((1, ATTN_W)), _full((1, HGRN_W)), row(D_MODEL)],
        out_shape=[jax.ShapeDtypeStruct((T, D_MODEL), F32), jax.ShapeDtypeStruct((1, D_MODEL), F32),
                   jax.ShapeDtypeStruct((T, ATTN_W), F32), jax.ShapeDtypeStruct((T, HGRN_W), F32),
                   jax.ShapeDtypeStruct((T, HGRN_W), BF16), jax.ShapeDtypeStruct((1, ATTN_W), F32),
                   jax.ShapeDtypeStruct((1, HGRN_W), F32), jax.ShapeDtypeStruct((T, D_MODEL), BF16)],
        scratch_shapes=[pltpu.VMEM((D_MODEL, 2 * D_FF), BF16), pltpu.SemaphoreType.DMA((N_CHIPS,))],
        compiler_params=_cp("arbitrary"),
    )(dgv, w_up4, h1, g2, dh2, w_out, attn_o, rec_o, hg, g_a, g_h)


def _in_bwd(dqkv, dhg, w_in4, x, g1, dh1, tm=512):
    T = x.shape[0]

    def body(*refs):
        parts = refs[:7]
        w_hbm, x_ref, g_ref, dh1_ref, dp_ref, dx_ref, dg_ref, w_full, sem = refs[7:]

        @pl.when(pl.program_id(0) == 0)
        def _():
            dg_ref[...] = jnp.zeros_like(dg_ref)
            _load_side_by_side(w_hbm, w_full, sem)

        dp = jnp.concatenate([p[...] for p in parts], axis=1)
        dp_ref[...] = dp
        du = _dot_nt(dp, w_full[...])
        xv = x_ref[...]
        r = _rms(xv, D_MODEL)
        n = xv * r
        dg_ref[...] += _colsum(du * n)
        dx_ref[...] = dh1_ref[...] + _rms_bwd(du * g_ref[...], n, r, D_MODEL)

    row = lambda w: pl.BlockSpec((tm, w), lambda i: (i, 0))
    return pl.pallas_call(
        body, name="in_bwd", grid=(T // tm,),
        in_specs=[row(ATTN_W)] * 7 + [ANY, row(D_MODEL), _full((1, D_MODEL)), row(D_MODEL)],
        out_specs=[row(IN_TOTAL), row(D_MODEL), _full((1, D_MODEL))],
        out_shape=[jax.ShapeDtypeStruct((T, IN_TOTAL), BF16), jax.ShapeDtypeStruct((T, D_MODEL), F32),
                   jax.ShapeDtypeStruct((1, D_MODEL), F32)],
        scratch_shapes=[pltpu.VMEM((D_MODEL, IN_TOTAL), BF16), pltpu.SemaphoreType.DMA((N_CHIPS,))],
        compiler_params=_cp("arbitrary"),
    )(*dqkv, *dhg, w_in4, x, g1, dh1)


def _dw(a, b, kb, nb_, name, tk=1024, side=1):
    T, K = a.shape
    N = b.shape[1]
    nk, nn, nt = K // kb, N // (nb_ * side), T // tk

    def body(a_ref, b_ref, o_ref, acc):
        t = pl.program_id(2)

        @pl.when(t == 0)
        def _():
            acc[...] = jnp.zeros_like(acc)

        acc[...] += _dot_tn(a_ref[...], b_ref[...].astype(BF16))

        @pl.when(t == nt - 1)
        def _():
            for s in range(side):
                o_ref[s] = acc[:, s * nb_:(s + 1) * nb_].astype(BF16)

    return pl.pallas_call(
        body, name=name, grid=(nk, nn, nt),
        in_specs=[pl.BlockSpec((tk, kb), lambda i, j, t: (t, i)),
                  pl.BlockSpec((tk, nb_ * side), lambda i, j, t: (t, j))],
        out_specs=pl.BlockSpec((side, kb, nb_), lambda i, j, t: (i * nn + j, 0, 0)),
        out_shape=jax.ShapeDtypeStruct((nk * nn * side, kb, nb_), BF16),
        scratch_shapes=[pltpu.VMEM((kb, nb_ * side), F32)],
        compiler_params=_cp("arbitrary", "arbitrary", "arbitrary"),
    )(a, b)


def _step_channel(a, x, tgt, g_a, g_h, w_out, g2, w_up4, conv_w, conv_b, w_down, gf):
    h1, mixed = _mix_out(a["attn_o"], a["rec_o"], a["hg"], x, g_a, g_h, w_out)
    u2, gate, val, conv, act, dh2, loss, dgf = _mlp_fwd(h1, g2, w_up4, conv_w, conv_b, w_down, gf, tgt)
    dgv, dcw, dcb, dw_down = _mlp_bwd(dh2, gate, val, conv, act, conv_w, w_down)
    dw_down = dw_down.reshape(N_CHIPS, D_FF // N_CHIPS, D_MODEL)
    dh1, dg2, da, dr, dgt, dga, dgh, dh1b = _up_out_bwd(dgv, w_up4, h1, g2, dh2, w_out, a["attn_o"], a["rec_o"],
                                                        a["hg"], g_a, g_h)
    dw_up = _dw(u2, dgv, D_MODEL, UP_SHARD, "dw_up", side=2)
    return dict(loss=loss, dgf=dgf, dcw=dcw, dcb=dcb, dg2=dg2, dga=dga, dgh=dgh, dh1=dh1, da=da, dr=dr, dgt=dgt,
                dw_down=dw_down, dw_up=dw_up, mixed=mixed, dh1b=dh1b)


def _step_mixers_bwd(a, b, x, g1, w_in4, lb, dqkv):
    dhq, dhf, dhi, dlb, dw_out = _hgrn_bwd(a["hg"], lb, a["states"], b["dr"], b["mixed"], b["dh1b"])
    dproj, dx, dg1 = _in_bwd(dqkv, [dhq, dhf, dhi, b["dgt"]], w_in4, x, g1, b["dh1"])
    dw_in = _dw(a["u1"], dproj, D_MODEL, IN_SHARD, "dw_in", side=2)
    return dict(dx=dx, dg1=dg1, dlb=dlb, dw_in=dw_in,
                dw_out=dw_out.reshape(N_CHIPS, D_MODEL // N_CHIPS, D_MODEL))


BIG = ("w_in", "w_out", "w_up", "w_down")
ANY = pl.BlockSpec(memory_space=pl.ANY)


def _place():
    x, y, c = lax.axis_index("x"), lax.axis_index("y"), lax.axis_index("c")
    chips = [(1 - x, y), (x, 1 - y), (1 - x, 1 - y)]
    return x, y, c, chips


def _remote(src, dst, send_sems, recv_sems, k, to):
    return pltpu.make_async_remote_copy(src_ref=src, dst_ref=dst, send_sem=send_sems.at[k], recv_sem=recv_sems.at[k],
                                        device_id=to, device_id_type=MESH)


def _gather_weights(shards, conv_w):
    n = len(shards)
    halves = [s.shape[0] // 2 for s in shards]

    def body(*refs):
        ins, cw, outs, ocw = refs[:n], refs[n], refs[n + 1:2 * n + 1], refs[2 * n + 1]
        send_sems, recv_sems = refs[2 * n + 2:]
        x, y, c, chips = _place()
        me, sibling = 2 * x + y, (x, y, 1 - c)

        def part(w, chip, half):
            return outs[w].at[chip, pl.ds(half * halves[w], halves[w]), :]

        sent = []
        for j, chip in enumerate(chips):
            for w in range(n):
                sent.append(_remote(ins[w].at[pl.ds(c * halves[w], halves[w]), :], part(w, me, c),
                                    send_sems, recv_sems, w * 3 + j, (*chip, c)))
            sent.append(_remote(cw, ocw.at[me], send_sems, recv_sems, 6 * n + j, (*chip, c)))
        for cp in sent:
            cp.start()
        for j, chip in enumerate(chips):
            kj = 2 * chip[0] + chip[1]
            for w in range(n):
                _remote(part(w, kj, c), part(w, kj, c), send_sems, recv_sems, w * 3 + j, (*chip, c)).wait_recv()
                fwd = _remote(part(w, kj, c), part(w, kj, c), send_sems, recv_sems, 3 * n + w * 3 + j, sibling)
                fwd.start()
                sent.append(fwd)
        for j, chip in enumerate(chips):
            kj = 2 * chip[0] + chip[1]
            for w in range(n):
                _remote(part(w, kj, 1 - c), part(w, kj, 1 - c), send_sems, recv_sems, 3 * n + w * 3 + j,
                        sibling).wait_recv()
            _remote(cw, ocw.at[kj], send_sems, recv_sems, 6 * n + j, (*chip, c)).wait_recv()
        for cp in sent:
            cp.wait_send()

    n_sem = 6 * n + 3
    outs = pl.pallas_call(
        body, name="gather_weights",
        in_specs=[ANY] * (n + 1), out_specs=[ANY] * (n + 1),
        out_shape=[jax.ShapeDtypeStruct((N_CHIPS,) + s.shape, s.dtype) for s in shards]
        + [jax.ShapeDtypeStruct((N_CHIPS,) + conv_w.shape, conv_w.dtype)],
        scratch_shapes=[pltpu.SemaphoreType.DMA((n_sem,)), pltpu.SemaphoreType.DMA((n_sem,))],
    )(*shards, conv_w)
    chip = 2 * lax.axis_index("x") + lax.axis_index("y")
    return [lax.dynamic_update_slice(o, s[None], (chip,) + (0,) * s.ndim) for o, s in zip(outs, [*shards, conv_w])]


def _allreduce_small(buf):
    rows = buf.shape[0]

    def body(in_ref, out_ref, slots, send_sems, recv_sems):
        x, y, c, _ = _place()
        me = 4 * x + 2 * y + c
        slots[me] = in_ref[...]
        sent = []
        for p in range(1, 8):
            to = (x ^ (p >> 2), y ^ ((p >> 1) & 1), c ^ (p & 1))
            sent.append(_remote(in_ref, slots.at[me], send_sems, recv_sems, p, to))
        for cp in sent:
            cp.start()
        for p in range(1, 8):
            frm = 4 * (x ^ (p >> 2)) + 2 * (y ^ ((p >> 1) & 1)) + (c ^ (p & 1))
            _remote(in_ref, slots.at[frm], send_sems, recv_sems, p, (x, y, c)).wait_recv()
        for cp in sent:
            cp.wait_send()
        acc = slots[0]
        for d in range(1, 8):
            acc = acc + slots[d]
        out_ref[...] = acc

    vm = pl.BlockSpec(memory_space=pltpu.VMEM)
    return pl.pallas_call(
        body, name="allreduce_small", in_specs=[vm], out_specs=vm,
        out_shape=jax.ShapeDtypeStruct(buf.shape, F32),
        scratch_shapes=[pltpu.VMEM((8, rows, 128), F32), pltpu.SemaphoreType.DMA((8,)), pltpu.SemaphoreType.DMA((8,))],
    )(buf)


def _sibling_peer():
    x, y, c, _ = _place()
    return [(x, y, 1 - c)]


def _chip_peers():
    x, y, c, chips = _place()
    return [(*chip, c) for chip in chips]


def _handshake(peers):
    barrier = pltpu.get_barrier_semaphore()
    for peer in peers:
        pl.semaphore_signal(barrier, inc=1, device_id=peer, device_id_type=MESH)
    pl.semaphore_wait(barrier, len(peers))


def _pair_exchange(gs, name, barrier_id):
    n = len(gs)
    halves = [g.shape[1] // 2 for g in gs]

    def body(*refs):
        g, got = refs[:n], refs[n:2 * n]
        send_sems, recv_sems = refs[2 * n:]
        _handshake(_sibling_peer())
        x, y, c, _ = _place()
        cps = [_remote(g[w].at[:, pl.ds((1 - c) * halves[w], halves[w]), :], got[w], send_sems, recv_sems, w,
                       (x, y, 1 - c)) for w in range(n)]
        for cp in cps:
            cp.start()
        for cp in cps:
            cp.wait()

    return pl.pallas_call(
        body, name=name, in_specs=[ANY] * n, out_specs=[ANY] * n,
        out_shape=[jax.ShapeDtypeStruct((N_CHIPS, h, g.shape[2]), g.dtype) for g, h in zip(gs, halves)],
        scratch_shapes=[pltpu.SemaphoreType.DMA((n,)), pltpu.SemaphoreType.DMA((n,))],
        compiler_params=pltpu.CompilerParams(collective_id=barrier_id),
    )(*gs)


def _core_id():
    return lax.axis_index("c").reshape(1).astype(jnp.int32)


def _pair_sum(g, got, name):
    h, C = got.shape[1:]

    def body(c_ref, g_ref, b_ref, o_ref):
        o_ref[...] = (g_ref[...].astype(F32) + b_ref[...].astype(F32)).astype(BF16)

    blk = pl.BlockSpec((1, h, C), lambda k, c_ref: (k, 0, 0))
    return pl.pallas_call(
        body, name=name,
        grid_spec=pltpu.PrefetchScalarGridSpec(
            num_scalar_prefetch=1, grid=(N_CHIPS,),
            in_specs=[pl.BlockSpec((1, h, C), lambda k, c_ref: (k, c_ref[0], 0)), blk], out_specs=blk),
        out_shape=jax.ShapeDtypeStruct(got.shape, BF16), compiler_params=_cp("arbitrary"))(_core_id(), g, got)


def _sum_partials(g, got, landed, name):
    h, C = got.shape[1:]

    def body(ids, g_ref, b_ref, l_ref, o_ref):
        acc = g_ref[0].astype(F32) + b_ref[0].astype(F32)
        for j in range(3):
            acc = acc + l_ref[j].astype(F32)
        o_ref[...] = acc

    ids = jnp.stack([2 * lax.axis_index("x") + lax.axis_index("y"), lax.axis_index("c")]).astype(jnp.int32)
    return pl.pallas_call(
        body, name=name,
        grid_spec=pltpu.PrefetchScalarGridSpec(
            num_scalar_prefetch=1, grid=(1,),
            in_specs=[pl.BlockSpec((1, h, C), lambda i, ids: (ids[0], ids[1], 0)),
                      pl.BlockSpec((1, h, C), lambda i, ids: (ids[0], 0, 0)),
                      pl.BlockSpec((3, h, C), lambda i, ids: (0, 0, 0))],
            out_specs=pl.BlockSpec((h, C), lambda i, ids: (ids[1], 0))),
        out_shape=jax.ShapeDtypeStruct((2 * h, C), F32), compiler_params=_cp("arbitrary"))(ids, g, got, landed)


def _pair_share(reds, name, barrier_id):
    n = len(reds)

    def body(*refs):
        out = refs[n:2 * n]
        send_sems, recv_sems = refs[2 * n:]
        _handshake(_sibling_peer())
        x, y, c, _ = _place()
        def half(w, which):
            h = out[w].shape[0] // 2
            return out[w].at[pl.ds(which * h, h), :]

        cps = [_remote(half(w, c), half(w, c), send_sems, recv_sems, w, (x, y, 1 - c)) for w in range(n)]
        for cp in cps:
            cp.start()
        for w in range(n):
            _remote(half(w, 1 - c), half(w, 1 - c), send_sems, recv_sems, w, (x, y, 1 - c)).wait_recv()
        for cp in cps:
            cp.wait_send()

    return pl.pallas_call(
        body, name=name, in_specs=[ANY] * n, out_specs=[ANY] * n,
        out_shape=[jax.ShapeDtypeStruct(r.shape, F32) for r in reds],
        input_output_aliases={w: w for w in range(n)},
        scratch_shapes=[pltpu.SemaphoreType.DMA((n,)), pltpu.SemaphoreType.DMA((n,))],
        compiler_params=pltpu.CompilerParams(collective_id=barrier_id),
    )(*reds)


HBM = pl.BlockSpec(memory_space=pltpu.HBM)
SEM = pl.BlockSpec(memory_space=pltpu.SEMAPHORE)
DATAFLOW = pltpu.SideEffectType.DATAFLOW_SIDE_EFFECTING


def _copies_start(name, srcs, lands, plan, n_copies, after, peers, barrier_id):
    ns, nb, na = len(srcs), len(srcs) + len(lands), len(after)

    def body(*refs):
        src_refs, land_refs = refs[:ns], refs[ns:nb]
        send_sems, recv_sems = refs[nb + na:nb + na + 2]
        token = refs[-1]
        _handshake(peers())
        for k, (src, there, _, to) in enumerate(plan(src_refs, land_refs)):
            _remote(src, there, send_sems, recv_sems, k, to).start()
        token[...] = jnp.zeros_like(token)

    hbm = lambda a: pltpu.HBM(a.shape, a.dtype)
    outs = pl.pallas_call(
        body, name=name,
        out_shape=(pltpu.SemaphoreType.DMA((n_copies,)), pltpu.SemaphoreType.DMA((n_copies,)),
                   *[hbm(a) for a in srcs], *[hbm(a) for a in lands], jax.ShapeDtypeStruct((8, 128), F32)),
        in_specs=[HBM] * nb + [ANY] * na,
        out_specs=(SEM, SEM, *[HBM] * nb, pl.BlockSpec(memory_space=pltpu.VMEM)),
        input_output_aliases={i: 2 + i for i in range(nb)},
        compiler_params=pltpu.CompilerParams(has_side_effects=DATAFLOW, collective_id=barrier_id),
    )(*[pltpu.with_memory_space_constraint(a, pltpu.HBM) for a in (*srcs, *lands)], *after)
    return outs[0], outs[1], outs[2:2 + ns], outs[2 + ns:2 + nb], outs[-1]


def _copies_wait(name, send_sems, recv_sems, srcs, lands, plan, after):
    ns, nb, na = len(srcs), len(srcs) + len(lands), len(after)

    def body(*refs):
        src_refs, land_refs = refs[:ns], refs[ns:nb]
        send_sems, recv_sems = refs[nb:nb + 2]
        for k, (src, _, here, to) in enumerate(plan(src_refs, land_refs)):
            cp = _remote(src, here, send_sems, recv_sems, k, to)
            cp.wait_send()
            cp.wait_recv()

    hbm = lambda a: pltpu.HBM(a.shape, a.dtype)
    outs = pl.pallas_call(
        body, name=name,
        out_shape=(*[hbm(a) for a in srcs], *[hbm(a) for a in lands]),
        in_specs=[HBM] * nb + [SEM, SEM] + [ANY] * na,
        out_specs=tuple([HBM] * nb),
        input_output_aliases={i: i for i in range(nb)},
        compiler_params=pltpu.CompilerParams(has_side_effects=DATAFLOW),
    )(*srcs, *lands, send_sems, recv_sems, *after)
    return outs[:ns], outs[ns:]


def _gather_plan(halves):
    def plan(shards, lands):
        x, y, c, chips = _place()
        me = 2 * x + y
        copies = []
        for w, h in enumerate(halves):
            rows = pl.ds(c * h, h)
            for chip in chips:
                copies.append((shards[w].at[rows, :], lands[w].at[me, rows, :],
                               lands[w].at[2 * chip[0] + chip[1], rows, :], (*chip, c)))
        return copies
    return plan


def _reduce_plan(n):
    def plan(ps, lands):
        x, y, c, chips = _place()
        return [(ps[w].at[2 * chip[0] + chip[1]], lands[w].at[j], lands[w].at[j], (*chip, c))
                for w in range(n) for j, chip in enumerate(chips)]
    return plan


def _forward_plan(halves):
    def plan(_, lands):
        x, y, c, chips = _place()

        def part(w, chip, half):
            return lands[w].at[2 * chip[0] + chip[1], pl.ds(half * halves[w], halves[w]), :]

        return [(part(w, chip, c), part(w, chip, c), part(w, chip, 1 - c), (x, y, 1 - c))
                for w in range(len(halves)) for chip in chips]
    return plan


def _pair_plan(halves):
    def plan(gs, gots):
        x, y, c, _ = _place()
        return [(gs[w].at[:, pl.ds((1 - c) * h, h), :], gots[w], gots[w], (x, y, 1 - c)) for w, h in enumerate(halves)]
    return plan


def _place_own(gathered, shards):
    chip = 2 * lax.axis_index("x") + lax.axis_index("y")
    return [lax.dynamic_update_slice(o, s[None], (chip, 0, 0)) for o, s in zip(gathered, shards)]


def _adamw(w, g, m, v, name, tr=None):
    R, C = w.shape
    tr = tr or R // 4

    def body(w_ref, g_ref, m_ref, v_ref, d_ref, nm_ref, nv_ref):
        d_ref[...], nm_ref[...], nv_ref[...] = _adamw_math(w_ref[...], g_ref[...], m_ref[...], v_ref[...])

    blk = pl.BlockSpec((tr, C), lambda i: (i, 0))
    return pl.pallas_call(body, name=name, grid=(R // tr,), in_specs=[blk] * 4, out_specs=[blk] * 3,
                          out_shape=[jax.ShapeDtypeStruct((R, C), F32)] * 3, compiler_params=_cp("arbitrary"))(w, g, m, v)


SMALL = (("norm1_g", 1, 1024), ("attn_norm_g", 1, 512), ("hgrn_norm_g", 1, 512), ("hgrn_lb_logits", 2, 512),
         ("norm2_g", 1, 1024), ("conv_b", 1, D_FF), ("final_norm_g", 1, 1024), ("conv_w", 3, D_FF))
LOSS_ROW = sum(r * c for _, r, c in SMALL) // 128
SMALL_ROWS = 136


def _rows_to_lanes(ref, row, width):
    return jnp.concatenate([ref[row + j:row + j + 1, :] for j in range(width // 128)], axis=1)


def _pack_small(grads, dlb, lb, loss):
    def body(*refs):
        parts, dlb_ref, lb_ref, loss_ref, out = refs[:len(SMALL) - 1], refs[-4], refs[-3], refs[-2], refs[-1]
        out[...] = jnp.zeros_like(out)
        lbv = lb_ref[...]
        dl = dlb_ref[...] * lbv * (1.0 - lbv)
        row = 0
        parts = list(parts)
        for name, rows, width in SMALL:
            for r in range(rows):
                if name == "hgrn_lb_logits":
                    src = dl if r == 0 else -dl
                    for j in range(width // 128):
                        out[row + j:row + j + 1, :] = src[:, 128 * j:128 * (j + 1)]
                else:
                    for j in range(width // 128):
                        out[row + j:row + j + 1, :] = parts[0][r:r + 1, 128 * j:128 * (j + 1)]
                row += width // 128
            if name != "hgrn_lb_logits":
                parts.pop(0)
        out[LOSS_ROW:LOSS_ROW + 1, :] = loss_ref[...]

    vm = pl.BlockSpec(memory_space=pltpu.VMEM)
    return pl.pallas_call(body, name="pack_small", in_specs=[vm] * (len(grads) + 3), out_specs=vm,
                          out_shape=jax.ShapeDtypeStruct((SMALL_ROWS, 128), F32))(*grads, dlb, lb, loss)


def _adamw_math(w, g, m, v):
    nm = ADAM_B1 * m + (1.0 - ADAM_B1) * g
    nv = ADAM_B2 * v + (1.0 - ADAM_B2) * (g * g)
    m_hat = nm / (1.0 - ADAM_B1 ** ADAM_STEP)
    v_hat = nv / (1.0 - ADAM_B2 ** ADAM_STEP)
    return -ADAM_LR * (m_hat / (jnp.sqrt(v_hat) + ADAM_EPS) + ADAM_WD * w), nm, nv


def _small_update(summed, g_conv_w, ws, ms, vs):
    n = len(SMALL)

    def body(*refs):
        s_ref, gcw_ref = refs[:2]
        w_refs, m_refs, v_refs = refs[2:2 + n], refs[2 + n:2 + 2 * n], refs[2 + 2 * n:2 + 3 * n]
        outs = refs[2 + 3 * n:]
        row = 0
        for k, (name, rows, width) in enumerate(SMALL):
            if name == "conv_w":
                g = gcw_ref[...]
            else:
                g = jnp.concatenate([_rows_to_lanes(s_ref, row + r * (width // 128), width) for r in range(rows)], axis=0)
            row += rows * (width // 128)
            d, nm, nv = _adamw_math(w_refs[k][...], g, m_refs[k][...], v_refs[k][...])
            for o, val in zip(outs[4 * k:4 * k + 4], (g, d, nm, nv)):
                o[...] = val

    vm = pl.BlockSpec(memory_space=pltpu.VMEM)
    outs = pl.pallas_call(
        body, name="small_update", in_specs=[vm] * (2 + 3 * n), out_specs=[vm] * (4 * n),
        out_shape=[jax.ShapeDtypeStruct(a.shape, F32) for a in ws for _ in range(4)],
    )(summed, g_conv_w, *ws, *ms, *vs)
    return [outs[4 * k:4 * k + 4] for k in range(n)]


def kernel(x, norm1_g, w_in, attn_norm_g, hgrn_norm_g, hgrn_lb_logits, w_out, norm2_g, w_up, conv_w, conv_b, w_down, final_norm_g, loss_target, m_norm1_g, m_w_in, m_attn_norm_g, m_hgrn_norm_g, m_hgrn_lb_logits, m_w_out, m_norm2_g, m_w_up, m_conv_w, m_conv_b, m_w_down, m_final_norm_g, v_norm1_g, v_w_in, v_attn_norm_g, v_hgrn_norm_g, v_hgrn_lb_logits, v_w_out, v_norm2_g, v_w_up, v_conv_w, v_conv_b, v_w_down, v_final_norm_g):
    w = dict(norm1_g=norm1_g, w_in=w_in, attn_norm_g=attn_norm_g, hgrn_norm_g=hgrn_norm_g,
             hgrn_lb_logits=hgrn_lb_logits, w_out=w_out, norm2_g=norm2_g, w_up=w_up, conv_w=conv_w, conv_b=conv_b,
             w_down=w_down, final_norm_g=final_norm_g)
    m = dict(norm1_g=m_norm1_g, w_in=m_w_in, attn_norm_g=m_attn_norm_g, hgrn_norm_g=m_hgrn_norm_g,
             hgrn_lb_logits=m_hgrn_lb_logits, w_out=m_w_out, norm2_g=m_norm2_g, w_up=m_w_up, conv_w=m_conv_w,
             conv_b=m_conv_b, w_down=m_w_down, final_norm_g=m_final_norm_g)
    v = dict(norm1_g=v_norm1_g, w_in=v_w_in, attn_norm_g=v_attn_norm_g, hgrn_norm_g=v_hgrn_norm_g,
             hgrn_lb_logits=v_hgrn_lb_logits, w_out=v_w_out, norm2_g=v_norm2_g, w_up=v_w_up, conv_w=v_conv_w,
             conv_b=v_conv_b, w_down=v_w_down, final_norm_g=v_final_norm_g)
    names = list(w)
    chip = 2 * lax.axis_index("x") + lax.axis_index("y")

    shards = {k: w[k][0].astype(BF16) for k in BIG}
    w_in4, conv_w4 = _gather_weights([shards["w_in"]], conv_w[0])
    conv_w_full = jnp.transpose(conv_w4, (1, 0, 2)).reshape(3, D_FF)
    lb = jax.nn.softmax(hgrn_lb_logits, axis=0)[0:1]
    late = [shards[k] for k in BIG[1:]]
    gather_plan = _gather_plan([s.shape[0] // 2 for s in late])
    started = _copies_start("gather_start", late, [lax.empty((N_CHIPS,) + s.shape, BF16) for s in late], gather_plan,
                            3 * len(late), after=(w_in4,), peers=_chip_peers, barrier_id=0)
    u1, qkv, hg = _in_proj(x[0], norm1_g + started[4][0:1, 0:1], w_in4)
    attn_o, lse = _attn_fwd(qkv)
    late, landed_w = _copies_wait("gather_wait", *started[:4], gather_plan, after=(attn_o,))
    forward_plan = _forward_plan([s.shape[0] // 2 for s in late])
    started = _copies_start("forward_start", [], landed_w, forward_plan, 3 * len(late), after=(),
                            peers=_sibling_peer, barrier_id=1)
    rec_o, states = _hgrn_fwd(hg, lb + started[4][0:1, 0:1])
    a = dict(u1=u1, qkv=qkv, hg=hg, attn_o=attn_o, lse=lse, rec_o=rec_o, states=states)
    w_out4, w_up4, w_down4 = _place_own(
        _copies_wait("forward_wait", *started[:4], forward_plan, after=(rec_o,))[1], late)

    b = _step_channel(a, x[0], loss_target[0], attn_norm_g, hgrn_norm_g, w_out4.reshape(D_MODEL, D_MODEL), norm2_g,
                      w_up4, conv_w_full, conv_b, w_down4.reshape(D_FF, D_MODEL), final_norm_g.reshape(1, D_MODEL))

    EARLY, LATE = ("w_up", "w_down"), ("w_in", "w_out")
    early = [b["dw_up"], b["dw_down"]]
    pair_plan = _pair_plan([gk.shape[1] // 2 for gk in early])
    started = _copies_start("pair_start", early,
                            [lax.empty((N_CHIPS, gk.shape[1] // 2, gk.shape[2]), BF16) for gk in early], pair_plan,
                            len(early), after=(), peers=_sibling_peer, barrier_id=2)
    dqkv = _attn_bwd(qkv, attn_o, lse, b["da"], started[4])
    early, gots = _copies_wait("pair_wait", *started[:4], pair_plan, after=(dqkv[0],))
    ps = [_pair_sum(gk, got, f"pair_sum_{k}") for gk, got, k in zip(early, gots, EARLY)]
    reduce_plan = _reduce_plan(len(ps))
    started = _copies_start("reduce_start", ps, [lax.empty((3,) + p.shape[1:], BF16) for p in ps], reduce_plan,
                            3 * len(ps), after=(), peers=_chip_peers, barrier_id=3)
    c = _step_mixers_bwd(a, b, x[0], norm1_g, w_in4, lb + started[4][0:1, 0:1], dqkv)
    late = [c["dw_in"], c["dw_out"]]
    gots_late = _pair_exchange(late, "pair_exchange_late", barrier_id=4)
    ps_late = [_pair_sum(gk, got, f"pair_sum_{k}") for gk, got, k in zip(late, gots_late, LATE)]
    plan_late = _reduce_plan(len(late))
    started_late = _copies_start("reduce_start_late", ps_late, [lax.empty((3,) + p.shape[1:], BF16) for p in ps_late],
                                 plan_late, 3 * len(late), after=(), peers=_chip_peers, barrier_id=5)
    landed = _copies_wait("reduce_wait", *started[:4], reduce_plan, after=(started_late[4],))[1]
    reds = [_sum_partials(gk, got, l, f"sum_partials_{k}") for gk, got, l, k in zip(early, gots, landed, EARLY)]
    g = dict(zip(EARLY, _pair_share(reds, "pair_share", barrier_id=6)))
    delta, new_m, new_v = {}, {}, {}
    for k in EARLY:
        delta[k], new_m[k], new_v[k] = _adamw(w[k][0], g[k], m[k][0], v[k][0], f"adamw_{k}")

    loss, dx = b["loss"], c["dx"]
    small = dict(g1=c["dg1"], g_a=b["dga"], g_h=b["dgh"], lb=c["dlb"], g2=b["dg2"], conv_w=b["dcw"], conv_b=b["dcb"],
                 gf=b["dgf"])
    summed = _allreduce_small(_pack_small(
        [small["g1"], small["g_a"], small["g_h"], small["g2"], small["conv_b"], small["gf"], small["conv_w"]],
        small["lb"], lb, loss))
    loss_total = summed[LOSS_ROW, 0]
    g_conv_w = lax.dynamic_slice(summed[LOSS_ROW - 3 * D_FF // 128:LOSS_ROW].reshape(3, D_FF),
                                 (0, chip * (D_FF // N_CHIPS)), (3, D_FF // N_CHIPS))
    two_d = lambda p, k: p[k].reshape(-1, p[k].shape[-1])
    updated = _small_update(summed, g_conv_w, *[[two_d(p, k) for k, _, _ in SMALL] for p in (w, m, v)])
    for (k, _, _), parts in zip(SMALL, updated):
        g[k], delta[k], new_m[k], new_v[k] = (a.reshape(w[k].shape) for a in parts)

    landed_late = _copies_wait("reduce_wait_late", *started_late[:4], plan_late,
                               after=(updated[0][1], delta["w_up"]))[1]
    reds = [_sum_partials(gk, got, l, f"sum_partials_{k}") for gk, got, l, k in zip(late, gots_late, landed_late, LATE)]
    g.update(zip(LATE, _pair_share(reds, "pair_share_late", barrier_id=7)))
    for k in LATE:
        delta[k], new_m[k], new_v[k] = _adamw(w[k][0], g[k], m[k][0], v[k][0], f"adamw_{k}")
    for k in BIG:
        g[k], delta[k], new_m[k], new_v[k] = g[k][None], delta[k][None], new_m[k][None], new_v[k][None]

    return (loss_total, dx[None], *[g[k] for k in names], *[delta[k] for k in names],
            *[new_m[k] for k in names], *[new_v[k] for k in names])
```

```python
import math

import jax
import jax.numpy as jnp
from jax import lax
from jax.experimental import pallas as pl
from jax.experimental.pallas import tpu as pltpu

F32 = jnp.float32
BF16 = jnp.bfloat16

D_MODEL = 1024
ATTN_W = 512
HGRN_W = 512
HEAD_PAIR = 128
ATTN_BLK = 128
DILATIONS = (1, 4, 16)
ATTN_CHAINS = 4
ATTN_CHAINS_FWD = 8
HGRN_HEADS = 4
HGRN_DIM = 128
HGRN_CHUNK = 64
SUPER = 256
HGRN_SIDE = 4
D_FF = 2816
FF_CHUNKS = ((0, 1536), (1536, D_FF))
MLP_BWD_CHUNKS = ((0, 768), (768, 1408), (1408, 2176), (2176, D_FF))
N_CHIPS = 4
IN_TOTAL = 3584
IN_SHARD = IN_TOTAL // N_CHIPS
UP_SHARD = 2 * D_FF // N_CHIPS
QKV_W = 3 * ATTN_W
HG_W = 4 * HGRN_W
EPS = 1e-6
NEG = -1e30
V7X_VMEM_BYTES = 64 * 1024 * 1024
VMEM_LIMIT = V7X_VMEM_BYTES - 8 * 1024 * 1024

ADAM_LR = 0.001
ADAM_B1 = 0.9
ADAM_B2 = 0.999
ADAM_EPS = 1e-08
ADAM_WD = 0.01
ADAM_STEP = 10

MESH = pl.DeviceIdType.MESH


def _cp(*sem):
    return pltpu.CompilerParams(dimension_semantics=sem or None, vmem_limit_bytes=VMEM_LIMIT)


def _dot(a, b):
    return jnp.dot(a, b, preferred_element_type=F32)


def _dot_nt(a, b):
    return lax.dot_general(a, b, (((1,), (1,)), ((), ())), preferred_element_type=F32)


def _dot_tn(a, b):
    return lax.dot_general(a, b, (((0,), (0,)), ((), ())), preferred_element_type=F32)


def _sigmoid(x):
    return 1.0 / (1.0 + jnp.exp(-x))


def _rms(x, width):
    return lax.rsqrt(jnp.sum(x * x, axis=-1, keepdims=True) * (1.0 / width) + EPS)


def _rms_bwd(dn, n, r, width):
    return r * (dn - n * (jnp.sum(dn * n, axis=-1, keepdims=True) * (1.0 / width)))


def _colsum(x):
    return jnp.sum(x, axis=0, keepdims=True)


def _row(v, k):
    rid = lax.broadcasted_iota(jnp.int32, v.shape, 0)
    return jnp.sum(jnp.where(rid == k, v, 0.0), axis=0, keepdims=True)


def _full(shape):
    return pl.BlockSpec(shape, lambda *_: (0,) * len(shape))


def _once(shape):
    return pl.BlockSpec(shape, lambda *_: (0,) * len(shape), pipeline_mode=pl.Buffered(1))


def _load_side_by_side(w_hbm, w_full, sem):
    width = w_hbm.shape[2]
    cps = [pltpu.make_async_copy(w_hbm.at[k], w_full.at[:, pl.ds(k * width, width)], sem.at[k]) for k in range(N_CHIPS)]
    for cp in cps:
        cp.start()
    for cp in cps:
        cp.wait()


def _in_proj(x, g1, w_in4, tm=512):
    T = x.shape[0]

    def body(x_ref, g_ref, w_hbm, u_ref, qkv_ref, hg_ref, w_full, sem):
        @pl.when(pl.program_id(0) == 0)
        def _():
            _load_side_by_side(w_hbm, w_full, sem)

        xv = x_ref[...]
        u = (xv * _rms(xv, D_MODEL) * g_ref[...]).astype(BF16)
        u_ref[...] = u
        p = _dot(u, w_full[...])
        qkv_ref[...] = p[:, :QKV_W]
        hg_ref[...] = p[:, QKV_W:]

    return pl.pallas_call(
        body, name="in_proj", grid=(T // tm,),
        in_specs=[pl.BlockSpec((tm, D_MODEL), lambda i: (i, 0)), _full((1, D_MODEL)), ANY],
        out_specs=[pl.BlockSpec((tm, D_MODEL), lambda i: (i, 0)), pl.BlockSpec((tm, QKV_W), lambda i: (i, 0)),
                   pl.BlockSpec((tm, HG_W), lambda i: (i, 0))],
        out_shape=[jax.ShapeDtypeStruct((T, D_MODEL), BF16), jax.ShapeDtypeStruct((T, QKV_W), F32),
                   jax.ShapeDtypeStruct((T, HG_W), F32)],
        scratch_shapes=[pltpu.VMEM((D_MODEL, IN_TOTAL), BF16), pltpu.SemaphoreType.DMA((N_CHIPS,))],
        compiler_params=_cp("arbitrary"),
    )(x, g1, w_in4)


def _attn_masks(bias_ref):
    lane = lax.broadcasted_iota(jnp.int32, (ATTN_BLK, HEAD_PAIR), 1)
    row = lax.broadcasted_iota(jnp.int32, (2 * ATTN_BLK, 2 * ATTN_BLK), 0)
    col = lax.broadcasted_iota(jnp.int32, (2 * ATTN_BLK, 2 * ATTN_BLK), 1)
    base = jnp.where(row >= ATTN_BLK, row - ATTN_BLK, row) - col
    for k in range(2):
        dist = base + k * ATTN_BLK
        bias_ref[k] = jnp.where((dist >= 0) & (dist <= ATTN_BLK), 0.0, NEG)
    bias_ref[2] = jnp.where(col >= ATTN_BLK, bias_ref[1], NEG)
    return lane < 64


def _two_heads(blk, first):
    zero = jnp.zeros_like(blk)
    return jnp.concatenate([jnp.where(first, blk, zero), jnp.where(first, zero, blk)], axis=0)


def _attn_rows(idx, nb, d):
    r, n = idx // nb, idx % nb
    kb = jnp.maximum(n - 1, 0)
    if d == 1:
        q0 = pl.multiple_of(n * ATTN_BLK, ATTN_BLK)
        k0 = pl.multiple_of(kb * ATTN_BLK, ATTN_BLK)
        return pl.ds(q0, ATTN_BLK), pl.ds(k0, 2 * ATTN_BLK), n - kb
    return (pl.ds(r + d * ATTN_BLK * n, ATTN_BLK, stride=d), pl.ds(r + d * ATTN_BLK * kb, 2 * ATTN_BLK, stride=d),
            n - kb)


def _attn_fwd(qkv):
    T = qkv.shape[0]

    n_blocks = T // ATTN_BLK

    def body(q_ref, k_ref, v_ref, o_ref, m_ref, l_ref, bias_ref):
        first = _attn_masks(bias_ref)
        for bi, d in enumerate(DILATIONS):
            nb = T // d // ATTN_BLK

            chains = ATTN_CHAINS_FWD
            per_chain = n_blocks // chains
            carried = d > 1 and per_chain % nb == 0

            def block(idx, kept=None, d=d, nb=nb, bi=bi, carried=carried):
                rows, keys, which = _attn_rows(idx, nb, d)
                q2 = _two_heads(q_ref[rows, :] * 0.125, first).astype(BF16)
                if carried:
                    k_own, v_own = k_ref[rows, :].astype(BF16), v_ref[rows, :].astype(BF16)
                    kw = jnp.concatenate([kept[0], k_own], axis=0)
                    vw = jnp.concatenate([kept[1], v_own], axis=0)
                    which = 2 - which
                else:
                    kw = k_ref[keys, :].astype(BF16)
                    vw = v_ref[keys, :].astype(BF16)
                old = (o_ref[rows, :], m_ref[rows, :], l_ref[rows, :]) if bi else None
                s = _dot_nt(q2, kw) + bias_ref[which]
                mb = jnp.max(s, axis=-1, keepdims=True)
                p = jnp.exp(s - mb)
                lb = jnp.sum(p, axis=-1, keepdims=True)
                o2 = _dot(p.astype(BF16), vw)
                o = jnp.where(first, o2[:ATTN_BLK], o2[ATTN_BLK:])
                m = jnp.where(first, mb[:ATTN_BLK], mb[ATTN_BLK:])
                l = jnp.where(first, lb[:ATTN_BLK], lb[ATTN_BLK:])
                if bi:
                    po, pm, pl_ = old
                    mn = jnp.maximum(pm, m)
                    wa = jnp.exp(pm - mn)
                    wb = jnp.exp(m - mn)
                    o, l, m = po * wa + o * wb, pl_ * wa + l * wb, mn
                return (rows, o, m, l), ((k_own, v_own) if carried else 0)

            def step(i, kept, block=block, carried=carried, chains=chains, per_chain=per_chain):
                done = [block(i + ch * per_chain, kept[ch] if carried else None) for ch in range(chains)]
                for (rows, o, m, l), _ in done:
                    o_ref[rows, :] = o
                    m_ref[rows, :] = m
                    l_ref[rows, :] = l
                return tuple(k for _, k in done) if carried else kept

            zero = jnp.zeros((ATTN_BLK, HEAD_PAIR), BF16)
            lax.fori_loop(0, per_chain, step, ((zero, zero),) * chains if carried else 0)

        def finish(i, carry):
            rows = pl.ds(pl.multiple_of(i * SUPER, SUPER), SUPER)
            l = l_ref[rows, :]
            o_ref[rows, :] = o_ref[rows, :] / l
            m_ref[rows, :] = m_ref[rows, :] + jnp.log(l)
            return carry

        lax.fori_loop(0, T // SUPER, finish, 0)

    col = lambda off: pl.BlockSpec((T, HEAD_PAIR), lambda j: (0, off + j))
    return pl.pallas_call(
        body, name="attn_fwd", grid=(4,),
        in_specs=[col(0), col(4), col(8)], out_specs=[col(0), col(0)],
        out_shape=[jax.ShapeDtypeStruct((T, ATTN_W), F32)] * 2,
        scratch_shapes=[pltpu.VMEM((T, HEAD_PAIR), F32), pltpu.VMEM((3, 2 * ATTN_BLK, 2 * ATTN_BLK), F32)],
        compiler_params=_cp("arbitrary"),
    )(qkv, qkv, qkv)


def _attn_bwd(qkv, o, lse, do, token=None):
    T = qkv.shape[0]
    per_chain = T // ATTN_BLK // ATTN_CHAINS
    extra = [] if token is None else [token]

    def body(q_ref, k_ref, v_ref, o_ref, lse_ref, do_ref, *rest):
        outs = rest[len(extra):len(extra) + 3]
        dq_ref, dk_ref, dv_ref, dkb_ref, dvb_ref, bias_ref = rest[len(extra) + 3:]
        first = _attn_masks(bias_ref)
        dq_ref[...] = jnp.zeros_like(dq_ref)
        dk_ref[...] = jnp.zeros_like(dk_ref)
        dv_ref[...] = jnp.zeros_like(dv_ref)

        def grads(rows, kw, vw, which):
            q2 = _two_heads(q_ref[rows, :] * 0.125, first).astype(BF16)
            lse_b = lse_ref[rows, :]
            dob = do_ref[rows, :]
            prod = dob * o_ref[rows, :]
            old = dq_ref[rows, :]
            lse2 = jnp.concatenate(
                [jnp.max(jnp.where(first, lse_b, NEG), axis=-1, keepdims=True),
                 jnp.max(jnp.where(first, NEG, lse_b), axis=-1, keepdims=True)], axis=0)
            p = jnp.exp(_dot_nt(q2, kw) + (bias_ref[which] - lse2))
            delta = jnp.concatenate(
                [jnp.sum(jnp.where(first, prod, 0.0), axis=-1, keepdims=True),
                 jnp.sum(jnp.where(first, 0.0, prod), axis=-1, keepdims=True)], axis=0)
            do2 = _two_heads(dob, first).astype(BF16)
            ds = (p * (_dot_nt(do2, vw) - delta)).astype(BF16)
            dq2 = _dot(ds, kw) * 0.125
            return (old + jnp.where(first, dq2[:ATTN_BLK], dq2[ATTN_BLK:]), _dot_tn(ds, q2),
                    _dot_tn(p.astype(BF16), do2))

        def block(idx):
            rows, keys, which = _attn_rows(idx, T // ATTN_BLK, 1)
            old = dk_ref[keys, :], dv_ref[keys, :]
            dq, ck, cv = grads(rows, k_ref[keys, :].astype(BF16), v_ref[keys, :].astype(BF16), which)
            return rows, keys, dq, old[0] + ck, old[1] + cv

        def step(i, carry):
            done = [block(i + ch * per_chain) for ch in range(ATTN_CHAINS)]
            for rows, keys, dq, dk, dv in done:
                dq_ref[rows, :] = dq
                dk_ref[keys, :] = dk
                dv_ref[keys, :] = dv
            return carry

        lax.fori_loop(0, per_chain, step, 0)

        for d in DILATIONS[1:]:
            nb = T // d // ATTN_BLK

            def block(idx, kept, d=d, nb=nb):
                r, n = idx // nb, idx % nb
                rows = pl.ds(r + d * ATTN_BLK * n, ATTN_BLK, stride=d)
                before = pl.ds(r + d * ATTN_BLK * jnp.maximum(n - 1, 0), ATTN_BLK, stride=d)
                k_prev, v_prev, dk_prev, dv_prev = kept
                k_own, v_own = k_ref[rows, :].astype(BF16), v_ref[rows, :].astype(BF16)
                dq, ck, cv = grads(rows, jnp.concatenate([k_prev, k_own], axis=0),
                                   jnp.concatenate([v_prev, v_own], axis=0), jnp.where(n > 0, 1, 2))
                stores = (rows, before, dq, dk_prev + ck[:ATTN_BLK], dv_prev + cv[:ATTN_BLK], ck[ATTN_BLK:], cv[ATTN_BLK:])
                return stores, (k_own, v_own, ck[ATTN_BLK:], cv[ATTN_BLK:])

            def step(i, kept, block=block):
                done = [block(i + ch * per_chain, kept[ch]) for ch in range(ATTN_CHAINS)]
                for (rows, before, dq, dk_done, dv_done, dk_own, dv_own), _ in done:
                    dq_ref[rows, :] = dq
                    dkb_ref[before, :] = dk_done
                    dvb_ref[before, :] = dv_done
                    dkb_ref[rows, :] = dk_own
                    dvb_ref[rows, :] = dv_own
                return tuple(k for _, k in done)

            zero = jnp.zeros((ATTN_BLK, HEAD_PAIR), F32)
            lax.fori_loop(0, per_chain, step, ((zero.astype(BF16), zero.astype(BF16), zero, zero),) * ATTN_CHAINS)

            def add(i, carry):
                rows = pl.ds(pl.multiple_of(i * SUPER, SUPER), SUPER)
                dk_ref[rows, :] += dkb_ref[rows, :]
                dv_ref[rows, :] += dvb_ref[rows, :]
                return carry

            lax.fori_loop(0, T // SUPER, add, 0)

        def emit(i, carry):
            rows = pl.ds(pl.multiple_of(i * SUPER, SUPER), SUPER)
            for out, acc in zip(outs, (dq_ref, dk_ref, dv_ref)):
                out[rows, :] = acc[rows, :].astype(BF16)
            return carry

        lax.fori_loop(0, T // SUPER, emit, 0)

    col = lambda off: pl.BlockSpec((T, HEAD_PAIR), lambda j: (0, off + j))
    return pl.pallas_call(
        body, name="attn_bwd", grid=(4,),
        in_specs=[col(0), col(4), col(8), col(0), col(0), col(0)] + [_full(t.shape) for t in extra],
        out_specs=[col(0)] * 3,
        out_shape=[jax.ShapeDtypeStruct((T, ATTN_W), BF16)] * 3,
        scratch_shapes=[pltpu.VMEM((T, HEAD_PAIR), F32)] * 5 + [pltpu.VMEM((3, 2 * ATTN_BLK, 2 * ATTN_BLK), F32)],
        compiler_params=_cp("arbitrary"),
    )(qkv, qkv, qkv, o, lse, do, *extra)


def _chunk_ids():
    row = lax.broadcasted_iota(jnp.int32, (SUPER, HGRN_DIM), 0)
    r2 = lax.broadcasted_iota(jnp.int32, (SUPER, SUPER), 0)
    c2 = lax.broadcasted_iota(jnp.int32, (SUPER, SUPER), 1)
    amask = ((r2 // HGRN_CHUNK) == (c2 // HGRN_CHUNK)) & (c2 <= r2)
    return row % HGRN_CHUNK, row // HGRN_CHUNK, amask


def _cumsum_chunk(x, rmod):
    s = 1
    while s < HGRN_CHUNK:
        x = x + jnp.where(rmod >= s, pltpu.roll(x, s, 0), 0.0)
        s *= 2
    return x


def _suffix_sum_chunk(x, rmod):
    s = 1
    while s < HGRN_CHUNK:
        x = x + jnp.where(rmod < HGRN_CHUNK - s, pltpu.roll(x, SUPER - s, 0), 0.0)
        s *= 2
    return x


def _chunk_rows(vs, cid):
    out = vs[-1]
    for c in reversed(range(len(vs) - 1)):
        out = jnp.where(cid == c, vs[c], out)
    return out


def _expand(x, cid):
    return jnp.concatenate([jnp.where(cid == c, x, 0.0) for c in range(SUPER // HGRN_CHUNK)], axis=1)


def _hgrn_gates(q, f, lbv, rmod, cid, tmp):
    sq = _sigmoid(q)
    sg = _sigmoid(f)
    forget = lbv + (1.0 - lbv) * sg
    key = 1.0 - forget
    b = _cumsum_chunk(jnp.log(forget), rmod)
    tmp[...] = b
    bends = [tmp[c * HGRN_CHUNK + HGRN_CHUNK - 1:(c + 1) * HGRN_CHUNK, :] for c in range(SUPER // HGRN_CHUNK)]
    eb = jnp.exp(b)
    enb = jnp.exp(-b)
    ebe = jnp.exp(_chunk_rows(bends, cid) - b)
    return sq, sg, forget, key, eb, enb, ebe, q * sq * eb, key * enb, key * ebe, [jnp.exp(v) for v in bends]


def _hgrn_fwd(hg, lb):
    T = hg.shape[0]
    nsc = T // SUPER
    NC = SUPER // HGRN_CHUNK

    def body(q_ref, f_ref, i_ref, lb_ref, o_ref, st_ref, state, tmp):
        rmod, cid, amask = _chunk_ids()
        state[...] = jnp.zeros_like(state)
        lbv = lb_ref[...]

        def local(sc, u):
            rows = pl.ds(pl.multiple_of(sc * SUPER, SUPER), SUPER)
            iv = i_ref[rows, :].astype(BF16)
            qd, ki, ke, dec = _hgrn_gates(q_ref[rows, :], f_ref[rows, :], lbv, rmod, cid, tmp.at[u])[-4:]
            a = jnp.where(amask, _dot_nt(qd.astype(BF16), ki.astype(BF16)), 0.0)
            return rows, qd, dec, _dot(a.astype(BF16), iv), _dot_tn(iv, _expand(ke, cid).astype(BF16))

        def step(i, carry):
            parts = [local(i * HGRN_SIDE + u, u) for u in range(HGRN_SIDE)]
            st = state[...]
            entering = []
            for u, (_, _, dec, _, ut) in enumerate(parts):
                st_ref[0, i * HGRN_SIDE + u] = st
                sts = []
                for c in range(NC):
                    sts.append(st)
                    st = st * dec[c] + ut[:, c * HGRN_DIM:(c + 1) * HGRN_DIM]
                entering.append(jnp.concatenate(sts, axis=1).astype(BF16))
            state[...] = st
            for (rows, qd, _, o, _), sts in zip(parts, entering):
                o_ref[rows, :] = o + _dot_nt(_expand(qd, cid).astype(BF16), sts)
            return carry

        lax.fori_loop(0, nsc // HGRN_SIDE, step, 0)

    col = lambda off: pl.BlockSpec((T, HGRN_DIM), lambda h: (0, off + h))
    return pl.pallas_call(
        body, name="hgrn_fwd", grid=(HGRN_HEADS,),
        in_specs=[col(0), col(4), col(8), pl.BlockSpec((1, HGRN_DIM), lambda h: (0, h))],
        out_specs=[pl.BlockSpec((T, HGRN_DIM), lambda h: (0, h)),
                   pl.BlockSpec((1, nsc, HGRN_DIM, HGRN_DIM), lambda h: (h, 0, 0, 0))],
        out_shape=[jax.ShapeDtypeStruct((T, HGRN_W), F32),
                   jax.ShapeDtypeStruct((HGRN_HEADS, nsc, HGRN_DIM, HGRN_DIM), F32)],
        scratch_shapes=[pltpu.VMEM((HGRN_DIM, HGRN_DIM), F32), pltpu.VMEM((HGRN_SIDE, SUPER, HGRN_DIM), F32)],
        compiler_params=_cp("arbitrary"),
    )(hg, hg, hg, lb)


def _hgrn_bwd(hg, lb, states, do):
    T = hg.shape[0]
    nsc = T // SUPER
    NC = SUPER // HGRN_CHUNK

    def body(q_ref, f_ref, i_ref, lb_ref, st_ref, do_ref, dq_ref, df_ref, di_ref, dlb_ref, dstate, tmp):
        rmod, cid, amask = _chunk_ids()
        dstate[...] = jnp.zeros_like(dstate)
        dlb_ref[...] = jnp.zeros_like(dlb_ref)
        lbv = lb_ref[...]

        def local(sc, u):
            rows = pl.ds(pl.multiple_of(sc * SUPER, SUPER), SUPER)
            q = q_ref[rows, :]
            ivf = i_ref[rows, :]
            iv = ivf.astype(BF16)
            dof = do_ref[rows, :]
            dob = dof.astype(BF16)
            sq, sg, forget, key, eb, enb, ebe, qd, ki, ke, dec = _hgrn_gates(q, f_ref[rows, :], lbv, rmod, cid,
                                                                            tmp.at[u])
            qdb, kib = qd.astype(BF16), ki.astype(BF16)
            keexp = _expand(ke, cid).astype(BF16)
            a = jnp.where(amask, _dot_nt(qdb, kib), 0.0).astype(BF16)
            ut = _dot_tn(iv, keexp)
            st = st_ref[0, sc]
            sts = []
            for c in range(NC):
                sts.append(st)
                st = st * dec[c] + ut[:, c * HGRN_DIM:(c + 1) * HGRN_DIM]
            gt = _dot_tn(dob, _expand(qd, cid).astype(BF16))
            da = jnp.where(amask, _dot_nt(dob, iv), 0.0).astype(BF16)
            ststack = jnp.concatenate(sts, axis=0).astype(BF16)
            return dict(rows=rows, q=q, sq=sq, sg=sg, forget=forget, eb=eb, enb=enb, ebe=ebe, qd=qd, ki=ki, ke=ke,
                        dec=dec, sts=sts, gt=gt, keexp=keexp, ivexp=_expand(ivf, cid).astype(BF16),
                        div=_dot_tn(a, dob), dki=_dot_tn(da, qdb),
                        dqd=_dot(da, kib) + _dot(_expand(dof, cid).astype(BF16), ststack))

        def finish(p, nxt, ddec):
            ncat = jnp.concatenate(nxt, axis=1).astype(BF16)
            nstack = jnp.concatenate(nxt, axis=0).astype(BF16)
            dke = _dot(p["ivexp"], nstack)
            dkk = dke * p["ke"]
            dkey = p["dki"] * p["enb"] + dke * p["ebe"]
            db = p["dqd"] * p["qd"] - p["dki"] * p["ki"] - dkk
            dbends = [_colsum(jnp.where(cid == c, dkk, 0.0)) + ddec[c] * p["dec"][c] for c in range(NC)]
            dforget = (_suffix_sum_chunk(db, rmod) + _chunk_rows(dbends, cid)) / p["forget"] - dkey
            sg, sq, q = p["sg"], p["sq"], p["q"]
            df_ref[p["rows"], :] = (dforget * (1.0 - lbv) * sg * (1.0 - sg)).astype(BF16)
            dq_ref[p["rows"], :] = (p["dqd"] * p["eb"] * (sq * (1.0 + q * (1.0 - sq)))).astype(BF16)
            di_ref[p["rows"], :] = (p["div"] + _dot_nt(p["keexp"], ncat)).astype(BF16)
            return _colsum(dforget * (1.0 - sg))

        def step(i, carry):
            parts = [local(nsc - 1 - (i * HGRN_SIDE + u), u) for u in range(HGRN_SIDE)]
            dst = dstate[...]
            chained = []
            for p in parts:
                nxt = [None] * NC
                ddec = [None] * NC
                for c in reversed(range(NC)):
                    nxt[c] = dst
                    ddec[c] = _colsum(dst * p["sts"][c])
                    dst = dst * p["dec"][c] + p["gt"][:, c * HGRN_DIM:(c + 1) * HGRN_DIM]
                chained.append((nxt, ddec))
            dstate[...] = dst
            dlb = dlb_ref[...]
            for p, (nxt, ddec) in zip(parts, chained):
                dlb = dlb + finish(p, nxt, ddec)
            dlb_ref[...] = dlb
            return carry

        lax.fori_loop(0, nsc // HGRN_SIDE, step, 0)

    col = lambda off: pl.BlockSpec((T, HGRN_DIM), lambda h: (0, off + h))
    own = pl.BlockSpec((T, HGRN_DIM), lambda h: (0, h))
    vec = pl.BlockSpec((1, HGRN_DIM), lambda h: (0, h))
    return pl.pallas_call(
        body, name="hgrn_bwd", grid=(HGRN_HEADS,),
        in_specs=[col(0), col(4), col(8), vec,
                  pl.BlockSpec((1, nsc, HGRN_DIM, HGRN_DIM), lambda h: (h, 0, 0, 0)), own],
        out_specs=[own, own, own, vec],
        out_shape=[jax.ShapeDtypeStruct((T, HGRN_W), BF16)] * 3 + [jax.ShapeDtypeStruct((1, HGRN_W), F32)],
        scratch_shapes=[pltpu.VMEM((HGRN_DIM, HGRN_DIM), F32), pltpu.VMEM((HGRN_SIDE, SUPER, HGRN_DIM), F32)],
        compiler_params=_cp("arbitrary"),
    )(hg, hg, hg, lb, states, do)


def _rec_heads(rec, gate, g_h):
    rr = jnp.concatenate(
        [jnp.broadcast_to(_rms(rec[:, h * HGRN_DIM:(h + 1) * HGRN_DIM], HGRN_DIM), (rec.shape[0], HGRN_DIM))
         for h in range(HGRN_HEADS)], axis=1)
    rn = rec * rr
    sg = _sigmoid(gate)
    return rr, rn, sg


def _mix_out(attn_o, rec_o, hg, x, g_a, g_h, w_out, tm=512):
    T = x.shape[0]

    def body(a_ref, r_ref, gt_ref, x_ref, ga_ref, gh_ref, w_ref, h1_ref, mixed_ref):
        a = a_ref[...]
        an = a * _rms(a, ATTN_W) * ga_ref[...]
        gate = gt_ref[...]
        _, rn, sg = _rec_heads(r_ref[...], gate, gh_ref[...])
        mixed = jnp.concatenate([an, rn * gh_ref[...] * (gate * sg)], axis=1).astype(BF16)
        mixed_ref[...] = mixed
        h1_ref[...] = x_ref[...] + _dot(mixed, w_ref[...])

    row = lambda w: pl.BlockSpec((tm, w), lambda i: (i, 0))
    return pl.pallas_call(
        body, name="mix_out", grid=(T // tm,),
        in_specs=[row(ATTN_W), row(HGRN_W), pl.BlockSpec((tm, HGRN_W), lambda i: (i, 3)), row(D_MODEL),
                  _full((1, ATTN_W)), _full((1, HGRN_W)), _once((D_MODEL, D_MODEL))],
        out_specs=[row(D_MODEL), row(D_MODEL)],
        out_shape=[jax.ShapeDtypeStruct((T, D_MODEL), F32), jax.ShapeDtypeStruct((T, D_MODEL), BF16)],
        compiler_params=_cp("arbitrary"),
    )(attn_o, rec_o, hg, x, g_a, g_h, w_out)


_INV_SQRT2 = 1.0 / math.sqrt(2.0)
_INV_SQRT2PI = 1.0 / math.sqrt(2.0 * math.pi)


def _gelu(x):
    return 0.5 * x * (1.0 + lax.erf(x * _INV_SQRT2))


def _gelu_and_grad(x):
    z = x * _INV_SQRT2
    cdf = 0.5 * (1.0 + lax.erf(z))
    return x * cdf, cdf + (x * _INV_SQRT2PI) * jnp.exp(-(z * z))


def _shift_down(g, prev, rowid):
    p1 = _row(prev, prev.shape[0] - 1)
    p2 = _row(prev, prev.shape[0] - 2)
    s1 = jnp.where(rowid == 0, p1, pltpu.roll(g, 1, 0))
    s2 = jnp.where(rowid == 0, p2, jnp.where(rowid == 1, p1, pltpu.roll(g, 2, 0)))
    return s1, s2


def _mlp_fwd(h1, g2, w_up4, conv_w, conv_b, w_down, gf, tgt, tm=256):
    T = h1.shape[0]

    def body(h_ref, g2_ref, wu_hbm, cw_ref, cb_ref, wd_ref, gf_ref, t_ref,
             u_ref, gate_ref, val_ref, conv_ref, act_ref, dh_ref, loss_ref, dgf_ref, carry, wu_ref, sem):
        i = pl.program_id(0)

        @pl.when(i == 0)
        def _():
            carry[...] = jnp.zeros_like(carry)
            loss_ref[...] = jnp.zeros_like(loss_ref)
            dgf_ref[...] = jnp.zeros_like(dgf_ref)
            _load_side_by_side(wu_hbm, wu_ref, sem)

        h = h_ref[...]
        u = (h * _rms(h, D_MODEL) * g2_ref[...]).astype(BF16)
        u_ref[...] = u
        y2 = jnp.zeros((tm, D_MODEL), F32)
        for lo, hi in FF_CHUNKS:
            cols = slice(lo, hi)
            rowid = lax.broadcasted_iota(jnp.int32, (tm, hi - lo), 0)
            gb = _dot(u, wu_ref[:, lo:hi]).astype(BF16)
            vb = _dot(u, wu_ref[:, D_FF + lo:D_FF + hi]).astype(BF16)
            gate_ref[:, cols] = gb
            val_ref[:, cols] = vb
            g = gb.astype(F32)
            s1, s2 = _shift_down(g, carry[:, cols], rowid)
            carry[:, cols] = g[tm - 8:, :]
            conv = cb_ref[:, cols] + cw_ref[0:1, cols] * s2 + cw_ref[1:2, cols] * s1 + cw_ref[2:3, cols] * g
            act = (_gelu(conv) * vb.astype(F32)).astype(BF16)
            conv_ref[:, cols] = conv.astype(BF16)
            act_ref[:, cols] = act
            y2 = y2 + _dot(act, wd_ref[cols, :])
        h2 = h + y2
        rf = _rms(h2, D_MODEL)
        n = h2 * rf
        gfv = gf_ref[...]
        e = n * gfv - t_ref[...]
        loss_ref[...] += jnp.sum(e * e) * (0.5 / D_MODEL)
        dy = e * (1.0 / D_MODEL)
        dgf_ref[...] += _colsum(dy * n)
        dh_ref[...] = _rms_bwd(dy * gfv, n, rf, D_MODEL)

    row = lambda w: pl.BlockSpec((tm, w), lambda i: (i, 0))
    return pl.pallas_call(
        body, name="mlp_fwd", grid=(T // tm,),
        in_specs=[row(D_MODEL), _full((1, D_MODEL)), ANY, _full((3, D_FF)),
                  _full((1, D_FF)), _once((D_FF, D_MODEL)), _full((1, D_MODEL)), row(D_MODEL)],
        out_specs=[row(D_MODEL), row(D_FF), row(D_FF), row(D_FF), row(D_FF), row(D_MODEL), _full((1, 128)),
                   _full((1, D_MODEL))],
        out_shape=[jax.ShapeDtypeStruct((T, D_MODEL), BF16)] + [jax.ShapeDtypeStruct((T, D_FF), BF16)] * 4
        + [jax.ShapeDtypeStruct((T, D_MODEL), F32),
                   jax.ShapeDtypeStruct((1, 128), F32), jax.ShapeDtypeStruct((1, D_MODEL), F32)],
        scratch_shapes=[pltpu.VMEM((8, D_FF), F32), pltpu.VMEM((D_MODEL, 2 * D_FF), BF16),
                        pltpu.SemaphoreType.DMA((N_CHIPS,))],
        compiler_params=_cp("arbitrary"),
    )(h1, g2, w_up4, conv_w, conv_b, w_down, gf, tgt)


def _mlp_bwd(dh2, gate, val, conv, act, conv_w, w_down, tm=256):
    T = dh2.shape[0]
    nb = T // tm

    def body(dh_ref, gate_ref, val_ref, conv_ref, act_ref, cw_ref, wd_ref, dgv_ref, dcw_ref, dcb_ref, dwd_ref,
             carry, acc):
        i = pl.program_id(0)

        @pl.when(i == 0)
        def _():
            carry[...] = jnp.zeros_like(carry)
            dcw_ref[...] = jnp.zeros_like(dcw_ref)
            dcb_ref[...] = jnp.zeros_like(dcb_ref)
            acc[...] = jnp.zeros_like(acc)

        dhb = dh_ref[...].astype(BF16)
        for lo, hi in MLP_BWD_CHUNKS:
            cols = slice(lo, hi)
            rowid = lax.broadcasted_iota(jnp.int32, (tm, hi - lo), 0)
            acc[cols, :] += _dot_tn(act_ref[:, cols], dhb)
            g = gate_ref[:, cols].astype(F32)
            v = val_ref[:, cols].astype(F32)
            cv = conv_ref[:, cols].astype(F32)
            dact = _dot_nt(dhb, wd_ref[cols, :])
            gl, gp = _gelu_and_grad(cv)
            dconv = dact * v * gp
            nxt = carry[:, cols]
            n0, n1 = _row(nxt, 0), _row(nxt, 1)
            u1 = jnp.where(rowid == tm - 1, n0, pltpu.roll(dconv, tm - 1, 0))
            u2 = jnp.where(rowid == tm - 1, n1, jnp.where(rowid == tm - 2, n0, pltpu.roll(dconv, tm - 2, 0)))
            carry[:, cols] = dconv[0:8, :]
            dcb_ref[:, cols] += _colsum(dconv)
            dcw_ref[0:1, cols] += _colsum(u2 * g)
            dcw_ref[1:2, cols] += _colsum(u1 * g)
            dcw_ref[2:3, cols] += _colsum(dconv * g)
            dgate = cw_ref[2:3, cols] * dconv + cw_ref[1:2, cols] * u1 + cw_ref[0:1, cols] * u2
            dgv_ref[:, cols] = dgate.astype(BF16)
            dgv_ref[:, D_FF + lo:D_FF + hi] = (dact * gl).astype(BF16)

        @pl.when(i == nb - 1)
        def _():
            for lo, hi in MLP_BWD_CHUNKS:
                dwd_ref[lo:hi, :] = acc[lo:hi, :].astype(BF16)

    rev = lambda w: pl.BlockSpec((tm, w), lambda i: (nb - 1 - i, 0))
    return pl.pallas_call(
        body, name="mlp_bwd", grid=(nb,),
        in_specs=[rev(D_MODEL), rev(D_FF), rev(D_FF), rev(D_FF), rev(D_FF), _full((3, D_FF)), _once((D_FF, D_MODEL))],
        out_specs=[rev(2 * D_FF), _full((3, D_FF)), _full((1, D_FF)), _once((D_FF, D_MODEL))],
        out_shape=[jax.ShapeDtypeStruct((T, 2 * D_FF), BF16), jax.ShapeDtypeStruct((3, D_FF), F32),
                   jax.ShapeDtypeStruct((1, D_FF), F32), jax.ShapeDtypeStruct((D_FF, D_MODEL), BF16)],
        scratch_shapes=[pltpu.VMEM((8, D_FF), F32), pltpu.VMEM((D_FF, D_MODEL), F32)],
        compiler_params=_cp("arbitrary"),
    )(dh2, gate, val, conv, act, conv_w, w_down)


def _up_out_bwd(dgv, w_up4, h1, g2, dh2, w_out, attn_o, rec_o, hg, g_a, g_h, tm=256):
    T = h1.shape[0]

    def body(dgv_ref, wu_hbm, h_ref, g2_ref, dh2_ref, wo_ref, a_ref, r_ref, gt_ref, ga_ref, gh_ref,
             dh1_ref, dg2_ref, da_ref, dr_ref, dgt_ref, dga_ref, dgh_ref, wu_ref, sem):
        @pl.when(pl.program_id(0) == 0)
        def _():
            dg2_ref[...] = jnp.zeros_like(dg2_ref)
            dga_ref[...] = jnp.zeros_like(dga_ref)
            dgh_ref[...] = jnp.zeros_like(dgh_ref)
            _load_side_by_side(wu_hbm, wu_ref, sem)

        du = _dot_nt(dgv_ref[...], wu_ref[...])
        h = h_ref[...]
        r = _rms(h, D_MODEL)
        n = h * r
        dg2_ref[...] += _colsum(du * n)
        dh1 = dh2_ref[...] + _rms_bwd(du * g2_ref[...], n, r, D_MODEL)
        dh1_ref[...] = dh1
        dmix = _dot_nt(dh1.astype(BF16), wo_ref[...])
        dan = dmix[:, :ATTN_W]
        a = a_ref[...]
        ra = _rms(a, ATTN_W)
        na = a * ra
        dga_ref[...] += _colsum(dan * na)
        da_ref[...] = _rms_bwd(dan * ga_ref[...], na, ra, ATTN_W)
        dmr = dmix[:, ATTN_W:]
        gate = gt_ref[...]
        ghv = gh_ref[...]
        rr, rn, sg = _rec_heads(r_ref[...], gate, ghv)
        dgt_ref[...] = (dmr * rn * ghv * (sg * (1.0 + gate * (1.0 - sg)))).astype(BF16)
        drecn = dmr * (gate * sg)
        dgh_ref[...] += _colsum(drecn * rn)
        drn = drecn * ghv
        prod = drn * rn
        mean = jnp.concatenate(
            [jnp.broadcast_to(jnp.sum(prod[:, h_ * HGRN_DIM:(h_ + 1) * HGRN_DIM], axis=-1, keepdims=True),
                              (tm, HGRN_DIM)) for h_ in range(HGRN_HEADS)], axis=1) * (1.0 / HGRN_DIM)
        dr_ref[...] = rr * (drn - rn * mean)

    row = lambda w: pl.BlockSpec((tm, w), lambda i: (i, 0))
    return pl.pallas_call(
        body, name="up_out_bwd", grid=(T // tm,),
        in_specs=[row(2 * D_FF), ANY, row(D_MODEL), _full((1, D_MODEL)),
                  row(D_MODEL), _once((D_MODEL, D_MODEL)), row(ATTN_W), row(HGRN_W),
                  pl.BlockSpec((tm, HGRN_W), lambda i: (i, 3)), _full((1, ATTN_W)), _full((1, HGRN_W))],
        out_specs=[row(D_MODEL), _full((1, D_MODEL)), row(ATTN_W), row(HGRN_W), row(HGRN_W),
                   _full((1, ATTN_W)), _full((1, HGRN_W))],
        out_shape=[jax.ShapeDtypeStruct((T, D_MODEL), F32), jax.ShapeDtypeStruct((1, D_MODEL), F32),
                   jax.ShapeDtypeStruct((T, ATTN_W), F32), jax.ShapeDtypeStruct((T, HGRN_W), F32),
                   jax.ShapeDtypeStruct((T, HGRN_W), BF16), jax.ShapeDtypeStruct((1, ATTN_W), F32),
                   jax.ShapeDtypeStruct((1, HGRN_W), F32)],
        scratch_shapes=[pltpu.VMEM((D_MODEL, 2 * D_FF), BF16), pltpu.SemaphoreType.DMA((N_CHIPS,))],
        compiler_params=_cp("arbitrary"),
    )(dgv, w_up4, h1, g2, dh2, w_out, attn_o, rec_o, hg, g_a, g_h)


def _in_bwd(dqkv, dhg, w_in4, u1, x, g1, dh1, tm=256):
    T = x.shape[0]
    nb = T // tm

    def body(*refs):
        parts = refs[:7]
        w_hbm, u_ref, x_ref, g_ref, dh1_ref, dw_ref, dx_ref, dg_ref, w_full, sem, acc = refs[7:]
        i = pl.program_id(0)

        @pl.when(i == 0)
        def _():
            dg_ref[...] = jnp.zeros_like(dg_ref)
            acc[...] = jnp.zeros_like(acc)
            _load_side_by_side(w_hbm, w_full, sem)

        dp = jnp.concatenate([p[...] for p in parts], axis=1)
        acc[...] += _dot_tn(u_ref[...], dp)
        du = _dot_nt(dp, w_full[...])
        xv = x_ref[...]
        r = _rms(xv, D_MODEL)
        n = xv * r
        dg_ref[...] += _colsum(du * n)
        dx_ref[...] = dh1_ref[...] + _rms_bwd(du * g_ref[...], n, r, D_MODEL)

        @pl.when(i == nb - 1)
        def _():
            for k in range(N_CHIPS):
                dw_ref[k] = acc[:, k * IN_SHARD:(k + 1) * IN_SHARD].astype(BF16)

    row = lambda w: pl.BlockSpec((tm, w), lambda i: (i, 0))
    return pl.pallas_call(
        body, name="in_bwd", grid=(nb,),
        in_specs=[row(ATTN_W)] * 7 + [ANY, row(D_MODEL), row(D_MODEL), _full((1, D_MODEL)), row(D_MODEL)],
        out_specs=[_once((N_CHIPS, D_MODEL, IN_SHARD)), row(D_MODEL), _full((1, D_MODEL))],
        out_shape=[jax.ShapeDtypeStruct((N_CHIPS, D_MODEL, IN_SHARD), BF16), jax.ShapeDtypeStruct((T, D_MODEL), F32),
                   jax.ShapeDtypeStruct((1, D_MODEL), F32)],
        scratch_shapes=[pltpu.VMEM((D_MODEL, IN_TOTAL), BF16), pltpu.SemaphoreType.DMA((N_CHIPS,)),
                        pltpu.VMEM((D_MODEL, IN_TOTAL), F32)],
        compiler_params=_cp("arbitrary"),
    )(*dqkv, *dhg, w_in4, u1, x, g1, dh1)


def _dw(a, b, kb, nb_, name, tk=1024, side=1):
    T, K = a.shape
    N = b.shape[1]
    nk, nn, nt = K // kb, N // (nb_ * side), T // tk

    def body(a_ref, b_ref, o_ref, acc):
        t = pl.program_id(2)

        @pl.when(t == 0)
        def _():
            acc[...] = jnp.zeros_like(acc)

        acc[...] += _dot_tn(a_ref[...], b_ref[...].astype(BF16))

        @pl.when(t == nt - 1)
        def _():
            for s in range(side):
                o_ref[s] = acc[:, s * nb_:(s + 1) * nb_].astype(BF16)

    return pl.pallas_call(
        body, name=name, grid=(nk, nn, nt),
        in_specs=[pl.BlockSpec((tk, kb), lambda i, j, t: (t, i)),
                  pl.BlockSpec((tk, nb_ * side), lambda i, j, t: (t, j))],
        out_specs=pl.BlockSpec((side, kb, nb_), lambda i, j, t: (i * nn + j, 0, 0)),
        out_shape=jax.ShapeDtypeStruct((nk * nn * side, kb, nb_), BF16),
        scratch_shapes=[pltpu.VMEM((kb, nb_ * side), F32)],
        compiler_params=_cp("arbitrary", "arbitrary", "arbitrary"),
    )(a, b)


def _step_channel(a, x, tgt, g_a, g_h, w_out, g2, w_up4, conv_w, conv_b, w_down, gf):
    h1, mixed = _mix_out(a["attn_o"], a["rec_o"], a["hg"], x, g_a, g_h, w_out)
    u2, gate, val, conv, act, dh2, loss, dgf = _mlp_fwd(h1, g2, w_up4, conv_w, conv_b, w_down, gf, tgt)
    dgv, dcw, dcb, dw_down = _mlp_bwd(dh2, gate, val, conv, act, conv_w, w_down)
    dw_down = dw_down.reshape(N_CHIPS, D_FF // N_CHIPS, D_MODEL)
    dh1, dg2, da, dr, dgt, dga, dgh = _up_out_bwd(dgv, w_up4, h1, g2, dh2, w_out, a["attn_o"], a["rec_o"], a["hg"],
                                                  g_a, g_h)
    dw_up = _dw(u2, dgv, D_MODEL, UP_SHARD, "dw_up", side=2)
    dw_out = _dw(mixed, dh1, D_MODEL, D_MODEL, "dw_out").reshape(N_CHIPS, D_MODEL // N_CHIPS, D_MODEL)
    return dict(loss=loss, dgf=dgf, dcw=dcw, dcb=dcb, dg2=dg2, dga=dga, dgh=dgh, dh1=dh1, da=da, dr=dr, dgt=dgt,
                dw_down=dw_down, dw_up=dw_up, dw_out=dw_out)


def _step_mixers_bwd(a, b, x, g1, w_in4, lb, dqkv):
    dhq, dhf, dhi, dlb = _hgrn_bwd(a["hg"], lb, a["states"], b["dr"])
    dw_in, dx, dg1 = _in_bwd(dqkv, [dhq, dhf, dhi, b["dgt"]], w_in4, a["u1"], x, g1, b["dh1"])
    return dict(dx=dx, dg1=dg1, dlb=dlb, dw_in=dw_in)


BIG = ("w_in", "w_out", "w_up", "w_down")
ANY = pl.BlockSpec(memory_space=pl.ANY)


def _place():
    x, y, c = lax.axis_index("x"), lax.axis_index("y"), lax.axis_index("c")
    chips = [(1 - x, y), (x, 1 - y), (1 - x, 1 - y)]
    return x, y, c, chips


def _remote(src, dst, send_sems, recv_sems, k, to):
    return pltpu.make_async_remote_copy(src_ref=src, dst_ref=dst, send_sem=send_sems.at[k], recv_sem=recv_sems.at[k],
                                        device_id=to, device_id_type=MESH)


def _gather_weights(shards, conv_w):
    n = len(shards)
    halves = [s.shape[0] // 2 for s in shards]

    def body(*refs):
        ins, cw, outs, ocw = refs[:n], refs[n], refs[n + 1:2 * n + 1], refs[2 * n + 1]
        send_sems, recv_sems = refs[2 * n + 2:]
        x, y, c, chips = _place()
        me, sibling = 2 * x + y, (x, y, 1 - c)

        def part(w, chip, half):
            return outs[w].at[chip, pl.ds(half * halves[w], halves[w]), :]

        sent = []
        for j, chip in enumerate(chips):
            for w in range(n):
                sent.append(_remote(ins[w].at[pl.ds(c * halves[w], halves[w]), :], part(w, me, c),
                                    send_sems, recv_sems, w * 3 + j, (*chip, c)))
            sent.append(_remote(cw, ocw.at[me], send_sems, recv_sems, 6 * n + j, (*chip, c)))
        for cp in sent:
            cp.start()
        for j, chip in enumerate(chips):
            kj = 2 * chip[0] + chip[1]
            for w in range(n):
                _remote(part(w, kj, c), part(w, kj, c), send_sems, recv_sems, w * 3 + j, (*chip, c)).wait_recv()
                fwd = _remote(part(w, kj, c), part(w, kj, c), send_sems, recv_sems, 3 * n + w * 3 + j, sibling)
                fwd.start()
                sent.append(fwd)
        for j, chip in enumerate(chips):
            kj = 2 * chip[0] + chip[1]
            for w in range(n):
                _remote(part(w, kj, 1 - c), part(w, kj, 1 - c), send_sems, recv_sems, 3 * n + w * 3 + j,
                        sibling).wait_recv()
            _remote(cw, ocw.at[kj], send_sems, recv_sems, 6 * n + j, (*chip, c)).wait_recv()
        for cp in sent:
            cp.wait_send()

    n_sem = 6 * n + 3
    outs = pl.pallas_call(
        body, name="gather_weights",
        in_specs=[ANY] * (n + 1), out_specs=[ANY] * (n + 1),
        out_shape=[jax.ShapeDtypeStruct((N_CHIPS,) + s.shape, s.dtype) for s in shards]
        + [jax.ShapeDtypeStruct((N_CHIPS,) + conv_w.shape, conv_w.dtype)],
        scratch_shapes=[pltpu.SemaphoreType.DMA((n_sem,)), pltpu.SemaphoreType.DMA((n_sem,))],
    )(*shards, conv_w)
    chip = 2 * lax.axis_index("x") + lax.axis_index("y")
    return [lax.dynamic_update_slice(o, s[None], (chip,) + (0,) * s.ndim) for o, s in zip(outs, [*shards, conv_w])]


def _allreduce_small(buf):
    rows = buf.shape[0]

    def body(in_ref, out_ref, slots, send_sems, recv_sems):
        x, y, c, _ = _place()
        me = 4 * x + 2 * y + c
        slots[me] = in_ref[...]
        sent = []
        for p in range(1, 8):
            to = (x ^ (p >> 2), y ^ ((p >> 1) & 1), c ^ (p & 1))
            sent.append(_remote(in_ref, slots.at[me], send_sems, recv_sems, p, to))
        for cp in sent:
            cp.start()
        for p in range(1, 8):
            frm = 4 * (x ^ (p >> 2)) + 2 * (y ^ ((p >> 1) & 1)) + (c ^ (p & 1))
            _remote(in_ref, slots.at[frm], send_sems, recv_sems, p, (x, y, c)).wait_recv()
        for cp in sent:
            cp.wait_send()
        acc = slots[0]
        for d in range(1, 8):
            acc = acc + slots[d]
        out_ref[...] = acc

    vm = pl.BlockSpec(memory_space=pltpu.VMEM)
    return pl.pallas_call(
        body, name="allreduce_small", in_specs=[vm], out_specs=vm,
        out_shape=jax.ShapeDtypeStruct(buf.shape, F32),
        scratch_shapes=[pltpu.VMEM((8, rows, 128), F32), pltpu.SemaphoreType.DMA((8,)), pltpu.SemaphoreType.DMA((8,))],
    )(buf)


def _sibling_peer():
    x, y, c, _ = _place()
    return [(x, y, 1 - c)]


def _chip_peers():
    x, y, c, chips = _place()
    return [(*chip, c) for chip in chips]


def _handshake(peers):
    barrier = pltpu.get_barrier_semaphore()
    for peer in peers:
        pl.semaphore_signal(barrier, inc=1, device_id=peer, device_id_type=MESH)
    pl.semaphore_wait(barrier, len(peers))


def _pair_exchange(gs, name, barrier_id):
    n = len(gs)
    halves = [g.shape[1] // 2 for g in gs]

    def body(*refs):
        g, got = refs[:n], refs[n:2 * n]
        send_sems, recv_sems = refs[2 * n:]
        _handshake(_sibling_peer())
        x, y, c, _ = _place()
        cps = [_remote(g[w].at[:, pl.ds((1 - c) * halves[w], halves[w]), :], got[w], send_sems, recv_sems, w,
                       (x, y, 1 - c)) for w in range(n)]
        for cp in cps:
            cp.start()
        for cp in cps:
            cp.wait()

    return pl.pallas_call(
        body, name=name, in_specs=[ANY] * n, out_specs=[ANY] * n,
        out_shape=[jax.ShapeDtypeStruct((N_CHIPS, h, g.shape[2]), g.dtype) for g, h in zip(gs, halves)],
        scratch_shapes=[pltpu.SemaphoreType.DMA((n,)), pltpu.SemaphoreType.DMA((n,))],
        compiler_params=pltpu.CompilerParams(collective_id=barrier_id),
    )(*gs)


def _core_id():
    return lax.axis_index("c").reshape(1).astype(jnp.int32)


def _pair_sum(g, got, name):
    h, C = got.shape[1:]

    def body(c_ref, g_ref, b_ref, o_ref):
        o_ref[...] = (g_ref[...].astype(F32) + b_ref[...].astype(F32)).astype(BF16)

    blk = pl.BlockSpec((1, h, C), lambda k, c_ref: (k, 0, 0))
    return pl.pallas_call(
        body, name=name,
        grid_spec=pltpu.PrefetchScalarGridSpec(
            num_scalar_prefetch=1, grid=(N_CHIPS,),
            in_specs=[pl.BlockSpec((1, h, C), lambda k, c_ref: (k, c_ref[0], 0)), blk], out_specs=blk),
        out_shape=jax.ShapeDtypeStruct(got.shape, BF16), compiler_params=_cp("arbitrary"))(_core_id(), g, got)


def _sum_partials(g, got, landed, name):
    h, C = got.shape[1:]

    def body(ids, g_ref, b_ref, l_ref, o_ref):
        acc = g_ref[0].astype(F32) + b_ref[0].astype(F32)
        for j in range(3):
            acc = acc + l_ref[j].astype(F32)
        o_ref[...] = acc

    ids = jnp.stack([2 * lax.axis_index("x") + lax.axis_index("y"), lax.axis_index("c")]).astype(jnp.int32)
    return pl.pallas_call(
        body, name=name,
        grid_spec=pltpu.PrefetchScalarGridSpec(
            num_scalar_prefetch=1, grid=(1,),
            in_specs=[pl.BlockSpec((1, h, C), lambda i, ids: (ids[0], ids[1], 0)),
                      pl.BlockSpec((1, h, C), lambda i, ids: (ids[0], 0, 0)),
                      pl.BlockSpec((3, h, C), lambda i, ids: (0, 0, 0))],
            out_specs=pl.BlockSpec((h, C), lambda i, ids: (ids[1], 0))),
        out_shape=jax.ShapeDtypeStruct((2 * h, C), F32), compiler_params=_cp("arbitrary"))(ids, g, got, landed)


def _pair_share(reds, name, barrier_id):
    n = len(reds)

    def body(*refs):
        out = refs[n:2 * n]
        send_sems, recv_sems = refs[2 * n:]
        _handshake(_sibling_peer())
        x, y, c, _ = _place()
        def half(w, which):
            h = out[w].shape[0] // 2
            return out[w].at[pl.ds(which * h, h), :]

        cps = [_remote(half(w, c), half(w, c), send_sems, recv_sems, w, (x, y, 1 - c)) for w in range(n)]
        for cp in cps:
            cp.start()
        for w in range(n):
            _remote(half(w, 1 - c), half(w, 1 - c), send_sems, recv_sems, w, (x, y, 1 - c)).wait_recv()
        for cp in cps:
            cp.wait_send()

    return pl.pallas_call(
        body, name=name, in_specs=[ANY] * n, out_specs=[ANY] * n,
        out_shape=[jax.ShapeDtypeStruct(r.shape, F32) for r in reds],
        input_output_aliases={w: w for w in range(n)},
        scratch_shapes=[pltpu.SemaphoreType.DMA((n,)), pltpu.SemaphoreType.DMA((n,))],
        compiler_params=pltpu.CompilerParams(collective_id=barrier_id),
    )(*reds)


HBM = pl.BlockSpec(memory_space=pltpu.HBM)
SEM = pl.BlockSpec(memory_space=pltpu.SEMAPHORE)
DATAFLOW = pltpu.SideEffectType.DATAFLOW_SIDE_EFFECTING


def _copies_start(name, srcs, lands, plan, n_copies, after, peers, barrier_id):
    ns, nb, na = len(srcs), len(srcs) + len(lands), len(after)

    def body(*refs):
        src_refs, land_refs = refs[:ns], refs[ns:nb]
        send_sems, recv_sems = refs[nb + na:nb + na + 2]
        token = refs[-1]
        _handshake(peers())
        for k, (src, there, _, to) in enumerate(plan(src_refs, land_refs)):
            _remote(src, there, send_sems, recv_sems, k, to).start()
        token[...] = jnp.zeros_like(token)

    hbm = lambda a: pltpu.HBM(a.shape, a.dtype)
    outs = pl.pallas_call(
        body, name=name,
        out_shape=(pltpu.SemaphoreType.DMA((n_copies,)), pltpu.SemaphoreType.DMA((n_copies,)),
                   *[hbm(a) for a in srcs], *[hbm(a) for a in lands], jax.ShapeDtypeStruct((8, 128), F32)),
        in_specs=[HBM] * nb + [ANY] * na,
        out_specs=(SEM, SEM, *[HBM] * nb, pl.BlockSpec(memory_space=pltpu.VMEM)),
        input_output_aliases={i: 2 + i for i in range(nb)},
        compiler_params=pltpu.CompilerParams(has_side_effects=DATAFLOW, collective_id=barrier_id),
    )(*[pltpu.with_memory_space_constraint(a, pltpu.HBM) for a in (*srcs, *lands)], *after)
    return outs[0], outs[1], outs[2:2 + ns], outs[2 + ns:2 + nb], outs[-1]


def _copies_wait(name, send_sems, recv_sems, srcs, lands, plan, after):
    ns, nb, na = len(srcs), len(srcs) + len(lands), len(after)

    def body(*refs):
        src_refs, land_refs = refs[:ns], refs[ns:nb]
        send_sems, recv_sems = refs[nb:nb + 2]
        for k, (src, _, here, to) in enumerate(plan(src_refs, land_refs)):
            cp = _remote(src, here, send_sems, recv_sems, k, to)
            cp.wait_send()
            cp.wait_recv()

    hbm = lambda a: pltpu.HBM(a.shape, a.dtype)
    outs = pl.pallas_call(
        body, name=name,
        out_shape=(*[hbm(a) for a in srcs], *[hbm(a) for a in lands]),
        in_specs=[HBM] * nb + [SEM, SEM] + [ANY] * na,
        out_specs=tuple([HBM] * nb),
        input_output_aliases={i: i for i in range(nb)},
        compiler_params=pltpu.CompilerParams(has_side_effects=DATAFLOW),
    )(*srcs, *lands, send_sems, recv_sems, *after)
    return outs[:ns], outs[ns:]


def _gather_plan(halves):
    def plan(shards, lands):
        x, y, c, chips = _place()
        me = 2 * x + y
        copies = []
        for w, h in enumerate(halves):
            rows = pl.ds(c * h, h)
            for chip in chips:
                copies.append((shards[w].at[rows, :], lands[w].at[me, rows, :],
                               lands[w].at[2 * chip[0] + chip[1], rows, :], (*chip, c)))
        return copies
    return plan


def _reduce_plan(n):
    def plan(ps, lands):
        x, y, c, chips = _place()
        return [(ps[w].at[2 * chip[0] + chip[1]], lands[w].at[j], lands[w].at[j], (*chip, c))
                for w in range(n) for j, chip in enumerate(chips)]
    return plan


def _forward_plan(halves):
    def plan(_, lands):
        x, y, c, chips = _place()

        def part(w, chip, half):
            return lands[w].at[2 * chip[0] + chip[1], pl.ds(half * halves[w], halves[w]), :]

        return [(part(w, chip, c), part(w, chip, c), part(w, chip, 1 - c), (x, y, 1 - c))
                for w in range(len(halves)) for chip in chips]
    return plan


def _pair_plan(halves):
    def plan(gs, gots):
        x, y, c, _ = _place()
        return [(gs[w].at[:, pl.ds((1 - c) * h, h), :], gots[w], gots[w], (x, y, 1 - c)) for w, h in enumerate(halves)]
    return plan


def _place_own(gathered, shards):
    chip = 2 * lax.axis_index("x") + lax.axis_index("y")
    return [lax.dynamic_update_slice(o, s[None], (chip, 0, 0)) for o, s in zip(gathered, shards)]


def _adamw(w, g, m, v, name, tr=None):
    R, C = w.shape
    tr = tr or R // 4

    def body(w_ref, g_ref, m_ref, v_ref, d_ref, nm_ref, nv_ref):
        d_ref[...], nm_ref[...], nv_ref[...] = _adamw_math(w_ref[...], g_ref[...], m_ref[...], v_ref[...])

    blk = pl.BlockSpec((tr, C), lambda i: (i, 0))
    return pl.pallas_call(body, name=name, grid=(R // tr,), in_specs=[blk] * 4, out_specs=[blk] * 3,
                          out_shape=[jax.ShapeDtypeStruct((R, C), F32)] * 3, compiler_params=_cp("arbitrary"))(w, g, m, v)


SMALL = (("norm1_g", 1, 1024), ("attn_norm_g", 1, 512), ("hgrn_norm_g", 1, 512), ("hgrn_lb_logits", 2, 512),
         ("norm2_g", 1, 1024), ("conv_b", 1, D_FF), ("final_norm_g", 1, 1024), ("conv_w", 3, D_FF))
LOSS_ROW = sum(r * c for _, r, c in SMALL) // 128
SMALL_ROWS = 136


def _rows_to_lanes(ref, row, width):
    return jnp.concatenate([ref[row + j:row + j + 1, :] for j in range(width // 128)], axis=1)


def _pack_small(grads, dlb, lb, loss):
    def body(*refs):
        parts, dlb_ref, lb_ref, loss_ref, out = refs[:len(SMALL) - 1], refs[-4], refs[-3], refs[-2], refs[-1]
        out[...] = jnp.zeros_like(out)
        lbv = lb_ref[...]
        dl = dlb_ref[...] * lbv * (1.0 - lbv)
        row = 0
        parts = list(parts)
        for name, rows, width in SMALL:
            for r in range(rows):
                if name == "hgrn_lb_logits":
                    src = dl if r == 0 else -dl
                    for j in range(width // 128):
                        out[row + j:row + j + 1, :] = src[:, 128 * j:128 * (j + 1)]
                else:
                    for j in range(width // 128):
                        out[row + j:row + j + 1, :] = parts[0][r:r + 1, 128 * j:128 * (j + 1)]
                row += width // 128
            if name != "hgrn_lb_logits":
                parts.pop(0)
        out[LOSS_ROW:LOSS_ROW + 1, :] = loss_ref[...]

    vm = pl.BlockSpec(memory_space=pltpu.VMEM)
    return pl.pallas_call(body, name="pack_small", in_specs=[vm] * (len(grads) + 3), out_specs=vm,
                          out_shape=jax.ShapeDtypeStruct((SMALL_ROWS, 128), F32))(*grads, dlb, lb, loss)


def _adamw_math(w, g, m, v):
    nm = ADAM_B1 * m + (1.0 - ADAM_B1) * g
    nv = ADAM_B2 * v + (1.0 - ADAM_B2) * (g * g)
    m_hat = nm / (1.0 - ADAM_B1 ** ADAM_STEP)
    v_hat = nv / (1.0 - ADAM_B2 ** ADAM_STEP)
    return -ADAM_LR * (m_hat / (jnp.sqrt(v_hat) + ADAM_EPS) + ADAM_WD * w), nm, nv


def _small_update(summed, g_conv_w, ws, ms, vs):
    n = len(SMALL)

    def body(*refs):
        s_ref, gcw_ref = refs[:2]
        w_refs, m_refs, v_refs = refs[2:2 + n], refs[2 + n:2 + 2 * n], refs[2 + 2 * n:2 + 3 * n]
        outs = refs[2 + 3 * n:]
        row = 0
        for k, (name, rows, width) in enumerate(SMALL):
            if name == "conv_w":
                g = gcw_ref[...]
            else:
                g = jnp.concatenate([_rows_to_lanes(s_ref, row + r * (width // 128), width) for r in range(rows)], axis=0)
            row += rows * (width // 128)
            d, nm, nv = _adamw_math(w_refs[k][...], g, m_refs[k][...], v_refs[k][...])
            for o, val in zip(outs[4 * k:4 * k + 4], (g, d, nm, nv)):
                o[...] = val

    vm = pl.BlockSpec(memory_space=pltpu.VMEM)
    outs = pl.pallas_call(
        body, name="small_update", in_specs=[vm] * (2 + 3 * n), out_specs=[vm] * (4 * n),
        out_shape=[jax.ShapeDtypeStruct(a.shape, F32) for a in ws for _ in range(4)],
    )(summed, g_conv_w, *ws, *ms, *vs)
    return [outs[4 * k:4 * k + 4] for k in range(n)]


def kernel(x, norm1_g, w_in, attn_norm_g, hgrn_norm_g, hgrn_lb_logits, w_out, norm2_g, w_up, conv_w, conv_b, w_down, final_norm_g, loss_target, m_norm1_g, m_w_in, m_attn_norm_g, m_hgrn_norm_g, m_hgrn_lb_logits, m_w_out, m_norm2_g, m_w_up, m_conv_w, m_conv_b, m_w_down, m_final_norm_g, v_norm1_g, v_w_in, v_attn_norm_g, v_hgrn_norm_g, v_hgrn_lb_logits, v_w_out, v_norm2_g, v_w_up, v_conv_w, v_conv_b, v_w_down, v_final_norm_g):
    w = dict(norm1_g=norm1_g, w_in=w_in, attn_norm_g=attn_norm_g, hgrn_norm_g=hgrn_norm_g,
             hgrn_lb_logits=hgrn_lb_logits, w_out=w_out, norm2_g=norm2_g, w_up=w_up, conv_w=conv_w, conv_b=conv_b,
             w_down=w_down, final_norm_g=final_norm_g)
    m = dict(norm1_g=m_norm1_g, w_in=m_w_in, attn_norm_g=m_attn_norm_g, hgrn_norm_g=m_hgrn_norm_g,
             hgrn_lb_logits=m_hgrn_lb_logits, w_out=m_w_out, norm2_g=m_norm2_g, w_up=m_w_up, conv_w=m_conv_w,
             conv_b=m_conv_b, w_down=m_w_down, final_norm_g=m_final_norm_g)
    v = dict(norm1_g=v_norm1_g, w_in=v_w_in, attn_norm_g=v_attn_norm_g, hgrn_norm_g=v_hgrn_norm_g,
             hgrn_lb_logits=v_hgrn_lb_logits, w_out=v_w_out, norm2_g=v_norm2_g, w_up=v_w_up, conv_w=v_conv_w,
             conv_b=v_conv_b, w_down=v_w_down, final_norm_g=v_final_norm_g)
    names = list(w)
    chip = 2 * lax.axis_index("x") + lax.axis_index("y")

    shards = {k: w[k][0].astype(BF16) for k in BIG}
    w_in4, conv_w4 = _gather_weights([shards["w_in"]], conv_w[0])
    conv_w_full = jnp.transpose(conv_w4, (1, 0, 2)).reshape(3, D_FF)
    lb = jax.nn.softmax(hgrn_lb_logits, axis=0)[0:1]
    late = [shards[k] for k in BIG[1:]]
    gather_plan = _gather_plan([s.shape[0] // 2 for s in late])
    started = _copies_start("gather_start", late, [lax.empty((N_CHIPS,) + s.shape, BF16) for s in late], gather_plan,
                            3 * len(late), after=(w_in4,), peers=_chip_peers, barrier_id=0)
    u1, qkv, hg = _in_proj(x[0], norm1_g + started[4][0:1, 0:1], w_in4)
    attn_o, lse = _attn_fwd(qkv)
    late, landed_w = _copies_wait("gather_wait", *started[:4], gather_plan, after=(attn_o,))
    forward_plan = _forward_plan([s.shape[0] // 2 for s in late])
    started = _copies_start("forward_start", [], landed_w, forward_plan, 3 * len(late), after=(),
                            peers=_sibling_peer, barrier_id=1)
    rec_o, states = _hgrn_fwd(hg, lb + started[4][0:1, 0:1])
    a = dict(u1=u1, qkv=qkv, hg=hg, attn_o=attn_o, lse=lse, rec_o=rec_o, states=states)
    w_out4, w_up4, w_down4 = _place_own(
        _copies_wait("forward_wait", *started[:4], forward_plan, after=(rec_o,))[1], late)

    b = _step_channel(a, x[0], loss_target[0], attn_norm_g, hgrn_norm_g, w_out4.reshape(D_MODEL, D_MODEL), norm2_g,
                      w_up4, conv_w_full, conv_b, w_down4.reshape(D_FF, D_MODEL), final_norm_g.reshape(1, D_MODEL))

    early = [b["dw_out"], b["dw_up"], b["dw_down"]]
    pair_plan = _pair_plan([gk.shape[1] // 2 for gk in early])
    started = _copies_start("pair_start", early,
                            [lax.empty((N_CHIPS, gk.shape[1] // 2, gk.shape[2]), BF16) for gk in early], pair_plan,
                            len(early), after=(), peers=_sibling_peer, barrier_id=2)
    dqkv = _attn_bwd(qkv, attn_o, lse, b["da"], started[4])
    early, gots = _copies_wait("pair_wait", *started[:4], pair_plan, after=(dqkv[0],))
    ps = [_pair_sum(gk, got, f"pair_sum_{k}") for gk, got, k in zip(early, gots, BIG[1:])]
    reduce_plan = _reduce_plan(len(ps))
    started = _copies_start("reduce_start", ps, [lax.empty((3,) + p.shape[1:], BF16) for p in ps], reduce_plan,
                            3 * len(ps), after=(), peers=_chip_peers, barrier_id=3)
    c = _step_mixers_bwd(a, b, x[0], norm1_g, w_in4, lb + started[4][0:1, 0:1], dqkv)
    gots_in = _pair_exchange([c["dw_in"]], "pair_exchange_w_in", barrier_id=4)
    ps_in = _pair_sum(c["dw_in"], gots_in[0], "pair_sum_w_in")
    plan_in = _reduce_plan(1)
    started_in = _copies_start("reduce_start_w_in", [ps_in], [lax.empty((3,) + ps_in.shape[1:], BF16)], plan_in, 3,
                               after=(), peers=_chip_peers, barrier_id=5)
    landed = _copies_wait("reduce_wait", *started[:4], reduce_plan, after=(started_in[4],))[1]
    reds = [_sum_partials(gk, got, l, f"sum_partials_{k}") for gk, got, l, k in zip(early, gots, landed, BIG[1:])]
    g = dict(zip(BIG[1:], _pair_share(reds, "pair_share", barrier_id=6)))
    delta, new_m, new_v = {}, {}, {}
    for k in BIG[1:]:
        delta[k], new_m[k], new_v[k] = _adamw(w[k][0], g[k], m[k][0], v[k][0], f"adamw_{k}")

    loss, dx = b["loss"], c["dx"]
    small = dict(g1=c["dg1"], g_a=b["dga"], g_h=b["dgh"], lb=c["dlb"], g2=b["dg2"], conv_w=b["dcw"], conv_b=b["dcb"],
                 gf=b["dgf"])
    summed = _allreduce_small(_pack_small(
        [small["g1"], small["g_a"], small["g_h"], small["g2"], small["conv_b"], small["gf"], small["conv_w"]],
        small["lb"], lb, loss))
    loss_total = summed[LOSS_ROW, 0]
    g_conv_w = lax.dynamic_slice(summed[LOSS_ROW - 3 * D_FF // 128:LOSS_ROW].reshape(3, D_FF),
                                 (0, chip * (D_FF // N_CHIPS)), (3, D_FF // N_CHIPS))
    two_d = lambda p, k: p[k].reshape(-1, p[k].shape[-1])
    updated = _small_update(summed, g_conv_w, *[[two_d(p, k) for k, _, _ in SMALL] for p in (w, m, v)])
    for (k, _, _), parts in zip(SMALL, updated):
        g[k], delta[k], new_m[k], new_v[k] = (a.reshape(w[k].shape) for a in parts)

    landed_in = _copies_wait("reduce_wait_w_in", *started_in[:4], plan_in, after=(updated[0][1], delta["w_up"]))[1]
    red_in = _sum_partials(c["dw_in"], gots_in[0], landed_in[0], "sum_partials_w_in")
    g["w_in"] = _pair_share([red_in], "pair_share_w_in", barrier_id=7)[0]
    delta["w_in"], new_m["w_in"], new_v["w_in"] = _adamw(w_in[0], g["w_in"], m_w_in[0], v_w_in[0], "adamw_w_in")
    for k in BIG:
        g[k], delta[k], new_m[k], new_v[k] = g[k][None], delta[k][None], new_m[k][None], new_v[k][None]

    return (loss_total, dx[None], *[g[k] for k in names], *[delta[k] for k in names],
            *[new_m[k] for k in names], *[new_v[k] for k in names])
```

```python
import math

import jax
import jax.numpy as jnp
from jax import lax
from jax.experimental import pallas as pl
from jax.experimental.pallas import tpu as pltpu

F32 = jnp.float32
BF16 = jnp.bfloat16

D_MODEL = 1024
ATTN_W = 512
HGRN_W = 512
HEAD_PAIR = 128
ATTN_BLK = 128
DILATIONS = (1, 4, 16)
ATTN_CHAINS = 4
ATTN_CHAINS_FWD = 8
HGRN_HEADS = 4
HGRN_DIM = 128
HGRN_CHUNK = 64
SUPER = 256
HGRN_SIDE = 4
D_FF = 2816
FF_CHUNKS = ((0, 1536), (1536, D_FF))
MLP_BWD_CHUNKS = ((0, 768), (768, 1408), (1408, 2176), (2176, D_FF))
N_CHIPS = 4
IN_TOTAL = 3584
IN_SHARD = IN_TOTAL // N_CHIPS
UP_SHARD = 2 * D_FF // N_CHIPS
QKV_W = 3 * ATTN_W
HG_W = 4 * HGRN_W
EPS = 1e-6
NEG = -1e30
V7X_VMEM_BYTES = 64 * 1024 * 1024
VMEM_LIMIT = V7X_VMEM_BYTES - 8 * 1024 * 1024

ADAM_LR = 0.001
ADAM_B1 = 0.9
ADAM_B2 = 0.999
ADAM_EPS = 1e-08
ADAM_WD = 0.01
ADAM_STEP = 10

MESH = pl.DeviceIdType.MESH


def _cp(*sem):
    return pltpu.CompilerParams(dimension_semantics=sem or None, vmem_limit_bytes=VMEM_LIMIT)


def _dot(a, b):
    return jnp.dot(a, b, preferred_element_type=F32)


def _dot_nt(a, b):
    return lax.dot_general(a, b, (((1,), (1,)), ((), ())), preferred_element_type=F32)


def _dot_tn(a, b):
    return lax.dot_general(a, b, (((0,), (0,)), ((), ())), preferred_element_type=F32)


def _sigmoid(x):
    return 1.0 / (1.0 + jnp.exp(-x))


def _rms(x, width):
    return lax.rsqrt(jnp.sum(x * x, axis=-1, keepdims=True) * (1.0 / width) + EPS)


def _rms_bwd(dn, n, r, width):
    return r * (dn - n * (jnp.sum(dn * n, axis=-1, keepdims=True) * (1.0 / width)))


def _colsum(x):
    return jnp.sum(x, axis=0, keepdims=True)


def _row(v, k):
    rid = lax.broadcasted_iota(jnp.int32, v.shape, 0)
    return jnp.sum(jnp.where(rid == k, v, 0.0), axis=0, keepdims=True)


def _full(shape):
    return pl.BlockSpec(shape, lambda *_: (0,) * len(shape))


def _once(shape):
    return pl.BlockSpec(shape, lambda *_: (0,) * len(shape), pipeline_mode=pl.Buffered(1))


def _load_side_by_side(w_hbm, w_full, sem):
    width = w_hbm.shape[2]
    cps = [pltpu.make_async_copy(w_hbm.at[k], w_full.at[:, pl.ds(k * width, width)], sem.at[k]) for k in range(N_CHIPS)]
    for cp in cps:
        cp.start()
    for cp in cps:
        cp.wait()


def _in_proj(x, g1, w_in4, tm=512):
    T = x.shape[0]

    def body(x_ref, g_ref, w_hbm, u_ref, qkv_ref, hg_ref, w_full, sem):
        @pl.when(pl.program_id(0) == 0)
        def _():
            _load_side_by_side(w_hbm, w_full, sem)

        xv = x_ref[...]
        u = (xv * _rms(xv, D_MODEL) * g_ref[...]).astype(BF16)
        u_ref[...] = u
        p = _dot(u, w_full[...])
        qkv_ref[...] = p[:, :QKV_W]
        hg_ref[...] = p[:, QKV_W:]

    return pl.pallas_call(
        body, name="in_proj", grid=(T // tm,),
        in_specs=[pl.BlockSpec((tm, D_MODEL), lambda i: (i, 0)), _full((1, D_MODEL)), ANY],
        out_specs=[pl.BlockSpec((tm, D_MODEL), lambda i: (i, 0)), pl.BlockSpec((tm, QKV_W), lambda i: (i, 0)),
                   pl.BlockSpec((tm, HG_W), lambda i: (i, 0))],
        out_shape=[jax.ShapeDtypeStruct((T, D_MODEL), BF16), jax.ShapeDtypeStruct((T, QKV_W), F32),
                   jax.ShapeDtypeStruct((T, HG_W), F32)],
        scratch_shapes=[pltpu.VMEM((D_MODEL, IN_TOTAL), BF16), pltpu.SemaphoreType.DMA((N_CHIPS,))],
        compiler_params=_cp("arbitrary"),
    )(x, g1, w_in4)


def _attn_masks(bias_ref):
    lane = lax.broadcasted_iota(jnp.int32, (ATTN_BLK, HEAD_PAIR), 1)
    row = lax.broadcasted_iota(jnp.int32, (2 * ATTN_BLK, 2 * ATTN_BLK), 0)
    col = lax.broadcasted_iota(jnp.int32, (2 * ATTN_BLK, 2 * ATTN_BLK), 1)
    base = jnp.where(row >= ATTN_BLK, row - ATTN_BLK, row) - col
    for k in range(2):
        dist = base + k * ATTN_BLK
        bias_ref[k] = jnp.where((dist >= 0) & (dist <= ATTN_BLK), 0.0, NEG)
    bias_ref[2] = jnp.where(col >= ATTN_BLK, bias_ref[1], NEG)
    return lane < 64


def _two_heads(blk, first):
    zero = jnp.zeros_like(blk)
    return jnp.concatenate([jnp.where(first, blk, zero), jnp.where(first, zero, blk)], axis=0)


def _attn_rows(idx, nb, d):
    r, n = idx // nb, idx % nb
    kb = jnp.maximum(n - 1, 0)
    if d == 1:
        q0 = pl.multiple_of(n * ATTN_BLK, ATTN_BLK)
        k0 = pl.multiple_of(kb * ATTN_BLK, ATTN_BLK)
        return pl.ds(q0, ATTN_BLK), pl.ds(k0, 2 * ATTN_BLK), n - kb
    return (pl.ds(r + d * ATTN_BLK * n, ATTN_BLK, stride=d), pl.ds(r + d * ATTN_BLK * kb, 2 * ATTN_BLK, stride=d),
            n - kb)


def _attn_fwd(qkv):
    T = qkv.shape[0]

    n_blocks = T // ATTN_BLK

    def body(q_ref, k_ref, v_ref, o_ref, m_ref, l_ref, bias_ref):
        first = _attn_masks(bias_ref)
        for bi, d in enumerate(DILATIONS):
            nb = T // d // ATTN_BLK

            chains = ATTN_CHAINS_FWD
            per_chain = n_blocks // chains
            carried = d > 1 and per_chain % nb == 0

            def block(idx, kept=None, d=d, nb=nb, bi=bi, carried=carried):
                rows, keys, which = _attn_rows(idx, nb, d)
                q2 = _two_heads(q_ref[rows, :] * 0.125, first).astype(BF16)
                if carried:
                    k_own, v_own = k_ref[rows, :].astype(BF16), v_ref[rows, :].astype(BF16)
                    kw = jnp.concatenate([kept[0], k_own], axis=0)
                    vw = jnp.concatenate([kept[1], v_own], axis=0)
                    which = 2 - which
                else:
                    kw = k_ref[keys, :].astype(BF16)
                    vw = v_ref[keys, :].astype(BF16)
                old = (o_ref[rows, :], m_ref[rows, :], l_ref[rows, :]) if bi else None
                s = _dot_nt(q2, kw) + bias_ref[which]
                mb = jnp.max(s, axis=-1, keepdims=True)
                p = jnp.exp(s - mb)
                lb = jnp.sum(p, axis=-1, keepdims=True)
                o2 = _dot(p.astype(BF16), vw)
                o = jnp.where(first, o2[:ATTN_BLK], o2[ATTN_BLK:])
                m = jnp.where(first, mb[:ATTN_BLK], mb[ATTN_BLK:])
                l = jnp.where(first, lb[:ATTN_BLK], lb[ATTN_BLK:])
                if bi:
                    po, pm, pl_ = old
                    mn = jnp.maximum(pm, m)
                    wa = jnp.exp(pm - mn)
                    wb = jnp.exp(m - mn)
                    o, l, m = po * wa + o * wb, pl_ * wa + l * wb, mn
                return (rows, o, m, l), ((k_own, v_own) if carried else 0)

            def step(i, kept, block=block, carried=carried, chains=chains, per_chain=per_chain):
                done = [block(i + ch * per_chain, kept[ch] if carried else None) for ch in range(chains)]
                for (rows, o, m, l), _ in done:
                    o_ref[rows, :] = o
                    m_ref[rows, :] = m
                    l_ref[rows, :] = l
                return tuple(k for _, k in done) if carried else kept

            zero = jnp.zeros((ATTN_BLK, HEAD_PAIR), BF16)
            lax.fori_loop(0, per_chain, step, ((zero, zero),) * chains if carried else 0)

        def finish(i, carry):
            rows = pl.ds(pl.multiple_of(i * SUPER, SUPER), SUPER)
            l = l_ref[rows, :]
            o_ref[rows, :] = o_ref[rows, :] / l
            m_ref[rows, :] = m_ref[rows, :] + jnp.log(l)
            return carry

        lax.fori_loop(0, T // SUPER, finish, 0)

    col = lambda off: pl.BlockSpec((T, HEAD_PAIR), lambda j: (0, off + j))
    return pl.pallas_call(
        body, name="attn_fwd", grid=(4,),
        in_specs=[col(0), col(4), col(8)], out_specs=[col(0), col(0)],
        out_shape=[jax.ShapeDtypeStruct((T, ATTN_W), F32)] * 2,
        scratch_shapes=[pltpu.VMEM((T, HEAD_PAIR), F32), pltpu.VMEM((3, 2 * ATTN_BLK, 2 * ATTN_BLK), F32)],
        compiler_params=_cp("arbitrary"),
    )(qkv, qkv, qkv)


def _attn_bwd(qkv, o, lse, do, token=None):
    T = qkv.shape[0]
    per_chain = T // ATTN_BLK // ATTN_CHAINS
    extra = [] if token is None else [token]

    def body(q_ref, k_ref, v_ref, o_ref, lse_ref, do_ref, *rest):
        outs = rest[len(extra):len(extra) + 3]
        dq_ref, dk_ref, dv_ref, dkb_ref, dvb_ref, bias_ref = rest[len(extra) + 3:]
        first = _attn_masks(bias_ref)
        dq_ref[...] = jnp.zeros_like(dq_ref)
        dk_ref[...] = jnp.zeros_like(dk_ref)
        dv_ref[...] = jnp.zeros_like(dv_ref)

        def grads(rows, kw, vw, which):
            q2 = _two_heads(q_ref[rows, :] * 0.125, first).astype(BF16)
            lse_b = lse_ref[rows, :]
            dob = do_ref[rows, :]
            prod = dob * o_ref[rows, :]
            old = dq_ref[rows, :]
            lse2 = jnp.concatenate(
                [jnp.max(jnp.where(first, lse_b, NEG), axis=-1, keepdims=True),
                 jnp.max(jnp.where(first, NEG, lse_b), axis=-1, keepdims=True)], axis=0)
            p = jnp.exp(_dot_nt(q2, kw) + (bias_ref[which] - lse2))
            delta = jnp.concatenate(
                [jnp.sum(jnp.where(first, prod, 0.0), axis=-1, keepdims=True),
                 jnp.sum(jnp.where(first, 0.0, prod), axis=-1, keepdims=True)], axis=0)
            do2 = _two_heads(dob, first).astype(BF16)
            ds = (p * (_dot_nt(do2, vw) - delta)).astype(BF16)
            dq2 = _dot(ds, kw) * 0.125
            return (old + jnp.where(first, dq2[:ATTN_BLK], dq2[ATTN_BLK:]), _dot_tn(ds, q2),
                    _dot_tn(p.astype(BF16), do2))

        def block(idx):
            rows, keys, which = _attn_rows(idx, T // ATTN_BLK, 1)
            old = dk_ref[keys, :], dv_ref[keys, :]
            dq, ck, cv = grads(rows, k_ref[keys, :].astype(BF16), v_ref[keys, :].astype(BF16), which)
            return rows, keys, dq, old[0] + ck, old[1] + cv

        def step(i, carry):
            done = [block(i + ch * per_chain) for ch in range(ATTN_CHAINS)]
            for rows, keys, dq, dk, dv in done:
                dq_ref[rows, :] = dq
                dk_ref[keys, :] = dk
                dv_ref[keys, :] = dv
            return carry

        lax.fori_loop(0, per_chain, step, 0)

        for d in DILATIONS[1:]:
            nb = T // d // ATTN_BLK

            def block(idx, kept, d=d, nb=nb):
                r, n = idx // nb, idx % nb
                rows = pl.ds(r + d * ATTN_BLK * n, ATTN_BLK, stride=d)
                before = pl.ds(r + d * ATTN_BLK * jnp.maximum(n - 1, 0), ATTN_BLK, stride=d)
                k_prev, v_prev, dk_prev, dv_prev = kept
                k_own, v_own = k_ref[rows, :].astype(BF16), v_ref[rows, :].astype(BF16)
                dq, ck, cv = grads(rows, jnp.concatenate([k_prev, k_own], axis=0),
                                   jnp.concatenate([v_prev, v_own], axis=0), jnp.where(n > 0, 1, 2))
                stores = (rows, before, dq, dk_prev + ck[:ATTN_BLK], dv_prev + cv[:ATTN_BLK], ck[ATTN_BLK:], cv[ATTN_BLK:])
                return stores, (k_own, v_own, ck[ATTN_BLK:], cv[ATTN_BLK:])

            def step(i, kept, block=block):
                done = [block(i + ch * per_chain, kept[ch]) for ch in range(ATTN_CHAINS)]
                for (rows, before, dq, dk_done, dv_done, dk_own, dv_own), _ in done:
                    dq_ref[rows, :] = dq
                    dkb_ref[before, :] = dk_done
                    dvb_ref[before, :] = dv_done
                    dkb_ref[rows, :] = dk_own
                    dvb_ref[rows, :] = dv_own
                return tuple(k for _, k in done)

            zero = jnp.zeros((ATTN_BLK, HEAD_PAIR), F32)
            lax.fori_loop(0, per_chain, step, ((zero.astype(BF16), zero.astype(BF16), zero, zero),) * ATTN_CHAINS)

            def add(i, carry):
                rows = pl.ds(pl.multiple_of(i * SUPER, SUPER), SUPER)
                dk_ref[rows, :] += dkb_ref[rows, :]
                dv_ref[rows, :] += dvb_ref[rows, :]
                return carry

            lax.fori_loop(0, T // SUPER, add, 0)

        def emit(i, carry):
            rows = pl.ds(pl.multiple_of(i * SUPER, SUPER), SUPER)
            for out, acc in zip(outs, (dq_ref, dk_ref, dv_ref)):
                out[rows, :] = acc[rows, :].astype(BF16)
            return carry

        lax.fori_loop(0, T // SUPER, emit, 0)

    col = lambda off: pl.BlockSpec((T, HEAD_PAIR), lambda j: (0, off + j))
    return pl.pallas_call(
        body, name="attn_bwd", grid=(4,),
        in_specs=[col(0), col(4), col(8), col(0), col(0), col(0)] + [_full(t.shape) for t in extra],
        out_specs=[col(0)] * 3,
        out_shape=[jax.ShapeDtypeStruct((T, ATTN_W), BF16)] * 3,
        scratch_shapes=[pltpu.VMEM((T, HEAD_PAIR), F32)] * 5 + [pltpu.VMEM((3, 2 * ATTN_BLK, 2 * ATTN_BLK), F32)],
        compiler_params=_cp("arbitrary"),
    )(qkv, qkv, qkv, o, lse, do, *extra)


def _chunk_ids():
    row = lax.broadcasted_iota(jnp.int32, (SUPER, HGRN_DIM), 0)
    r2 = lax.broadcasted_iota(jnp.int32, (SUPER, SUPER), 0)
    c2 = lax.broadcasted_iota(jnp.int32, (SUPER, SUPER), 1)
    amask = ((r2 // HGRN_CHUNK) == (c2 // HGRN_CHUNK)) & (c2 <= r2)
    return row % HGRN_CHUNK, row // HGRN_CHUNK, amask


def _cumsum_chunk(x, rmod):
    s = 1
    while s < HGRN_CHUNK:
        x = x + jnp.where(rmod >= s, pltpu.roll(x, s, 0), 0.0)
        s *= 2
    return x


def _suffix_sum_chunk(x, rmod):
    s = 1
    while s < HGRN_CHUNK:
        x = x + jnp.where(rmod < HGRN_CHUNK - s, pltpu.roll(x, SUPER - s, 0), 0.0)
        s *= 2
    return x


def _chunk_rows(vs, cid):
    out = vs[-1]
    for c in reversed(range(len(vs) - 1)):
        out = jnp.where(cid == c, vs[c], out)
    return out


def _expand(x, cid):
    return jnp.concatenate([jnp.where(cid == c, x, 0.0) for c in range(SUPER // HGRN_CHUNK)], axis=1)


def _hgrn_gates(q, f, lbv, rmod, cid, tmp):
    sq = _sigmoid(q)
    sg = _sigmoid(f)
    forget = lbv + (1.0 - lbv) * sg
    key = 1.0 - forget
    b = _cumsum_chunk(jnp.log(forget), rmod)
    tmp[...] = b
    bends = [tmp[c * HGRN_CHUNK + HGRN_CHUNK - 1:(c + 1) * HGRN_CHUNK, :] for c in range(SUPER // HGRN_CHUNK)]
    eb = jnp.exp(b)
    enb = jnp.exp(-b)
    ebe = jnp.exp(_chunk_rows(bends, cid) - b)
    return sq, sg, forget, key, eb, enb, ebe, q * sq * eb, key * enb, key * ebe, [jnp.exp(v) for v in bends]


def _hgrn_fwd(hg, lb):
    T = hg.shape[0]
    nsc = T // SUPER
    NC = SUPER // HGRN_CHUNK

    def body(q_ref, f_ref, i_ref, lb_ref, o_ref, st_ref, state, tmp):
        rmod, cid, amask = _chunk_ids()
        state[...] = jnp.zeros_like(state)
        lbv = lb_ref[...]

        def local(sc, u):
            rows = pl.ds(pl.multiple_of(sc * SUPER, SUPER), SUPER)
            iv = i_ref[rows, :].astype(BF16)
            qd, ki, ke, dec = _hgrn_gates(q_ref[rows, :], f_ref[rows, :], lbv, rmod, cid, tmp.at[u])[-4:]
            a = jnp.where(amask, _dot_nt(qd.astype(BF16), ki.astype(BF16)), 0.0)
            return rows, qd, dec, _dot(a.astype(BF16), iv), _dot_tn(iv, _expand(ke, cid).astype(BF16))

        def step(i, carry):
            parts = [local(i * HGRN_SIDE + u, u) for u in range(HGRN_SIDE)]
            st = state[...]
            entering = []
            for u, (_, _, dec, _, ut) in enumerate(parts):
                st_ref[0, i * HGRN_SIDE + u] = st
                sts = []
                for c in range(NC):
                    sts.append(st)
                    st = st * dec[c] + ut[:, c * HGRN_DIM:(c + 1) * HGRN_DIM]
                entering.append(jnp.concatenate(sts, axis=1).astype(BF16))
            state[...] = st
            for (rows, qd, _, o, _), sts in zip(parts, entering):
                o_ref[rows, :] = o + _dot_nt(_expand(qd, cid).astype(BF16), sts)
            return carry

        lax.fori_loop(0, nsc // HGRN_SIDE, step, 0)

    col = lambda off: pl.BlockSpec((T, HGRN_DIM), lambda h: (0, off + h))
    return pl.pallas_call(
        body, name="hgrn_fwd", grid=(HGRN_HEADS,),
        in_specs=[col(0), col(4), col(8), pl.BlockSpec((1, HGRN_DIM), lambda h: (0, h))],
        out_specs=[pl.BlockSpec((T, HGRN_DIM), lambda h: (0, h)),
                   pl.BlockSpec((1, nsc, HGRN_DIM, HGRN_DIM), lambda h: (h, 0, 0, 0))],
        out_shape=[jax.ShapeDtypeStruct((T, HGRN_W), F32),
                   jax.ShapeDtypeStruct((HGRN_HEADS, nsc, HGRN_DIM, HGRN_DIM), F32)],
        scratch_shapes=[pltpu.VMEM((HGRN_DIM, HGRN_DIM), F32), pltpu.VMEM((HGRN_SIDE, SUPER, HGRN_DIM), F32)],
        compiler_params=_cp("arbitrary"),
    )(hg, hg, hg, lb)


def _hgrn_bwd(hg, lb, states, do):
    T = hg.shape[0]
    nsc = T // SUPER
    NC = SUPER // HGRN_CHUNK

    def body(q_ref, f_ref, i_ref, lb_ref, st_ref, do_ref, dq_ref, df_ref, di_ref, dlb_ref, dstate, tmp):
        rmod, cid, amask = _chunk_ids()
        dstate[...] = jnp.zeros_like(dstate)
        dlb_ref[...] = jnp.zeros_like(dlb_ref)
        lbv = lb_ref[...]

        def local(sc, u):
            rows = pl.ds(pl.multiple_of(sc * SUPER, SUPER), SUPER)
            q = q_ref[rows, :]
            ivf = i_ref[rows, :]
            iv = ivf.astype(BF16)
            dof = do_ref[rows, :]
            dob = dof.astype(BF16)
            sq, sg, forget, key, eb, enb, ebe, qd, ki, ke, dec = _hgrn_gates(q, f_ref[rows, :], lbv, rmod, cid,
                                                                            tmp.at[u])
            qdb, kib = qd.astype(BF16), ki.astype(BF16)
            keexp = _expand(ke, cid).astype(BF16)
            a = jnp.where(amask, _dot_nt(qdb, kib), 0.0).astype(BF16)
            ut = _dot_tn(iv, keexp)
            st = st_ref[0, sc]
            sts = []
            for c in range(NC):
                sts.append(st)
                st = st * dec[c] + ut[:, c * HGRN_DIM:(c + 1) * HGRN_DIM]
            gt = _dot_tn(dob, _expand(qd, cid).astype(BF16))
            da = jnp.where(amask, _dot_nt(dob, iv), 0.0).astype(BF16)
            ststack = jnp.concatenate(sts, axis=0).astype(BF16)
            return dict(rows=rows, q=q, sq=sq, sg=sg, forget=forget, eb=eb, enb=enb, ebe=ebe, qd=qd, ki=ki, ke=ke,
                        dec=dec, sts=sts, gt=gt, keexp=keexp, ivexp=_expand(ivf, cid).astype(BF16),
                        div=_dot_tn(a, dob), dki=_dot_tn(da, qdb),
                        dqd=_dot(da, kib) + _dot(_expand(dof, cid).astype(BF16), ststack))

        def finish(p, nxt, ddec):
            ncat = jnp.concatenate(nxt, axis=1).astype(BF16)
            nstack = jnp.concatenate(nxt, axis=0).astype(BF16)
            dke = _dot(p["ivexp"], nstack)
            dkk = dke * p["ke"]
            dkey = p["dki"] * p["enb"] + dke * p["ebe"]
            db = p["dqd"] * p["qd"] - p["dki"] * p["ki"] - dkk
            dbends = [_colsum(jnp.where(cid == c, dkk, 0.0)) + ddec[c] * p["dec"][c] for c in range(NC)]
            dforget = (_suffix_sum_chunk(db, rmod) + _chunk_rows(dbends, cid)) / p["forget"] - dkey
            sg, sq, q = p["sg"], p["sq"], p["q"]
            df_ref[p["rows"], :] = (dforget * (1.0 - lbv) * sg * (1.0 - sg)).astype(BF16)
            dq_ref[p["rows"], :] = (p["dqd"] * p["eb"] * (sq * (1.0 + q * (1.0 - sq)))).astype(BF16)
            di_ref[p["rows"], :] = (p["div"] + _dot_nt(p["keexp"], ncat)).astype(BF16)
            return _colsum(dforget * (1.0 - sg))

        def step(i, carry):
            parts = [local(nsc - 1 - (i * HGRN_SIDE + u), u) for u in range(HGRN_SIDE)]
            dst = dstate[...]
            chained = []
            for p in parts:
                nxt = [None] * NC
                ddec = [None] * NC
                for c in reversed(range(NC)):
                    nxt[c] = dst
                    ddec[c] = _colsum(dst * p["sts"][c])
                    dst = dst * p["dec"][c] + p["gt"][:, c * HGRN_DIM:(c + 1) * HGRN_DIM]
                chained.append((nxt, ddec))
            dstate[...] = dst
            dlb = dlb_ref[...]
            for p, (nxt, ddec) in zip(parts, chained):
                dlb = dlb + finish(p, nxt, ddec)
            dlb_ref[...] = dlb
            return carry

        lax.fori_loop(0, nsc // HGRN_SIDE, step, 0)

    col = lambda off: pl.BlockSpec((T, HGRN_DIM), lambda h: (0, off + h))
    own = pl.BlockSpec((T, HGRN_DIM), lambda h: (0, h))
    vec = pl.BlockSpec((1, HGRN_DIM), lambda h: (0, h))
    return pl.pallas_call(
        body, name="hgrn_bwd", grid=(HGRN_HEADS,),
        in_specs=[col(0), col(4), col(8), vec,
                  pl.BlockSpec((1, nsc, HGRN_DIM, HGRN_DIM), lambda h: (h, 0, 0, 0)), own],
        out_specs=[own, own, own, vec],
        out_shape=[jax.ShapeDtypeStruct((T, HGRN_W), BF16)] * 3 + [jax.ShapeDtypeStruct((1, HGRN_W), F32)],
        scratch_shapes=[pltpu.VMEM((HGRN_DIM, HGRN_DIM), F32), pltpu.VMEM((HGRN_SIDE, SUPER, HGRN_DIM), F32)],
        compiler_params=_cp("arbitrary"),
    )(hg, hg, hg, lb, states, do)


def _rec_heads(rec, gate, g_h):
    rr = jnp.concatenate(
        [jnp.broadcast_to(_rms(rec[:, h * HGRN_DIM:(h + 1) * HGRN_DIM], HGRN_DIM), (rec.shape[0], HGRN_DIM))
         for h in range(HGRN_HEADS)], axis=1)
    rn = rec * rr
    sg = _sigmoid(gate)
    return rr, rn, sg


def _mix_out(attn_o, rec_o, hg, x, g_a, g_h, w_out, tm=512):
    T = x.shape[0]

    def body(a_ref, r_ref, gt_ref, x_ref, ga_ref, gh_ref, w_ref, h1_ref, mixed_ref):
        a = a_ref[...]
        an = a * _rms(a, ATTN_W) * ga_ref[...]
        gate = gt_ref[...]
        _, rn, sg = _rec_heads(r_ref[...], gate, gh_ref[...])
        mixed = jnp.concatenate([an, rn * gh_ref[...] * (gate * sg)], axis=1).astype(BF16)
        mixed_ref[...] = mixed
        h1_ref[...] = x_ref[...] + _dot(mixed, w_ref[...])

    row = lambda w: pl.BlockSpec((tm, w), lambda i: (i, 0))
    return pl.pallas_call(
        body, name="mix_out", grid=(T // tm,),
        in_specs=[row(ATTN_W), row(HGRN_W), pl.BlockSpec((tm, HGRN_W), lambda i: (i, 3)), row(D_MODEL),
                  _full((1, ATTN_W)), _full((1, HGRN_W)), _once((D_MODEL, D_MODEL))],
        out_specs=[row(D_MODEL), row(D_MODEL)],
        out_shape=[jax.ShapeDtypeStruct((T, D_MODEL), F32), jax.ShapeDtypeStruct((T, D_MODEL), BF16)],
        compiler_params=_cp("arbitrary"),
    )(attn_o, rec_o, hg, x, g_a, g_h, w_out)


_INV_SQRT2 = 1.0 / math.sqrt(2.0)
_INV_SQRT2PI = 1.0 / math.sqrt(2.0 * math.pi)


def _gelu(x):
    return 0.5 * x * (1.0 + lax.erf(x * _INV_SQRT2))


def _gelu_and_grad(x):
    z = x * _INV_SQRT2
    cdf = 0.5 * (1.0 + lax.erf(z))
    return x * cdf, cdf + (x * _INV_SQRT2PI) * jnp.exp(-(z * z))


def _shift_down(g, prev, rowid):
    p1 = _row(prev, prev.shape[0] - 1)
    p2 = _row(prev, prev.shape[0] - 2)
    s1 = jnp.where(rowid == 0, p1, pltpu.roll(g, 1, 0))
    s2 = jnp.where(rowid == 0, p2, jnp.where(rowid == 1, p1, pltpu.roll(g, 2, 0)))
    return s1, s2


def _mlp_fwd(h1, g2, w_up4, conv_w, conv_b, w_down, gf, tgt, tm=256):
    T = h1.shape[0]

    def body(h_ref, g2_ref, wu_hbm, cw_ref, cb_ref, wd_ref, gf_ref, t_ref,
             u_ref, gate_ref, val_ref, conv_ref, act_ref, dh_ref, loss_ref, dgf_ref, carry, wu_ref, sem):
        i = pl.program_id(0)

        @pl.when(i == 0)
        def _():
            carry[...] = jnp.zeros_like(carry)
            loss_ref[...] = jnp.zeros_like(loss_ref)
            dgf_ref[...] = jnp.zeros_like(dgf_ref)
            _load_side_by_side(wu_hbm, wu_ref, sem)

        h = h_ref[...]
        u = (h * _rms(h, D_MODEL) * g2_ref[...]).astype(BF16)
        u_ref[...] = u
        y2 = jnp.zeros((tm, D_MODEL), F32)
        for lo, hi in FF_CHUNKS:
            cols = slice(lo, hi)
            rowid = lax.broadcasted_iota(jnp.int32, (tm, hi - lo), 0)
            gb = _dot(u, wu_ref[:, lo:hi]).astype(BF16)
            vb = _dot(u, wu_ref[:, D_FF + lo:D_FF + hi]).astype(BF16)
            gate_ref[:, cols] = gb
            val_ref[:, cols] = vb
            g = gb.astype(F32)
            s1, s2 = _shift_down(g, carry[:, cols], rowid)
            carry[:, cols] = g[tm - 8:, :]
            conv = cb_ref[:, cols] + cw_ref[0:1, cols] * s2 + cw_ref[1:2, cols] * s1 + cw_ref[2:3, cols] * g
            act = (_gelu(conv) * vb.astype(F32)).astype(BF16)
            conv_ref[:, cols] = conv.astype(BF16)
            act_ref[:, cols] = act
            y2 = y2 + _dot(act, wd_ref[cols, :])
        h2 = h + y2
        rf = _rms(h2, D_MODEL)
        n = h2 * rf
        gfv = gf_ref[...]
        e = n * gfv - t_ref[...]
        loss_ref[...] += jnp.sum(e * e) * (0.5 / D_MODEL)
        dy = e * (1.0 / D_MODEL)
        dgf_ref[...] += _colsum(dy * n)
        dh_ref[...] = _rms_bwd(dy * gfv, n, rf, D_MODEL)

    row = lambda w: pl.BlockSpec((tm, w), lambda i: (i, 0))
    return pl.pallas_call(
        body, name="mlp_fwd", grid=(T // tm,),
        in_specs=[row(D_MODEL), _full((1, D_MODEL)), ANY, _full((3, D_FF)),
                  _full((1, D_FF)), _once((D_FF, D_MODEL)), _full((1, D_MODEL)), row(D_MODEL)],
        out_specs=[row(D_MODEL), row(D_FF), row(D_FF), row(D_FF), row(D_FF), row(D_MODEL), _full((1, 128)),
                   _full((1, D_MODEL))],
        out_shape=[jax.ShapeDtypeStruct((T, D_MODEL), BF16)] + [jax.ShapeDtypeStruct((T, D_FF), BF16)] * 4
        + [jax.ShapeDtypeStruct((T, D_MODEL), F32),
                   jax.ShapeDtypeStruct((1, 128), F32), jax.ShapeDtypeStruct((1, D_MODEL), F32)],
        scratch_shapes=[pltpu.VMEM((8, D_FF), F32), pltpu.VMEM((D_MODEL, 2 * D_FF), BF16),
                        pltpu.SemaphoreType.DMA((N_CHIPS,))],
        compiler_params=_cp("arbitrary"),
    )(h1, g2, w_up4, conv_w, conv_b, w_down, gf, tgt)


def _mlp_bwd(dh2, gate, val, conv, act, conv_w, w_down, tm=256):
    T = dh2.shape[0]
    nb = T // tm

    def body(dh_ref, gate_ref, val_ref, conv_ref, act_ref, cw_ref, wd_ref, dgv_ref, dcw_ref, dcb_ref, dwd_ref,
             carry, acc):
        i = pl.program_id(0)

        @pl.when(i == 0)
        def _():
            carry[...] = jnp.zeros_like(carry)
            dcw_ref[...] = jnp.zeros_like(dcw_ref)
            dcb_ref[...] = jnp.zeros_like(dcb_ref)
            acc[...] = jnp.zeros_like(acc)

        dhb = dh_ref[...].astype(BF16)
        for lo, hi in MLP_BWD_CHUNKS:
            cols = slice(lo, hi)
            rowid = lax.broadcasted_iota(jnp.int32, (tm, hi - lo), 0)
            acc[cols, :] += _dot_tn(act_ref[:, cols], dhb)
            g = gate_ref[:, cols].astype(F32)
            v = val_ref[:, cols].astype(F32)
            cv = conv_ref[:, cols].astype(F32)
            dact = _dot_nt(dhb, wd_ref[cols, :])
            gl, gp = _gelu_and_grad(cv)
            dconv = dact * v * gp
            nxt = carry[:, cols]
            n0, n1 = _row(nxt, 0), _row(nxt, 1)
            u1 = jnp.where(rowid == tm - 1, n0, pltpu.roll(dconv, tm - 1, 0))
            u2 = jnp.where(rowid == tm - 1, n1, jnp.where(rowid == tm - 2, n0, pltpu.roll(dconv, tm - 2, 0)))
            carry[:, cols] = dconv[0:8, :]
            dcb_ref[:, cols] += _colsum(dconv)
            dcw_ref[0:1, cols] += _colsum(u2 * g)
            dcw_ref[1:2, cols] += _colsum(u1 * g)
            dcw_ref[2:3, cols] += _colsum(dconv * g)
            dgate = cw_ref[2:3, cols] * dconv + cw_ref[1:2, cols] * u1 + cw_ref[0:1, cols] * u2
            dgv_ref[:, cols] = dgate.astype(BF16)
            dgv_ref[:, D_FF + lo:D_FF + hi] = (dact * gl).astype(BF16)

        @pl.when(i == nb - 1)
        def _():
            for lo, hi in MLP_BWD_CHUNKS:
                dwd_ref[lo:hi, :] = acc[lo:hi, :].astype(BF16)

    rev = lambda w: pl.BlockSpec((tm, w), lambda i: (nb - 1 - i, 0))
    return pl.pallas_call(
        body, name="mlp_bwd", grid=(nb,),
        in_specs=[rev(D_MODEL), rev(D_FF), rev(D_FF), rev(D_FF), rev(D_FF), _full((3, D_FF)), _once((D_FF, D_MODEL))],
        out_specs=[rev(2 * D_FF), _full((3, D_FF)), _full((1, D_FF)), _once((D_FF, D_MODEL))],
        out_shape=[jax.ShapeDtypeStruct((T, 2 * D_FF), BF16), jax.ShapeDtypeStruct((3, D_FF), F32),
                   jax.ShapeDtypeStruct((1, D_FF), F32), jax.ShapeDtypeStruct((D_FF, D_MODEL), BF16)],
        scratch_shapes=[pltpu.VMEM((8, D_FF), F32), pltpu.VMEM((D_FF, D_MODEL), F32)],
        compiler_params=_cp("arbitrary"),
    )(dh2, gate, val, conv, act, conv_w, w_down)


def _up_out_bwd(dgv, w_up4, h1, g2, dh2, w_out, mixed, attn_o, rec_o, hg, g_a, g_h, tm=256):
    T = h1.shape[0]
    nb = T // tm

    def body(dgv_ref, wu_hbm, h_ref, g2_ref, dh2_ref, wo_ref, mx_ref, a_ref, r_ref, gt_ref, ga_ref, gh_ref,
             dh1_ref, dg2_ref, da_ref, dr_ref, dgt_ref, dga_ref, dgh_ref, dwo_ref, wu_ref, sem, acc):
        i = pl.program_id(0)

        @pl.when(i == 0)
        def _():
            dg2_ref[...] = jnp.zeros_like(dg2_ref)
            dga_ref[...] = jnp.zeros_like(dga_ref)
            dgh_ref[...] = jnp.zeros_like(dgh_ref)
            acc[...] = jnp.zeros_like(acc)
            _load_side_by_side(wu_hbm, wu_ref, sem)

        du = _dot_nt(dgv_ref[...], wu_ref[...])
        h = h_ref[...]
        r = _rms(h, D_MODEL)
        n = h * r
        dg2_ref[...] += _colsum(du * n)
        dh1 = dh2_ref[...] + _rms_bwd(du * g2_ref[...], n, r, D_MODEL)
        dh1_ref[...] = dh1
        dh1b = dh1.astype(BF16)
        acc[...] += _dot_tn(mx_ref[...], dh1b)
        dmix = _dot_nt(dh1b, wo_ref[...])
        dan = dmix[:, :ATTN_W]
        a = a_ref[...]
        ra = _rms(a, ATTN_W)
        na = a * ra
        dga_ref[...] += _colsum(dan * na)
        da_ref[...] = _rms_bwd(dan * ga_ref[...], na, ra, ATTN_W)
        dmr = dmix[:, ATTN_W:]
        gate = gt_ref[...]
        ghv = gh_ref[...]
        rr, rn, sg = _rec_heads(r_ref[...], gate, ghv)
        dgt_ref[...] = (dmr * rn * ghv * (sg * (1.0 + gate * (1.0 - sg)))).astype(BF16)
        drecn = dmr * (gate * sg)
        dgh_ref[...] += _colsum(drecn * rn)
        drn = drecn * ghv
        prod = drn * rn
        mean = jnp.concatenate(
            [jnp.broadcast_to(jnp.sum(prod[:, h_ * HGRN_DIM:(h_ + 1) * HGRN_DIM], axis=-1, keepdims=True),
                              (tm, HGRN_DIM)) for h_ in range(HGRN_HEADS)], axis=1) * (1.0 / HGRN_DIM)
        dr_ref[...] = rr * (drn - rn * mean)

        @pl.when(i == nb - 1)
        def _():
            dwo_ref[...] = acc[...].astype(BF16)

    row = lambda w: pl.BlockSpec((tm, w), lambda i: (i, 0))
    return pl.pallas_call(
        body, name="up_out_bwd", grid=(nb,),
        in_specs=[row(2 * D_FF), ANY, row(D_MODEL), _full((1, D_MODEL)),
                  row(D_MODEL), _once((D_MODEL, D_MODEL)), row(D_MODEL), row(ATTN_W), row(HGRN_W),
                  pl.BlockSpec((tm, HGRN_W), lambda i: (i, 3)), _full((1, ATTN_W)), _full((1, HGRN_W))],
        out_specs=[row(D_MODEL), _full((1, D_MODEL)), row(ATTN_W), row(HGRN_W), row(HGRN_W),
                   _full((1, ATTN_W)), _full((1, HGRN_W)), _once((D_MODEL, D_MODEL))],
        out_shape=[jax.ShapeDtypeStruct((T, D_MODEL), F32), jax.ShapeDtypeStruct((1, D_MODEL), F32),
                   jax.ShapeDtypeStruct((T, ATTN_W), F32), jax.ShapeDtypeStruct((T, HGRN_W), F32),
                   jax.ShapeDtypeStruct((T, HGRN_W), BF16), jax.ShapeDtypeStruct((1, ATTN_W), F32),
                   jax.ShapeDtypeStruct((1, HGRN_W), F32), jax.ShapeDtypeStruct((D_MODEL, D_MODEL), BF16)],
        scratch_shapes=[pltpu.VMEM((D_MODEL, 2 * D_FF), BF16), pltpu.SemaphoreType.DMA((N_CHIPS,)),
                        pltpu.VMEM((D_MODEL, D_MODEL), F32)],
        compiler_params=_cp("arbitrary"),
    )(dgv, w_up4, h1, g2, dh2, w_out, mixed, attn_o, rec_o, hg, g_a, g_h)


def _in_bwd(dqkv, dhg, w_in4, u1, x, g1, dh1, tm=256):
    T = x.shape[0]
    nb = T // tm

    def body(*refs):
        parts = refs[:7]
        w_hbm, u_ref, x_ref, g_ref, dh1_ref, dw_ref, dx_ref, dg_ref, w_full, sem, acc = refs[7:]
        i = pl.program_id(0)

        @pl.when(i == 0)
        def _():
            dg_ref[...] = jnp.zeros_like(dg_ref)
            acc[...] = jnp.zeros_like(acc)
            _load_side_by_side(w_hbm, w_full, sem)

        dp = jnp.concatenate([p[...] for p in parts], axis=1)
        acc[...] += _dot_tn(u_ref[...], dp)
        du = _dot_nt(dp, w_full[...])
        xv = x_ref[...]
        r = _rms(xv, D_MODEL)
        n = xv * r
        dg_ref[...] += _colsum(du * n)
        dx_ref[...] = dh1_ref[...] + _rms_bwd(du * g_ref[...], n, r, D_MODEL)

        @pl.when(i == nb - 1)
        def _():
            for k in range(N_CHIPS):
                dw_ref[k] = acc[:, k * IN_SHARD:(k + 1) * IN_SHARD].astype(BF16)

    row = lambda w: pl.BlockSpec((tm, w), lambda i: (i, 0))
    return pl.pallas_call(
        body, name="in_bwd", grid=(nb,),
        in_specs=[row(ATTN_W)] * 7 + [ANY, row(D_MODEL), row(D_MODEL), _full((1, D_MODEL)), row(D_MODEL)],
        out_specs=[_once((N_CHIPS, D_MODEL, IN_SHARD)), row(D_MODEL), _full((1, D_MODEL))],
        out_shape=[jax.ShapeDtypeStruct((N_CHIPS, D_MODEL, IN_SHARD), BF16), jax.ShapeDtypeStruct((T, D_MODEL), F32),
                   jax.ShapeDtypeStruct((1, D_MODEL), F32)],
        scratch_shapes=[pltpu.VMEM((D_MODEL, IN_TOTAL), BF16), pltpu.SemaphoreType.DMA((N_CHIPS,)),
                        pltpu.VMEM((D_MODEL, IN_TOTAL), F32)],
        compiler_params=_cp("arbitrary"),
    )(*dqkv, *dhg, w_in4, u1, x, g1, dh1)


def _dw(a, b, kb, nb_, name, tk=1024, side=1):
    T, K = a.shape
    N = b.shape[1]
    nk, nn, nt = K // kb, N // (nb_ * side), T // tk

    def body(a_ref, b_ref, o_ref, acc):
        t = pl.program_id(2)

        @pl.when(t == 0)
        def _():
            acc[...] = jnp.zeros_like(acc)

        acc[...] += _dot_tn(a_ref[...], b_ref[...].astype(BF16))

        @pl.when(t == nt - 1)
        def _():
            for s in range(side):
                o_ref[s] = acc[:, s * nb_:(s + 1) * nb_].astype(BF16)

    return pl.pallas_call(
        body, name=name, grid=(nk, nn, nt),
        in_specs=[pl.BlockSpec((tk, kb), lambda i, j, t: (t, i)),
                  pl.BlockSpec((tk, nb_ * side), lambda i, j, t: (t, j))],
        out_specs=pl.BlockSpec((side, kb, nb_), lambda i, j, t: (i * nn + j, 0, 0)),
        out_shape=jax.ShapeDtypeStruct((nk * nn * side, kb, nb_), BF16),
        scratch_shapes=[pltpu.VMEM((kb, nb_ * side), F32)],
        compiler_params=_cp("arbitrary", "arbitrary", "arbitrary"),
    )(a, b)


def _step_channel(a, x, tgt, g_a, g_h, w_out, g2, w_up4, conv_w, conv_b, w_down, gf):
    h1, mixed = _mix_out(a["attn_o"], a["rec_o"], a["hg"], x, g_a, g_h, w_out)
    u2, gate, val, conv, act, dh2, loss, dgf = _mlp_fwd(h1, g2, w_up4, conv_w, conv_b, w_down, gf, tgt)
    dgv, dcw, dcb, dw_down = _mlp_bwd(dh2, gate, val, conv, act, conv_w, w_down)
    dw_down = dw_down.reshape(N_CHIPS, D_FF // N_CHIPS, D_MODEL)
    dh1, dg2, da, dr, dgt, dga, dgh, dw_out = _up_out_bwd(dgv, w_up4, h1, g2, dh2, w_out, mixed, a["attn_o"],
                                                          a["rec_o"], a["hg"], g_a, g_h)
    dw_up = _dw(u2, dgv, D_MODEL, UP_SHARD, "dw_up", side=2)
    dw_out = dw_out.reshape(N_CHIPS, D_MODEL // N_CHIPS, D_MODEL)
    return dict(loss=loss, dgf=dgf, dcw=dcw, dcb=dcb, dg2=dg2, dga=dga, dgh=dgh, dh1=dh1, da=da, dr=dr, dgt=dgt,
                dw_down=dw_down, dw_up=dw_up, dw_out=dw_out)


def _step_mixers_bwd(a, b, x, g1, w_in4, lb, dqkv):
    dhq, dhf, dhi, dlb = _hgrn_bwd(a["hg"], lb, a["states"], b["dr"])
    dw_in, dx, dg1 = _in_bwd(dqkv, [dhq, dhf, dhi, b["dgt"]], w_in4, a["u1"], x, g1, b["dh1"])
    return dict(dx=dx, dg1=dg1, dlb=dlb, dw_in=dw_in)


BIG = ("w_in", "w_out", "w_up", "w_down")
ANY = pl.BlockSpec(memory_space=pl.ANY)


def _place():
    x, y, c = lax.axis_index("x"), lax.axis_index("y"), lax.axis_index("c")
    chips = [(1 - x, y), (x, 1 - y), (1 - x, 1 - y)]
    return x, y, c, chips


def _remote(src, dst, send_sems, recv_sems, k, to):
    return pltpu.make_async_remote_copy(src_ref=src, dst_ref=dst, send_sem=send_sems.at[k], recv_sem=recv_sems.at[k],
                                        device_id=to, device_id_type=MESH)


def _gather_weights(shards, conv_w):
    n = len(shards)
    halves = [s.shape[0] // 2 for s in shards]

    def body(*refs):
        ins, cw, outs, ocw = refs[:n], refs[n], refs[n + 1:2 * n + 1], refs[2 * n + 1]
        send_sems, recv_sems = refs[2 * n + 2:]
        x, y, c, chips = _place()
        me, sibling = 2 * x + y, (x, y, 1 - c)

        def part(w, chip, half):
            return outs[w].at[chip, pl.ds(half * halves[w], halves[w]), :]

        sent = []
        for j, chip in enumerate(chips):
            for w in range(n):
                sent.append(_remote(ins[w].at[pl.ds(c * halves[w], halves[w]), :], part(w, me, c),
                                    send_sems, recv_sems, w * 3 + j, (*chip, c)))
            sent.append(_remote(cw, ocw.at[me], send_sems, recv_sems, 6 * n + j, (*chip, c)))
        for cp in sent:
            cp.start()
        for j, chip in enumerate(chips):
            kj = 2 * chip[0] + chip[1]
            for w in range(n):
                _remote(part(w, kj, c), part(w, kj, c), send_sems, recv_sems, w * 3 + j, (*chip, c)).wait_recv()
                fwd = _remote(part(w, kj, c), part(w, kj, c), send_sems, recv_sems, 3 * n + w * 3 + j, sibling)
                fwd.start()
                sent.append(fwd)
        for j, chip in enumerate(chips):
            kj = 2 * chip[0] + chip[1]
            for w in range(n):
                _remote(part(w, kj, 1 - c), part(w, kj, 1 - c), send_sems, recv_sems, 3 * n + w * 3 + j,
                        sibling).wait_recv()
            _remote(cw, ocw.at[kj], send_sems, recv_sems, 6 * n + j, (*chip, c)).wait_recv()
        for cp in sent:
            cp.wait_send()

    n_sem = 6 * n + 3
    outs = pl.pallas_call(
        body, name="gather_weights",
        in_specs=[ANY] * (n + 1), out_specs=[ANY] * (n + 1),
        out_shape=[jax.ShapeDtypeStruct((N_CHIPS,) + s.shape, s.dtype) for s in shards]
        + [jax.ShapeDtypeStruct((N_CHIPS,) + conv_w.shape, conv_w.dtype)],
        scratch_shapes=[pltpu.SemaphoreType.DMA((n_sem,)), pltpu.SemaphoreType.DMA((n_sem,))],
    )(*shards, conv_w)
    chip = 2 * lax.axis_index("x") + lax.axis_index("y")
    return [lax.dynamic_update_slice(o, s[None], (chip,) + (0,) * s.ndim) for o, s in zip(outs, [*shards, conv_w])]


def _allreduce_small(buf):
    rows = buf.shape[0]

    def body(in_ref, out_ref, slots, send_sems, recv_sems):
        x, y, c, _ = _place()
        me = 4 * x + 2 * y + c
        slots[me] = in_ref[...]
        sent = []
        for p in range(1, 8):
            to = (x ^ (p >> 2), y ^ ((p >> 1) & 1), c ^ (p & 1))
            sent.append(_remote(in_ref, slots.at[me], send_sems, recv_sems, p, to))
        for cp in sent:
            cp.start()
        for p in range(1, 8):
            frm = 4 * (x ^ (p >> 2)) + 2 * (y ^ ((p >> 1) & 1)) + (c ^ (p & 1))
            _remote(in_ref, slots.at[frm], send_sems, recv_sems, p, (x, y, c)).wait_recv()
        for cp in sent:
            cp.wait_send()
        acc = slots[0]
        for d in range(1, 8):
            acc = acc + slots[d]
        out_ref[...] = acc

    vm = pl.BlockSpec(memory_space=pltpu.VMEM)
    return pl.pallas_call(
        body, name="allreduce_small", in_specs=[vm], out_specs=vm,
        out_shape=jax.ShapeDtypeStruct(buf.shape, F32),
        scratch_shapes=[pltpu.VMEM((8, rows, 128), F32), pltpu.SemaphoreType.DMA((8,)), pltpu.SemaphoreType.DMA((8,))],
    )(buf)


def _sibling_peer():
    x, y, c, _ = _place()
    return [(x, y, 1 - c)]


def _chip_peers():
    x, y, c, chips = _place()
    return [(*chip, c) for chip in chips]


def _handshake(peers):
    barrier = pltpu.get_barrier_semaphore()
    for peer in peers:
        pl.semaphore_signal(barrier, inc=1, device_id=peer, device_id_type=MESH)
    pl.semaphore_wait(barrier, len(peers))


def _pair_exchange(gs, name, barrier_id):
    n = len(gs)
    halves = [g.shape[1] // 2 for g in gs]

    def body(*refs):
        g, got = refs[:n], refs[n:2 * n]
        send_sems, recv_sems = refs[2 * n:]
        _handshake(_sibling_peer())
        x, y, c, _ = _place()
        cps = [_remote(g[w].at[:, pl.ds((1 - c) * halves[w], halves[w]), :], got[w], send_sems, recv_sems, w,
                       (x, y, 1 - c)) for w in range(n)]
        for cp in cps:
            cp.start()
        for cp in cps:
            cp.wait()

    return pl.pallas_call(
        body, name=name, in_specs=[ANY] * n, out_specs=[ANY] * n,
        out_shape=[jax.ShapeDtypeStruct((N_CHIPS, h, g.shape[2]), g.dtype) for g, h in zip(gs, halves)],
        scratch_shapes=[pltpu.SemaphoreType.DMA((n,)), pltpu.SemaphoreType.DMA((n,))],
        compiler_params=pltpu.CompilerParams(collective_id=barrier_id),
    )(*gs)


def _core_id():
    return lax.axis_index("c").reshape(1).astype(jnp.int32)


def _pair_sum(g, got, name):
    h, C = got.shape[1:]

    def body(c_ref, g_ref, b_ref, o_ref):
        o_ref[...] = (g_ref[...].astype(F32) + b_ref[...].astype(F32)).astype(BF16)

    blk = pl.BlockSpec((1, h, C), lambda k, c_ref: (k, 0, 0))
    return pl.pallas_call(
        body, name=name,
        grid_spec=pltpu.PrefetchScalarGridSpec(
            num_scalar_prefetch=1, grid=(N_CHIPS,),
            in_specs=[pl.BlockSpec((1, h, C), lambda k, c_ref: (k, c_ref[0], 0)), blk], out_specs=blk),
        out_shape=jax.ShapeDtypeStruct(got.shape, BF16), compiler_params=_cp("arbitrary"))(_core_id(), g, got)


def _sum_partials(g, got, landed, name):
    h, C = got.shape[1:]

    def body(ids, g_ref, b_ref, l_ref, o_ref):
        acc = g_ref[0].astype(F32) + b_ref[0].astype(F32)
        for j in range(3):
            acc = acc + l_ref[j].astype(F32)
        o_ref[...] = acc

    ids = jnp.stack([2 * lax.axis_index("x") + lax.axis_index("y"), lax.axis_index("c")]).astype(jnp.int32)
    return pl.pallas_call(
        body, name=name,
        grid_spec=pltpu.PrefetchScalarGridSpec(
            num_scalar_prefetch=1, grid=(1,),
            in_specs=[pl.BlockSpec((1, h, C), lambda i, ids: (ids[0], ids[1], 0)),
                      pl.BlockSpec((1, h, C), lambda i, ids: (ids[0], 0, 0)),
                      pl.BlockSpec((3, h, C), lambda i, ids: (0, 0, 0))],
            out_specs=pl.BlockSpec((h, C), lambda i, ids: (ids[1], 0))),
        out_shape=jax.ShapeDtypeStruct((2 * h, C), F32), compiler_params=_cp("arbitrary"))(ids, g, got, landed)


def _pair_share(reds, name, barrier_id):
    n = len(reds)

    def body(*refs):
        out = refs[n:2 * n]
        send_sems, recv_sems = refs[2 * n:]
        _handshake(_sibling_peer())
        x, y, c, _ = _place()
        def half(w, which):
            h = out[w].shape[0] // 2
            return out[w].at[pl.ds(which * h, h), :]

        cps = [_remote(half(w, c), half(w, c), send_sems, recv_sems, w, (x, y, 1 - c)) for w in range(n)]
        for cp in cps:
            cp.start()
        for w in range(n):
            _remote(half(w, 1 - c), half(w, 1 - c), send_sems, recv_sems, w, (x, y, 1 - c)).wait_recv()
        for cp in cps:
            cp.wait_send()

    return pl.pallas_call(
        body, name=name, in_specs=[ANY] * n, out_specs=[ANY] * n,
        out_shape=[jax.ShapeDtypeStruct(r.shape, F32) for r in reds],
        input_output_aliases={w: w for w in range(n)},
        scratch_shapes=[pltpu.SemaphoreType.DMA((n,)), pltpu.SemaphoreType.DMA((n,))],
        compiler_params=pltpu.CompilerParams(collective_id=barrier_id),
    )(*reds)


HBM = pl.BlockSpec(memory_space=pltpu.HBM)
SEM = pl.BlockSpec(memory_space=pltpu.SEMAPHORE)
DATAFLOW = pltpu.SideEffectType.DATAFLOW_SIDE_EFFECTING


def _copies_start(name, srcs, lands, plan, n_copies, after, peers, barrier_id):
    ns, nb, na = len(srcs), len(srcs) + len(lands), len(after)

    def body(*refs):
        src_refs, land_refs = refs[:ns], refs[ns:nb]
        send_sems, recv_sems = refs[nb + na:nb + na + 2]
        token = refs[-1]
        _handshake(peers())
        for k, (src, there, _, to) in enumerate(plan(src_refs, land_refs)):
            _remote(src, there, send_sems, recv_sems, k, to).start()
        token[...] = jnp.zeros_like(token)

    hbm = lambda a: pltpu.HBM(a.shape, a.dtype)
    outs = pl.pallas_call(
        body, name=name,
        out_shape=(pltpu.SemaphoreType.DMA((n_copies,)), pltpu.SemaphoreType.DMA((n_copies,)),
                   *[hbm(a) for a in srcs], *[hbm(a) for a in lands], jax.ShapeDtypeStruct((8, 128), F32)),
        in_specs=[HBM] * nb + [ANY] * na,
        out_specs=(SEM, SEM, *[HBM] * nb, pl.BlockSpec(memory_space=pltpu.VMEM)),
        input_output_aliases={i: 2 + i for i in range(nb)},
        compiler_params=pltpu.CompilerParams(has_side_effects=DATAFLOW, collective_id=barrier_id),
    )(*[pltpu.with_memory_space_constraint(a, pltpu.HBM) for a in (*srcs, *lands)], *after)
    return outs[0], outs[1], outs[2:2 + ns], outs[2 + ns:2 + nb], outs[-1]


def _copies_wait(name, send_sems, recv_sems, srcs, lands, plan, after):
    ns, nb, na = len(srcs), len(srcs) + len(lands), len(after)

    def body(*refs):
        src_refs, land_refs = refs[:ns], refs[ns:nb]
        send_sems, recv_sems = refs[nb:nb + 2]
        for k, (src, _, here, to) in enumerate(plan(src_refs, land_refs)):
            cp = _remote(src, here, send_sems, recv_sems, k, to)
            cp.wait_send()
            cp.wait_recv()

    hbm = lambda a: pltpu.HBM(a.shape, a.dtype)
    outs = pl.pallas_call(
        body, name=name,
        out_shape=(*[hbm(a) for a in srcs], *[hbm(a) for a in lands]),
        in_specs=[HBM] * nb + [SEM, SEM] + [ANY] * na,
        out_specs=tuple([HBM] * nb),
        input_output_aliases={i: i for i in range(nb)},
        compiler_params=pltpu.CompilerParams(has_side_effects=DATAFLOW),
    )(*srcs, *lands, send_sems, recv_sems, *after)
    return outs[:ns], outs[ns:]


def _gather_plan(halves):
    def plan(shards, lands):
        x, y, c, chips = _place()
        me = 2 * x + y
        copies = []
        for w, h in enumerate(halves):
            rows = pl.ds(c * h, h)
            for chip in chips:
                copies.append((shards[w].at[rows, :], lands[w].at[me, rows, :],
                               lands[w].at[2 * chip[0] + chip[1], rows, :], (*chip, c)))
        return copies
    return plan


def _reduce_plan(n):
    def plan(ps, lands):
        x, y, c, chips = _place()
        return [(ps[w].at[2 * chip[0] + chip[1]], lands[w].at[j], lands[w].at[j], (*chip, c))
                for w in range(n) for j, chip in enumerate(chips)]
    return plan


def _forward_plan(halves):
    def plan(_, lands):
        x, y, c, chips = _place()

        def part(w, chip, half):
            return lands[w].at[2 * chip[0] + chip[1], pl.ds(half * halves[w], halves[w]), :]

        return [(part(w, chip, c), part(w, chip, c), part(w, chip, 1 - c), (x, y, 1 - c))
                for w in range(len(halves)) for chip in chips]
    return plan


def _pair_plan(halves):
    def plan(gs, gots):
        x, y, c, _ = _place()
        return [(gs[w].at[:, pl.ds((1 - c) * h, h), :], gots[w], gots[w], (x, y, 1 - c)) for w, h in enumerate(halves)]
    return plan


def _place_own(gathered, shards):
    chip = 2 * lax.axis_index("x") + lax.axis_index("y")
    return [lax.dynamic_update_slice(o, s[None], (chip, 0, 0)) for o, s in zip(gathered, shards)]


def _adamw(w, g, m, v, name, tr=None):
    R, C = w.shape
    tr = tr or R // 4

    def body(w_ref, g_ref, m_ref, v_ref, d_ref, nm_ref, nv_ref):
        d_ref[...], nm_ref[...], nv_ref[...] = _adamw_math(w_ref[...], g_ref[...], m_ref[...], v_ref[...])

    blk = pl.BlockSpec((tr, C), lambda i: (i, 0))
    return pl.pallas_call(body, name=name, grid=(R // tr,), in_specs=[blk] * 4, out_specs=[blk] * 3,
                          out_shape=[jax.ShapeDtypeStruct((R, C), F32)] * 3, compiler_params=_cp("arbitrary"))(w, g, m, v)


SMALL = (("norm1_g", 1, 1024), ("attn_norm_g", 1, 512), ("hgrn_norm_g", 1, 512), ("hgrn_lb_logits", 2, 512),
         ("norm2_g", 1, 1024), ("conv_b", 1, D_FF), ("final_norm_g", 1, 1024), ("conv_w", 3, D_FF))
LOSS_ROW = sum(r * c for _, r, c in SMALL) // 128
SMALL_ROWS = 136


def _rows_to_lanes(ref, row, width):
    return jnp.concatenate([ref[row + j:row + j + 1, :] for j in range(width // 128)], axis=1)


def _pack_small(grads, dlb, lb, loss):
    def body(*refs):
        parts, dlb_ref, lb_ref, loss_ref, out = refs[:len(SMALL) - 1], refs[-4], refs[-3], refs[-2], refs[-1]
        out[...] = jnp.zeros_like(out)
        lbv = lb_ref[...]
        dl = dlb_ref[...] * lbv * (1.0 - lbv)
        row = 0
        parts = list(parts)
        for name, rows, width in SMALL:
            for r in range(rows):
                if name == "hgrn_lb_logits":
                    src = dl if r == 0 else -dl
                    for j in range(width // 128):
                        out[row + j:row + j + 1, :] = src[:, 128 * j:128 * (j + 1)]
                else:
                    for j in range(width // 128):
                        out[row + j:row + j + 1, :] = parts[0][r:r + 1, 128 * j:128 * (j + 1)]
                row += width // 128
            if name != "hgrn_lb_logits":
                parts.pop(0)
        out[LOSS_ROW:LOSS_ROW + 1, :] = loss_ref[...]

    vm = pl.BlockSpec(memory_space=pltpu.VMEM)
    return pl.pallas_call(body, name="pack_small", in_specs=[vm] * (len(grads) + 3), out_specs=vm,
                          out_shape=jax.ShapeDtypeStruct((SMALL_ROWS, 128), F32))(*grads, dlb, lb, loss)


def _adamw_math(w, g, m, v):
    nm = ADAM_B1 * m + (1.0 - ADAM_B1) * g
    nv = ADAM_B2 * v + (1.0 - ADAM_B2) * (g * g)
    m_hat = nm / (1.0 - ADAM_B1 ** ADAM_STEP)
    v_hat = nv / (1.0 - ADAM_B2 ** ADAM_STEP)
    return -ADAM_LR * (m_hat / (jnp.sqrt(v_hat) + ADAM_EPS) + ADAM_WD * w), nm, nv


def _small_update(summed, g_conv_w, ws, ms, vs):
    n = len(SMALL)

    def body(*refs):
        s_ref, gcw_ref = refs[:2]
        w_refs, m_refs, v_refs = refs[2:2 + n], refs[2 + n:2 + 2 * n], refs[2 + 2 * n:2 + 3 * n]
        outs = refs[2 + 3 * n:]
        row = 0
        for k, (name, rows, width) in enumerate(SMALL):
            if name == "conv_w":
                g = gcw_ref[...]
            else:
                g = jnp.concatenate([_rows_to_lanes(s_ref, row + r * (width // 128), width) for r in range(rows)], axis=0)
            row += rows * (width // 128)
            d, nm, nv = _adamw_math(w_refs[k][...], g, m_refs[k][...], v_refs[k][...])
            for o, val in zip(outs[4 * k:4 * k + 4], (g, d, nm, nv)):
                o[...] = val

    vm = pl.BlockSpec(memory_space=pltpu.VMEM)
    outs = pl.pallas_call(
        body, name="small_update", in_specs=[vm] * (2 + 3 * n), out_specs=[vm] * (4 * n),
        out_shape=[jax.ShapeDtypeStruct(a.shape, F32) for a in ws for _ in range(4)],
    )(summed, g_conv_w, *ws, *ms, *vs)
    return [outs[4 * k:4 * k + 4] for k in range(n)]


def kernel(x, norm1_g, w_in, attn_norm_g, hgrn_norm_g, hgrn_lb_logits, w_out, norm2_g, w_up, conv_w, conv_b, w_down, final_norm_g, loss_target, m_norm1_g, m_w_in, m_attn_norm_g, m_hgrn_norm_g, m_hgrn_lb_logits, m_w_out, m_norm2_g, m_w_up, m_conv_w, m_conv_b, m_w_down, m_final_norm_g, v_norm1_g, v_w_in, v_attn_norm_g, v_hgrn_norm_g, v_hgrn_lb_logits, v_w_out, v_norm2_g, v_w_up, v_conv_w, v_conv_b, v_w_down, v_final_norm_g):
    w = dict(norm1_g=norm1_g, w_in=w_in, attn_norm_g=attn_norm_g, hgrn_norm_g=hgrn_norm_g,
             hgrn_lb_logits=hgrn_lb_logits, w_out=w_out, norm2_g=norm2_g, w_up=w_up, conv_w=conv_w, conv_b=conv_b,
             w_down=w_down, final_norm_g=final_norm_g)
    m = dict(norm1_g=m_norm1_g, w_in=m_w_in, attn_norm_g=m_attn_norm_g, hgrn_norm_g=m_hgrn_norm_g,
             hgrn_lb_logits=m_hgrn_lb_logits, w_out=m_w_out, norm2_g=m_norm2_g, w_up=m_w_up, conv_w=m_conv_w,
             conv_b=m_conv_b, w_down=m_w_down, final_norm_g=m_final_norm_g)
    v = dict(norm1_g=v_norm1_g, w_in=v_w_in, attn_norm_g=v_attn_norm_g, hgrn_norm_g=v_hgrn_norm_g,
             hgrn_lb_logits=v_hgrn_lb_logits, w_out=v_w_out, norm2_g=v_norm2_g, w_up=v_w_up, conv_w=v_conv_w,
             conv_b=v_conv_b, w_down=v_w_down, final_norm_g=v_final_norm_g)
    names = list(w)
    chip = 2 * lax.axis_index("x") + lax.axis_index("y")

    shards = {k: w[k][0].astype(BF16) for k in BIG}
    w_in4, conv_w4 = _gather_weights([shards["w_in"]], conv_w[0])
    conv_w_full = jnp.transpose(conv_w4, (1, 0, 2)).reshape(3, D_FF)
    lb = jax.nn.softmax(hgrn_lb_logits, axis=0)[0:1]
    late = [shards[k] for k in BIG[1:]]
    gather_plan = _gather_plan([s.shape[0] // 2 for s in late])
    started = _copies_start("gather_start", late, [lax.empty((N_CHIPS,) + s.shape, BF16) for s in late], gather_plan,
                            3 * len(late), after=(w_in4,), peers=_chip_peers, barrier_id=0)
    u1, qkv, hg = _in_proj(x[0], norm1_g + started[4][0:1, 0:1], w_in4)
    attn_o, lse = _attn_fwd(qkv)
    late, landed_w = _copies_wait("gather_wait", *started[:4], gather_plan, after=(attn_o,))
    forward_plan = _forward_plan([s.shape[0] // 2 for s in late])
    started = _copies_start("forward_start", [], landed_w, forward_plan, 3 * len(late), after=(),
                            peers=_sibling_peer, barrier_id=1)
    rec_o, states = _hgrn_fwd(hg, lb + started[4][0:1, 0:1])
    a = dict(u1=u1, qkv=qkv, hg=hg, attn_o=attn_o, lse=lse, rec_o=rec_o, states=states)
    w_out4, w_up4, w_down4 = _place_own(
        _copies_wait("forward_wait", *started[:4], forward_plan, after=(rec_o,))[1], late)

    b = _step_channel(a, x[0], loss_target[0], attn_norm_g, hgrn_norm_g, w_out4.reshape(D_MODEL, D_MODEL), norm2_g,
                      w_up4, conv_w_full, conv_b, w_down4.reshape(D_FF, D_MODEL), final_norm_g.reshape(1, D_MODEL))

    early = [b["dw_out"], b["dw_up"], b["dw_down"]]
    pair_plan = _pair_plan([gk.shape[1] // 2 for gk in early])
    started = _copies_start("pair_start", early,
                            [lax.empty((N_CHIPS, gk.shape[1] // 2, gk.shape[2]), BF16) for gk in early], pair_plan,
                            len(early), after=(), peers=_sibling_peer, barrier_id=2)
    dqkv = _attn_bwd(qkv, attn_o, lse, b["da"], started[4])
    early, gots = _copies_wait("pair_wait", *started[:4], pair_plan, after=(dqkv[0],))
    ps = [_pair_sum(gk, got, f"pair_sum_{k}") for gk, got, k in zip(early, gots, BIG[1:])]
    reduce_plan = _reduce_plan(len(ps))
    started = _copies_start("reduce_start", ps, [lax.empty((3,) + p.shape[1:], BF16) for p in ps], reduce_plan,
                            3 * len(ps), after=(), peers=_chip_peers, barrier_id=3)
    c = _step_mixers_bwd(a, b, x[0], norm1_g, w_in4, lb + started[4][0:1, 0:1], dqkv)
    gots_in = _pair_exchange([c["dw_in"]], "pair_exchange_w_in", barrier_id=4)
    ps_in = _pair_sum(c["dw_in"], gots_in[0], "pair_sum_w_in")
    plan_in = _reduce_plan(1)
    started_in = _copies_start("reduce_start_w_in", [ps_in], [lax.empty((3,) + ps_in.shape[1:], BF16)], plan_in, 3,
                               after=(), peers=_chip_peers, barrier_id=5)
    landed = _copies_wait("reduce_wait", *started[:4], reduce_plan, after=(started_in[4],))[1]
    reds = [_sum_partials(gk, got, l, f"sum_partials_{k}") for gk, got, l, k in zip(early, gots, landed, BIG[1:])]
    g = dict(zip(BIG[1:], _pair_share(reds, "pair_share", barrier_id=6)))
    delta, new_m, new_v = {}, {}, {}
    for k in BIG[1:]:
        delta[k], new_m[k], new_v[k] = _adamw(w[k][0], g[k], m[k][0], v[k][0], f"adamw_{k}")

    loss, dx = b["loss"], c["dx"]
    small = dict(g1=c["dg1"], g_a=b["dga"], g_h=b["dgh"], lb=c["dlb"], g2=b["dg2"], conv_w=b["dcw"], conv_b=b["dcb"],
                 gf=b["dgf"])
    summed = _allreduce_small(_pack_small(
        [small["g1"], small["g_a"], small["g_h"], small["g2"], small["conv_b"], small["gf"], small["conv_w"]],
        small["lb"], lb, loss))
    loss_total = summed[LOSS_ROW, 0]
    g_conv_w = lax.dynamic_slice(summed[LOSS_ROW - 3 * D_FF // 128:LOSS_ROW].reshape(3, D_FF),
                                 (0, chip * (D_FF // N_CHIPS)), (3, D_FF // N_CHIPS))
    two_d = lambda p, k: p[k].reshape(-1, p[k].shape[-1])
    updated = _small_update(summed, g_conv_w, *[[two_d(p, k) for k, _, _ in SMALL] for p in (w, m, v)])
    for (k, _, _), parts in zip(SMALL, updated):
        g[k], delta[k], new_m[k], new_v[k] = (a.reshape(w[k].shape) for a in parts)

    landed_in = _copies_wait("reduce_wait_w_in", *started_in[:4], plan_in, after=(updated[0][1], delta["w_up"]))[1]
    red_in = _sum_partials(c["dw_in"], gots_in[0], landed_in[0], "sum_partials_w_in")
    g["w_in"] = _pair_share([red_in], "pair_share_w_in", barrier_id=7)[0]
    delta["w_in"], new_m["w_in"], new_v["w_in"] = _adamw(w_in[0], g["w_in"], m_w_in[0], v_w_in[0], "adamw_w_in")
    for k in BIG:
        g[k], delta[k], new_m[k], new_v[k] = g[k][None], delta[k][None], new_m[k][None], new_v[k][None]

    return (loss_total, dx[None], *[g[k] for k in names], *[delta[k] for k in names],
            *[new_m[k] for k in names], *[new_v[k] for k in names])
```

```python
import math

import jax
import jax.numpy as jnp
from jax import lax
from jax.experimental import pallas as pl
from jax.experimental.pallas import tpu as pltpu

F32 = jnp.float32
BF16 = jnp.bfloat16

D_MODEL = 1024
ATTN_W = 512
HGRN_W = 512
HEAD_PAIR = 128
ATTN_BLK = 128
DILATIONS = (1, 4, 16)
ATTN_CHAINS = 4
ATTN_CHAINS_FWD = 8
HGRN_HEADS = 4
HGRN_DIM = 128
HGRN_CHUNK = 64
SUPER = 256
HGRN_SIDE = 4
D_FF = 2816
FF_CHUNKS = ((0, 1536), (1536, D_FF))
MLP_BWD_CHUNKS = ((0, 768), (768, 1408), (1408, 2176), (2176, D_FF))
N_CHIPS = 4
IN_TOTAL = 3584
IN_SHARD = IN_TOTAL // N_CHIPS
UP_SHARD = 2 * D_FF // N_CHIPS
QKV_W = 3 * ATTN_W
HG_W = 4 * HGRN_W
EPS = 1e-6
NEG = -1e30
V7X_VMEM_BYTES = 64 * 1024 * 1024
VMEM_LIMIT = V7X_VMEM_BYTES - 8 * 1024 * 1024

ADAM_LR = 0.001
ADAM_B1 = 0.9
ADAM_B2 = 0.999
ADAM_EPS = 1e-08
ADAM_WD = 0.01
ADAM_STEP = 10

MESH = pl.DeviceIdType.MESH


def _cp(*sem):
    return pltpu.CompilerParams(dimension_semantics=sem or None, vmem_limit_bytes=VMEM_LIMIT)


def _dot(a, b):
    return jnp.dot(a, b, preferred_element_type=F32)


def _dot_nt(a, b):
    return lax.dot_general(a, b, (((1,), (1,)), ((), ())), preferred_element_type=F32)


def _dot_tn(a, b):
    return lax.dot_general(a, b, (((0,), (0,)), ((), ())), preferred_element_type=F32)


def _sigmoid(x):
    return 1.0 / (1.0 + jnp.exp(-x))


def _rms(x, width):
    return lax.rsqrt(jnp.sum(x * x, axis=-1, keepdims=True) * (1.0 / width) + EPS)


def _rms_bwd(dn, n, r, width):
    return r * (dn - n * (jnp.sum(dn * n, axis=-1, keepdims=True) * (1.0 / width)))


def _colsum(x):
    return jnp.sum(x, axis=0, keepdims=True)


def _row(v, k):
    rid = lax.broadcasted_iota(jnp.int32, v.shape, 0)
    return jnp.sum(jnp.where(rid == k, v, 0.0), axis=0, keepdims=True)


def _full(shape):
    return pl.BlockSpec(shape, lambda *_: (0,) * len(shape))


def _once(shape):
    return pl.BlockSpec(shape, lambda *_: (0,) * len(shape), pipeline_mode=pl.Buffered(1))


def _load_side_by_side(w_hbm, w_full, sem):
    width = w_hbm.shape[2]
    cps = [pltpu.make_async_copy(w_hbm.at[k], w_full.at[:, pl.ds(k * width, width)], sem.at[k]) for k in range(N_CHIPS)]
    for cp in cps:
        cp.start()
    for cp in cps:
        cp.wait()


def _in_proj(x, g1, w_in4, tm=512):
    T = x.shape[0]

    def body(x_ref, g_ref, w_hbm, u_ref, qkv_ref, hg_ref, w_full, sem):
        @pl.when(pl.program_id(0) == 0)
        def _():
            _load_side_by_side(w_hbm, w_full, sem)

        xv = x_ref[...]
        u = (xv * _rms(xv, D_MODEL) * g_ref[...]).astype(BF16)
        u_ref[...] = u
        p = _dot(u, w_full[...])
        qkv_ref[...] = p[:, :QKV_W]
        hg_ref[...] = p[:, QKV_W:]

    return pl.pallas_call(
        body, name="in_proj", grid=(T // tm,),
        in_specs=[pl.BlockSpec((tm, D_MODEL), lambda i: (i, 0)), _full((1, D_MODEL)), ANY],
        out_specs=[pl.BlockSpec((tm, D_MODEL), lambda i: (i, 0)), pl.BlockSpec((tm, QKV_W), lambda i: (i, 0)),
                   pl.BlockSpec((tm, HG_W), lambda i: (i, 0))],
        out_shape=[jax.ShapeDtypeStruct((T, D_MODEL), BF16), jax.ShapeDtypeStruct((T, QKV_W), F32),
                   jax.ShapeDtypeStruct((T, HG_W), F32)],
        scratch_shapes=[pltpu.VMEM((D_MODEL, IN_TOTAL), BF16), pltpu.SemaphoreType.DMA((N_CHIPS,))],
        compiler_params=_cp("arbitrary"),
    )(x, g1, w_in4)


def _attn_masks(bias_ref):
    lane = lax.broadcasted_iota(jnp.int32, (ATTN_BLK, HEAD_PAIR), 1)
    row = lax.broadcasted_iota(jnp.int32, (2 * ATTN_BLK, 2 * ATTN_BLK), 0)
    col = lax.broadcasted_iota(jnp.int32, (2 * ATTN_BLK, 2 * ATTN_BLK), 1)
    base = jnp.where(row >= ATTN_BLK, row - ATTN_BLK, row) - col
    for k in range(2):
        dist = base + k * ATTN_BLK
        bias_ref[k] = jnp.where((dist >= 0) & (dist <= ATTN_BLK), 0.0, NEG)
    bias_ref[2] = jnp.where(col >= ATTN_BLK, bias_ref[1], NEG)
    return lane < 64


def _two_heads(blk, first):
    zero = jnp.zeros_like(blk)
    return jnp.concatenate([jnp.where(first, blk, zero), jnp.where(first, zero, blk)], axis=0)


def _attn_rows(idx, nb, d):
    r, n = idx // nb, idx % nb
    kb = jnp.maximum(n - 1, 0)
    if d == 1:
        q0 = pl.multiple_of(n * ATTN_BLK, ATTN_BLK)
        k0 = pl.multiple_of(kb * ATTN_BLK, ATTN_BLK)
        return pl.ds(q0, ATTN_BLK), pl.ds(k0, 2 * ATTN_BLK), n - kb
    return (pl.ds(r + d * ATTN_BLK * n, ATTN_BLK, stride=d), pl.ds(r + d * ATTN_BLK * kb, 2 * ATTN_BLK, stride=d),
            n - kb)


def _attn_fwd(qkv):
    T = qkv.shape[0]

    n_blocks = T // ATTN_BLK

    def body(q_ref, k_ref, v_ref, o_ref, m_ref, l_ref, bias_ref):
        first = _attn_masks(bias_ref)
        for bi, d in enumerate(DILATIONS):
            nb = T // d // ATTN_BLK

            chains = ATTN_CHAINS_FWD
            per_chain = n_blocks // chains
            carried = d > 1 and per_chain % nb == 0

            def block(idx, kept=None, d=d, nb=nb, bi=bi, carried=carried):
                rows, keys, which = _attn_rows(idx, nb, d)
                q2 = _two_heads(q_ref[rows, :] * 0.125, first).astype(BF16)
                if carried:
                    k_own, v_own = k_ref[rows, :].astype(BF16), v_ref[rows, :].astype(BF16)
                    kw = jnp.concatenate([kept[0], k_own], axis=0)
                    vw = jnp.concatenate([kept[1], v_own], axis=0)
                    which = 2 - which
                else:
                    kw = k_ref[keys, :].astype(BF16)
                    vw = v_ref[keys, :].astype(BF16)
                old = (o_ref[rows, :], m_ref[rows, :], l_ref[rows, :]) if bi else None
                s = _dot_nt(q2, kw) + bias_ref[which]
                mb = jnp.max(s, axis=-1, keepdims=True)
                p = jnp.exp(s - mb)
                lb = jnp.sum(p, axis=-1, keepdims=True)
                o2 = _dot(p.astype(BF16), vw)
                o = jnp.where(first, o2[:ATTN_BLK], o2[ATTN_BLK:])
                m = jnp.where(first, mb[:ATTN_BLK], mb[ATTN_BLK:])
                l = jnp.where(first, lb[:ATTN_BLK], lb[ATTN_BLK:])
                if bi:
                    po, pm, pl_ = old
                    mn = jnp.maximum(pm, m)
                    wa = jnp.exp(pm - mn)
                    wb = jnp.exp(m - mn)
                    o, l, m = po * wa + o * wb, pl_ * wa + l * wb, mn
                return (rows, o, m, l), ((k_own, v_own) if carried else 0)

            def step(i, kept, block=block, carried=carried, chains=chains, per_chain=per_chain):
                done = [block(i + ch * per_chain, kept[ch] if carried else None) for ch in range(chains)]
                for (rows, o, m, l), _ in done:
                    o_ref[rows, :] = o
                    m_ref[rows, :] = m
                    l_ref[rows, :] = l
                return tuple(k for _, k in done) if carried else kept

            zero = jnp.zeros((ATTN_BLK, HEAD_PAIR), BF16)
            lax.fori_loop(0, per_chain, step, ((zero, zero),) * chains if carried else 0)

        def finish(i, carry):
            rows = pl.ds(pl.multiple_of(i * SUPER, SUPER), SUPER)
            l = l_ref[rows, :]
            o_ref[rows, :] = o_ref[rows, :] / l
            m_ref[rows, :] = m_ref[rows, :] + jnp.log(l)
            return carry

        lax.fori_loop(0, T // SUPER, finish, 0)

    col = lambda off: pl.BlockSpec((T, HEAD_PAIR), lambda j: (0, off + j))
    return pl.pallas_call(
        body, name="attn_fwd", grid=(4,),
        in_specs=[col(0), col(4), col(8)], out_specs=[col(0), col(0)],
        out_shape=[jax.ShapeDtypeStruct((T, ATTN_W), F32)] * 2,
        scratch_shapes=[pltpu.VMEM((T, HEAD_PAIR), F32), pltpu.VMEM((3, 2 * ATTN_BLK, 2 * ATTN_BLK), F32)],
        compiler_params=_cp("arbitrary"),
    )(qkv, qkv, qkv)


def _attn_bwd(qkv, o, lse, do, token=None):
    T = qkv.shape[0]
    per_chain = T // ATTN_BLK // ATTN_CHAINS
    extra = [] if token is None else [token]

    def body(q_ref, k_ref, v_ref, o_ref, lse_ref, do_ref, *rest):
        outs = rest[len(extra):len(extra) + 3]
        dq_ref, dk_ref, dv_ref, dkb_ref, dvb_ref, bias_ref = rest[len(extra) + 3:]
        first = _attn_masks(bias_ref)
        dq_ref[...] = jnp.zeros_like(dq_ref)
        dk_ref[...] = jnp.zeros_like(dk_ref)
        dv_ref[...] = jnp.zeros_like(dv_ref)

        def grads(rows, kw, vw, which):
            q2 = _two_heads(q_ref[rows, :] * 0.125, first).astype(BF16)
            lse_b = lse_ref[rows, :]
            dob = do_ref[rows, :]
            prod = dob * o_ref[rows, :]
            old = dq_ref[rows, :]
            lse2 = jnp.concatenate(
                [jnp.max(jnp.where(first, lse_b, NEG), axis=-1, keepdims=True),
                 jnp.max(jnp.where(first, NEG, lse_b), axis=-1, keepdims=True)], axis=0)
            p = jnp.exp(_dot_nt(q2, kw) + (bias_ref[which] - lse2))
            delta = jnp.concatenate(
                [jnp.sum(jnp.where(first, prod, 0.0), axis=-1, keepdims=True),
                 jnp.sum(jnp.where(first, 0.0, prod), axis=-1, keepdims=True)], axis=0)
            do2 = _two_heads(dob, first).astype(BF16)
            ds = (p * (_dot_nt(do2, vw) - delta)).astype(BF16)
            dq2 = _dot(ds, kw) * 0.125
            return (old + jnp.where(first, dq2[:ATTN_BLK], dq2[ATTN_BLK:]), _dot_tn(ds, q2),
                    _dot_tn(p.astype(BF16), do2))

        def block(idx):
            rows, keys, which = _attn_rows(idx, T // ATTN_BLK, 1)
            old = dk_ref[keys, :], dv_ref[keys, :]
            dq, ck, cv = grads(rows, k_ref[keys, :].astype(BF16), v_ref[keys, :].astype(BF16), which)
            return rows, keys, dq, old[0] + ck, old[1] + cv

        def step(i, carry):
            done = [block(i + ch * per_chain) for ch in range(ATTN_CHAINS)]
            for rows, keys, dq, dk, dv in done:
                dq_ref[rows, :] = dq
                dk_ref[keys, :] = dk
                dv_ref[keys, :] = dv
            return carry

        lax.fori_loop(0, per_chain, step, 0)

        for d in DILATIONS[1:]:
            nb = T // d // ATTN_BLK

            def block(idx, kept, d=d, nb=nb):
                r, n = idx // nb, idx % nb
                rows = pl.ds(r + d * ATTN_BLK * n, ATTN_BLK, stride=d)
                before = pl.ds(r + d * ATTN_BLK * jnp.maximum(n - 1, 0), ATTN_BLK, stride=d)
                k_prev, v_prev, dk_prev, dv_prev = kept
                k_own, v_own = k_ref[rows, :].astype(BF16), v_ref[rows, :].astype(BF16)
                dq, ck, cv = grads(rows, jnp.concatenate([k_prev, k_own], axis=0),
                                   jnp.concatenate([v_prev, v_own], axis=0), jnp.where(n > 0, 1, 2))
                stores = (rows, before, dq, dk_prev + ck[:ATTN_BLK], dv_prev + cv[:ATTN_BLK], ck[ATTN_BLK:], cv[ATTN_BLK:])
                return stores, (k_own, v_own, ck[ATTN_BLK:], cv[ATTN_BLK:])

            def step(i, kept, block=block):
                done = [block(i + ch * per_chain, kept[ch]) for ch in range(ATTN_CHAINS)]
                for (rows, before, dq, dk_done, dv_done, dk_own, dv_own), _ in done:
                    dq_ref[rows, :] = dq
                    dkb_ref[before, :] = dk_done
                    dvb_ref[before, :] = dv_done
                    dkb_ref[rows, :] = dk_own
                    dvb_ref[rows, :] = dv_own
                return tuple(k for _, k in done)

            zero = jnp.zeros((ATTN_BLK, HEAD_PAIR), F32)
            lax.fori_loop(0, per_chain, step, ((zero.astype(BF16), zero.astype(BF16), zero, zero),) * ATTN_CHAINS)

            def add(i, carry):
                rows = pl.ds(pl.multiple_of(i * SUPER, SUPER), SUPER)
                dk_ref[rows, :] += dkb_ref[rows, :]
                dv_ref[rows, :] += dvb_ref[rows, :]
                return carry

            lax.fori_loop(0, T // SUPER, add, 0)

        def emit(i, carry):
            rows = pl.ds(pl.multiple_of(i * SUPER, SUPER), SUPER)
            for out, acc in zip(outs, (dq_ref, dk_ref, dv_ref)):
                out[rows, :] = acc[rows, :].astype(BF16)
            return carry

        lax.fori_loop(0, T // SUPER, emit, 0)

    col = lambda off: pl.BlockSpec((T, HEAD_PAIR), lambda j: (0, off + j))
    return pl.pallas_call(
        body, name="attn_bwd", grid=(4,),
        in_specs=[col(0), col(4), col(8), col(0), col(0), col(0)] + [_full(t.shape) for t in extra],
        out_specs=[col(0)] * 3,
        out_shape=[jax.ShapeDtypeStruct((T, ATTN_W), BF16)] * 3,
        scratch_shapes=[pltpu.VMEM((T, HEAD_PAIR), F32)] * 5 + [pltpu.VMEM((3, 2 * ATTN_BLK, 2 * ATTN_BLK), F32)],
        compiler_params=_cp("arbitrary"),
    )(qkv, qkv, qkv, o, lse, do, *extra)


def _chunk_ids():
    row = lax.broadcasted_iota(jnp.int32, (SUPER, HGRN_DIM), 0)
    r2 = lax.broadcasted_iota(jnp.int32, (SUPER, SUPER), 0)
    c2 = lax.broadcasted_iota(jnp.int32, (SUPER, SUPER), 1)
    amask = ((r2 // HGRN_CHUNK) == (c2 // HGRN_CHUNK)) & (c2 <= r2)
    return row % HGRN_CHUNK, row // HGRN_CHUNK, amask


def _cumsum_chunk(x, rmod):
    s = 1
    while s < HGRN_CHUNK:
        x = x + jnp.where(rmod >= s, pltpu.roll(x, s, 0), 0.0)
        s *= 2
    return x


def _suffix_sum_chunk(x, rmod):
    s = 1
    while s < HGRN_CHUNK:
        x = x + jnp.where(rmod < HGRN_CHUNK - s, pltpu.roll(x, SUPER - s, 0), 0.0)
        s *= 2
    return x


def _chunk_rows(vs, cid):
    out = vs[-1]
    for c in reversed(range(len(vs) - 1)):
        out = jnp.where(cid == c, vs[c], out)
    return out


def _expand(x, cid):
    return jnp.concatenate([jnp.where(cid == c, x, 0.0) for c in range(SUPER // HGRN_CHUNK)], axis=1)


def _hgrn_gates(q, f, lbv, rmod, cid, tmp):
    sq = _sigmoid(q)
    sg = _sigmoid(f)
    forget = lbv + (1.0 - lbv) * sg
    key = 1.0 - forget
    b = _cumsum_chunk(jnp.log(forget), rmod)
    tmp[...] = b
    bends = [tmp[c * HGRN_CHUNK + HGRN_CHUNK - 1:(c + 1) * HGRN_CHUNK, :] for c in range(SUPER // HGRN_CHUNK)]
    eb = jnp.exp(b)
    enb = jnp.exp(-b)
    ebe = jnp.exp(_chunk_rows(bends, cid) - b)
    return sq, sg, forget, key, eb, enb, ebe, q * sq * eb, key * enb, key * ebe, [jnp.exp(v) for v in bends]


def _hgrn_fwd(hg, lb):
    T = hg.shape[0]
    nsc = T // SUPER
    NC = SUPER // HGRN_CHUNK

    def body(q_ref, f_ref, i_ref, lb_ref, o_ref, st_ref, state, tmp):
        rmod, cid, amask = _chunk_ids()
        state[...] = jnp.zeros_like(state)
        lbv = lb_ref[...]

        def local(sc, u):
            rows = pl.ds(pl.multiple_of(sc * SUPER, SUPER), SUPER)
            iv = i_ref[rows, :].astype(BF16)
            qd, ki, ke, dec = _hgrn_gates(q_ref[rows, :], f_ref[rows, :], lbv, rmod, cid, tmp.at[u])[-4:]
            a = jnp.where(amask, _dot_nt(qd.astype(BF16), ki.astype(BF16)), 0.0)
            return rows, qd, dec, _dot(a.astype(BF16), iv), _dot_tn(iv, _expand(ke, cid).astype(BF16))

        def step(i, carry):
            parts = [local(i * HGRN_SIDE + u, u) for u in range(HGRN_SIDE)]
            st = state[...]
            entering = []
            for u, (_, _, dec, _, ut) in enumerate(parts):
                st_ref[0, i * HGRN_SIDE + u] = st
                sts = []
                for c in range(NC):
                    sts.append(st)
                    st = st * dec[c] + ut[:, c * HGRN_DIM:(c + 1) * HGRN_DIM]
                entering.append(jnp.concatenate(sts, axis=1).astype(BF16))
            state[...] = st
            for (rows, qd, _, o, _), sts in zip(parts, entering):
                o_ref[rows, :] = o + _dot_nt(_expand(qd, cid).astype(BF16), sts)
            return carry

        lax.fori_loop(0, nsc // HGRN_SIDE, step, 0)

    col = lambda off: pl.BlockSpec((T, HGRN_DIM), lambda h: (0, off + h))
    return pl.pallas_call(
        body, name="hgrn_fwd", grid=(HGRN_HEADS,),
        in_specs=[col(0), col(4), col(8), pl.BlockSpec((1, HGRN_DIM), lambda h: (0, h))],
        out_specs=[pl.BlockSpec((T, HGRN_DIM), lambda h: (0, h)),
                   pl.BlockSpec((1, nsc, HGRN_DIM, HGRN_DIM), lambda h: (h, 0, 0, 0))],
        out_shape=[jax.ShapeDtypeStruct((T, HGRN_W), F32),
                   jax.ShapeDtypeStruct((HGRN_HEADS, nsc, HGRN_DIM, HGRN_DIM), F32)],
        scratch_shapes=[pltpu.VMEM((HGRN_DIM, HGRN_DIM), F32), pltpu.VMEM((HGRN_SIDE, SUPER, HGRN_DIM), F32)],
        compiler_params=_cp("arbitrary"),
    )(hg, hg, hg, lb)


def _hgrn_bwd(hg, lb, states, do):
    T = hg.shape[0]
    nsc = T // SUPER
    NC = SUPER // HGRN_CHUNK

    def body(q_ref, f_ref, i_ref, lb_ref, st_ref, do_ref, dq_ref, df_ref, di_ref, dlb_ref, dstate, tmp):
        rmod, cid, amask = _chunk_ids()
        dstate[...] = jnp.zeros_like(dstate)
        dlb_ref[...] = jnp.zeros_like(dlb_ref)
        lbv = lb_ref[...]

        def local(sc, u):
            rows = pl.ds(pl.multiple_of(sc * SUPER, SUPER), SUPER)
            q = q_ref[rows, :]
            ivf = i_ref[rows, :]
            iv = ivf.astype(BF16)
            dof = do_ref[rows, :]
            dob = dof.astype(BF16)
            sq, sg, forget, key, eb, enb, ebe, qd, ki, ke, dec = _hgrn_gates(q, f_ref[rows, :], lbv, rmod, cid,
                                                                            tmp.at[u])
            qdb, kib = qd.astype(BF16), ki.astype(BF16)
            keexp = _expand(ke, cid).astype(BF16)
            a = jnp.where(amask, _dot_nt(qdb, kib), 0.0).astype(BF16)
            ut = _dot_tn(iv, keexp)
            st = st_ref[0, sc]
            sts = []
            for c in range(NC):
                sts.append(st)
                st = st * dec[c] + ut[:, c * HGRN_DIM:(c + 1) * HGRN_DIM]
            gt = _dot_tn(dob, _expand(qd, cid).astype(BF16))
            da = jnp.where(amask, _dot_nt(dob, iv), 0.0).astype(BF16)
            ststack = jnp.concatenate(sts, axis=0).astype(BF16)
            return dict(rows=rows, q=q, sq=sq, sg=sg, forget=forget, eb=eb, enb=enb, ebe=ebe, qd=qd, ki=ki, ke=ke,
                        dec=dec, sts=sts, gt=gt, keexp=keexp, ivexp=_expand(ivf, cid).astype(BF16),
                        div=_dot_tn(a, dob), dki=_dot_tn(da, qdb),
                        dqd=_dot(da, kib) + _dot(_expand(dof, cid).astype(BF16), ststack))

        def finish(p, nxt, ddec):
            ncat = jnp.concatenate(nxt, axis=1).astype(BF16)
            nstack = jnp.concatenate(nxt, axis=0).astype(BF16)
            dke = _dot(p["ivexp"], nstack)
            dkk = dke * p["ke"]
            dkey = p["dki"] * p["enb"] + dke * p["ebe"]
            db = p["dqd"] * p["qd"] - p["dki"] * p["ki"] - dkk
            dbends = [_colsum(jnp.where(cid == c, dkk, 0.0)) + ddec[c] * p["dec"][c] for c in range(NC)]
            dforget = (_suffix_sum_chunk(db, rmod) + _chunk_rows(dbends, cid)) / p["forget"] - dkey
            sg, sq, q = p["sg"], p["sq"], p["q"]
            df_ref[p["rows"], :] = (dforget * (1.0 - lbv) * sg * (1.0 - sg)).astype(BF16)
            dq_ref[p["rows"], :] = (p["dqd"] * p["eb"] * (sq * (1.0 + q * (1.0 - sq)))).astype(BF16)
            di_ref[p["rows"], :] = (p["div"] + _dot_nt(p["keexp"], ncat)).astype(BF16)
            return _colsum(dforget * (1.0 - sg))

        def step(i, carry):
            parts = [local(nsc - 1 - (i * HGRN_SIDE + u), u) for u in range(HGRN_SIDE)]
            dst = dstate[...]
            chained = []
            for p in parts:
                nxt = [None] * NC
                ddec = [None] * NC
                for c in reversed(range(NC)):
                    nxt[c] = dst
                    ddec[c] = _colsum(dst * p["sts"][c])
                    dst = dst * p["dec"][c] + p["gt"][:, c * HGRN_DIM:(c + 1) * HGRN_DIM]
                chained.append((nxt, ddec))
            dstate[...] = dst
            dlb = dlb_ref[...]
            for p, (nxt, ddec) in zip(parts, chained):
                dlb = dlb + finish(p, nxt, ddec)
            dlb_ref[...] = dlb
            return carry

        lax.fori_loop(0, nsc // HGRN_SIDE, step, 0)

    col = lambda off: pl.BlockSpec((T, HGRN_DIM), lambda h: (0, off + h))
    own = pl.BlockSpec((T, HGRN_DIM), lambda h: (0, h))
    vec = pl.BlockSpec((1, HGRN_DIM), lambda h: (0, h))
    return pl.pallas_call(
        body, name="hgrn_bwd", grid=(HGRN_HEADS,),
        in_specs=[col(0), col(4), col(8), vec,
                  pl.BlockSpec((1, nsc, HGRN_DIM, HGRN_DIM), lambda h: (h, 0, 0, 0)), own],
        out_specs=[own, own, own, vec],
        out_shape=[jax.ShapeDtypeStruct((T, HGRN_W), BF16)] * 3 + [jax.ShapeDtypeStruct((1, HGRN_W), F32)],
        scratch_shapes=[pltpu.VMEM((HGRN_DIM, HGRN_DIM), F32), pltpu.VMEM((HGRN_SIDE, SUPER, HGRN_DIM), F32)],
        compiler_params=_cp("arbitrary"),
    )(hg, hg, hg, lb, states, do)


def _rec_heads(rec, gate, g_h):
    rr = jnp.concatenate(
        [jnp.broadcast_to(_rms(rec[:, h * HGRN_DIM:(h + 1) * HGRN_DIM], HGRN_DIM), (rec.shape[0], HGRN_DIM))
         for h in range(HGRN_HEADS)], axis=1)
    rn = rec * rr
    sg = _sigmoid(gate)
    return rr, rn, sg


_INV_SQRT2 = 1.0 / math.sqrt(2.0)
_INV_SQRT2PI = 1.0 / math.sqrt(2.0 * math.pi)


def _gelu(x):
    return 0.5 * x * (1.0 + lax.erf(x * _INV_SQRT2))


def _gelu_and_grad(x):
    z = x * _INV_SQRT2
    cdf = 0.5 * (1.0 + lax.erf(z))
    return x * cdf, cdf + (x * _INV_SQRT2PI) * jnp.exp(-(z * z))


def _shift_down(g, prev, rowid):
    p1 = _row(prev, prev.shape[0] - 1)
    p2 = _row(prev, prev.shape[0] - 2)
    s1 = jnp.where(rowid == 0, p1, pltpu.roll(g, 1, 0))
    s2 = jnp.where(rowid == 0, p2, jnp.where(rowid == 1, p1, pltpu.roll(g, 2, 0)))
    return s1, s2


def _mlp_fwd(attn_o, rec_o, hg, x, g_a, g_h, w_out, g2, w_up4, conv_w, conv_b, w_down, gf, tgt, tm=256):
    T = x.shape[0]

    def body(a_ref, r_ref, gt_ref, x_ref, ga_ref, gh_ref, wo_ref, g2_ref, wu_hbm, cw_ref, cb_ref, wd_ref, gf_ref, t_ref,
             h1_ref, mixed_ref, u_ref, gate_ref, val_ref, conv_ref, act_ref, dh_ref, loss_ref, dgf_ref,
             carry, wu_ref, sem):
        i = pl.program_id(0)

        @pl.when(i == 0)
        def _():
            carry[...] = jnp.zeros_like(carry)
            loss_ref[...] = jnp.zeros_like(loss_ref)
            dgf_ref[...] = jnp.zeros_like(dgf_ref)
            _load_side_by_side(wu_hbm, wu_ref, sem)

        a = a_ref[...]
        an = a * _rms(a, ATTN_W) * ga_ref[...]
        og = gt_ref[...]
        _, rn, sg = _rec_heads(r_ref[...], og, gh_ref[...])
        mixed = jnp.concatenate([an, rn * gh_ref[...] * (og * sg)], axis=1).astype(BF16)
        mixed_ref[...] = mixed
        h = x_ref[...] + _dot(mixed, wo_ref[...])
        h1_ref[...] = h
        u = (h * _rms(h, D_MODEL) * g2_ref[...]).astype(BF16)
        u_ref[...] = u
        y2 = jnp.zeros((tm, D_MODEL), F32)
        for lo, hi in FF_CHUNKS:
            cols = slice(lo, hi)
            rowid = lax.broadcasted_iota(jnp.int32, (tm, hi - lo), 0)
            gb = _dot(u, wu_ref[:, lo:hi]).astype(BF16)
            vb = _dot(u, wu_ref[:, D_FF + lo:D_FF + hi]).astype(BF16)
            gate_ref[:, cols] = gb
            val_ref[:, cols] = vb
            g = gb.astype(F32)
            s1, s2 = _shift_down(g, carry[:, cols], rowid)
            carry[:, cols] = g[tm - 8:, :]
            conv = cb_ref[:, cols] + cw_ref[0:1, cols] * s2 + cw_ref[1:2, cols] * s1 + cw_ref[2:3, cols] * g
            act = (_gelu(conv) * vb.astype(F32)).astype(BF16)
            conv_ref[:, cols] = conv.astype(BF16)
            act_ref[:, cols] = act
            y2 = y2 + _dot(act, wd_ref[cols, :])
        h2 = h + y2
        rf = _rms(h2, D_MODEL)
        n = h2 * rf
        gfv = gf_ref[...]
        e = n * gfv - t_ref[...]
        loss_ref[...] += jnp.sum(e * e) * (0.5 / D_MODEL)
        dy = e * (1.0 / D_MODEL)
        dgf_ref[...] += _colsum(dy * n)
        dh_ref[...] = _rms_bwd(dy * gfv, n, rf, D_MODEL)

    row = lambda w: pl.BlockSpec((tm, w), lambda i: (i, 0))
    return pl.pallas_call(
        body, name="mlp_fwd", grid=(T // tm,),
        in_specs=[row(ATTN_W), row(HGRN_W), pl.BlockSpec((tm, HGRN_W), lambda i: (i, 3)), row(D_MODEL),
                  _full((1, ATTN_W)), _full((1, HGRN_W)), _once((D_MODEL, D_MODEL)),
                  _full((1, D_MODEL)), ANY, _full((3, D_FF)),
                  _full((1, D_FF)), _once((D_FF, D_MODEL)), _full((1, D_MODEL)), row(D_MODEL)],
        out_specs=[row(D_MODEL), row(D_MODEL), row(D_MODEL), row(D_FF), row(D_FF), row(D_FF), row(D_FF), row(D_MODEL),
                   _full((1, 128)), _full((1, D_MODEL))],
        out_shape=[jax.ShapeDtypeStruct((T, D_MODEL), F32), jax.ShapeDtypeStruct((T, D_MODEL), BF16),
                   jax.ShapeDtypeStruct((T, D_MODEL), BF16)] + [jax.ShapeDtypeStruct((T, D_FF), BF16)] * 4
        + [jax.ShapeDtypeStruct((T, D_MODEL), F32),
                   jax.ShapeDtypeStruct((1, 128), F32), jax.ShapeDtypeStruct((1, D_MODEL), F32)],
        scratch_shapes=[pltpu.VMEM((8, D_FF), F32), pltpu.VMEM((D_MODEL, 2 * D_FF), BF16),
                        pltpu.SemaphoreType.DMA((N_CHIPS,))],
        compiler_params=_cp("arbitrary"),
    )(attn_o, rec_o, hg, x, g_a, g_h, w_out, g2, w_up4, conv_w, conv_b, w_down, gf, tgt)


def _mlp_bwd(dh2, gate, val, conv, act, conv_w, w_down, tm=256):
    T = dh2.shape[0]
    nb = T // tm

    def body(dh_ref, gate_ref, val_ref, conv_ref, act_ref, cw_ref, wd_ref, dgv_ref, dcw_ref, dcb_ref, dwd_ref,
             carry, acc):
        i = pl.program_id(0)

        @pl.when(i == 0)
        def _():
            carry[...] = jnp.zeros_like(carry)
            dcw_ref[...] = jnp.zeros_like(dcw_ref)
            dcb_ref[...] = jnp.zeros_like(dcb_ref)
            acc[...] = jnp.zeros_like(acc)

        dhb = dh_ref[...].astype(BF16)
        for lo, hi in MLP_BWD_CHUNKS:
            cols = slice(lo, hi)
            rowid = lax.broadcasted_iota(jnp.int32, (tm, hi - lo), 0)
            acc[cols, :] += _dot_tn(act_ref[:, cols], dhb)
            g = gate_ref[:, cols].astype(F32)
            v = val_ref[:, cols].astype(F32)
            cv = conv_ref[:, cols].astype(F32)
            dact = _dot_nt(dhb, wd_ref[cols, :])
            gl, gp = _gelu_and_grad(cv)
            dconv = dact * v * gp
            nxt = carry[:, cols]
            n0, n1 = _row(nxt, 0), _row(nxt, 1)
            u1 = jnp.where(rowid == tm - 1, n0, pltpu.roll(dconv, tm - 1, 0))
            u2 = jnp.where(rowid == tm - 1, n1, jnp.where(rowid == tm - 2, n0, pltpu.roll(dconv, tm - 2, 0)))
            carry[:, cols] = dconv[0:8, :]
            dcb_ref[:, cols] += _colsum(dconv)
            dcw_ref[0:1, cols] += _colsum(u2 * g)
            dcw_ref[1:2, cols] += _colsum(u1 * g)
            dcw_ref[2:3, cols] += _colsum(dconv * g)
            dgate = cw_ref[2:3, cols] * dconv + cw_ref[1:2, cols] * u1 + cw_ref[0:1, cols] * u2
            dgv_ref[:, cols] = dgate.astype(BF16)
            dgv_ref[:, D_FF + lo:D_FF + hi] = (dact * gl).astype(BF16)

        @pl.when(i == nb - 1)
        def _():
            for lo, hi in MLP_BWD_CHUNKS:
                dwd_ref[lo:hi, :] = acc[lo:hi, :].astype(BF16)

    rev = lambda w: pl.BlockSpec((tm, w), lambda i: (nb - 1 - i, 0))
    return pl.pallas_call(
        body, name="mlp_bwd", grid=(nb,),
        in_specs=[rev(D_MODEL), rev(D_FF), rev(D_FF), rev(D_FF), rev(D_FF), _full((3, D_FF)), _once((D_FF, D_MODEL))],
        out_specs=[rev(2 * D_FF), _full((3, D_FF)), _full((1, D_FF)), _once((D_FF, D_MODEL))],
        out_shape=[jax.ShapeDtypeStruct((T, 2 * D_FF), BF16), jax.ShapeDtypeStruct((3, D_FF), F32),
                   jax.ShapeDtypeStruct((1, D_FF), F32), jax.ShapeDtypeStruct((D_FF, D_MODEL), BF16)],
        scratch_shapes=[pltpu.VMEM((8, D_FF), F32), pltpu.VMEM((D_FF, D_MODEL), F32)],
        compiler_params=_cp("arbitrary"),
    )(dh2, gate, val, conv, act, conv_w, w_down)


def _up_out_bwd(dgv, w_up4, h1, g2, dh2, w_out, mixed, attn_o, rec_o, hg, g_a, g_h, tm=256):
    T = h1.shape[0]
    nb = T // tm

    def body(dgv_ref, wu_hbm, h_ref, g2_ref, dh2_ref, wo_ref, mx_ref, a_ref, r_ref, gt_ref, ga_ref, gh_ref,
             dh1_ref, dg2_ref, da_ref, dr_ref, dgt_ref, dga_ref, dgh_ref, dwo_ref, wu_ref, sem, acc):
        i = pl.program_id(0)

        @pl.when(i == 0)
        def _():
            dg2_ref[...] = jnp.zeros_like(dg2_ref)
            dga_ref[...] = jnp.zeros_like(dga_ref)
            dgh_ref[...] = jnp.zeros_like(dgh_ref)
            acc[...] = jnp.zeros_like(acc)
            _load_side_by_side(wu_hbm, wu_ref, sem)

        du = _dot_nt(dgv_ref[...], wu_ref[...])
        h = h_ref[...]
        r = _rms(h, D_MODEL)
        n = h * r
        dg2_ref[...] += _colsum(du * n)
        dh1 = dh2_ref[...] + _rms_bwd(du * g2_ref[...], n, r, D_MODEL)
        dh1_ref[...] = dh1
        dh1b = dh1.astype(BF16)
        acc[...] += _dot_tn(mx_ref[...], dh1b)
        dmix = _dot_nt(dh1b, wo_ref[...])
        dan = dmix[:, :ATTN_W]
        a = a_ref[...]
        ra = _rms(a, ATTN_W)
        na = a * ra
        dga_ref[...] += _colsum(dan * na)
        da_ref[...] = _rms_bwd(dan * ga_ref[...], na, ra, ATTN_W)
        dmr = dmix[:, ATTN_W:]
        gate = gt_ref[...]
        ghv = gh_ref[...]
        rr, rn, sg = _rec_heads(r_ref[...], gate, ghv)
        dgt_ref[...] = (dmr * rn * ghv * (sg * (1.0 + gate * (1.0 - sg)))).astype(BF16)
        drecn = dmr * (gate * sg)
        dgh_ref[...] += _colsum(drecn * rn)
        drn = drecn * ghv
        prod = drn * rn
        mean = jnp.concatenate(
            [jnp.broadcast_to(jnp.sum(prod[:, h_ * HGRN_DIM:(h_ + 1) * HGRN_DIM], axis=-1, keepdims=True),
                              (tm, HGRN_DIM)) for h_ in range(HGRN_HEADS)], axis=1) * (1.0 / HGRN_DIM)
        dr_ref[...] = rr * (drn - rn * mean)

        @pl.when(i == nb - 1)
        def _():
            dwo_ref[...] = acc[...].astype(BF16)

    row = lambda w: pl.BlockSpec((tm, w), lambda i: (i, 0))
    return pl.pallas_call(
        body, name="up_out_bwd", grid=(nb,),
        in_specs=[row(2 * D_FF), ANY, row(D_MODEL), _full((1, D_MODEL)),
                  row(D_MODEL), _once((D_MODEL, D_MODEL)), row(D_MODEL), row(ATTN_W), row(HGRN_W),
                  pl.BlockSpec((tm, HGRN_W), lambda i: (i, 3)), _full((1, ATTN_W)), _full((1, HGRN_W))],
        out_specs=[row(D_MODEL), _full((1, D_MODEL)), row(ATTN_W), row(HGRN_W), row(HGRN_W),
                   _full((1, ATTN_W)), _full((1, HGRN_W)), _once((D_MODEL, D_MODEL))],
        out_shape=[jax.ShapeDtypeStruct((T, D_MODEL), F32), jax.ShapeDtypeStruct((1, D_MODEL), F32),
                   jax.ShapeDtypeStruct((T, ATTN_W), F32), jax.ShapeDtypeStruct((T, HGRN_W), F32),
                   jax.ShapeDtypeStruct((T, HGRN_W), BF16), jax.ShapeDtypeStruct((1, ATTN_W), F32),
                   jax.ShapeDtypeStruct((1, HGRN_W), F32), jax.ShapeDtypeStruct((D_MODEL, D_MODEL), BF16)],
        scratch_shapes=[pltpu.VMEM((D_MODEL, 2 * D_FF), BF16), pltpu.SemaphoreType.DMA((N_CHIPS,)),
                        pltpu.VMEM((D_MODEL, D_MODEL), F32)],
        compiler_params=_cp("arbitrary"),
    )(dgv, w_up4, h1, g2, dh2, w_out, mixed, attn_o, rec_o, hg, g_a, g_h)


def _in_bwd(dqkv, dhg, w_in4, u1, x, g1, dh1, tm=256):
    T = x.shape[0]
    nb = T // tm

    def body(*refs):
        parts = refs[:7]
        w_hbm, u_ref, x_ref, g_ref, dh1_ref, dw_ref, dx_ref, dg_ref, w_full, sem, acc = refs[7:]
        i = pl.program_id(0)

        @pl.when(i == 0)
        def _():
            dg_ref[...] = jnp.zeros_like(dg_ref)
            acc[...] = jnp.zeros_like(acc)
            _load_side_by_side(w_hbm, w_full, sem)

        dp = jnp.concatenate([p[...] for p in parts], axis=1)
        acc[...] += _dot_tn(u_ref[...], dp)
        du = _dot_nt(dp, w_full[...])
        xv = x_ref[...]
        r = _rms(xv, D_MODEL)
        n = xv * r
        dg_ref[...] += _colsum(du * n)
        dx_ref[...] = dh1_ref[...] + _rms_bwd(du * g_ref[...], n, r, D_MODEL)

        @pl.when(i == nb - 1)
        def _():
            for k in range(N_CHIPS):
                dw_ref[k] = acc[:, k * IN_SHARD:(k + 1) * IN_SHARD].astype(BF16)

    row = lambda w: pl.BlockSpec((tm, w), lambda i: (i, 0))
    return pl.pallas_call(
        body, name="in_bwd", grid=(nb,),
        in_specs=[row(ATTN_W)] * 7 + [ANY, row(D_MODEL), row(D_MODEL), _full((1, D_MODEL)), row(D_MODEL)],
        out_specs=[_once((N_CHIPS, D_MODEL, IN_SHARD)), row(D_MODEL), _full((1, D_MODEL))],
        out_shape=[jax.ShapeDtypeStruct((N_CHIPS, D_MODEL, IN_SHARD), BF16), jax.ShapeDtypeStruct((T, D_MODEL), F32),
                   jax.ShapeDtypeStruct((1, D_MODEL), F32)],
        scratch_shapes=[pltpu.VMEM((D_MODEL, IN_TOTAL), BF16), pltpu.SemaphoreType.DMA((N_CHIPS,)),
                        pltpu.VMEM((D_MODEL, IN_TOTAL), F32)],
        compiler_params=_cp("arbitrary"),
    )(*dqkv, *dhg, w_in4, u1, x, g1, dh1)


def _dw(a, b, kb, nb_, name, tk=1024, side=1):
    T, K = a.shape
    N = b.shape[1]
    nk, nn, nt = K // kb, N // (nb_ * side), T // tk

    def body(a_ref, b_ref, o_ref, acc):
        t = pl.program_id(2)

        @pl.when(t == 0)
        def _():
            acc[...] = jnp.zeros_like(acc)

        acc[...] += _dot_tn(a_ref[...], b_ref[...].astype(BF16))

        @pl.when(t == nt - 1)
        def _():
            for s in range(side):
                o_ref[s] = acc[:, s * nb_:(s + 1) * nb_].astype(BF16)

    return pl.pallas_call(
        body, name=name, grid=(nk, nn, nt),
        in_specs=[pl.BlockSpec((tk, kb), lambda i, j, t: (t, i)),
                  pl.BlockSpec((tk, nb_ * side), lambda i, j, t: (t, j))],
        out_specs=pl.BlockSpec((side, kb, nb_), lambda i, j, t: (i * nn + j, 0, 0)),
        out_shape=jax.ShapeDtypeStruct((nk * nn * side, kb, nb_), BF16),
        scratch_shapes=[pltpu.VMEM((kb, nb_ * side), F32)],
        compiler_params=_cp("arbitrary", "arbitrary", "arbitrary"),
    )(a, b)


def _step_channel(a, x, tgt, g_a, g_h, w_out, g2, w_up4, conv_w, conv_b, w_down, gf):
    h1, mixed, u2, gate, val, conv, act, dh2, loss, dgf = _mlp_fwd(
        a["attn_o"], a["rec_o"], a["hg"], x, g_a, g_h, w_out, g2, w_up4, conv_w, conv_b, w_down, gf, tgt)
    dgv, dcw, dcb, dw_down = _mlp_bwd(dh2, gate, val, conv, act, conv_w, w_down)
    dw_down = dw_down.reshape(N_CHIPS, D_FF // N_CHIPS, D_MODEL)
    dh1, dg2, da, dr, dgt, dga, dgh, dw_out = _up_out_bwd(dgv, w_up4, h1, g2, dh2, w_out, mixed, a["attn_o"],
                                                          a["rec_o"], a["hg"], g_a, g_h)
    dw_up = _dw(u2, dgv, D_MODEL, UP_SHARD, "dw_up", side=2)
    dw_out = dw_out.reshape(N_CHIPS, D_MODEL // N_CHIPS, D_MODEL)
    return dict(loss=loss, dgf=dgf, dcw=dcw, dcb=dcb, dg2=dg2, dga=dga, dgh=dgh, dh1=dh1, da=da, dr=dr, dgt=dgt,
                dw_down=dw_down, dw_up=dw_up, dw_out=dw_out)


def _step_mixers_bwd(a, b, x, g1, w_in4, lb, dqkv):
    dhq, dhf, dhi, dlb = _hgrn_bwd(a["hg"], lb, a["states"], b["dr"])
    dw_in, dx, dg1 = _in_bwd(dqkv, [dhq, dhf, dhi, b["dgt"]], w_in4, a["u1"], x, g1, b["dh1"])
    return dict(dx=dx, dg1=dg1, dlb=dlb, dw_in=dw_in)


BIG = ("w_in", "w_out", "w_up", "w_down")
ANY = pl.BlockSpec(memory_space=pl.ANY)


def _place():
    x, y, c = lax.axis_index("x"), lax.axis_index("y"), lax.axis_index("c")
    chips = [(1 - x, y), (x, 1 - y), (1 - x, 1 - y)]
    return x, y, c, chips


def _remote(src, dst, send_sems, recv_sems, k, to):
    return pltpu.make_async_remote_copy(src_ref=src, dst_ref=dst, send_sem=send_sems.at[k], recv_sem=recv_sems.at[k],
                                        device_id=to, device_id_type=MESH)


def _gather_weights(shards, conv_w):
    n = len(shards)
    halves = [s.shape[0] // 2 for s in shards]

    def body(*refs):
        ins, cw, outs, ocw = refs[:n], refs[n], refs[n + 1:2 * n + 1], refs[2 * n + 1]
        send_sems, recv_sems = refs[2 * n + 2:]
        x, y, c, chips = _place()
        me, sibling = 2 * x + y, (x, y, 1 - c)

        def part(w, chip, half):
            return outs[w].at[chip, pl.ds(half * halves[w], halves[w]), :]

        sent = []
        for j, chip in enumerate(chips):
            for w in range(n):
                sent.append(_remote(ins[w].at[pl.ds(c * halves[w], halves[w]), :], part(w, me, c),
                                    send_sems, recv_sems, w * 3 + j, (*chip, c)))
            sent.append(_remote(cw, ocw.at[me], send_sems, recv_sems, 6 * n + j, (*chip, c)))
        for cp in sent:
            cp.start()
        for j, chip in enumerate(chips):
            kj = 2 * chip[0] + chip[1]
            for w in range(n):
                _remote(part(w, kj, c), part(w, kj, c), send_sems, recv_sems, w * 3 + j, (*chip, c)).wait_recv()
                fwd = _remote(part(w, kj, c), part(w, kj, c), send_sems, recv_sems, 3 * n + w * 3 + j, sibling)
                fwd.start()
                sent.append(fwd)
        for j, chip in enumerate(chips):
            kj = 2 * chip[0] + chip[1]
            for w in range(n):
                _remote(part(w, kj, 1 - c), part(w, kj, 1 - c), send_sems, recv_sems, 3 * n + w * 3 + j,
                        sibling).wait_recv()
            _remote(cw, ocw.at[kj], send_sems, recv_sems, 6 * n + j, (*chip, c)).wait_recv()
        for cp in sent:
            cp.wait_send()

    n_sem = 6 * n + 3
    outs = pl.pallas_call(
        body, name="gather_weights",
        in_specs=[ANY] * (n + 1), out_specs=[ANY] * (n + 1),
        out_shape=[jax.ShapeDtypeStruct((N_CHIPS,) + s.shape, s.dtype) for s in shards]
        + [jax.ShapeDtypeStruct((N_CHIPS,) + conv_w.shape, conv_w.dtype)],
        scratch_shapes=[pltpu.SemaphoreType.DMA((n_sem,)), pltpu.SemaphoreType.DMA((n_sem,))],
    )(*shards, conv_w)
    chip = 2 * lax.axis_index("x") + lax.axis_index("y")
    return [lax.dynamic_update_slice(o, s[None], (chip,) + (0,) * s.ndim) for o, s in zip(outs, [*shards, conv_w])]


def _allreduce_small(buf):
    rows = buf.shape[0]

    def body(in_ref, out_ref, slots, send_sems, recv_sems):
        x, y, c, _ = _place()
        me = 4 * x + 2 * y + c
        slots[me] = in_ref[...]
        sent = []
        for p in range(1, 8):
            to = (x ^ (p >> 2), y ^ ((p >> 1) & 1), c ^ (p & 1))
            sent.append(_remote(in_ref, slots.at[me], send_sems, recv_sems, p, to))
        for cp in sent:
            cp.start()
        for p in range(1, 8):
            frm = 4 * (x ^ (p >> 2)) + 2 * (y ^ ((p >> 1) & 1)) + (c ^ (p & 1))
            _remote(in_ref, slots.at[frm], send_sems, recv_sems, p, (x, y, c)).wait_recv()
        for cp in sent:
            cp.wait_send()
        acc = slots[0]
        for d in range(1, 8):
            acc = acc + slots[d]
        out_ref[...] = acc

    vm = pl.BlockSpec(memory_space=pltpu.VMEM)
    return pl.pallas_call(
        body, name="allreduce_small", in_specs=[vm], out_specs=vm,
        out_shape=jax.ShapeDtypeStruct(buf.shape, F32),
        scratch_shapes=[pltpu.VMEM((8, rows, 128), F32), pltpu.SemaphoreType.DMA((8,)), pltpu.SemaphoreType.DMA((8,))],
    )(buf)


def _sibling_peer():
    x, y, c, _ = _place()
    return [(x, y, 1 - c)]


def _chip_peers():
    x, y, c, chips = _place()
    return [(*chip, c) for chip in chips]


def _handshake(peers):
    barrier = pltpu.get_barrier_semaphore()
    for peer in peers:
        pl.semaphore_signal(barrier, inc=1, device_id=peer, device_id_type=MESH)
    pl.semaphore_wait(barrier, len(peers))


def _pair_exchange(gs, name, barrier_id):
    n = len(gs)
    halves = [g.shape[1] // 2 for g in gs]

    def body(*refs):
        g, got = refs[:n], refs[n:2 * n]
        send_sems, recv_sems = refs[2 * n:]
        _handshake(_sibling_peer())
        x, y, c, _ = _place()
        cps = [_remote(g[w].at[:, pl.ds((1 - c) * halves[w], halves[w]), :], got[w], send_sems, recv_sems, w,
                       (x, y, 1 - c)) for w in range(n)]
        for cp in cps:
            cp.start()
        for cp in cps:
            cp.wait()

    return pl.pallas_call(
        body, name=name, in_specs=[ANY] * n, out_specs=[ANY] * n,
        out_shape=[jax.ShapeDtypeStruct((N_CHIPS, h, g.shape[2]), g.dtype) for g, h in zip(gs, halves)],
        scratch_shapes=[pltpu.SemaphoreType.DMA((n,)), pltpu.SemaphoreType.DMA((n,))],
        compiler_params=pltpu.CompilerParams(collective_id=barrier_id),
    )(*gs)


def _core_id():
    return lax.axis_index("c").reshape(1).astype(jnp.int32)


def _pair_sum(g, got, name):
    h, C = got.shape[1:]

    def body(c_ref, g_ref, b_ref, o_ref):
        o_ref[...] = (g_ref[...].astype(F32) + b_ref[...].astype(F32)).astype(BF16)

    blk = pl.BlockSpec((1, h, C), lambda k, c_ref: (k, 0, 0))
    return pl.pallas_call(
        body, name=name,
        grid_spec=pltpu.PrefetchScalarGridSpec(
            num_scalar_prefetch=1, grid=(N_CHIPS,),
            in_specs=[pl.BlockSpec((1, h, C), lambda k, c_ref: (k, c_ref[0], 0)), blk], out_specs=blk),
        out_shape=jax.ShapeDtypeStruct(got.shape, BF16), compiler_params=_cp("arbitrary"))(_core_id(), g, got)


def _sum_partials(g, got, landed, name):
    h, C = got.shape[1:]

    def body(ids, g_ref, b_ref, l_ref, o_ref):
        acc = g_ref[0].astype(F32) + b_ref[0].astype(F32)
        for j in range(3):
            acc = acc + l_ref[j].astype(F32)
        o_ref[...] = acc

    ids = jnp.stack([2 * lax.axis_index("x") + lax.axis_index("y"), lax.axis_index("c")]).astype(jnp.int32)
    return pl.pallas_call(
        body, name=name,
        grid_spec=pltpu.PrefetchScalarGridSpec(
            num_scalar_prefetch=1, grid=(1,),
            in_specs=[pl.BlockSpec((1, h, C), lambda i, ids: (ids[0], ids[1], 0)),
                      pl.BlockSpec((1, h, C), lambda i, ids: (ids[0], 0, 0)),
                      pl.BlockSpec((3, h, C), lambda i, ids: (0, 0, 0))],
            out_specs=pl.BlockSpec((h, C), lambda i, ids: (ids[1], 0))),
        out_shape=jax.ShapeDtypeStruct((2 * h, C), F32), compiler_params=_cp("arbitrary"))(ids, g, got, landed)


def _pair_share(reds, name, barrier_id):
    n = len(reds)

    def body(*refs):
        out = refs[n:2 * n]
        send_sems, recv_sems = refs[2 * n:]
        _handshake(_sibling_peer())
        x, y, c, _ = _place()
        def half(w, which):
            h = out[w].shape[0] // 2
            return out[w].at[pl.ds(which * h, h), :]

        cps = [_remote(half(w, c), half(w, c), send_sems, recv_sems, w, (x, y, 1 - c)) for w in range(n)]
        for cp in cps:
            cp.start()
        for w in range(n):
            _remote(half(w, 1 - c), half(w, 1 - c), send_sems, recv_sems, w, (x, y, 1 - c)).wait_recv()
        for cp in cps:
            cp.wait_send()

    return pl.pallas_call(
        body, name=name, in_specs=[ANY] * n, out_specs=[ANY] * n,
        out_shape=[jax.ShapeDtypeStruct(r.shape, F32) for r in reds],
        input_output_aliases={w: w for w in range(n)},
        scratch_shapes=[pltpu.SemaphoreType.DMA((n,)), pltpu.SemaphoreType.DMA((n,))],
        compiler_params=pltpu.CompilerParams(collective_id=barrier_id),
    )(*reds)


HBM = pl.BlockSpec(memory_space=pltpu.HBM)
SEM = pl.BlockSpec(memory_space=pltpu.SEMAPHORE)
DATAFLOW = pltpu.SideEffectType.DATAFLOW_SIDE_EFFECTING


def _copies_start(name, srcs, lands, plan, n_copies, after, peers, barrier_id):
    ns, nb, na = len(srcs), len(srcs) + len(lands), len(after)

    def body(*refs):
        src_refs, land_refs = refs[:ns], refs[ns:nb]
        send_sems, recv_sems = refs[nb + na:nb + na + 2]
        token = refs[-1]
        _handshake(peers())
        for k, (src, there, _, to) in enumerate(plan(src_refs, land_refs)):
            _remote(src, there, send_sems, recv_sems, k, to).start()
        token[...] = jnp.zeros_like(token)

    hbm = lambda a: pltpu.HBM(a.shape, a.dtype)
    outs = pl.pallas_call(
        body, name=name,
        out_shape=(pltpu.SemaphoreType.DMA((n_copies,)), pltpu.SemaphoreType.DMA((n_copies,)),
                   *[hbm(a) for a in srcs], *[hbm(a) for a in lands], jax.ShapeDtypeStruct((8, 128), F32)),
        in_specs=[HBM] * nb + [ANY] * na,
        out_specs=(SEM, SEM, *[HBM] * nb, pl.BlockSpec(memory_space=pltpu.VMEM)),
        input_output_aliases={i: 2 + i for i in range(nb)},
        compiler_params=pltpu.CompilerParams(has_side_effects=DATAFLOW, collective_id=barrier_id),
    )(*[pltpu.with_memory_space_constraint(a, pltpu.HBM) for a in (*srcs, *lands)], *after)
    return outs[0], outs[1], outs[2:2 + ns], outs[2 + ns:2 + nb], outs[-1]


def _copies_wait(name, send_sems, recv_sems, srcs, lands, plan, after):
    ns, nb, na = len(srcs), len(srcs) + len(lands), len(after)

    def body(*refs):
        src_refs, land_refs = refs[:ns], refs[ns:nb]
        send_sems, recv_sems = refs[nb:nb + 2]
        for k, (src, _, here, to) in enumerate(plan(src_refs, land_refs)):
            cp = _remote(src, here, send_sems, recv_sems, k, to)
            cp.wait_send()
            cp.wait_recv()

    hbm = lambda a: pltpu.HBM(a.shape, a.dtype)
    outs = pl.pallas_call(
        body, name=name,
        out_shape=(*[hbm(a) for a in srcs], *[hbm(a) for a in lands]),
        in_specs=[HBM] * nb + [SEM, SEM] + [ANY] * na,
        out_specs=tuple([HBM] * nb),
        input_output_aliases={i: i for i in range(nb)},
        compiler_params=pltpu.CompilerParams(has_side_effects=DATAFLOW),
    )(*srcs, *lands, send_sems, recv_sems, *after)
    return outs[:ns], outs[ns:]


def _gather_plan(halves):
    def plan(shards, lands):
        x, y, c, chips = _place()
        me = 2 * x + y
        copies = []
        for w, h in enumerate(halves):
            rows = pl.ds(c * h, h)
            for chip in chips:
                copies.append((shards[w].at[rows, :], lands[w].at[me, rows, :],
                               lands[w].at[2 * chip[0] + chip[1], rows, :], (*chip, c)))
        return copies
    return plan


def _reduce_plan(n):
    def plan(ps, lands):
        x, y, c, chips = _place()
        return [(ps[w].at[2 * chip[0] + chip[1]], lands[w].at[j], lands[w].at[j], (*chip, c))
                for w in range(n) for j, chip in enumerate(chips)]
    return plan


def _forward_plan(halves):
    def plan(_, lands):
        x, y, c, chips = _place()

        def part(w, chip, half):
            return lands[w].at[2 * chip[0] + chip[1], pl.ds(half * halves[w], halves[w]), :]

        return [(part(w, chip, c), part(w, chip, c), part(w, chip, 1 - c), (x, y, 1 - c))
                for w in range(len(halves)) for chip in chips]
    return plan


def _pair_plan(halves):
    def plan(gs, gots):
        x, y, c, _ = _place()
        return [(gs[w].at[:, pl.ds((1 - c) * h, h), :], gots[w], gots[w], (x, y, 1 - c)) for w, h in enumerate(halves)]
    return plan


def _place_own(gathered, shards):
    chip = 2 * lax.axis_index("x") + lax.axis_index("y")
    return [lax.dynamic_update_slice(o, s[None], (chip, 0, 0)) for o, s in zip(gathered, shards)]


def _adamw(w, g, m, v, name, tr=None):
    R, C = w.shape
    tr = tr or R // 4

    def body(w_ref, g_ref, m_ref, v_ref, d_ref, nm_ref, nv_ref):
        d_ref[...], nm_ref[...], nv_ref[...] = _adamw_math(w_ref[...], g_ref[...], m_ref[...], v_ref[...])

    blk = pl.BlockSpec((tr, C), lambda i: (i, 0))
    return pl.pallas_call(body, name=name, grid=(R // tr,), in_specs=[blk] * 4, out_specs=[blk] * 3,
                          out_shape=[jax.ShapeDtypeStruct((R, C), F32)] * 3, compiler_params=_cp("arbitrary"))(w, g, m, v)


SMALL = (("norm1_g", 1, 1024), ("attn_norm_g", 1, 512), ("hgrn_norm_g", 1, 512), ("hgrn_lb_logits", 2, 512),
         ("norm2_g", 1, 1024), ("conv_b", 1, D_FF), ("final_norm_g", 1, 1024), ("conv_w", 3, D_FF))
LOSS_ROW = sum(r * c for _, r, c in SMALL) // 128
SMALL_ROWS = 136


def _rows_to_lanes(ref, row, width):
    return jnp.concatenate([ref[row + j:row + j + 1, :] for j in range(width // 128)], axis=1)


def _pack_small(grads, dlb, lb, loss):
    def body(*refs):
        parts, dlb_ref, lb_ref, loss_ref, out = refs[:len(SMALL) - 1], refs[-4], refs[-3], refs[-2], refs[-1]
        out[...] = jnp.zeros_like(out)
        lbv = lb_ref[...]
        dl = dlb_ref[...] * lbv * (1.0 - lbv)
        row = 0
        parts = list(parts)
        for name, rows, width in SMALL:
            for r in range(rows):
                if name == "hgrn_lb_logits":
                    src = dl if r == 0 else -dl
                    for j in range(width // 128):
                        out[row + j:row + j + 1, :] = src[:, 128 * j:128 * (j + 1)]
                else:
                    for j in range(width // 128):
                        out[row + j:row + j + 1, :] = parts[0][r:r + 1, 128 * j:128 * (j + 1)]
                row += width // 128
            if name != "hgrn_lb_logits":
                parts.pop(0)
        out[LOSS_ROW:LOSS_ROW + 1, :] = loss_ref[...]

    vm = pl.BlockSpec(memory_space=pltpu.VMEM)
    return pl.pallas_call(body, name="pack_small", in_specs=[vm] * (len(grads) + 3), out_specs=vm,
                          out_shape=jax.ShapeDtypeStruct((SMALL_ROWS, 128), F32))(*grads, dlb, lb, loss)


def _adamw_math(w, g, m, v):
    nm = ADAM_B1 * m + (1.0 - ADAM_B1) * g
    nv = ADAM_B2 * v + (1.0 - ADAM_B2) * (g * g)
    m_hat = nm / (1.0 - ADAM_B1 ** ADAM_STEP)
    v_hat = nv / (1.0 - ADAM_B2 ** ADAM_STEP)
    return -ADAM_LR * (m_hat / (jnp.sqrt(v_hat) + ADAM_EPS) + ADAM_WD * w), nm, nv


def _small_update(summed, g_conv_w, ws, ms, vs):
    n = len(SMALL)

    def body(*refs):
        s_ref, gcw_ref = refs[:2]
        w_refs, m_refs, v_refs = refs[2:2 + n], refs[2 + n:2 + 2 * n], refs[2 + 2 * n:2 + 3 * n]
        outs = refs[2 + 3 * n:]
        row = 0
        for k, (name, rows, width) in enumerate(SMALL):
            if name == "conv_w":
                g = gcw_ref[...]
            else:
                g = jnp.concatenate([_rows_to_lanes(s_ref, row + r * (width // 128), width) for r in range(rows)], axis=0)
            row += rows * (width // 128)
            d, nm, nv = _adamw_math(w_refs[k][...], g, m_refs[k][...], v_refs[k][...])
            for o, val in zip(outs[4 * k:4 * k + 4], (g, d, nm, nv)):
                o[...] = val

    vm = pl.BlockSpec(memory_space=pltpu.VMEM)
    outs = pl.pallas_call(
        body, name="small_update", in_specs=[vm] * (2 + 3 * n), out_specs=[vm] * (4 * n),
        out_shape=[jax.ShapeDtypeStruct(a.shape, F32) for a in ws for _ in range(4)],
    )(summed, g_conv_w, *ws, *ms, *vs)
    return [outs[4 * k:4 * k + 4] for k in range(n)]


def kernel(x, norm1_g, w_in, attn_norm_g, hgrn_norm_g, hgrn_lb_logits, w_out, norm2_g, w_up, conv_w, conv_b, w_down, final_norm_g, loss_target, m_norm1_g, m_w_in, m_attn_norm_g, m_hgrn_norm_g, m_hgrn_lb_logits, m_w_out, m_norm2_g, m_w_up, m_conv_w, m_conv_b, m_w_down, m_final_norm_g, v_norm1_g, v_w_in, v_attn_norm_g, v_hgrn_norm_g, v_hgrn_lb_logits, v_w_out, v_norm2_g, v_w_up, v_conv_w, v_conv_b, v_w_down, v_final_norm_g):
    w = dict(norm1_g=norm1_g, w_in=w_in, attn_norm_g=attn_norm_g, hgrn_norm_g=hgrn_norm_g,
             hgrn_lb_logits=hgrn_lb_logits, w_out=w_out, norm2_g=norm2_g, w_up=w_up, conv_w=conv_w, conv_b=conv_b,
             w_down=w_down, final_norm_g=final_norm_g)
    m = dict(norm1_g=m_norm1_g, w_in=m_w_in, attn_norm_g=m_attn_norm_g, hgrn_norm_g=m_hgrn_norm_g,
             hgrn_lb_logits=m_hgrn_lb_logits, w_out=m_w_out, norm2_g=m_norm2_g, w_up=m_w_up, conv_w=m_conv_w,
             conv_b=m_conv_b, w_down=m_w_down, final_norm_g=m_final_norm_g)
    v = dict(norm1_g=v_norm1_g, w_in=v_w_in, attn_norm_g=v_attn_norm_g, hgrn_norm_g=v_hgrn_norm_g,
             hgrn_lb_logits=v_hgrn_lb_logits, w_out=v_w_out, norm2_g=v_norm2_g, w_up=v_w_up, conv_w=v_conv_w,
             conv_b=v_conv_b, w_down=v_w_down, final_norm_g=v_final_norm_g)
    names = list(w)
    chip = 2 * lax.axis_index("x") + lax.axis_index("y")

    shards = {k: w[k][0].astype(BF16) for k in BIG}
    w_in4, conv_w4 = _gather_weights([shards["w_in"]], conv_w[0])
    conv_w_full = jnp.transpose(conv_w4, (1, 0, 2)).reshape(3, D_FF)
    lb = jax.nn.softmax(hgrn_lb_logits, axis=0)[0:1]
    late = [shards[k] for k in BIG[1:]]
    gather_plan = _gather_plan([s.shape[0] // 2 for s in late])
    started = _copies_start("gather_start", late, [lax.empty((N_CHIPS,) + s.shape, BF16) for s in late], gather_plan,
                            3 * len(late), after=(w_in4,), peers=_chip_peers, barrier_id=0)
    u1, qkv, hg = _in_proj(x[0], norm1_g + started[4][0:1, 0:1], w_in4)
    attn_o, lse = _attn_fwd(qkv)
    late, landed_w = _copies_wait("gather_wait", *started[:4], gather_plan, after=(attn_o,))
    forward_plan = _forward_plan([s.shape[0] // 2 for s in late])
    started = _copies_start("forward_start", [], landed_w, forward_plan, 3 * len(late), after=(),
                            peers=_sibling_peer, barrier_id=1)
    rec_o, states = _hgrn_fwd(hg, lb + started[4][0:1, 0:1])
    a = dict(u1=u1, qkv=qkv, hg=hg, attn_o=attn_o, lse=lse, rec_o=rec_o, states=states)
    w_out4, w_up4, w_down4 = _place_own(
        _copies_wait("forward_wait", *started[:4], forward_plan, after=(rec_o,))[1], late)

    b = _step_channel(a, x[0], loss_target[0], attn_norm_g, hgrn_norm_g, w_out4.reshape(D_MODEL, D_MODEL), norm2_g,
                      w_up4, conv_w_full, conv_b, w_down4.reshape(D_FF, D_MODEL), final_norm_g.reshape(1, D_MODEL))

    early = [b["dw_out"], b["dw_up"], b["dw_down"]]
    pair_plan = _pair_plan([gk.shape[1] // 2 for gk in early])
    started = _copies_start("pair_start", early,
                            [lax.empty((N_CHIPS, gk.shape[1] // 2, gk.shape[2]), BF16) for gk in early], pair_plan,
                            len(early), after=(), peers=_sibling_peer, barrier_id=2)
    dqkv = _attn_bwd(qkv, attn_o, lse, b["da"], started[4])
    early, gots = _copies_wait("pair_wait", *started[:4], pair_plan, after=(dqkv[0],))
    ps = [_pair_sum(gk, got, f"pair_sum_{k}") for gk, got, k in zip(early, gots, BIG[1:])]
    reduce_plan = _reduce_plan(len(ps))
    started = _copies_start("reduce_start", ps, [lax.empty((3,) + p.shape[1:], BF16) for p in ps], reduce_plan,
                            3 * len(ps), after=(), peers=_chip_peers, barrier_id=3)
    c = _step_mixers_bwd(a, b, x[0], norm1_g, w_in4, lb + started[4][0:1, 0:1], dqkv)
    gots_in = _pair_exchange([c["dw_in"]], "pair_exchange_w_in", barrier_id=4)
    ps_in = _pair_sum(c["dw_in"], gots_in[0], "pair_sum_w_in")
    plan_in = _reduce_plan(1)
    started_in = _copies_start("reduce_start_w_in", [ps_in], [lax.empty((3,) + ps_in.shape[1:], BF16)], plan_in, 3,
                               after=(), peers=_chip_peers, barrier_id=5)
    landed = _copies_wait("reduce_wait", *started[:4], reduce_plan, after=(started_in[4],))[1]
    reds = [_sum_partials(gk, got, l, f"sum_partials_{k}") for gk, got, l, k in zip(early, gots, landed, BIG[1:])]
    g = dict(zip(BIG[1:], _pair_share(reds, "pair_share", barrier_id=6)))
    delta, new_m, new_v = {}, {}, {}
    for k in BIG[1:]:
        delta[k], new_m[k], new_v[k] = _adamw(w[k][0], g[k], m[k][0], v[k][0], f"adamw_{k}")

    loss, dx = b["loss"], c["dx"]
    small = dict(g1=c["dg1"], g_a=b["dga"], g_h=b["dgh"], lb=c["dlb"], g2=b["dg2"], conv_w=b["dcw"], conv_b=b["dcb"],
                 gf=b["dgf"])
    summed = _allreduce_small(_pack_small(
        [small["g1"], small["g_a"], small["g_h"], small["g2"], small["conv_b"], small["gf"], small["conv_w"]],
        small["lb"], lb, loss))
    loss_total = summed[LOSS_ROW, 0]
    g_conv_w = lax.dynamic_slice(summed[LOSS_ROW - 3 * D_FF // 128:LOSS_ROW].reshape(3, D_FF),
                                 (0, chip * (D_FF // N_CHIPS)), (3, D_FF // N_CHIPS))
    two_d = lambda p, k: p[k].reshape(-1, p[k].shape[-1])
    updated = _small_update(summed, g_conv_w, *[[two_d(p, k) for k, _, _ in SMALL] for p in (w, m, v)])
    for (k, _, _), parts in zip(SMALL, updated):
        g[k], delta[k], new_m[k], new_v[k] = (a.reshape(w[k].shape) for a in parts)

    landed_in = _copies_wait("reduce_wait_w_in", *started_in[:4], plan_in, after=(updated[0][1], delta["w_up"]))[1]
    red_in = _sum_partials(c["dw_in"], gots_in[0], landed_in[0], "sum_partials_w_in")
    g["w_in"] = _pair_share([red_in], "pair_share_w_in", barrier_id=7)[0]
    delta["w_in"], new_m["w_in"], new_v["w_in"] = _adamw(w_in[0], g["w_in"], m_w_in[0], v_w_in[0], "adamw_w_in")
    for k in BIG:
        g[k], delta[k], new_m[k], new_v[k] = g[k][None], delta[k][None], new_m[k][None], new_v[k][None]

    return (loss_total, dx[None], *[g[k] for k in names], *[delta[k] for k in names],
            *[new_m[k] for k in names], *[new_v[k] for k in names])
```

```python
import math

import jax
import jax.numpy as jnp
from jax import lax
from jax.experimental import pallas as pl
from jax.experimental.pallas import tpu as pltpu

F32 = jnp.float32
BF16 = jnp.bfloat16

D_MODEL = 1024
ATTN_W = 512
HGRN_W = 512
HEAD_PAIR = 128
ATTN_BLK = 128
DILATIONS = (1, 4, 16)
ATTN_CHAINS = 4
ATTN_CHAINS_FWD = 8
HGRN_HEADS = 4
HGRN_DIM = 128
HGRN_CHUNK = 64
SUPER = 256
HGRN_SIDE = 8
D_FF = 2816
FF_CHUNKS = ((0, 1536), (1536, D_FF))
MLP_BWD_CHUNKS = ((0, 768), (768, 1408), (1408, 2176), (2176, D_FF))
N_CHIPS = 4
IN_TOTAL = 3584
IN_SHARD = IN_TOTAL // N_CHIPS
UP_SHARD = 2 * D_FF // N_CHIPS
QKV_W = 3 * ATTN_W
HG_W = 4 * HGRN_W
EPS = 1e-6
NEG = -1e30
V7X_VMEM_BYTES = 64 * 1024 * 1024
VMEM_LIMIT = V7X_VMEM_BYTES - 8 * 1024 * 1024

ADAM_LR = 0.001
ADAM_B1 = 0.9
ADAM_B2 = 0.999
ADAM_EPS = 1e-08
ADAM_WD = 0.01
ADAM_STEP = 10

MESH = pl.DeviceIdType.MESH


def _cp(*sem):
    return pltpu.CompilerParams(dimension_semantics=sem or None, vmem_limit_bytes=VMEM_LIMIT)


def _dot(a, b):
    return jnp.dot(a, b, preferred_element_type=F32)


def _dot_nt(a, b):
    return lax.dot_general(a, b, (((1,), (1,)), ((), ())), preferred_element_type=F32)


def _dot_tn(a, b):
    return lax.dot_general(a, b, (((0,), (0,)), ((), ())), preferred_element_type=F32)


def _sigmoid(x):
    return 1.0 / (1.0 + jnp.exp(-x))


def _rms(x, width):
    return lax.rsqrt(jnp.sum(x * x, axis=-1, keepdims=True) * (1.0 / width) + EPS)


def _rms_bwd(dn, n, r, width):
    return r * (dn - n * (jnp.sum(dn * n, axis=-1, keepdims=True) * (1.0 / width)))


def _colsum(x):
    return jnp.sum(x, axis=0, keepdims=True)


def _row(v, k):
    rid = lax.broadcasted_iota(jnp.int32, v.shape, 0)
    return jnp.sum(jnp.where(rid == k, v, 0.0), axis=0, keepdims=True)


def _full(shape):
    return pl.BlockSpec(shape, lambda *_: (0,) * len(shape))


def _once(shape):
    return pl.BlockSpec(shape, lambda *_: (0,) * len(shape), pipeline_mode=pl.Buffered(1))


def _load_side_by_side(w_hbm, w_full, sem):
    width = w_hbm.shape[2]
    cps = [pltpu.make_async_copy(w_hbm.at[k], w_full.at[:, pl.ds(k * width, width)], sem.at[k]) for k in range(N_CHIPS)]
    for cp in cps:
        cp.start()
    for cp in cps:
        cp.wait()


def _in_proj(x, g1, w_in4, tm=512):
    T = x.shape[0]

    def body(x_ref, g_ref, w_hbm, u_ref, qkv_ref, hg_ref, w_full, sem):
        @pl.when(pl.program_id(0) == 0)
        def _():
            _load_side_by_side(w_hbm, w_full, sem)

        xv = x_ref[...]
        u = (xv * _rms(xv, D_MODEL) * g_ref[...]).astype(BF16)
        u_ref[...] = u
        p = _dot(u, w_full[...])
        qkv_ref[...] = p[:, :QKV_W]
        hg_ref[...] = p[:, QKV_W:]

    return pl.pallas_call(
        body, name="in_proj", grid=(T // tm,),
        in_specs=[pl.BlockSpec((tm, D_MODEL), lambda i: (i, 0)), _full((1, D_MODEL)), ANY],
        out_specs=[pl.BlockSpec((tm, D_MODEL), lambda i: (i, 0)), pl.BlockSpec((tm, QKV_W), lambda i: (i, 0)),
                   pl.BlockSpec((tm, HG_W), lambda i: (i, 0))],
        out_shape=[jax.ShapeDtypeStruct((T, D_MODEL), BF16), jax.ShapeDtypeStruct((T, QKV_W), F32),
                   jax.ShapeDtypeStruct((T, HG_W), F32)],
        scratch_shapes=[pltpu.VMEM((D_MODEL, IN_TOTAL), BF16), pltpu.SemaphoreType.DMA((N_CHIPS,))],
        compiler_params=_cp("arbitrary"),
    )(x, g1, w_in4)


def _attn_masks(bias_ref):
    lane = lax.broadcasted_iota(jnp.int32, (ATTN_BLK, HEAD_PAIR), 1)
    row = lax.broadcasted_iota(jnp.int32, (2 * ATTN_BLK, 2 * ATTN_BLK), 0)
    col = lax.broadcasted_iota(jnp.int32, (2 * ATTN_BLK, 2 * ATTN_BLK), 1)
    base = jnp.where(row >= ATTN_BLK, row - ATTN_BLK, row) - col
    for k in range(2):
        dist = base + k * ATTN_BLK
        bias_ref[k] = jnp.where((dist >= 0) & (dist <= ATTN_BLK), 0.0, NEG)
    bias_ref[2] = jnp.where(col >= ATTN_BLK, bias_ref[1], NEG)
    return lane < 64


def _two_heads(blk, first):
    zero = jnp.zeros_like(blk)
    return jnp.concatenate([jnp.where(first, blk, zero), jnp.where(first, zero, blk)], axis=0)


def _attn_rows(idx, nb, d):
    r, n = idx // nb, idx % nb
    kb = jnp.maximum(n - 1, 0)
    if d == 1:
        q0 = pl.multiple_of(n * ATTN_BLK, ATTN_BLK)
        k0 = pl.multiple_of(kb * ATTN_BLK, ATTN_BLK)
        return pl.ds(q0, ATTN_BLK), pl.ds(k0, 2 * ATTN_BLK), n - kb
    return (pl.ds(r + d * ATTN_BLK * n, ATTN_BLK, stride=d), pl.ds(r + d * ATTN_BLK * kb, 2 * ATTN_BLK, stride=d),
            n - kb)


def _attn_fwd(qkv):
    T = qkv.shape[0]

    n_blocks = T // ATTN_BLK

    def body(q_ref, k_ref, v_ref, o_ref, m_ref, l_ref, bias_ref):
        first = _attn_masks(bias_ref)
        for bi, d in enumerate(DILATIONS):
            nb = T // d // ATTN_BLK

            chains = ATTN_CHAINS_FWD
            per_chain = n_blocks // chains
            carried = d > 1 and per_chain % nb == 0

            def block(idx, kept=None, d=d, nb=nb, bi=bi, carried=carried):
                rows, keys, which = _attn_rows(idx, nb, d)
                q2 = _two_heads(q_ref[rows, :] * 0.125, first).astype(BF16)
                if carried:
                    k_own, v_own = k_ref[rows, :].astype(BF16), v_ref[rows, :].astype(BF16)
                    kw = jnp.concatenate([kept[0], k_own], axis=0)
                    vw = jnp.concatenate([kept[1], v_own], axis=0)
                    which = 2 - which
                else:
                    kw = k_ref[keys, :].astype(BF16)
                    vw = v_ref[keys, :].astype(BF16)
                old = (o_ref[rows, :], m_ref[rows, :], l_ref[rows, :]) if bi else None
                s = _dot_nt(q2, kw) + bias_ref[which]
                mb = jnp.max(s, axis=-1, keepdims=True)
                p = jnp.exp(s - mb)
                lb = jnp.sum(p, axis=-1, keepdims=True)
                o2 = _dot(p.astype(BF16), vw)
                o = jnp.where(first, o2[:ATTN_BLK], o2[ATTN_BLK:])
                m = jnp.where(first, mb[:ATTN_BLK], mb[ATTN_BLK:])
                l = jnp.where(first, lb[:ATTN_BLK], lb[ATTN_BLK:])
                if bi:
                    po, pm, pl_ = old
                    mn = jnp.maximum(pm, m)
                    wa = jnp.exp(pm - mn)
                    wb = jnp.exp(m - mn)
                    o, l, m = po * wa + o * wb, pl_ * wa + l * wb, mn
                return (rows, o, m, l), ((k_own, v_own) if carried else 0)

            def step(i, kept, block=block, carried=carried, chains=chains, per_chain=per_chain):
                done = [block(i + ch * per_chain, kept[ch] if carried else None) for ch in range(chains)]
                for (rows, o, m, l), _ in done:
                    o_ref[rows, :] = o
                    m_ref[rows, :] = m
                    l_ref[rows, :] = l
                return tuple(k for _, k in done) if carried else kept

            zero = jnp.zeros((ATTN_BLK, HEAD_PAIR), BF16)
            lax.fori_loop(0, per_chain, step, ((zero, zero),) * chains if carried else 0)

        def finish(i, carry):
            rows = pl.ds(pl.multiple_of(i * SUPER, SUPER), SUPER)
            l = l_ref[rows, :]
            o_ref[rows, :] = o_ref[rows, :] / l
            m_ref[rows, :] = m_ref[rows, :] + jnp.log(l)
            return carry

        lax.fori_loop(0, T // SUPER, finish, 0)

    col = lambda off: pl.BlockSpec((T, HEAD_PAIR), lambda j: (0, off + j))
    return pl.pallas_call(
        body, name="attn_fwd", grid=(4,),
        in_specs=[col(0), col(4), col(8)], out_specs=[col(0), col(0)],
        out_shape=[jax.ShapeDtypeStruct((T, ATTN_W), F32)] * 2,
        scratch_shapes=[pltpu.VMEM((T, HEAD_PAIR), F32), pltpu.VMEM((3, 2 * ATTN_BLK, 2 * ATTN_BLK), F32)],
        compiler_params=_cp("arbitrary"),
    )(qkv, qkv, qkv)


def _attn_bwd(qkv, o, lse, do, token=None):
    T = qkv.shape[0]
    per_chain = T // ATTN_BLK // ATTN_CHAINS
    extra = [] if token is None else [token]

    def body(q_ref, k_ref, v_ref, o_ref, lse_ref, do_ref, *rest):
        outs = rest[len(extra):len(extra) + 3]
        dq_ref, dk_ref, dv_ref, dkb_ref, dvb_ref, bias_ref = rest[len(extra) + 3:]
        first = _attn_masks(bias_ref)
        dq_ref[...] = jnp.zeros_like(dq_ref)
        dk_ref[...] = jnp.zeros_like(dk_ref)
        dv_ref[...] = jnp.zeros_like(dv_ref)

        def grads(rows, kw, vw, which):
            q2 = _two_heads(q_ref[rows, :] * 0.125, first).astype(BF16)
            lse_b = lse_ref[rows, :]
            dob = do_ref[rows, :]
            prod = dob * o_ref[rows, :]
            old = dq_ref[rows, :]
            lse2 = jnp.concatenate(
                [jnp.max(jnp.where(first, lse_b, NEG), axis=-1, keepdims=True),
                 jnp.max(jnp.where(first, NEG, lse_b), axis=-1, keepdims=True)], axis=0)
            p = jnp.exp(_dot_nt(q2, kw) + (bias_ref[which] - lse2))
            delta = jnp.concatenate(
                [jnp.sum(jnp.where(first, prod, 0.0), axis=-1, keepdims=True),
                 jnp.sum(jnp.where(first, 0.0, prod), axis=-1, keepdims=True)], axis=0)
            do2 = _two_heads(dob, first).astype(BF16)
            ds = (p * (_dot_nt(do2, vw) - delta)).astype(BF16)
            dq2 = _dot(ds, kw) * 0.125
            return (old + jnp.where(first, dq2[:ATTN_BLK], dq2[ATTN_BLK:]), _dot_tn(ds, q2),
                    _dot_tn(p.astype(BF16), do2))

        def block(idx):
            rows, keys, which = _attn_rows(idx, T // ATTN_BLK, 1)
            old = dk_ref[keys, :], dv_ref[keys, :]
            dq, ck, cv = grads(rows, k_ref[keys, :].astype(BF16), v_ref[keys, :].astype(BF16), which)
            return rows, keys, dq, old[0] + ck, old[1] + cv

        def step(i, carry):
            done = [block(i + ch * per_chain) for ch in range(ATTN_CHAINS)]
            for rows, keys, dq, dk, dv in done:
                dq_ref[rows, :] = dq
                dk_ref[keys, :] = dk
                dv_ref[keys, :] = dv
            return carry

        lax.fori_loop(0, per_chain, step, 0)

        for d in DILATIONS[1:]:
            nb = T // d // ATTN_BLK

            def block(idx, kept, d=d, nb=nb):
                r, n = idx // nb, idx % nb
                rows = pl.ds(r + d * ATTN_BLK * n, ATTN_BLK, stride=d)
                before = pl.ds(r + d * ATTN_BLK * jnp.maximum(n - 1, 0), ATTN_BLK, stride=d)
                k_prev, v_prev, dk_prev, dv_prev = kept
                k_own, v_own = k_ref[rows, :].astype(BF16), v_ref[rows, :].astype(BF16)
                dq, ck, cv = grads(rows, jnp.concatenate([k_prev, k_own], axis=0),
                                   jnp.concatenate([v_prev, v_own], axis=0), jnp.where(n > 0, 1, 2))
                stores = (rows, before, dq, dk_prev + ck[:ATTN_BLK], dv_prev + cv[:ATTN_BLK], ck[ATTN_BLK:], cv[ATTN_BLK:])
                return stores, (k_own, v_own, ck[ATTN_BLK:], cv[ATTN_BLK:])

            def step(i, kept, block=block):
                done = [block(i + ch * per_chain, kept[ch]) for ch in range(ATTN_CHAINS)]
                for (rows, before, dq, dk_done, dv_done, dk_own, dv_own), _ in done:
                    dq_ref[rows, :] = dq
                    dkb_ref[before, :] = dk_done
                    dvb_ref[before, :] = dv_done
                    dkb_ref[rows, :] = dk_own
                    dvb_ref[rows, :] = dv_own
                return tuple(k for _, k in done)

            zero = jnp.zeros((ATTN_BLK, HEAD_PAIR), F32)
            lax.fori_loop(0, per_chain, step, ((zero.astype(BF16), zero.astype(BF16), zero, zero),) * ATTN_CHAINS)

            def add(i, carry):
                rows = pl.ds(pl.multiple_of(i * SUPER, SUPER), SUPER)
                dk_ref[rows, :] += dkb_ref[rows, :]
                dv_ref[rows, :] += dvb_ref[rows, :]
                return carry

            lax.fori_loop(0, T // SUPER, add, 0)

        def emit(i, carry):
            rows = pl.ds(pl.multiple_of(i * SUPER, SUPER), SUPER)
            for out, acc in zip(outs, (dq_ref, dk_ref, dv_ref)):
                out[rows, :] = acc[rows, :].astype(BF16)
            return carry

        lax.fori_loop(0, T // SUPER, emit, 0)

    col = lambda off: pl.BlockSpec((T, HEAD_PAIR), lambda j: (0, off + j))
    return pl.pallas_call(
        body, name="attn_bwd", grid=(4,),
        in_specs=[col(0), col(4), col(8), col(0), col(0), col(0)] + [_full(t.shape) for t in extra],
        out_specs=[col(0)] * 3,
        out_shape=[jax.ShapeDtypeStruct((T, ATTN_W), BF16)] * 3,
        scratch_shapes=[pltpu.VMEM((T, HEAD_PAIR), F32)] * 5 + [pltpu.VMEM((3, 2 * ATTN_BLK, 2 * ATTN_BLK), F32)],
        compiler_params=_cp("arbitrary"),
    )(qkv, qkv, qkv, o, lse, do, *extra)


def _chunk_ids():
    row = lax.broadcasted_iota(jnp.int32, (SUPER, HGRN_DIM), 0)
    r2 = lax.broadcasted_iota(jnp.int32, (SUPER, SUPER), 0)
    c2 = lax.broadcasted_iota(jnp.int32, (SUPER, SUPER), 1)
    amask = ((r2 // HGRN_CHUNK) == (c2 // HGRN_CHUNK)) & (c2 <= r2)
    return row % HGRN_CHUNK, row // HGRN_CHUNK, amask


def _cumsum_chunk(x, rmod):
    s = 1
    while s < HGRN_CHUNK:
        x = x + jnp.where(rmod >= s, pltpu.roll(x, s, 0), 0.0)
        s *= 2
    return x


def _suffix_sum_chunk(x, rmod):
    s = 1
    while s < HGRN_CHUNK:
        x = x + jnp.where(rmod < HGRN_CHUNK - s, pltpu.roll(x, SUPER - s, 0), 0.0)
        s *= 2
    return x


def _chunk_rows(vs, cid):
    out = vs[-1]
    for c in reversed(range(len(vs) - 1)):
        out = jnp.where(cid == c, vs[c], out)
    return out


def _expand(x, cid):
    return jnp.concatenate([jnp.where(cid == c, x, 0.0) for c in range(SUPER // HGRN_CHUNK)], axis=1)


def _hgrn_gates(q, f, lbv, rmod, cid, tmp):
    sq = _sigmoid(q)
    sg = _sigmoid(f)
    forget = lbv + (1.0 - lbv) * sg
    key = 1.0 - forget
    b = _cumsum_chunk(jnp.log(forget), rmod)
    tmp[...] = b
    bends = [tmp[c * HGRN_CHUNK + HGRN_CHUNK - 1:(c + 1) * HGRN_CHUNK, :] for c in range(SUPER // HGRN_CHUNK)]
    eb = jnp.exp(b)
    enb = jnp.exp(-b)
    ebe = jnp.exp(_chunk_rows(bends, cid) - b)
    return sq, sg, forget, key, eb, enb, ebe, q * sq * eb, key * enb, key * ebe, [jnp.exp(v) for v in bends]


def _hgrn_fwd(hg, lb):
    T = hg.shape[0]
    nsc = T // SUPER
    NC = SUPER // HGRN_CHUNK

    def body(q_ref, f_ref, i_ref, lb_ref, o_ref, st_ref, state, tmp):
        rmod, cid, amask = _chunk_ids()
        state[...] = jnp.zeros_like(state)
        lbv = lb_ref[...]

        def local(sc, u):
            rows = pl.ds(pl.multiple_of(sc * SUPER, SUPER), SUPER)
            iv = i_ref[rows, :].astype(BF16)
            qd, ki, ke, dec = _hgrn_gates(q_ref[rows, :], f_ref[rows, :], lbv, rmod, cid, tmp.at[u])[-4:]
            a = jnp.where(amask, _dot_nt(qd.astype(BF16), ki.astype(BF16)), 0.0)
            return rows, qd, dec, _dot(a.astype(BF16), iv), _dot_tn(iv, _expand(ke, cid).astype(BF16))

        def step(i, carry):
            parts = [local(i * HGRN_SIDE + u, u) for u in range(HGRN_SIDE)]
            st = state[...]
            entering = []
            for u, (_, _, dec, _, ut) in enumerate(parts):
                st_ref[0, i * HGRN_SIDE + u] = st
                sts = []
                for c in range(NC):
                    sts.append(st)
                    st = st * dec[c] + ut[:, c * HGRN_DIM:(c + 1) * HGRN_DIM]
                entering.append(jnp.concatenate(sts, axis=1).astype(BF16))
            state[...] = st
            for (rows, qd, _, o, _), sts in zip(parts, entering):
                o_ref[rows, :] = o + _dot_nt(_expand(qd, cid).astype(BF16), sts)
            return carry

        lax.fori_loop(0, nsc // HGRN_SIDE, step, 0)

    col = lambda off: pl.BlockSpec((T, HGRN_DIM), lambda h: (0, off + h))
    return pl.pallas_call(
        body, name="hgrn_fwd", grid=(HGRN_HEADS,),
        in_specs=[col(0), col(4), col(8), pl.BlockSpec((1, HGRN_DIM), lambda h: (0, h))],
        out_specs=[pl.BlockSpec((T, HGRN_DIM), lambda h: (0, h)),
                   pl.BlockSpec((1, nsc, HGRN_DIM, HGRN_DIM), lambda h: (h, 0, 0, 0))],
        out_shape=[jax.ShapeDtypeStruct((T, HGRN_W), F32),
                   jax.ShapeDtypeStruct((HGRN_HEADS, nsc, HGRN_DIM, HGRN_DIM), F32)],
        scratch_shapes=[pltpu.VMEM((HGRN_DIM, HGRN_DIM), F32), pltpu.VMEM((HGRN_SIDE, SUPER, HGRN_DIM), F32)],
        compiler_params=_cp("arbitrary"),
    )(hg, hg, hg, lb)


def _hgrn_bwd(hg, lb, states, do):
    T = hg.shape[0]
    nsc = T // SUPER
    NC = SUPER // HGRN_CHUNK

    def body(q_ref, f_ref, i_ref, lb_ref, st_ref, do_ref, dq_ref, df_ref, di_ref, dlb_ref, dstate, tmp):
        rmod, cid, amask = _chunk_ids()
        dstate[...] = jnp.zeros_like(dstate)
        dlb_ref[...] = jnp.zeros_like(dlb_ref)
        lbv = lb_ref[...]

        def local(sc, u):
            rows = pl.ds(pl.multiple_of(sc * SUPER, SUPER), SUPER)
            q = q_ref[rows, :]
            ivf = i_ref[rows, :]
            iv = ivf.astype(BF16)
            dof = do_ref[rows, :]
            dob = dof.astype(BF16)
            sq, sg, forget, key, eb, enb, ebe, qd, ki, ke, dec = _hgrn_gates(q, f_ref[rows, :], lbv, rmod, cid,
                                                                            tmp.at[u])
            qdb, kib = qd.astype(BF16), ki.astype(BF16)
            keexp = _expand(ke, cid).astype(BF16)
            a = jnp.where(amask, _dot_nt(qdb, kib), 0.0).astype(BF16)
            ut = _dot_tn(iv, keexp)
            st = st_ref[0, sc]
            sts = []
            for c in range(NC):
                sts.append(st)
                st = st * dec[c] + ut[:, c * HGRN_DIM:(c + 1) * HGRN_DIM]
            gt = _dot_tn(dob, _expand(qd, cid).astype(BF16))
            da = jnp.where(amask, _dot_nt(dob, iv), 0.0).astype(BF16)
            ststack = jnp.concatenate(sts, axis=0).astype(BF16)
            return dict(rows=rows, q=q, sq=sq, sg=sg, forget=forget, eb=eb, enb=enb, ebe=ebe, qd=qd, ki=ki, ke=ke,
                        dec=dec, sts=sts, gt=gt, keexp=keexp, ivexp=_expand(ivf, cid).astype(BF16),
                        div=_dot_tn(a, dob), dki=_dot_tn(da, qdb),
                        dqd=_dot(da, kib) + _dot(_expand(dof, cid).astype(BF16), ststack))

        def finish(p, nxt, ddec):
            ncat = jnp.concatenate(nxt, axis=1).astype(BF16)
            nstack = jnp.concatenate(nxt, axis=0).astype(BF16)
            dke = _dot(p["ivexp"], nstack)
            dkk = dke * p["ke"]
            dkey = p["dki"] * p["enb"] + dke * p["ebe"]
            db = p["dqd"] * p["qd"] - p["dki"] * p["ki"] - dkk
            dbends = [_colsum(jnp.where(cid == c, dkk, 0.0)) + ddec[c] * p["dec"][c] for c in range(NC)]
            dforget = (_suffix_sum_chunk(db, rmod) + _chunk_rows(dbends, cid)) / p["forget"] - dkey
            sg, sq, q = p["sg"], p["sq"], p["q"]
            df_ref[p["rows"], :] = (dforget * (1.0 - lbv) * sg * (1.0 - sg)).astype(BF16)
            dq_ref[p["rows"], :] = (p["dqd"] * p["eb"] * (sq * (1.0 + q * (1.0 - sq)))).astype(BF16)
            di_ref[p["rows"], :] = (p["div"] + _dot_nt(p["keexp"], ncat)).astype(BF16)
            return _colsum(dforget * (1.0 - sg))

        def step(i, carry):
            parts = [local(nsc - 1 - (i * HGRN_SIDE + u), u) for u in range(HGRN_SIDE)]
            dst = dstate[...]
            chained = []
            for p in parts:
                nxt = [None] * NC
                ddec = [None] * NC
                for c in reversed(range(NC)):
                    nxt[c] = dst
                    ddec[c] = _colsum(dst * p["sts"][c])
                    dst = dst * p["dec"][c] + p["gt"][:, c * HGRN_DIM:(c + 1) * HGRN_DIM]
                chained.append((nxt, ddec))
            dstate[...] = dst
            dlb = dlb_ref[...]
            for p, (nxt, ddec) in zip(parts, chained):
                dlb = dlb + finish(p, nxt, ddec)
            dlb_ref[...] = dlb
            return carry

        lax.fori_loop(0, nsc // HGRN_SIDE, step, 0)

    col = lambda off: pl.BlockSpec((T, HGRN_DIM), lambda h: (0, off + h))
    own = pl.BlockSpec((T, HGRN_DIM), lambda h: (0, h))
    vec = pl.BlockSpec((1, HGRN_DIM), lambda h: (0, h))
    return pl.pallas_call(
        body, name="hgrn_bwd", grid=(HGRN_HEADS,),
        in_specs=[col(0), col(4), col(8), vec,
                  pl.BlockSpec((1, nsc, HGRN_DIM, HGRN_DIM), lambda h: (h, 0, 0, 0)), own],
        out_specs=[own, own, own, vec],
        out_shape=[jax.ShapeDtypeStruct((T, HGRN_W), BF16)] * 3 + [jax.ShapeDtypeStruct((1, HGRN_W), F32)],
        scratch_shapes=[pltpu.VMEM((HGRN_DIM, HGRN_DIM), F32), pltpu.VMEM((HGRN_SIDE, SUPER, HGRN_DIM), F32)],
        compiler_params=_cp("arbitrary"),
    )(hg, hg, hg, lb, states, do)


def _rec_heads(rec, gate, g_h):
    rr = jnp.concatenate(
        [jnp.broadcast_to(_rms(rec[:, h * HGRN_DIM:(h + 1) * HGRN_DIM], HGRN_DIM), (rec.shape[0], HGRN_DIM))
         for h in range(HGRN_HEADS)], axis=1)
    rn = rec * rr
    sg = _sigmoid(gate)
    return rr, rn, sg


_INV_SQRT2 = 1.0 / math.sqrt(2.0)
_INV_SQRT2PI = 1.0 / math.sqrt(2.0 * math.pi)


def _gelu(x):
    return 0.5 * x * (1.0 + lax.erf(x * _INV_SQRT2))


def _gelu_and_grad(x):
    z = x * _INV_SQRT2
    cdf = 0.5 * (1.0 + lax.erf(z))
    return x * cdf, cdf + (x * _INV_SQRT2PI) * jnp.exp(-(z * z))


def _shift_down(g, prev, rowid):
    p1 = _row(prev, prev.shape[0] - 1)
    p2 = _row(prev, prev.shape[0] - 2)
    s1 = jnp.where(rowid == 0, p1, pltpu.roll(g, 1, 0))
    s2 = jnp.where(rowid == 0, p2, jnp.where(rowid == 1, p1, pltpu.roll(g, 2, 0)))
    return s1, s2


def _mlp_fwd(attn_o, rec_o, hg, x, g_a, g_h, w_out, g2, w_up4, conv_w, conv_b, w_down, gf, tgt, tm=256):
    T = x.shape[0]

    def body(a_ref, r_ref, gt_ref, x_ref, ga_ref, gh_ref, wo_ref, g2_ref, wu_hbm, cw_ref, cb_ref, wd_ref, gf_ref, t_ref,
             h1_ref, mixed_ref, u_ref, gate_ref, val_ref, conv_ref, act_ref, dh_ref, loss_ref, dgf_ref,
             carry, wu_ref, sem):
        i = pl.program_id(0)

        @pl.when(i == 0)
        def _():
            carry[...] = jnp.zeros_like(carry)
            loss_ref[...] = jnp.zeros_like(loss_ref)
            dgf_ref[...] = jnp.zeros_like(dgf_ref)
            _load_side_by_side(wu_hbm, wu_ref, sem)

        a = a_ref[...]
        an = a * _rms(a, ATTN_W) * ga_ref[...]
        og = gt_ref[...]
        _, rn, sg = _rec_heads(r_ref[...], og, gh_ref[...])
        mixed = jnp.concatenate([an, rn * gh_ref[...] * (og * sg)], axis=1).astype(BF16)
        mixed_ref[...] = mixed
        h = x_ref[...] + _dot(mixed, wo_ref[...])
        h1_ref[...] = h
        u = (h * _rms(h, D_MODEL) * g2_ref[...]).astype(BF16)
        u_ref[...] = u
        y2 = jnp.zeros((tm, D_MODEL), F32)
        for lo, hi in FF_CHUNKS:
            cols = slice(lo, hi)
            rowid = lax.broadcasted_iota(jnp.int32, (tm, hi - lo), 0)
            gb = _dot(u, wu_ref[:, lo:hi]).astype(BF16)
            vb = _dot(u, wu_ref[:, D_FF + lo:D_FF + hi]).astype(BF16)
            gate_ref[:, cols] = gb
            val_ref[:, cols] = vb
            g = gb.astype(F32)
            s1, s2 = _shift_down(g, carry[:, cols], rowid)
            carry[:, cols] = g[tm - 8:, :]
            conv = cb_ref[:, cols] + cw_ref[0:1, cols] * s2 + cw_ref[1:2, cols] * s1 + cw_ref[2:3, cols] * g
            act = (_gelu(conv) * vb.astype(F32)).astype(BF16)
            conv_ref[:, cols] = conv.astype(BF16)
            act_ref[:, cols] = act
            y2 = y2 + _dot(act, wd_ref[cols, :])
        h2 = h + y2
        rf = _rms(h2, D_MODEL)
        n = h2 * rf
        gfv = gf_ref[...]
        e = n * gfv - t_ref[...]
        loss_ref[...] += jnp.sum(e * e) * (0.5 / D_MODEL)
        dy = e * (1.0 / D_MODEL)
        dgf_ref[...] += _colsum(dy * n)
        dh_ref[...] = _rms_bwd(dy * gfv, n, rf, D_MODEL)

    row = lambda w: pl.BlockSpec((tm, w), lambda i: (i, 0))
    return pl.pallas_call(
        body, name="mlp_fwd", grid=(T // tm,),
        in_specs=[row(ATTN_W), row(HGRN_W), pl.BlockSpec((tm, HGRN_W), lambda i: (i, 3)), row(D_MODEL),
                  _full((1, ATTN_W)), _full((1, HGRN_W)), _once((D_MODEL, D_MODEL)),
                  _full((1, D_MODEL)), ANY, _full((3, D_FF)),
                  _full((1, D_FF)), _once((D_FF, D_MODEL)), _full((1, D_MODEL)), row(D_MODEL)],
        out_specs=[row(D_MODEL), row(D_MODEL), row(D_MODEL), row(D_FF), row(D_FF), row(D_FF), row(D_FF), row(D_MODEL),
                   _full((1, 128)), _full((1, D_MODEL))],
        out_shape=[jax.ShapeDtypeStruct((T, D_MODEL), F32), jax.ShapeDtypeStruct((T, D_MODEL), BF16),
                   jax.ShapeDtypeStruct((T, D_MODEL), BF16)] + [jax.ShapeDtypeStruct((T, D_FF), BF16)] * 4
        + [jax.ShapeDtypeStruct((T, D_MODEL), F32),
                   jax.ShapeDtypeStruct((1, 128), F32), jax.ShapeDtypeStruct((1, D_MODEL), F32)],
        scratch_shapes=[pltpu.VMEM((8, D_FF), F32), pltpu.VMEM((D_MODEL, 2 * D_FF), BF16),
                        pltpu.SemaphoreType.DMA((N_CHIPS,))],
        compiler_params=_cp("arbitrary"),
    )(attn_o, rec_o, hg, x, g_a, g_h, w_out, g2, w_up4, conv_w, conv_b, w_down, gf, tgt)


def _mlp_bwd(dh2, gate, val, conv, act, conv_w, w_down, tm=256):
    T = dh2.shape[0]
    nb = T // tm

    def body(dh_ref, gate_ref, val_ref, conv_ref, act_ref, cw_ref, wd_ref, dgv_ref, dcw_ref, dcb_ref, dwd_ref,
             carry, acc):
        i = pl.program_id(0)

        @pl.when(i == 0)
        def _():
            carry[...] = jnp.zeros_like(carry)
            dcw_ref[...] = jnp.zeros_like(dcw_ref)
            dcb_ref[...] = jnp.zeros_like(dcb_ref)
            acc[...] = jnp.zeros_like(acc)

        dhb = dh_ref[...].astype(BF16)
        for lo, hi in MLP_BWD_CHUNKS:
            cols = slice(lo, hi)
            rowid = lax.broadcasted_iota(jnp.int32, (tm, hi - lo), 0)
            acc[cols, :] += _dot_tn(act_ref[:, cols], dhb)
            g = gate_ref[:, cols].astype(F32)
            v = val_ref[:, cols].astype(F32)
            cv = conv_ref[:, cols].astype(F32)
            dact = _dot_nt(dhb, wd_ref[cols, :])
            gl, gp = _gelu_and_grad(cv)
            dconv = dact * v * gp
            nxt = carry[:, cols]
            n0, n1 = _row(nxt, 0), _row(nxt, 1)
            u1 = jnp.where(rowid == tm - 1, n0, pltpu.roll(dconv, tm - 1, 0))
            u2 = jnp.where(rowid == tm - 1, n1, jnp.where(rowid == tm - 2, n0, pltpu.roll(dconv, tm - 2, 0)))
            carry[:, cols] = dconv[0:8, :]
            dcb_ref[:, cols] += _colsum(dconv)
            dcw_ref[0:1, cols] += _colsum(u2 * g)
            dcw_ref[1:2, cols] += _colsum(u1 * g)
            dcw_ref[2:3, cols] += _colsum(dconv * g)
            dgate = cw_ref[2:3, cols] * dconv + cw_ref[1:2, cols] * u1 + cw_ref[0:1, cols] * u2
            dgv_ref[:, cols] = dgate.astype(BF16)
            dgv_ref[:, D_FF + lo:D_FF + hi] = (dact * gl).astype(BF16)

        @pl.when(i == nb - 1)
        def _():
            for lo, hi in MLP_BWD_CHUNKS:
                dwd_ref[lo:hi, :] = acc[lo:hi, :].astype(BF16)

    rev = lambda w: pl.BlockSpec((tm, w), lambda i: (nb - 1 - i, 0))
    return pl.pallas_call(
        body, name="mlp_bwd", grid=(nb,),
        in_specs=[rev(D_MODEL), rev(D_FF), rev(D_FF), rev(D_FF), rev(D_FF), _full((3, D_FF)), _once((D_FF, D_MODEL))],
        out_specs=[rev(2 * D_FF), _full((3, D_FF)), _full((1, D_FF)), _once((D_FF, D_MODEL))],
        out_shape=[jax.ShapeDtypeStruct((T, 2 * D_FF), BF16), jax.ShapeDtypeStruct((3, D_FF), F32),
                   jax.ShapeDtypeStruct((1, D_FF), F32), jax.ShapeDtypeStruct((D_FF, D_MODEL), BF16)],
        scratch_shapes=[pltpu.VMEM((8, D_FF), F32), pltpu.VMEM((D_FF, D_MODEL), F32)],
        compiler_params=_cp("arbitrary"),
    )(dh2, gate, val, conv, act, conv_w, w_down)


def _up_out_bwd(dgv, w_up4, h1, g2, dh2, w_out, mixed, attn_o, rec_o, hg, g_a, g_h, tm=256):
    T = h1.shape[0]
    nb = T // tm

    def body(dgv_ref, wu_hbm, h_ref, g2_ref, dh2_ref, wo_ref, mx_ref, a_ref, r_ref, gt_ref, ga_ref, gh_ref,
             dh1_ref, dg2_ref, da_ref, dr_ref, dgt_ref, dga_ref, dgh_ref, dwo_ref, wu_ref, sem, acc):
        i = pl.program_id(0)

        @pl.when(i == 0)
        def _():
            dg2_ref[...] = jnp.zeros_like(dg2_ref)
            dga_ref[...] = jnp.zeros_like(dga_ref)
            dgh_ref[...] = jnp.zeros_like(dgh_ref)
            acc[...] = jnp.zeros_like(acc)
            _load_side_by_side(wu_hbm, wu_ref, sem)

        du = _dot_nt(dgv_ref[...], wu_ref[...])
        h = h_ref[...]
        r = _rms(h, D_MODEL)
        n = h * r
        dg2_ref[...] += _colsum(du * n)
        dh1 = dh2_ref[...] + _rms_bwd(du * g2_ref[...], n, r, D_MODEL)
        dh1_ref[...] = dh1
        dh1b = dh1.astype(BF16)
        acc[...] += _dot_tn(mx_ref[...], dh1b)
        dmix = _dot_nt(dh1b, wo_ref[...])
        dan = dmix[:, :ATTN_W]
        a = a_ref[...]
        ra = _rms(a, ATTN_W)
        na = a * ra
        dga_ref[...] += _colsum(dan * na)
        da_ref[...] = _rms_bwd(dan * ga_ref[...], na, ra, ATTN_W)
        dmr = dmix[:, ATTN_W:]
        gate = gt_ref[...]
        ghv = gh_ref[...]
        rr, rn, sg = _rec_heads(r_ref[...], gate, ghv)
        dgt_ref[...] = (dmr * rn * ghv * (sg * (1.0 + gate * (1.0 - sg)))).astype(BF16)
        drecn = dmr * (gate * sg)
        dgh_ref[...] += _colsum(drecn * rn)
        drn = drecn * ghv
        prod = drn * rn
        mean = jnp.concatenate(
            [jnp.broadcast_to(jnp.sum(prod[:, h_ * HGRN_DIM:(h_ + 1) * HGRN_DIM], axis=-1, keepdims=True),
                              (tm, HGRN_DIM)) for h_ in range(HGRN_HEADS)], axis=1) * (1.0 / HGRN_DIM)
        dr_ref[...] = rr * (drn - rn * mean)

        @pl.when(i == nb - 1)
        def _():
            dwo_ref[...] = acc[...].astype(BF16)

    row = lambda w: pl.BlockSpec((tm, w), lambda i: (i, 0))
    return pl.pallas_call(
        body, name="up_out_bwd", grid=(nb,),
        in_specs=[row(2 * D_FF), ANY, row(D_MODEL), _full((1, D_MODEL)),
                  row(D_MODEL), _once((D_MODEL, D_MODEL)), row(D_MODEL), row(ATTN_W), row(HGRN_W),
                  pl.BlockSpec((tm, HGRN_W), lambda i: (i, 3)), _full((1, ATTN_W)), _full((1, HGRN_W))],
        out_specs=[row(D_MODEL), _full((1, D_MODEL)), row(ATTN_W), row(HGRN_W), row(HGRN_W),
                   _full((1, ATTN_W)), _full((1, HGRN_W)), _once((D_MODEL, D_MODEL))],
        out_shape=[jax.ShapeDtypeStruct((T, D_MODEL), F32), jax.ShapeDtypeStruct((1, D_MODEL), F32),
                   jax.ShapeDtypeStruct((T, ATTN_W), F32), jax.ShapeDtypeStruct((T, HGRN_W), F32),
                   jax.ShapeDtypeStruct((T, HGRN_W), BF16), jax.ShapeDtypeStruct((1, ATTN_W), F32),
                   jax.ShapeDtypeStruct((1, HGRN_W), F32), jax.ShapeDtypeStruct((D_MODEL, D_MODEL), BF16)],
        scratch_shapes=[pltpu.VMEM((D_MODEL, 2 * D_FF), BF16), pltpu.SemaphoreType.DMA((N_CHIPS,)),
                        pltpu.VMEM((D_MODEL, D_MODEL), F32)],
        compiler_params=_cp("arbitrary"),
    )(dgv, w_up4, h1, g2, dh2, w_out, mixed, attn_o, rec_o, hg, g_a, g_h)


def _in_bwd(dqkv, dhg, w_in4, u1, x, g1, dh1, tm=256):
    T = x.shape[0]
    nb = T // tm

    def body(*refs):
        parts = refs[:7]
        w_hbm, u_ref, x_ref, g_ref, dh1_ref, dw_ref, dx_ref, dg_ref, w_full, sem, acc = refs[7:]
        i = pl.program_id(0)

        @pl.when(i == 0)
        def _():
            dg_ref[...] = jnp.zeros_like(dg_ref)
            acc[...] = jnp.zeros_like(acc)
            _load_side_by_side(w_hbm, w_full, sem)

        dp = jnp.concatenate([p[...] for p in parts], axis=1)
        acc[...] += _dot_tn(u_ref[...], dp)
        du = _dot_nt(dp, w_full[...])
        xv = x_ref[...]
        r = _rms(xv, D_MODEL)
        n = xv * r
        dg_ref[...] += _colsum(du * n)
        dx_ref[...] = dh1_ref[...] + _rms_bwd(du * g_ref[...], n, r, D_MODEL)

        @pl.when(i == nb - 1)
        def _():
            for k in range(N_CHIPS):
                dw_ref[k] = acc[:, k * IN_SHARD:(k + 1) * IN_SHARD].astype(BF16)

    row = lambda w: pl.BlockSpec((tm, w), lambda i: (i, 0))
    return pl.pallas_call(
        body, name="in_bwd", grid=(nb,),
        in_specs=[row(ATTN_W)] * 7 + [ANY, row(D_MODEL), row(D_MODEL), _full((1, D_MODEL)), row(D_MODEL)],
        out_specs=[_once((N_CHIPS, D_MODEL, IN_SHARD)), row(D_MODEL), _full((1, D_MODEL))],
        out_shape=[jax.ShapeDtypeStruct((N_CHIPS, D_MODEL, IN_SHARD), BF16), jax.ShapeDtypeStruct((T, D_MODEL), F32),
                   jax.ShapeDtypeStruct((1, D_MODEL), F32)],
        scratch_shapes=[pltpu.VMEM((D_MODEL, IN_TOTAL), BF16), pltpu.SemaphoreType.DMA((N_CHIPS,)),
                        pltpu.VMEM((D_MODEL, IN_TOTAL), F32)],
        compiler_params=_cp("arbitrary"),
    )(*dqkv, *dhg, w_in4, u1, x, g1, dh1)


def _dw(a, b, kb, nb_, name, tk=1024, side=1):
    T, K = a.shape
    N = b.shape[1]
    nk, nn, nt = K // kb, N // (nb_ * side), T // tk

    def body(a_ref, b_ref, o_ref, acc):
        t = pl.program_id(2)

        @pl.when(t == 0)
        def _():
            acc[...] = jnp.zeros_like(acc)

        acc[...] += _dot_tn(a_ref[...], b_ref[...].astype(BF16))

        @pl.when(t == nt - 1)
        def _():
            for s in range(side):
                o_ref[s] = acc[:, s * nb_:(s + 1) * nb_].astype(BF16)

    return pl.pallas_call(
        body, name=name, grid=(nk, nn, nt),
        in_specs=[pl.BlockSpec((tk, kb), lambda i, j, t: (t, i)),
                  pl.BlockSpec((tk, nb_ * side), lambda i, j, t: (t, j))],
        out_specs=pl.BlockSpec((side, kb, nb_), lambda i, j, t: (i * nn + j, 0, 0)),
        out_shape=jax.ShapeDtypeStruct((nk * nn * side, kb, nb_), BF16),
        scratch_shapes=[pltpu.VMEM((kb, nb_ * side), F32)],
        compiler_params=_cp("arbitrary", "arbitrary", "arbitrary"),
    )(a, b)


def _step_channel(a, x, tgt, g_a, g_h, w_out, g2, w_up4, conv_w, conv_b, w_down, gf):
    h1, mixed, u2, gate, val, conv, act, dh2, loss, dgf = _mlp_fwd(
        a["attn_o"], a["rec_o"], a["hg"], x, g_a, g_h, w_out, g2, w_up4, conv_w, conv_b, w_down, gf, tgt)
    dgv, dcw, dcb, dw_down = _mlp_bwd(dh2, gate, val, conv, act, conv_w, w_down)
    dw_down = dw_down.reshape(N_CHIPS, D_FF // N_CHIPS, D_MODEL)
    dh1, dg2, da, dr, dgt, dga, dgh, dw_out = _up_out_bwd(dgv, w_up4, h1, g2, dh2, w_out, mixed, a["attn_o"],
                                                          a["rec_o"], a["hg"], g_a, g_h)
    dw_up = _dw(u2, dgv, D_MODEL, UP_SHARD, "dw_up", side=2)
    dw_out = dw_out.reshape(N_CHIPS, D_MODEL // N_CHIPS, D_MODEL)
    return dict(loss=loss, dgf=dgf, dcw=dcw, dcb=dcb, dg2=dg2, dga=dga, dgh=dgh, dh1=dh1, da=da, dr=dr, dgt=dgt,
                dw_down=dw_down, dw_up=dw_up, dw_out=dw_out)


def _step_mixers_bwd(a, b, x, g1, w_in4, lb, dqkv):
    dhq, dhf, dhi, dlb = _hgrn_bwd(a["hg"], lb, a["states"], b["dr"])
    dw_in, dx, dg1 = _in_bwd(dqkv, [dhq, dhf, dhi, b["dgt"]], w_in4, a["u1"], x, g1, b["dh1"])
    return dict(dx=dx, dg1=dg1, dlb=dlb, dw_in=dw_in)


BIG = ("w_in", "w_out", "w_up", "w_down")
ANY = pl.BlockSpec(memory_space=pl.ANY)


def _place():
    x, y, c = lax.axis_index("x"), lax.axis_index("y"), lax.axis_index("c")
    chips = [(1 - x, y), (x, 1 - y), (1 - x, 1 - y)]
    return x, y, c, chips


def _remote(src, dst, send_sems, recv_sems, k, to):
    return pltpu.make_async_remote_copy(src_ref=src, dst_ref=dst, send_sem=send_sems.at[k], recv_sem=recv_sems.at[k],
                                        device_id=to, device_id_type=MESH)


def _gather_weights(shards, conv_w):
    n = len(shards)
    halves = [s.shape[0] // 2 for s in shards]

    def body(*refs):
        ins, cw, outs, ocw = refs[:n], refs[n], refs[n + 1:2 * n + 1], refs[2 * n + 1]
        send_sems, recv_sems = refs[2 * n + 2:]
        x, y, c, chips = _place()
        me, sibling = 2 * x + y, (x, y, 1 - c)

        def part(w, chip, half):
            return outs[w].at[chip, pl.ds(half * halves[w], halves[w]), :]

        sent = []
        for j, chip in enumerate(chips):
            for w in range(n):
                sent.append(_remote(ins[w].at[pl.ds(c * halves[w], halves[w]), :], part(w, me, c),
                                    send_sems, recv_sems, w * 3 + j, (*chip, c)))
            sent.append(_remote(cw, ocw.at[me], send_sems, recv_sems, 6 * n + j, (*chip, c)))
        for cp in sent:
            cp.start()
        for j, chip in enumerate(chips):
            kj = 2 * chip[0] + chip[1]
            for w in range(n):
                _remote(part(w, kj, c), part(w, kj, c), send_sems, recv_sems, w * 3 + j, (*chip, c)).wait_recv()
                fwd = _remote(part(w, kj, c), part(w, kj, c), send_sems, recv_sems, 3 * n + w * 3 + j, sibling)
                fwd.start()
                sent.append(fwd)
        for j, chip in enumerate(chips):
            kj = 2 * chip[0] + chip[1]
            for w in range(n):
                _remote(part(w, kj, 1 - c), part(w, kj, 1 - c), send_sems, recv_sems, 3 * n + w * 3 + j,
                        sibling).wait_recv()
            _remote(cw, ocw.at[kj], send_sems, recv_sems, 6 * n + j, (*chip, c)).wait_recv()
        for cp in sent:
            cp.wait_send()

    n_sem = 6 * n + 3
    outs = pl.pallas_call(
        body, name="gather_weights",
        in_specs=[ANY] * (n + 1), out_specs=[ANY] * (n + 1),
        out_shape=[jax.ShapeDtypeStruct((N_CHIPS,) + s.shape, s.dtype) for s in shards]
        + [jax.ShapeDtypeStruct((N_CHIPS,) + conv_w.shape, conv_w.dtype)],
        scratch_shapes=[pltpu.SemaphoreType.DMA((n_sem,)), pltpu.SemaphoreType.DMA((n_sem,))],
    )(*shards, conv_w)
    chip = 2 * lax.axis_index("x") + lax.axis_index("y")
    return [lax.dynamic_update_slice(o, s[None], (chip,) + (0,) * s.ndim) for o, s in zip(outs, [*shards, conv_w])]


def _allreduce_small(buf):
    rows = buf.shape[0]

    def body(in_ref, out_ref, slots, send_sems, recv_sems):
        x, y, c, _ = _place()
        me = 4 * x + 2 * y + c
        slots[me] = in_ref[...]
        sent = []
        for p in range(1, 8):
            to = (x ^ (p >> 2), y ^ ((p >> 1) & 1), c ^ (p & 1))
            sent.append(_remote(in_ref, slots.at[me], send_sems, recv_sems, p, to))
        for cp in sent:
            cp.start()
        for p in range(1, 8):
            frm = 4 * (x ^ (p >> 2)) + 2 * (y ^ ((p >> 1) & 1)) + (c ^ (p & 1))
            _remote(in_ref, slots.at[frm], send_sems, recv_sems, p, (x, y, c)).wait_recv()
        for cp in sent:
            cp.wait_send()
        acc = slots[0]
        for d in range(1, 8):
            acc = acc + slots[d]
        out_ref[...] = acc

    vm = pl.BlockSpec(memory_space=pltpu.VMEM)
    return pl.pallas_call(
        body, name="allreduce_small", in_specs=[vm], out_specs=vm,
        out_shape=jax.ShapeDtypeStruct(buf.shape, F32),
        scratch_shapes=[pltpu.VMEM((8, rows, 128), F32), pltpu.SemaphoreType.DMA((8,)), pltpu.SemaphoreType.DMA((8,))],
    )(buf)


def _sibling_peer():
    x, y, c, _ = _place()
    return [(x, y, 1 - c)]


def _chip_peers():
    x, y, c, chips = _place()
    return [(*chip, c) for chip in chips]


def _handshake(peers):
    barrier = pltpu.get_barrier_semaphore()
    for peer in peers:
        pl.semaphore_signal(barrier, inc=1, device_id=peer, device_id_type=MESH)
    pl.semaphore_wait(barrier, len(peers))


def _pair_exchange(gs, name, barrier_id):
    n = len(gs)
    halves = [g.shape[1] // 2 for g in gs]

    def body(*refs):
        g, got = refs[:n], refs[n:2 * n]
        send_sems, recv_sems = refs[2 * n:]
        _handshake(_sibling_peer())
        x, y, c, _ = _place()
        cps = [_remote(g[w].at[:, pl.ds((1 - c) * halves[w], halves[w]), :], got[w], send_sems, recv_sems, w,
                       (x, y, 1 - c)) for w in range(n)]
        for cp in cps:
            cp.start()
        for cp in cps:
            cp.wait()

    return pl.pallas_call(
        body, name=name, in_specs=[ANY] * n, out_specs=[ANY] * n,
        out_shape=[jax.ShapeDtypeStruct((N_CHIPS, h, g.shape[2]), g.dtype) for g, h in zip(gs, halves)],
        scratch_shapes=[pltpu.SemaphoreType.DMA((n,)), pltpu.SemaphoreType.DMA((n,))],
        compiler_params=pltpu.CompilerParams(collective_id=barrier_id),
    )(*gs)


def _core_id():
    return lax.axis_index("c").reshape(1).astype(jnp.int32)


def _pair_sum(g, got, name):
    h, C = got.shape[1:]

    def body(c_ref, g_ref, b_ref, o_ref):
        o_ref[...] = (g_ref[...].astype(F32) + b_ref[...].astype(F32)).astype(BF16)

    blk = pl.BlockSpec((1, h, C), lambda k, c_ref: (k, 0, 0))
    return pl.pallas_call(
        body, name=name,
        grid_spec=pltpu.PrefetchScalarGridSpec(
            num_scalar_prefetch=1, grid=(N_CHIPS,),
            in_specs=[pl.BlockSpec((1, h, C), lambda k, c_ref: (k, c_ref[0], 0)), blk], out_specs=blk),
        out_shape=jax.ShapeDtypeStruct(got.shape, BF16), compiler_params=_cp("arbitrary"))(_core_id(), g, got)


def _sum_partials(g, got, landed, name):
    h, C = got.shape[1:]

    def body(ids, g_ref, b_ref, l_ref, o_ref):
        acc = g_ref[0].astype(F32) + b_ref[0].astype(F32)
        for j in range(3):
            acc = acc + l_ref[j].astype(F32)
        o_ref[...] = acc

    ids = jnp.stack([2 * lax.axis_index("x") + lax.axis_index("y"), lax.axis_index("c")]).astype(jnp.int32)
    return pl.pallas_call(
        body, name=name,
        grid_spec=pltpu.PrefetchScalarGridSpec(
            num_scalar_prefetch=1, grid=(1,),
            in_specs=[pl.BlockSpec((1, h, C), lambda i, ids: (ids[0], ids[1], 0)),
                      pl.BlockSpec((1, h, C), lambda i, ids: (ids[0], 0, 0)),
                      pl.BlockSpec((3, h, C), lambda i, ids: (0, 0, 0))],
            out_specs=pl.BlockSpec((h, C), lambda i, ids: (ids[1], 0))),
        out_shape=jax.ShapeDtypeStruct((2 * h, C), F32), compiler_params=_cp("arbitrary"))(ids, g, got, landed)


def _pair_share(reds, name, barrier_id):
    n = len(reds)

    def body(*refs):
        out = refs[n:2 * n]
        send_sems, recv_sems = refs[2 * n:]
        _handshake(_sibling_peer())
        x, y, c, _ = _place()
        def half(w, which):
            h = out[w].shape[0] // 2
            return out[w].at[pl.ds(which * h, h), :]

        cps = [_remote(half(w, c), half(w, c), send_sems, recv_sems, w, (x, y, 1 - c)) for w in range(n)]
        for cp in cps:
            cp.start()
        for w in range(n):
            _remote(half(w, 1 - c), half(w, 1 - c), send_sems, recv_sems, w, (x, y, 1 - c)).wait_recv()
        for cp in cps:
            cp.wait_send()

    return pl.pallas_call(
        body, name=name, in_specs=[ANY] * n, out_specs=[ANY] * n,
        out_shape=[jax.ShapeDtypeStruct(r.shape, F32) for r in reds],
        input_output_aliases={w: w for w in range(n)},
        scratch_shapes=[pltpu.SemaphoreType.DMA((n,)), pltpu.SemaphoreType.DMA((n,))],
        compiler_params=pltpu.CompilerParams(collective_id=barrier_id),
    )(*reds)


HBM = pl.BlockSpec(memory_space=pltpu.HBM)
SEM = pl.BlockSpec(memory_space=pltpu.SEMAPHORE)
DATAFLOW = pltpu.SideEffectType.DATAFLOW_SIDE_EFFECTING


def _copies_start(name, srcs, lands, plan, n_copies, after, peers, barrier_id):
    ns, nb, na = len(srcs), len(srcs) + len(lands), len(after)

    def body(*refs):
        src_refs, land_refs = refs[:ns], refs[ns:nb]
        send_sems, recv_sems = refs[nb + na:nb + na + 2]
        token = refs[-1]
        _handshake(peers())
        for k, (src, there, _, to) in enumerate(plan(src_refs, land_refs)):
            _remote(src, there, send_sems, recv_sems, k, to).start()
        token[...] = jnp.zeros_like(token)

    hbm = lambda a: pltpu.HBM(a.shape, a.dtype)
    outs = pl.pallas_call(
        body, name=name,
        out_shape=(pltpu.SemaphoreType.DMA((n_copies,)), pltpu.SemaphoreType.DMA((n_copies,)),
                   *[hbm(a) for a in srcs], *[hbm(a) for a in lands], jax.ShapeDtypeStruct((8, 128), F32)),
        in_specs=[HBM] * nb + [ANY] * na,
        out_specs=(SEM, SEM, *[HBM] * nb, pl.BlockSpec(memory_space=pltpu.VMEM)),
        input_output_aliases={i: 2 + i for i in range(nb)},
        compiler_params=pltpu.CompilerParams(has_side_effects=DATAFLOW, collective_id=barrier_id),
    )(*[pltpu.with_memory_space_constraint(a, pltpu.HBM) for a in (*srcs, *lands)], *after)
    return outs[0], outs[1], outs[2:2 + ns], outs[2 + ns:2 + nb], outs[-1]


def _copies_wait(name, send_sems, recv_sems, srcs, lands, plan, after):
    ns, nb, na = len(srcs), len(srcs) + len(lands), len(after)

    def body(*refs):
        src_refs, land_refs = refs[:ns], refs[ns:nb]
        send_sems, recv_sems = refs[nb:nb + 2]
        for k, (src, _, here, to) in enumerate(plan(src_refs, land_refs)):
            cp = _remote(src, here, send_sems, recv_sems, k, to)
            cp.wait_send()
            cp.wait_recv()

    hbm = lambda a: pltpu.HBM(a.shape, a.dtype)
    outs = pl.pallas_call(
        body, name=name,
        out_shape=(*[hbm(a) for a in srcs], *[hbm(a) for a in lands]),
        in_specs=[HBM] * nb + [SEM, SEM] + [ANY] * na,
        out_specs=tuple([HBM] * nb),
        input_output_aliases={i: i for i in range(nb)},
        compiler_params=pltpu.CompilerParams(has_side_effects=DATAFLOW),
    )(*srcs, *lands, send_sems, recv_sems, *after)
    return outs[:ns], outs[ns:]


def _gather_plan(halves):
    def plan(shards, lands):
        x, y, c, chips = _place()
        me = 2 * x + y
        copies = []
        for w, h in enumerate(halves):
            rows = pl.ds(c * h, h)
            for chip in chips:
                copies.append((shards[w].at[rows, :], lands[w].at[me, rows, :],
                               lands[w].at[2 * chip[0] + chip[1], rows, :], (*chip, c)))
        return copies
    return plan


def _reduce_plan(n):
    def plan(ps, lands):
        x, y, c, chips = _place()
        return [(ps[w].at[2 * chip[0] + chip[1]], lands[w].at[j], lands[w].at[j], (*chip, c))
                for w in range(n) for j, chip in enumerate(chips)]
    return plan


def _forward_plan(halves):
    def plan(_, lands):
        x, y, c, chips = _place()

        def part(w, chip, half):
            return lands[w].at[2 * chip[0] + chip[1], pl.ds(half * halves[w], halves[w]), :]

        return [(part(w, chip, c), part(w, chip, c), part(w, chip, 1 - c), (x, y, 1 - c))
                for w in range(len(halves)) for chip in chips]
    return plan


def _pair_plan(halves):
    def plan(gs, gots):
        x, y, c, _ = _place()
        return [(gs[w].at[:, pl.ds((1 - c) * h, h), :], gots[w], gots[w], (x, y, 1 - c)) for w, h in enumerate(halves)]
    return plan


def _place_own(gathered, shards):
    chip = 2 * lax.axis_index("x") + lax.axis_index("y")
    return [lax.dynamic_update_slice(o, s[None], (chip, 0, 0)) for o, s in zip(gathered, shards)]


def _adamw(w, g, m, v, name, tr=None):
    R, C = w.shape
    tr = tr or R // 4

    def body(w_ref, g_ref, m_ref, v_ref, g_out, d_ref, nm_ref, nv_ref):
        gv = g_ref[...]
        g_out[...] = gv
        d_ref[...], nm_ref[...], nv_ref[...] = _adamw_math(w_ref[...], gv, m_ref[...], v_ref[...])

    blk = pl.BlockSpec((tr, C), lambda i: (i, 0))
    return pl.pallas_call(body, name=name, grid=(R // tr,), in_specs=[blk] * 4, out_specs=[blk] * 4,
                          out_shape=[jax.ShapeDtypeStruct((R, C), F32)] * 4, compiler_params=_cp("arbitrary"))(w, g, m, v)


SMALL = (("norm1_g", 1, 1024), ("attn_norm_g", 1, 512), ("hgrn_norm_g", 1, 512), ("hgrn_lb_logits", 2, 512),
         ("norm2_g", 1, 1024), ("conv_b", 1, D_FF), ("final_norm_g", 1, 1024), ("conv_w", 3, D_FF))
LOSS_ROW = sum(r * c for _, r, c in SMALL) // 128
SMALL_ROWS = 136


def _rows_to_lanes(ref, row, width):
    return jnp.concatenate([ref[row + j:row + j + 1, :] for j in range(width // 128)], axis=1)


def _pack_small(grads, dlb, lb, loss):
    def body(*refs):
        parts, dlb_ref, lb_ref, loss_ref, out = refs[:len(SMALL) - 1], refs[-4], refs[-3], refs[-2], refs[-1]
        out[...] = jnp.zeros_like(out)
        lbv = lb_ref[...]
        dl = dlb_ref[...] * lbv * (1.0 - lbv)
        row = 0
        parts = list(parts)
        for name, rows, width in SMALL:
            for r in range(rows):
                if name == "hgrn_lb_logits":
                    src = dl if r == 0 else -dl
                    for j in range(width // 128):
                        out[row + j:row + j + 1, :] = src[:, 128 * j:128 * (j + 1)]
                else:
                    for j in range(width // 128):
                        out[row + j:row + j + 1, :] = parts[0][r:r + 1, 128 * j:128 * (j + 1)]
                row += width // 128
            if name != "hgrn_lb_logits":
                parts.pop(0)
        out[LOSS_ROW:LOSS_ROW + 1, :] = loss_ref[...]

    vm = pl.BlockSpec(memory_space=pltpu.VMEM)
    return pl.pallas_call(body, name="pack_small", in_specs=[vm] * (len(grads) + 3), out_specs=vm,
                          out_shape=jax.ShapeDtypeStruct((SMALL_ROWS, 128), F32))(*grads, dlb, lb, loss)


def _adamw_math(w, g, m, v):
    nm = ADAM_B1 * m + (1.0 - ADAM_B1) * g
    nv = ADAM_B2 * v + (1.0 - ADAM_B2) * (g * g)
    m_hat = nm / (1.0 - ADAM_B1 ** ADAM_STEP)
    v_hat = nv / (1.0 - ADAM_B2 ** ADAM_STEP)
    return -ADAM_LR * (m_hat / (jnp.sqrt(v_hat) + ADAM_EPS) + ADAM_WD * w), nm, nv


def _small_update(summed, g_conv_w, ws, ms, vs):
    n = len(SMALL)

    def body(*refs):
        s_ref, gcw_ref = refs[:2]
        w_refs, m_refs, v_refs = refs[2:2 + n], refs[2 + n:2 + 2 * n], refs[2 + 2 * n:2 + 3 * n]
        outs = refs[2 + 3 * n:]
        row = 0
        for k, (name, rows, width) in enumerate(SMALL):
            if name == "conv_w":
                g = gcw_ref[...]
            else:
                g = jnp.concatenate([_rows_to_lanes(s_ref, row + r * (width // 128), width) for r in range(rows)], axis=0)
            row += rows * (width // 128)
            d, nm, nv = _adamw_math(w_refs[k][...], g, m_refs[k][...], v_refs[k][...])
            for o, val in zip(outs[4 * k:4 * k + 4], (g, d, nm, nv)):
                o[...] = val

    vm = pl.BlockSpec(memory_space=pltpu.VMEM)
    outs = pl.pallas_call(
        body, name="small_update", in_specs=[vm] * (2 + 3 * n), out_specs=[vm] * (4 * n),
        out_shape=[jax.ShapeDtypeStruct(a.shape, F32) for a in ws for _ in range(4)],
    )(summed, g_conv_w, *ws, *ms, *vs)
    return [outs[4 * k:4 * k + 4] for k in range(n)]


def kernel(x, norm1_g, w_in, attn_norm_g, hgrn_norm_g, hgrn_lb_logits, w_out, norm2_g, w_up, conv_w, conv_b, w_down, final_norm_g, loss_target, m_norm1_g, m_w_in, m_attn_norm_g, m_hgrn_norm_g, m_hgrn_lb_logits, m_w_out, m_norm2_g, m_w_up, m_conv_w, m_conv_b, m_w_down, m_final_norm_g, v_norm1_g, v_w_in, v_attn_norm_g, v_hgrn_norm_g, v_hgrn_lb_logits, v_w_out, v_norm2_g, v_w_up, v_conv_w, v_conv_b, v_w_down, v_final_norm_g):
    w = dict(norm1_g=norm1_g, w_in=w_in, attn_norm_g=attn_norm_g, hgrn_norm_g=hgrn_norm_g,
             hgrn_lb_logits=hgrn_lb_logits, w_out=w_out, norm2_g=norm2_g, w_up=w_up, conv_w=conv_w, conv_b=conv_b,
             w_down=w_down, final_norm_g=final_norm_g)
    m = dict(norm1_g=m_norm1_g, w_in=m_w_in, attn_norm_g=m_attn_norm_g, hgrn_norm_g=m_hgrn_norm_g,
             hgrn_lb_logits=m_hgrn_lb_logits, w_out=m_w_out, norm2_g=m_norm2_g, w_up=m_w_up, conv_w=m_conv_w,
             conv_b=m_conv_b, w_down=m_w_down, final_norm_g=m_final_norm_g)
    v = dict(norm1_g=v_norm1_g, w_in=v_w_in, attn_norm_g=v_attn_norm_g, hgrn_norm_g=v_hgrn_norm_g,
             hgrn_lb_logits=v_hgrn_lb_logits, w_out=v_w_out, norm2_g=v_norm2_g, w_up=v_w_up, conv_w=v_conv_w,
             conv_b=v_conv_b, w_down=v_w_down, final_norm_g=v_final_norm_g)
    names = list(w)
    chip = 2 * lax.axis_index("x") + lax.axis_index("y")

    shards = {k: w[k][0].astype(BF16) for k in BIG}
    w_in4, conv_w4 = _gather_weights([shards["w_in"]], conv_w[0])
    conv_w_full = jnp.transpose(conv_w4, (1, 0, 2)).reshape(3, D_FF)
    lb = jax.nn.softmax(hgrn_lb_logits, axis=0)[0:1]
    late = [shards[k] for k in BIG[1:]]
    gather_plan = _gather_plan([s.shape[0] // 2 for s in late])
    started = _copies_start("gather_start", late, [lax.empty((N_CHIPS,) + s.shape, BF16) for s in late], gather_plan,
                            3 * len(late), after=(w_in4,), peers=_chip_peers, barrier_id=0)
    u1, qkv, hg = _in_proj(x[0], norm1_g + started[4][0:1, 0:1], w_in4)
    attn_o, lse = _attn_fwd(qkv)
    late, landed_w = _copies_wait("gather_wait", *started[:4], gather_plan, after=(attn_o,))
    forward_plan = _forward_plan([s.shape[0] // 2 for s in late])
    started = _copies_start("forward_start", [], landed_w, forward_plan, 3 * len(late), after=(),
                            peers=_sibling_peer, barrier_id=1)
    rec_o, states = _hgrn_fwd(hg, lb + started[4][0:1, 0:1])
    a = dict(u1=u1, qkv=qkv, hg=hg, attn_o=attn_o, lse=lse, rec_o=rec_o, states=states)
    w_out4, w_up4, w_down4 = _place_own(
        _copies_wait("forward_wait", *started[:4], forward_plan, after=(rec_o,))[1], late)

    b = _step_channel(a, x[0], loss_target[0], attn_norm_g, hgrn_norm_g, w_out4.reshape(D_MODEL, D_MODEL), norm2_g,
                      w_up4, conv_w_full, conv_b, w_down4.reshape(D_FF, D_MODEL), final_norm_g.reshape(1, D_MODEL))

    early = [b["dw_out"], b["dw_up"], b["dw_down"]]
    pair_plan = _pair_plan([gk.shape[1] // 2 for gk in early])
    started = _copies_start("pair_start", early,
                            [lax.empty((N_CHIPS, gk.shape[1] // 2, gk.shape[2]), BF16) for gk in early], pair_plan,
                            len(early), after=(), peers=_sibling_peer, barrier_id=2)
    dqkv = _attn_bwd(qkv, attn_o, lse, b["da"], started[4])
    early, gots = _copies_wait("pair_wait", *started[:4], pair_plan, after=(dqkv[0],))
    ps = [_pair_sum(gk, got, f"pair_sum_{k}") for gk, got, k in zip(early, gots, BIG[1:])]
    reduce_plan = _reduce_plan(len(ps))
    started = _copies_start("reduce_start", ps, [lax.empty((3,) + p.shape[1:], BF16) for p in ps], reduce_plan,
                            3 * len(ps), after=(), peers=_chip_peers, barrier_id=3)
    c = _step_mixers_bwd(a, b, x[0], norm1_g, w_in4, lb + started[4][0:1, 0:1], dqkv)
    gots_in = _pair_exchange([c["dw_in"]], "pair_exchange_w_in", barrier_id=4)
    ps_in = _pair_sum(c["dw_in"], gots_in[0], "pair_sum_w_in")
    plan_in = _reduce_plan(1)
    started_in = _copies_start("reduce_start_w_in", [ps_in], [lax.empty((3,) + ps_in.shape[1:], BF16)], plan_in, 3,
                               after=(), peers=_chip_peers, barrier_id=5)
    landed = _copies_wait("reduce_wait", *started[:4], reduce_plan, after=(started_in[4],))[1]
    reds = [_sum_partials(gk, got, l, f"sum_partials_{k}") for gk, got, l, k in zip(early, gots, landed, BIG[1:])]
    g = dict(zip(BIG[1:], _pair_share(reds, "pair_share", barrier_id=6)))
    delta, new_m, new_v = {}, {}, {}
    for k in BIG[1:]:
        g[k], delta[k], new_m[k], new_v[k] = _adamw(w[k][0], g[k], m[k][0], v[k][0], f"adamw_{k}")

    loss, dx = b["loss"], c["dx"]
    small = dict(g1=c["dg1"], g_a=b["dga"], g_h=b["dgh"], lb=c["dlb"], g2=b["dg2"], conv_w=b["dcw"], conv_b=b["dcb"],
                 gf=b["dgf"])
    summed = _allreduce_small(_pack_small(
        [small["g1"], small["g_a"], small["g_h"], small["g2"], small["conv_b"], small["gf"], small["conv_w"]],
        small["lb"], lb, loss))
    loss_total = summed[LOSS_ROW, 0]
    g_conv_w = lax.dynamic_slice(summed[LOSS_ROW - 3 * D_FF // 128:LOSS_ROW].reshape(3, D_FF),
                                 (0, chip * (D_FF // N_CHIPS)), (3, D_FF // N_CHIPS))
    two_d = lambda p, k: p[k].reshape(-1, p[k].shape[-1])
    updated = _small_update(summed, g_conv_w, *[[two_d(p, k) for k, _, _ in SMALL] for p in (w, m, v)])
    for (k, _, _), parts in zip(SMALL, updated):
        g[k], delta[k], new_m[k], new_v[k] = (a.reshape(w[k].shape) for a in parts)

    landed_in = _copies_wait("reduce_wait_w_in", *started_in[:4], plan_in, after=(updated[0][1], delta["w_up"]))[1]
    red_in = _sum_partials(c["dw_in"], gots_in[0], landed_in[0], "sum_partials_w_in")
    g["w_in"] = _pair_share([red_in], "pair_share_w_in", barrier_id=7)[0]
    g["w_in"], delta["w_in"], new_m["w_in"], new_v["w_in"] = _adamw(w_in[0], g["w_in"], m_w_in[0], v_w_in[0],
                                                                    "adamw_w_in")
    for k in BIG:
        g[k], delta[k], new_m[k], new_v[k] = g[k][None], delta[k][None], new_m[k][None], new_v[k][None]

    return (loss_total, dx[None], *[g[k] for k in names], *[delta[k] for k in names],
            *[new_m[k] for k in names], *[new_v[k] for k in names])
```

```python
import math

import jax
import jax.numpy as jnp
from jax import lax
from jax.experimental import pallas as pl
from jax.experimental.pallas import tpu as pltpu

F32 = jnp.float32
BF16 = jnp.bfloat16

D_MODEL = 1024
ATTN_W = 512
HGRN_W = 512
HEAD_PAIR = 128
ATTN_BLK = 128
DILATIONS = (1, 4, 16)
ATTN_CHAINS = 4
ATTN_CHAINS_FWD = 8
HGRN_HEADS = 4
HGRN_DIM = 128
HGRN_CHUNK = 64
SUPER = 256
HGRN_SIDE = 8
D_FF = 2816
FF_CHUNKS = ((0, 1536), (1536, D_FF))
MLP_BWD_CHUNKS = ((0, 768), (768, 1408), (1408, 2176), (2176, D_FF))
N_CHIPS = 4
IN_TOTAL = 3584
IN_SHARD = IN_TOTAL // N_CHIPS
UP_SHARD = 2 * D_FF // N_CHIPS
QKV_W = 3 * ATTN_W
HG_W = 4 * HGRN_W
EPS = 1e-6
NEG = -1e30
V7X_VMEM_BYTES = 64 * 1024 * 1024
VMEM_LIMIT = V7X_VMEM_BYTES - 8 * 1024 * 1024

ADAM_LR = 0.001
ADAM_B1 = 0.9
ADAM_B2 = 0.999
ADAM_EPS = 1e-08
ADAM_WD = 0.01
ADAM_STEP = 10

MESH = pl.DeviceIdType.MESH


def _cp(*sem):
    return pltpu.CompilerParams(dimension_semantics=sem or None, vmem_limit_bytes=VMEM_LIMIT)


def _dot(a, b):
    return jnp.dot(a, b, preferred_element_type=F32)


def _dot_nt(a, b):
    return lax.dot_general(a, b, (((1,), (1,)), ((), ())), preferred_element_type=F32)


def _dot_tn(a, b):
    return lax.dot_general(a, b, (((0,), (0,)), ((), ())), preferred_element_type=F32)


def _sigmoid(x):
    return 1.0 / (1.0 + jnp.exp(-x))


def _rms(x, width):
    return lax.rsqrt(jnp.sum(x * x, axis=-1, keepdims=True) * (1.0 / width) + EPS)


def _rms_bwd(dn, n, r, width):
    return r * (dn - n * (jnp.sum(dn * n, axis=-1, keepdims=True) * (1.0 / width)))


def _colsum(x):
    return jnp.sum(x, axis=0, keepdims=True)


def _row(v, k):
    rid = lax.broadcasted_iota(jnp.int32, v.shape, 0)
    return jnp.sum(jnp.where(rid == k, v, 0.0), axis=0, keepdims=True)


def _full(shape):
    return pl.BlockSpec(shape, lambda *_: (0,) * len(shape))


def _once(shape):
    return pl.BlockSpec(shape, lambda *_: (0,) * len(shape), pipeline_mode=pl.Buffered(1))


def _load_side_by_side(w_hbm, w_full, sem):
    width = w_hbm.shape[2]
    cps = [pltpu.make_async_copy(w_hbm.at[k], w_full.at[:, pl.ds(k * width, width)], sem.at[k]) for k in range(N_CHIPS)]
    for cp in cps:
        cp.start()
    for cp in cps:
        cp.wait()


def _in_proj(x, g1, w_in4, tm=512):
    T = x.shape[0]

    def body(x_ref, g_ref, w_hbm, u_ref, qkv_ref, hg_ref, w_full, sem):
        @pl.when(pl.program_id(0) == 0)
        def _():
            _load_side_by_side(w_hbm, w_full, sem)

        xv = x_ref[...]
        u = (xv * _rms(xv, D_MODEL) * g_ref[...]).astype(BF16)
        u_ref[...] = u
        p = _dot(u, w_full[...])
        qkv_ref[...] = p[:, :QKV_W]
        hg_ref[...] = p[:, QKV_W:]

    return pl.pallas_call(
        body, name="in_proj", grid=(T // tm,),
        in_specs=[pl.BlockSpec((tm, D_MODEL), lambda i: (i, 0)), _full((1, D_MODEL)), ANY],
        out_specs=[pl.BlockSpec((tm, D_MODEL), lambda i: (i, 0)), pl.BlockSpec((tm, QKV_W), lambda i: (i, 0)),
                   pl.BlockSpec((tm, HG_W), lambda i: (i, 0))],
        out_shape=[jax.ShapeDtypeStruct((T, D_MODEL), BF16), jax.ShapeDtypeStruct((T, QKV_W), F32),
                   jax.ShapeDtypeStruct((T, HG_W), F32)],
        scratch_shapes=[pltpu.VMEM((D_MODEL, IN_TOTAL), BF16), pltpu.SemaphoreType.DMA((N_CHIPS,))],
        compiler_params=_cp("arbitrary"),
    )(x, g1, w_in4)


def _attn_masks(bias_ref):
    lane = lax.broadcasted_iota(jnp.int32, (ATTN_BLK, HEAD_PAIR), 1)
    row = lax.broadcasted_iota(jnp.int32, (2 * ATTN_BLK, 2 * ATTN_BLK), 0)
    col = lax.broadcasted_iota(jnp.int32, (2 * ATTN_BLK, 2 * ATTN_BLK), 1)
    base = jnp.where(row >= ATTN_BLK, row - ATTN_BLK, row) - col
    for k in range(2):
        dist = base + k * ATTN_BLK
        bias_ref[k] = jnp.where((dist >= 0) & (dist <= ATTN_BLK), 0.0, NEG)
    bias_ref[2] = jnp.where(col >= ATTN_BLK, bias_ref[1], NEG)
    return lane < 64


def _two_heads(blk, first):
    zero = jnp.zeros_like(blk)
    return jnp.concatenate([jnp.where(first, blk, zero), jnp.where(first, zero, blk)], axis=0)


def _attn_rows(idx, nb, d):
    r, n = idx // nb, idx % nb
    kb = jnp.maximum(n - 1, 0)
    if d == 1:
        q0 = pl.multiple_of(n * ATTN_BLK, ATTN_BLK)
        k0 = pl.multiple_of(kb * ATTN_BLK, ATTN_BLK)
        return pl.ds(q0, ATTN_BLK), pl.ds(k0, 2 * ATTN_BLK), n - kb
    return (pl.ds(r + d * ATTN_BLK * n, ATTN_BLK, stride=d), pl.ds(r + d * ATTN_BLK * kb, 2 * ATTN_BLK, stride=d),
            n - kb)


def _attn_fwd(qkv):
    T = qkv.shape[0]

    n_blocks = T // ATTN_BLK

    def body(q_ref, k_ref, v_ref, o_ref, m_ref, l_ref, bias_ref):
        first = _attn_masks(bias_ref)
        for bi, d in enumerate(DILATIONS):
            nb = T // d // ATTN_BLK

            chains = ATTN_CHAINS_FWD
            per_chain = n_blocks // chains
            carried = d > 1 and per_chain % nb == 0

            def block(idx, kept=None, d=d, nb=nb, bi=bi, carried=carried):
                rows, keys, which = _attn_rows(idx, nb, d)
                q2 = _two_heads(q_ref[rows, :] * 0.125, first).astype(BF16)
                if carried:
                    k_own, v_own = k_ref[rows, :].astype(BF16), v_ref[rows, :].astype(BF16)
                    kw = jnp.concatenate([kept[0], k_own], axis=0)
                    vw = jnp.concatenate([kept[1], v_own], axis=0)
                    which = 2 - which
                else:
                    kw = k_ref[keys, :].astype(BF16)
                    vw = v_ref[keys, :].astype(BF16)
                old = (o_ref[rows, :], m_ref[rows, :], l_ref[rows, :]) if bi else None
                s = _dot_nt(q2, kw) + bias_ref[which]
                mb = jnp.max(s, axis=-1, keepdims=True)
                p = jnp.exp(s - mb)
                lb = jnp.sum(p, axis=-1, keepdims=True)
                o2 = _dot(p.astype(BF16), vw)
                o = jnp.where(first, o2[:ATTN_BLK], o2[ATTN_BLK:])
                m = jnp.where(first, mb[:ATTN_BLK], mb[ATTN_BLK:])
                l = jnp.where(first, lb[:ATTN_BLK], lb[ATTN_BLK:])
                if bi:
                    po, pm, pl_ = old
                    mn = jnp.maximum(pm, m)
                    wa = jnp.exp(pm - mn)
                    wb = jnp.exp(m - mn)
                    o, l, m = po * wa + o * wb, pl_ * wa + l * wb, mn
                return (rows, o, m, l), ((k_own, v_own) if carried else 0)

            def step(i, kept, block=block, carried=carried, chains=chains, per_chain=per_chain):
                done = [block(i + ch * per_chain, kept[ch] if carried else None) for ch in range(chains)]
                for (rows, o, m, l), _ in done:
                    o_ref[rows, :] = o
                    m_ref[rows, :] = m
                    l_ref[rows, :] = l
                return tuple(k for _, k in done) if carried else kept

            zero = jnp.zeros((ATTN_BLK, HEAD_PAIR), BF16)
            lax.fori_loop(0, per_chain, step, ((zero, zero),) * chains if carried else 0)

        def finish(i, carry):
            rows = pl.ds(pl.multiple_of(i * SUPER, SUPER), SUPER)
            l = l_ref[rows, :]
            o_ref[rows, :] = o_ref[rows, :] / l
            m_ref[rows, :] = m_ref[rows, :] + jnp.log(l)
            return carry

        lax.fori_loop(0, T // SUPER, finish, 0)

    col = lambda off: pl.BlockSpec((T, HEAD_PAIR), lambda j: (0, off + j))
    return pl.pallas_call(
        body, name="attn_fwd", grid=(4,),
        in_specs=[col(0), col(4), col(8)], out_specs=[col(0), col(0)],
        out_shape=[jax.ShapeDtypeStruct((T, ATTN_W), F32)] * 2,
        scratch_shapes=[pltpu.VMEM((T, HEAD_PAIR), F32), pltpu.VMEM((3, 2 * ATTN_BLK, 2 * ATTN_BLK), F32)],
        compiler_params=_cp("arbitrary"),
    )(qkv, qkv, qkv)


def _attn_bwd(qkv, o, lse, do, token=None):
    T = qkv.shape[0]
    per_chain = T // ATTN_BLK // ATTN_CHAINS
    extra = [] if token is None else [token]

    def body(q_ref, k_ref, v_ref, o_ref, lse_ref, do_ref, *rest):
        outs = rest[len(extra):len(extra) + 3]
        dq_ref, dk_ref, dv_ref, dkb_ref, dvb_ref, bias_ref = rest[len(extra) + 3:]
        first = _attn_masks(bias_ref)
        dq_ref[...] = jnp.zeros_like(dq_ref)
        dk_ref[...] = jnp.zeros_like(dk_ref)
        dv_ref[...] = jnp.zeros_like(dv_ref)

        def grads(rows, kw, vw, which):
            q2 = _two_heads(q_ref[rows, :] * 0.125, first).astype(BF16)
            lse_b = lse_ref[rows, :]
            dob = do_ref[rows, :]
            prod = dob * o_ref[rows, :]
            old = dq_ref[rows, :]
            lse2 = jnp.concatenate(
                [jnp.max(jnp.where(first, lse_b, NEG), axis=-1, keepdims=True),
                 jnp.max(jnp.where(first, NEG, lse_b), axis=-1, keepdims=True)], axis=0)
            p = jnp.exp(_dot_nt(q2, kw) + (bias_ref[which] - lse2))
            delta = jnp.concatenate(
                [jnp.sum(jnp.where(first, prod, 0.0), axis=-1, keepdims=True),
                 jnp.sum(jnp.where(first, 0.0, prod), axis=-1, keepdims=True)], axis=0)
            do2 = _two_heads(dob, first).astype(BF16)
            ds = (p * (_dot_nt(do2, vw) - delta)).astype(BF16)
            dq2 = _dot(ds, kw) * 0.125
            return (old + jnp.where(first, dq2[:ATTN_BLK], dq2[ATTN_BLK:]), _dot_tn(ds, q2),
                    _dot_tn(p.astype(BF16), do2))

        def block(idx):
            rows, keys, which = _attn_rows(idx, T // ATTN_BLK, 1)
            old = dk_ref[keys, :], dv_ref[keys, :]
            dq, ck, cv = grads(rows, k_ref[keys, :].astype(BF16), v_ref[keys, :].astype(BF16), which)
            return rows, keys, dq, old[0] + ck, old[1] + cv

        def step(i, carry):
            done = [block(i + ch * per_chain) for ch in range(ATTN_CHAINS)]
            for rows, keys, dq, dk, dv in done:
                dq_ref[rows, :] = dq
                dk_ref[keys, :] = dk
                dv_ref[keys, :] = dv
            return carry

        lax.fori_loop(0, per_chain, step, 0)

        for d in DILATIONS[1:]:
            nb = T // d // ATTN_BLK

            def block(idx, kept, d=d, nb=nb):
                r, n = idx // nb, idx % nb
                rows = pl.ds(r + d * ATTN_BLK * n, ATTN_BLK, stride=d)
                before = pl.ds(r + d * ATTN_BLK * jnp.maximum(n - 1, 0), ATTN_BLK, stride=d)
                k_prev, v_prev, dk_prev, dv_prev = kept
                k_own, v_own = k_ref[rows, :].astype(BF16), v_ref[rows, :].astype(BF16)
                dq, ck, cv = grads(rows, jnp.concatenate([k_prev, k_own], axis=0),
                                   jnp.concatenate([v_prev, v_own], axis=0), jnp.where(n > 0, 1, 2))
                stores = (rows, before, dq, dk_prev + ck[:ATTN_BLK], dv_prev + cv[:ATTN_BLK], ck[ATTN_BLK:], cv[ATTN_BLK:])
                return stores, (k_own, v_own, ck[ATTN_BLK:], cv[ATTN_BLK:])

            def step(i, kept, block=block):
                done = [block(i + ch * per_chain, kept[ch]) for ch in range(ATTN_CHAINS)]
                for (rows, before, dq, dk_done, dv_done, dk_own, dv_own), _ in done:
                    dq_ref[rows, :] = dq
                    dkb_ref[before, :] = dk_done
                    dvb_ref[before, :] = dv_done
                    dkb_ref[rows, :] = dk_own
                    dvb_ref[rows, :] = dv_own
                return tuple(k for _, k in done)

            zero = jnp.zeros((ATTN_BLK, HEAD_PAIR), F32)
            lax.fori_loop(0, per_chain, step, ((zero.astype(BF16), zero.astype(BF16), zero, zero),) * ATTN_CHAINS)

            def add(i, carry):
                rows = pl.ds(pl.multiple_of(i * SUPER, SUPER), SUPER)
                dk_ref[rows, :] += dkb_ref[rows, :]
                dv_ref[rows, :] += dvb_ref[rows, :]
                return carry

            lax.fori_loop(0, T // SUPER, add, 0)

        def emit(i, carry):
            rows = pl.ds(pl.multiple_of(i * SUPER, SUPER), SUPER)
            for out, acc in zip(outs, (dq_ref, dk_ref, dv_ref)):
                out[rows, :] = acc[rows, :].astype(BF16)
            return carry

        lax.fori_loop(0, T // SUPER, emit, 0)

    col = lambda off: pl.BlockSpec((T, HEAD_PAIR), lambda j: (0, off + j))
    return pl.pallas_call(
        body, name="attn_bwd", grid=(4,),
        in_specs=[col(0), col(4), col(8), col(0), col(0), col(0)] + [_full(t.shape) for t in extra],
        out_specs=[col(0)] * 3,
        out_shape=[jax.ShapeDtypeStruct((T, ATTN_W), BF16)] * 3,
        scratch_shapes=[pltpu.VMEM((T, HEAD_PAIR), F32)] * 5 + [pltpu.VMEM((3, 2 * ATTN_BLK, 2 * ATTN_BLK), F32)],
        compiler_params=_cp("arbitrary"),
    )(qkv, qkv, qkv, o, lse, do, *extra)


def _chunk_ids():
    row = lax.broadcasted_iota(jnp.int32, (SUPER, HGRN_DIM), 0)
    r2 = lax.broadcasted_iota(jnp.int32, (SUPER, SUPER), 0)
    c2 = lax.broadcasted_iota(jnp.int32, (SUPER, SUPER), 1)
    amask = ((r2 // HGRN_CHUNK) == (c2 // HGRN_CHUNK)) & (c2 <= r2)
    return row % HGRN_CHUNK, row // HGRN_CHUNK, amask


def _cumsum_chunk(x, rmod):
    s = 1
    while s < HGRN_CHUNK:
        x = x + jnp.where(rmod >= s, pltpu.roll(x, s, 0), 0.0)
        s *= 2
    return x


def _suffix_sum_chunk(x, rmod):
    s = 1
    while s < HGRN_CHUNK:
        x = x + jnp.where(rmod < HGRN_CHUNK - s, pltpu.roll(x, SUPER - s, 0), 0.0)
        s *= 2
    return x


def _chunk_rows(vs, cid):
    out = vs[-1]
    for c in reversed(range(len(vs) - 1)):
        out = jnp.where(cid == c, vs[c], out)
    return out


def _expand(x, cid):
    return jnp.concatenate([jnp.where(cid == c, x, 0.0) for c in range(SUPER // HGRN_CHUNK)], axis=1)


def _hgrn_gates(q, f, lbv, rmod, cid, tmp):
    sq = _sigmoid(q)
    sg = _sigmoid(f)
    forget = lbv + (1.0 - lbv) * sg
    key = 1.0 - forget
    b = _cumsum_chunk(jnp.log(forget), rmod)
    tmp[...] = b
    bends = [tmp[c * HGRN_CHUNK + HGRN_CHUNK - 1:(c + 1) * HGRN_CHUNK, :] for c in range(SUPER // HGRN_CHUNK)]
    eb = jnp.exp(b)
    enb = jnp.exp(-b)
    ebe = jnp.exp(_chunk_rows(bends, cid) - b)
    return sq, sg, forget, key, eb, enb, ebe, q * sq * eb, key * enb, key * ebe, [jnp.exp(v) for v in bends]


def _hgrn_fwd(hg, lb):
    T = hg.shape[0]
    nsc = T // SUPER
    NC = SUPER // HGRN_CHUNK

    def body(q_ref, f_ref, i_ref, lb_ref, o_ref, st_ref, state, tmp):
        rmod, cid, amask = _chunk_ids()
        state[...] = jnp.zeros_like(state)
        lbv = lb_ref[...]

        def local(sc, u):
            rows = pl.ds(pl.multiple_of(sc * SUPER, SUPER), SUPER)
            iv = i_ref[rows, :].astype(BF16)
            qd, ki, ke, dec = _hgrn_gates(q_ref[rows, :], f_ref[rows, :], lbv, rmod, cid, tmp.at[u])[-4:]
            a = jnp.where(amask, _dot_nt(qd.astype(BF16), ki.astype(BF16)), 0.0)
            return rows, qd, dec, _dot(a.astype(BF16), iv), _dot_tn(iv, _expand(ke, cid).astype(BF16))

        def step(i, carry):
            parts = [local(i * HGRN_SIDE + u, u) for u in range(HGRN_SIDE)]
            st = state[...]
            entering = []
            for u, (_, _, dec, _, ut) in enumerate(parts):
                st_ref[0, i * HGRN_SIDE + u] = st
                sts = []
                for c in range(NC):
                    sts.append(st)
                    st = st * dec[c] + ut[:, c * HGRN_DIM:(c + 1) * HGRN_DIM]
                entering.append(jnp.concatenate(sts, axis=1).astype(BF16))
            state[...] = st
            for (rows, qd, _, o, _), sts in zip(parts, entering):
                o_ref[rows, :] = o + _dot_nt(_expand(qd, cid).astype(BF16), sts)
            return carry

        lax.fori_loop(0, nsc // HGRN_SIDE, step, 0)

    col = lambda off: pl.BlockSpec((T, HGRN_DIM), lambda h: (0, off + h))
    return pl.pallas_call(
        body, name="hgrn_fwd", grid=(HGRN_HEADS,),
        in_specs=[col(0), col(4), col(8), pl.BlockSpec((1, HGRN_DIM), lambda h: (0, h))],
        out_specs=[pl.BlockSpec((T, HGRN_DIM), lambda h: (0, h)),
                   pl.BlockSpec((1, nsc, HGRN_DIM, HGRN_DIM), lambda h: (h, 0, 0, 0))],
        out_shape=[jax.ShapeDtypeStruct((T, HGRN_W), F32),
                   jax.ShapeDtypeStruct((HGRN_HEADS, nsc, HGRN_DIM, HGRN_DIM), F32)],
        scratch_shapes=[pltpu.VMEM((HGRN_DIM, HGRN_DIM), F32), pltpu.VMEM((HGRN_SIDE, SUPER, HGRN_DIM), F32)],
        compiler_params=_cp("arbitrary"),
    )(hg, hg, hg, lb)


def _hgrn_bwd(hg, lb, states, do):
    T = hg.shape[0]
    nsc = T // SUPER
    NC = SUPER // HGRN_CHUNK

    def body(q_ref, f_ref, i_ref, lb_ref, st_ref, do_ref, dq_ref, df_ref, di_ref, dlb_ref, dstate, tmp):
        rmod, cid, amask = _chunk_ids()
        dstate[...] = jnp.zeros_like(dstate)
        dlb_ref[...] = jnp.zeros_like(dlb_ref)
        lbv = lb_ref[...]

        def local(sc, u):
            rows = pl.ds(pl.multiple_of(sc * SUPER, SUPER), SUPER)
            q = q_ref[rows, :]
            ivf = i_ref[rows, :]
            iv = ivf.astype(BF16)
            dof = do_ref[rows, :]
            dob = dof.astype(BF16)
            sq, sg, forget, key, eb, enb, ebe, qd, ki, ke, dec = _hgrn_gates(q, f_ref[rows, :], lbv, rmod, cid,
                                                                            tmp.at[u])
            qdb, kib = qd.astype(BF16), ki.astype(BF16)
            keexp = _expand(ke, cid).astype(BF16)
            a = jnp.where(amask, _dot_nt(qdb, kib), 0.0).astype(BF16)
            ut = _dot_tn(iv, keexp)
            st = st_ref[0, sc]
            sts = []
            for c in range(NC):
                sts.append(st)
                st = st * dec[c] + ut[:, c * HGRN_DIM:(c + 1) * HGRN_DIM]
            gt = _dot_tn(dob, _expand(qd, cid).astype(BF16))
            da = jnp.where(amask, _dot_nt(dob, iv), 0.0).astype(BF16)
            ststack = jnp.concatenate(sts, axis=0).astype(BF16)
            return dict(rows=rows, q=q, sq=sq, sg=sg, forget=forget, eb=eb, enb=enb, ebe=ebe, qd=qd, ki=ki, ke=ke,
                        dec=dec, sts=sts, gt=gt, keexp=keexp, ivexp=_expand(ivf, cid).astype(BF16),
                        div=_dot_tn(a, dob), dki=_dot_tn(da, qdb),
                        dqd=_dot(da, kib) + _dot(_expand(dof, cid).astype(BF16), ststack))

        def finish(p, nxt, ddec):
            ncat = jnp.concatenate(nxt, axis=1).astype(BF16)
            nstack = jnp.concatenate(nxt, axis=0).astype(BF16)
            dke = _dot(p["ivexp"], nstack)
            dkk = dke * p["ke"]
            dkey = p["dki"] * p["enb"] + dke * p["ebe"]
            db = p["dqd"] * p["qd"] - p["dki"] * p["ki"] - dkk
            dbends = [_colsum(jnp.where(cid == c, dkk, 0.0)) + ddec[c] * p["dec"][c] for c in range(NC)]
            dforget = (_suffix_sum_chunk(db, rmod) + _chunk_rows(dbends, cid)) / p["forget"] - dkey
            sg, sq, q = p["sg"], p["sq"], p["q"]
            df_ref[p["rows"], :] = (dforget * (1.0 - lbv) * sg * (1.0 - sg)).astype(BF16)
            dq_ref[p["rows"], :] = (p["dqd"] * p["eb"] * (sq * (1.0 + q * (1.0 - sq)))).astype(BF16)
            di_ref[p["rows"], :] = (p["div"] + _dot_nt(p["keexp"], ncat)).astype(BF16)
            return _colsum(dforget * (1.0 - sg))

        def step(i, carry):
            parts = [local(nsc - 1 - (i * HGRN_SIDE + u), u) for u in range(HGRN_SIDE)]
            dst = dstate[...]
            chained = []
            for p in parts:
                nxt = [None] * NC
                ddec = [None] * NC
                for c in reversed(range(NC)):
                    nxt[c] = dst
                    ddec[c] = _colsum(dst * p["sts"][c])
                    dst = dst * p["dec"][c] + p["gt"][:, c * HGRN_DIM:(c + 1) * HGRN_DIM]
                chained.append((nxt, ddec))
            dstate[...] = dst
            dlb = dlb_ref[...]
            for p, (nxt, ddec) in zip(parts, chained):
                dlb = dlb + finish(p, nxt, ddec)
            dlb_ref[...] = dlb
            return carry

        lax.fori_loop(0, nsc // HGRN_SIDE, step, 0)

    col = lambda off: pl.BlockSpec((T, HGRN_DIM), lambda h: (0, off + h))
    own = pl.BlockSpec((T, HGRN_DIM), lambda h: (0, h))
    vec = pl.BlockSpec((1, HGRN_DIM), lambda h: (0, h))
    return pl.pallas_call(
        body, name="hgrn_bwd", grid=(HGRN_HEADS,),
        in_specs=[col(0), col(4), col(8), vec,
                  pl.BlockSpec((1, nsc, HGRN_DIM, HGRN_DIM), lambda h: (h, 0, 0, 0)), own],
        out_specs=[own, own, own, vec],
        out_shape=[jax.ShapeDtypeStruct((T, HGRN_W), BF16)] * 3 + [jax.ShapeDtypeStruct((1, HGRN_W), F32)],
        scratch_shapes=[pltpu.VMEM((HGRN_DIM, HGRN_DIM), F32), pltpu.VMEM((HGRN_SIDE, SUPER, HGRN_DIM), F32)],
        compiler_params=_cp("arbitrary"),
    )(hg, hg, hg, lb, states, do)


def _rec_heads(rec, gate, g_h):
    rr = jnp.concatenate(
        [jnp.broadcast_to(_rms(rec[:, h * HGRN_DIM:(h + 1) * HGRN_DIM], HGRN_DIM), (rec.shape[0], HGRN_DIM))
         for h in range(HGRN_HEADS)], axis=1)
    rn = rec * rr
    sg = _sigmoid(gate)
    return rr, rn, sg


_INV_SQRT2 = 1.0 / math.sqrt(2.0)
_INV_SQRT2PI = 1.0 / math.sqrt(2.0 * math.pi)


def _gelu(x):
    return 0.5 * x * (1.0 + lax.erf(x * _INV_SQRT2))


def _gelu_and_grad(x):
    z = x * _INV_SQRT2
    cdf = 0.5 * (1.0 + lax.erf(z))
    return x * cdf, cdf + (x * _INV_SQRT2PI) * jnp.exp(-(z * z))


def _shift_down(g, prev, rowid):
    p1 = _row(prev, prev.shape[0] - 1)
    p2 = _row(prev, prev.shape[0] - 2)
    s1 = jnp.where(rowid == 0, p1, pltpu.roll(g, 1, 0))
    s2 = jnp.where(rowid == 0, p2, jnp.where(rowid == 1, p1, pltpu.roll(g, 2, 0)))
    return s1, s2


def _mlp_fwd(attn_o, rec_o, hg, x, g_a, g_h, w_out, g2, w_up4, conv_w, conv_b, w_down, gf, tgt, tm=256):
    T = x.shape[0]

    def body(a_ref, r_ref, gt_ref, x_ref, ga_ref, gh_ref, wo_ref, g2_ref, wu_hbm, cw_ref, cb_ref, wd_ref, gf_ref, t_ref,
             h1_ref, mixed_ref, u_ref, gate_ref, val_ref, conv_ref, act_ref, dh_ref, loss_ref, dgf_ref,
             carry, wu_ref, sem):
        i = pl.program_id(0)

        @pl.when(i == 0)
        def _():
            carry[...] = jnp.zeros_like(carry)
            loss_ref[...] = jnp.zeros_like(loss_ref)
            dgf_ref[...] = jnp.zeros_like(dgf_ref)
            _load_side_by_side(wu_hbm, wu_ref, sem)

        a = a_ref[...]
        an = a * _rms(a, ATTN_W) * ga_ref[...]
        og = gt_ref[...]
        _, rn, sg = _rec_heads(r_ref[...], og, gh_ref[...])
        mixed = jnp.concatenate([an, rn * gh_ref[...] * (og * sg)], axis=1).astype(BF16)
        mixed_ref[...] = mixed
        h = x_ref[...] + _dot(mixed, wo_ref[...])
        h1_ref[...] = h
        u = (h * _rms(h, D_MODEL) * g2_ref[...]).astype(BF16)
        u_ref[...] = u
        y2 = jnp.zeros((tm, D_MODEL), F32)
        for lo, hi in FF_CHUNKS:
            cols = slice(lo, hi)
            rowid = lax.broadcasted_iota(jnp.int32, (tm, hi - lo), 0)
            gb = _dot(u, wu_ref[:, lo:hi]).astype(BF16)
            vb = _dot(u, wu_ref[:, D_FF + lo:D_FF + hi]).astype(BF16)
            gate_ref[:, cols] = gb
            val_ref[:, cols] = vb
            g = gb.astype(F32)
            s1, s2 = _shift_down(g, carry[:, cols], rowid)
            carry[:, cols] = g[tm - 8:, :]
            conv = cb_ref[:, cols] + cw_ref[0:1, cols] * s2 + cw_ref[1:2, cols] * s1 + cw_ref[2:3, cols] * g
            act = (_gelu(conv) * vb.astype(F32)).astype(BF16)
            conv_ref[:, cols] = conv.astype(BF16)
            act_ref[:, cols] = act
            y2 = y2 + _dot(act, wd_ref[cols, :])
        h2 = h + y2
        rf = _rms(h2, D_MODEL)
        n = h2 * rf
        gfv = gf_ref[...]
        e = n * gfv - t_ref[...]
        loss_ref[...] += jnp.sum(e * e) * (0.5 / D_MODEL)
        dy = e * (1.0 / D_MODEL)
        dgf_ref[...] += _colsum(dy * n)
        dh_ref[...] = _rms_bwd(dy * gfv, n, rf, D_MODEL)

    row = lambda w: pl.BlockSpec((tm, w), lambda i: (i, 0))
    return pl.pallas_call(
        body, name="mlp_fwd", grid=(T // tm,),
        in_specs=[row(ATTN_W), row(HGRN_W), pl.BlockSpec((tm, HGRN_W), lambda i: (i, 3)), row(D_MODEL),
                  _full((1, ATTN_W)), _full((1, HGRN_W)), _once((D_MODEL, D_MODEL)),
                  _full((1, D_MODEL)), ANY, _full((3, D_FF)),
                  _full((1, D_FF)), _once((D_FF, D_MODEL)), _full((1, D_MODEL)), row(D_MODEL)],
        out_specs=[row(D_MODEL), row(D_MODEL), row(D_MODEL), row(D_FF), row(D_FF), row(D_FF), row(D_FF), row(D_MODEL),
                   _full((1, 128)), _full((1, D_MODEL))],
        out_shape=[jax.ShapeDtypeStruct((T, D_MODEL), F32), jax.ShapeDtypeStruct((T, D_MODEL), BF16),
                   jax.ShapeDtypeStruct((T, D_MODEL), BF16)] + [jax.ShapeDtypeStruct((T, D_FF), BF16)] * 4
        + [jax.ShapeDtypeStruct((T, D_MODEL), F32),
                   jax.ShapeDtypeStruct((1, 128), F32), jax.ShapeDtypeStruct((1, D_MODEL), F32)],
        scratch_shapes=[pltpu.VMEM((8, D_FF), F32), pltpu.VMEM((D_MODEL, 2 * D_FF), BF16),
                        pltpu.SemaphoreType.DMA((N_CHIPS,))],
        compiler_params=_cp("arbitrary"),
    )(attn_o, rec_o, hg, x, g_a, g_h, w_out, g2, w_up4, conv_w, conv_b, w_down, gf, tgt)


def _mlp_bwd(dh2, gate, val, conv, act, conv_w, w_down, tm=256):
    T = dh2.shape[0]
    nb = T // tm

    def body(dh_ref, gate_ref, val_ref, conv_ref, act_ref, cw_ref, wd_ref, dgv_ref, dcw_ref, dcb_ref, dwd_ref,
             carry, acc):
        i = pl.program_id(0)

        @pl.when(i == 0)
        def _():
            carry[...] = jnp.zeros_like(carry)
            dcw_ref[...] = jnp.zeros_like(dcw_ref)
            dcb_ref[...] = jnp.zeros_like(dcb_ref)
            acc[...] = jnp.zeros_like(acc)

        dhb = dh_ref[...].astype(BF16)
        for lo, hi in MLP_BWD_CHUNKS:
            cols = slice(lo, hi)
            rowid = lax.broadcasted_iota(jnp.int32, (tm, hi - lo), 0)
            acc[cols, :] += _dot_tn(act_ref[:, cols], dhb)
            g = gate_ref[:, cols].astype(F32)
            v = val_ref[:, cols].astype(F32)
            cv = conv_ref[:, cols].astype(F32)
            dact = _dot_nt(dhb, wd_ref[cols, :])
            gl, gp = _gelu_and_grad(cv)
            dconv = dact * v * gp
            nxt = carry[:, cols]
            n0, n1 = _row(nxt, 0), _row(nxt, 1)
            u1 = jnp.where(rowid == tm - 1, n0, pltpu.roll(dconv, tm - 1, 0))
            u2 = jnp.where(rowid == tm - 1, n1, jnp.where(rowid == tm - 2, n0, pltpu.roll(dconv, tm - 2, 0)))
            carry[:, cols] = dconv[0:8, :]
            dcb_ref[:, cols] += _colsum(dconv)
            dcw_ref[0:1, cols] += _colsum(u2 * g)
            dcw_ref[1:2, cols] += _colsum(u1 * g)
            dcw_ref[2:3, cols] += _colsum(dconv * g)
            dgate = cw_ref[2:3, cols] * dconv + cw_ref[1:2, cols] * u1 + cw_ref[0:1, cols] * u2
            dgv_ref[:, cols] = dgate.astype(BF16)
            dgv_ref[:, D_FF + lo:D_FF + hi] = (dact * gl).astype(BF16)

        @pl.when(i == nb - 1)
        def _():
            for lo, hi in MLP_BWD_CHUNKS:
                dwd_ref[lo:hi, :] = acc[lo:hi, :].astype(BF16)

    rev = lambda w: pl.BlockSpec((tm, w), lambda i: (nb - 1 - i, 0))
    return pl.pallas_call(
        body, name="mlp_bwd", grid=(nb,),
        in_specs=[rev(D_MODEL), rev(D_FF), rev(D_FF), rev(D_FF), rev(D_FF), _full((3, D_FF)), _once((D_FF, D_MODEL))],
        out_specs=[rev(2 * D_FF), _full((3, D_FF)), _full((1, D_FF)), _once((D_FF, D_MODEL))],
        out_shape=[jax.ShapeDtypeStruct((T, 2 * D_FF), BF16), jax.ShapeDtypeStruct((3, D_FF), F32),
                   jax.ShapeDtypeStruct((1, D_FF), F32), jax.ShapeDtypeStruct((D_FF, D_MODEL), BF16)],
        scratch_shapes=[pltpu.VMEM((8, D_FF), F32), pltpu.VMEM((D_FF, D_MODEL), F32)],
        compiler_params=_cp("arbitrary"),
    )(dh2, gate, val, conv, act, conv_w, w_down)


def _up_out_bwd(dgv, w_up4, h1, g2, dh2, w_out, mixed, attn_o, rec_o, hg, g_a, g_h, tm=256):
    T = h1.shape[0]
    nb = T // tm

    def body(dgv_ref, wu_hbm, h_ref, g2_ref, dh2_ref, wo_ref, mx_ref, a_ref, r_ref, gt_ref, ga_ref, gh_ref,
             dh1_ref, dg2_ref, da_ref, dr_ref, dgt_ref, dga_ref, dgh_ref, dwo_ref, wu_ref, sem, acc):
        i = pl.program_id(0)

        @pl.when(i == 0)
        def _():
            dg2_ref[...] = jnp.zeros_like(dg2_ref)
            dga_ref[...] = jnp.zeros_like(dga_ref)
            dgh_ref[...] = jnp.zeros_like(dgh_ref)
            acc[...] = jnp.zeros_like(acc)
            _load_side_by_side(wu_hbm, wu_ref, sem)

        du = _dot_nt(dgv_ref[...], wu_ref[...])
        h = h_ref[...]
        r = _rms(h, D_MODEL)
        n = h * r
        dg2_ref[...] += _colsum(du * n)
        dh1 = dh2_ref[...] + _rms_bwd(du * g2_ref[...], n, r, D_MODEL)
        dh1_ref[...] = dh1
        dh1b = dh1.astype(BF16)
        acc[...] += _dot_tn(mx_ref[...], dh1b)
        dmix = _dot_nt(dh1b, wo_ref[...])
        dan = dmix[:, :ATTN_W]
        a = a_ref[...]
        ra = _rms(a, ATTN_W)
        na = a * ra
        dga_ref[...] += _colsum(dan * na)
        da_ref[...] = _rms_bwd(dan * ga_ref[...], na, ra, ATTN_W)
        dmr = dmix[:, ATTN_W:]
        gate = gt_ref[...]
        ghv = gh_ref[...]
        rr, rn, sg = _rec_heads(r_ref[...], gate, ghv)
        dgt_ref[...] = (dmr * rn * ghv * (sg * (1.0 + gate * (1.0 - sg)))).astype(BF16)
        drecn = dmr * (gate * sg)
        dgh_ref[...] += _colsum(drecn * rn)
        drn = drecn * ghv
        prod = drn * rn
        mean = jnp.concatenate(
            [jnp.broadcast_to(jnp.sum(prod[:, h_ * HGRN_DIM:(h_ + 1) * HGRN_DIM], axis=-1, keepdims=True),
                              (tm, HGRN_DIM)) for h_ in range(HGRN_HEADS)], axis=1) * (1.0 / HGRN_DIM)
        dr_ref[...] = rr * (drn - rn * mean)

        @pl.when(i == nb - 1)
        def _():
            dwo_ref[...] = acc[...].astype(BF16)

    row = lambda w: pl.BlockSpec((tm, w), lambda i: (i, 0))
    return pl.pallas_call(
        body, name="up_out_bwd", grid=(nb,),
        in_specs=[row(2 * D_FF), ANY, row(D_MODEL), _full((1, D_MODEL)),
                  row(D_MODEL), _once((D_MODEL, D_MODEL)), row(D_MODEL), row(ATTN_W), row(HGRN_W),
                  pl.BlockSpec((tm, HGRN_W), lambda i: (i, 3)), _full((1, ATTN_W)), _full((1, HGRN_W))],
        out_specs=[row(D_MODEL), _full((1, D_MODEL)), row(ATTN_W), row(HGRN_W), row(HGRN_W),
                   _full((1, ATTN_W)), _full((1, HGRN_W)), _once((D_MODEL, D_MODEL))],
        out_shape=[jax.ShapeDtypeStruct((T, D_MODEL), F32), jax.ShapeDtypeStruct((1, D_MODEL), F32),
                   jax.ShapeDtypeStruct((T, ATTN_W), F32), jax.ShapeDtypeStruct((T, HGRN_W), F32),
                   jax.ShapeDtypeStruct((T, HGRN_W), BF16), jax.ShapeDtypeStruct((1, ATTN_W), F32),
                   jax.ShapeDtypeStruct((1, HGRN_W), F32), jax.ShapeDtypeStruct((D_MODEL, D_MODEL), BF16)],
        scratch_shapes=[pltpu.VMEM((D_MODEL, 2 * D_FF), BF16), pltpu.SemaphoreType.DMA((N_CHIPS,)),
                        pltpu.VMEM((D_MODEL, D_MODEL), F32)],
        compiler_params=_cp("arbitrary"),
    )(dgv, w_up4, h1, g2, dh2, w_out, mixed, attn_o, rec_o, hg, g_a, g_h)


def _in_bwd(dqkv, dhg, w_in4, u1, x, g1, dh1, tm=256):
    T = x.shape[0]
    nb = T // tm

    def body(*refs):
        parts = refs[:7]
        w_hbm, u_ref, x_ref, g_ref, dh1_ref, dw_ref, dx_ref, dg_ref, w_full, sem, acc = refs[7:]
        i = pl.program_id(0)

        @pl.when(i == 0)
        def _():
            dg_ref[...] = jnp.zeros_like(dg_ref)
            acc[...] = jnp.zeros_like(acc)
            _load_side_by_side(w_hbm, w_full, sem)

        dp = jnp.concatenate([p[...] for p in parts], axis=1)
        acc[...] += _dot_tn(u_ref[...], dp)
        du = _dot_nt(dp, w_full[...])
        xv = x_ref[...]
        r = _rms(xv, D_MODEL)
        n = xv * r
        dg_ref[...] += _colsum(du * n)
        dx_ref[...] = dh1_ref[...] + _rms_bwd(du * g_ref[...], n, r, D_MODEL)

        @pl.when(i == nb - 1)
        def _():
            for k in range(N_CHIPS):
                dw_ref[k] = acc[:, k * IN_SHARD:(k + 1) * IN_SHARD].astype(BF16)

    row = lambda w: pl.BlockSpec((tm, w), lambda i: (i, 0))
    return pl.pallas_call(
        body, name="in_bwd", grid=(nb,),
        in_specs=[row(ATTN_W)] * 7 + [ANY, row(D_MODEL), row(D_MODEL), _full((1, D_MODEL)), row(D_MODEL)],
        out_specs=[_once((N_CHIPS, D_MODEL, IN_SHARD)), row(D_MODEL), _full((1, D_MODEL))],
        out_shape=[jax.ShapeDtypeStruct((N_CHIPS, D_MODEL, IN_SHARD), BF16), jax.ShapeDtypeStruct((T, D_MODEL), F32),
                   jax.ShapeDtypeStruct((1, D_MODEL), F32)],
        scratch_shapes=[pltpu.VMEM((D_MODEL, IN_TOTAL), BF16), pltpu.SemaphoreType.DMA((N_CHIPS,)),
                        pltpu.VMEM((D_MODEL, IN_TOTAL), F32)],
        compiler_params=_cp("arbitrary"),
    )(*dqkv, *dhg, w_in4, u1, x, g1, dh1)


def _dw(a, b, kb, nb_, name, tk=1024, side=1):
    T, K = a.shape
    N = b.shape[1]
    nk, nn, nt = K // kb, N // (nb_ * side), T // tk

    def body(a_ref, b_ref, o_ref, acc):
        t = pl.program_id(2)

        @pl.when(t == 0)
        def _():
            acc[...] = jnp.zeros_like(acc)

        acc[...] += _dot_tn(a_ref[...], b_ref[...].astype(BF16))

        @pl.when(t == nt - 1)
        def _():
            for s in range(side):
                o_ref[s] = acc[:, s * nb_:(s + 1) * nb_].astype(BF16)

    return pl.pallas_call(
        body, name=name, grid=(nk, nn, nt),
        in_specs=[pl.BlockSpec((tk, kb), lambda i, j, t: (t, i)),
                  pl.BlockSpec((tk, nb_ * side), lambda i, j, t: (t, j))],
        out_specs=pl.BlockSpec((side, kb, nb_), lambda i, j, t: (i * nn + j, 0, 0)),
        out_shape=jax.ShapeDtypeStruct((nk * nn * side, kb, nb_), BF16),
        scratch_shapes=[pltpu.VMEM((kb, nb_ * side), F32)],
        compiler_params=_cp("arbitrary", "arbitrary", "arbitrary"),
    )(a, b)


def _step_channel(a, x, tgt, g_a, g_h, w_out, g2, w_up4, conv_w, conv_b, w_down, gf):
    h1, mixed, u2, gate, val, conv, act, dh2, loss, dgf = _mlp_fwd(
        a["attn_o"], a["rec_o"], a["hg"], x, g_a, g_h, w_out, g2, w_up4, conv_w, conv_b, w_down, gf, tgt)
    dgv, dcw, dcb, dw_down = _mlp_bwd(dh2, gate, val, conv, act, conv_w, w_down)
    dw_down = dw_down.reshape(N_CHIPS, D_FF // N_CHIPS, D_MODEL)
    dh1, dg2, da, dr, dgt, dga, dgh, dw_out = _up_out_bwd(dgv, w_up4, h1, g2, dh2, w_out, mixed, a["attn_o"],
                                                          a["rec_o"], a["hg"], g_a, g_h)
    dw_up = _dw(u2, dgv, D_MODEL, UP_SHARD, "dw_up", side=2)
    dw_out = dw_out.reshape(N_CHIPS, D_MODEL // N_CHIPS, D_MODEL)
    return dict(loss=loss, dgf=dgf, dcw=dcw, dcb=dcb, dg2=dg2, dga=dga, dgh=dgh, dh1=dh1, da=da, dr=dr, dgt=dgt,
                dw_down=dw_down, dw_up=dw_up, dw_out=dw_out)


def _step_mixers_bwd(a, b, x, g1, w_in4, lb, dqkv):
    dhq, dhf, dhi, dlb = _hgrn_bwd(a["hg"], lb, a["states"], b["dr"])
    dw_in, dx, dg1 = _in_bwd(dqkv, [dhq, dhf, dhi, b["dgt"]], w_in4, a["u1"], x, g1, b["dh1"])
    return dict(dx=dx, dg1=dg1, dlb=dlb, dw_in=dw_in)


BIG = ("w_in", "w_out", "w_up", "w_down")
ANY = pl.BlockSpec(memory_space=pl.ANY)


def _place():
    x, y, c = lax.axis_index("x"), lax.axis_index("y"), lax.axis_index("c")
    chips = [(1 - x, y), (x, 1 - y), (1 - x, 1 - y)]
    return x, y, c, chips


def _remote(src, dst, send_sems, recv_sems, k, to):
    return pltpu.make_async_remote_copy(src_ref=src, dst_ref=dst, send_sem=send_sems.at[k], recv_sem=recv_sems.at[k],
                                        device_id=to, device_id_type=MESH)


def _gather_weights(shards, conv_w):
    n = len(shards)
    halves = [s.shape[0] // 2 for s in shards]

    def body(*refs):
        ins, cw, outs, ocw = refs[:n], refs[n], refs[n + 1:2 * n + 1], refs[2 * n + 1]
        send_sems, recv_sems = refs[2 * n + 2:]
        x, y, c, chips = _place()
        me, sibling = 2 * x + y, (x, y, 1 - c)

        def part(w, chip, half):
            return outs[w].at[chip, pl.ds(half * halves[w], halves[w]), :]

        sent = []
        for j, chip in enumerate(chips):
            for w in range(n):
                sent.append(_remote(ins[w].at[pl.ds(c * halves[w], halves[w]), :], part(w, me, c),
                                    send_sems, recv_sems, w * 3 + j, (*chip, c)))
            sent.append(_remote(cw, ocw.at[me], send_sems, recv_sems, 6 * n + j, (*chip, c)))
        for cp in sent:
            cp.start()
        for j, chip in enumerate(chips):
            kj = 2 * chip[0] + chip[1]
            for w in range(n):
                _remote(part(w, kj, c), part(w, kj, c), send_sems, recv_sems, w * 3 + j, (*chip, c)).wait_recv()
                fwd = _remote(part(w, kj, c), part(w, kj, c), send_sems, recv_sems, 3 * n + w * 3 + j, sibling)
                fwd.start()
                sent.append(fwd)
        for j, chip in enumerate(chips):
            kj = 2 * chip[0] + chip[1]
            for w in range(n):
                _remote(part(w, kj, 1 - c), part(w, kj, 1 - c), send_sems, recv_sems, 3 * n + w * 3 + j,
                        sibling).wait_recv()
            _remote(cw, ocw.at[kj], send_sems, recv_sems, 6 * n + j, (*chip, c)).wait_recv()
        for cp in sent:
            cp.wait_send()

    n_sem = 6 * n + 3
    outs = pl.pallas_call(
        body, name="gather_weights",
        in_specs=[ANY] * (n + 1), out_specs=[ANY] * (n + 1),
        out_shape=[jax.ShapeDtypeStruct((N_CHIPS,) + s.shape, s.dtype) for s in shards]
        + [jax.ShapeDtypeStruct((N_CHIPS,) + conv_w.shape, conv_w.dtype)],
        scratch_shapes=[pltpu.SemaphoreType.DMA((n_sem,)), pltpu.SemaphoreType.DMA((n_sem,))],
    )(*shards, conv_w)
    chip = 2 * lax.axis_index("x") + lax.axis_index("y")
    return [lax.dynamic_update_slice(o, s[None], (chip,) + (0,) * s.ndim) for o, s in zip(outs, [*shards, conv_w])]


def _allreduce_small(buf):
    rows = buf.shape[0]

    def body(in_ref, out_ref, slots, send_sems, recv_sems):
        x, y, c, _ = _place()
        me = 4 * x + 2 * y + c
        slots[me] = in_ref[...]
        sent = []
        for p in range(1, 8):
            to = (x ^ (p >> 2), y ^ ((p >> 1) & 1), c ^ (p & 1))
            sent.append(_remote(in_ref, slots.at[me], send_sems, recv_sems, p, to))
        for cp in sent:
            cp.start()
        for p in range(1, 8):
            frm = 4 * (x ^ (p >> 2)) + 2 * (y ^ ((p >> 1) & 1)) + (c ^ (p & 1))
            _remote(in_ref, slots.at[frm], send_sems, recv_sems, p, (x, y, c)).wait_recv()
        for cp in sent:
            cp.wait_send()
        acc = slots[0]
        for d in range(1, 8):
            acc = acc + slots[d]
        out_ref[...] = acc

    vm = pl.BlockSpec(memory_space=pltpu.VMEM)
    return pl.pallas_call(
        body, name="allreduce_small", in_specs=[vm], out_specs=vm,
        out_shape=jax.ShapeDtypeStruct(buf.shape, F32),
        scratch_shapes=[pltpu.VMEM((8, rows, 128), F32), pltpu.SemaphoreType.DMA((8,)), pltpu.SemaphoreType.DMA((8,))],
    )(buf)


def _sibling_peer():
    x, y, c, _ = _place()
    return [(x, y, 1 - c)]


def _chip_peers():
    x, y, c, chips = _place()
    return [(*chip, c) for chip in chips]


def _handshake(peers):
    barrier = pltpu.get_barrier_semaphore()
    for peer in peers:
        pl.semaphore_signal(barrier, inc=1, device_id=peer, device_id_type=MESH)
    pl.semaphore_wait(barrier, len(peers))


def _pair_exchange(gs, name, barrier_id):
    n = len(gs)
    halves = [g.shape[1] // 2 for g in gs]

    def body(*refs):
        g, got = refs[:n], refs[n:2 * n]
        send_sems, recv_sems = refs[2 * n:]
        _handshake(_sibling_peer())
        x, y, c, _ = _place()
        cps = [_remote(g[w].at[:, pl.ds((1 - c) * halves[w], halves[w]), :], got[w], send_sems, recv_sems, w,
                       (x, y, 1 - c)) for w in range(n)]
        for cp in cps:
            cp.start()
        for cp in cps:
            cp.wait()

    return pl.pallas_call(
        body, name=name, in_specs=[ANY] * n, out_specs=[ANY] * n,
        out_shape=[jax.ShapeDtypeStruct((N_CHIPS, h, g.shape[2]), g.dtype) for g, h in zip(gs, halves)],
        scratch_shapes=[pltpu.SemaphoreType.DMA((n,)), pltpu.SemaphoreType.DMA((n,))],
        compiler_params=pltpu.CompilerParams(collective_id=barrier_id),
    )(*gs)


def _core_id():
    return lax.axis_index("c").reshape(1).astype(jnp.int32)


def _pair_sum(gs, gots, name):
    n = len(gs)

    def body(c_ref, *refs):
        for g_ref, b_ref, o_ref in zip(refs[:n], refs[n:2 * n], refs[2 * n:]):
            o_ref[...] = (g_ref[...].astype(F32) + b_ref[...].astype(F32)).astype(BF16)

    mine = lambda got: pl.BlockSpec((1,) + got.shape[1:], lambda k, c_ref: (k, c_ref[0], 0))
    blk = lambda got: pl.BlockSpec((1,) + got.shape[1:], lambda k, c_ref: (k, 0, 0))
    return pl.pallas_call(
        body, name=name,
        grid_spec=pltpu.PrefetchScalarGridSpec(
            num_scalar_prefetch=1, grid=(N_CHIPS,),
            in_specs=[mine(got) for got in gots] + [blk(got) for got in gots], out_specs=[blk(got) for got in gots]),
        out_shape=[jax.ShapeDtypeStruct(got.shape, BF16) for got in gots],
        compiler_params=_cp("arbitrary"))(_core_id(), *gs, *gots)


def _sum_partials(gs, gots, landeds, name):
    n = len(gs)

    def body(ids, *refs):
        for g_ref, b_ref, l_ref, o_ref in zip(refs[:n], refs[n:2 * n], refs[2 * n:3 * n], refs[3 * n:]):
            acc = g_ref[0].astype(F32) + b_ref[0].astype(F32)
            for j in range(3):
                acc = acc + l_ref[j].astype(F32)
            o_ref[...] = acc

    ids = jnp.stack([2 * lax.axis_index("x") + lax.axis_index("y"), lax.axis_index("c")]).astype(jnp.int32)
    shp = [got.shape[1:] for got in gots]
    return pl.pallas_call(
        body, name=name,
        grid_spec=pltpu.PrefetchScalarGridSpec(
            num_scalar_prefetch=1, grid=(1,),
            in_specs=[pl.BlockSpec((1,) + s, lambda i, ids: (ids[0], ids[1], 0)) for s in shp]
            + [pl.BlockSpec((1,) + s, lambda i, ids: (ids[0], 0, 0)) for s in shp]
            + [pl.BlockSpec((3,) + s, lambda i, ids: (0, 0, 0)) for s in shp],
            out_specs=[pl.BlockSpec(s, lambda i, ids: (ids[1], 0)) for s in shp]),
        out_shape=[jax.ShapeDtypeStruct((2 * s[0], s[1]), F32) for s in shp],
        compiler_params=_cp("arbitrary"))(ids, *gs, *gots, *landeds)


def _pair_share(reds, name, barrier_id):
    n = len(reds)

    def body(*refs):
        out = refs[n:2 * n]
        send_sems, recv_sems = refs[2 * n:]
        _handshake(_sibling_peer())
        x, y, c, _ = _place()
        def half(w, which):
            h = out[w].shape[0] // 2
            return out[w].at[pl.ds(which * h, h), :]

        cps = [_remote(half(w, c), half(w, c), send_sems, recv_sems, w, (x, y, 1 - c)) for w in range(n)]
        for cp in cps:
            cp.start()
        for w in range(n):
            _remote(half(w, 1 - c), half(w, 1 - c), send_sems, recv_sems, w, (x, y, 1 - c)).wait_recv()
        for cp in cps:
            cp.wait_send()

    return pl.pallas_call(
        body, name=name, in_specs=[ANY] * n, out_specs=[ANY] * n,
        out_shape=[jax.ShapeDtypeStruct(r.shape, F32) for r in reds],
        input_output_aliases={w: w for w in range(n)},
        scratch_shapes=[pltpu.SemaphoreType.DMA((n,)), pltpu.SemaphoreType.DMA((n,))],
        compiler_params=pltpu.CompilerParams(collective_id=barrier_id),
    )(*reds)


HBM = pl.BlockSpec(memory_space=pltpu.HBM)
SEM = pl.BlockSpec(memory_space=pltpu.SEMAPHORE)
DATAFLOW = pltpu.SideEffectType.DATAFLOW_SIDE_EFFECTING


def _copies_start(name, srcs, lands, plan, n_copies, after, peers, barrier_id):
    ns, nb, na = len(srcs), len(srcs) + len(lands), len(after)

    def body(*refs):
        src_refs, land_refs = refs[:ns], refs[ns:nb]
        send_sems, recv_sems = refs[nb + na:nb + na + 2]
        token = refs[-1]
        _handshake(peers())
        for k, (src, there, _, to) in enumerate(plan(src_refs, land_refs)):
            _remote(src, there, send_sems, recv_sems, k, to).start()
        token[...] = jnp.zeros_like(token)

    hbm = lambda a: pltpu.HBM(a.shape, a.dtype)
    outs = pl.pallas_call(
        body, name=name,
        out_shape=(pltpu.SemaphoreType.DMA((n_copies,)), pltpu.SemaphoreType.DMA((n_copies,)),
                   *[hbm(a) for a in srcs], *[hbm(a) for a in lands], jax.ShapeDtypeStruct((8, 128), F32)),
        in_specs=[HBM] * nb + [ANY] * na,
        out_specs=(SEM, SEM, *[HBM] * nb, pl.BlockSpec(memory_space=pltpu.VMEM)),
        input_output_aliases={i: 2 + i for i in range(nb)},
        compiler_params=pltpu.CompilerParams(has_side_effects=DATAFLOW, collective_id=barrier_id),
    )(*[pltpu.with_memory_space_constraint(a, pltpu.HBM) for a in (*srcs, *lands)], *after)
    return outs[0], outs[1], outs[2:2 + ns], outs[2 + ns:2 + nb], outs[-1]


def _copies_wait(name, send_sems, recv_sems, srcs, lands, plan, after):
    ns, nb, na = len(srcs), len(srcs) + len(lands), len(after)

    def body(*refs):
        src_refs, land_refs = refs[:ns], refs[ns:nb]
        send_sems, recv_sems = refs[nb:nb + 2]
        for k, (src, _, here, to) in enumerate(plan(src_refs, land_refs)):
            cp = _remote(src, here, send_sems, recv_sems, k, to)
            cp.wait_send()
            cp.wait_recv()

    hbm = lambda a: pltpu.HBM(a.shape, a.dtype)
    outs = pl.pallas_call(
        body, name=name,
        out_shape=(*[hbm(a) for a in srcs], *[hbm(a) for a in lands]),
        in_specs=[HBM] * nb + [SEM, SEM] + [ANY] * na,
        out_specs=tuple([HBM] * nb),
        input_output_aliases={i: i for i in range(nb)},
        compiler_params=pltpu.CompilerParams(has_side_effects=DATAFLOW),
    )(*srcs, *lands, send_sems, recv_sems, *after)
    return outs[:ns], outs[ns:]


def _gather_plan(halves):
    def plan(shards, lands):
        x, y, c, chips = _place()
        me = 2 * x + y
        copies = []
        for w, h in enumerate(halves):
            rows = pl.ds(c * h, h)
            for chip in chips:
                copies.append((shards[w].at[rows, :], lands[w].at[me, rows, :],
                               lands[w].at[2 * chip[0] + chip[1], rows, :], (*chip, c)))
        return copies
    return plan


def _reduce_plan(n):
    def plan(ps, lands):
        x, y, c, chips = _place()
        return [(ps[w].at[2 * chip[0] + chip[1]], lands[w].at[j], lands[w].at[j], (*chip, c))
                for w in range(n) for j, chip in enumerate(chips)]
    return plan


def _forward_plan(halves):
    def plan(_, lands):
        x, y, c, chips = _place()

        def part(w, chip, half):
            return lands[w].at[2 * chip[0] + chip[1], pl.ds(half * halves[w], halves[w]), :]

        return [(part(w, chip, c), part(w, chip, c), part(w, chip, 1 - c), (x, y, 1 - c))
                for w in range(len(halves)) for chip in chips]
    return plan


def _pair_plan(halves):
    def plan(gs, gots):
        x, y, c, _ = _place()
        return [(gs[w].at[:, pl.ds((1 - c) * h, h), :], gots[w], gots[w], (x, y, 1 - c)) for w, h in enumerate(halves)]
    return plan


def _place_own(gathered, shards):
    chip = 2 * lax.axis_index("x") + lax.axis_index("y")
    return [lax.dynamic_update_slice(o, s[None], (chip, 0, 0)) for o, s in zip(gathered, shards)]


def _adamw(w, g, m, v, name, tr=None):
    R, C = w.shape
    tr = tr or R // 4

    def body(w_ref, g_ref, m_ref, v_ref, g_out, d_ref, nm_ref, nv_ref):
        gv = g_ref[...]
        g_out[...] = gv
        d_ref[...], nm_ref[...], nv_ref[...] = _adamw_math(w_ref[...], gv, m_ref[...], v_ref[...])

    blk = pl.BlockSpec((tr, C), lambda i: (i, 0))
    return pl.pallas_call(body, name=name, grid=(R // tr,), in_specs=[blk] * 4, out_specs=[blk] * 4,
                          out_shape=[jax.ShapeDtypeStruct((R, C), F32)] * 4, compiler_params=_cp("arbitrary"))(w, g, m, v)


SMALL = (("norm1_g", 1, 1024), ("attn_norm_g", 1, 512), ("hgrn_norm_g", 1, 512), ("hgrn_lb_logits", 2, 512),
         ("norm2_g", 1, 1024), ("conv_b", 1, D_FF), ("final_norm_g", 1, 1024), ("conv_w", 3, D_FF))
LOSS_ROW = sum(r * c for _, r, c in SMALL) // 128
SMALL_ROWS = 136


def _rows_to_lanes(ref, row, width):
    return jnp.concatenate([ref[row + j:row + j + 1, :] for j in range(width // 128)], axis=1)


def _pack_small(grads, dlb, lb, loss):
    def body(*refs):
        parts, dlb_ref, lb_ref, loss_ref, out = refs[:len(SMALL) - 1], refs[-4], refs[-3], refs[-2], refs[-1]
        out[...] = jnp.zeros_like(out)
        lbv = lb_ref[...]
        dl = dlb_ref[...] * lbv * (1.0 - lbv)
        row = 0
        parts = list(parts)
        for name, rows, width in SMALL:
            for r in range(rows):
                if name == "hgrn_lb_logits":
                    src = dl if r == 0 else -dl
                    for j in range(width // 128):
                        out[row + j:row + j + 1, :] = src[:, 128 * j:128 * (j + 1)]
                else:
                    for j in range(width // 128):
                        out[row + j:row + j + 1, :] = parts[0][r:r + 1, 128 * j:128 * (j + 1)]
                row += width // 128
            if name != "hgrn_lb_logits":
                parts.pop(0)
        out[LOSS_ROW:LOSS_ROW + 1, :] = loss_ref[...]

    vm = pl.BlockSpec(memory_space=pltpu.VMEM)
    return pl.pallas_call(body, name="pack_small", in_specs=[vm] * (len(grads) + 3), out_specs=vm,
                          out_shape=jax.ShapeDtypeStruct((SMALL_ROWS, 128), F32))(*grads, dlb, lb, loss)


def _adamw_math(w, g, m, v):
    nm = ADAM_B1 * m + (1.0 - ADAM_B1) * g
    nv = ADAM_B2 * v + (1.0 - ADAM_B2) * (g * g)
    m_hat = nm / (1.0 - ADAM_B1 ** ADAM_STEP)
    v_hat = nv / (1.0 - ADAM_B2 ** ADAM_STEP)
    return -ADAM_LR * (m_hat / (jnp.sqrt(v_hat) + ADAM_EPS) + ADAM_WD * w), nm, nv


def _small_update(summed, g_conv_w, ws, ms, vs):
    n = len(SMALL)

    def body(*refs):
        s_ref, gcw_ref = refs[:2]
        w_refs, m_refs, v_refs = refs[2:2 + n], refs[2 + n:2 + 2 * n], refs[2 + 2 * n:2 + 3 * n]
        outs = refs[2 + 3 * n:]
        row = 0
        for k, (name, rows, width) in enumerate(SMALL):
            if name == "conv_w":
                g = gcw_ref[...]
            else:
                g = jnp.concatenate([_rows_to_lanes(s_ref, row + r * (width // 128), width) for r in range(rows)], axis=0)
            row += rows * (width // 128)
            d, nm, nv = _adamw_math(w_refs[k][...], g, m_refs[k][...], v_refs[k][...])
            for o, val in zip(outs[4 * k:4 * k + 4], (g, d, nm, nv)):
                o[...] = val

    vm = pl.BlockSpec(memory_space=pltpu.VMEM)
    outs = pl.pallas_call(
        body, name="small_update", in_specs=[vm] * (2 + 3 * n), out_specs=[vm] * (4 * n),
        out_shape=[jax.ShapeDtypeStruct(a.shape, F32) for a in ws for _ in range(4)],
    )(summed, g_conv_w, *ws, *ms, *vs)
    return [outs[4 * k:4 * k + 4] for k in range(n)]


def kernel(x, norm1_g, w_in, attn_norm_g, hgrn_norm_g, hgrn_lb_logits, w_out, norm2_g, w_up, conv_w, conv_b, w_down, final_norm_g, loss_target, m_norm1_g, m_w_in, m_attn_norm_g, m_hgrn_norm_g, m_hgrn_lb_logits, m_w_out, m_norm2_g, m_w_up, m_conv_w, m_conv_b, m_w_down, m_final_norm_g, v_norm1_g, v_w_in, v_attn_norm_g, v_hgrn_norm_g, v_hgrn_lb_logits, v_w_out, v_norm2_g, v_w_up, v_conv_w, v_conv_b, v_w_down, v_final_norm_g):
    w = dict(norm1_g=norm1_g, w_in=w_in, attn_norm_g=attn_norm_g, hgrn_norm_g=hgrn_norm_g,
             hgrn_lb_logits=hgrn_lb_logits, w_out=w_out, norm2_g=norm2_g, w_up=w_up, conv_w=conv_w, conv_b=conv_b,
             w_down=w_down, final_norm_g=final_norm_g)
    m = dict(norm1_g=m_norm1_g, w_in=m_w_in, attn_norm_g=m_attn_norm_g, hgrn_norm_g=m_hgrn_norm_g,
             hgrn_lb_logits=m_hgrn_lb_logits, w_out=m_w_out, norm2_g=m_norm2_g, w_up=m_w_up, conv_w=m_conv_w,
             conv_b=m_conv_b, w_down=m_w_down, final_norm_g=m_final_norm_g)
    v = dict(norm1_g=v_norm1_g, w_in=v_w_in, attn_norm_g=v_attn_norm_g, hgrn_norm_g=v_hgrn_norm_g,
             hgrn_lb_logits=v_hgrn_lb_logits, w_out=v_w_out, norm2_g=v_norm2_g, w_up=v_w_up, conv_w=v_conv_w,
             conv_b=v_conv_b, w_down=v_w_down, final_norm_g=v_final_norm_g)
    names = list(w)
    chip = 2 * lax.axis_index("x") + lax.axis_index("y")

    shards = {k: w[k][0].astype(BF16) for k in BIG}
    w_in4, conv_w4 = _gather_weights([shards["w_in"]], conv_w[0])
    conv_w_full = jnp.transpose(conv_w4, (1, 0, 2)).reshape(3, D_FF)
    lb = jax.nn.softmax(hgrn_lb_logits, axis=0)[0:1]
    late = [shards[k] for k in BIG[1:]]
    gather_plan = _gather_plan([s.shape[0] // 2 for s in late])
    started = _copies_start("gather_start", late, [lax.empty((N_CHIPS,) + s.shape, BF16) for s in late], gather_plan,
                            3 * len(late), after=(w_in4,), peers=_chip_peers, barrier_id=0)
    u1, qkv, hg = _in_proj(x[0], norm1_g + started[4][0:1, 0:1], w_in4)
    attn_o, lse = _attn_fwd(qkv)
    late, landed_w = _copies_wait("gather_wait", *started[:4], gather_plan, after=(attn_o,))
    forward_plan = _forward_plan([s.shape[0] // 2 for s in late])
    started = _copies_start("forward_start", [], landed_w, forward_plan, 3 * len(late), after=(),
                            peers=_sibling_peer, barrier_id=1)
    rec_o, states = _hgrn_fwd(hg, lb + started[4][0:1, 0:1])
    a = dict(u1=u1, qkv=qkv, hg=hg, attn_o=attn_o, lse=lse, rec_o=rec_o, states=states)
    w_out4, w_up4, w_down4 = _place_own(
        _copies_wait("forward_wait", *started[:4], forward_plan, after=(rec_o,))[1], late)

    b = _step_channel(a, x[0], loss_target[0], attn_norm_g, hgrn_norm_g, w_out4.reshape(D_MODEL, D_MODEL), norm2_g,
                      w_up4, conv_w_full, conv_b, w_down4.reshape(D_FF, D_MODEL), final_norm_g.reshape(1, D_MODEL))

    early = [b["dw_out"], b["dw_up"], b["dw_down"]]
    pair_plan = _pair_plan([gk.shape[1] // 2 for gk in early])
    started = _copies_start("pair_start", early,
                            [lax.empty((N_CHIPS, gk.shape[1] // 2, gk.shape[2]), BF16) for gk in early], pair_plan,
                            len(early), after=(), peers=_sibling_peer, barrier_id=2)
    dqkv = _attn_bwd(qkv, attn_o, lse, b["da"], started[4])
    early, gots = _copies_wait("pair_wait", *started[:4], pair_plan, after=(dqkv[0],))
    ps = _pair_sum(early, gots, "pair_sum")
    reduce_plan = _reduce_plan(len(ps))
    started = _copies_start("reduce_start", ps, [lax.empty((3,) + p.shape[1:], BF16) for p in ps], reduce_plan,
                            3 * len(ps), after=(), peers=_chip_peers, barrier_id=3)
    c = _step_mixers_bwd(a, b, x[0], norm1_g, w_in4, lb + started[4][0:1, 0:1], dqkv)
    gots_in = _pair_exchange([c["dw_in"]], "pair_exchange_w_in", barrier_id=4)
    ps_in = _pair_sum([c["dw_in"]], gots_in, "pair_sum_w_in")[0]
    plan_in = _reduce_plan(1)
    started_in = _copies_start("reduce_start_w_in", [ps_in], [lax.empty((3,) + ps_in.shape[1:], BF16)], plan_in, 3,
                               after=(), peers=_chip_peers, barrier_id=5)
    landed = _copies_wait("reduce_wait", *started[:4], reduce_plan, after=(started_in[4],))[1]
    reds = _sum_partials(early, gots, landed, "sum_partials")
    g = dict(zip(BIG[1:], _pair_share(reds, "pair_share", barrier_id=6)))
    delta, new_m, new_v = {}, {}, {}
    for k in BIG[1:]:
        g[k], delta[k], new_m[k], new_v[k] = _adamw(w[k][0], g[k], m[k][0], v[k][0], f"adamw_{k}")

    loss, dx = b["loss"], c["dx"]
    small = dict(g1=c["dg1"], g_a=b["dga"], g_h=b["dgh"], lb=c["dlb"], g2=b["dg2"], conv_w=b["dcw"], conv_b=b["dcb"],
                 gf=b["dgf"])
    summed = _allreduce_small(_pack_small(
        [small["g1"], small["g_a"], small["g_h"], small["g2"], small["conv_b"], small["gf"], small["conv_w"]],
        small["lb"], lb, loss))
    loss_total = summed[LOSS_ROW, 0]
    g_conv_w = lax.dynamic_slice(summed[LOSS_ROW - 3 * D_FF // 128:LOSS_ROW].reshape(3, D_FF),
                                 (0, chip * (D_FF // N_CHIPS)), (3, D_FF // N_CHIPS))
    two_d = lambda p, k: p[k].reshape(-1, p[k].shape[-1])
    updated = _small_update(summed, g_conv_w, *[[two_d(p, k) for k, _, _ in SMALL] for p in (w, m, v)])
    for (k, _, _), parts in zip(SMALL, updated):
        g[k], delta[k], new_m[k], new_v[k] = (a.reshape(w[k].shape) for a in parts)

    landed_in = _copies_wait("reduce_wait_w_in", *started_in[:4], plan_in, after=(updated[0][1], delta["w_up"]))[1]
    red_in = _sum_partials([c["dw_in"]], gots_in, landed_in, "sum_partials_w_in")
    g["w_in"] = _pair_share(red_in, "pair_share_w_in", barrier_id=7)[0]
    g["w_in"], delta["w_in"], new_m["w_in"], new_v["w_in"] = _adamw(w_in[0], g["w_in"], m_w_in[0], v_w_in[0],
                                                                    "adamw_w_in")
    for k in BIG:
        g[k], delta[k], new_m[k], new_v[k] = g[k][None], delta[k][None], new_m[k][None], new_v[k][None]

    return (loss_total, dx[None], *[g[k] for k in names], *[delta[k] for k in names],
            *[new_m[k] for k in names], *[new_v[k] for k in names])
```

```python
import math

import jax
import jax.numpy as jnp
from jax import lax
from jax.experimental import pallas as pl
from jax.experimental.pallas import tpu as pltpu

F32 = jnp.float32
BF16 = jnp.bfloat16

D_MODEL = 1024
ATTN_W = 512
HGRN_W = 512
HEAD_PAIR = 128
ATTN_BLK = 128
DILATIONS = (1, 4, 16)
ATTN_CHAINS = 4
ATTN_CHAINS_FWD = 8
HGRN_HEADS = 4
HGRN_DIM = 128
HGRN_CHUNK = 64
SUPER = 256
HGRN_SIDE = 8
D_FF = 2816
FF_CHUNKS = ((0, 1536), (1536, D_FF))
MLP_BWD_CHUNKS = ((0, 768), (768, 1408), (1408, 2176), (2176, D_FF))
N_CHIPS = 4
IN_TOTAL = 3584
IN_SHARD = IN_TOTAL // N_CHIPS
UP_SHARD = 2 * D_FF // N_CHIPS
QKV_W = 3 * ATTN_W
HG_W = 4 * HGRN_W
EPS = 1e-6
NEG = -1e30
V7X_VMEM_BYTES = 64 * 1024 * 1024
VMEM_LIMIT = V7X_VMEM_BYTES - 8 * 1024 * 1024

ADAM_LR = 0.001
ADAM_B1 = 0.9
ADAM_B2 = 0.999
ADAM_EPS = 1e-08
ADAM_WD = 0.01
ADAM_STEP = 10

MESH = pl.DeviceIdType.MESH


def _cp(*sem):
    return pltpu.CompilerParams(dimension_semantics=sem or None, vmem_limit_bytes=VMEM_LIMIT)


def _dot(a, b):
    return jnp.dot(a, b, preferred_element_type=F32)


def _dot_nt(a, b):
    return lax.dot_general(a, b, (((1,), (1,)), ((), ())), preferred_element_type=F32)


def _dot_tn(a, b):
    return lax.dot_general(a, b, (((0,), (0,)), ((), ())), preferred_element_type=F32)


def _sigmoid(x):
    return 1.0 / (1.0 + jnp.exp(-x))


def _rms(x, width):
    return lax.rsqrt(jnp.sum(x * x, axis=-1, keepdims=True) * (1.0 / width) + EPS)


def _rms_bwd(dn, n, r, width):
    return r * (dn - n * (jnp.sum(dn * n, axis=-1, keepdims=True) * (1.0 / width)))


def _colsum(x):
    return jnp.sum(x, axis=0, keepdims=True)


def _row(v, k):
    rid = lax.broadcasted_iota(jnp.int32, v.shape, 0)
    return jnp.sum(jnp.where(rid == k, v, 0.0), axis=0, keepdims=True)


def _full(shape):
    return pl.BlockSpec(shape, lambda *_: (0,) * len(shape))


def _once(shape):
    return pl.BlockSpec(shape, lambda *_: (0,) * len(shape), pipeline_mode=pl.Buffered(1))


def _load_side_by_side(w_hbm, w_full, sem):
    width = w_hbm.shape[2]
    cps = [pltpu.make_async_copy(w_hbm.at[k], w_full.at[:, pl.ds(k * width, width)], sem.at[k]) for k in range(N_CHIPS)]
    for cp in cps:
        cp.start()
    for cp in cps:
        cp.wait()


def _in_proj(x, g1, w_in4, tm=512):
    T = x.shape[0]

    def body(x_ref, g_ref, w_hbm, u_ref, qkv_ref, hg_ref, w_full, sem):
        @pl.when(pl.program_id(0) == 0)
        def _():
            _load_side_by_side(w_hbm, w_full, sem)

        xv = x_ref[...]
        u = (xv * _rms(xv, D_MODEL) * g_ref[...]).astype(BF16)
        u_ref[...] = u
        p = _dot(u, w_full[...])
        qkv_ref[...] = p[:, :QKV_W]
        hg_ref[...] = p[:, QKV_W:]

    return pl.pallas_call(
        body, name="in_proj", grid=(T // tm,),
        in_specs=[pl.BlockSpec((tm, D_MODEL), lambda i: (i, 0)), _full((1, D_MODEL)), ANY],
        out_specs=[pl.BlockSpec((tm, D_MODEL), lambda i: (i, 0)), pl.BlockSpec((tm, QKV_W), lambda i: (i, 0)),
                   pl.BlockSpec((tm, HG_W), lambda i: (i, 0))],
        out_shape=[jax.ShapeDtypeStruct((T, D_MODEL), BF16), jax.ShapeDtypeStruct((T, QKV_W), F32),
                   jax.ShapeDtypeStruct((T, HG_W), F32)],
        scratch_shapes=[pltpu.VMEM((D_MODEL, IN_TOTAL), BF16), pltpu.SemaphoreType.DMA((N_CHIPS,))],
        compiler_params=_cp("arbitrary"),
    )(x, g1, w_in4)


def _attn_masks(bias_ref):
    lane = lax.broadcasted_iota(jnp.int32, (ATTN_BLK, HEAD_PAIR), 1)
    row = lax.broadcasted_iota(jnp.int32, (2 * ATTN_BLK, 2 * ATTN_BLK), 0)
    col = lax.broadcasted_iota(jnp.int32, (2 * ATTN_BLK, 2 * ATTN_BLK), 1)
    base = jnp.where(row >= ATTN_BLK, row - ATTN_BLK, row) - col
    for k in range(2):
        dist = base + k * ATTN_BLK
        bias_ref[k] = jnp.where((dist >= 0) & (dist <= ATTN_BLK), 0.0, NEG)
    bias_ref[2] = jnp.where(col >= ATTN_BLK, bias_ref[1], NEG)
    return lane < 64


def _two_heads(blk, first):
    zero = jnp.zeros_like(blk)
    return jnp.concatenate([jnp.where(first, blk, zero), jnp.where(first, zero, blk)], axis=0)


def _attn_rows(idx, nb, d):
    r, n = idx // nb, idx % nb
    kb = jnp.maximum(n - 1, 0)
    if d == 1:
        q0 = pl.multiple_of(n * ATTN_BLK, ATTN_BLK)
        k0 = pl.multiple_of(kb * ATTN_BLK, ATTN_BLK)
        return pl.ds(q0, ATTN_BLK), pl.ds(k0, 2 * ATTN_BLK), n - kb
    return (pl.ds(r + d * ATTN_BLK * n, ATTN_BLK, stride=d), pl.ds(r + d * ATTN_BLK * kb, 2 * ATTN_BLK, stride=d),
            n - kb)


def _attn_fwd(qkv):
    T = qkv.shape[0]

    n_blocks = T // ATTN_BLK

    def body(q_ref, k_ref, v_ref, o_ref, m_ref, l_ref, bias_ref):
        first = _attn_masks(bias_ref)
        for bi, d in enumerate(DILATIONS):
            nb = T // d // ATTN_BLK

            chains = ATTN_CHAINS_FWD
            per_chain = n_blocks // chains
            carried = d > 1 and per_chain % nb == 0

            def block(idx, kept=None, d=d, nb=nb, bi=bi, carried=carried):
                rows, keys, which = _attn_rows(idx, nb, d)
                q2 = _two_heads(q_ref[rows, :] * 0.125, first).astype(BF16)
                if carried:
                    k_own, v_own = k_ref[rows, :].astype(BF16), v_ref[rows, :].astype(BF16)
                    kw = jnp.concatenate([kept[0], k_own], axis=0)
                    vw = jnp.concatenate([kept[1], v_own], axis=0)
                    which = 2 - which
                else:
                    kw = k_ref[keys, :].astype(BF16)
                    vw = v_ref[keys, :].astype(BF16)
                old = (o_ref[rows, :], m_ref[rows, :], l_ref[rows, :]) if bi else None
                s = _dot_nt(q2, kw) + bias_ref[which]
                mb = jnp.max(s, axis=-1, keepdims=True)
                p = jnp.exp(s - mb)
                lb = jnp.sum(p, axis=-1, keepdims=True)
                o2 = _dot(p.astype(BF16), vw)
                o = jnp.where(first, o2[:ATTN_BLK], o2[ATTN_BLK:])
                m = jnp.where(first, mb[:ATTN_BLK], mb[ATTN_BLK:])
                l = jnp.where(first, lb[:ATTN_BLK], lb[ATTN_BLK:])
                if bi:
                    po, pm, pl_ = old
                    mn = jnp.maximum(pm, m)
                    wa = jnp.exp(pm - mn)
                    wb = jnp.exp(m - mn)
                    o, l, m = po * wa + o * wb, pl_ * wa + l * wb, mn
                return (rows, o, m, l), ((k_own, v_own) if carried else 0)

            def step(i, kept, block=block, carried=carried, chains=chains, per_chain=per_chain):
                done = [block(i + ch * per_chain, kept[ch] if carried else None) for ch in range(chains)]
                for (rows, o, m, l), _ in done:
                    o_ref[rows, :] = o
                    m_ref[rows, :] = m
                    l_ref[rows, :] = l
                return tuple(k for _, k in done) if carried else kept

            zero = jnp.zeros((ATTN_BLK, HEAD_PAIR), BF16)
            lax.fori_loop(0, per_chain, step, ((zero, zero),) * chains if carried else 0)

        def finish(i, carry):
            rows = pl.ds(pl.multiple_of(i * SUPER, SUPER), SUPER)
            l = l_ref[rows, :]
            o_ref[rows, :] = o_ref[rows, :] / l
            m_ref[rows, :] = m_ref[rows, :] + jnp.log(l)
            return carry

        lax.fori_loop(0, T // SUPER, finish, 0)

    col = lambda off: pl.BlockSpec((T, HEAD_PAIR), lambda j: (0, off + j))
    return pl.pallas_call(
        body, name="attn_fwd", grid=(4,),
        in_specs=[col(0), col(4), col(8)], out_specs=[col(0), col(0)],
        out_shape=[jax.ShapeDtypeStruct((T, ATTN_W), F32)] * 2,
        scratch_shapes=[pltpu.VMEM((T, HEAD_PAIR), F32), pltpu.VMEM((3, 2 * ATTN_BLK, 2 * ATTN_BLK), F32)],
        compiler_params=_cp("arbitrary"),
    )(qkv, qkv, qkv)


def _attn_bwd(qkv, o, lse, do, token=None):
    T = qkv.shape[0]
    per_chain = T // ATTN_BLK // ATTN_CHAINS
    extra = [] if token is None else [token]

    def body(q_ref, k_ref, v_ref, o_ref, lse_ref, do_ref, *rest):
        outs = rest[len(extra):len(extra) + 3]
        dq_ref, dk_ref, dv_ref, dkb_ref, dvb_ref, bias_ref = rest[len(extra) + 3:]
        first = _attn_masks(bias_ref)
        dq_ref[...] = jnp.zeros_like(dq_ref)
        dk_ref[...] = jnp.zeros_like(dk_ref)
        dv_ref[...] = jnp.zeros_like(dv_ref)

        def grads(rows, kw, vw, which):
            q2 = _two_heads(q_ref[rows, :] * 0.125, first).astype(BF16)
            lse_b = lse_ref[rows, :]
            dob = do_ref[rows, :]
            prod = dob * o_ref[rows, :]
            old = dq_ref[rows, :]
            lse2 = jnp.concatenate(
                [jnp.max(jnp.where(first, lse_b, NEG), axis=-1, keepdims=True),
                 jnp.max(jnp.where(first, NEG, lse_b), axis=-1, keepdims=True)], axis=0)
            p = jnp.exp(_dot_nt(q2, kw) + (bias_ref[which] - lse2))
            delta = jnp.concatenate(
                [jnp.sum(jnp.where(first, prod, 0.0), axis=-1, keepdims=True),
                 jnp.sum(jnp.where(first, 0.0, prod), axis=-1, keepdims=True)], axis=0)
            do2 = _two_heads(dob, first).astype(BF16)
            ds = (p * (_dot_nt(do2, vw) - delta)).astype(BF16)
            dq2 = _dot(ds, kw) * 0.125
            return (old + jnp.where(first, dq2[:ATTN_BLK], dq2[ATTN_BLK:]), _dot_tn(ds, q2),
                    _dot_tn(p.astype(BF16), do2))

        def block(idx):
            rows, keys, which = _attn_rows(idx, T // ATTN_BLK, 1)
            old = dk_ref[keys, :], dv_ref[keys, :]
            dq, ck, cv = grads(rows, k_ref[keys, :].astype(BF16), v_ref[keys, :].astype(BF16), which)
            return rows, keys, dq, old[0] + ck, old[1] + cv

        def step(i, carry):
            done = [block(i + ch * per_chain) for ch in range(ATTN_CHAINS)]
            for rows, keys, dq, dk, dv in done:
                dq_ref[rows, :] = dq
                dk_ref[keys, :] = dk
                dv_ref[keys, :] = dv
            return carry

        lax.fori_loop(0, per_chain, step, 0)

        for d in DILATIONS[1:]:
            nb = T // d // ATTN_BLK

            def block(idx, kept, d=d, nb=nb):
                r, n = idx // nb, idx % nb
                rows = pl.ds(r + d * ATTN_BLK * n, ATTN_BLK, stride=d)
                before = pl.ds(r + d * ATTN_BLK * jnp.maximum(n - 1, 0), ATTN_BLK, stride=d)
                k_prev, v_prev, dk_prev, dv_prev = kept
                k_own, v_own = k_ref[rows, :].astype(BF16), v_ref[rows, :].astype(BF16)
                dq, ck, cv = grads(rows, jnp.concatenate([k_prev, k_own], axis=0),
                                   jnp.concatenate([v_prev, v_own], axis=0), jnp.where(n > 0, 1, 2))
                stores = (rows, before, dq, dk_prev + ck[:ATTN_BLK], dv_prev + cv[:ATTN_BLK], ck[ATTN_BLK:], cv[ATTN_BLK:])
                return stores, (k_own, v_own, ck[ATTN_BLK:], cv[ATTN_BLK:])

            def step(i, kept, block=block):
                done = [block(i + ch * per_chain, kept[ch]) for ch in range(ATTN_CHAINS)]
                for (rows, before, dq, dk_done, dv_done, dk_own, dv_own), _ in done:
                    dq_ref[rows, :] = dq
                    dkb_ref[before, :] = dk_done
                    dvb_ref[before, :] = dv_done
                    dkb_ref[rows, :] = dk_own
                    dvb_ref[rows, :] = dv_own
                return tuple(k for _, k in done)

            zero = jnp.zeros((ATTN_BLK, HEAD_PAIR), F32)
            lax.fori_loop(0, per_chain, step, ((zero.astype(BF16), zero.astype(BF16), zero, zero),) * ATTN_CHAINS)

            def add(i, carry):
                rows = pl.ds(pl.multiple_of(i * SUPER, SUPER), SUPER)
                dk_ref[rows, :] += dkb_ref[rows, :]
                dv_ref[rows, :] += dvb_ref[rows, :]
                return carry

            lax.fori_loop(0, T // SUPER, add, 0)

        def emit(i, carry):
            rows = pl.ds(pl.multiple_of(i * SUPER, SUPER), SUPER)
            for out, acc in zip(outs, (dq_ref, dk_ref, dv_ref)):
                out[rows, :] = acc[rows, :].astype(BF16)
            return carry

        lax.fori_loop(0, T // SUPER, emit, 0)

    col = lambda off: pl.BlockSpec((T, HEAD_PAIR), lambda j: (0, off + j))
    return pl.pallas_call(
        body, name="attn_bwd", grid=(4,),
        in_specs=[col(0), col(4), col(8), col(0), col(0), col(0)] + [_full(t.shape) for t in extra],
        out_specs=[col(0)] * 3,
        out_shape=[jax.ShapeDtypeStruct((T, ATTN_W), BF16)] * 3,
        scratch_shapes=[pltpu.VMEM((T, HEAD_PAIR), F32)] * 5 + [pltpu.VMEM((3, 2 * ATTN_BLK, 2 * ATTN_BLK), F32)],
        compiler_params=_cp("arbitrary"),
    )(qkv, qkv, qkv, o, lse, do, *extra)


def _chunk_ids():
    row = lax.broadcasted_iota(jnp.int32, (SUPER, HGRN_DIM), 0)
    r2 = lax.broadcasted_iota(jnp.int32, (SUPER, SUPER), 0)
    c2 = lax.broadcasted_iota(jnp.int32, (SUPER, SUPER), 1)
    amask = ((r2 // HGRN_CHUNK) == (c2 // HGRN_CHUNK)) & (c2 <= r2)
    return row % HGRN_CHUNK, row // HGRN_CHUNK, amask


def _cumsum_chunk(x, rmod):
    s = 1
    while s < HGRN_CHUNK:
        x = x + jnp.where(rmod >= s, pltpu.roll(x, s, 0), 0.0)
        s *= 2
    return x


def _suffix_sum_chunk(x, rmod):
    s = 1
    while s < HGRN_CHUNK:
        x = x + jnp.where(rmod < HGRN_CHUNK - s, pltpu.roll(x, SUPER - s, 0), 0.0)
        s *= 2
    return x


def _chunk_rows(vs, cid):
    out = vs[-1]
    for c in reversed(range(len(vs) - 1)):
        out = jnp.where(cid == c, vs[c], out)
    return out


def _expand(x, cid):
    return jnp.concatenate([jnp.where(cid == c, x, 0.0) for c in range(SUPER // HGRN_CHUNK)], axis=1)


def _hgrn_gates(q, f, lbv, rmod, cid, tmp):
    sq = _sigmoid(q)
    sg = _sigmoid(f)
    forget = lbv + (1.0 - lbv) * sg
    key = 1.0 - forget
    b = _cumsum_chunk(jnp.log(forget), rmod)
    tmp[...] = b
    bends = [tmp[c * HGRN_CHUNK + HGRN_CHUNK - 1:(c + 1) * HGRN_CHUNK, :] for c in range(SUPER // HGRN_CHUNK)]
    eb = jnp.exp(b)
    enb = jnp.exp(-b)
    ebe = jnp.exp(_chunk_rows(bends, cid) - b)
    return sq, sg, forget, key, eb, enb, ebe, q * sq * eb, key * enb, key * ebe, [jnp.exp(v) for v in bends]


def _hgrn_fwd(hg, lb):
    T = hg.shape[0]
    nsc = T // SUPER
    NC = SUPER // HGRN_CHUNK

    def body(q_ref, f_ref, i_ref, lb_ref, o_ref, st_ref, state, tmp):
        rmod, cid, amask = _chunk_ids()
        state[...] = jnp.zeros_like(state)
        lbv = lb_ref[...]

        def local(sc, u):
            rows = pl.ds(pl.multiple_of(sc * SUPER, SUPER), SUPER)
            iv = i_ref[rows, :].astype(BF16)
            qd, ki, ke, dec = _hgrn_gates(q_ref[rows, :], f_ref[rows, :], lbv, rmod, cid, tmp.at[u])[-4:]
            a = jnp.where(amask, _dot_nt(qd.astype(BF16), ki.astype(BF16)), 0.0)
            return rows, qd, dec, _dot(a.astype(BF16), iv), _dot_tn(iv, _expand(ke, cid).astype(BF16))

        def step(i, carry):
            parts = [local(i * HGRN_SIDE + u, u) for u in range(HGRN_SIDE)]
            st = state[...]
            entering = []
            for u, (_, _, dec, _, ut) in enumerate(parts):
                st_ref[0, i * HGRN_SIDE + u] = st
                sts = []
                for c in range(NC):
                    sts.append(st)
                    st = st * dec[c] + ut[:, c * HGRN_DIM:(c + 1) * HGRN_DIM]
                entering.append(jnp.concatenate(sts, axis=1).astype(BF16))
            state[...] = st
            for (rows, qd, _, o, _), sts in zip(parts, entering):
                o_ref[rows, :] = o + _dot_nt(_expand(qd, cid).astype(BF16), sts)
            return carry

        lax.fori_loop(0, nsc // HGRN_SIDE, step, 0)

    col = lambda off: pl.BlockSpec((T, HGRN_DIM), lambda h: (0, off + h))
    return pl.pallas_call(
        body, name="hgrn_fwd", grid=(HGRN_HEADS,),
        in_specs=[col(0), col(4), col(8), pl.BlockSpec((1, HGRN_DIM), lambda h: (0, h))],
        out_specs=[pl.BlockSpec((T, HGRN_DIM), lambda h: (0, h)),
                   pl.BlockSpec((1, nsc, HGRN_DIM, HGRN_DIM), lambda h: (h, 0, 0, 0))],
        out_shape=[jax.ShapeDtypeStruct((T, HGRN_W), F32),
                   jax.ShapeDtypeStruct((HGRN_HEADS, nsc, HGRN_DIM, HGRN_DIM), F32)],
        scratch_shapes=[pltpu.VMEM((HGRN_DIM, HGRN_DIM), F32), pltpu.VMEM((HGRN_SIDE, SUPER, HGRN_DIM), F32)],
        compiler_params=_cp("arbitrary"),
    )(hg, hg, hg, lb)


def _hgrn_bwd(hg, lb, states, do):
    T = hg.shape[0]
    nsc = T // SUPER
    NC = SUPER // HGRN_CHUNK

    def body(q_ref, f_ref, i_ref, lb_ref, st_ref, do_ref, dq_ref, df_ref, di_ref, dlb_ref, dstate, tmp):
        rmod, cid, amask = _chunk_ids()
        dstate[...] = jnp.zeros_like(dstate)
        dlb_ref[...] = jnp.zeros_like(dlb_ref)
        lbv = lb_ref[...]

        def local(sc, u):
            rows = pl.ds(pl.multiple_of(sc * SUPER, SUPER), SUPER)
            q = q_ref[rows, :]
            ivf = i_ref[rows, :]
            iv = ivf.astype(BF16)
            dof = do_ref[rows, :]
            dob = dof.astype(BF16)
            sq, sg, forget, key, eb, enb, ebe, qd, ki, ke, dec = _hgrn_gates(q, f_ref[rows, :], lbv, rmod, cid,
                                                                            tmp.at[u])
            qdb, kib = qd.astype(BF16), ki.astype(BF16)
            keexp = _expand(ke, cid).astype(BF16)
            a = jnp.where(amask, _dot_nt(qdb, kib), 0.0).astype(BF16)
            ut = _dot_tn(iv, keexp)
            st = st_ref[0, sc]
            sts = []
            for c in range(NC):
                sts.append(st)
                st = st * dec[c] + ut[:, c * HGRN_DIM:(c + 1) * HGRN_DIM]
            gt = _dot_tn(dob, _expand(qd, cid).astype(BF16))
            da = jnp.where(amask, _dot_nt(dob, iv), 0.0).astype(BF16)
            ststack = jnp.concatenate(sts, axis=0).astype(BF16)
            return dict(rows=rows, q=q, sq=sq, sg=sg, forget=forget, eb=eb, enb=enb, ebe=ebe, qd=qd, ki=ki, ke=ke,
                        dec=dec, sts=sts, gt=gt, keexp=keexp, ivexp=_expand(ivf, cid).astype(BF16),
                        div=_dot_tn(a, dob), dki=_dot_tn(da, qdb),
                        dqd=_dot(da, kib) + _dot(_expand(dof, cid).astype(BF16), ststack))

        def finish(p, nxt, ddec):
            ncat = jnp.concatenate(nxt, axis=1).astype(BF16)
            nstack = jnp.concatenate(nxt, axis=0).astype(BF16)
            dke = _dot(p["ivexp"], nstack)
            dkk = dke * p["ke"]
            dkey = p["dki"] * p["enb"] + dke * p["ebe"]
            db = p["dqd"] * p["qd"] - p["dki"] * p["ki"] - dkk
            dbends = [_colsum(jnp.where(cid == c, dkk, 0.0)) + ddec[c] * p["dec"][c] for c in range(NC)]
            dforget = (_suffix_sum_chunk(db, rmod) + _chunk_rows(dbends, cid)) / p["forget"] - dkey
            sg, sq, q = p["sg"], p["sq"], p["q"]
            df_ref[p["rows"], :] = (dforget * (1.0 - lbv) * sg * (1.0 - sg)).astype(BF16)
            dq_ref[p["rows"], :] = (p["dqd"] * p["eb"] * (sq * (1.0 + q * (1.0 - sq)))).astype(BF16)
            di_ref[p["rows"], :] = (p["div"] + _dot_nt(p["keexp"], ncat)).astype(BF16)
            return _colsum(dforget * (1.0 - sg))

        def step(i, carry):
            parts = [local(nsc - 1 - (i * HGRN_SIDE + u), u) for u in range(HGRN_SIDE)]
            dst = dstate[...]
            chained = []
            for p in parts:
                nxt = [None] * NC
                ddec = [None] * NC
                for c in reversed(range(NC)):
                    nxt[c] = dst
                    ddec[c] = _colsum(dst * p["sts"][c])
                    dst = dst * p["dec"][c] + p["gt"][:, c * HGRN_DIM:(c + 1) * HGRN_DIM]
                chained.append((nxt, ddec))
            dstate[...] = dst
            dlb = dlb_ref[...]
            for p, (nxt, ddec) in zip(parts, chained):
                dlb = dlb + finish(p, nxt, ddec)
            dlb_ref[...] = dlb
            return carry

        lax.fori_loop(0, nsc // HGRN_SIDE, step, 0)

    col = lambda off: pl.BlockSpec((T, HGRN_DIM), lambda h: (0, off + h))
    own = pl.BlockSpec((T, HGRN_DIM), lambda h: (0, h))
    vec = pl.BlockSpec((1, HGRN_DIM), lambda h: (0, h))
    return pl.pallas_call(
        body, name="hgrn_bwd", grid=(HGRN_HEADS,),
        in_specs=[col(0), col(4), col(8), vec,
                  pl.BlockSpec((1, nsc, HGRN_DIM, HGRN_DIM), lambda h: (h, 0, 0, 0)), own],
        out_specs=[own, own, own, vec],
        out_shape=[jax.ShapeDtypeStruct((T, HGRN_W), BF16)] * 3 + [jax.ShapeDtypeStruct((1, HGRN_W), F32)],
        scratch_shapes=[pltpu.VMEM((HGRN_DIM, HGRN_DIM), F32), pltpu.VMEM((HGRN_SIDE, SUPER, HGRN_DIM), F32)],
        compiler_params=_cp("arbitrary"),
    )(hg, hg, hg, lb, states, do)


def _rec_heads(rec, gate, g_h):
    rr = jnp.concatenate(
        [jnp.broadcast_to(_rms(rec[:, h * HGRN_DIM:(h + 1) * HGRN_DIM], HGRN_DIM), (rec.shape[0], HGRN_DIM))
         for h in range(HGRN_HEADS)], axis=1)
    rn = rec * rr
    sg = _sigmoid(gate)
    return rr, rn, sg


_INV_SQRT2 = 1.0 / math.sqrt(2.0)
_INV_SQRT2PI = 1.0 / math.sqrt(2.0 * math.pi)


def _gelu(x):
    return 0.5 * x * (1.0 + lax.erf(x * _INV_SQRT2))


def _gelu_and_grad(x):
    z = x * _INV_SQRT2
    cdf = 0.5 * (1.0 + lax.erf(z))
    return x * cdf, cdf + (x * _INV_SQRT2PI) * jnp.exp(-(z * z))


def _shift_down(g, prev, rowid):
    p1 = _row(prev, prev.shape[0] - 1)
    p2 = _row(prev, prev.shape[0] - 2)
    s1 = jnp.where(rowid == 0, p1, pltpu.roll(g, 1, 0))
    s2 = jnp.where(rowid == 0, p2, jnp.where(rowid == 1, p1, pltpu.roll(g, 2, 0)))
    return s1, s2


def _mlp_fwd(attn_o, rec_o, hg, x, g_a, g_h, w_out, g2, w_up4, conv_w, conv_b, w_down, gf, tgt, tm=256):
    T = x.shape[0]

    def body(a_ref, r_ref, gt_ref, x_ref, ga_ref, gh_ref, wo_ref, g2_ref, wu_hbm, cw_ref, cb_ref, wd_ref, gf_ref, t_ref,
             h1_ref, mixed_ref, u_ref, gate_ref, val_ref, conv_ref, act_ref, dh_ref, loss_ref, dgf_ref,
             carry, wu_ref, sem):
        i = pl.program_id(0)

        @pl.when(i == 0)
        def _():
            carry[...] = jnp.zeros_like(carry)
            loss_ref[...] = jnp.zeros_like(loss_ref)
            dgf_ref[...] = jnp.zeros_like(dgf_ref)
            _load_side_by_side(wu_hbm, wu_ref, sem)

        a = a_ref[...]
        an = a * _rms(a, ATTN_W) * ga_ref[...]
        og = gt_ref[...]
        _, rn, sg = _rec_heads(r_ref[...], og, gh_ref[...])
        mixed = jnp.concatenate([an, rn * gh_ref[...] * (og * sg)], axis=1).astype(BF16)
        mixed_ref[...] = mixed
        h = x_ref[...] + _dot(mixed, wo_ref[...])
        h1_ref[...] = h
        u = (h * _rms(h, D_MODEL) * g2_ref[...]).astype(BF16)
        u_ref[...] = u
        y2 = jnp.zeros((tm, D_MODEL), F32)
        for lo, hi in FF_CHUNKS:
            cols = slice(lo, hi)
            rowid = lax.broadcasted_iota(jnp.int32, (tm, hi - lo), 0)
            gb = _dot(u, wu_ref[:, lo:hi]).astype(BF16)
            vb = _dot(u, wu_ref[:, D_FF + lo:D_FF + hi]).astype(BF16)
            gate_ref[:, cols] = gb
            val_ref[:, cols] = vb
            g = gb.astype(F32)
            s1, s2 = _shift_down(g, carry[:, cols], rowid)
            carry[:, cols] = g[tm - 8:, :]
            conv = cb_ref[:, cols] + cw_ref[0:1, cols] * s2 + cw_ref[1:2, cols] * s1 + cw_ref[2:3, cols] * g
            act = (_gelu(conv) * vb.astype(F32)).astype(BF16)
            conv_ref[:, cols] = conv.astype(BF16)
            act_ref[:, cols] = act
            y2 = y2 + _dot(act, wd_ref[cols, :])
        h2 = h + y2
        rf = _rms(h2, D_MODEL)
        n = h2 * rf
        gfv = gf_ref[...]
        e = n * gfv - t_ref[...]
        loss_ref[...] += jnp.sum(e * e) * (0.5 / D_MODEL)
        dy = e * (1.0 / D_MODEL)
        dgf_ref[...] += _colsum(dy * n)
        dh_ref[...] = _rms_bwd(dy * gfv, n, rf, D_MODEL)

    row = lambda w: pl.BlockSpec((tm, w), lambda i: (i, 0))
    return pl.pallas_call(
        body, name="mlp_fwd", grid=(T // tm,),
        in_specs=[row(ATTN_W), row(HGRN_W), pl.BlockSpec((tm, HGRN_W), lambda i: (i, 3)), row(D_MODEL),
                  _full((1, ATTN_W)), _full((1, HGRN_W)), _once((D_MODEL, D_MODEL)),
                  _full((1, D_MODEL)), ANY, _full((3, D_FF)),
                  _full((1, D_FF)), _once((D_FF, D_MODEL)), _full((1, D_MODEL)), row(D_MODEL)],
        out_specs=[row(D_MODEL), row(D_MODEL), row(D_MODEL), row(D_FF), row(D_FF), row(D_FF), row(D_FF), row(D_MODEL),
                   _full((1, 128)), _full((1, D_MODEL))],
        out_shape=[jax.ShapeDtypeStruct((T, D_MODEL), F32), jax.ShapeDtypeStruct((T, D_MODEL), BF16),
                   jax.ShapeDtypeStruct((T, D_MODEL), BF16)] + [jax.ShapeDtypeStruct((T, D_FF), BF16)] * 4
        + [jax.ShapeDtypeStruct((T, D_MODEL), F32),
                   jax.ShapeDtypeStruct((1, 128), F32), jax.ShapeDtypeStruct((1, D_MODEL), F32)],
        scratch_shapes=[pltpu.VMEM((8, D_FF), F32), pltpu.VMEM((D_MODEL, 2 * D_FF), BF16),
                        pltpu.SemaphoreType.DMA((N_CHIPS,))],
        compiler_params=_cp("arbitrary"),
    )(attn_o, rec_o, hg, x, g_a, g_h, w_out, g2, w_up4, conv_w, conv_b, w_down, gf, tgt)


def _mlp_bwd(dh2, gate, val, conv, act, conv_w, w_down, tm=256):
    T = dh2.shape[0]
    nb = T // tm

    def body(dh_ref, gate_ref, val_ref, conv_ref, act_ref, cw_ref, wd_ref, dgv_ref, dcw_ref, dcb_ref, dwd_ref,
             carry, acc):
        i = pl.program_id(0)

        @pl.when(i == 0)
        def _():
            carry[...] = jnp.zeros_like(carry)
            dcw_ref[...] = jnp.zeros_like(dcw_ref)
            dcb_ref[...] = jnp.zeros_like(dcb_ref)
            acc[...] = jnp.zeros_like(acc)

        dhb = dh_ref[...].astype(BF16)
        for lo, hi in MLP_BWD_CHUNKS:
            cols = slice(lo, hi)
            rowid = lax.broadcasted_iota(jnp.int32, (tm, hi - lo), 0)
            acc[cols, :] += _dot_tn(act_ref[:, cols], dhb)
            g = gate_ref[:, cols].astype(F32)
            v = val_ref[:, cols].astype(F32)
            cv = conv_ref[:, cols].astype(F32)
            dact = _dot_nt(dhb, wd_ref[cols, :])
            gl, gp = _gelu_and_grad(cv)
            dconv = dact * v * gp
            nxt = carry[:, cols]
            n0, n1 = _row(nxt, 0), _row(nxt, 1)
            u1 = jnp.where(rowid == tm - 1, n0, pltpu.roll(dconv, tm - 1, 0))
            u2 = jnp.where(rowid == tm - 1, n1, jnp.where(rowid == tm - 2, n0, pltpu.roll(dconv, tm - 2, 0)))
            carry[:, cols] = dconv[0:8, :]
            dcb_ref[:, cols] += _colsum(dconv)
            dcw_ref[0:1, cols] += _colsum(u2 * g)
            dcw_ref[1:2, cols] += _colsum(u1 * g)
            dcw_ref[2:3, cols] += _colsum(dconv * g)
            dgate = cw_ref[2:3, cols] * dconv + cw_ref[1:2, cols] * u1 + cw_ref[0:1, cols] * u2
            dgv_ref[:, cols] = dgate.astype(BF16)
            dgv_ref[:, D_FF + lo:D_FF + hi] = (dact * gl).astype(BF16)

        @pl.when(i == nb - 1)
        def _():
            for lo, hi in MLP_BWD_CHUNKS:
                dwd_ref[lo:hi, :] = acc[lo:hi, :].astype(BF16)

    rev = lambda w: pl.BlockSpec((tm, w), lambda i: (nb - 1 - i, 0))
    return pl.pallas_call(
        body, name="mlp_bwd", grid=(nb,),
        in_specs=[rev(D_MODEL), rev(D_FF), rev(D_FF), rev(D_FF), rev(D_FF), _full((3, D_FF)), _once((D_FF, D_MODEL))],
        out_specs=[rev(2 * D_FF), _full((3, D_FF)), _full((1, D_FF)), _once((D_FF, D_MODEL))],
        out_shape=[jax.ShapeDtypeStruct((T, 2 * D_FF), BF16), jax.ShapeDtypeStruct((3, D_FF), F32),
                   jax.ShapeDtypeStruct((1, D_FF), F32), jax.ShapeDtypeStruct((D_FF, D_MODEL), BF16)],
        scratch_shapes=[pltpu.VMEM((8, D_FF), F32), pltpu.VMEM((D_FF, D_MODEL), F32)],
        compiler_params=_cp("arbitrary"),
    )(dh2, gate, val, conv, act, conv_w, w_down)


def _up_out_bwd(dgv, w_up4, h1, g2, dh2, w_out, mixed, attn_o, rec_o, hg, g_a, g_h, tm=256):
    T = h1.shape[0]
    nb = T // tm

    def body(dgv_ref, wu_hbm, h_ref, g2_ref, dh2_ref, wo_ref, mx_ref, a_ref, r_ref, gt_ref, ga_ref, gh_ref,
             dh1_ref, dg2_ref, da_ref, dr_ref, dgt_ref, dga_ref, dgh_ref, dwo_ref, wu_ref, sem, acc):
        i = pl.program_id(0)

        @pl.when(i == 0)
        def _():
            dg2_ref[...] = jnp.zeros_like(dg2_ref)
            dga_ref[...] = jnp.zeros_like(dga_ref)
            dgh_ref[...] = jnp.zeros_like(dgh_ref)
            acc[...] = jnp.zeros_like(acc)
            _load_side_by_side(wu_hbm, wu_ref, sem)

        du = _dot_nt(dgv_ref[...], wu_ref[...])
        h = h_ref[...]
        r = _rms(h, D_MODEL)
        n = h * r
        dg2_ref[...] += _colsum(du * n)
        dh1 = dh2_ref[...] + _rms_bwd(du * g2_ref[...], n, r, D_MODEL)
        dh1_ref[...] = dh1
        dh1b = dh1.astype(BF16)
        acc[...] += _dot_tn(mx_ref[...], dh1b)
        dmix = _dot_nt(dh1b, wo_ref[...])
        dan = dmix[:, :ATTN_W]
        a = a_ref[...]
        ra = _rms(a, ATTN_W)
        na = a * ra
        dga_ref[...] += _colsum(dan * na)
        da_ref[...] = _rms_bwd(dan * ga_ref[...], na, ra, ATTN_W)
        dmr = dmix[:, ATTN_W:]
        gate = gt_ref[...]
        ghv = gh_ref[...]
        rr, rn, sg = _rec_heads(r_ref[...], gate, ghv)
        dgt_ref[...] = (dmr * rn * ghv * (sg * (1.0 + gate * (1.0 - sg)))).astype(BF16)
        drecn = dmr * (gate * sg)
        dgh_ref[...] += _colsum(drecn * rn)
        drn = drecn * ghv
        prod = drn * rn
        mean = jnp.concatenate(
            [jnp.broadcast_to(jnp.sum(prod[:, h_ * HGRN_DIM:(h_ + 1) * HGRN_DIM], axis=-1, keepdims=True),
                              (tm, HGRN_DIM)) for h_ in range(HGRN_HEADS)], axis=1) * (1.0 / HGRN_DIM)
        dr_ref[...] = rr * (drn - rn * mean)

        @pl.when(i == nb - 1)
        def _():
            dwo_ref[...] = acc[...].astype(BF16)

    row = lambda w: pl.BlockSpec((tm, w), lambda i: (i, 0))
    return pl.pallas_call(
        body, name="up_out_bwd", grid=(nb,),
        in_specs=[row(2 * D_FF), ANY, row(D_MODEL), _full((1, D_MODEL)),
                  row(D_MODEL), _once((D_MODEL, D_MODEL)), row(D_MODEL), row(ATTN_W), row(HGRN_W),
                  pl.BlockSpec((tm, HGRN_W), lambda i: (i, 3)), _full((1, ATTN_W)), _full((1, HGRN_W))],
        out_specs=[row(D_MODEL), _full((1, D_MODEL)), row(ATTN_W), row(HGRN_W), row(HGRN_W),
                   _full((1, ATTN_W)), _full((1, HGRN_W)), _once((D_MODEL, D_MODEL))],
        out_shape=[jax.ShapeDtypeStruct((T, D_MODEL), F32), jax.ShapeDtypeStruct((1, D_MODEL), F32),
                   jax.ShapeDtypeStruct((T, ATTN_W), F32), jax.ShapeDtypeStruct((T, HGRN_W), F32),
                   jax.ShapeDtypeStruct((T, HGRN_W), BF16), jax.ShapeDtypeStruct((1, ATTN_W), F32),
                   jax.ShapeDtypeStruct((1, HGRN_W), F32), jax.ShapeDtypeStruct((D_MODEL, D_MODEL), BF16)],
        scratch_shapes=[pltpu.VMEM((D_MODEL, 2 * D_FF), BF16), pltpu.SemaphoreType.DMA((N_CHIPS,)),
                        pltpu.VMEM((D_MODEL, D_MODEL), F32)],
        compiler_params=_cp("arbitrary"),
    )(dgv, w_up4, h1, g2, dh2, w_out, mixed, attn_o, rec_o, hg, g_a, g_h)


def _in_bwd(dqkv, dhg, w_in4, u1, x, g1, dh1, tm=256):
    T = x.shape[0]
    nb = T // tm

    def body(*refs):
        parts = refs[:7]
        w_hbm, u_ref, x_ref, g_ref, dh1_ref, dw_ref, dx_ref, dg_ref, w_full, sem, acc = refs[7:]
        i = pl.program_id(0)

        @pl.when(i == 0)
        def _():
            dg_ref[...] = jnp.zeros_like(dg_ref)
            acc[...] = jnp.zeros_like(acc)
            _load_side_by_side(w_hbm, w_full, sem)

        dp = jnp.concatenate([p[...] for p in parts], axis=1)
        acc[...] += _dot_tn(u_ref[...], dp)
        du = _dot_nt(dp, w_full[...])
        xv = x_ref[...]
        r = _rms(xv, D_MODEL)
        n = xv * r
        dg_ref[...] += _colsum(du * n)
        dx_ref[...] = dh1_ref[...] + _rms_bwd(du * g_ref[...], n, r, D_MODEL)

        @pl.when(i == nb - 1)
        def _():
            for k in range(N_CHIPS):
                dw_ref[k] = acc[:, k * IN_SHARD:(k + 1) * IN_SHARD].astype(BF16)

    row = lambda w: pl.BlockSpec((tm, w), lambda i: (i, 0))
    return pl.pallas_call(
        body, name="in_bwd", grid=(nb,),
        in_specs=[row(ATTN_W)] * 7 + [ANY, row(D_MODEL), row(D_MODEL), _full((1, D_MODEL)), row(D_MODEL)],
        out_specs=[_once((N_CHIPS, D_MODEL, IN_SHARD)), row(D_MODEL), _full((1, D_MODEL))],
        out_shape=[jax.ShapeDtypeStruct((N_CHIPS, D_MODEL, IN_SHARD), BF16), jax.ShapeDtypeStruct((T, D_MODEL), F32),
                   jax.ShapeDtypeStruct((1, D_MODEL), F32)],
        scratch_shapes=[pltpu.VMEM((D_MODEL, IN_TOTAL), BF16), pltpu.SemaphoreType.DMA((N_CHIPS,)),
                        pltpu.VMEM((D_MODEL, IN_TOTAL), F32)],
        compiler_params=_cp("arbitrary"),
    )(*dqkv, *dhg, w_in4, u1, x, g1, dh1)


def _dw(a, b, kb, nb_, name, tk=1024, side=1):
    T, K = a.shape
    N = b.shape[1]
    nk, nn, nt = K // kb, N // (nb_ * side), T // tk

    def body(a_ref, b_ref, o_ref, acc):
        t = pl.program_id(2)

        @pl.when(t == 0)
        def _():
            acc[...] = jnp.zeros_like(acc)

        acc[...] += _dot_tn(a_ref[...], b_ref[...].astype(BF16))

        @pl.when(t == nt - 1)
        def _():
            for s in range(side):
                o_ref[s] = acc[:, s * nb_:(s + 1) * nb_].astype(BF16)

    return pl.pallas_call(
        body, name=name, grid=(nk, nn, nt),
        in_specs=[pl.BlockSpec((tk, kb), lambda i, j, t: (t, i)),
                  pl.BlockSpec((tk, nb_ * side), lambda i, j, t: (t, j))],
        out_specs=pl.BlockSpec((side, kb, nb_), lambda i, j, t: (i * nn + j, 0, 0)),
        out_shape=jax.ShapeDtypeStruct((nk * nn * side, kb, nb_), BF16),
        scratch_shapes=[pltpu.VMEM((kb, nb_ * side), F32)],
        compiler_params=_cp("arbitrary", "arbitrary", "arbitrary"),
    )(a, b)


def _step_channel(a, x, tgt, g_a, g_h, w_out, g2, w_up4, conv_w, conv_b, w_down, gf):
    h1, mixed, u2, gate, val, conv, act, dh2, loss, dgf = _mlp_fwd(
        a["attn_o"], a["rec_o"], a["hg"], x, g_a, g_h, w_out, g2, w_up4, conv_w, conv_b, w_down, gf, tgt)
    dgv, dcw, dcb, dw_down = _mlp_bwd(dh2, gate, val, conv, act, conv_w, w_down)
    dw_down = dw_down.reshape(N_CHIPS, D_FF // N_CHIPS, D_MODEL)
    dh1, dg2, da, dr, dgt, dga, dgh, dw_out = _up_out_bwd(dgv, w_up4, h1, g2, dh2, w_out, mixed, a["attn_o"],
                                                          a["rec_o"], a["hg"], g_a, g_h)
    dw_up = _dw(u2, dgv, D_MODEL, UP_SHARD, "dw_up", side=2)
    dw_out = dw_out.reshape(N_CHIPS, D_MODEL // N_CHIPS, D_MODEL)
    return dict(loss=loss, dgf=dgf, dcw=dcw, dcb=dcb, dg2=dg2, dga=dga, dgh=dgh, dh1=dh1, da=da, dr=dr, dgt=dgt,
                dw_down=dw_down, dw_up=dw_up, dw_out=dw_out)


def _step_mixers_bwd(a, b, x, g1, w_in4, lb, dqkv):
    dhq, dhf, dhi, dlb = _hgrn_bwd(a["hg"], lb, a["states"], b["dr"])
    dw_in, dx, dg1 = _in_bwd(dqkv, [dhq, dhf, dhi, b["dgt"]], w_in4, a["u1"], x, g1, b["dh1"])
    return dict(dx=dx, dg1=dg1, dlb=dlb, dw_in=dw_in)


BIG = ("w_in", "w_out", "w_up", "w_down")
ANY = pl.BlockSpec(memory_space=pl.ANY)


def _place():
    x, y, c = lax.axis_index("x"), lax.axis_index("y"), lax.axis_index("c")
    chips = [(1 - x, y), (x, 1 - y), (1 - x, 1 - y)]
    return x, y, c, chips


def _remote(src, dst, send_sems, recv_sems, k, to):
    return pltpu.make_async_remote_copy(src_ref=src, dst_ref=dst, send_sem=send_sems.at[k], recv_sem=recv_sems.at[k],
                                        device_id=to, device_id_type=MESH)


def _gather_weights(shards, conv_w):
    n = len(shards)
    halves = [s.shape[0] // 2 for s in shards]

    def body(*refs):
        ins, cw, outs, ocw = refs[:n], refs[n], refs[n + 1:2 * n + 1], refs[2 * n + 1]
        send_sems, recv_sems = refs[2 * n + 2:]
        x, y, c, chips = _place()
        me, sibling = 2 * x + y, (x, y, 1 - c)

        def part(w, chip, half):
            return outs[w].at[chip, pl.ds(half * halves[w], halves[w]), :]

        sent = []
        for j, chip in enumerate(chips):
            for w in range(n):
                sent.append(_remote(ins[w].at[pl.ds(c * halves[w], halves[w]), :], part(w, me, c),
                                    send_sems, recv_sems, w * 3 + j, (*chip, c)))
            sent.append(_remote(cw, ocw.at[me], send_sems, recv_sems, 6 * n + j, (*chip, c)))
        for cp in sent:
            cp.start()
        for j, chip in enumerate(chips):
            kj = 2 * chip[0] + chip[1]
            for w in range(n):
                _remote(part(w, kj, c), part(w, kj, c), send_sems, recv_sems, w * 3 + j, (*chip, c)).wait_recv()
                fwd = _remote(part(w, kj, c), part(w, kj, c), send_sems, recv_sems, 3 * n + w * 3 + j, sibling)
                fwd.start()
                sent.append(fwd)
        for j, chip in enumerate(chips):
            kj = 2 * chip[0] + chip[1]
            for w in range(n):
                _remote(part(w, kj, 1 - c), part(w, kj, 1 - c), send_sems, recv_sems, 3 * n + w * 3 + j,
                        sibling).wait_recv()
            _remote(cw, ocw.at[kj], send_sems, recv_sems, 6 * n + j, (*chip, c)).wait_recv()
        for cp in sent:
            cp.wait_send()

    n_sem = 6 * n + 3
    outs = pl.pallas_call(
        body, name="gather_weights",
        in_specs=[ANY] * (n + 1), out_specs=[ANY] * (n + 1),
        out_shape=[jax.ShapeDtypeStruct((N_CHIPS,) + s.shape, s.dtype) for s in shards]
        + [jax.ShapeDtypeStruct((N_CHIPS,) + conv_w.shape, conv_w.dtype)],
        scratch_shapes=[pltpu.SemaphoreType.DMA((n_sem,)), pltpu.SemaphoreType.DMA((n_sem,))],
    )(*shards, conv_w)
    chip = 2 * lax.axis_index("x") + lax.axis_index("y")
    return [lax.dynamic_update_slice(o, s[None], (chip,) + (0,) * s.ndim) for o, s in zip(outs, [*shards, conv_w])]


def _allreduce_small(buf):
    rows = buf.shape[0]

    def body(in_ref, out_ref, slots, send_sems, recv_sems):
        x, y, c, _ = _place()
        me = 4 * x + 2 * y + c
        slots[me] = in_ref[...]
        sent = []
        for p in range(1, 8):
            to = (x ^ (p >> 2), y ^ ((p >> 1) & 1), c ^ (p & 1))
            sent.append(_remote(in_ref, slots.at[me], send_sems, recv_sems, p, to))
        for cp in sent:
            cp.start()
        for p in range(1, 8):
            frm = 4 * (x ^ (p >> 2)) + 2 * (y ^ ((p >> 1) & 1)) + (c ^ (p & 1))
            _remote(in_ref, slots.at[frm], send_sems, recv_sems, p, (x, y, c)).wait_recv()
        for cp in sent:
            cp.wait_send()
        acc = slots[0]
        for d in range(1, 8):
            acc = acc + slots[d]
        out_ref[...] = acc

    vm = pl.BlockSpec(memory_space=pltpu.VMEM)
    return pl.pallas_call(
        body, name="allreduce_small", in_specs=[vm], out_specs=vm,
        out_shape=jax.ShapeDtypeStruct(buf.shape, F32),
        scratch_shapes=[pltpu.VMEM((8, rows, 128), F32), pltpu.SemaphoreType.DMA((8,)), pltpu.SemaphoreType.DMA((8,))],
    )(buf)


def _sibling_peer():
    x, y, c, _ = _place()
    return [(x, y, 1 - c)]


def _chip_peers():
    x, y, c, chips = _place()
    return [(*chip, c) for chip in chips]


def _handshake(peers):
    barrier = pltpu.get_barrier_semaphore()
    for peer in peers:
        pl.semaphore_signal(barrier, inc=1, device_id=peer, device_id_type=MESH)
    pl.semaphore_wait(barrier, len(peers))


def _pair_exchange(gs, name, barrier_id):
    n = len(gs)
    halves = [g.shape[1] // 2 for g in gs]

    def body(*refs):
        g, got = refs[:n], refs[n:2 * n]
        send_sems, recv_sems = refs[2 * n:]
        _handshake(_sibling_peer())
        x, y, c, _ = _place()
        cps = [_remote(g[w].at[:, pl.ds((1 - c) * halves[w], halves[w]), :], got[w], send_sems, recv_sems, w,
                       (x, y, 1 - c)) for w in range(n)]
        for cp in cps:
            cp.start()
        for cp in cps:
            cp.wait()

    return pl.pallas_call(
        body, name=name, in_specs=[ANY] * n, out_specs=[ANY] * n,
        out_shape=[jax.ShapeDtypeStruct((N_CHIPS, h, g.shape[2]), g.dtype) for g, h in zip(gs, halves)],
        scratch_shapes=[pltpu.SemaphoreType.DMA((n,)), pltpu.SemaphoreType.DMA((n,))],
        compiler_params=pltpu.CompilerParams(collective_id=barrier_id),
    )(*gs)


def _core_id():
    return lax.axis_index("c").reshape(1).astype(jnp.int32)


def _pair_sum(gs, gots, name):
    n = len(gs)

    def body(c_ref, *refs):
        for g_ref, b_ref, o_ref in zip(refs[:n], refs[n:2 * n], refs[2 * n:]):
            o_ref[...] = (g_ref[...].astype(F32) + b_ref[...].astype(F32)).astype(BF16)

    mine = lambda got: pl.BlockSpec((1,) + got.shape[1:], lambda k, c_ref: (k, c_ref[0], 0))
    blk = lambda got: pl.BlockSpec((1,) + got.shape[1:], lambda k, c_ref: (k, 0, 0))
    return pl.pallas_call(
        body, name=name,
        grid_spec=pltpu.PrefetchScalarGridSpec(
            num_scalar_prefetch=1, grid=(N_CHIPS,),
            in_specs=[mine(got) for got in gots] + [blk(got) for got in gots], out_specs=[blk(got) for got in gots]),
        out_shape=[jax.ShapeDtypeStruct(got.shape, BF16) for got in gots],
        compiler_params=_cp("arbitrary"))(_core_id(), *gs, *gots)


def _sum_partials(gs, gots, landeds, name):
    n = len(gs)

    def body(ids, *refs):
        for g_ref, b_ref, l_ref, o_ref in zip(refs[:n], refs[n:2 * n], refs[2 * n:3 * n], refs[3 * n:]):
            acc = g_ref[0].astype(F32) + b_ref[0].astype(F32)
            for j in range(3):
                acc = acc + l_ref[j].astype(F32)
            o_ref[...] = acc

    ids = jnp.stack([2 * lax.axis_index("x") + lax.axis_index("y"), lax.axis_index("c")]).astype(jnp.int32)
    shp = [got.shape[1:] for got in gots]
    return pl.pallas_call(
        body, name=name,
        grid_spec=pltpu.PrefetchScalarGridSpec(
            num_scalar_prefetch=1, grid=(1,),
            in_specs=[pl.BlockSpec((1,) + s, lambda i, ids: (ids[0], ids[1], 0)) for s in shp]
            + [pl.BlockSpec((1,) + s, lambda i, ids: (ids[0], 0, 0)) for s in shp]
            + [pl.BlockSpec((3,) + s, lambda i, ids: (0, 0, 0)) for s in shp],
            out_specs=[pl.BlockSpec(s, lambda i, ids: (ids[1], 0)) for s in shp]),
        out_shape=[jax.ShapeDtypeStruct((2 * s[0], s[1]), F32) for s in shp],
        compiler_params=_cp("arbitrary"))(ids, *gs, *gots, *landeds)


def _pair_share(reds, name, barrier_id):
    n = len(reds)

    def body(*refs):
        out = refs[n:2 * n]
        send_sems, recv_sems = refs[2 * n:]
        _handshake(_sibling_peer())
        x, y, c, _ = _place()
        def half(w, which):
            h = out[w].shape[0] // 2
            return out[w].at[pl.ds(which * h, h), :]

        cps = [_remote(half(w, c), half(w, c), send_sems, recv_sems, w, (x, y, 1 - c)) for w in range(n)]
        for cp in cps:
            cp.start()
        for w in range(n):
            _remote(half(w, 1 - c), half(w, 1 - c), send_sems, recv_sems, w, (x, y, 1 - c)).wait_recv()
        for cp in cps:
            cp.wait_send()

    return pl.pallas_call(
        body, name=name, in_specs=[ANY] * n, out_specs=[ANY] * n,
        out_shape=[jax.ShapeDtypeStruct(r.shape, F32) for r in reds],
        input_output_aliases={w: w for w in range(n)},
        scratch_shapes=[pltpu.SemaphoreType.DMA((n,)), pltpu.SemaphoreType.DMA((n,))],
        compiler_params=pltpu.CompilerParams(collective_id=barrier_id),
    )(*reds)


HBM = pl.BlockSpec(memory_space=pltpu.HBM)
SEM = pl.BlockSpec(memory_space=pltpu.SEMAPHORE)
DATAFLOW = pltpu.SideEffectType.DATAFLOW_SIDE_EFFECTING


def _copies_start(name, srcs, lands, plan, n_copies, after, peers, barrier_id):
    ns, nb, na = len(srcs), len(srcs) + len(lands), len(after)

    def body(*refs):
        src_refs, land_refs = refs[:ns], refs[ns:nb]
        send_sems, recv_sems = refs[nb + na:nb + na + 2]
        token = refs[-1]
        _handshake(peers())
        for k, (src, there, _, to) in enumerate(plan(src_refs, land_refs)):
            _remote(src, there, send_sems, recv_sems, k, to).start()
        token[...] = jnp.zeros_like(token)

    hbm = lambda a: pltpu.HBM(a.shape, a.dtype)
    outs = pl.pallas_call(
        body, name=name,
        out_shape=(pltpu.SemaphoreType.DMA((n_copies,)), pltpu.SemaphoreType.DMA((n_copies,)),
                   *[hbm(a) for a in srcs], *[hbm(a) for a in lands], jax.ShapeDtypeStruct((8, 128), F32)),
        in_specs=[HBM] * nb + [ANY] * na,
        out_specs=(SEM, SEM, *[HBM] * nb, pl.BlockSpec(memory_space=pltpu.VMEM)),
        input_output_aliases={i: 2 + i for i in range(nb)},
        compiler_params=pltpu.CompilerParams(has_side_effects=DATAFLOW, collective_id=barrier_id),
    )(*[pltpu.with_memory_space_constraint(a, pltpu.HBM) for a in (*srcs, *lands)], *after)
    return outs[0], outs[1], outs[2:2 + ns], outs[2 + ns:2 + nb], outs[-1]


def _copies_wait(name, send_sems, recv_sems, srcs, lands, plan, after):
    ns, nb, na = len(srcs), len(srcs) + len(lands), len(after)

    def body(*refs):
        src_refs, land_refs = refs[:ns], refs[ns:nb]
        send_sems, recv_sems = refs[nb:nb + 2]
        for k, (src, _, here, to) in enumerate(plan(src_refs, land_refs)):
            cp = _remote(src, here, send_sems, recv_sems, k, to)
            cp.wait_send()
            cp.wait_recv()

    hbm = lambda a: pltpu.HBM(a.shape, a.dtype)
    outs = pl.pallas_call(
        body, name=name,
        out_shape=(*[hbm(a) for a in srcs], *[hbm(a) for a in lands]),
        in_specs=[HBM] * nb + [SEM, SEM] + [ANY] * na,
        out_specs=tuple([HBM] * nb),
        input_output_aliases={i: i for i in range(nb)},
        compiler_params=pltpu.CompilerParams(has_side_effects=DATAFLOW),
    )(*srcs, *lands, send_sems, recv_sems, *after)
    return outs[:ns], outs[ns:]


def _gather_plan(halves):
    def plan(shards, lands):
        x, y, c, chips = _place()
        me = 2 * x + y
        copies = []
        for w, h in enumerate(halves):
            rows = pl.ds(c * h, h)
            for chip in chips:
                copies.append((shards[w].at[rows, :], lands[w].at[me, rows, :],
                               lands[w].at[2 * chip[0] + chip[1], rows, :], (*chip, c)))
        return copies
    return plan


def _reduce_plan(n):
    def plan(ps, lands):
        x, y, c, chips = _place()
        return [(ps[w].at[2 * chip[0] + chip[1]], lands[w].at[j], lands[w].at[j], (*chip, c))
                for w in range(n) for j, chip in enumerate(chips)]
    return plan


def _forward_plan(halves):
    def plan(_, lands):
        x, y, c, chips = _place()

        def part(w, chip, half):
            return lands[w].at[2 * chip[0] + chip[1], pl.ds(half * halves[w], halves[w]), :]

        return [(part(w, chip, c), part(w, chip, c), part(w, chip, 1 - c), (x, y, 1 - c))
                for w in range(len(halves)) for chip in chips]
    return plan


def _pair_plan(halves):
    def plan(gs, gots):
        x, y, c, _ = _place()
        return [(gs[w].at[:, pl.ds((1 - c) * h, h), :], gots[w], gots[w], (x, y, 1 - c)) for w, h in enumerate(halves)]
    return plan


def _place_own(gathered, shards):
    chip = 2 * lax.axis_index("x") + lax.axis_index("y")
    return [lax.dynamic_update_slice(o, s[None], (chip, 0, 0)) for o, s in zip(gathered, shards)]


ADAMW_STEPS = 4


def _adamw(ws, gs, ms, vs, name):
    n = len(ws)

    def body(*refs):
        ins, outs = refs[:4 * n], refs[4 * n:]
        for k in range(n):
            gv = ins[n + k][...]
            outs[4 * k][...] = gv
            outs[4 * k + 1][...], outs[4 * k + 2][...], outs[4 * k + 3][...] = _adamw_math(
                ins[k][...], gv, ins[2 * n + k][...], ins[3 * n + k][...])

    blk = lambda a: pl.BlockSpec((a.shape[0] // ADAMW_STEPS, a.shape[1]), lambda i: (i, 0))
    outs = pl.pallas_call(
        body, name=name, grid=(ADAMW_STEPS,), in_specs=[blk(a) for a in ws] * 4,
        out_specs=[blk(a) for a in ws for _ in range(4)],
        out_shape=[jax.ShapeDtypeStruct(a.shape, F32) for a in ws for _ in range(4)],
        compiler_params=_cp("arbitrary"))(*ws, *gs, *ms, *vs)
    return [outs[4 * k:4 * k + 4] for k in range(n)]


SMALL = (("norm1_g", 1, 1024), ("attn_norm_g", 1, 512), ("hgrn_norm_g", 1, 512), ("hgrn_lb_logits", 2, 512),
         ("norm2_g", 1, 1024), ("conv_b", 1, D_FF), ("final_norm_g", 1, 1024), ("conv_w", 3, D_FF))
LOSS_ROW = sum(r * c for _, r, c in SMALL) // 128
SMALL_ROWS = 136


def _rows_to_lanes(ref, row, width):
    return jnp.concatenate([ref[row + j:row + j + 1, :] for j in range(width // 128)], axis=1)


def _pack_small(grads, dlb, lb, loss):
    def body(*refs):
        parts, dlb_ref, lb_ref, loss_ref, out = refs[:len(SMALL) - 1], refs[-4], refs[-3], refs[-2], refs[-1]
        out[...] = jnp.zeros_like(out)
        lbv = lb_ref[...]
        dl = dlb_ref[...] * lbv * (1.0 - lbv)
        row = 0
        parts = list(parts)
        for name, rows, width in SMALL:
            for r in range(rows):
                if name == "hgrn_lb_logits":
                    src = dl if r == 0 else -dl
                    for j in range(width // 128):
                        out[row + j:row + j + 1, :] = src[:, 128 * j:128 * (j + 1)]
                else:
                    for j in range(width // 128):
                        out[row + j:row + j + 1, :] = parts[0][r:r + 1, 128 * j:128 * (j + 1)]
                row += width // 128
            if name != "hgrn_lb_logits":
                parts.pop(0)
        out[LOSS_ROW:LOSS_ROW + 1, :] = loss_ref[...]

    vm = pl.BlockSpec(memory_space=pltpu.VMEM)
    return pl.pallas_call(body, name="pack_small", in_specs=[vm] * (len(grads) + 3), out_specs=vm,
                          out_shape=jax.ShapeDtypeStruct((SMALL_ROWS, 128), F32))(*grads, dlb, lb, loss)


def _adamw_math(w, g, m, v):
    nm = ADAM_B1 * m + (1.0 - ADAM_B1) * g
    nv = ADAM_B2 * v + (1.0 - ADAM_B2) * (g * g)
    m_hat = nm / (1.0 - ADAM_B1 ** ADAM_STEP)
    v_hat = nv / (1.0 - ADAM_B2 ** ADAM_STEP)
    return -ADAM_LR * (m_hat / (jnp.sqrt(v_hat) + ADAM_EPS) + ADAM_WD * w), nm, nv


def _small_update(summed, g_conv_w, ws, ms, vs):
    n = len(SMALL)

    def body(*refs):
        s_ref, gcw_ref = refs[:2]
        w_refs, m_refs, v_refs = refs[2:2 + n], refs[2 + n:2 + 2 * n], refs[2 + 2 * n:2 + 3 * n]
        outs = refs[2 + 3 * n:]
        row = 0
        for k, (name, rows, width) in enumerate(SMALL):
            if name == "conv_w":
                g = gcw_ref[...]
            else:
                g = jnp.concatenate([_rows_to_lanes(s_ref, row + r * (width // 128), width) for r in range(rows)], axis=0)
            row += rows * (width // 128)
            d, nm, nv = _adamw_math(w_refs[k][...], g, m_refs[k][...], v_refs[k][...])
            for o, val in zip(outs[4 * k:4 * k + 4], (g, d, nm, nv)):
                o[...] = val

    vm = pl.BlockSpec(memory_space=pltpu.VMEM)
    outs = pl.pallas_call(
        body, name="small_update", in_specs=[vm] * (2 + 3 * n), out_specs=[vm] * (4 * n),
        out_shape=[jax.ShapeDtypeStruct(a.shape, F32) for a in ws for _ in range(4)],
    )(summed, g_conv_w, *ws, *ms, *vs)
    return [outs[4 * k:4 * k + 4] for k in range(n)]


def kernel(x, norm1_g, w_in, attn_norm_g, hgrn_norm_g, hgrn_lb_logits, w_out, norm2_g, w_up, conv_w, conv_b, w_down, final_norm_g, loss_target, m_norm1_g, m_w_in, m_attn_norm_g, m_hgrn_norm_g, m_hgrn_lb_logits, m_w_out, m_norm2_g, m_w_up, m_conv_w, m_conv_b, m_w_down, m_final_norm_g, v_norm1_g, v_w_in, v_attn_norm_g, v_hgrn_norm_g, v_hgrn_lb_logits, v_w_out, v_norm2_g, v_w_up, v_conv_w, v_conv_b, v_w_down, v_final_norm_g):
    w = dict(norm1_g=norm1_g, w_in=w_in, attn_norm_g=attn_norm_g, hgrn_norm_g=hgrn_norm_g,
             hgrn_lb_logits=hgrn_lb_logits, w_out=w_out, norm2_g=norm2_g, w_up=w_up, conv_w=conv_w, conv_b=conv_b,
             w_down=w_down, final_norm_g=final_norm_g)
    m = dict(norm1_g=m_norm1_g, w_in=m_w_in, attn_norm_g=m_attn_norm_g, hgrn_norm_g=m_hgrn_norm_g,
             hgrn_lb_logits=m_hgrn_lb_logits, w_out=m_w_out, norm2_g=m_norm2_g, w_up=m_w_up, conv_w=m_conv_w,
             conv_b=m_conv_b, w_down=m_w_down, final_norm_g=m_final_norm_g)
    v = dict(norm1_g=v_norm1_g, w_in=v_w_in, attn_norm_g=v_attn_norm_g, hgrn_norm_g=v_hgrn_norm_g,
             hgrn_lb_logits=v_hgrn_lb_logits, w_out=v_w_out, norm2_g=v_norm2_g, w_up=v_w_up, conv_w=v_conv_w,
             conv_b=v_conv_b, w_down=v_w_down, final_norm_g=v_final_norm_g)
    names = list(w)
    chip = 2 * lax.axis_index("x") + lax.axis_index("y")

    shards = {k: w[k][0].astype(BF16) for k in BIG}
    w_in4, conv_w4 = _gather_weights([shards["w_in"]], conv_w[0])
    conv_w_full = jnp.transpose(conv_w4, (1, 0, 2)).reshape(3, D_FF)
    lb = jax.nn.softmax(hgrn_lb_logits, axis=0)[0:1]
    late = [shards[k] for k in BIG[1:]]
    gather_plan = _gather_plan([s.shape[0] // 2 for s in late])
    started = _copies_start("gather_start", late, [lax.empty((N_CHIPS,) + s.shape, BF16) for s in late], gather_plan,
                            3 * len(late), after=(w_in4,), peers=_chip_peers, barrier_id=0)
    u1, qkv, hg = _in_proj(x[0], norm1_g + started[4][0:1, 0:1], w_in4)
    attn_o, lse = _attn_fwd(qkv)
    late, landed_w = _copies_wait("gather_wait", *started[:4], gather_plan, after=(attn_o,))
    forward_plan = _forward_plan([s.shape[0] // 2 for s in late])
    started = _copies_start("forward_start", [], landed_w, forward_plan, 3 * len(late), after=(),
                            peers=_sibling_peer, barrier_id=1)
    rec_o, states = _hgrn_fwd(hg, lb + started[4][0:1, 0:1])
    a = dict(u1=u1, qkv=qkv, hg=hg, attn_o=attn_o, lse=lse, rec_o=rec_o, states=states)
    w_out4, w_up4, w_down4 = _place_own(
        _copies_wait("forward_wait", *started[:4], forward_plan, after=(rec_o,))[1], late)

    b = _step_channel(a, x[0], loss_target[0], attn_norm_g, hgrn_norm_g, w_out4.reshape(D_MODEL, D_MODEL), norm2_g,
                      w_up4, conv_w_full, conv_b, w_down4.reshape(D_FF, D_MODEL), final_norm_g.reshape(1, D_MODEL))

    early = [b["dw_out"], b["dw_up"], b["dw_down"]]
    pair_plan = _pair_plan([gk.shape[1] // 2 for gk in early])
    started = _copies_start("pair_start", early,
                            [lax.empty((N_CHIPS, gk.shape[1] // 2, gk.shape[2]), BF16) for gk in early], pair_plan,
                            len(early), after=(), peers=_sibling_peer, barrier_id=2)
    dqkv = _attn_bwd(qkv, attn_o, lse, b["da"], started[4])
    early, gots = _copies_wait("pair_wait", *started[:4], pair_plan, after=(dqkv[0],))
    ps = _pair_sum(early, gots, "pair_sum")
    reduce_plan = _reduce_plan(len(ps))
    started = _copies_start("reduce_start", ps, [lax.empty((3,) + p.shape[1:], BF16) for p in ps], reduce_plan,
                            3 * len(ps), after=(), peers=_chip_peers, barrier_id=3)
    c = _step_mixers_bwd(a, b, x[0], norm1_g, w_in4, lb + started[4][0:1, 0:1], dqkv)
    gots_in = _pair_exchange([c["dw_in"]], "pair_exchange_w_in", barrier_id=4)
    ps_in = _pair_sum([c["dw_in"]], gots_in, "pair_sum_w_in")[0]
    plan_in = _reduce_plan(1)
    started_in = _copies_start("reduce_start_w_in", [ps_in], [lax.empty((3,) + ps_in.shape[1:], BF16)], plan_in, 3,
                               after=(), peers=_chip_peers, barrier_id=5)
    landed = _copies_wait("reduce_wait", *started[:4], reduce_plan, after=(started_in[4],))[1]
    reds = _sum_partials(early, gots, landed, "sum_partials")
    g = dict(zip(BIG[1:], _pair_share(reds, "pair_share", barrier_id=6)))
    delta, new_m, new_v = {}, {}, {}
    shard = lambda p: [p[k][0] for k in BIG[1:]]
    for k, parts in zip(BIG[1:], _adamw(shard(w), [g[k] for k in BIG[1:]], shard(m), shard(v), "adamw")):
        g[k], delta[k], new_m[k], new_v[k] = parts

    loss, dx = b["loss"], c["dx"]
    small = dict(g1=c["dg1"], g_a=b["dga"], g_h=b["dgh"], lb=c["dlb"], g2=b["dg2"], conv_w=b["dcw"], conv_b=b["dcb"],
                 gf=b["dgf"])
    summed = _allreduce_small(_pack_small(
        [small["g1"], small["g_a"], small["g_h"], small["g2"], small["conv_b"], small["gf"], small["conv_w"]],
        small["lb"], lb, loss))
    loss_total = summed[LOSS_ROW, 0]
    g_conv_w = lax.dynamic_slice(summed[LOSS_ROW - 3 * D_FF // 128:LOSS_ROW].reshape(3, D_FF),
                                 (0, chip * (D_FF // N_CHIPS)), (3, D_FF // N_CHIPS))
    two_d = lambda p, k: p[k].reshape(-1, p[k].shape[-1])
    updated = _small_update(summed, g_conv_w, *[[two_d(p, k) for k, _, _ in SMALL] for p in (w, m, v)])
    for (k, _, _), parts in zip(SMALL, updated):
        g[k], delta[k], new_m[k], new_v[k] = (a.reshape(w[k].shape) for a in parts)

    landed_in = _copies_wait("reduce_wait_w_in", *started_in[:4], plan_in, after=(updated[0][1], delta["w_up"]))[1]
    red_in = _sum_partials([c["dw_in"]], gots_in, landed_in, "sum_partials_w_in")
    g["w_in"] = _pair_share(red_in, "pair_share_w_in", barrier_id=7)[0]
    g["w_in"], delta["w_in"], new_m["w_in"], new_v["w_in"] = _adamw([w_in[0]], [g["w_in"]], [m_w_in[0]], [v_w_in[0]],
                                                                    "adamw_w_in")[0]
    for k in BIG:
        g[k], delta[k], new_m[k], new_v[k] = g[k][None], delta[k][None], new_m[k][None], new_v[k][None]

    return (loss_total, dx[None], *[g[k] for k in names], *[delta[k] for k in names],
            *[new_m[k] for k in names], *[new_v[k] for k in names])
```

```python
import math

import jax
import jax.numpy as jnp
from jax import lax
from jax.experimental import pallas as pl
from jax.experimental.pallas import tpu as pltpu

F32 = jnp.float32
BF16 = jnp.bfloat16

D_MODEL = 1024
ATTN_W = 512
HGRN_W = 512
HEAD_PAIR = 128
ATTN_BLK = 128
DILATIONS = (1, 4, 16)
ATTN_CHAINS = 4
ATTN_CHAINS_FWD = 8
HGRN_HEADS = 4
HGRN_DIM = 128
HGRN_CHUNK = 64
SUPER = 256
HGRN_SIDE = 8
D_FF = 2816
FF_CHUNKS = ((0, 1536), (1536, D_FF))
MLP_BWD_CHUNKS = ((0, 768), (768, 1408), (1408, 2176), (2176, D_FF))
N_CHIPS = 4
IN_TOTAL = 3584
IN_SHARD = IN_TOTAL // N_CHIPS
UP_SHARD = 2 * D_FF // N_CHIPS
QKV_W = 3 * ATTN_W
HG_W = 4 * HGRN_W
EPS = 1e-6
NEG = -1e30
V7X_VMEM_BYTES = 64 * 1024 * 1024
VMEM_LIMIT = V7X_VMEM_BYTES - 8 * 1024 * 1024

ADAM_LR = 0.001
ADAM_B1 = 0.9
ADAM_B2 = 0.999
ADAM_EPS = 1e-08
ADAM_WD = 0.01
ADAM_STEP = 10

MESH = pl.DeviceIdType.MESH


def _cp(*sem):
    return pltpu.CompilerParams(dimension_semantics=sem or None, vmem_limit_bytes=VMEM_LIMIT)


def _dot(a, b):
    return jnp.dot(a, b, preferred_element_type=F32)


def _dot_nt(a, b):
    return lax.dot_general(a, b, (((1,), (1,)), ((), ())), preferred_element_type=F32)


def _dot_tn(a, b):
    return lax.dot_general(a, b, (((0,), (0,)), ((), ())), preferred_element_type=F32)


def _sigmoid(x):
    return 1.0 / (1.0 + jnp.exp(-x))


def _rms(x, width):
    return lax.rsqrt(jnp.sum(x * x, axis=-1, keepdims=True) * (1.0 / width) + EPS)


def _rms_bwd(dn, n, r, width):
    return r * (dn - n * (jnp.sum(dn * n, axis=-1, keepdims=True) * (1.0 / width)))


def _colsum(x):
    return jnp.sum(x, axis=0, keepdims=True)


def _row(v, k):
    rid = lax.broadcasted_iota(jnp.int32, v.shape, 0)
    return jnp.sum(jnp.where(rid == k, v, 0.0), axis=0, keepdims=True)


def _full(shape):
    return pl.BlockSpec(shape, lambda *_: (0,) * len(shape))


def _once(shape):
    return pl.BlockSpec(shape, lambda *_: (0,) * len(shape), pipeline_mode=pl.Buffered(1))


def _load_side_by_side(w_hbm, w_full, sem):
    width = w_hbm.shape[2]
    cps = [pltpu.make_async_copy(w_hbm.at[k], w_full.at[:, pl.ds(k * width, width)], sem.at[k]) for k in range(N_CHIPS)]
    for cp in cps:
        cp.start()
    for cp in cps:
        cp.wait()


def _in_proj(x, g1, w_in4, tm=512):
    T = x.shape[0]

    def body(x_ref, g_ref, w_hbm, u_ref, qkv_ref, hg_ref, w_full, sem):
        @pl.when(pl.program_id(0) == 0)
        def _():
            _load_side_by_side(w_hbm, w_full, sem)

        xv = x_ref[...]
        u = (xv * _rms(xv, D_MODEL) * g_ref[...]).astype(BF16)
        u_ref[...] = u
        p = _dot(u, w_full[...])
        qkv_ref[...] = p[:, :QKV_W]
        hg_ref[...] = p[:, QKV_W:]

    return pl.pallas_call(
        body, name="in_proj", grid=(T // tm,),
        in_specs=[pl.BlockSpec((tm, D_MODEL), lambda i: (i, 0)), _full((1, D_MODEL)), ANY],
        out_specs=[pl.BlockSpec((tm, D_MODEL), lambda i: (i, 0)), pl.BlockSpec((tm, QKV_W), lambda i: (i, 0)),
                   pl.BlockSpec((tm, HG_W), lambda i: (i, 0))],
        out_shape=[jax.ShapeDtypeStruct((T, D_MODEL), BF16), jax.ShapeDtypeStruct((T, QKV_W), F32),
                   jax.ShapeDtypeStruct((T, HG_W), F32)],
        scratch_shapes=[pltpu.VMEM((D_MODEL, IN_TOTAL), BF16), pltpu.SemaphoreType.DMA((N_CHIPS,))],
        compiler_params=_cp("arbitrary"),
    )(x, g1, w_in4)


def _attn_masks(bias_ref):
    lane = lax.broadcasted_iota(jnp.int32, (ATTN_BLK, HEAD_PAIR), 1)
    row = lax.broadcasted_iota(jnp.int32, (2 * ATTN_BLK, 2 * ATTN_BLK), 0)
    col = lax.broadcasted_iota(jnp.int32, (2 * ATTN_BLK, 2 * ATTN_BLK), 1)
    base = jnp.where(row >= ATTN_BLK, row - ATTN_BLK, row) - col
    for k in range(2):
        dist = base + k * ATTN_BLK
        bias_ref[k] = jnp.where((dist >= 0) & (dist <= ATTN_BLK), 0.0, NEG)
    bias_ref[2] = jnp.where(col >= ATTN_BLK, bias_ref[1], NEG)
    return lane < 64


def _two_heads(blk, first):
    zero = jnp.zeros_like(blk)
    return jnp.concatenate([jnp.where(first, blk, zero), jnp.where(first, zero, blk)], axis=0)


def _attn_rows(idx, nb, d):
    r, n = idx // nb, idx % nb
    kb = jnp.maximum(n - 1, 0)
    if d == 1:
        q0 = pl.multiple_of(n * ATTN_BLK, ATTN_BLK)
        k0 = pl.multiple_of(kb * ATTN_BLK, ATTN_BLK)
        return pl.ds(q0, ATTN_BLK), pl.ds(k0, 2 * ATTN_BLK), n - kb
    return (pl.ds(r + d * ATTN_BLK * n, ATTN_BLK, stride=d), pl.ds(r + d * ATTN_BLK * kb, 2 * ATTN_BLK, stride=d),
            n - kb)


def _attn_fwd(qkv):
    T = qkv.shape[0]

    n_blocks = T // ATTN_BLK

    def body(q_ref, k_ref, v_ref, o_ref, m_ref, l_ref, bias_ref):
        first = _attn_masks(bias_ref)
        for bi, d in enumerate(DILATIONS):
            nb = T // d // ATTN_BLK

            chains = ATTN_CHAINS_FWD
            per_chain = n_blocks // chains
            carried = d > 1 and per_chain % nb == 0

            def block(idx, kept=None, d=d, nb=nb, bi=bi, carried=carried):
                rows, keys, which = _attn_rows(idx, nb, d)
                q2 = _two_heads(q_ref[rows, :] * 0.125, first).astype(BF16)
                if carried:
                    k_own, v_own = k_ref[rows, :].astype(BF16), v_ref[rows, :].astype(BF16)
                    kw = jnp.concatenate([kept[0], k_own], axis=0)
                    vw = jnp.concatenate([kept[1], v_own], axis=0)
                    which = 2 - which
                else:
                    kw = k_ref[keys, :].astype(BF16)
                    vw = v_ref[keys, :].astype(BF16)
                old = (o_ref[rows, :], m_ref[rows, :], l_ref[rows, :]) if bi else None
                s = _dot_nt(q2, kw) + bias_ref[which]
                mb = jnp.max(s, axis=-1, keepdims=True)
                p = jnp.exp(s - mb)
                lb = jnp.sum(p, axis=-1, keepdims=True)
                o2 = _dot(p.astype(BF16), vw)
                o = jnp.where(first, o2[:ATTN_BLK], o2[ATTN_BLK:])
                m = jnp.where(first, mb[:ATTN_BLK], mb[ATTN_BLK:])
                l = jnp.where(first, lb[:ATTN_BLK], lb[ATTN_BLK:])
                if bi:
                    po, pm, pl_ = old
                    mn = jnp.maximum(pm, m)
                    wa = jnp.exp(pm - mn)
                    wb = jnp.exp(m - mn)
                    o, l, m = po * wa + o * wb, pl_ * wa + l * wb, mn
                return (rows, o, m, l), ((k_own, v_own) if carried else 0)

            def step(i, kept, block=block, carried=carried, chains=chains, per_chain=per_chain):
                done = [block(i + ch * per_chain, kept[ch] if carried else None) for ch in range(chains)]
                for (rows, o, m, l), _ in done:
                    o_ref[rows, :] = o
                    m_ref[rows, :] = m
                    l_ref[rows, :] = l
                return tuple(k for _, k in done) if carried else kept

            zero = jnp.zeros((ATTN_BLK, HEAD_PAIR), BF16)
            lax.fori_loop(0, per_chain, step, ((zero, zero),) * chains if carried else 0)

        def finish(i, carry):
            rows = pl.ds(pl.multiple_of(i * SUPER, SUPER), SUPER)
            l = l_ref[rows, :]
            o_ref[rows, :] = o_ref[rows, :] / l
            m_ref[rows, :] = m_ref[rows, :] + jnp.log(l)
            return carry

        lax.fori_loop(0, T // SUPER, finish, 0)

    col = lambda off: pl.BlockSpec((T, HEAD_PAIR), lambda j: (0, off + j))
    return pl.pallas_call(
        body, name="attn_fwd", grid=(4,),
        in_specs=[col(0), col(4), col(8)], out_specs=[col(0), col(0)],
        out_shape=[jax.ShapeDtypeStruct((T, ATTN_W), F32)] * 2,
        scratch_shapes=[pltpu.VMEM((T, HEAD_PAIR), F32), pltpu.VMEM((3, 2 * ATTN_BLK, 2 * ATTN_BLK), F32)],
        compiler_params=_cp("arbitrary"),
    )(qkv, qkv, qkv)


def _attn_bwd(qkv, o, lse, do, token=None):
    T = qkv.shape[0]
    per_chain = T // ATTN_BLK // ATTN_CHAINS
    extra = [] if token is None else [token]

    def body(q_ref, k_ref, v_ref, o_ref, lse_ref, do_ref, *rest):
        outs = rest[len(extra):len(extra) + 3]
        dq_ref, dk_ref, dv_ref, dkb_ref, dvb_ref, bias_ref = rest[len(extra) + 3:]
        first = _attn_masks(bias_ref)
        dq_ref[...] = jnp.zeros_like(dq_ref)
        dk_ref[...] = jnp.zeros_like(dk_ref)
        dv_ref[...] = jnp.zeros_like(dv_ref)

        def grads(rows, kw, vw, which):
            q2 = _two_heads(q_ref[rows, :] * 0.125, first).astype(BF16)
            lse_b = lse_ref[rows, :]
            dob = do_ref[rows, :]
            prod = dob * o_ref[rows, :]
            old = dq_ref[rows, :]
            lse2 = jnp.concatenate(
                [jnp.max(jnp.where(first, lse_b, NEG), axis=-1, keepdims=True),
                 jnp.max(jnp.where(first, NEG, lse_b), axis=-1, keepdims=True)], axis=0)
            p = jnp.exp(_dot_nt(q2, kw) + (bias_ref[which] - lse2))
            delta = jnp.concatenate(
                [jnp.sum(jnp.where(first, prod, 0.0), axis=-1, keepdims=True),
                 jnp.sum(jnp.where(first, 0.0, prod), axis=-1, keepdims=True)], axis=0)
            do2 = _two_heads(dob, first).astype(BF16)
            ds = (p * (_dot_nt(do2, vw) - delta)).astype(BF16)
            dq2 = _dot(ds, kw) * 0.125
            return (old + jnp.where(first, dq2[:ATTN_BLK], dq2[ATTN_BLK:]), _dot_tn(ds, q2),
                    _dot_tn(p.astype(BF16), do2))

        def block(idx):
            rows, keys, which = _attn_rows(idx, T // ATTN_BLK, 1)
            old = dk_ref[keys, :], dv_ref[keys, :]
            dq, ck, cv = grads(rows, k_ref[keys, :].astype(BF16), v_ref[keys, :].astype(BF16), which)
            return rows, keys, dq, old[0] + ck, old[1] + cv

        def step(i, carry):
            done = [block(i + ch * per_chain) for ch in range(ATTN_CHAINS)]
            for rows, keys, dq, dk, dv in done:
                dq_ref[rows, :] = dq
                dk_ref[keys, :] = dk
                dv_ref[keys, :] = dv
            return carry

        lax.fori_loop(0, per_chain, step, 0)

        for d in DILATIONS[1:]:
            nb = T // d // ATTN_BLK

            def block(idx, kept, d=d, nb=nb):
                r, n = idx // nb, idx % nb
                rows = pl.ds(r + d * ATTN_BLK * n, ATTN_BLK, stride=d)
                before = pl.ds(r + d * ATTN_BLK * jnp.maximum(n - 1, 0), ATTN_BLK, stride=d)
                k_prev, v_prev, dk_prev, dv_prev = kept
                k_own, v_own = k_ref[rows, :].astype(BF16), v_ref[rows, :].astype(BF16)
                dq, ck, cv = grads(rows, jnp.concatenate([k_prev, k_own], axis=0),
                                   jnp.concatenate([v_prev, v_own], axis=0), jnp.where(n > 0, 1, 2))
                stores = (rows, before, dq, dk_prev + ck[:ATTN_BLK], dv_prev + cv[:ATTN_BLK], ck[ATTN_BLK:], cv[ATTN_BLK:])
                return stores, (k_own, v_own, ck[ATTN_BLK:], cv[ATTN_BLK:])

            def step(i, kept, block=block):
                done = [block(i + ch * per_chain, kept[ch]) for ch in range(ATTN_CHAINS)]
                for (rows, before, dq, dk_done, dv_done, dk_own, dv_own), _ in done:
                    dq_ref[rows, :] = dq
                    dkb_ref[before, :] = dk_done
                    dvb_ref[before, :] = dv_done
                    dkb_ref[rows, :] = dk_own
                    dvb_ref[rows, :] = dv_own
                return tuple(k for _, k in done)

            zero = jnp.zeros((ATTN_BLK, HEAD_PAIR), F32)
            lax.fori_loop(0, per_chain, step, ((zero.astype(BF16), zero.astype(BF16), zero, zero),) * ATTN_CHAINS)

            def add(i, carry):
                rows = pl.ds(pl.multiple_of(i * SUPER, SUPER), SUPER)
                dk_ref[rows, :] += dkb_ref[rows, :]
                dv_ref[rows, :] += dvb_ref[rows, :]
                return carry

            lax.fori_loop(0, T // SUPER, add, 0)

        def emit(i, carry):
            rows = pl.ds(pl.multiple_of(i * SUPER, SUPER), SUPER)
            for out, acc in zip(outs, (dq_ref, dk_ref, dv_ref)):
                out[rows, :] = acc[rows, :].astype(BF16)
            return carry

        lax.fori_loop(0, T // SUPER, emit, 0)

    col = lambda off: pl.BlockSpec((T, HEAD_PAIR), lambda j: (0, off + j))
    return pl.pallas_call(
        body, name="attn_bwd", grid=(4,),
        in_specs=[col(0), col(4), col(8), col(0), col(0), col(0)] + [_full(t.shape) for t in extra],
        out_specs=[col(0)] * 3,
        out_shape=[jax.ShapeDtypeStruct((T, ATTN_W), BF16)] * 3,
        scratch_shapes=[pltpu.VMEM((T, HEAD_PAIR), F32)] * 5 + [pltpu.VMEM((3, 2 * ATTN_BLK, 2 * ATTN_BLK), F32)],
        compiler_params=_cp("arbitrary"),
    )(qkv, qkv, qkv, o, lse, do, *extra)


def _chunk_ids():
    row = lax.broadcasted_iota(jnp.int32, (SUPER, HGRN_DIM), 0)
    r2 = lax.broadcasted_iota(jnp.int32, (SUPER, SUPER), 0)
    c2 = lax.broadcasted_iota(jnp.int32, (SUPER, SUPER), 1)
    amask = ((r2 // HGRN_CHUNK) == (c2 // HGRN_CHUNK)) & (c2 <= r2)
    return row % HGRN_CHUNK, row // HGRN_CHUNK, amask


def _cumsum_chunk(x, rmod):
    s = 1
    while s < HGRN_CHUNK:
        x = x + jnp.where(rmod >= s, pltpu.roll(x, s, 0), 0.0)
        s *= 2
    return x


def _suffix_sum_chunk(x, rmod):
    s = 1
    while s < HGRN_CHUNK:
        x = x + jnp.where(rmod < HGRN_CHUNK - s, pltpu.roll(x, SUPER - s, 0), 0.0)
        s *= 2
    return x


def _chunk_rows(vs, cid):
    out = vs[-1]
    for c in reversed(range(len(vs) - 1)):
        out = jnp.where(cid == c, vs[c], out)
    return out


def _expand(x, cid):
    return jnp.concatenate([jnp.where(cid == c, x, 0.0) for c in range(SUPER // HGRN_CHUNK)], axis=1)


def _hgrn_gates(q, f, lbv, rmod, cid, tmp):
    sq = _sigmoid(q)
    sg = _sigmoid(f)
    forget = lbv + (1.0 - lbv) * sg
    key = 1.0 - forget
    b = _cumsum_chunk(jnp.log(forget), rmod)
    tmp[...] = b
    bends = [tmp[c * HGRN_CHUNK + HGRN_CHUNK - 1:(c + 1) * HGRN_CHUNK, :] for c in range(SUPER // HGRN_CHUNK)]
    eb = jnp.exp(b)
    enb = jnp.exp(-b)
    ebe = jnp.exp(_chunk_rows(bends, cid) - b)
    return sq, sg, forget, key, eb, enb, ebe, q * sq * eb, key * enb, key * ebe, [jnp.exp(v) for v in bends]


def _hgrn_fwd(hg, lb):
    T = hg.shape[0]
    nsc = T // SUPER
    NC = SUPER // HGRN_CHUNK

    def body(q_ref, f_ref, i_ref, lb_ref, o_ref, st_ref, state, tmp):
        rmod, cid, amask = _chunk_ids()
        state[...] = jnp.zeros_like(state)
        lbv = lb_ref[...]

        def local(sc, u):
            rows = pl.ds(pl.multiple_of(sc * SUPER, SUPER), SUPER)
            iv = i_ref[rows, :].astype(BF16)
            qd, ki, ke, dec = _hgrn_gates(q_ref[rows, :], f_ref[rows, :], lbv, rmod, cid, tmp.at[u])[-4:]
            a = jnp.where(amask, _dot_nt(qd.astype(BF16), ki.astype(BF16)), 0.0)
            return rows, qd, dec, _dot(a.astype(BF16), iv), _dot_tn(iv, _expand(ke, cid).astype(BF16))

        def step(i, carry):
            parts = [local(i * HGRN_SIDE + u, u) for u in range(HGRN_SIDE)]
            st = state[...]
            entering = []
            for u, (_, _, dec, _, ut) in enumerate(parts):
                st_ref[0, i * HGRN_SIDE + u] = st
                sts = []
                for c in range(NC):
                    sts.append(st)
                    st = st * dec[c] + ut[:, c * HGRN_DIM:(c + 1) * HGRN_DIM]
                entering.append(jnp.concatenate(sts, axis=1).astype(BF16))
            state[...] = st
            for (rows, qd, _, o, _), sts in zip(parts, entering):
                o_ref[rows, :] = o + _dot_nt(_expand(qd, cid).astype(BF16), sts)
            return carry

        lax.fori_loop(0, nsc // HGRN_SIDE, step, 0)

    col = lambda off: pl.BlockSpec((T, HGRN_DIM), lambda h: (0, off + h))
    return pl.pallas_call(
        body, name="hgrn_fwd", grid=(HGRN_HEADS,),
        in_specs=[col(0), col(4), col(8), pl.BlockSpec((1, HGRN_DIM), lambda h: (0, h))],
        out_specs=[pl.BlockSpec((T, HGRN_DIM), lambda h: (0, h)),
                   pl.BlockSpec((1, nsc, HGRN_DIM, HGRN_DIM), lambda h: (h, 0, 0, 0))],
        out_shape=[jax.ShapeDtypeStruct((T, HGRN_W), F32),
                   jax.ShapeDtypeStruct((HGRN_HEADS, nsc, HGRN_DIM, HGRN_DIM), F32)],
        scratch_shapes=[pltpu.VMEM((HGRN_DIM, HGRN_DIM), F32), pltpu.VMEM((HGRN_SIDE, SUPER, HGRN_DIM), F32)],
        compiler_params=_cp("arbitrary"),
    )(hg, hg, hg, lb)


def _hgrn_bwd(hg, lb, states, do):
    T = hg.shape[0]
    nsc = T // SUPER
    NC = SUPER // HGRN_CHUNK

    def body(q_ref, f_ref, i_ref, lb_ref, st_ref, do_ref, dq_ref, df_ref, di_ref, dlb_ref, dstate, tmp):
        rmod, cid, amask = _chunk_ids()
        dstate[...] = jnp.zeros_like(dstate)
        dlb_ref[...] = jnp.zeros_like(dlb_ref)
        lbv = lb_ref[...]

        def local(sc, u):
            rows = pl.ds(pl.multiple_of(sc * SUPER, SUPER), SUPER)
            q = q_ref[rows, :]
            ivf = i_ref[rows, :]
            iv = ivf.astype(BF16)
            dof = do_ref[rows, :]
            dob = dof.astype(BF16)
            sq, sg, forget, key, eb, enb, ebe, qd, ki, ke, dec = _hgrn_gates(q, f_ref[rows, :], lbv, rmod, cid,
                                                                            tmp.at[u])
            qdb, kib = qd.astype(BF16), ki.astype(BF16)
            keexp = _expand(ke, cid).astype(BF16)
            a = jnp.where(amask, _dot_nt(qdb, kib), 0.0).astype(BF16)
            ut = _dot_tn(iv, keexp)
            st = st_ref[0, sc]
            sts = []
            for c in range(NC):
                sts.append(st)
                st = st * dec[c] + ut[:, c * HGRN_DIM:(c + 1) * HGRN_DIM]
            gt = _dot_tn(dob, _expand(qd, cid).astype(BF16))
            da = jnp.where(amask, _dot_nt(dob, iv), 0.0).astype(BF16)
            ststack = jnp.concatenate(sts, axis=0).astype(BF16)
            return dict(rows=rows, q=q, sq=sq, sg=sg, forget=forget, eb=eb, enb=enb, ebe=ebe, qd=qd, ki=ki, ke=ke,
                        dec=dec, sts=sts, gt=gt, keexp=keexp, ivexp=_expand(ivf, cid).astype(BF16),
                        div=_dot_tn(a, dob), dki=_dot_tn(da, qdb),
                        dqd=_dot(da, kib) + _dot(_expand(dof, cid).astype(BF16), ststack))

        def finish(p, nxt, ddec):
            ncat = jnp.concatenate(nxt, axis=1).astype(BF16)
            nstack = jnp.concatenate(nxt, axis=0).astype(BF16)
            dke = _dot(p["ivexp"], nstack)
            dkk = dke * p["ke"]
            dkey = p["dki"] * p["enb"] + dke * p["ebe"]
            db = p["dqd"] * p["qd"] - p["dki"] * p["ki"] - dkk
            dbends = [_colsum(jnp.where(cid == c, dkk, 0.0)) + ddec[c] * p["dec"][c] for c in range(NC)]
            dforget = (_suffix_sum_chunk(db, rmod) + _chunk_rows(dbends, cid)) / p["forget"] - dkey
            sg, sq, q = p["sg"], p["sq"], p["q"]
            df_ref[p["rows"], :] = (dforget * (1.0 - lbv) * sg * (1.0 - sg)).astype(BF16)
            dq_ref[p["rows"], :] = (p["dqd"] * p["eb"] * (sq * (1.0 + q * (1.0 - sq)))).astype(BF16)
            di_ref[p["rows"], :] = (p["div"] + _dot_nt(p["keexp"], ncat)).astype(BF16)
            return _colsum(dforget * (1.0 - sg))

        def step(i, carry):
            parts = [local(nsc - 1 - (i * HGRN_SIDE + u), u) for u in range(HGRN_SIDE)]
            dst = dstate[...]
            chained = []
            for p in parts:
                nxt = [None] * NC
                ddec = [None] * NC
                for c in reversed(range(NC)):
                    nxt[c] = dst
                    ddec[c] = _colsum(dst * p["sts"][c])
                    dst = dst * p["dec"][c] + p["gt"][:, c * HGRN_DIM:(c + 1) * HGRN_DIM]
                chained.append((nxt, ddec))
            dstate[...] = dst
            dlb = dlb_ref[...]
            for p, (nxt, ddec) in zip(parts, chained):
                dlb = dlb + finish(p, nxt, ddec)
            dlb_ref[...] = dlb
            return carry

        lax.fori_loop(0, nsc // HGRN_SIDE, step, 0)

    col = lambda off: pl.BlockSpec((T, HGRN_DIM), lambda h: (0, off + h))
    own = pl.BlockSpec((T, HGRN_DIM), lambda h: (0, h))
    vec = pl.BlockSpec((1, HGRN_DIM), lambda h: (0, h))
    return pl.pallas_call(
        body, name="hgrn_bwd", grid=(HGRN_HEADS,),
        in_specs=[col(0), col(4), col(8), vec,
                  pl.BlockSpec((1, nsc, HGRN_DIM, HGRN_DIM), lambda h: (h, 0, 0, 0)), own],
        out_specs=[own, own, own, vec],
        out_shape=[jax.ShapeDtypeStruct((T, HGRN_W), BF16)] * 3 + [jax.ShapeDtypeStruct((1, HGRN_W), F32)],
        scratch_shapes=[pltpu.VMEM((HGRN_DIM, HGRN_DIM), F32), pltpu.VMEM((HGRN_SIDE, SUPER, HGRN_DIM), F32)],
        compiler_params=_cp("arbitrary"),
    )(hg, hg, hg, lb, states, do)


def _rec_heads(rec, gate, g_h):
    rr = jnp.concatenate(
        [jnp.broadcast_to(_rms(rec[:, h * HGRN_DIM:(h + 1) * HGRN_DIM], HGRN_DIM), (rec.shape[0], HGRN_DIM))
         for h in range(HGRN_HEADS)], axis=1)
    rn = rec * rr
    sg = _sigmoid(gate)
    return rr, rn, sg


_INV_SQRT2 = 1.0 / math.sqrt(2.0)
_INV_SQRT2PI = 1.0 / math.sqrt(2.0 * math.pi)


def _gelu(x):
    return 0.5 * x * (1.0 + lax.erf(x * _INV_SQRT2))


def _gelu_and_grad(x):
    z = x * _INV_SQRT2
    cdf = 0.5 * (1.0 + lax.erf(z))
    return x * cdf, cdf + (x * _INV_SQRT2PI) * jnp.exp(-(z * z))


def _shift_down(g, prev, rowid):
    p1 = _row(prev, prev.shape[0] - 1)
    p2 = _row(prev, prev.shape[0] - 2)
    s1 = jnp.where(rowid == 0, p1, pltpu.roll(g, 1, 0))
    s2 = jnp.where(rowid == 0, p2, jnp.where(rowid == 1, p1, pltpu.roll(g, 2, 0)))
    return s1, s2


def _mlp_fwd(attn_o, rec_o, hg, x, g_a, g_h, w_out, g2, w_up4, conv_w, conv_b, w_down, gf, tgt, tm=256):
    T = x.shape[0]

    def body(a_ref, r_ref, gt_ref, x_ref, ga_ref, gh_ref, wo_ref, g2_ref, wu_hbm, cw_ref, cb_ref, wd_ref, gf_ref, t_ref,
             h1_ref, mixed_ref, u_ref, gate_ref, val_ref, conv_ref, act_ref, dh_ref, loss_ref, dgf_ref,
             carry, wu_ref, sem):
        i = pl.program_id(0)

        @pl.when(i == 0)
        def _():
            carry[...] = jnp.zeros_like(carry)
            loss_ref[...] = jnp.zeros_like(loss_ref)
            dgf_ref[...] = jnp.zeros_like(dgf_ref)
            _load_side_by_side(wu_hbm, wu_ref, sem)

        a = a_ref[...]
        an = a * _rms(a, ATTN_W) * ga_ref[...]
        og = gt_ref[...]
        _, rn, sg = _rec_heads(r_ref[...], og, gh_ref[...])
        mixed = jnp.concatenate([an, rn * gh_ref[...] * (og * sg)], axis=1).astype(BF16)
        mixed_ref[...] = mixed
        h = x_ref[...] + _dot(mixed, wo_ref[...])
        h1_ref[...] = h
        u = (h * _rms(h, D_MODEL) * g2_ref[...]).astype(BF16)
        u_ref[...] = u
        y2 = jnp.zeros((tm, D_MODEL), F32)
        for lo, hi in FF_CHUNKS:
            cols = slice(lo, hi)
            rowid = lax.broadcasted_iota(jnp.int32, (tm, hi - lo), 0)
            gb = _dot(u, wu_ref[:, lo:hi]).astype(BF16)
            vb = _dot(u, wu_ref[:, D_FF + lo:D_FF + hi]).astype(BF16)
            gate_ref[:, cols] = gb
            val_ref[:, cols] = vb
            g = gb.astype(F32)
            s1, s2 = _shift_down(g, carry[:, cols], rowid)
            carry[:, cols] = g[tm - 8:, :]
            conv = cb_ref[:, cols] + cw_ref[0:1, cols] * s2 + cw_ref[1:2, cols] * s1 + cw_ref[2:3, cols] * g
            act = (_gelu(conv) * vb.astype(F32)).astype(BF16)
            conv_ref[:, cols] = conv.astype(BF16)
            act_ref[:, cols] = act
            y2 = y2 + _dot(act, wd_ref[cols, :])
        h2 = h + y2
        rf = _rms(h2, D_MODEL)
        n = h2 * rf
        gfv = gf_ref[...]
        e = n * gfv - t_ref[...]
        loss_ref[...] += jnp.sum(e * e) * (0.5 / D_MODEL)
        dy = e * (1.0 / D_MODEL)
        dgf_ref[...] += _colsum(dy * n)
        dh_ref[...] = _rms_bwd(dy * gfv, n, rf, D_MODEL)

    row = lambda w: pl.BlockSpec((tm, w), lambda i: (i, 0))
    return pl.pallas_call(
        body, name="mlp_fwd", grid=(T // tm,),
        in_specs=[row(ATTN_W), row(HGRN_W), pl.BlockSpec((tm, HGRN_W), lambda i: (i, 3)), row(D_MODEL),
                  _full((1, ATTN_W)), _full((1, HGRN_W)), _once((D_MODEL, D_MODEL)),
                  _full((1, D_MODEL)), ANY, _full((3, D_FF)),
                  _full((1, D_FF)), _once((D_FF, D_MODEL)), _full((1, D_MODEL)), row(D_MODEL)],
        out_specs=[row(D_MODEL), row(D_MODEL), row(D_MODEL), row(D_FF), row(D_FF), row(D_FF), row(D_FF), row(D_MODEL),
                   _full((1, 128)), _full((1, D_MODEL))],
        out_shape=[jax.ShapeDtypeStruct((T, D_MODEL), F32), jax.ShapeDtypeStruct((T, D_MODEL), BF16),
                   jax.ShapeDtypeStruct((T, D_MODEL), BF16)] + [jax.ShapeDtypeStruct((T, D_FF), BF16)] * 4
        + [jax.ShapeDtypeStruct((T, D_MODEL), F32),
                   jax.ShapeDtypeStruct((1, 128), F32), jax.ShapeDtypeStruct((1, D_MODEL), F32)],
        scratch_shapes=[pltpu.VMEM((8, D_FF), F32), pltpu.VMEM((D_MODEL, 2 * D_FF), BF16),
                        pltpu.SemaphoreType.DMA((N_CHIPS,))],
        compiler_params=_cp("arbitrary"),
    )(attn_o, rec_o, hg, x, g_a, g_h, w_out, g2, w_up4, conv_w, conv_b, w_down, gf, tgt)


def _mlp_bwd(dh2, gate, val, conv, act, conv_w, w_down, tm=256):
    T = dh2.shape[0]
    nb = T // tm

    def body(dh_ref, gate_ref, val_ref, conv_ref, act_ref, cw_ref, wd_ref, dgv_ref, dcw_ref, dcb_ref, dwd_ref,
             carry, acc):
        i = pl.program_id(0)

        @pl.when(i == 0)
        def _():
            carry[...] = jnp.zeros_like(carry)
            dcw_ref[...] = jnp.zeros_like(dcw_ref)
            dcb_ref[...] = jnp.zeros_like(dcb_ref)
            acc[...] = jnp.zeros_like(acc)

        dhb = dh_ref[...].astype(BF16)
        for lo, hi in MLP_BWD_CHUNKS:
            cols = slice(lo, hi)
            rowid = lax.broadcasted_iota(jnp.int32, (tm, hi - lo), 0)
            acc[cols, :] += _dot_tn(act_ref[:, cols], dhb)
            g = gate_ref[:, cols].astype(F32)
            v = val_ref[:, cols].astype(F32)
            cv = conv_ref[:, cols].astype(F32)
            dact = _dot_nt(dhb, wd_ref[cols, :])
            gl, gp = _gelu_and_grad(cv)
            dconv = dact * v * gp
            nxt = carry[:, cols]
            n0, n1 = _row(nxt, 0), _row(nxt, 1)
            u1 = jnp.where(rowid == tm - 1, n0, pltpu.roll(dconv, tm - 1, 0))
            u2 = jnp.where(rowid == tm - 1, n1, jnp.where(rowid == tm - 2, n0, pltpu.roll(dconv, tm - 2, 0)))
            carry[:, cols] = dconv[0:8, :]
            dcb_ref[:, cols] += _colsum(dconv)
            dcw_ref[0:1, cols] += _colsum(u2 * g)
            dcw_ref[1:2, cols] += _colsum(u1 * g)
            dcw_ref[2:3, cols] += _colsum(dconv * g)
            dgate = cw_ref[2:3, cols] * dconv + cw_ref[1:2, cols] * u1 + cw_ref[0:1, cols] * u2
            dgv_ref[:, cols] = dgate.astype(BF16)
            dgv_ref[:, D_FF + lo:D_FF + hi] = (dact * gl).astype(BF16)

        @pl.when(i == nb - 1)
        def _():
            for lo, hi in MLP_BWD_CHUNKS:
                dwd_ref[lo:hi, :] = acc[lo:hi, :].astype(BF16)

    rev = lambda w: pl.BlockSpec((tm, w), lambda i: (nb - 1 - i, 0))
    return pl.pallas_call(
        body, name="mlp_bwd", grid=(nb,),
        in_specs=[rev(D_MODEL), rev(D_FF), rev(D_FF), rev(D_FF), rev(D_FF), _full((3, D_FF)), _once((D_FF, D_MODEL))],
        out_specs=[rev(2 * D_FF), _full((3, D_FF)), _full((1, D_FF)), _once((D_FF, D_MODEL))],
        out_shape=[jax.ShapeDtypeStruct((T, 2 * D_FF), BF16), jax.ShapeDtypeStruct((3, D_FF), F32),
                   jax.ShapeDtypeStruct((1, D_FF), F32), jax.ShapeDtypeStruct((D_FF, D_MODEL), BF16)],
        scratch_shapes=[pltpu.VMEM((8, D_FF), F32), pltpu.VMEM((D_FF, D_MODEL), F32)],
        compiler_params=_cp("arbitrary"),
    )(dh2, gate, val, conv, act, conv_w, w_down)


def _up_out_bwd(dgv, w_up4, h1, g2, dh2, w_out, mixed, attn_o, rec_o, hg, g_a, g_h, tm=256):
    T = h1.shape[0]
    nb = T // tm

    def body(dgv_ref, wu_hbm, h_ref, g2_ref, dh2_ref, wo_ref, mx_ref, a_ref, r_ref, gt_ref, ga_ref, gh_ref,
             dh1_ref, dg2_ref, da_ref, dr_ref, dgt_ref, dga_ref, dgh_ref, dwo_ref, wu_ref, sem, acc):
        i = pl.program_id(0)

        @pl.when(i == 0)
        def _():
            dg2_ref[...] = jnp.zeros_like(dg2_ref)
            dga_ref[...] = jnp.zeros_like(dga_ref)
            dgh_ref[...] = jnp.zeros_like(dgh_ref)
            acc[...] = jnp.zeros_like(acc)
            _load_side_by_side(wu_hbm, wu_ref, sem)

        du = _dot_nt(dgv_ref[...], wu_ref[...])
        h = h_ref[...]
        r = _rms(h, D_MODEL)
        n = h * r
        dg2_ref[...] += _colsum(du * n)
        dh1 = dh2_ref[...] + _rms_bwd(du * g2_ref[...], n, r, D_MODEL)
        dh1_ref[...] = dh1
        dh1b = dh1.astype(BF16)
        acc[...] += _dot_tn(mx_ref[...], dh1b)
        dmix = _dot_nt(dh1b, wo_ref[...])
        dan = dmix[:, :ATTN_W]
        a = a_ref[...]
        ra = _rms(a, ATTN_W)
        na = a * ra
        dga_ref[...] += _colsum(dan * na)
        da_ref[...] = _rms_bwd(dan * ga_ref[...], na, ra, ATTN_W)
        dmr = dmix[:, ATTN_W:]
        gate = gt_ref[...]
        ghv = gh_ref[...]
        rr, rn, sg = _rec_heads(r_ref[...], gate, ghv)
        dgt_ref[...] = (dmr * rn * ghv * (sg * (1.0 + gate * (1.0 - sg)))).astype(BF16)
        drecn = dmr * (gate * sg)
        dgh_ref[...] += _colsum(drecn * rn)
        drn = drecn * ghv
        prod = drn * rn
        mean = jnp.concatenate(
            [jnp.broadcast_to(jnp.sum(prod[:, h_ * HGRN_DIM:(h_ + 1) * HGRN_DIM], axis=-1, keepdims=True),
                              (tm, HGRN_DIM)) for h_ in range(HGRN_HEADS)], axis=1) * (1.0 / HGRN_DIM)
        dr_ref[...] = rr * (drn - rn * mean)

        @pl.when(i == nb - 1)
        def _():
            dwo_ref[...] = acc[...].astype(BF16)

    row = lambda w: pl.BlockSpec((tm, w), lambda i: (i, 0))
    return pl.pallas_call(
        body, name="up_out_bwd", grid=(nb,),
        in_specs=[row(2 * D_FF), ANY, row(D_MODEL), _full((1, D_MODEL)),
                  row(D_MODEL), _once((D_MODEL, D_MODEL)), row(D_MODEL), row(ATTN_W), row(HGRN_W),
                  pl.BlockSpec((tm, HGRN_W), lambda i: (i, 3)), _full((1, ATTN_W)), _full((1, HGRN_W))],
        out_specs=[row(D_MODEL), _full((1, D_MODEL)), row(ATTN_W), row(HGRN_W), row(HGRN_W),
                   _full((1, ATTN_W)), _full((1, HGRN_W)), _once((D_MODEL, D_MODEL))],
        out_shape=[jax.ShapeDtypeStruct((T, D_MODEL), F32), jax.ShapeDtypeStruct((1, D_MODEL), F32),
                   jax.ShapeDtypeStruct((T, ATTN_W), F32), jax.ShapeDtypeStruct((T, HGRN_W), F32),
                   jax.ShapeDtypeStruct((T, HGRN_W), BF16), jax.ShapeDtypeStruct((1, ATTN_W), F32),
                   jax.ShapeDtypeStruct((1, HGRN_W), F32), jax.ShapeDtypeStruct((D_MODEL, D_MODEL), BF16)],
        scratch_shapes=[pltpu.VMEM((D_MODEL, 2 * D_FF), BF16), pltpu.SemaphoreType.DMA((N_CHIPS,)),
                        pltpu.VMEM((D_MODEL, D_MODEL), F32)],
        compiler_params=_cp("arbitrary"),
    )(dgv, w_up4, h1, g2, dh2, w_out, mixed, attn_o, rec_o, hg, g_a, g_h)


def _in_bwd(dqkv, dhg, w_in4, u1, x, g1, dh1, tm=256):
    T = x.shape[0]
    nb = T // tm

    def body(*refs):
        parts = refs[:7]
        w_hbm, u_ref, x_ref, g_ref, dh1_ref, dw_ref, dx_ref, dg_ref, w_full, sem, acc = refs[7:]
        i = pl.program_id(0)

        @pl.when(i == 0)
        def _():
            dg_ref[...] = jnp.zeros_like(dg_ref)
            acc[...] = jnp.zeros_like(acc)
            _load_side_by_side(w_hbm, w_full, sem)

        dp = jnp.concatenate([p[...] for p in parts], axis=1)
        acc[...] += _dot_tn(u_ref[...], dp)
        du = _dot_nt(dp, w_full[...])
        xv = x_ref[...]
        r = _rms(xv, D_MODEL)
        n = xv * r
        dg_ref[...] += _colsum(du * n)
        dx_ref[...] = dh1_ref[...] + _rms_bwd(du * g_ref[...], n, r, D_MODEL)

        @pl.when(i == nb - 1)
        def _():
            for k in range(N_CHIPS):
                dw_ref[k] = acc[:, k * IN_SHARD:(k + 1) * IN_SHARD].astype(BF16)

    row = lambda w: pl.BlockSpec((tm, w), lambda i: (i, 0))
    return pl.pallas_call(
        body, name="in_bwd", grid=(nb,),
        in_specs=[row(ATTN_W)] * 7 + [ANY, row(D_MODEL), row(D_MODEL), _full((1, D_MODEL)), row(D_MODEL)],
        out_specs=[_once((N_CHIPS, D_MODEL, IN_SHARD)), row(D_MODEL), _full((1, D_MODEL))],
        out_shape=[jax.ShapeDtypeStruct((N_CHIPS, D_MODEL, IN_SHARD), BF16), jax.ShapeDtypeStruct((T, D_MODEL), F32),
                   jax.ShapeDtypeStruct((1, D_MODEL), F32)],
        scratch_shapes=[pltpu.VMEM((D_MODEL, IN_TOTAL), BF16), pltpu.SemaphoreType.DMA((N_CHIPS,)),
                        pltpu.VMEM((D_MODEL, IN_TOTAL), F32)],
        compiler_params=_cp("arbitrary"),
    )(*dqkv, *dhg, w_in4, u1, x, g1, dh1)


def _dw(a, b, kb, nb_, name, tk=1024, side=1):
    T, K = a.shape
    N = b.shape[1]
    nk, nn, nt = K // kb, N // (nb_ * side), T // tk

    def body(a_ref, b_ref, o_ref, acc):
        t = pl.program_id(2)

        @pl.when(t == 0)
        def _():
            acc[...] = jnp.zeros_like(acc)

        acc[...] += _dot_tn(a_ref[...], b_ref[...].astype(BF16))

        @pl.when(t == nt - 1)
        def _():
            for s in range(side):
                o_ref[s] = acc[:, s * nb_:(s + 1) * nb_].astype(BF16)

    return pl.pallas_call(
        body, name=name, grid=(nk, nn, nt),
        in_specs=[pl.BlockSpec((tk, kb), lambda i, j, t: (t, i)),
                  pl.BlockSpec((tk, nb_ * side), lambda i, j, t: (t, j))],
        out_specs=pl.BlockSpec((side, kb, nb_), lambda i, j, t: (i * nn + j, 0, 0)),
        out_shape=jax.ShapeDtypeStruct((nk * nn * side, kb, nb_), BF16),
        scratch_shapes=[pltpu.VMEM((kb, nb_ * side), F32)],
        compiler_params=_cp("arbitrary", "arbitrary", "arbitrary"),
    )(a, b)


def _step_channel(a, x, tgt, g_a, g_h, w_out, g2, w_up4, conv_w, conv_b, w_down, gf):
    h1, mixed, u2, gate, val, conv, act, dh2, loss, dgf = _mlp_fwd(
        a["attn_o"], a["rec_o"], a["hg"], x, g_a, g_h, w_out, g2, w_up4, conv_w, conv_b, w_down, gf, tgt)
    dgv, dcw, dcb, dw_down = _mlp_bwd(dh2, gate, val, conv, act, conv_w, w_down)
    dw_down = dw_down.reshape(N_CHIPS, D_FF // N_CHIPS, D_MODEL)
    dh1, dg2, da, dr, dgt, dga, dgh, dw_out = _up_out_bwd(dgv, w_up4, h1, g2, dh2, w_out, mixed, a["attn_o"],
                                                          a["rec_o"], a["hg"], g_a, g_h)
    dw_up = _dw(u2, dgv, D_MODEL, UP_SHARD, "dw_up", side=2)
    dw_out = dw_out.reshape(N_CHIPS, D_MODEL // N_CHIPS, D_MODEL)
    return dict(loss=loss, dgf=dgf, dcw=dcw, dcb=dcb, dg2=dg2, dga=dga, dgh=dgh, dh1=dh1, da=da, dr=dr, dgt=dgt,
                dw_down=dw_down, dw_up=dw_up, dw_out=dw_out)


def _step_mixers_bwd(a, b, x, g1, w_in4, lb, dqkv):
    dhq, dhf, dhi, dlb = _hgrn_bwd(a["hg"], lb, a["states"], b["dr"])
    dw_in, dx, dg1 = _in_bwd(dqkv, [dhq, dhf, dhi, b["dgt"]], w_in4, a["u1"], x, g1, b["dh1"])
    return dict(dx=dx, dg1=dg1, dlb=dlb, dw_in=dw_in)


BIG = ("w_in", "w_out", "w_up", "w_down")
ANY = pl.BlockSpec(memory_space=pl.ANY)


def _place():
    x, y, c = lax.axis_index("x"), lax.axis_index("y"), lax.axis_index("c")
    chips = [(1 - x, y), (x, 1 - y), (1 - x, 1 - y)]
    return x, y, c, chips


def _remote(src, dst, send_sems, recv_sems, k, to):
    return pltpu.make_async_remote_copy(src_ref=src, dst_ref=dst, send_sem=send_sems.at[k], recv_sem=recv_sems.at[k],
                                        device_id=to, device_id_type=MESH)


def _gather_weights(shards, conv_w):
    n = len(shards)
    halves = [s.shape[0] // 2 for s in shards]

    def body(*refs):
        ins, cw, outs, ocw = refs[:n], refs[n], refs[n + 1:2 * n + 1], refs[2 * n + 1]
        send_sems, recv_sems = refs[2 * n + 2:]
        x, y, c, chips = _place()
        me, sibling = 2 * x + y, (x, y, 1 - c)

        def part(w, chip, half):
            return outs[w].at[chip, pl.ds(half * halves[w], halves[w]), :]

        sent = []
        for j, chip in enumerate(chips):
            for w in range(n):
                sent.append(_remote(ins[w].at[pl.ds(c * halves[w], halves[w]), :], part(w, me, c),
                                    send_sems, recv_sems, w * 3 + j, (*chip, c)))
            sent.append(_remote(cw, ocw.at[me], send_sems, recv_sems, 6 * n + j, (*chip, c)))
        for cp in sent:
            cp.start()
        for j, chip in enumerate(chips):
            kj = 2 * chip[0] + chip[1]
            for w in range(n):
                _remote(part(w, kj, c), part(w, kj, c), send_sems, recv_sems, w * 3 + j, (*chip, c)).wait_recv()
                fwd = _remote(part(w, kj, c), part(w, kj, c), send_sems, recv_sems, 3 * n + w * 3 + j, sibling)
                fwd.start()
                sent.append(fwd)
        for j, chip in enumerate(chips):
            kj = 2 * chip[0] + chip[1]
            for w in range(n):
                _remote(part(w, kj, 1 - c), part(w, kj, 1 - c), send_sems, recv_sems, 3 * n + w * 3 + j,
                        sibling).wait_recv()
            _remote(cw, ocw.at[kj], send_sems, recv_sems, 6 * n + j, (*chip, c)).wait_recv()
        for cp in sent:
            cp.wait_send()

    n_sem = 6 * n + 3
    outs = pl.pallas_call(
        body, name="gather_weights",
        in_specs=[ANY] * (n + 1), out_specs=[ANY] * (n + 1),
        out_shape=[jax.ShapeDtypeStruct((N_CHIPS,) + s.shape, s.dtype) for s in shards]
        + [jax.ShapeDtypeStruct((N_CHIPS,) + conv_w.shape, conv_w.dtype)],
        scratch_shapes=[pltpu.SemaphoreType.DMA((n_sem,)), pltpu.SemaphoreType.DMA((n_sem,))],
    )(*shards, conv_w)
    chip = 2 * lax.axis_index("x") + lax.axis_index("y")
    return [lax.dynamic_update_slice(o, s[None], (chip,) + (0,) * s.ndim) for o, s in zip(outs, [*shards, conv_w])]


def _allreduce_small(buf):
    rows = buf.shape[0]

    def body(in_ref, out_ref, slots, send_sems, recv_sems):
        x, y, c, _ = _place()
        me = 4 * x + 2 * y + c
        slots[me] = in_ref[...]
        sent = []
        for p in range(1, 8):
            to = (x ^ (p >> 2), y ^ ((p >> 1) & 1), c ^ (p & 1))
            sent.append(_remote(in_ref, slots.at[me], send_sems, recv_sems, p, to))
        for cp in sent:
            cp.start()
        for p in range(1, 8):
            frm = 4 * (x ^ (p >> 2)) + 2 * (y ^ ((p >> 1) & 1)) + (c ^ (p & 1))
            _remote(in_ref, slots.at[frm], send_sems, recv_sems, p, (x, y, c)).wait_recv()
        for cp in sent:
            cp.wait_send()
        acc = slots[0]
        for d in range(1, 8):
            acc = acc + slots[d]
        out_ref[...] = acc

    vm = pl.BlockSpec(memory_space=pltpu.VMEM)
    return pl.pallas_call(
        body, name="allreduce_small", in_specs=[vm], out_specs=vm,
        out_shape=jax.ShapeDtypeStruct(buf.shape, F32),
        scratch_shapes=[pltpu.VMEM((8, rows, 128), F32), pltpu.SemaphoreType.DMA((8,)), pltpu.SemaphoreType.DMA((8,))],
    )(buf)


def _sibling_peer():
    x, y, c, _ = _place()
    return [(x, y, 1 - c)]


def _chip_peers():
    x, y, c, chips = _place()
    return [(*chip, c) for chip in chips]


def _handshake(peers):
    barrier = pltpu.get_barrier_semaphore()
    for peer in peers:
        pl.semaphore_signal(barrier, inc=1, device_id=peer, device_id_type=MESH)
    pl.semaphore_wait(barrier, len(peers))


def _pair_exchange(gs, name, barrier_id):
    n = len(gs)
    halves = [g.shape[1] // 2 for g in gs]

    def body(*refs):
        g, got = refs[:n], refs[n:2 * n]
        send_sems, recv_sems = refs[2 * n:]
        _handshake(_sibling_peer())
        x, y, c, _ = _place()
        cps = [_remote(g[w].at[:, pl.ds((1 - c) * halves[w], halves[w]), :], got[w], send_sems, recv_sems, w,
                       (x, y, 1 - c)) for w in range(n)]
        for cp in cps:
            cp.start()
        for cp in cps:
            cp.wait()

    return pl.pallas_call(
        body, name=name, in_specs=[ANY] * n, out_specs=[ANY] * n,
        out_shape=[jax.ShapeDtypeStruct((N_CHIPS, h, g.shape[2]), g.dtype) for g, h in zip(gs, halves)],
        scratch_shapes=[pltpu.SemaphoreType.DMA((n,)), pltpu.SemaphoreType.DMA((n,))],
        compiler_params=pltpu.CompilerParams(collective_id=barrier_id),
    )(*gs)


def _core_id():
    return lax.axis_index("c").reshape(1).astype(jnp.int32)


def _pair_sum(gs, gots, name):
    n = len(gs)

    def body(c_ref, *refs):
        for g_ref, b_ref, o_ref in zip(refs[:n], refs[n:2 * n], refs[2 * n:]):
            o_ref[...] = (g_ref[...].astype(F32) + b_ref[...].astype(F32)).astype(BF16)

    mine = lambda got: pl.BlockSpec((1,) + got.shape[1:], lambda k, c_ref: (k, c_ref[0], 0))
    blk = lambda got: pl.BlockSpec((1,) + got.shape[1:], lambda k, c_ref: (k, 0, 0))
    return pl.pallas_call(
        body, name=name,
        grid_spec=pltpu.PrefetchScalarGridSpec(
            num_scalar_prefetch=1, grid=(N_CHIPS,),
            in_specs=[mine(got) for got in gots] + [blk(got) for got in gots], out_specs=[blk(got) for got in gots]),
        out_shape=[jax.ShapeDtypeStruct(got.shape, BF16) for got in gots],
        compiler_params=_cp("arbitrary"))(_core_id(), *gs, *gots)


def _sum_partials(gs, gots, landeds, name):
    n = len(gs)

    def body(ids, *refs):
        for g_ref, b_ref, l_ref, o_ref in zip(refs[:n], refs[n:2 * n], refs[2 * n:3 * n], refs[3 * n:]):
            acc = g_ref[0].astype(F32) + b_ref[0].astype(F32)
            for j in range(3):
                acc = acc + l_ref[j].astype(F32)
            o_ref[...] = acc

    ids = jnp.stack([2 * lax.axis_index("x") + lax.axis_index("y"), lax.axis_index("c")]).astype(jnp.int32)
    shp = [got.shape[1:] for got in gots]
    return pl.pallas_call(
        body, name=name,
        grid_spec=pltpu.PrefetchScalarGridSpec(
            num_scalar_prefetch=1, grid=(1,),
            in_specs=[pl.BlockSpec((1,) + s, lambda i, ids: (ids[0], ids[1], 0)) for s in shp]
            + [pl.BlockSpec((1,) + s, lambda i, ids: (ids[0], 0, 0)) for s in shp]
            + [pl.BlockSpec((3,) + s, lambda i, ids: (0, 0, 0)) for s in shp],
            out_specs=[pl.BlockSpec(s, lambda i, ids: (ids[1], 0)) for s in shp]),
        out_shape=[jax.ShapeDtypeStruct((2 * s[0], s[1]), F32) for s in shp],
        compiler_params=_cp("arbitrary"))(ids, *gs, *gots, *landeds)


def _pair_share(reds, name, barrier_id):
    n = len(reds)

    def body(*refs):
        out = refs[n:2 * n]
        send_sems, recv_sems = refs[2 * n:]
        _handshake(_sibling_peer())
        x, y, c, _ = _place()
        def half(w, which):
            h = out[w].shape[0] // 2
            return out[w].at[pl.ds(which * h, h), :]

        cps = [_remote(half(w, c), half(w, c), send_sems, recv_sems, w, (x, y, 1 - c)) for w in range(n)]
        for cp in cps:
            cp.start()
        for w in range(n):
            _remote(half(w, 1 - c), half(w, 1 - c), send_sems, recv_sems, w, (x, y, 1 - c)).wait_recv()
        for cp in cps:
            cp.wait_send()

    return pl.pallas_call(
        body, name=name, in_specs=[ANY] * n, out_specs=[ANY] * n,
        out_shape=[jax.ShapeDtypeStruct(r.shape, F32) for r in reds],
        input_output_aliases={w: w for w in range(n)},
        scratch_shapes=[pltpu.SemaphoreType.DMA((n,)), pltpu.SemaphoreType.DMA((n,))],
        compiler_params=pltpu.CompilerParams(collective_id=barrier_id),
    )(*reds)


HBM = pl.BlockSpec(memory_space=pltpu.HBM)
SEM = pl.BlockSpec(memory_space=pltpu.SEMAPHORE)
DATAFLOW = pltpu.SideEffectType.DATAFLOW_SIDE_EFFECTING


def _copies_start(name, srcs, lands, plan, n_copies, after, peers, barrier_id):
    ns, nb, na = len(srcs), len(srcs) + len(lands), len(after)

    def body(*refs):
        src_refs, land_refs = refs[:ns], refs[ns:nb]
        send_sems, recv_sems = refs[nb + na:nb + na + 2]
        token = refs[-1]
        _handshake(peers())
        for k, (src, there, _, to) in enumerate(plan(src_refs, land_refs)):
            _remote(src, there, send_sems, recv_sems, k, to).start()
        token[...] = jnp.zeros_like(token)

    hbm = lambda a: pltpu.HBM(a.shape, a.dtype)
    outs = pl.pallas_call(
        body, name=name,
        out_shape=(pltpu.SemaphoreType.DMA((n_copies,)), pltpu.SemaphoreType.DMA((n_copies,)),
                   *[hbm(a) for a in srcs], *[hbm(a) for a in lands], jax.ShapeDtypeStruct((8, 128), F32)),
        in_specs=[HBM] * nb + [ANY] * na,
        out_specs=(SEM, SEM, *[HBM] * nb, pl.BlockSpec(memory_space=pltpu.VMEM)),
        input_output_aliases={i: 2 + i for i in range(nb)},
        compiler_params=pltpu.CompilerParams(has_side_effects=DATAFLOW, collective_id=barrier_id),
    )(*[pltpu.with_memory_space_constraint(a, pltpu.HBM) for a in (*srcs, *lands)], *after)
    return outs[0], outs[1], outs[2:2 + ns], outs[2 + ns:2 + nb], outs[-1]


def _copies_wait(name, send_sems, recv_sems, srcs, lands, plan, after):
    ns, nb, na = len(srcs), len(srcs) + len(lands), len(after)

    def body(*refs):
        src_refs, land_refs = refs[:ns], refs[ns:nb]
        send_sems, recv_sems = refs[nb:nb + 2]
        for k, (src, _, here, to) in enumerate(plan(src_refs, land_refs)):
            cp = _remote(src, here, send_sems, recv_sems, k, to)
            cp.wait_send()
            cp.wait_recv()

    hbm = lambda a: pltpu.HBM(a.shape, a.dtype)
    outs = pl.pallas_call(
        body, name=name,
        out_shape=(*[hbm(a) for a in srcs], *[hbm(a) for a in lands]),
        in_specs=[HBM] * nb + [SEM, SEM] + [ANY] * na,
        out_specs=tuple([HBM] * nb),
        input_output_aliases={i: i for i in range(nb)},
        compiler_params=pltpu.CompilerParams(has_side_effects=DATAFLOW),
    )(*srcs, *lands, send_sems, recv_sems, *after)
    return outs[:ns], outs[ns:]


def _gather_plan(halves):
    def plan(shards, lands):
        x, y, c, chips = _place()
        me = 2 * x + y
        copies = []
        for w, h in enumerate(halves):
            rows = pl.ds(c * h, h)
            for chip in chips:
                copies.append((shards[w].at[rows, :], lands[w].at[me, rows, :],
                               lands[w].at[2 * chip[0] + chip[1], rows, :], (*chip, c)))
        return copies
    return plan


def _reduce_plan(n):
    def plan(ps, lands):
        x, y, c, chips = _place()
        return [(ps[w].at[2 * chip[0] + chip[1]], lands[w].at[j], lands[w].at[j], (*chip, c))
                for w in range(n) for j, chip in enumerate(chips)]
    return plan


def _forward_plan(halves):
    def plan(_, lands):
        x, y, c, chips = _place()

        def part(w, chip, half):
            return lands[w].at[2 * chip[0] + chip[1], pl.ds(half * halves[w], halves[w]), :]

        return [(part(w, chip, c), part(w, chip, c), part(w, chip, 1 - c), (x, y, 1 - c))
                for w in range(len(halves)) for chip in chips]
    return plan


def _pair_plan(halves):
    def plan(gs, gots):
        x, y, c, _ = _place()
        return [(gs[w].at[:, pl.ds((1 - c) * h, h), :], gots[w], gots[w], (x, y, 1 - c)) for w, h in enumerate(halves)]
    return plan


def _place_own(gathered, shards):
    chip = 2 * lax.axis_index("x") + lax.axis_index("y")
    return [lax.dynamic_update_slice(o, s[None], (chip, 0, 0)) for o, s in zip(gathered, shards)]


ADAMW_STEPS = 4


def _to_bf16(ws):
    def body(*refs):
        for src, dst in zip(refs[:len(ws)], refs[len(ws):]):
            dst[...] = src[...].astype(BF16)

    blk = lambda a: pl.BlockSpec((a.shape[0] // ADAMW_STEPS, a.shape[1]), lambda i: (i, 0))
    return pl.pallas_call(
        body, name="to_bf16", grid=(ADAMW_STEPS,), in_specs=[blk(a) for a in ws], out_specs=[blk(a) for a in ws],
        out_shape=[jax.ShapeDtypeStruct(a.shape, BF16) for a in ws], compiler_params=_cp("arbitrary"))(*ws)


def _adamw(ws, gs, ms, vs, name):
    n = len(ws)

    def body(*refs):
        ins, outs = refs[:4 * n], refs[4 * n:]
        for k in range(n):
            gv = ins[n + k][...]
            outs[4 * k][...] = gv
            outs[4 * k + 1][...], outs[4 * k + 2][...], outs[4 * k + 3][...] = _adamw_math(
                ins[k][...], gv, ins[2 * n + k][...], ins[3 * n + k][...])

    blk = lambda a: pl.BlockSpec((a.shape[0] // ADAMW_STEPS, a.shape[1]), lambda i: (i, 0))
    outs = pl.pallas_call(
        body, name=name, grid=(ADAMW_STEPS,), in_specs=[blk(a) for a in ws] * 4,
        out_specs=[blk(a) for a in ws for _ in range(4)],
        out_shape=[jax.ShapeDtypeStruct(a.shape, F32) for a in ws for _ in range(4)],
        compiler_params=_cp("arbitrary"))(*ws, *gs, *ms, *vs)
    return [outs[4 * k:4 * k + 4] for k in range(n)]


SMALL = (("norm1_g", 1, 1024), ("attn_norm_g", 1, 512), ("hgrn_norm_g", 1, 512), ("hgrn_lb_logits", 2, 512),
         ("norm2_g", 1, 1024), ("conv_b", 1, D_FF), ("final_norm_g", 1, 1024), ("conv_w", 3, D_FF))
LOSS_ROW = sum(r * c for _, r, c in SMALL) // 128
SMALL_ROWS = 136


def _rows_to_lanes(ref, row, width):
    return jnp.concatenate([ref[row + j:row + j + 1, :] for j in range(width // 128)], axis=1)


def _pack_small(grads, dlb, lb, loss):
    def body(*refs):
        parts, dlb_ref, lb_ref, loss_ref, out = refs[:len(SMALL) - 1], refs[-4], refs[-3], refs[-2], refs[-1]
        out[...] = jnp.zeros_like(out)
        lbv = lb_ref[...]
        dl = dlb_ref[...] * lbv * (1.0 - lbv)
        row = 0
        parts = list(parts)
        for name, rows, width in SMALL:
            for r in range(rows):
                if name == "hgrn_lb_logits":
                    src = dl if r == 0 else -dl
                    for j in range(width // 128):
                        out[row + j:row + j + 1, :] = src[:, 128 * j:128 * (j + 1)]
                else:
                    for j in range(width // 128):
                        out[row + j:row + j + 1, :] = parts[0][r:r + 1, 128 * j:128 * (j + 1)]
                row += width // 128
            if name != "hgrn_lb_logits":
                parts.pop(0)
        out[LOSS_ROW:LOSS_ROW + 1, :] = loss_ref[...]

    vm = pl.BlockSpec(memory_space=pltpu.VMEM)
    return pl.pallas_call(body, name="pack_small", in_specs=[vm] * (len(grads) + 3), out_specs=vm,
                          out_shape=jax.ShapeDtypeStruct((SMALL_ROWS, 128), F32))(*grads, dlb, lb, loss)


def _adamw_math(w, g, m, v):
    nm = ADAM_B1 * m + (1.0 - ADAM_B1) * g
    nv = ADAM_B2 * v + (1.0 - ADAM_B2) * (g * g)
    m_hat = nm / (1.0 - ADAM_B1 ** ADAM_STEP)
    v_hat = nv / (1.0 - ADAM_B2 ** ADAM_STEP)
    return -ADAM_LR * (m_hat / (jnp.sqrt(v_hat) + ADAM_EPS) + ADAM_WD * w), nm, nv


def _small_update(summed, g_conv_w, ws, ms, vs):
    n = len(SMALL)

    def body(*refs):
        s_ref, gcw_ref = refs[:2]
        w_refs, m_refs, v_refs = refs[2:2 + n], refs[2 + n:2 + 2 * n], refs[2 + 2 * n:2 + 3 * n]
        outs = refs[2 + 3 * n:]
        row = 0
        for k, (name, rows, width) in enumerate(SMALL):
            if name == "conv_w":
                g = gcw_ref[...]
            else:
                g = jnp.concatenate([_rows_to_lanes(s_ref, row + r * (width // 128), width) for r in range(rows)], axis=0)
            row += rows * (width // 128)
            d, nm, nv = _adamw_math(w_refs[k][...], g, m_refs[k][...], v_refs[k][...])
            for o, val in zip(outs[4 * k:4 * k + 4], (g, d, nm, nv)):
                o[...] = val

    vm = pl.BlockSpec(memory_space=pltpu.VMEM)
    outs = pl.pallas_call(
        body, name="small_update", in_specs=[vm] * (2 + 3 * n), out_specs=[vm] * (4 * n),
        out_shape=[jax.ShapeDtypeStruct(a.shape, F32) for a in ws for _ in range(4)],
    )(summed, g_conv_w, *ws, *ms, *vs)
    return [outs[4 * k:4 * k + 4] for k in range(n)]


def kernel(x, norm1_g, w_in, attn_norm_g, hgrn_norm_g, hgrn_lb_logits, w_out, norm2_g, w_up, conv_w, conv_b, w_down, final_norm_g, loss_target, m_norm1_g, m_w_in, m_attn_norm_g, m_hgrn_norm_g, m_hgrn_lb_logits, m_w_out, m_norm2_g, m_w_up, m_conv_w, m_conv_b, m_w_down, m_final_norm_g, v_norm1_g, v_w_in, v_attn_norm_g, v_hgrn_norm_g, v_hgrn_lb_logits, v_w_out, v_norm2_g, v_w_up, v_conv_w, v_conv_b, v_w_down, v_final_norm_g):
    w = dict(norm1_g=norm1_g, w_in=w_in, attn_norm_g=attn_norm_g, hgrn_norm_g=hgrn_norm_g,
             hgrn_lb_logits=hgrn_lb_logits, w_out=w_out, norm2_g=norm2_g, w_up=w_up, conv_w=conv_w, conv_b=conv_b,
             w_down=w_down, final_norm_g=final_norm_g)
    m = dict(norm1_g=m_norm1_g, w_in=m_w_in, attn_norm_g=m_attn_norm_g, hgrn_norm_g=m_hgrn_norm_g,
             hgrn_lb_logits=m_hgrn_lb_logits, w_out=m_w_out, norm2_g=m_norm2_g, w_up=m_w_up, conv_w=m_conv_w,
             conv_b=m_conv_b, w_down=m_w_down, final_norm_g=m_final_norm_g)
    v = dict(norm1_g=v_norm1_g, w_in=v_w_in, attn_norm_g=v_attn_norm_g, hgrn_norm_g=v_hgrn_norm_g,
             hgrn_lb_logits=v_hgrn_lb_logits, w_out=v_w_out, norm2_g=v_norm2_g, w_up=v_w_up, conv_w=v_conv_w,
             conv_b=v_conv_b, w_down=v_w_down, final_norm_g=v_final_norm_g)
    names = list(w)
    chip = 2 * lax.axis_index("x") + lax.axis_index("y")

    shards = dict(zip(BIG, _to_bf16([w[k][0] for k in BIG])))
    w_in4, conv_w4 = _gather_weights([shards["w_in"]], conv_w[0])
    conv_w_full = jnp.transpose(conv_w4, (1, 0, 2)).reshape(3, D_FF)
    lb = jax.nn.softmax(hgrn_lb_logits, axis=0)[0:1]
    late = [shards[k] for k in BIG[1:]]
    gather_plan = _gather_plan([s.shape[0] // 2 for s in late])
    started = _copies_start("gather_start", late, [lax.empty((N_CHIPS,) + s.shape, BF16) for s in late], gather_plan,
                            3 * len(late), after=(w_in4,), peers=_chip_peers, barrier_id=0)
    u1, qkv, hg = _in_proj(x[0], norm1_g + started[4][0:1, 0:1], w_in4)
    attn_o, lse = _attn_fwd(qkv)
    late, landed_w = _copies_wait("gather_wait", *started[:4], gather_plan, after=(attn_o,))
    forward_plan = _forward_plan([s.shape[0] // 2 for s in late])
    started = _copies_start("forward_start", [], landed_w, forward_plan, 3 * len(late), after=(),
                            peers=_sibling_peer, barrier_id=1)
    rec_o, states = _hgrn_fwd(hg, lb + started[4][0:1, 0:1])
    a = dict(u1=u1, qkv=qkv, hg=hg, attn_o=attn_o, lse=lse, rec_o=rec_o, states=states)
    w_out4, w_up4, w_down4 = _place_own(
        _copies_wait("forward_wait", *started[:4], forward_plan, after=(rec_o,))[1], late)

    b = _step_channel(a, x[0], loss_target[0], attn_norm_g, hgrn_norm_g, w_out4.reshape(D_MODEL, D_MODEL), norm2_g,
                      w_up4, conv_w_full, conv_b, w_down4.reshape(D_FF, D_MODEL), final_norm_g.reshape(1, D_MODEL))

    early = [b["dw_out"], b["dw_up"], b["dw_down"]]
    pair_plan = _pair_plan([gk.shape[1] // 2 for gk in early])
    started = _copies_start("pair_start", early,
                            [lax.empty((N_CHIPS, gk.shape[1] // 2, gk.shape[2]), BF16) for gk in early], pair_plan,
                            len(early), after=(), peers=_sibling_peer, barrier_id=2)
    dqkv = _attn_bwd(qkv, attn_o, lse, b["da"], started[4])
    early, gots = _copies_wait("pair_wait", *started[:4], pair_plan, after=(dqkv[0],))
    ps = _pair_sum(early, gots, "pair_sum")
    reduce_plan = _reduce_plan(len(ps))
    started = _copies_start("reduce_start", ps, [lax.empty((3,) + p.shape[1:], BF16) for p in ps], reduce_plan,
                            3 * len(ps), after=(), peers=_chip_peers, barrier_id=3)
    c = _step_mixers_bwd(a, b, x[0], norm1_g, w_in4, lb + started[4][0:1, 0:1], dqkv)
    gots_in = _pair_exchange([c["dw_in"]], "pair_exchange_w_in", barrier_id=4)
    ps_in = _pair_sum([c["dw_in"]], gots_in, "pair_sum_w_in")[0]
    plan_in = _reduce_plan(1)
    started_in = _copies_start("reduce_start_w_in", [ps_in], [lax.empty((3,) + ps_in.shape[1:], BF16)], plan_in, 3,
                               after=(), peers=_chip_peers, barrier_id=5)
    landed = _copies_wait("reduce_wait", *started[:4], reduce_plan, after=(started_in[4],))[1]
    reds = _sum_partials(early, gots, landed, "sum_partials")
    g = dict(zip(BIG[1:], _pair_share(reds, "pair_share", barrier_id=6)))
    delta, new_m, new_v = {}, {}, {}
    shard = lambda p: [p[k][0] for k in BIG[1:]]
    for k, parts in zip(BIG[1:], _adamw(shard(w), [g[k] for k in BIG[1:]], shard(m), shard(v), "adamw")):
        g[k], delta[k], new_m[k], new_v[k] = parts

    loss, dx = b["loss"], c["dx"]
    small = dict(g1=c["dg1"], g_a=b["dga"], g_h=b["dgh"], lb=c["dlb"], g2=b["dg2"], conv_w=b["dcw"], conv_b=b["dcb"],
                 gf=b["dgf"])
    summed = _allreduce_small(_pack_small(
        [small["g1"], small["g_a"], small["g_h"], small["g2"], small["conv_b"], small["gf"], small["conv_w"]],
        small["lb"], lb, loss))
    loss_total = summed[LOSS_ROW, 0]
    g_conv_w = lax.dynamic_slice(summed[LOSS_ROW - 3 * D_FF // 128:LOSS_ROW].reshape(3, D_FF),
                                 (0, chip * (D_FF // N_CHIPS)), (3, D_FF // N_CHIPS))
    two_d = lambda p, k: p[k].reshape(-1, p[k].shape[-1])
    updated = _small_update(summed, g_conv_w, *[[two_d(p, k) for k, _, _ in SMALL] for p in (w, m, v)])
    for (k, _, _), parts in zip(SMALL, updated):
        g[k], delta[k], new_m[k], new_v[k] = (a.reshape(w[k].shape) for a in parts)

    landed_in = _copies_wait("reduce_wait_w_in", *started_in[:4], plan_in, after=(updated[0][1], delta["w_up"]))[1]
    red_in = _sum_partials([c["dw_in"]], gots_in, landed_in, "sum_partials_w_in")
    g["w_in"] = _pair_share(red_in, "pair_share_w_in", barrier_id=7)[0]
    g["w_in"], delta["w_in"], new_m["w_in"], new_v["w_in"] = _adamw([w_in[0]], [g["w_in"]], [m_w_in[0]], [v_w_in[0]],
                                                                    "adamw_w_in")[0]
    for k in BIG:
        g[k], delta[k], new_m[k], new_v[k] = g[k][None], delta[k][None], new_m[k][None], new_v[k][None]

    return (loss_total, dx[None], *[g[k] for k in names], *[delta[k] for k in names],
            *[new_m[k] for k in names], *[new_v[k] for k in names])
```

```python
import math

import jax
import jax.numpy as jnp
from jax import lax
from jax.experimental import pallas as pl
from jax.experimental.pallas import tpu as pltpu

F32 = jnp.float32
BF16 = jnp.bfloat16

D_MODEL = 1024
ATTN_W = 512
HGRN_W = 512
HEAD_PAIR = 128
ATTN_BLK = 128
DILATIONS = (1, 4, 16)
ATTN_CHAINS = 4
ATTN_CHAINS_FWD = 8
HGRN_HEADS = 4
HGRN_DIM = 128
HGRN_CHUNK = 64
SUPER = 256
HGRN_SIDE = 8
D_FF = 2816
FF_CHUNKS = ((0, 1536), (1536, D_FF))
MLP_BWD_CHUNKS = tuple((lo, min(lo + 512, D_FF)) for lo in range(0, D_FF, 512))
N_CHIPS = 4
IN_TOTAL = 3584
IN_SHARD = IN_TOTAL // N_CHIPS
UP_SHARD = 2 * D_FF // N_CHIPS
QKV_W = 3 * ATTN_W
HG_W = 4 * HGRN_W
EPS = 1e-6
NEG = -1e30
V7X_VMEM_BYTES = 64 * 1024 * 1024
VMEM_LIMIT = V7X_VMEM_BYTES - 8 * 1024 * 1024

ADAM_LR = 0.001
ADAM_B1 = 0.9
ADAM_B2 = 0.999
ADAM_EPS = 1e-08
ADAM_WD = 0.01
ADAM_STEP = 10

MESH = pl.DeviceIdType.MESH


def _cp(*sem):
    return pltpu.CompilerParams(dimension_semantics=sem or None, vmem_limit_bytes=VMEM_LIMIT)


def _dot(a, b):
    return jnp.dot(a, b, preferred_element_type=F32)


def _dot_nt(a, b):
    return lax.dot_general(a, b, (((1,), (1,)), ((), ())), preferred_element_type=F32)


def _dot_tn(a, b):
    return lax.dot_general(a, b, (((0,), (0,)), ((), ())), preferred_element_type=F32)


def _sigmoid(x):
    return 1.0 / (1.0 + jnp.exp(-x))


def _rms(x, width):
    return lax.rsqrt(jnp.sum(x * x, axis=-1, keepdims=True) * (1.0 / width) + EPS)


def _rms_bwd(dn, n, r, width):
    return r * (dn - n * (jnp.sum(dn * n, axis=-1, keepdims=True) * (1.0 / width)))


def _colsum(x):
    return jnp.sum(x, axis=0, keepdims=True)


def _row(v, k):
    rid = lax.broadcasted_iota(jnp.int32, v.shape, 0)
    return jnp.sum(jnp.where(rid == k, v, 0.0), axis=0, keepdims=True)


def _full(shape):
    return pl.BlockSpec(shape, lambda *_: (0,) * len(shape))


def _once(shape):
    return pl.BlockSpec(shape, lambda *_: (0,) * len(shape), pipeline_mode=pl.Buffered(1))


def _load_side_by_side(w_hbm, w_full, sem):
    width = w_hbm.shape[2]
    cps = [pltpu.make_async_copy(w_hbm.at[k], w_full.at[:, pl.ds(k * width, width)], sem.at[k]) for k in range(N_CHIPS)]
    for cp in cps:
        cp.start()
    for cp in cps:
        cp.wait()


def _in_proj(x, g1, w_in4, tm=512):
    T = x.shape[0]

    def body(x_ref, g_ref, w_hbm, u_ref, qkv_ref, hg_ref, w_full, sem):
        @pl.when(pl.program_id(0) == 0)
        def _():
            _load_side_by_side(w_hbm, w_full, sem)

        xv = x_ref[...]
        u = (xv * _rms(xv, D_MODEL) * g_ref[...]).astype(BF16)
        u_ref[...] = u
        p = _dot(u, w_full[...])
        qkv_ref[...] = p[:, :QKV_W]
        hg_ref[...] = p[:, QKV_W:]

    return pl.pallas_call(
        body, name="in_proj", grid=(T // tm,),
        in_specs=[pl.BlockSpec((tm, D_MODEL), lambda i: (i, 0)), _full((1, D_MODEL)), ANY],
        out_specs=[pl.BlockSpec((tm, D_MODEL), lambda i: (i, 0)), pl.BlockSpec((tm, QKV_W), lambda i: (i, 0)),
                   pl.BlockSpec((tm, HG_W), lambda i: (i, 0))],
        out_shape=[jax.ShapeDtypeStruct((T, D_MODEL), BF16), jax.ShapeDtypeStruct((T, QKV_W), F32),
                   jax.ShapeDtypeStruct((T, HG_W), F32)],
        scratch_shapes=[pltpu.VMEM((D_MODEL, IN_TOTAL), BF16), pltpu.SemaphoreType.DMA((N_CHIPS,))],
        compiler_params=_cp("arbitrary"),
    )(x, g1, w_in4)


def _attn_masks(bias_ref):
    lane = lax.broadcasted_iota(jnp.int32, (ATTN_BLK, HEAD_PAIR), 1)
    row = lax.broadcasted_iota(jnp.int32, (2 * ATTN_BLK, 2 * ATTN_BLK), 0)
    col = lax.broadcasted_iota(jnp.int32, (2 * ATTN_BLK, 2 * ATTN_BLK), 1)
    base = jnp.where(row >= ATTN_BLK, row - ATTN_BLK, row) - col
    for k in range(2):
        dist = base + k * ATTN_BLK
        bias_ref[k] = jnp.where((dist >= 0) & (dist <= ATTN_BLK), 0.0, NEG)
    bias_ref[2] = jnp.where(col >= ATTN_BLK, bias_ref[1], NEG)
    return lane < 64


def _two_heads(blk, first):
    zero = jnp.zeros_like(blk)
    return jnp.concatenate([jnp.where(first, blk, zero), jnp.where(first, zero, blk)], axis=0)


def _attn_rows(idx, nb, d):
    r, n = idx // nb, idx % nb
    kb = jnp.maximum(n - 1, 0)
    if d == 1:
        q0 = pl.multiple_of(n * ATTN_BLK, ATTN_BLK)
        k0 = pl.multiple_of(kb * ATTN_BLK, ATTN_BLK)
        return pl.ds(q0, ATTN_BLK), pl.ds(k0, 2 * ATTN_BLK), n - kb
    return (pl.ds(r + d * ATTN_BLK * n, ATTN_BLK, stride=d), pl.ds(r + d * ATTN_BLK * kb, 2 * ATTN_BLK, stride=d),
            n - kb)


def _attn_fwd(qkv):
    T = qkv.shape[0]

    n_blocks = T // ATTN_BLK

    def body(q_ref, k_ref, v_ref, o_ref, m_ref, l_ref, bias_ref):
        first = _attn_masks(bias_ref)
        for bi, d in enumerate(DILATIONS):
            nb = T // d // ATTN_BLK

            chains = ATTN_CHAINS_FWD
            per_chain = n_blocks // chains
            carried = d > 1 and per_chain % nb == 0

            def block(idx, kept=None, d=d, nb=nb, bi=bi, carried=carried):
                rows, keys, which = _attn_rows(idx, nb, d)
                q2 = _two_heads(q_ref[rows, :] * 0.125, first).astype(BF16)
                if carried:
                    k_own, v_own = k_ref[rows, :].astype(BF16), v_ref[rows, :].astype(BF16)
                    kw = jnp.concatenate([kept[0], k_own], axis=0)
                    vw = jnp.concatenate([kept[1], v_own], axis=0)
                    which = 2 - which
                else:
                    kw = k_ref[keys, :].astype(BF16)
                    vw = v_ref[keys, :].astype(BF16)
                old = (o_ref[rows, :], m_ref[rows, :], l_ref[rows, :]) if bi else None
                s = _dot_nt(q2, kw) + bias_ref[which]
                mb = jnp.max(s, axis=-1, keepdims=True)
                p = jnp.exp(s - mb)
                lb = jnp.sum(p, axis=-1, keepdims=True)
                o2 = _dot(p.astype(BF16), vw)
                o = jnp.where(first, o2[:ATTN_BLK], o2[ATTN_BLK:])
                m = jnp.where(first, mb[:ATTN_BLK], mb[ATTN_BLK:])
                l = jnp.where(first, lb[:ATTN_BLK], lb[ATTN_BLK:])
                if bi:
                    po, pm, pl_ = old
                    mn = jnp.maximum(pm, m)
                    wa = jnp.exp(pm - mn)
                    wb = jnp.exp(m - mn)
                    o, l, m = po * wa + o * wb, pl_ * wa + l * wb, mn
                return (rows, o, m, l), ((k_own, v_own) if carried else 0)

            def step(i, kept, block=block, carried=carried, chains=chains, per_chain=per_chain):
                done = [block(i + ch * per_chain, kept[ch] if carried else None) for ch in range(chains)]
                for (rows, o, m, l), _ in done:
                    o_ref[rows, :] = o
                    m_ref[rows, :] = m
                    l_ref[rows, :] = l
                return tuple(k for _, k in done) if carried else kept

            zero = jnp.zeros((ATTN_BLK, HEAD_PAIR), BF16)
            lax.fori_loop(0, per_chain, step, ((zero, zero),) * chains if carried else 0)

        def finish(i, carry):
            rows = pl.ds(pl.multiple_of(i * SUPER, SUPER), SUPER)
            l = l_ref[rows, :]
            o_ref[rows, :] = o_ref[rows, :] / l
            m_ref[rows, :] = m_ref[rows, :] + jnp.log(l)
            return carry

        lax.fori_loop(0, T // SUPER, finish, 0)

    col = lambda off: pl.BlockSpec((T, HEAD_PAIR), lambda j: (0, off + j))
    return pl.pallas_call(
        body, name="attn_fwd", grid=(4,),
        in_specs=[col(0), col(4), col(8)], out_specs=[col(0), col(0)],
        out_shape=[jax.ShapeDtypeStruct((T, ATTN_W), F32)] * 2,
        scratch_shapes=[pltpu.VMEM((T, HEAD_PAIR), F32), pltpu.VMEM((3, 2 * ATTN_BLK, 2 * ATTN_BLK), F32)],
        compiler_params=_cp("arbitrary"),
    )(qkv, qkv, qkv)


def _attn_bwd(qkv, o, lse, do, token=None):
    T = qkv.shape[0]
    per_chain = T // ATTN_BLK // ATTN_CHAINS
    extra = [] if token is None else [token]

    def body(q_ref, k_ref, v_ref, o_ref, lse_ref, do_ref, *rest):
        outs = rest[len(extra):len(extra) + 3]
        dq_ref, dk_ref, dv_ref, dkb_ref, dvb_ref, bias_ref = rest[len(extra) + 3:]
        first = _attn_masks(bias_ref)
        dq_ref[...] = jnp.zeros_like(dq_ref)
        dk_ref[...] = jnp.zeros_like(dk_ref)
        dv_ref[...] = jnp.zeros_like(dv_ref)

        def grads(rows, kw, vw, which):
            q2 = _two_heads(q_ref[rows, :] * 0.125, first).astype(BF16)
            lse_b = lse_ref[rows, :]
            dob = do_ref[rows, :]
            prod = dob * o_ref[rows, :]
            old = dq_ref[rows, :]
            lse2 = jnp.concatenate(
                [jnp.max(jnp.where(first, lse_b, NEG), axis=-1, keepdims=True),
                 jnp.max(jnp.where(first, NEG, lse_b), axis=-1, keepdims=True)], axis=0)
            p = jnp.exp(_dot_nt(q2, kw) + (bias_ref[which] - lse2))
            delta = jnp.concatenate(
                [jnp.sum(jnp.where(first, prod, 0.0), axis=-1, keepdims=True),
                 jnp.sum(jnp.where(first, 0.0, prod), axis=-1, keepdims=True)], axis=0)
            do2 = _two_heads(dob, first).astype(BF16)
            ds = (p * (_dot_nt(do2, vw) - delta)).astype(BF16)
            dq2 = _dot(ds, kw) * 0.125
            return (old + jnp.where(first, dq2[:ATTN_BLK], dq2[ATTN_BLK:]), _dot_tn(ds, q2),
                    _dot_tn(p.astype(BF16), do2))

        def block(idx):
            rows, keys, which = _attn_rows(idx, T // ATTN_BLK, 1)
            old = dk_ref[keys, :], dv_ref[keys, :]
            dq, ck, cv = grads(rows, k_ref[keys, :].astype(BF16), v_ref[keys, :].astype(BF16), which)
            return rows, keys, dq, old[0] + ck, old[1] + cv

        def step(i, carry):
            done = [block(i + ch * per_chain) for ch in range(ATTN_CHAINS)]
            for rows, keys, dq, dk, dv in done:
                dq_ref[rows, :] = dq
                dk_ref[keys, :] = dk
                dv_ref[keys, :] = dv
            return carry

        lax.fori_loop(0, per_chain, step, 0)

        for d in DILATIONS[1:]:
            nb = T // d // ATTN_BLK

            def block(idx, kept, d=d, nb=nb):
                r, n = idx // nb, idx % nb
                rows = pl.ds(r + d * ATTN_BLK * n, ATTN_BLK, stride=d)
                before = pl.ds(r + d * ATTN_BLK * jnp.maximum(n - 1, 0), ATTN_BLK, stride=d)
                k_prev, v_prev, dk_prev, dv_prev = kept
                k_own, v_own = k_ref[rows, :].astype(BF16), v_ref[rows, :].astype(BF16)
                dq, ck, cv = grads(rows, jnp.concatenate([k_prev, k_own], axis=0),
                                   jnp.concatenate([v_prev, v_own], axis=0), jnp.where(n > 0, 1, 2))
                stores = (rows, before, dq, dk_prev + ck[:ATTN_BLK], dv_prev + cv[:ATTN_BLK], ck[ATTN_BLK:], cv[ATTN_BLK:])
                return stores, (k_own, v_own, ck[ATTN_BLK:], cv[ATTN_BLK:])

            def step(i, kept, block=block):
                done = [block(i + ch * per_chain, kept[ch]) for ch in range(ATTN_CHAINS)]
                for (rows, before, dq, dk_done, dv_done, dk_own, dv_own), _ in done:
                    dq_ref[rows, :] = dq
                    dkb_ref[before, :] = dk_done
                    dvb_ref[before, :] = dv_done
                    dkb_ref[rows, :] = dk_own
                    dvb_ref[rows, :] = dv_own
                return tuple(k for _, k in done)

            zero = jnp.zeros((ATTN_BLK, HEAD_PAIR), F32)
            lax.fori_loop(0, per_chain, step, ((zero.astype(BF16), zero.astype(BF16), zero, zero),) * ATTN_CHAINS)

            def add(i, carry):
                rows = pl.ds(pl.multiple_of(i * SUPER, SUPER), SUPER)
                dk_ref[rows, :] += dkb_ref[rows, :]
                dv_ref[rows, :] += dvb_ref[rows, :]
                return carry

            lax.fori_loop(0, T // SUPER, add, 0)

        def emit(i, carry):
            rows = pl.ds(pl.multiple_of(i * SUPER, SUPER), SUPER)
            for out, acc in zip(outs, (dq_ref, dk_ref, dv_ref)):
                out[rows, :] = acc[rows, :].astype(BF16)
            return carry

        lax.fori_loop(0, T // SUPER, emit, 0)

    col = lambda off: pl.BlockSpec((T, HEAD_PAIR), lambda j: (0, off + j))
    return pl.pallas_call(
        body, name="attn_bwd", grid=(4,),
        in_specs=[col(0), col(4), col(8), col(0), col(0), col(0)] + [_full(t.shape) for t in extra],
        out_specs=[col(0)] * 3,
        out_shape=[jax.ShapeDtypeStruct((T, ATTN_W), BF16)] * 3,
        scratch_shapes=[pltpu.VMEM((T, HEAD_PAIR), F32)] * 5 + [pltpu.VMEM((3, 2 * ATTN_BLK, 2 * ATTN_BLK), F32)],
        compiler_params=_cp("arbitrary"),
    )(qkv, qkv, qkv, o, lse, do, *extra)


def _chunk_ids():
    row = lax.broadcasted_iota(jnp.int32, (SUPER, HGRN_DIM), 0)
    r2 = lax.broadcasted_iota(jnp.int32, (SUPER, SUPER), 0)
    c2 = lax.broadcasted_iota(jnp.int32, (SUPER, SUPER), 1)
    amask = ((r2 // HGRN_CHUNK) == (c2 // HGRN_CHUNK)) & (c2 <= r2)
    return row % HGRN_CHUNK, row // HGRN_CHUNK, amask


def _cumsum_chunk(x, rmod):
    s = 1
    while s < HGRN_CHUNK:
        x = x + jnp.where(rmod >= s, pltpu.roll(x, s, 0), 0.0)
        s *= 2
    return x


def _suffix_sum_chunk(x, rmod):
    s = 1
    while s < HGRN_CHUNK:
        x = x + jnp.where(rmod < HGRN_CHUNK - s, pltpu.roll(x, SUPER - s, 0), 0.0)
        s *= 2
    return x


def _chunk_rows(vs, cid):
    out = vs[-1]
    for c in reversed(range(len(vs) - 1)):
        out = jnp.where(cid == c, vs[c], out)
    return out


def _expand(x, cid):
    return jnp.concatenate([jnp.where(cid == c, x, 0.0) for c in range(SUPER // HGRN_CHUNK)], axis=1)


def _hgrn_gates(q, f, lbv, rmod, cid, tmp):
    sq = _sigmoid(q)
    sg = _sigmoid(f)
    forget = lbv + (1.0 - lbv) * sg
    key = 1.0 - forget
    b = _cumsum_chunk(jnp.log(forget), rmod)
    tmp[...] = b
    bends = [tmp[c * HGRN_CHUNK + HGRN_CHUNK - 1:(c + 1) * HGRN_CHUNK, :] for c in range(SUPER // HGRN_CHUNK)]
    eb = jnp.exp(b)
    enb = jnp.exp(-b)
    ebe = jnp.exp(_chunk_rows(bends, cid) - b)
    return sq, sg, forget, key, eb, enb, ebe, q * sq * eb, key * enb, key * ebe, [jnp.exp(v) for v in bends]


def _hgrn_fwd(hg, lb):
    T = hg.shape[0]
    nsc = T // SUPER
    NC = SUPER // HGRN_CHUNK

    def body(q_ref, f_ref, i_ref, lb_ref, o_ref, st_ref, state, tmp):
        rmod, cid, amask = _chunk_ids()
        state[...] = jnp.zeros_like(state)
        lbv = lb_ref[...]

        def local(sc, u):
            rows = pl.ds(pl.multiple_of(sc * SUPER, SUPER), SUPER)
            iv = i_ref[rows, :].astype(BF16)
            qd, ki, ke, dec = _hgrn_gates(q_ref[rows, :], f_ref[rows, :], lbv, rmod, cid, tmp.at[u])[-4:]
            a = jnp.where(amask, _dot_nt(qd.astype(BF16), ki.astype(BF16)), 0.0)
            return rows, qd, dec, _dot(a.astype(BF16), iv), _dot_tn(iv, _expand(ke, cid).astype(BF16))

        def step(i, carry):
            parts = [local(i * HGRN_SIDE + u, u) for u in range(HGRN_SIDE)]
            st = state[...]
            entering = []
            for u, (_, _, dec, _, ut) in enumerate(parts):
                st_ref[0, i * HGRN_SIDE + u] = st
                sts = []
                for c in range(NC):
                    sts.append(st)
                    st = st * dec[c] + ut[:, c * HGRN_DIM:(c + 1) * HGRN_DIM]
                entering.append(jnp.concatenate(sts, axis=1).astype(BF16))
            state[...] = st
            for (rows, qd, _, o, _), sts in zip(parts, entering):
                o_ref[rows, :] = o + _dot_nt(_expand(qd, cid).astype(BF16), sts)
            return carry

        lax.fori_loop(0, nsc // HGRN_SIDE, step, 0)

    col = lambda off: pl.BlockSpec((T, HGRN_DIM), lambda h: (0, off + h))
    return pl.pallas_call(
        body, name="hgrn_fwd", grid=(HGRN_HEADS,),
        in_specs=[col(0), col(4), col(8), pl.BlockSpec((1, HGRN_DIM), lambda h: (0, h))],
        out_specs=[pl.BlockSpec((T, HGRN_DIM), lambda h: (0, h)),
                   pl.BlockSpec((1, nsc, HGRN_DIM, HGRN_DIM), lambda h: (h, 0, 0, 0))],
        out_shape=[jax.ShapeDtypeStruct((T, HGRN_W), F32),
                   jax.ShapeDtypeStruct((HGRN_HEADS, nsc, HGRN_DIM, HGRN_DIM), F32)],
        scratch_shapes=[pltpu.VMEM((HGRN_DIM, HGRN_DIM), F32), pltpu.VMEM((HGRN_SIDE, SUPER, HGRN_DIM), F32)],
        compiler_params=_cp("arbitrary"),
    )(hg, hg, hg, lb)


def _hgrn_bwd(hg, lb, states, do):
    T = hg.shape[0]
    nsc = T // SUPER
    NC = SUPER // HGRN_CHUNK

    def body(q_ref, f_ref, i_ref, lb_ref, st_ref, do_ref, dq_ref, df_ref, di_ref, dlb_ref, dstate, tmp):
        rmod, cid, amask = _chunk_ids()
        dstate[...] = jnp.zeros_like(dstate)
        dlb_ref[...] = jnp.zeros_like(dlb_ref)
        lbv = lb_ref[...]

        def local(sc, u):
            rows = pl.ds(pl.multiple_of(sc * SUPER, SUPER), SUPER)
            q = q_ref[rows, :]
            ivf = i_ref[rows, :]
            iv = ivf.astype(BF16)
            dof = do_ref[rows, :]
            dob = dof.astype(BF16)
            sq, sg, forget, key, eb, enb, ebe, qd, ki, ke, dec = _hgrn_gates(q, f_ref[rows, :], lbv, rmod, cid,
                                                                            tmp.at[u])
            qdb, kib = qd.astype(BF16), ki.astype(BF16)
            keexp = _expand(ke, cid).astype(BF16)
            a = jnp.where(amask, _dot_nt(qdb, kib), 0.0).astype(BF16)
            ut = _dot_tn(iv, keexp)
            st = st_ref[0, sc]
            sts = []
            for c in range(NC):
                sts.append(st)
                st = st * dec[c] + ut[:, c * HGRN_DIM:(c + 1) * HGRN_DIM]
            gt = _dot_tn(dob, _expand(qd, cid).astype(BF16))
            da = jnp.where(amask, _dot_nt(dob, iv), 0.0).astype(BF16)
            ststack = jnp.concatenate(sts, axis=0).astype(BF16)
            return dict(rows=rows, q=q, sq=sq, sg=sg, forget=forget, eb=eb, enb=enb, ebe=ebe, qd=qd, ki=ki, ke=ke,
                        dec=dec, sts=sts, gt=gt, keexp=keexp, ivexp=_expand(ivf, cid).astype(BF16),
                        div=_dot_tn(a, dob), dki=_dot_tn(da, qdb),
                        dqd=_dot(da, kib) + _dot(_expand(dof, cid).astype(BF16), ststack))

        def finish(p, nxt, ddec):
            ncat = jnp.concatenate(nxt, axis=1).astype(BF16)
            nstack = jnp.concatenate(nxt, axis=0).astype(BF16)
            dke = _dot(p["ivexp"], nstack)
            dkk = dke * p["ke"]
            dkey = p["dki"] * p["enb"] + dke * p["ebe"]
            db = p["dqd"] * p["qd"] - p["dki"] * p["ki"] - dkk
            dbends = [_colsum(jnp.where(cid == c, dkk, 0.0)) + ddec[c] * p["dec"][c] for c in range(NC)]
            dforget = (_suffix_sum_chunk(db, rmod) + _chunk_rows(dbends, cid)) / p["forget"] - dkey
            sg, sq, q = p["sg"], p["sq"], p["q"]
            df_ref[p["rows"], :] = (dforget * (1.0 - lbv) * sg * (1.0 - sg)).astype(BF16)
            dq_ref[p["rows"], :] = (p["dqd"] * p["eb"] * (sq * (1.0 + q * (1.0 - sq)))).astype(BF16)
            di_ref[p["rows"], :] = (p["div"] + _dot_nt(p["keexp"], ncat)).astype(BF16)
            return _colsum(dforget * (1.0 - sg))

        def step(i, carry):
            parts = [local(nsc - 1 - (i * HGRN_SIDE + u), u) for u in range(HGRN_SIDE)]
            dst = dstate[...]
            chained = []
            for p in parts:
                nxt = [None] * NC
                ddec = [None] * NC
                for c in reversed(range(NC)):
                    nxt[c] = dst
                    ddec[c] = _colsum(dst * p["sts"][c])
                    dst = dst * p["dec"][c] + p["gt"][:, c * HGRN_DIM:(c + 1) * HGRN_DIM]
                chained.append((nxt, ddec))
            dstate[...] = dst
            dlb = dlb_ref[...]
            for p, (nxt, ddec) in zip(parts, chained):
                dlb = dlb + finish(p, nxt, ddec)
            dlb_ref[...] = dlb
            return carry

        lax.fori_loop(0, nsc // HGRN_SIDE, step, 0)

    col = lambda off: pl.BlockSpec((T, HGRN_DIM), lambda h: (0, off + h))
    own = pl.BlockSpec((T, HGRN_DIM), lambda h: (0, h))
    vec = pl.BlockSpec((1, HGRN_DIM), lambda h: (0, h))
    return pl.pallas_call(
        body, name="hgrn_bwd", grid=(HGRN_HEADS,),
        in_specs=[col(0), col(4), col(8), vec,
                  pl.BlockSpec((1, nsc, HGRN_DIM, HGRN_DIM), lambda h: (h, 0, 0, 0)), own],
        out_specs=[own, own, own, vec],
        out_shape=[jax.ShapeDtypeStruct((T, HGRN_W), BF16)] * 3 + [jax.ShapeDtypeStruct((1, HGRN_W), F32)],
        scratch_shapes=[pltpu.VMEM((HGRN_DIM, HGRN_DIM), F32), pltpu.VMEM((HGRN_SIDE, SUPER, HGRN_DIM), F32)],
        compiler_params=_cp("arbitrary"),
    )(hg, hg, hg, lb, states, do)


def _rec_heads(rec, gate, g_h):
    rr = jnp.concatenate(
        [jnp.broadcast_to(_rms(rec[:, h * HGRN_DIM:(h + 1) * HGRN_DIM], HGRN_DIM), (rec.shape[0], HGRN_DIM))
         for h in range(HGRN_HEADS)], axis=1)
    rn = rec * rr
    sg = _sigmoid(gate)
    return rr, rn, sg


_INV_SQRT2 = 1.0 / math.sqrt(2.0)
_INV_SQRT2PI = 1.0 / math.sqrt(2.0 * math.pi)


def _gelu(x):
    return 0.5 * x * (1.0 + lax.erf(x * _INV_SQRT2))


def _gelu_and_grad(x):
    z = x * _INV_SQRT2
    cdf = 0.5 * (1.0 + lax.erf(z))
    return x * cdf, cdf + (x * _INV_SQRT2PI) * jnp.exp(-(z * z))


def _shift_down(g, prev, rowid):
    p1 = _row(prev, prev.shape[0] - 1)
    p2 = _row(prev, prev.shape[0] - 2)
    s1 = jnp.where(rowid == 0, p1, pltpu.roll(g, 1, 0))
    s2 = jnp.where(rowid == 0, p2, jnp.where(rowid == 1, p1, pltpu.roll(g, 2, 0)))
    return s1, s2


def _mlp_fwd(attn_o, rec_o, hg, x, g_a, g_h, w_out, g2, w_up4, conv_w, conv_b, w_down, gf, tgt, tm=256):
    T = x.shape[0]

    def body(a_ref, r_ref, gt_ref, x_ref, ga_ref, gh_ref, wo_ref, g2_ref, wu_hbm, cw_ref, cb_ref, wd_ref, gf_ref, t_ref,
             h1_ref, mixed_ref, u_ref, gate_ref, val_ref, conv_ref, act_ref, dh_ref, loss_ref, dgf_ref,
             carry, wu_ref, sem):
        i = pl.program_id(0)

        @pl.when(i == 0)
        def _():
            carry[...] = jnp.zeros_like(carry)
            loss_ref[...] = jnp.zeros_like(loss_ref)
            dgf_ref[...] = jnp.zeros_like(dgf_ref)
            _load_side_by_side(wu_hbm, wu_ref, sem)

        a = a_ref[...]
        an = a * _rms(a, ATTN_W) * ga_ref[...]
        og = gt_ref[...]
        _, rn, sg = _rec_heads(r_ref[...], og, gh_ref[...])
        mixed = jnp.concatenate([an, rn * gh_ref[...] * (og * sg)], axis=1).astype(BF16)
        mixed_ref[...] = mixed
        h = x_ref[...] + _dot(mixed, wo_ref[...])
        h1_ref[...] = h
        u = (h * _rms(h, D_MODEL) * g2_ref[...]).astype(BF16)
        u_ref[...] = u
        y2 = jnp.zeros((tm, D_MODEL), F32)
        for lo, hi in FF_CHUNKS:
            cols = slice(lo, hi)
            rowid = lax.broadcasted_iota(jnp.int32, (tm, hi - lo), 0)
            gb = _dot(u, wu_ref[:, lo:hi]).astype(BF16)
            vb = _dot(u, wu_ref[:, D_FF + lo:D_FF + hi]).astype(BF16)
            gate_ref[:, cols] = gb
            val_ref[:, cols] = vb
            g = gb.astype(F32)
            s1, s2 = _shift_down(g, carry[:, cols], rowid)
            carry[:, cols] = g[tm - 8:, :]
            conv = cb_ref[:, cols] + cw_ref[0:1, cols] * s2 + cw_ref[1:2, cols] * s1 + cw_ref[2:3, cols] * g
            act = (_gelu(conv) * vb.astype(F32)).astype(BF16)
            conv_ref[:, cols] = conv.astype(BF16)
            act_ref[:, cols] = act
            y2 = y2 + _dot(act, wd_ref[cols, :])
        h2 = h + y2
        rf = _rms(h2, D_MODEL)
        n = h2 * rf
        gfv = gf_ref[...]
        e = n * gfv - t_ref[...]
        loss_ref[...] += jnp.sum(e * e) * (0.5 / D_MODEL)
        dy = e * (1.0 / D_MODEL)
        dgf_ref[...] += _colsum(dy * n)
        dh_ref[...] = _rms_bwd(dy * gfv, n, rf, D_MODEL)

    row = lambda w: pl.BlockSpec((tm, w), lambda i: (i, 0))
    return pl.pallas_call(
        body, name="mlp_fwd", grid=(T // tm,),
        in_specs=[row(ATTN_W), row(HGRN_W), pl.BlockSpec((tm, HGRN_W), lambda i: (i, 3)), row(D_MODEL),
                  _full((1, ATTN_W)), _full((1, HGRN_W)), _once((D_MODEL, D_MODEL)),
                  _full((1, D_MODEL)), ANY, _full((3, D_FF)),
                  _full((1, D_FF)), _once((D_FF, D_MODEL)), _full((1, D_MODEL)), row(D_MODEL)],
        out_specs=[row(D_MODEL), row(D_MODEL), row(D_MODEL), row(D_FF), row(D_FF), row(D_FF), row(D_FF), row(D_MODEL),
                   _full((1, 128)), _full((1, D_MODEL))],
        out_shape=[jax.ShapeDtypeStruct((T, D_MODEL), F32), jax.ShapeDtypeStruct((T, D_MODEL), BF16),
                   jax.ShapeDtypeStruct((T, D_MODEL), BF16)] + [jax.ShapeDtypeStruct((T, D_FF), BF16)] * 4
        + [jax.ShapeDtypeStruct((T, D_MODEL), F32),
                   jax.ShapeDtypeStruct((1, 128), F32), jax.ShapeDtypeStruct((1, D_MODEL), F32)],
        scratch_shapes=[pltpu.VMEM((8, D_FF), F32), pltpu.VMEM((D_MODEL, 2 * D_FF), BF16),
                        pltpu.SemaphoreType.DMA((N_CHIPS,))],
        compiler_params=_cp("arbitrary"),
    )(attn_o, rec_o, hg, x, g_a, g_h, w_out, g2, w_up4, conv_w, conv_b, w_down, gf, tgt)


def _mlp_bwd(dh2, gate, val, conv, act, conv_w, w_down, tm=256):
    T = dh2.shape[0]
    nb = T // tm

    def body(dh_ref, gate_ref, val_ref, conv_ref, act_ref, cw_ref, wd_ref, dgv_ref, dcw_ref, dcb_ref, dwd_ref,
             carry, acc):
        i = pl.program_id(0)

        @pl.when(i == 0)
        def _():
            carry[...] = jnp.zeros_like(carry)
            dcw_ref[...] = jnp.zeros_like(dcw_ref)
            dcb_ref[...] = jnp.zeros_like(dcb_ref)
            acc[...] = jnp.zeros_like(acc)

        dhb = dh_ref[...].astype(BF16)
        for lo, hi in MLP_BWD_CHUNKS:
            cols = slice(lo, hi)
            rowid = lax.broadcasted_iota(jnp.int32, (tm, hi - lo), 0)
            acc[cols, :] += _dot_tn(act_ref[:, cols], dhb)
            g = gate_ref[:, cols].astype(F32)
            v = val_ref[:, cols].astype(F32)
            cv = conv_ref[:, cols].astype(F32)
            dact = _dot_nt(dhb, wd_ref[cols, :])
            gl, gp = _gelu_and_grad(cv)
            dconv = dact * v * gp
            nxt = carry[:, cols]
            n0, n1 = _row(nxt, 0), _row(nxt, 1)
            u1 = jnp.where(rowid == tm - 1, n0, pltpu.roll(dconv, tm - 1, 0))
            u2 = jnp.where(rowid == tm - 1, n1, jnp.where(rowid == tm - 2, n0, pltpu.roll(dconv, tm - 2, 0)))
            carry[:, cols] = dconv[0:8, :]
            dcb_ref[:, cols] += _colsum(dconv)
            dcw_ref[0:1, cols] += _colsum(u2 * g)
            dcw_ref[1:2, cols] += _colsum(u1 * g)
            dcw_ref[2:3, cols] += _colsum(dconv * g)
            dgate = cw_ref[2:3, cols] * dconv + cw_ref[1:2, cols] * u1 + cw_ref[0:1, cols] * u2
            dgv_ref[:, cols] = dgate.astype(BF16)
            dgv_ref[:, D_FF + lo:D_FF + hi] = (dact * gl).astype(BF16)

        @pl.when(i == nb - 1)
        def _():
            for lo, hi in MLP_BWD_CHUNKS:
                dwd_ref[lo:hi, :] = acc[lo:hi, :].astype(BF16)

    rev = lambda w: pl.BlockSpec((tm, w), lambda i: (nb - 1 - i, 0))
    return pl.pallas_call(
        body, name="mlp_bwd", grid=(nb,),
        in_specs=[rev(D_MODEL), rev(D_FF), rev(D_FF), rev(D_FF), rev(D_FF), _full((3, D_FF)), _once((D_FF, D_MODEL))],
        out_specs=[rev(2 * D_FF), _full((3, D_FF)), _full((1, D_FF)), _once((D_FF, D_MODEL))],
        out_shape=[jax.ShapeDtypeStruct((T, 2 * D_FF), BF16), jax.ShapeDtypeStruct((3, D_FF), F32),
                   jax.ShapeDtypeStruct((1, D_FF), F32), jax.ShapeDtypeStruct((D_FF, D_MODEL), BF16)],
        scratch_shapes=[pltpu.VMEM((8, D_FF), F32), pltpu.VMEM((D_FF, D_MODEL), F32)],
        compiler_params=_cp("arbitrary"),
    )(dh2, gate, val, conv, act, conv_w, w_down)


def _up_out_bwd(dgv, w_up4, h1, g2, dh2, w_out, mixed, attn_o, rec_o, hg, g_a, g_h, tm=256):
    T = h1.shape[0]
    nb = T // tm

    def body(dgv_ref, wu_hbm, h_ref, g2_ref, dh2_ref, wo_ref, mx_ref, a_ref, r_ref, gt_ref, ga_ref, gh_ref,
             dh1_ref, dg2_ref, da_ref, dr_ref, dgt_ref, dga_ref, dgh_ref, dwo_ref, wu_ref, sem, acc):
        i = pl.program_id(0)

        @pl.when(i == 0)
        def _():
            dg2_ref[...] = jnp.zeros_like(dg2_ref)
            dga_ref[...] = jnp.zeros_like(dga_ref)
            dgh_ref[...] = jnp.zeros_like(dgh_ref)
            acc[...] = jnp.zeros_like(acc)
            _load_side_by_side(wu_hbm, wu_ref, sem)

        du = _dot_nt(dgv_ref[...], wu_ref[...])
        h = h_ref[...]
        r = _rms(h, D_MODEL)
        n = h * r
        dg2_ref[...] += _colsum(du * n)
        dh1 = dh2_ref[...] + _rms_bwd(du * g2_ref[...], n, r, D_MODEL)
        dh1_ref[...] = dh1
        dh1b = dh1.astype(BF16)
        acc[...] += _dot_tn(mx_ref[...], dh1b)
        dmix = _dot_nt(dh1b, wo_ref[...])
        dan = dmix[:, :ATTN_W]
        a = a_ref[...]
        ra = _rms(a, ATTN_W)
        na = a * ra
        dga_ref[...] += _colsum(dan * na)
        da_ref[...] = _rms_bwd(dan * ga_ref[...], na, ra, ATTN_W)
        dmr = dmix[:, ATTN_W:]
        gate = gt_ref[...]
        ghv = gh_ref[...]
        rr, rn, sg = _rec_heads(r_ref[...], gate, ghv)
        dgt_ref[...] = (dmr * rn * ghv * (sg * (1.0 + gate * (1.0 - sg)))).astype(BF16)
        drecn = dmr * (gate * sg)
        dgh_ref[...] += _colsum(drecn * rn)
        drn = drecn * ghv
        prod = drn * rn
        mean = jnp.concatenate(
            [jnp.broadcast_to(jnp.sum(prod[:, h_ * HGRN_DIM:(h_ + 1) * HGRN_DIM], axis=-1, keepdims=True),
                              (tm, HGRN_DIM)) for h_ in range(HGRN_HEADS)], axis=1) * (1.0 / HGRN_DIM)
        dr_ref[...] = rr * (drn - rn * mean)

        @pl.when(i == nb - 1)
        def _():
            dwo_ref[...] = acc[...].astype(BF16)

    row = lambda w: pl.BlockSpec((tm, w), lambda i: (i, 0))
    return pl.pallas_call(
        body, name="up_out_bwd", grid=(nb,),
        in_specs=[row(2 * D_FF), ANY, row(D_MODEL), _full((1, D_MODEL)),
                  row(D_MODEL), _once((D_MODEL, D_MODEL)), row(D_MODEL), row(ATTN_W), row(HGRN_W),
                  pl.BlockSpec((tm, HGRN_W), lambda i: (i, 3)), _full((1, ATTN_W)), _full((1, HGRN_W))],
        out_specs=[row(D_MODEL), _full((1, D_MODEL)), row(ATTN_W), row(HGRN_W), row(HGRN_W),
                   _full((1, ATTN_W)), _full((1, HGRN_W)), _once((D_MODEL, D_MODEL))],
        out_shape=[jax.ShapeDtypeStruct((T, D_MODEL), F32), jax.ShapeDtypeStruct((1, D_MODEL), F32),
                   jax.ShapeDtypeStruct((T, ATTN_W), F32), jax.ShapeDtypeStruct((T, HGRN_W), F32),
                   jax.ShapeDtypeStruct((T, HGRN_W), BF16), jax.ShapeDtypeStruct((1, ATTN_W), F32),
                   jax.ShapeDtypeStruct((1, HGRN_W), F32), jax.ShapeDtypeStruct((D_MODEL, D_MODEL), BF16)],
        scratch_shapes=[pltpu.VMEM((D_MODEL, 2 * D_FF), BF16), pltpu.SemaphoreType.DMA((N_CHIPS,)),
                        pltpu.VMEM((D_MODEL, D_MODEL), F32)],
        compiler_params=_cp("arbitrary"),
    )(dgv, w_up4, h1, g2, dh2, w_out, mixed, attn_o, rec_o, hg, g_a, g_h)


def _in_bwd(dqkv, dhg, w_in4, u1, x, g1, dh1, tm=256):
    T = x.shape[0]
    nb = T // tm

    def body(*refs):
        parts = refs[:7]
        w_hbm, u_ref, x_ref, g_ref, dh1_ref, dw_ref, dx_ref, dg_ref, w_full, sem, acc = refs[7:]
        i = pl.program_id(0)

        @pl.when(i == 0)
        def _():
            dg_ref[...] = jnp.zeros_like(dg_ref)
            acc[...] = jnp.zeros_like(acc)
            _load_side_by_side(w_hbm, w_full, sem)

        dp = jnp.concatenate([p[...] for p in parts], axis=1)
        acc[...] += _dot_tn(u_ref[...], dp)
        du = _dot_nt(dp, w_full[...])
        xv = x_ref[...]
        r = _rms(xv, D_MODEL)
        n = xv * r
        dg_ref[...] += _colsum(du * n)
        dx_ref[...] = dh1_ref[...] + _rms_bwd(du * g_ref[...], n, r, D_MODEL)

        @pl.when(i == nb - 1)
        def _():
            for k in range(N_CHIPS):
                dw_ref[k] = acc[:, k * IN_SHARD:(k + 1) * IN_SHARD].astype(BF16)

    row = lambda w: pl.BlockSpec((tm, w), lambda i: (i, 0))
    return pl.pallas_call(
        body, name="in_bwd", grid=(nb,),
        in_specs=[row(ATTN_W)] * 7 + [ANY, row(D_MODEL), row(D_MODEL), _full((1, D_MODEL)), row(D_MODEL)],
        out_specs=[_once((N_CHIPS, D_MODEL, IN_SHARD)), row(D_MODEL), _full((1, D_MODEL))],
        out_shape=[jax.ShapeDtypeStruct((N_CHIPS, D_MODEL, IN_SHARD), BF16), jax.ShapeDtypeStruct((T, D_MODEL), F32),
                   jax.ShapeDtypeStruct((1, D_MODEL), F32)],
        scratch_shapes=[pltpu.VMEM((D_MODEL, IN_TOTAL), BF16), pltpu.SemaphoreType.DMA((N_CHIPS,)),
                        pltpu.VMEM((D_MODEL, IN_TOTAL), F32)],
        compiler_params=_cp("arbitrary"),
    )(*dqkv, *dhg, w_in4, u1, x, g1, dh1)


def _dw(a, b, kb, nb_, name, tk=1024, side=1):
    T, K = a.shape
    N = b.shape[1]
    nk, nn, nt = K // kb, N // (nb_ * side), T // tk

    def body(a_ref, b_ref, o_ref, acc):
        t = pl.program_id(2)

        @pl.when(t == 0)
        def _():
            acc[...] = jnp.zeros_like(acc)

        acc[...] += _dot_tn(a_ref[...], b_ref[...].astype(BF16))

        @pl.when(t == nt - 1)
        def _():
            for s in range(side):
                o_ref[s] = acc[:, s * nb_:(s + 1) * nb_].astype(BF16)

    return pl.pallas_call(
        body, name=name, grid=(nk, nn, nt),
        in_specs=[pl.BlockSpec((tk, kb), lambda i, j, t: (t, i)),
                  pl.BlockSpec((tk, nb_ * side), lambda i, j, t: (t, j))],
        out_specs=pl.BlockSpec((side, kb, nb_), lambda i, j, t: (i * nn + j, 0, 0)),
        out_shape=jax.ShapeDtypeStruct((nk * nn * side, kb, nb_), BF16),
        scratch_shapes=[pltpu.VMEM((kb, nb_ * side), F32)],
        compiler_params=_cp("arbitrary", "arbitrary", "arbitrary"),
    )(a, b)


def _step_channel(a, x, tgt, g_a, g_h, w_out, g2, w_up4, conv_w, conv_b, w_down, gf):
    h1, mixed, u2, gate, val, conv, act, dh2, loss, dgf = _mlp_fwd(
        a["attn_o"], a["rec_o"], a["hg"], x, g_a, g_h, w_out, g2, w_up4, conv_w, conv_b, w_down, gf, tgt)
    dgv, dcw, dcb, dw_down = _mlp_bwd(dh2, gate, val, conv, act, conv_w, w_down)
    dw_down = dw_down.reshape(N_CHIPS, D_FF // N_CHIPS, D_MODEL)
    dh1, dg2, da, dr, dgt, dga, dgh, dw_out = _up_out_bwd(dgv, w_up4, h1, g2, dh2, w_out, mixed, a["attn_o"],
                                                          a["rec_o"], a["hg"], g_a, g_h)
    dw_up = _dw(u2, dgv, D_MODEL, UP_SHARD, "dw_up", side=2)
    dw_out = dw_out.reshape(N_CHIPS, D_MODEL // N_CHIPS, D_MODEL)
    return dict(loss=loss, dgf=dgf, dcw=dcw, dcb=dcb, dg2=dg2, dga=dga, dgh=dgh, dh1=dh1, da=da, dr=dr, dgt=dgt,
                dw_down=dw_down, dw_up=dw_up, dw_out=dw_out)


def _step_mixers_bwd(a, b, x, g1, w_in4, lb, dqkv):
    dhq, dhf, dhi, dlb = _hgrn_bwd(a["hg"], lb, a["states"], b["dr"])
    dw_in, dx, dg1 = _in_bwd(dqkv, [dhq, dhf, dhi, b["dgt"]], w_in4, a["u1"], x, g1, b["dh1"])
    return dict(dx=dx, dg1=dg1, dlb=dlb, dw_in=dw_in)


BIG = ("w_in", "w_out", "w_up", "w_down")
ANY = pl.BlockSpec(memory_space=pl.ANY)


def _place():
    x, y, c = lax.axis_index("x"), lax.axis_index("y"), lax.axis_index("c")
    chips = [(1 - x, y), (x, 1 - y), (1 - x, 1 - y)]
    return x, y, c, chips


def _remote(src, dst, send_sems, recv_sems, k, to):
    return pltpu.make_async_remote_copy(src_ref=src, dst_ref=dst, send_sem=send_sems.at[k], recv_sem=recv_sems.at[k],
                                        device_id=to, device_id_type=MESH)


def _gather_weights(shards, conv_w):
    n = len(shards)
    halves = [s.shape[0] // 2 for s in shards]

    def body(*refs):
        ins, cw, outs, ocw = refs[:n], refs[n], refs[n + 1:2 * n + 1], refs[2 * n + 1]
        send_sems, recv_sems = refs[2 * n + 2:]
        x, y, c, chips = _place()
        me, sibling = 2 * x + y, (x, y, 1 - c)

        def part(w, chip, half):
            return outs[w].at[chip, pl.ds(half * halves[w], halves[w]), :]

        sent = []
        for j, chip in enumerate(chips):
            for w in range(n):
                sent.append(_remote(ins[w].at[pl.ds(c * halves[w], halves[w]), :], part(w, me, c),
                                    send_sems, recv_sems, w * 3 + j, (*chip, c)))
            sent.append(_remote(cw, ocw.at[me], send_sems, recv_sems, 6 * n + j, (*chip, c)))
        for cp in sent:
            cp.start()
        for j, chip in enumerate(chips):
            kj = 2 * chip[0] + chip[1]
            for w in range(n):
                _remote(part(w, kj, c), part(w, kj, c), send_sems, recv_sems, w * 3 + j, (*chip, c)).wait_recv()
                fwd = _remote(part(w, kj, c), part(w, kj, c), send_sems, recv_sems, 3 * n + w * 3 + j, sibling)
                fwd.start()
                sent.append(fwd)
        for j, chip in enumerate(chips):
            kj = 2 * chip[0] + chip[1]
            for w in range(n):
                _remote(part(w, kj, 1 - c), part(w, kj, 1 - c), send_sems, recv_sems, 3 * n + w * 3 + j,
                        sibling).wait_recv()
            _remote(cw, ocw.at[kj], send_sems, recv_sems, 6 * n + j, (*chip, c)).wait_recv()
        for cp in sent:
            cp.wait_send()

    n_sem = 6 * n + 3
    outs = pl.pallas_call(
        body, name="gather_weights",
        in_specs=[ANY] * (n + 1), out_specs=[ANY] * (n + 1),
        out_shape=[jax.ShapeDtypeStruct((N_CHIPS,) + s.shape, s.dtype) for s in shards]
        + [jax.ShapeDtypeStruct((N_CHIPS,) + conv_w.shape, conv_w.dtype)],
        scratch_shapes=[pltpu.SemaphoreType.DMA((n_sem,)), pltpu.SemaphoreType.DMA((n_sem,))],
    )(*shards, conv_w)
    chip = 2 * lax.axis_index("x") + lax.axis_index("y")
    return [lax.dynamic_update_slice(o, s[None], (chip,) + (0,) * s.ndim) for o, s in zip(outs, [*shards, conv_w])]


def _allreduce_small(buf):
    rows = buf.shape[0]

    def body(in_ref, out_ref, slots, send_sems, recv_sems):
        x, y, c, _ = _place()
        me = 4 * x + 2 * y + c
        slots[me] = in_ref[...]
        sent = []
        for p in range(1, 8):
            to = (x ^ (p >> 2), y ^ ((p >> 1) & 1), c ^ (p & 1))
            sent.append(_remote(in_ref, slots.at[me], send_sems, recv_sems, p, to))
        for cp in sent:
            cp.start()
        for p in range(1, 8):
            frm = 4 * (x ^ (p >> 2)) + 2 * (y ^ ((p >> 1) & 1)) + (c ^ (p & 1))
            _remote(in_ref, slots.at[frm], send_sems, recv_sems, p, (x, y, c)).wait_recv()
        for cp in sent:
            cp.wait_send()
        acc = slots[0]
        for d in range(1, 8):
            acc = acc + slots[d]
        out_ref[...] = acc

    vm = pl.BlockSpec(memory_space=pltpu.VMEM)
    return pl.pallas_call(
        body, name="allreduce_small", in_specs=[vm], out_specs=vm,
        out_shape=jax.ShapeDtypeStruct(buf.shape, F32),
        scratch_shapes=[pltpu.VMEM((8, rows, 128), F32), pltpu.SemaphoreType.DMA((8,)), pltpu.SemaphoreType.DMA((8,))],
    )(buf)


def _sibling_peer():
    x, y, c, _ = _place()
    return [(x, y, 1 - c)]


def _chip_peers():
    x, y, c, chips = _place()
    return [(*chip, c) for chip in chips]


def _handshake(peers):
    barrier = pltpu.get_barrier_semaphore()
    for peer in peers:
        pl.semaphore_signal(barrier, inc=1, device_id=peer, device_id_type=MESH)
    pl.semaphore_wait(barrier, len(peers))


def _pair_exchange(gs, name, barrier_id):
    n = len(gs)
    halves = [g.shape[1] // 2 for g in gs]

    def body(*refs):
        g, got = refs[:n], refs[n:2 * n]
        send_sems, recv_sems = refs[2 * n:]
        _handshake(_sibling_peer())
        x, y, c, _ = _place()
        cps = [_remote(g[w].at[:, pl.ds((1 - c) * halves[w], halves[w]), :], got[w], send_sems, recv_sems, w,
                       (x, y, 1 - c)) for w in range(n)]
        for cp in cps:
            cp.start()
        for cp in cps:
            cp.wait()

    return pl.pallas_call(
        body, name=name, in_specs=[ANY] * n, out_specs=[ANY] * n,
        out_shape=[jax.ShapeDtypeStruct((N_CHIPS, h, g.shape[2]), g.dtype) for g, h in zip(gs, halves)],
        scratch_shapes=[pltpu.SemaphoreType.DMA((n,)), pltpu.SemaphoreType.DMA((n,))],
        compiler_params=pltpu.CompilerParams(collective_id=barrier_id),
    )(*gs)


def _core_id():
    return lax.axis_index("c").reshape(1).astype(jnp.int32)


def _pair_sum(gs, gots, name):
    n = len(gs)

    def body(c_ref, *refs):
        for g_ref, b_ref, o_ref in zip(refs[:n], refs[n:2 * n], refs[2 * n:]):
            o_ref[...] = (g_ref[...].astype(F32) + b_ref[...].astype(F32)).astype(BF16)

    mine = lambda got: pl.BlockSpec((1,) + got.shape[1:], lambda k, c_ref: (k, c_ref[0], 0))
    blk = lambda got: pl.BlockSpec((1,) + got.shape[1:], lambda k, c_ref: (k, 0, 0))
    return pl.pallas_call(
        body, name=name,
        grid_spec=pltpu.PrefetchScalarGridSpec(
            num_scalar_prefetch=1, grid=(N_CHIPS,),
            in_specs=[mine(got) for got in gots] + [blk(got) for got in gots], out_specs=[blk(got) for got in gots]),
        out_shape=[jax.ShapeDtypeStruct(got.shape, BF16) for got in gots],
        compiler_params=_cp("arbitrary"))(_core_id(), *gs, *gots)


def _sum_partials(gs, gots, landeds, name):
    n = len(gs)

    def body(ids, *refs):
        for g_ref, b_ref, l_ref, o_ref in zip(refs[:n], refs[n:2 * n], refs[2 * n:3 * n], refs[3 * n:]):
            acc = g_ref[0].astype(F32) + b_ref[0].astype(F32)
            for j in range(3):
                acc = acc + l_ref[j].astype(F32)
            o_ref[...] = acc

    ids = jnp.stack([2 * lax.axis_index("x") + lax.axis_index("y"), lax.axis_index("c")]).astype(jnp.int32)
    shp = [got.shape[1:] for got in gots]
    return pl.pallas_call(
        body, name=name,
        grid_spec=pltpu.PrefetchScalarGridSpec(
            num_scalar_prefetch=1, grid=(1,),
            in_specs=[pl.BlockSpec((1,) + s, lambda i, ids: (ids[0], ids[1], 0)) for s in shp]
            + [pl.BlockSpec((1,) + s, lambda i, ids: (ids[0], 0, 0)) for s in shp]
            + [pl.BlockSpec((3,) + s, lambda i, ids: (0, 0, 0)) for s in shp],
            out_specs=[pl.BlockSpec(s, lambda i, ids: (ids[1], 0)) for s in shp]),
        out_shape=[jax.ShapeDtypeStruct((2 * s[0], s[1]), F32) for s in shp],
        compiler_params=_cp("arbitrary"))(ids, *gs, *gots, *landeds)


def _pair_share(reds, name, barrier_id):
    n = len(reds)

    def body(*refs):
        out = refs[n:2 * n]
        send_sems, recv_sems = refs[2 * n:]
        _handshake(_sibling_peer())
        x, y, c, _ = _place()
        def half(w, which):
            h = out[w].shape[0] // 2
            return out[w].at[pl.ds(which * h, h), :]

        cps = [_remote(half(w, c), half(w, c), send_sems, recv_sems, w, (x, y, 1 - c)) for w in range(n)]
        for cp in cps:
            cp.start()
        for w in range(n):
            _remote(half(w, 1 - c), half(w, 1 - c), send_sems, recv_sems, w, (x, y, 1 - c)).wait_recv()
        for cp in cps:
            cp.wait_send()

    return pl.pallas_call(
        body, name=name, in_specs=[ANY] * n, out_specs=[ANY] * n,
        out_shape=[jax.ShapeDtypeStruct(r.shape, F32) for r in reds],
        input_output_aliases={w: w for w in range(n)},
        scratch_shapes=[pltpu.SemaphoreType.DMA((n,)), pltpu.SemaphoreType.DMA((n,))],
        compiler_params=pltpu.CompilerParams(collective_id=barrier_id),
    )(*reds)


HBM = pl.BlockSpec(memory_space=pltpu.HBM)
SEM = pl.BlockSpec(memory_space=pltpu.SEMAPHORE)
DATAFLOW = pltpu.SideEffectType.DATAFLOW_SIDE_EFFECTING


def _copies_start(name, srcs, lands, plan, n_copies, after, peers, barrier_id):
    ns, nb, na = len(srcs), len(srcs) + len(lands), len(after)

    def body(*refs):
        src_refs, land_refs = refs[:ns], refs[ns:nb]
        send_sems, recv_sems = refs[nb + na:nb + na + 2]
        token = refs[-1]
        _handshake(peers())
        for k, (src, there, _, to) in enumerate(plan(src_refs, land_refs)):
            _remote(src, there, send_sems, recv_sems, k, to).start()
        token[...] = jnp.zeros_like(token)

    hbm = lambda a: pltpu.HBM(a.shape, a.dtype)
    outs = pl.pallas_call(
        body, name=name,
        out_shape=(pltpu.SemaphoreType.DMA((n_copies,)), pltpu.SemaphoreType.DMA((n_copies,)),
                   *[hbm(a) for a in srcs], *[hbm(a) for a in lands], jax.ShapeDtypeStruct((8, 128), F32)),
        in_specs=[HBM] * nb + [ANY] * na,
        out_specs=(SEM, SEM, *[HBM] * nb, pl.BlockSpec(memory_space=pltpu.VMEM)),
        input_output_aliases={i: 2 + i for i in range(nb)},
        compiler_params=pltpu.CompilerParams(has_side_effects=DATAFLOW, collective_id=barrier_id),
    )(*[pltpu.with_memory_space_constraint(a, pltpu.HBM) for a in (*srcs, *lands)], *after)
    return outs[0], outs[1], outs[2:2 + ns], outs[2 + ns:2 + nb], outs[-1]


def _copies_wait(name, send_sems, recv_sems, srcs, lands, plan, after):
    ns, nb, na = len(srcs), len(srcs) + len(lands), len(after)

    def body(*refs):
        src_refs, land_refs = refs[:ns], refs[ns:nb]
        send_sems, recv_sems = refs[nb:nb + 2]
        for k, (src, _, here, to) in enumerate(plan(src_refs, land_refs)):
            cp = _remote(src, here, send_sems, recv_sems, k, to)
            cp.wait_send()
            cp.wait_recv()

    hbm = lambda a: pltpu.HBM(a.shape, a.dtype)
    outs = pl.pallas_call(
        body, name=name,
        out_shape=(*[hbm(a) for a in srcs], *[hbm(a) for a in lands]),
        in_specs=[HBM] * nb + [SEM, SEM] + [ANY] * na,
        out_specs=tuple([HBM] * nb),
        input_output_aliases={i: i for i in range(nb)},
        compiler_params=pltpu.CompilerParams(has_side_effects=DATAFLOW),
    )(*srcs, *lands, send_sems, recv_sems, *after)
    return outs[:ns], outs[ns:]


def _gather_plan(halves):
    def plan(shards, lands):
        x, y, c, chips = _place()
        me = 2 * x + y
        copies = []
        for w, h in enumerate(halves):
            rows = pl.ds(c * h, h)
            for chip in chips:
                copies.append((shards[w].at[rows, :], lands[w].at[me, rows, :],
                               lands[w].at[2 * chip[0] + chip[1], rows, :], (*chip, c)))
        return copies
    return plan


def _reduce_plan(n):
    def plan(ps, lands):
        x, y, c, chips = _place()
        return [(ps[w].at[2 * chip[0] + chip[1]], lands[w].at[j], lands[w].at[j], (*chip, c))
                for w in range(n) for j, chip in enumerate(chips)]
    return plan


def _forward_plan(halves):
    def plan(_, lands):
        x, y, c, chips = _place()

        def part(w, chip, half):
            return lands[w].at[2 * chip[0] + chip[1], pl.ds(half * halves[w], halves[w]), :]

        return [(part(w, chip, c), part(w, chip, c), part(w, chip, 1 - c), (x, y, 1 - c))
                for w in range(len(halves)) for chip in chips]
    return plan


def _pair_plan(halves):
    def plan(gs, gots):
        x, y, c, _ = _place()
        return [(gs[w].at[:, pl.ds((1 - c) * h, h), :], gots[w], gots[w], (x, y, 1 - c)) for w, h in enumerate(halves)]
    return plan


def _place_own(gathered, shards):
    chip = 2 * lax.axis_index("x") + lax.axis_index("y")
    return [lax.dynamic_update_slice(o, s[None], (chip, 0, 0)) for o, s in zip(gathered, shards)]


ADAMW_STEPS = 4


def _to_bf16(ws):
    def body(*refs):
        for src, dst in zip(refs[:len(ws)], refs[len(ws):]):
            dst[...] = src[...].astype(BF16)

    blk = lambda a: pl.BlockSpec((a.shape[0] // ADAMW_STEPS, a.shape[1]), lambda i: (i, 0))
    return pl.pallas_call(
        body, name="to_bf16", grid=(ADAMW_STEPS,), in_specs=[blk(a) for a in ws], out_specs=[blk(a) for a in ws],
        out_shape=[jax.ShapeDtypeStruct(a.shape, BF16) for a in ws], compiler_params=_cp("arbitrary"))(*ws)


def _adamw(ws, gs, ms, vs, name):
    n = len(ws)

    def body(*refs):
        ins, outs = refs[:4 * n], refs[4 * n:]
        for k in range(n):
            gv = ins[n + k][...]
            outs[4 * k][...] = gv
            outs[4 * k + 1][...], outs[4 * k + 2][...], outs[4 * k + 3][...] = _adamw_math(
                ins[k][...], gv, ins[2 * n + k][...], ins[3 * n + k][...])

    blk = lambda a: pl.BlockSpec((a.shape[0] // ADAMW_STEPS, a.shape[1]), lambda i: (i, 0))
    outs = pl.pallas_call(
        body, name=name, grid=(ADAMW_STEPS,), in_specs=[blk(a) for a in ws] * 4,
        out_specs=[blk(a) for a in ws for _ in range(4)],
        out_shape=[jax.ShapeDtypeStruct(a.shape, F32) for a in ws for _ in range(4)],
        compiler_params=_cp("arbitrary"))(*ws, *gs, *ms, *vs)
    return [outs[4 * k:4 * k + 4] for k in range(n)]


SMALL = (("norm1_g", 1, 1024), ("attn_norm_g", 1, 512), ("hgrn_norm_g", 1, 512), ("hgrn_lb_logits", 2, 512),
         ("norm2_g", 1, 1024), ("conv_b", 1, D_FF), ("final_norm_g", 1, 1024), ("conv_w", 3, D_FF))
LOSS_ROW = sum(r * c for _, r, c in SMALL) // 128
SMALL_ROWS = 136


def _rows_to_lanes(ref, row, width):
    return jnp.concatenate([ref[row + j:row + j + 1, :] for j in range(width // 128)], axis=1)


def _pack_small(grads, dlb, lb, loss):
    def body(*refs):
        parts, dlb_ref, lb_ref, loss_ref, out = refs[:len(SMALL) - 1], refs[-4], refs[-3], refs[-2], refs[-1]
        out[...] = jnp.zeros_like(out)
        lbv = lb_ref[...]
        dl = dlb_ref[...] * lbv * (1.0 - lbv)
        row = 0
        parts = list(parts)
        for name, rows, width in SMALL:
            for r in range(rows):
                if name == "hgrn_lb_logits":
                    src = dl if r == 0 else -dl
                    for j in range(width // 128):
                        out[row + j:row + j + 1, :] = src[:, 128 * j:128 * (j + 1)]
                else:
                    for j in range(width // 128):
                        out[row + j:row + j + 1, :] = parts[0][r:r + 1, 128 * j:128 * (j + 1)]
                row += width // 128
            if name != "hgrn_lb_logits":
                parts.pop(0)
        out[LOSS_ROW:LOSS_ROW + 1, :] = loss_ref[...]

    vm = pl.BlockSpec(memory_space=pltpu.VMEM)
    return pl.pallas_call(body, name="pack_small", in_specs=[vm] * (len(grads) + 3), out_specs=vm,
                          out_shape=jax.ShapeDtypeStruct((SMALL_ROWS, 128), F32))(*grads, dlb, lb, loss)


def _adamw_math(w, g, m, v):
    nm = ADAM_B1 * m + (1.0 - ADAM_B1) * g
    nv = ADAM_B2 * v + (1.0 - ADAM_B2) * (g * g)
    m_hat = nm / (1.0 - ADAM_B1 ** ADAM_STEP)
    v_hat = nv / (1.0 - ADAM_B2 ** ADAM_STEP)
    return -ADAM_LR * (m_hat / (jnp.sqrt(v_hat) + ADAM_EPS) + ADAM_WD * w), nm, nv


def _small_update(summed, g_conv_w, ws, ms, vs):
    n = len(SMALL)

    def body(*refs):
        s_ref, gcw_ref = refs[:2]
        w_refs, m_refs, v_refs = refs[2:2 + n], refs[2 + n:2 + 2 * n], refs[2 + 2 * n:2 + 3 * n]
        outs = refs[2 + 3 * n:]
        row = 0
        for k, (name, rows, width) in enumerate(SMALL):
            if name == "conv_w":
                g = gcw_ref[...]
            else:
                g = jnp.concatenate([_rows_to_lanes(s_ref, row + r * (width // 128), width) for r in range(rows)], axis=0)
            row += rows * (width // 128)
            d, nm, nv = _adamw_math(w_refs[k][...], g, m_refs[k][...], v_refs[k][...])
            for o, val in zip(outs[4 * k:4 * k + 4], (g, d, nm, nv)):
                o[...] = val

    vm = pl.BlockSpec(memory_space=pltpu.VMEM)
    outs = pl.pallas_call(
        body, name="small_update", in_specs=[vm] * (2 + 3 * n), out_specs=[vm] * (4 * n),
        out_shape=[jax.ShapeDtypeStruct(a.shape, F32) for a in ws for _ in range(4)],
    )(summed, g_conv_w, *ws, *ms, *vs)
    return [outs[4 * k:4 * k + 4] for k in range(n)]


def kernel(x, norm1_g, w_in, attn_norm_g, hgrn_norm_g, hgrn_lb_logits, w_out, norm2_g, w_up, conv_w, conv_b, w_down, final_norm_g, loss_target, m_norm1_g, m_w_in, m_attn_norm_g, m_hgrn_norm_g, m_hgrn_lb_logits, m_w_out, m_norm2_g, m_w_up, m_conv_w, m_conv_b, m_w_down, m_final_norm_g, v_norm1_g, v_w_in, v_attn_norm_g, v_hgrn_norm_g, v_hgrn_lb_logits, v_w_out, v_norm2_g, v_w_up, v_conv_w, v_conv_b, v_w_down, v_final_norm_g):
    w = dict(norm1_g=norm1_g, w_in=w_in, attn_norm_g=attn_norm_g, hgrn_norm_g=hgrn_norm_g,
             hgrn_lb_logits=hgrn_lb_logits, w_out=w_out, norm2_g=norm2_g, w_up=w_up, conv_w=conv_w, conv_b=conv_b,
             w_down=w_down, final_norm_g=final_norm_g)
    m = dict(norm1_g=m_norm1_g, w_in=m_w_in, attn_norm_g=m_attn_norm_g, hgrn_norm_g=m_hgrn_norm_g,
             hgrn_lb_logits=m_hgrn_lb_logits, w_out=m_w_out, norm2_g=m_norm2_g, w_up=m_w_up, conv_w=m_conv_w,
             conv_b=m_conv_b, w_down=m_w_down, final_norm_g=m_final_norm_g)
    v = dict(norm1_g=v_norm1_g, w_in=v_w_in, attn_norm_g=v_attn_norm_g, hgrn_norm_g=v_hgrn_norm_g,
             hgrn_lb_logits=v_hgrn_lb_logits, w_out=v_w_out, norm2_g=v_norm2_g, w_up=v_w_up, conv_w=v_conv_w,
             conv_b=v_conv_b, w_down=v_w_down, final_norm_g=v_final_norm_g)
    names = list(w)
    chip = 2 * lax.axis_index("x") + lax.axis_index("y")

    shards = dict(zip(BIG, _to_bf16([w[k][0] for k in BIG])))
    w_in4, conv_w4 = _gather_weights([shards["w_in"]], conv_w[0])
    conv_w_full = jnp.transpose(conv_w4, (1, 0, 2)).reshape(3, D_FF)
    lb = jax.nn.softmax(hgrn_lb_logits, axis=0)[0:1]
    late = [shards[k] for k in BIG[1:]]
    gather_plan = _gather_plan([s.shape[0] // 2 for s in late])
    started = _copies_start("gather_start", late, [lax.empty((N_CHIPS,) + s.shape, BF16) for s in late], gather_plan,
                            3 * len(late), after=(w_in4,), peers=_chip_peers, barrier_id=0)
    u1, qkv, hg = _in_proj(x[0], norm1_g + started[4][0:1, 0:1], w_in4)
    attn_o, lse = _attn_fwd(qkv)
    late, landed_w = _copies_wait("gather_wait", *started[:4], gather_plan, after=(attn_o,))
    forward_plan = _forward_plan([s.shape[0] // 2 for s in late])
    started = _copies_start("forward_start", [], landed_w, forward_plan, 3 * len(late), after=(),
                            peers=_sibling_peer, barrier_id=1)
    rec_o, states = _hgrn_fwd(hg, lb + started[4][0:1, 0:1])
    a = dict(u1=u1, qkv=qkv, hg=hg, attn_o=attn_o, lse=lse, rec_o=rec_o, states=states)
    w_out4, w_up4, w_down4 = _place_own(
        _copies_wait("forward_wait", *started[:4], forward_plan, after=(rec_o,))[1], late)

    b = _step_channel(a, x[0], loss_target[0], attn_norm_g, hgrn_norm_g, w_out4.reshape(D_MODEL, D_MODEL), norm2_g,
                      w_up4, conv_w_full, conv_b, w_down4.reshape(D_FF, D_MODEL), final_norm_g.reshape(1, D_MODEL))

    early = [b["dw_out"], b["dw_up"], b["dw_down"]]
    pair_plan = _pair_plan([gk.shape[1] // 2 for gk in early])
    started = _copies_start("pair_start", early,
                            [lax.empty((N_CHIPS, gk.shape[1] // 2, gk.shape[2]), BF16) for gk in early], pair_plan,
                            len(early), after=(), peers=_sibling_peer, barrier_id=2)
    dqkv = _attn_bwd(qkv, attn_o, lse, b["da"], started[4])
    early, gots = _copies_wait("pair_wait", *started[:4], pair_plan, after=(dqkv[0],))
    ps = _pair_sum(early, gots, "pair_sum")
    reduce_plan = _reduce_plan(len(ps))
    started = _copies_start("reduce_start", ps, [lax.empty((3,) + p.shape[1:], BF16) for p in ps], reduce_plan,
                            3 * len(ps), after=(), peers=_chip_peers, barrier_id=3)
    c = _step_mixers_bwd(a, b, x[0], norm1_g, w_in4, lb + started[4][0:1, 0:1], dqkv)
    gots_in = _pair_exchange([c["dw_in"]], "pair_exchange_w_in", barrier_id=4)
    ps_in = _pair_sum([c["dw_in"]], gots_in, "pair_sum_w_in")[0]
    plan_in = _reduce_plan(1)
    started_in = _copies_start("reduce_start_w_in", [ps_in], [lax.empty((3,) + ps_in.shape[1:], BF16)], plan_in, 3,
                               after=(), peers=_chip_peers, barrier_id=5)
    landed = _copies_wait("reduce_wait", *started[:4], reduce_plan, after=(started_in[4],))[1]
    reds = _sum_partials(early, gots, landed, "sum_partials")
    g = dict(zip(BIG[1:], _pair_share(reds, "pair_share", barrier_id=6)))
    delta, new_m, new_v = {}, {}, {}
    shard = lambda p: [p[k][0] for k in BIG[1:]]
    for k, parts in zip(BIG[1:], _adamw(shard(w), [g[k] for k in BIG[1:]], shard(m), shard(v), "adamw")):
        g[k], delta[k], new_m[k], new_v[k] = parts

    loss, dx = b["loss"], c["dx"]
    small = dict(g1=c["dg1"], g_a=b["dga"], g_h=b["dgh"], lb=c["dlb"], g2=b["dg2"], conv_w=b["dcw"], conv_b=b["dcb"],
                 gf=b["dgf"])
    summed = _allreduce_small(_pack_small(
        [small["g1"], small["g_a"], small["g_h"], small["g2"], small["conv_b"], small["gf"], small["conv_w"]],
        small["lb"], lb, loss))
    loss_total = summed[LOSS_ROW, 0]
    g_conv_w = lax.dynamic_slice(summed[LOSS_ROW - 3 * D_FF // 128:LOSS_ROW].reshape(3, D_FF),
                                 (0, chip * (D_FF // N_CHIPS)), (3, D_FF // N_CHIPS))
    two_d = lambda p, k: p[k].reshape(-1, p[k].shape[-1])
    updated = _small_update(summed, g_conv_w, *[[two_d(p, k) for k, _, _ in SMALL] for p in (w, m, v)])
    for (k, _, _), parts in zip(SMALL, updated):
        g[k], delta[k], new_m[k], new_v[k] = (a.reshape(w[k].shape) for a in parts)

    landed_in = _copies_wait("reduce_wait_w_in", *started_in[:4], plan_in, after=(updated[0][1], delta["w_up"]))[1]
    red_in = _sum_partials([c["dw_in"]], gots_in, landed_in, "sum_partials_w_in")
    g["w_in"] = _pair_share(red_in, "pair_share_w_in", barrier_id=7)[0]
    g["w_in"], delta["w_in"], new_m["w_in"], new_v["w_in"] = _adamw([w_in[0]], [g["w_in"]], [m_w_in[0]], [v_w_in[0]],
                                                                    "adamw_w_in")[0]
    for k in BIG:
        g[k], delta[k], new_m[k], new_v[k] = g[k][None], delta[k][None], new_m[k][None], new_v[k][None]

    return (loss_total, dx[None], *[g[k] for k in names], *[delta[k] for k in names],
            *[new_m[k] for k in names], *[new_v[k] for k in names])
```
